```python
import math
import jax, jax.numpy as jnp
from jax import lax
import numpy as np

D_MODEL = 1024
BATCH = 16
SEQ = 2048
DEPTH = 2

HEAD_DIM = 64
N_SB_HEADS = 4
N_FOX_HEADS = 4
N_DIL_HEADS = 4
N_LRU_BLOCKS = 4
LRU_BLOCK = HEAD_DIM
LRU_WIDTH = N_LRU_BLOCKS * LRU_BLOCK
SB_W = N_SB_HEADS * HEAD_DIM
FOX_W = N_FOX_HEADS * HEAD_DIM
DIL_W = N_DIL_HEADS * HEAD_DIM
MIX_WIDTH = SB_W + FOX_W + DIL_W + LRU_WIDTH
N_IN = 3 * SB_W + 3 * FOX_W + N_FOX_HEADS + 3 * DIL_W + 2 * LRU_WIDTH
BLOCK = 128
DILATED_PATTERNS = ((128, 1), (512, 4), (2048, 16))
LRU_CONV_WIDTH = 4
LRU_C = 8.0
FFN_CONV_WIDTH = 3
D_FF = 2816
N_MEM = 256
N_CROSS_HEADS = 4
CROSS_W = N_CROSS_HEADS * HEAD_DIM
NUM_BUCKETS = 32
MAX_DISTANCE = 2048
EPS = 1e-6

kernel_name = "hymba_style_sb_fox_dilated_rglru_trunk"


def rms_norm(x, g):
    x32 = x.astype(jnp.float32)
    y = x32 * lax.rsqrt(jnp.mean(x32 * x32, axis=-1, keepdims=True) + EPS)
    return (y * g.astype(jnp.float32)).astype(x.dtype)


def causal_dwconv(x, w, b):
    k_w = w.shape[0]
    s = x.shape[1]
    xp = jnp.pad(x, ((0, 0), (k_w - 1, 0), (0, 0)))
    y = b
    for j in range(k_w):
        y = y + w[j] * xp[:, j:j + s]
    return y


def t5_bucket(dist):
    n = jnp.maximum(dist, 0)
    max_exact = NUM_BUCKETS // 2
    nf = jnp.maximum(n, 1).astype(jnp.float32)
    large = max_exact + (jnp.log(nf / max_exact) / math.log(MAX_DISTANCE / max_exact)
                         * (NUM_BUCKETS - max_exact)).astype(jnp.int32)
    large = jnp.minimum(large, NUM_BUCKETS - 1)
    return jnp.where(n < max_exact, n, large)


def split_cols(t, sizes):
    out, start = [], 0
    for n in sizes:
        out.append(t[..., start:start + n])
        start += n
    return out


def split_qkv(t, n_heads):
    b, s, _ = t.shape
    t = t.reshape(b, s, 3, n_heads, HEAD_DIM).transpose(2, 0, 3, 1, 4)
    return t[0] * HEAD_DIM ** -0.5, t[1], t[2]


def merge_heads(o):
    b, h, s, d = o.shape
    return o.transpose(0, 2, 1, 3).reshape(b, s, h * d)


def stick_breaking_attention(q, k, v):
    s_len = q.shape[2]
    outs = []
    for i in range(s_len // BLOCK):
        q0, q1 = i * BLOCK, (i + 1) * BLOCK
        z = jnp.einsum('bhqd,bhkd->bhqk', q[:, :, q0:q1], k[:, :, :q1]).astype(jnp.float32)
        strict = jnp.arange(q1)[None, :] < jnp.arange(q0, q1)[:, None]
        log_keep = jnp.where(strict, jax.nn.log_sigmoid(-z), 0.0)
        log_keep_after = lax.cumsum(log_keep, axis=3, reverse=True) - log_keep
        att = jnp.where(strict, jnp.exp(jax.nn.log_sigmoid(z) + log_keep_after), 0.0)
        outs.append(jnp.einsum('bhqk,bhkd->bhqd', att, v[:, :, :q1].astype(jnp.float32)))
    return jnp.concatenate(outs, axis=2)


def forgetting_attention(q, k, v, f_logit):
    s_len = q.shape[2]
    log_f = jax.nn.log_sigmoid(f_logit.astype(jnp.float32))
    cum = jnp.cumsum(log_f, axis=1).transpose(0, 2, 1)
    outs = []
    for i in range(s_len // BLOCK):
        q0, q1 = i * BLOCK, (i + 1) * BLOCK
        z = jnp.einsum('bhqd,bhkd->bhqk', q[:, :, q0:q1], k[:, :, :q1]).astype(jnp.float32)
        z = z + cum[:, :, q0:q1, None] - cum[:, :, None, :q1]
        causal = jnp.arange(q1)[None, :] <= jnp.arange(q0, q1)[:, None]
        p = jax.nn.softmax(jnp.where(causal, z, -jnp.inf), axis=-1)
        outs.append(jnp.einsum('bhqk,bhkd->bhqd', p, v[:, :, :q1].astype(jnp.float32)))
    return jnp.concatenate(outs, axis=2)


def dilated_branch(q, k, v, rel_bias, dil, steps):
    b, h, s, hd = q.shape
    L = s // dil
    qb_len = math.gcd(BLOCK, L)
    nb = L // qb_len

    def to_cls(t):
        return t.reshape(b, h, L, dil, hd).transpose(0, 1, 3, 2, 4)

    qc = to_cls(q).reshape(b, h, dil, nb, qb_len, hd)
    pad = ((0, 0), (0, 0), (0, 0), (steps, 0), (0, 0))
    kc = jnp.pad(to_cls(k), pad)
    vc = jnp.pad(to_cls(v), pad)
    idx = (jnp.arange(nb) * qb_len)[:, None] + jnp.arange(qb_len + steps)[None, :]
    kb = kc[:, :, :, idx]
    vb = vc[:, :, :, idx]
    sc = jnp.einsum('bhrnqd,bhrnkd->bhrnqk', qc, kb).astype(jnp.float32)
    qpos = (jnp.arange(nb) * qb_len)[:, None] + jnp.arange(qb_len)[None, :]
    kpos = idx - steps
    delta = qpos[:, :, None] - kpos[:, None, :]
    valid = (delta >= 0) & (delta <= steps) & (kpos[:, None, :] >= 0)
    bias = jnp.moveaxis(rel_bias.astype(jnp.float32)[t5_bucket(delta * dil)], -1, 0)[:, None]
    sc = jnp.where(valid, sc + bias, -jnp.inf)
    m = jnp.max(sc, axis=-1, keepdims=True)
    p = jnp.exp(sc - m)
    l = jnp.sum(p, axis=-1)
    o = jnp.einsum('bhrnqk,bhrnkd->bhrnqd', p, vb.astype(jnp.float32)) / l[..., None]
    lse = m[..., 0] + jnp.log(l)
    o = o.reshape(b, h, dil, L, hd).transpose(0, 1, 3, 2, 4).reshape(b, h, s, hd)
    lse = lse.reshape(b, h, dil, L).transpose(0, 1, 3, 2).reshape(b, h, s)
    return o, lse


def dilated_attention(q, k, v, rel_bias):
    outs, lses = [], []
    for window, dil in DILATED_PATTERNS:
        o, lse = dilated_branch(q, k, v, rel_bias, dil, window // dil)
        outs.append(o)
        lses.append(lse)
    wts = jax.nn.softmax(jnp.stack(lses), axis=0)
    return jnp.sum(wts[..., None] * jnp.stack(outs), axis=0)


def rg_lru_branch(x_br, gate_br, conv_w, conv_b, w_a, b_a, w_x, b_x, lam):
    b, s, c = x_br.shape
    f32 = jnp.float32
    xc = causal_dwconv(x_br.astype(f32), conv_w.astype(f32), conv_b.astype(f32))
    xg = xc.reshape(b, s, N_LRU_BLOCKS, LRU_BLOCK)
    r = jax.nn.sigmoid(jnp.einsum('bsgi,gij->bsgj', xg, w_a.astype(f32)).reshape(b, s, c) + b_a.astype(f32))
    i_gate = jax.nn.sigmoid(jnp.einsum('bsgi,gij->bsgj', xg, w_x.astype(f32)).reshape(b, s, c) + b_x.astype(f32))
    log_a = -LRU_C * r * jax.nn.softplus(-lam.astype(f32))
    a = jnp.exp(log_a)
    u = jnp.sqrt(-jnp.expm1(2.0 * log_a)) * (i_gate * xc)

    def combine(e1, e2):
        a1, b1 = e1
        a2, b2 = e2
        return a1 * a2, a2 * b1 + b2

    _, h = lax.associative_scan(combine, (a, u), axis=1)
    return h * jax.nn.gelu(gate_br.astype(f32), approximate=False)


def memory_cross_attention(h, mem_n, w_cq, w_ck, w_cv, w_co):
    b, s, _ = h.shape
    m = mem_n.shape[1]
    q = (h @ w_cq).reshape(b, s, N_CROSS_HEADS, HEAD_DIM).transpose(0, 2, 1, 3) * HEAD_DIM ** -0.5
    k = (mem_n @ w_ck).reshape(b, m, N_CROSS_HEADS, HEAD_DIM).transpose(0, 2, 1, 3)
    v = (mem_n @ w_cv).reshape(b, m, N_CROSS_HEADS, HEAD_DIM).transpose(0, 2, 1, 3)
    p = jax.nn.softmax(jnp.einsum('bhqd,bhkd->bhqk', q, k).astype(jnp.float32), axis=-1)
    o = jnp.einsum('bhqk,bhkd->bhqd', p, v.astype(jnp.float32))
    return merge_heads(o).astype(h.dtype) @ w_co


def _fwd_setup_inputs(seed: int = 0) -> dict:
    key = jax.random.key(seed)
    ks = iter(jax.random.split(key, 40))
    f32 = jnp.float32
    L = DEPTH

    def normal(shape, scale):
        return scale * jax.random.normal(next(ks), shape, f32)

    def gain(shape):
        return 1.0 + normal(shape, 0.02)

    u = jax.random.uniform(next(ks), (L, LRU_WIDTH), f32, 0.9, 0.999)
    a0 = u ** (1.0 / LRU_C)
    lru_lambda = jnp.log(a0) - jnp.log1p(-a0)
    return {
        "x": normal((BATCH, SEQ, D_MODEL), 1.0),
        "mem": normal((BATCH, N_MEM, D_MODEL), 1.0),
        "norm_mix_g": gain((L, D_MODEL)),
        "w_in": normal((L, D_MODEL, N_IN), D_MODEL ** -0.5),
        "b_forget": jax.random.uniform(next(ks), (L, N_FOX_HEADS), f32, 1.0, 4.0),
        "lru_conv_w": normal((L, LRU_CONV_WIDTH, LRU_WIDTH), LRU_CONV_WIDTH ** -0.5),
        "lru_conv_b": normal((L, LRU_WIDTH), 0.01),
        "lru_w_a": normal((L, N_LRU_BLOCKS, LRU_BLOCK, LRU_BLOCK), LRU_BLOCK ** -0.5),
        "lru_b_a": normal((L, LRU_WIDTH), 0.01),
        "lru_w_x": normal((L, N_LRU_BLOCKS, LRU_BLOCK, LRU_BLOCK), LRU_BLOCK ** -0.5),
        "lru_b_x": normal((L, LRU_WIDTH), 0.01),
        "lru_lambda": lru_lambda,
        "w_out": normal((L, MIX_WIDTH, D_MODEL), MIX_WIDTH ** -0.5),
        "norm_cross_g": gain((L, D_MODEL)),
        "norm_mem_g": gain((L, D_MODEL)),
        "w_cq": normal((L, D_MODEL, CROSS_W), D_MODEL ** -0.5),
        "w_ck": normal((L, D_MODEL, CROSS_W), D_MODEL ** -0.5),
        "w_cv": normal((L, D_MODEL, CROSS_W), D_MODEL ** -0.5),
        "w_co": normal((L, CROSS_W, D_MODEL), CROSS_W ** -0.5),
        "norm_ffn_g": gain((L, D_MODEL)),
        "w_up": normal((L, D_MODEL, 2 * D_FF), D_MODEL ** -0.5),
        "ffn_conv_w": normal((L, FFN_CONV_WIDTH, 2 * D_FF), FFN_CONV_WIDTH ** -0.5),
        "ffn_conv_b": normal((L, 2 * D_FF), 0.01),
        "w_down": normal((L, D_FF, D_MODEL), D_FF ** -0.5),
        "rel_bias": normal((NUM_BUCKETS, N_DIL_HEADS), 0.5),
        "final_norm_g": gain((D_MODEL,)),
    }


def _fwd_reference(x, mem, norm_mix_g, w_in, b_forget, lru_conv_w, lru_conv_b, lru_w_a, lru_b_a,
              lru_w_x, lru_b_x, lru_lambda, w_out, norm_cross_g, norm_mem_g, w_cq, w_ck, w_cv,
              w_co, norm_ffn_g, w_up, ffn_conv_w, ffn_conv_b, w_down, rel_bias, final_norm_g):
    col_sizes = [3 * SB_W, 3 * FOX_W, N_FOX_HEADS, 3 * DIL_W, LRU_WIDTH, LRU_WIDTH]
    for l in range(DEPTH):
        h = rms_norm(x, norm_mix_g[l])
        proj = h @ w_in[l]
        sb_qkv, fox_qkv, fox_f, dil_qkv, lru_x, lru_gate = split_cols(proj, col_sizes)
        o_sb = stick_breaking_attention(*split_qkv(sb_qkv, N_SB_HEADS))
        o_fox = forgetting_attention(*split_qkv(fox_qkv, N_FOX_HEADS), fox_f + b_forget[l])
        o_dil = dilated_attention(*split_qkv(dil_qkv, N_DIL_HEADS), rel_bias)
        o_lru = rg_lru_branch(lru_x, lru_gate, lru_conv_w[l], lru_conv_b[l], lru_w_a[l],
                              lru_b_a[l], lru_w_x[l], lru_b_x[l], lru_lambda[l])
        mixed = jnp.concatenate([merge_heads(o_sb), merge_heads(o_fox), merge_heads(o_dil), o_lru],
                                axis=-1).astype(x.dtype)
        x = x + mixed @ w_out[l]
        x = x + memory_cross_attention(rms_norm(x, norm_cross_g[l]), rms_norm(mem, norm_mem_g[l]),
                                       w_cq[l], w_ck[l], w_cv[l], w_co[l])
        hf = rms_norm(x, norm_ffn_g[l]) @ w_up[l]
        hf = causal_dwconv(hf, ffn_conv_w[l], ffn_conv_b[l])
        up, gate = hf[..., :D_FF], hf[..., D_FF:]
        x = x + (jax.nn.gelu(gate, approximate=False) * up).astype(x.dtype) @ w_down[l]
    return rms_norm(x, final_norm_g)


import jax as _jax
import jax.numpy as _jnp

TWIN_FORMAT = 'train_step'
FWD_PARAMS = ['x', 'mem', 'norm_mix_g', 'w_in', 'b_forget', 'lru_conv_w', 'lru_conv_b', 'lru_w_a', 'lru_b_a', 'lru_w_x', 'lru_b_x', 'lru_lambda', 'w_out', 'norm_cross_g', 'norm_mem_g', 'w_cq', 'w_ck', 'w_cv', 'w_co', 'norm_ffn_g', 'w_up', 'ffn_conv_w', 'ffn_conv_b', 'w_down', 'rel_bias', 'final_norm_g']
TWIN_WEIGHTS = ['norm_mix_g', 'w_in', 'b_forget', 'lru_conv_w', 'lru_conv_b', 'lru_w_a', 'lru_b_a', 'lru_w_x', 'lru_b_x', 'lru_lambda', 'w_out', 'norm_cross_g', 'norm_mem_g', 'w_cq', 'w_ck', 'w_cv', 'w_co', 'norm_ffn_g', 'w_up', 'ffn_conv_w', 'ffn_conv_b', 'w_down', 'rel_bias', 'final_norm_g']
TWIN_DIFF_INPUT = 'x'
TWIN_INPUTS = ['x', 'mem', 'norm_mix_g', 'w_in', 'b_forget', 'lru_conv_w', 'lru_conv_b', 'lru_w_a', 'lru_b_a', 'lru_w_x', 'lru_b_x', 'lru_lambda', 'w_out', 'norm_cross_g', 'norm_mem_g', 'w_cq', 'w_ck', 'w_cv', 'w_co', 'norm_ffn_g', 'w_up', 'ffn_conv_w', 'ffn_conv_b', 'w_down', 'rel_bias', 'final_norm_g', 'loss_target', 'm_norm_mix_g', 'm_w_in', 'm_b_forget', 'm_lru_conv_w', 'm_lru_conv_b', 'm_lru_w_a', 'm_lru_b_a', 'm_lru_w_x', 'm_lru_b_x', 'm_lru_lambda', 'm_w_out', 'm_norm_cross_g', 'm_norm_mem_g', 'm_w_cq', 'm_w_ck', 'm_w_cv', 'm_w_co', 'm_norm_ffn_g', 'm_w_up', 'm_ffn_conv_w', 'm_ffn_conv_b', 'm_w_down', 'm_rel_bias', 'm_final_norm_g', 'v_norm_mix_g', 'v_w_in', 'v_b_forget', 'v_lru_conv_w', 'v_lru_conv_b', 'v_lru_w_a', 'v_lru_b_a', 'v_lru_w_x', 'v_lru_b_x', 'v_lru_lambda', 'v_w_out', 'v_norm_cross_g', 'v_norm_mem_g', 'v_w_cq', 'v_w_ck', 'v_w_cv', 'v_w_co', 'v_norm_ffn_g', 'v_w_up', 'v_ffn_conv_w', 'v_ffn_conv_b', 'v_w_down', 'v_rel_bias', 'v_final_norm_g']
TWIN_OUTPUTS = ['loss', 'grad_x', 'grad_norm_mix_g', 'grad_w_in', 'grad_b_forget', 'grad_lru_conv_w', 'grad_lru_conv_b', 'grad_lru_w_a', 'grad_lru_b_a', 'grad_lru_w_x', 'grad_lru_b_x', 'grad_lru_lambda', 'grad_w_out', 'grad_norm_cross_g', 'grad_norm_mem_g', 'grad_w_cq', 'grad_w_ck', 'grad_w_cv', 'grad_w_co', 'grad_norm_ffn_g', 'grad_w_up', 'grad_ffn_conv_w', 'grad_ffn_conv_b', 'grad_w_down', 'grad_rel_bias', 'grad_final_norm_g', 'delta_norm_mix_g', 'delta_w_in', 'delta_b_forget', 'delta_lru_conv_w', 'delta_lru_conv_b', 'delta_lru_w_a', 'delta_lru_b_a', 'delta_lru_w_x', 'delta_lru_b_x', 'delta_lru_lambda', 'delta_w_out', 'delta_norm_cross_g', 'delta_norm_mem_g', 'delta_w_cq', 'delta_w_ck', 'delta_w_cv', 'delta_w_co', 'delta_norm_ffn_g', 'delta_w_up', 'delta_ffn_conv_w', 'delta_ffn_conv_b', 'delta_w_down', 'delta_rel_bias', 'delta_final_norm_g', 'new_m_norm_mix_g', 'new_m_w_in', 'new_m_b_forget', 'new_m_lru_conv_w', 'new_m_lru_conv_b', 'new_m_lru_w_a', 'new_m_lru_b_a', 'new_m_lru_w_x', 'new_m_lru_b_x', 'new_m_lru_lambda', 'new_m_w_out', 'new_m_norm_cross_g', 'new_m_norm_mem_g', 'new_m_w_cq', 'new_m_w_ck', 'new_m_w_cv', 'new_m_w_co', 'new_m_norm_ffn_g', 'new_m_w_up', 'new_m_ffn_conv_w', 'new_m_ffn_conv_b', 'new_m_w_down', 'new_m_rel_bias', 'new_m_final_norm_g', 'new_v_norm_mix_g', 'new_v_w_in', 'new_v_b_forget', 'new_v_lru_conv_w', 'new_v_lru_conv_b', 'new_v_lru_w_a', 'new_v_lru_b_a', 'new_v_lru_w_x', 'new_v_lru_b_x', 'new_v_lru_lambda', 'new_v_w_out', 'new_v_norm_cross_g', 'new_v_norm_mem_g', 'new_v_w_cq', 'new_v_w_ck', 'new_v_w_cv', 'new_v_w_co', 'new_v_norm_ffn_g', 'new_v_w_up', 'new_v_ffn_conv_w', 'new_v_ffn_conv_b', 'new_v_w_down', 'new_v_rel_bias', 'new_v_final_norm_g']
TWIN_LEAF_KINDS = {'loss': 'loss', 'grad_x': 'grad_x', 'grad_norm_mix_g': 'grad_w', 'grad_w_in': 'grad_w', 'grad_b_forget': 'grad_w', 'grad_lru_conv_w': 'grad_w', 'grad_lru_conv_b': 'grad_w', 'grad_lru_w_a': 'grad_w', 'grad_lru_b_a': 'grad_w', 'grad_lru_w_x': 'grad_w', 'grad_lru_b_x': 'grad_w', 'grad_lru_lambda': 'grad_w', 'grad_w_out': 'grad_w', 'grad_norm_cross_g': 'grad_w', 'grad_norm_mem_g': 'grad_w', 'grad_w_cq': 'grad_w', 'grad_w_ck': 'grad_w', 'grad_w_cv': 'grad_w', 'grad_w_co': 'grad_w', 'grad_norm_ffn_g': 'grad_w', 'grad_w_up': 'grad_w', 'grad_ffn_conv_w': 'grad_w', 'grad_ffn_conv_b': 'grad_w', 'grad_w_down': 'grad_w', 'grad_rel_bias': 'grad_w', 'grad_final_norm_g': 'grad_w', 'delta_norm_mix_g': 'delta_w', 'delta_w_in': 'delta_w', 'delta_b_forget': 'delta_w', 'delta_lru_conv_w': 'delta_w', 'delta_lru_conv_b': 'delta_w', 'delta_lru_w_a': 'delta_w', 'delta_lru_b_a': 'delta_w', 'delta_lru_w_x': 'delta_w', 'delta_lru_b_x': 'delta_w', 'delta_lru_lambda': 'delta_w', 'delta_w_out': 'delta_w', 'delta_norm_cross_g': 'delta_w', 'delta_norm_mem_g': 'delta_w', 'delta_w_cq': 'delta_w', 'delta_w_ck': 'delta_w', 'delta_w_cv': 'delta_w', 'delta_w_co': 'delta_w', 'delta_norm_ffn_g': 'delta_w', 'delta_w_up': 'delta_w', 'delta_ffn_conv_w': 'delta_w', 'delta_ffn_conv_b': 'delta_w', 'delta_w_down': 'delta_w', 'delta_rel_bias': 'delta_w', 'delta_final_norm_g': 'delta_w', 'new_m_norm_mix_g': 'new_m', 'new_m_w_in': 'new_m', 'new_m_b_forget': 'new_m', 'new_m_lru_conv_w': 'new_m', 'new_m_lru_conv_b': 'new_m', 'new_m_lru_w_a': 'new_m', 'new_m_lru_b_a': 'new_m', 'new_m_lru_w_x': 'new_m', 'new_m_lru_b_x': 'new_m', 'new_m_lru_lambda': 'new_m', 'new_m_w_out': 'new_m', 'new_m_norm_cross_g': 'new_m', 'new_m_norm_mem_g': 'new_m', 'new_m_w_cq': 'new_m', 'new_m_w_ck': 'new_m', 'new_m_w_cv': 'new_m', 'new_m_w_co': 'new_m', 'new_m_norm_ffn_g': 'new_m', 'new_m_w_up': 'new_m', 'new_m_ffn_conv_w': 'new_m', 'new_m_ffn_conv_b': 'new_m', 'new_m_w_down': 'new_m', 'new_m_rel_bias': 'new_m', 'new_m_final_norm_g': 'new_m', 'new_v_norm_mix_g': 'new_v', 'new_v_w_in': 'new_v', 'new_v_b_forget': 'new_v', 'new_v_lru_conv_w': 'new_v', 'new_v_lru_conv_b': 'new_v', 'new_v_lru_w_a': 'new_v', 'new_v_lru_b_a': 'new_v', 'new_v_lru_w_x': 'new_v', 'new_v_lru_b_x': 'new_v', 'new_v_lru_lambda': 'new_v', 'new_v_w_out': 'new_v', 'new_v_norm_cross_g': 'new_v', 'new_v_norm_mem_g': 'new_v', 'new_v_w_cq': 'new_v', 'new_v_w_ck': 'new_v', 'new_v_w_cv': 'new_v', 'new_v_w_co': 'new_v', 'new_v_norm_ffn_g': 'new_v', 'new_v_w_up': 'new_v', 'new_v_ffn_conv_w': 'new_v', 'new_v_ffn_conv_b': 'new_v', 'new_v_w_down': 'new_v', 'new_v_rel_bias': 'new_v', 'new_v_final_norm_g': 'new_v'}


def _forward(args):
    return _fwd_reference(*[args[k] for k in FWD_PARAMS])


def _output_shape():
    out = _jax.eval_shape(lambda: _forward(_fwd_setup_inputs(0)))
    return out.shape, out.dtype

N_MICROBATCH = 1
ADAM_LR = 0.001
ADAM_B1 = 0.9
ADAM_B2 = 0.999
ADAM_EPS = 1e-08
ADAM_WD = 0.01
ADAM_STEP = 10
PER_EXAMPLE_BATCH_AXIS = {'x': 0, 'mem': 0, 'loss_target': 0}
SHARED_INPUTS = []
_WEIGHT_DTYPES = {'norm_mix_g': _jnp.float32, 'w_in': _jnp.float32, 'b_forget': _jnp.float32, 'lru_conv_w': _jnp.float32, 'lru_conv_b': _jnp.float32, 'lru_w_a': _jnp.float32, 'lru_b_a': _jnp.float32, 'lru_w_x': _jnp.float32, 'lru_b_x': _jnp.float32, 'lru_lambda': _jnp.float32, 'w_out': _jnp.float32, 'norm_cross_g': _jnp.float32, 'norm_mem_g': _jnp.float32, 'w_cq': _jnp.float32, 'w_ck': _jnp.float32, 'w_cv': _jnp.float32, 'w_co': _jnp.float32, 'norm_ffn_g': _jnp.float32, 'w_up': _jnp.float32, 'ffn_conv_w': _jnp.float32, 'ffn_conv_b': _jnp.float32, 'w_down': _jnp.float32, 'rel_bias': _jnp.float32, 'final_norm_g': _jnp.float32}
MOMENT_SCALE = {'norm_mix_g': 1.022867e-01, 'w_in': 6.135688e-02, 'b_forget': 2.955066e-01, 'lru_conv_w': 1.038729e-01, 'lru_conv_b': 1.145046e+00, 'lru_w_a': 3.924040e-02, 'lru_b_a': 3.069439e-02, 'lru_w_x': 6.754256e-02, 'lru_b_x': 4.193486e-02, 'lru_lambda': 5.554843e-02, 'w_out': 7.923533e-02, 'norm_cross_g': 1.904768e-02, 'norm_mem_g': 3.162616e-02, 'w_cq': 3.837831e-02, 'w_ck': 3.897824e-02, 'w_cv': 4.448182e-02, 'w_co': 2.240606e-02, 'norm_ffn_g': 1.360993e-01, 'w_up': 5.665401e-02, 'ffn_conv_w': 5.596644e-02, 'ffn_conv_b': 6.049407e-02, 'w_down': 9.246962e-02, 'rel_bias': 6.686498e-02, 'final_norm_g': 3.197567e+01}


def _to_microbatches(a, axis):
    t = _jnp.moveaxis(a, axis, 0)
    t = t.reshape((N_MICROBATCH, t.shape[0] // N_MICROBATCH) + t.shape[1:])
    return _jnp.moveaxis(t, 1, axis + 1)


def setup_inputs(seed: int = 0) -> dict:
    inp = _fwd_setup_inputs(seed)
    key = _jax.random.fold_in(_jax.random.key(seed), 7919)
    shape, _ = _output_shape()
    out = dict(inp)
    out["loss_target"] = _jax.random.normal(_jax.random.fold_in(key, 0), shape, _jnp.float32)
    for i, name in enumerate(TWIN_WEIGHTS):
        w = inp[name].astype(_jnp.float32)
        if MOMENT_SCALE is None:
            s = _jnp.sqrt(_jnp.mean(_jnp.square(w)) + 1e-30)
        else:
            s = MOMENT_SCALE[name]
        km, kv = _jax.random.split(_jax.random.fold_in(key, i + 1))
        out[name] = w
        out["m_" + name] = s * _jax.random.normal(km, w.shape, _jnp.float32)
        out["v_" + name] = (s * s) * _jax.random.uniform(kv, w.shape, _jnp.float32, 0.5, 1.5)
    if N_MICROBATCH > 1:
        for name, axis in PER_EXAMPLE_BATCH_AXIS.items():
            out[name] = _to_microbatches(out[name], axis)
    return {'x': out['x'], 'mem': out['mem'], 'norm_mix_g': out['norm_mix_g'], 'w_in': out['w_in'], 'b_forget': out['b_forget'], 'lru_conv_w': out['lru_conv_w'], 'lru_conv_b': out['lru_conv_b'], 'lru_w_a': out['lru_w_a'], 'lru_b_a': out['lru_b_a'], 'lru_w_x': out['lru_w_x'], 'lru_b_x': out['lru_b_x'], 'lru_lambda': out['lru_lambda'], 'w_out': out['w_out'], 'norm_cross_g': out['norm_cross_g'], 'norm_mem_g': out['norm_mem_g'], 'w_cq': out['w_cq'], 'w_ck': out['w_ck'], 'w_cv': out['w_cv'], 'w_co': out['w_co'], 'norm_ffn_g': out['norm_ffn_g'], 'w_up': out['w_up'], 'ffn_conv_w': out['ffn_conv_w'], 'ffn_conv_b': out['ffn_conv_b'], 'w_down': out['w_down'], 'rel_bias': out['rel_bias'], 'final_norm_g': out['final_norm_g'], 'loss_target': out['loss_target'], 'm_norm_mix_g': out['m_norm_mix_g'], 'm_w_in': out['m_w_in'], 'm_b_forget': out['m_b_forget'], 'm_lru_conv_w': out['m_lru_conv_w'], 'm_lru_conv_b': out['m_lru_conv_b'], 'm_lru_w_a': out['m_lru_w_a'], 'm_lru_b_a': out['m_lru_b_a'], 'm_lru_w_x': out['m_lru_w_x'], 'm_lru_b_x': out['m_lru_b_x'], 'm_lru_lambda': out['m_lru_lambda'], 'm_w_out': out['m_w_out'], 'm_norm_cross_g': out['m_norm_cross_g'], 'm_norm_mem_g': out['m_norm_mem_g'], 'm_w_cq': out['m_w_cq'], 'm_w_ck': out['m_w_ck'], 'm_w_cv': out['m_w_cv'], 'm_w_co': out['m_w_co'], 'm_norm_ffn_g': out['m_norm_ffn_g'], 'm_w_up': out['m_w_up'], 'm_ffn_conv_w': out['m_ffn_conv_w'], 'm_ffn_conv_b': out['m_ffn_conv_b'], 'm_w_down': out['m_w_down'], 'm_rel_bias': out['m_rel_bias'], 'm_final_norm_g': out['m_final_norm_g'], 'v_norm_mix_g': out['v_norm_mix_g'], 'v_w_in': out['v_w_in'], 'v_b_forget': out['v_b_forget'], 'v_lru_conv_w': out['v_lru_conv_w'], 'v_lru_conv_b': out['v_lru_conv_b'], 'v_lru_w_a': out['v_lru_w_a'], 'v_lru_b_a': out['v_lru_b_a'], 'v_lru_w_x': out['v_lru_w_x'], 'v_lru_b_x': out['v_lru_b_x'], 'v_lru_lambda': out['v_lru_lambda'], 'v_w_out': out['v_w_out'], 'v_norm_cross_g': out['v_norm_cross_g'], 'v_norm_mem_g': out['v_norm_mem_g'], 'v_w_cq': out['v_w_cq'], 'v_w_ck': out['v_w_ck'], 'v_w_cv': out['v_w_cv'], 'v_w_co': out['v_w_co'], 'v_norm_ffn_g': out['v_norm_ffn_g'], 'v_w_up': out['v_w_up'], 'v_ffn_conv_w': out['v_ffn_conv_w'], 'v_ffn_conv_b': out['v_ffn_conv_b'], 'v_w_down': out['v_w_down'], 'v_rel_bias': out['v_rel_bias'], 'v_final_norm_g': out['v_final_norm_g']}


def _loss(weights, diff, rest, loss_target):
    with _jax.named_scope("forward"):
        args = {**rest, TWIN_DIFF_INPUT: diff, **{k: w.astype(_WEIGHT_DTYPES[k]) for k, w in weights.items()}}
        y = _forward(args)
    with _jax.named_scope("loss_head"):
        err = _jnp.square(y.astype(_jnp.float32) - loss_target)
        return 0.5 * _jnp.sum(_jnp.mean(err, axis=-1)) if err.ndim else 0.5 * err


def _adamw(w, g, m, v):
    m = ADAM_B1 * m + (1.0 - ADAM_B1) * g
    v = ADAM_B2 * v + (1.0 - ADAM_B2) * _jnp.square(g)
    m_hat = m / (1.0 - ADAM_B1 ** ADAM_STEP)
    v_hat = v / (1.0 - ADAM_B2 ** ADAM_STEP)
    delta = -ADAM_LR * (m_hat / (_jnp.sqrt(v_hat) + ADAM_EPS) + ADAM_WD * w)
    return delta, m, v


def reference(x, mem, norm_mix_g, w_in, b_forget, lru_conv_w, lru_conv_b, lru_w_a, lru_b_a, lru_w_x, lru_b_x, lru_lambda, w_out, norm_cross_g, norm_mem_g, w_cq, w_ck, w_cv, w_co, norm_ffn_g, w_up, ffn_conv_w, ffn_conv_b, w_down, rel_bias, final_norm_g, loss_target, m_norm_mix_g, m_w_in, m_b_forget, m_lru_conv_w, m_lru_conv_b, m_lru_w_a, m_lru_b_a, m_lru_w_x, m_lru_b_x, m_lru_lambda, m_w_out, m_norm_cross_g, m_norm_mem_g, m_w_cq, m_w_ck, m_w_cv, m_w_co, m_norm_ffn_g, m_w_up, m_ffn_conv_w, m_ffn_conv_b, m_w_down, m_rel_bias, m_final_norm_g, v_norm_mix_g, v_w_in, v_b_forget, v_lru_conv_w, v_lru_conv_b, v_lru_w_a, v_lru_b_a, v_lru_w_x, v_lru_b_x, v_lru_lambda, v_w_out, v_norm_cross_g, v_norm_mem_g, v_w_cq, v_w_ck, v_w_cv, v_w_co, v_norm_ffn_g, v_w_up, v_ffn_conv_w, v_ffn_conv_b, v_w_down, v_rel_bias, v_final_norm_g):
    given = dict(x=x, mem=mem, norm_mix_g=norm_mix_g, w_in=w_in, b_forget=b_forget, lru_conv_w=lru_conv_w, lru_conv_b=lru_conv_b, lru_w_a=lru_w_a, lru_b_a=lru_b_a, lru_w_x=lru_w_x, lru_b_x=lru_b_x, lru_lambda=lru_lambda, w_out=w_out, norm_cross_g=norm_cross_g, norm_mem_g=norm_mem_g, w_cq=w_cq, w_ck=w_ck, w_cv=w_cv, w_co=w_co, norm_ffn_g=norm_ffn_g, w_up=w_up, ffn_conv_w=ffn_conv_w, ffn_conv_b=ffn_conv_b, w_down=w_down, rel_bias=rel_bias, final_norm_g=final_norm_g, loss_target=loss_target, m_norm_mix_g=m_norm_mix_g, m_w_in=m_w_in, m_b_forget=m_b_forget, m_lru_conv_w=m_lru_conv_w, m_lru_conv_b=m_lru_conv_b, m_lru_w_a=m_lru_w_a, m_lru_b_a=m_lru_b_a, m_lru_w_x=m_lru_w_x, m_lru_b_x=m_lru_b_x, m_lru_lambda=m_lru_lambda, m_w_out=m_w_out, m_norm_cross_g=m_norm_cross_g, m_norm_mem_g=m_norm_mem_g, m_w_cq=m_w_cq, m_w_ck=m_w_ck, m_w_cv=m_w_cv, m_w_co=m_w_co, m_norm_ffn_g=m_norm_ffn_g, m_w_up=m_w_up, m_ffn_conv_w=m_ffn_conv_w, m_ffn_conv_b=m_ffn_conv_b, m_w_down=m_w_down, m_rel_bias=m_rel_bias, m_final_norm_g=m_final_norm_g, v_norm_mix_g=v_norm_mix_g, v_w_in=v_w_in, v_b_forget=v_b_forget, v_lru_conv_w=v_lru_conv_w, v_lru_conv_b=v_lru_conv_b, v_lru_w_a=v_lru_w_a, v_lru_b_a=v_lru_b_a, v_lru_w_x=v_lru_w_x, v_lru_b_x=v_lru_b_x, v_lru_lambda=v_lru_lambda, v_w_out=v_w_out, v_norm_cross_g=v_norm_cross_g, v_norm_mem_g=v_norm_mem_g, v_w_cq=v_w_cq, v_w_ck=v_w_ck, v_w_cv=v_w_cv, v_w_co=v_w_co, v_norm_ffn_g=v_norm_ffn_g, v_w_up=v_w_up, v_ffn_conv_w=v_ffn_conv_w, v_ffn_conv_b=v_ffn_conv_b, v_w_down=v_w_down, v_rel_bias=v_rel_bias, v_final_norm_g=v_final_norm_g)
    weights = {n: given[n] for n in TWIN_WEIGHTS}
    shared = {n: given[n] for n in SHARED_INPUTS}
    per_example = {n: given[n] for n in ['x', 'mem']}
    grad_fn = _jax.value_and_grad(_loss, argnums=(0, 1))

    def one_microbatch(ex, loss_target):
        ex = dict(ex)
        diff = ex.pop(TWIN_DIFF_INPUT)
        return grad_fn(weights, diff, {**shared, **ex}, loss_target)

    if N_MICROBATCH == 1:
        loss, (grad_w, grad_x) = one_microbatch(per_example, given["loss_target"])
    else:
        def body(carry, xs):
            loss_sum, grad_sum = carry
            l_k, (gw_k, gx_k) = one_microbatch(xs[0], xs[1])
            with _jax.named_scope("update"):
                return (loss_sum + l_k, _jax.tree.map(_jnp.add, grad_sum, gw_k)), gx_k

        init = (_jnp.zeros((), _jnp.float32), _jax.tree.map(_jnp.zeros_like, weights))
        (loss, grad_w), grad_x = _jax.lax.scan(body, init, (per_example, given["loss_target"]))
    with _jax.named_scope("update"):
        delta_w, new_m, new_v = {}, {}, {}
        for n in TWIN_WEIGHTS:
            delta_w[n], new_m[n], new_v[n] = _adamw(weights[n], grad_w[n], given["m_" + n], given["v_" + n])
    return (loss, grad_x, *[grad_w[n] for n in TWIN_WEIGHTS], *[delta_w[n] for n in TWIN_WEIGHTS],
            *[new_m[n] for n in TWIN_WEIGHTS], *[new_v[n] for n in TWIN_WEIGHTS])
```

```python
import functools
import math

import numpy as np
import jax
import jax.numpy as jnp
from jax import lax
from jax.experimental import pallas as pl
from jax.experimental.pallas import tpu as pltpu

F32 = jnp.float32
BF16 = jnp.bfloat16

D_MODEL = 1024
HEAD_DIM = 64
N_HEADS = 4
GROUP_W = N_HEADS * HEAD_DIM
N_IN = 2820
D_FF = 2816
LRU_C = 8.0
EPS = 1e-6
NUM_BUCKETS = 32
MAX_DISTANCE = 2048
DILATED_PATTERNS = ((128, 1), (512, 4), (2048, 16))
ADAM_LR, ADAM_B1, ADAM_B2, ADAM_EPS, ADAM_WD, ADAM_STEP = 0.001, 0.9, 0.999, 1e-08, 0.01, 10

LANES = 128
SUBLANES = 8
VMEM_LIMIT = 48 * 1024 * 1024

PROJ_W = 3072
COL_SB, COL_FOX, COL_DIL, COL_LRUX, COL_LRUG, COL_F = 0, 768, 1536, 2304, 2560, 2816

ATT_TILE = 256
MASKED = -1e30
SCALE = HEAD_DIM ** -0.5

NT_DIMS = (((1,), (1,)), ((), ()))
TN_DIMS = (((0,), (0,)), ((), ()))


def _params(sem):
    return pltpu.CompilerParams(dimension_semantics=sem, vmem_limit_bytes=VMEM_LIMIT)


def _tile(n, target, unit=LANES):
    if n <= target:
        return n
    t = (target // unit) * unit
    while t > unit and n % t:
        t -= unit
    assert n % t == 0, (n, target, unit)
    return t


def _mm(a, b, *, ta=False, tb=False, res=None, name, ti=512, tj=512, tc=1408):
    m, kc = (a.shape[1], a.shape[0]) if ta else a.shape
    n = b.shape[0] if tb else b.shape[1]
    assert (b.shape[1] if tb else b.shape[0]) == kc
    ti, tj, tc = _tile(m, ti, LANES if ta else SUBLANES), _tile(n, tj), _tile(kc, tc, SUBLANES if ta and tb else LANES)
    nk = kc // tc
    dims = (((0 if ta else 1,), (1 if tb else 0,)), ((), ()))

    def body(*refs):
        if res is None:
            a_ref, b_ref, o_ref, acc_ref = refs
        else:
            a_ref, b_ref, r_ref, o_ref, acc_ref = refs
        k = pl.program_id(2)

        @pl.when(k == 0)
        def _():
            acc_ref[...] = jnp.zeros_like(acc_ref)

        acc_ref[...] += lax.dot_general(a_ref[...].astype(BF16), b_ref[...].astype(BF16), dims,
                                        preferred_element_type=F32)

        @pl.when(k == nk - 1)
        def _():
            if res is None:
                o_ref[...] = acc_ref[...]
            else:
                o_ref[...] = r_ref[...] + acc_ref[...]

    a_spec = pl.BlockSpec((tc, ti), lambda i, j, k: (k, i)) if ta else pl.BlockSpec((ti, tc), lambda i, j, k: (i, k))
    b_spec = pl.BlockSpec((tj, tc), lambda i, j, k: (j, k)) if tb else pl.BlockSpec((tc, tj), lambda i, j, k: (k, j))
    o_spec = pl.BlockSpec((ti, tj), lambda i, j, k: (i, j))
    in_specs = [a_spec, b_spec] + ([] if res is None else [o_spec])
    args = (a, b) + (() if res is None else (res,))
    return pl.pallas_call(
        body, name=name, grid=(m // ti, n // tj, nk), in_specs=in_specs, out_specs=o_spec,
        out_shape=jax.ShapeDtypeStruct((m, n), F32), scratch_shapes=[pltpu.VMEM((ti, tj), F32)],
        compiler_params=_params(("parallel", "parallel", "arbitrary")))(*args)


def _xhat(x):
    return x * lax.rsqrt(jnp.mean(x * x, axis=-1, keepdims=True) + EPS)


def _norm_bwd_rows(dy, x, g):
    rstd = lax.rsqrt(jnp.mean(x * x, axis=-1, keepdims=True) + EPS)
    xh = x * rstd
    dxh = dy * g
    dx = rstd * (dxh - xh * jnp.mean(dxh * xh, axis=-1, keepdims=True))
    return dx, dy * xh


def _rmsnorm(x, g, *, name, rows=512):
    t, d = x.shape
    tr = _tile(t, rows, SUBLANES)

    def body(x_ref, g_ref, o_ref):
        o_ref[...] = _xhat(x_ref[...]) * g_ref[...]

    return pl.pallas_call(
        body, name=name, grid=(t // tr,),
        in_specs=[pl.BlockSpec((tr, d), lambda i: (i, 0)), pl.BlockSpec((1, d), lambda i: (0, 0))],
        out_specs=pl.BlockSpec((tr, d), lambda i: (i, 0)), out_shape=jax.ShapeDtypeStruct((t, d), F32),
        compiler_params=_params(("parallel",)))(x, g.reshape(1, d))


def _rmsnorm_bwd(dy, x, g, dres, *, name, rows=512):
    t, d = x.shape
    tr = _tile(t, rows, SUBLANES)

    def body(*refs):
        if dres is None:
            dy_ref, x_ref, g_ref, dx_ref, dg_ref = refs
        else:
            dy_ref, x_ref, g_ref, r_ref, dx_ref, dg_ref = refs
        dx, dgr = _norm_bwd_rows(dy_ref[...], x_ref[...], g_ref[...])
        dx_ref[...] = dx if dres is None else r_ref[...] + dx

        @pl.when(pl.program_id(0) == 0)
        def _():
            dg_ref[...] = jnp.zeros_like(dg_ref)

        dg_ref[...] += jnp.sum(dgr, axis=0, keepdims=True)

    row = pl.BlockSpec((tr, d), lambda i: (i, 0))
    vec = pl.BlockSpec((1, d), lambda i: (0, 0))
    in_specs = [row, row, vec] + ([] if dres is None else [row])
    args = (dy, x, g.reshape(1, d)) + (() if dres is None else (dres,))
    dx, dg = pl.pallas_call(
        body, name=name, grid=(t // tr,), in_specs=in_specs, out_specs=[row, vec],
        out_shape=[jax.ShapeDtypeStruct((t, d), F32), jax.ShapeDtypeStruct((1, d), F32)],
        compiler_params=_params(("arbitrary",)))(*args)
    return dx, dg.reshape(d)


def _loss_head(x, g, target, *, rows=512):
    t, d = x.shape
    tr = _tile(t, rows, SUBLANES)

    def body(x_ref, g_ref, t_ref, dx_ref, dg_ref, loss_ref):
        x_, g_ = x_ref[...], g_ref[...]
        err = _xhat(x_) * g_ - t_ref[...]
        dx, dgr = _norm_bwd_rows(err * (1.0 / d), x_, g_)
        dx_ref[...] = dx

        @pl.when(pl.program_id(0) == 0)
        def _():
            dg_ref[...] = jnp.zeros_like(dg_ref)
            loss_ref[...] = jnp.zeros_like(loss_ref)

        dg_ref[...] += jnp.sum(dgr, axis=0, keepdims=True)
        loss_ref[...] += 0.5 * jnp.sum(jnp.mean(err * err, axis=-1, keepdims=True), axis=0, keepdims=True)

    row = pl.BlockSpec((tr, d), lambda i: (i, 0))
    vec = pl.BlockSpec((1, d), lambda i: (0, 0))
    one = pl.BlockSpec((1, 1), lambda i: (0, 0))
    dx, dg, loss = pl.pallas_call(
        body, name="loss_head", grid=(t // tr,), in_specs=[row, vec, row], out_specs=[row, vec, one],
        out_shape=[jax.ShapeDtypeStruct((t, d), F32), jax.ShapeDtypeStruct((1, d), F32),
                   jax.ShapeDtypeStruct((1, 1), F32)],
        compiler_params=_params(("arbitrary",)))(x, g.reshape(1, d), target)
    return loss.reshape(()), dx, dg.reshape(d)


def _head_masks(shape):
    lane = lax.broadcasted_iota(jnp.int32, shape, len(shape) - 1)
    return lane < HEAD_DIM, lane >= HEAD_DIM


def _split_heads(x):
    m0, m1 = _head_masks(x.shape)
    zero = jnp.zeros_like(x)
    return jnp.where(m0, x, zero), jnp.where(m1, x, zero)


def _lane_pair(a0, a1, rows):
    m0, _ = _head_masks((rows, LANES))
    return jnp.where(m0, a0, a1)


def _att_specs(seq_q, seq_k, qcol, kcol, vcol):
    q = pl.BlockSpec((seq_q, LANES), lambda p, b: (b, qcol + p))
    k = pl.BlockSpec((seq_k, LANES), lambda p, b: (b, kcol + p))
    v = pl.BlockSpec((seq_k, LANES), lambda p, b: (b, vcol + p))
    o = pl.BlockSpec((seq_q, LANES), lambda p, b: (b, p))
    okv = pl.BlockSpec((seq_k, LANES), lambda p, b: (b, p))
    return q, k, v, o, okv


def _softmax_attn_fwd(qa, ka, va, *, nb, qcol, kcol, vcol, mode, extra=(), name):
    seq_q, seq_k = qa.shape[0] // nb, ka.shape[0] // nb
    tq, tk = min(ATT_TILE, seq_q), min(ATT_TILE, seq_k)
    nq, nk = seq_q // tq, seq_k // tk
    causal = mode != "cross"
    assert not causal or (tq == tk and seq_q == seq_k)

    def body(*refs):
        q_ref, k_ref, v_ref = refs[:3]
        o_ref, lse_ref = refs[-2:]
        ex = refs[3:-2]

        def q_tile(i, _):
            r0 = pl.multiple_of(i * tq, tq)
            q = (q_ref[pl.ds(r0, tq), :] * SCALE).astype(BF16)
            qm = _split_heads(q)
            if mode == "fox":
                cq = ex[0][pl.ds(r0, tq), :]
                row = r0 + lax.broadcasted_iota(jnp.int32, (tq, tk), 0)

            def k_tile(j, carry):
                m, l, acc = carry
                c0 = pl.multiple_of(j * tk, tk)
                kt = k_ref[pl.ds(c0, tk), :].astype(BF16)
                vm = _split_heads(v_ref[pl.ds(c0, tk), :].astype(BF16))
                if mode == "fox":
                    ck = ex[1][j]
                    keep = (c0 + lax.broadcasted_iota(jnp.int32, (tq, tk), 1)) <= row
                new_m, new_l, alphas, pv = [], [], [], None
                for h in range(2):
                    s = lax.dot_general(qm[h], kt, NT_DIMS, preferred_element_type=F32)
                    if mode == "fox":
                        s = jnp.where(keep, s + cq[:, h:h + 1] - ck[h:h + 1, :], MASKED)
                    elif mode == "dil":
                        s = s + ex[0][h, i - j]
                    mh = jnp.maximum(m[h], jnp.max(s, axis=-1, keepdims=True))
                    p = jnp.exp(s - mh)
                    alpha = jnp.exp(m[h] - mh)
                    new_m.append(mh)
                    new_l.append(alpha * l[h] + jnp.sum(p, axis=-1, keepdims=True))
                    alphas.append(alpha)
                    d = jnp.dot(p.astype(BF16), vm[h], preferred_element_type=F32)
                    pv = d if pv is None else pv + d
                acc = acc * _lane_pair(alphas[0], alphas[1], tq) + pv
                return tuple(new_m), tuple(new_l), acc

            init = ((jnp.full((tq, 1), MASKED, F32),) * 2, (jnp.zeros((tq, 1), F32),) * 2,
                    jnp.zeros((tq, LANES), F32))
            m, l, acc = lax.fori_loop(0, i + 1 if causal else nk, k_tile, init)
            o_ref[pl.ds(r0, tq), :] = acc / _lane_pair(l[0], l[1], tq)
            lse_ref[pl.ds(r0, tq), :] = _lane_pair(m[0] + jnp.log(l[0]), m[1] + jnp.log(l[1]), tq)
            return 0

        lax.fori_loop(0, nq, q_tile, 0)

    qs, ks, vs, os_, _ = _att_specs(seq_q, seq_k, qcol, kcol, vcol)
    if mode == "fox":
        cum_col, cum_row = extra
        ex_specs = [pl.BlockSpec((None, None, seq_q, 2), lambda p, b: (b, p, 0, 0)),
                    pl.BlockSpec((None, None, nk, 2, tk), lambda p, b: (b, p, 0, 0, 0))]
    elif mode == "dil":
        (bias,) = extra
        ex_specs = [pl.BlockSpec((None, 2, nq, tq, tk), lambda p, b: (p, 0, 0, 0, 0))]
    else:
        ex_specs = []
    shape = jax.ShapeDtypeStruct((nb * seq_q, 2 * LANES), F32)
    return pl.pallas_call(
        body, name=name, grid=(2, nb), in_specs=[qs, ks, vs] + ex_specs, out_specs=[os_, os_],
        out_shape=[shape, shape], compiler_params=_params(("parallel", "arbitrary")))(qa, ka, va, *extra)


def _softmax_attn_bwd(qa, ka, va, o, lse, do, *, nb, qcol, kcol, vcol, mode, extra=(), name):
    seq_q, seq_k = qa.shape[0] // nb, ka.shape[0] // nb
    tq, tk = min(ATT_TILE, seq_q), min(ATT_TILE, seq_k)
    nq, nk = seq_q // tq, seq_k // tk
    causal = mode != "cross"
    n_ex = {"fox": 2, "dil": 1, "cross": 0}[mode]

    def body(*refs):
        q_ref, k_ref, v_ref, o_ref, lse_ref, do_ref = refs[:6]
        ex = refs[6:6 + n_ex]
        dq_ref, dk_ref, dv_ref = refs[6 + n_ex:9 + n_ex]
        dk_ref[...] = jnp.zeros_like(dk_ref)
        dv_ref[...] = jnp.zeros_like(dv_ref)
        if mode == "fox":
            dcum_ref, dcq_ref = refs[-2:]
            dcum_ref[...] = jnp.zeros_like(dcum_ref)
        if mode == "dil":
            dbias_ref = refs[-1]

            @pl.when(pl.program_id(1) == 0)
            def _():
                dbias_ref[...] = jnp.zeros_like(dbias_ref)

        def q_tile(i, _):
            r0 = pl.multiple_of(i * tq, tq)
            q = (q_ref[pl.ds(r0, tq), :] * SCALE).astype(BF16)
            qm = _split_heads(q)
            do_f = do_ref[pl.ds(r0, tq), :]
            dom = _split_heads(do_f.astype(BF16))
            dd = _split_heads(do_f * o_ref[pl.ds(r0, tq), :])
            delta = [jnp.sum(dd[h], axis=-1, keepdims=True) for h in range(2)]
            lse_t = lse_ref[pl.ds(r0, tq), :]
            lse_h = [lse_t[:, 0:1], lse_t[:, HEAD_DIM:HEAD_DIM + 1]]
            if mode == "fox":
                cq = ex[0][pl.ds(r0, tq), :]
                row = r0 + lax.broadcasted_iota(jnp.int32, (tq, tk), 0)

            def k_tile(j, carry):
                dq, rs = carry
                c0 = pl.multiple_of(j * tk, tk)
                kt = k_ref[pl.ds(c0, tk), :].astype(BF16)
                vt = v_ref[pl.ds(c0, tk), :].astype(BF16)
                km = _split_heads(kt)
                if mode == "fox":
                    ck = ex[1][j]
                    keep = (c0 + lax.broadcasted_iota(jnp.int32, (tq, tk), 1)) <= row
                dk_t, dv_t, new_rs = None, None, []
                for h in range(2):
                    s = lax.dot_general(qm[h], kt, NT_DIMS, preferred_element_type=F32)
                    if mode == "fox":
                        s = jnp.where(keep, s + cq[:, h:h + 1] - ck[h:h + 1, :], MASKED)
                    elif mode == "dil":
                        s = s + ex[0][h, i - j]
                    p = jnp.exp(s - lse_h[h])
                    dp = lax.dot_general(dom[h], vt, NT_DIMS, preferred_element_type=F32)
                    ds = p * (dp - delta[h])
                    dsb = ds.astype(BF16)
                    dq = dq + jnp.dot(dsb, km[h], preferred_element_type=F32)
                    a = lax.dot_general(dsb, qm[h], TN_DIMS, preferred_element_type=F32)
                    b = lax.dot_general(p.astype(BF16), dom[h], TN_DIMS, preferred_element_type=F32)
                    dk_t = a if dk_t is None else dk_t + a
                    dv_t = b if dv_t is None else dv_t + b
                    if mode == "fox":
                        dcum_ref[j, h:h + 1, :] -= jnp.sum(ds, axis=0, keepdims=True)
                        new_rs.append(rs[h] + jnp.sum(ds, axis=-1, keepdims=True))
                    elif mode == "dil":
                        dbias_ref[h, i - j] += ds
                dk_ref[pl.ds(c0, tk), :] += dk_t
                dv_ref[pl.ds(c0, tk), :] += dv_t
                return dq, (tuple(new_rs) if mode == "fox" else rs)

            zero = (jnp.zeros((tq, 1), F32),) * 2
            dq, rs = lax.fori_loop(0, i + 1 if causal else nk, k_tile, (jnp.zeros((tq, LANES), F32), zero))
            dq_ref[pl.ds(r0, tq), :] = dq * SCALE
            if mode == "fox":
                dcq_ref[pl.ds(r0, tq), :] = jnp.where(lax.broadcasted_iota(jnp.int32, (tq, 2), 1) == 0, rs[0], rs[1])
            return 0

        lax.fori_loop(0, nq, q_tile, 0)

    qs, ks, vs, os_, okv = _att_specs(seq_q, seq_k, qcol, kcol, vcol)
    sq = jax.ShapeDtypeStruct((nb * seq_q, 2 * LANES), F32)
    sk = jax.ShapeDtypeStruct((nb * seq_k, 2 * LANES), F32)
    out_specs, out_shape = [os_, okv, okv], [sq, sk, sk]
    if mode == "fox":
        ex_specs = [pl.BlockSpec((None, None, seq_q, 2), lambda p, b: (b, p, 0, 0)),
                    pl.BlockSpec((None, None, nk, 2, tk), lambda p, b: (b, p, 0, 0, 0))]
        out_specs.append(pl.BlockSpec((None, None, nk, 2, tk), lambda p, b: (b, p, 0, 0, 0)))
        out_shape.append(jax.ShapeDtypeStruct((nb, 2, nk, 2, tk), F32))
        out_specs.append(pl.BlockSpec((None, None, seq_q, 2), lambda p, b: (b, p, 0, 0)))
        out_shape.append(jax.ShapeDtypeStruct((nb, 2, seq_q, 2), F32))
    elif mode == "dil":
        ex_specs = [pl.BlockSpec((None, 2, nq, tq, tk), lambda p, b: (p, 0, 0, 0, 0))]
        out_specs.append(pl.BlockSpec((None, 2, nq, tq, tk), lambda p, b: (p, 0, 0, 0, 0)))
        out_shape.append(jax.ShapeDtypeStruct((2, 2, nq, tq, tk), F32))
    else:
        ex_specs = []
    return pl.pallas_call(
        body, name=name, grid=(2, nb), in_specs=[qs, ks, vs, os_, os_, os_] + ex_specs, out_specs=out_specs,
        out_shape=out_shape, compiler_params=_params(("parallel", "arbitrary")))(qa, ka, va, o, lse, do, *extra)


def _log_sigmoid(z):
    return jnp.minimum(z, 0.0) - jnp.log(1.0 + jnp.exp(-jnp.abs(z)))


def _split_bf16(x):
    hi = x.astype(BF16)
    return hi, (x - hi.astype(F32)).astype(BF16)


def _tri(n, fn):
    r = lax.broadcasted_iota(jnp.int32, (n, n), 0)
    c = lax.broadcasted_iota(jnp.int32, (n, n), 1)
    return jnp.where(fn(r, c), 1.0, 0.0).astype(BF16)


def _sb_attn_fwd(proj, *, nb, name):
    seq = proj.shape[0] // nb
    tq = tk = min(ATT_TILE, seq)
    nq = seq // tq
    qcol, kcol, vcol = COL_SB // LANES, COL_SB // LANES + 2, COL_SB // LANES + 4

    def body(q_ref, k_ref, v_ref, o_ref, lt_ref):
        after = _tri(tk, lambda r, c: r > c)

        def q_tile(i, _):
            r0 = pl.multiple_of(i * tq, tq)
            qm = _split_heads((q_ref[pl.ds(r0, tq), :] * SCALE).astype(BF16))
            row = r0 + lax.broadcasted_iota(jnp.int32, (tq, tk), 0)

            def k_tile(jj, carry):
                c, acc = carry
                j = i - jj
                c0 = pl.multiple_of(j * tk, tk)
                kt = k_ref[pl.ds(c0, tk), :].astype(BF16)
                vm = _split_heads(v_ref[pl.ds(c0, tk), :].astype(BF16))
                strict = (c0 + lax.broadcasted_iota(jnp.int32, (tq, tk), 1)) < row
                new_c = []
                for h in range(2):
                    z = lax.dot_general(qm[h], kt, NT_DIMS, preferred_element_type=F32)
                    ls = _log_sigmoid(z)
                    lk = jnp.where(strict, ls - z, 0.0)
                    hi, lo = _split_bf16(lk)
                    sfx = (jnp.dot(hi, after, preferred_element_type=F32)
                           + jnp.dot(lo, after, preferred_element_type=F32))
                    att = jnp.where(strict, jnp.exp(ls + sfx + c[h]), 0.0)
                    acc = acc + jnp.dot(att.astype(BF16), vm[h], preferred_element_type=F32)
                    new_c.append(c[h] + jnp.sum(lk, axis=-1, keepdims=True))
                return tuple(new_c), acc

            init = ((jnp.zeros((tq, 1), F32),) * 2, jnp.zeros((tq, LANES), F32))
            c, acc = lax.fori_loop(0, i + 1, k_tile, init)
            o_ref[pl.ds(r0, tq), :] = acc
            lt_ref[pl.ds(r0, tq), :] = _lane_pair(c[0], c[1], tq)
            return 0

        lax.fori_loop(0, nq, q_tile, 0)

    qs, ks, vs, os_, _ = _att_specs(seq, seq, qcol, kcol, vcol)
    shape = jax.ShapeDtypeStruct((nb * seq, 2 * LANES), F32)
    return pl.pallas_call(
        body, name=name, grid=(2, nb), in_specs=[qs, ks, vs], out_specs=[os_, os_], out_shape=[shape, shape],
        compiler_params=_params(("parallel", "arbitrary")))(proj, proj, proj)


def _sb_attn_bwd(proj, ltot, do, *, nb, name):
    seq = proj.shape[0] // nb
    tq = tk = min(ATT_TILE, seq)
    nq = seq // tq
    qcol, kcol, vcol = COL_SB // LANES, COL_SB // LANES + 2, COL_SB // LANES + 4

    def body(q_ref, k_ref, v_ref, lt_ref, do_ref, dq_ref, dk_ref, dv_ref):
        upto = _tri(tk, lambda r, c: r <= c)
        before = _tri(tk, lambda r, c: r < c)
        dk_ref[...] = jnp.zeros_like(dk_ref)
        dv_ref[...] = jnp.zeros_like(dv_ref)

        def q_tile(i, _):
            r0 = pl.multiple_of(i * tq, tq)
            qm = _split_heads((q_ref[pl.ds(r0, tq), :] * SCALE).astype(BF16))
            dom = _split_heads(do_ref[pl.ds(r0, tq), :].astype(BF16))
            lt_t = lt_ref[pl.ds(r0, tq), :]
            lt_h = [lt_t[:, 0:1], lt_t[:, HEAD_DIM:HEAD_DIM + 1]]
            row = r0 + lax.broadcasted_iota(jnp.int32, (tq, tk), 0)

            def k_tile(j, carry):
                pc, qc, dq = carry
                c0 = pl.multiple_of(j * tk, tk)
                kt = k_ref[pl.ds(c0, tk), :].astype(BF16)
                vt = v_ref[pl.ds(c0, tk), :].astype(BF16)
                km = _split_heads(kt)
                strict = (c0 + lax.broadcasted_iota(jnp.int32, (tq, tk), 1)) < row
                new_pc, new_qc, dk_t, dv_t = [], [], None, None
                for h in range(2):
                    z = lax.dot_general(qm[h], kt, NT_DIMS, preferred_element_type=F32)
                    ls = _log_sigmoid(z)
                    lk = jnp.where(strict, ls - z, 0.0)
                    hi, lo = _split_bf16(lk)
                    pin = (jnp.dot(hi, upto, preferred_element_type=F32)
                           + jnp.dot(lo, upto, preferred_element_type=F32))
                    att = jnp.where(strict, jnp.exp(ls + (lt_h[h] - pc[h] - pin)), 0.0)
                    da = lax.dot_general(dom[h], vt, NT_DIMS, preferred_element_type=F32)
                    dg = att * da
                    qx = qc[h] + jnp.dot(dg.astype(BF16), before, preferred_element_type=F32)
                    sig = jnp.exp(ls)
                    dz = jnp.where(strict, dg * (1.0 - sig) - sig * qx, 0.0)
                    dzb = dz.astype(BF16)
                    dq = dq + jnp.dot(dzb, km[h], preferred_element_type=F32)
                    a = lax.dot_general(dzb, qm[h], TN_DIMS, preferred_element_type=F32)
                    b = lax.dot_general(att.astype(BF16), dom[h], TN_DIMS, preferred_element_type=F32)
                    dk_t = a if dk_t is None else dk_t + a
                    dv_t = b if dv_t is None else dv_t + b
                    new_pc.append(pc[h] + jnp.sum(lk, axis=-1, keepdims=True))
                    new_qc.append(qc[h] + jnp.sum(dg, axis=-1, keepdims=True))
                dk_ref[pl.ds(c0, tk), :] += dk_t
                dv_ref[pl.ds(c0, tk), :] += dv_t
                return tuple(new_pc), tuple(new_qc), dq

            zero = (jnp.zeros((tq, 1), F32),) * 2
            _, _, dq = lax.fori_loop(0, i + 1, k_tile, (zero, zero, jnp.zeros((tq, LANES), F32)))
            dq_ref[pl.ds(r0, tq), :] = dq * SCALE
            return 0

        lax.fori_loop(0, nq, q_tile, 0)

    qs, ks, vs, os_, _ = _att_specs(seq, seq, qcol, kcol, vcol)
    shape = jax.ShapeDtypeStruct((nb * seq, 2 * LANES), F32)
    return pl.pallas_call(
        body, name=name, grid=(2, nb), in_specs=[qs, ks, vs, os_, os_], out_specs=[os_, os_, os_],
        out_shape=[shape] * 3, compiler_params=_params(("parallel", "arbitrary")))(proj, proj, proj, ltot, do)


def _lane_scan(x, reverse=False):
    n = x.shape[-1]
    lane = lax.broadcasted_iota(jnp.int32, x.shape, 1)
    k = 1
    while k < n:
        if reverse:
            x = x + jnp.where(lane < n - k, pltpu.roll(x, n - k, 1), 0.0)
        else:
            x = x + jnp.where(lane >= k, pltpu.roll(x, k, 1), 0.0)
        k *= 2
    return x


def _fox_gate_fwd(f_rows, b_rows):
    def body(f_ref, b_ref, o_ref):
        o_ref[...] = _lane_scan(_log_sigmoid(f_ref[...] + b_ref[...]))

    return pl.pallas_call(body, name="fox_gate_fwd", out_shape=jax.ShapeDtypeStruct(f_rows.shape, F32))(f_rows, b_rows)


def _fox_gate_bwd(dcum, f_rows, b_rows):
    def body(d_ref, f_ref, b_ref, df_ref, db_ref):
        z = f_ref[...] + b_ref[...]
        df = _lane_scan(d_ref[...], reverse=True) * jnp.exp(_log_sigmoid(-z))
        df_ref[...] = df
        rs = jnp.sum(df, axis=-1, keepdims=True)
        tot = rs
        for e in range(1, f_rows.shape[0] // N_HEADS):
            tot = tot + pltpu.roll(rs, e * N_HEADS, 0)
        db_ref[...] = tot

    return pl.pallas_call(
        body, name="fox_gate_bwd",
        out_shape=[jax.ShapeDtypeStruct(f_rows.shape, F32), jax.ShapeDtypeStruct((f_rows.shape[0], 1), F32)],
    )(dcum, f_rows, b_rows)


def _dil_tables(seq):
    t = min(ATT_TILE, seq)
    n = seq // t
    a = np.arange(t)
    d = (np.arange(n)[:, None, None] * t + a[None, :, None] - a[None, None, :]).astype(np.int64)
    count = np.zeros(d.shape, np.int64)
    for window, dil in DILATED_PATTERNS:
        count += (d >= 0) & (d % dil == 0) & (d // dil <= window // dil)
    nn = np.maximum(d, 0)
    max_exact = NUM_BUCKETS // 2
    nf = np.maximum(nn, 1).astype(np.float32)
    large = max_exact + (np.log(nf / np.float32(max_exact)) / np.float32(math.log(MAX_DISTANCE / max_exact))
                         * np.float32(NUM_BUCKETS - max_exact)).astype(np.int32)
    bucket = np.where(nn < max_exact, nn, np.minimum(large, NUM_BUCKETS - 1))
    bucket = np.where(count > 0, bucket, -1).astype(np.int32)
    logc = np.where(count > 0, np.log(np.maximum(count, 1)), MASKED).astype(np.float32)
    return bucket, logc


def _dil_bias(rel_bias, seq):
    bucket, logc = _dil_tables(seq)
    n, t, _ = bucket.shape

    def body(rb_ref, bk_ref, lc_ref, o_ref):
        h = pl.program_id(0) * 2 + pl.program_id(1)
        bk = bk_ref[...]
        out = lc_ref[...]
        for b in range(NUM_BUCKETS):
            out = jnp.where(bk == b, out + rb_ref[b, h], out)
        o_ref[...] = out

    full = pl.BlockSpec((n, t, t), lambda p, h: (0, 0, 0))
    return pl.pallas_call(
        body, name="dil_bias", grid=(2, 2),
        in_specs=[pl.BlockSpec(memory_space=pltpu.SMEM), full, full],
        out_specs=pl.BlockSpec((None, None, n, t, t), lambda p, h: (p, h, 0, 0, 0)),
        out_shape=jax.ShapeDtypeStruct((2, 2, n, t, t), F32),
        compiler_params=_params(("parallel", "parallel")))(rel_bias, jnp.asarray(bucket), jnp.asarray(logc))


def _dil_bias_bwd(dbias, seq):
    bucket, _ = _dil_tables(seq)
    n, t, _ = bucket.shape

    def body(d_ref, bk_ref, o_ref):
        bk = bk_ref[...]
        lane = lax.broadcasted_iota(jnp.int32, (1, LANES), 1)
        for b in range(NUM_BUCKETS):
            rowv = jnp.zeros((1, LANES), F32)
            for h in range(N_HEADS):
                s = jnp.sum(jnp.where(bk == b, d_ref[h // 2, h % 2], 0.0))
                rowv = jnp.where(lane == h, s, rowv)
            o_ref[b:b + 1, :] = rowv

    out = pl.pallas_call(body, name="dil_bias_bwd", out_shape=jax.ShapeDtypeStruct((NUM_BUCKETS, LANES), F32),
                         compiler_params=pltpu.CompilerParams(vmem_limit_bytes=VMEM_LIMIT))(dbias, jnp.asarray(bucket))
    return out[:, :N_HEADS]


def _shift_rows(x, k, row, fill=0.0):
    n = x.shape[0]
    if k > 0:
        return jnp.where(row >= k, pltpu.roll(x, k, 0), fill)
    return jnp.where(row < n + k, pltpu.roll(x, n + k, 0), fill)


def _row_scan(a, u, row, reverse=False):
    n = a.shape[0]
    k = 1
    while k < n:
        s = -k if reverse else k
        u = a * _shift_rows(u, s, row) + u
        a = a * _shift_rows(a, s, row, 1.0)
        k *= 2
    return u


def _sigmoid(x):
    return 1.0 / (1.0 + jnp.exp(-x))


def _gelu(g):
    return 0.5 * g * (1.0 + lax.erf(g * (2.0 ** -0.5)))


def _gelu_grad(g):
    return 0.5 * (1.0 + lax.erf(g * (2.0 ** -0.5))) + g * jnp.exp(-0.5 * g * g) * (1.0 / math.sqrt(2.0 * math.pi))


def _neg_expm1(x):
    small = -x * (1.0 + x * (0.5 + x * (1.0 / 6.0 + x * (1.0 / 24.0))))
    return jnp.where(x > -0.03, small, 1.0 - jnp.exp(x))


def _lru_core(x, vec, wa, wx, row):
    xs = [_shift_rows(x, 3 - j, row) if j < 3 else x for j in range(4)]
    xc = vec[4:5, :]
    for j in range(4):
        xc = xc + vec[j:j + 1, :] * xs[j]
    xcb = xc.astype(BF16)
    r = _sigmoid(jnp.dot(xcb, wa, preferred_element_type=F32) + vec[5:6, :])
    ig = _sigmoid(jnp.dot(xcb, wx, preferred_element_type=F32) + vec[6:7, :])
    lam = vec[7:8, :]
    sp = jnp.maximum(-lam, 0.0) - _log_sigmoid(jnp.abs(lam))
    la = -LRU_C * r * sp
    a = jnp.exp(la)
    mult = jnp.sqrt(_neg_expm1(2.0 * la))
    return xs, xc, xcb, r, ig, sp, la, a, mult


def _lru_specs(seq):
    tok = pl.BlockSpec((seq, LANES), lambda hf, b: (b, hf))
    vec = pl.BlockSpec((SUBLANES, LANES), lambda hf, b: (0, hf))
    mat = pl.BlockSpec((None, LANES, LANES), lambda hf, b: (hf, 0, 0))
    return tok, vec, mat


def _lru_fwd(proj, vec, wa, wx, *, nb, name):
    seq = proj.shape[0] // nb

    def body(x_ref, g_ref, vec_ref, wa_ref, wx_ref, o_ref):
        row = lax.broadcasted_iota(jnp.int32, (seq, LANES), 0)
        _, xc, _, _, ig, _, _, a, mult = _lru_core(x_ref[...], vec_ref[...], wa_ref[...], wx_ref[...], row)
        h = _row_scan(a, mult * (ig * xc), row)
        o_ref[...] = h * _gelu(g_ref[...])

    tok, vecs, mat = _lru_specs(seq)
    xs = pl.BlockSpec((seq, LANES), lambda hf, b: (b, COL_LRUX // LANES + hf))
    gs = pl.BlockSpec((seq, LANES), lambda hf, b: (b, COL_LRUG // LANES + hf))
    return pl.pallas_call(
        body, name=name, grid=(2, nb), in_specs=[xs, gs, vecs, mat, mat], out_specs=tok,
        out_shape=jax.ShapeDtypeStruct((proj.shape[0], 2 * LANES), F32),
        compiler_params=_params(("parallel", "arbitrary")))(proj, proj, vec, wa, wx)


def _lru_bwd(proj, vec, wa, wx, dout, *, nb, name):
    seq = proj.shape[0] // nb

    def body(x_ref, g_ref, vec_ref, wa_ref, wx_ref, do_ref, dx_ref, dg_ref, dvec_ref, dwa_ref, dwx_ref):
        row = lax.broadcasted_iota(jnp.int32, (seq, LANES), 0)
        vec_, wa_, wx_ = vec_ref[...], wa_ref[...], wx_ref[...]
        xs, xc, xcb, r, ig, sp, la, a, mult = _lru_core(x_ref[...], vec_, wa_, wx_, row)
        h = _row_scan(a, mult * (ig * xc), row)
        gate, do = g_ref[...], do_ref[...]
        dg_ref[...] = do * h * _gelu_grad(gate)
        dh = do * _gelu(gate)
        gacc = _row_scan(_shift_rows(a, -1, row), dh, row, reverse=True)
        da = gacc * _shift_rows(h, 1, row)
        dmult = gacc * (ig * xc)
        dig = gacc * (mult * xc)
        dxc = gacc * (mult * ig)
        dla = da * a - dmult * (a * a) / mult
        dr = (-LRU_C) * sp * dla
        dsp = jnp.sum((-LRU_C) * r * dla, axis=0, keepdims=True)
        dpr = dr * r * (1.0 - r)
        dpi = dig * ig * (1.0 - ig)
        dprb, dpib = dpr.astype(BF16), dpi.astype(BF16)
        dxc = (dxc + lax.dot_general(dprb, wa_, NT_DIMS, preferred_element_type=F32)
               + lax.dot_general(dpib, wx_, NT_DIMS, preferred_element_type=F32))
        dx = vec_[3:4, :] * dxc
        for j in range(3):
            dx = dx + vec_[j:j + 1, :] * _shift_rows(dxc, -(3 - j), row)
        dx_ref[...] = dx

        @pl.when(pl.program_id(1) == 0)
        def _():
            dvec_ref[...] = jnp.zeros_like(dvec_ref)
            dwa_ref[...] = jnp.zeros_like(dwa_ref)
            dwx_ref[...] = jnp.zeros_like(dwx_ref)

        for j in range(4):
            dvec_ref[j:j + 1, :] += jnp.sum(dxc * xs[j], axis=0, keepdims=True)
        dvec_ref[4:5, :] += jnp.sum(dxc, axis=0, keepdims=True)
        dvec_ref[5:6, :] += jnp.sum(dpr, axis=0, keepdims=True)
        dvec_ref[6:7, :] += jnp.sum(dpi, axis=0, keepdims=True)
        lam = vec_[7:8, :]
        dvec_ref[7:8, :] += -dsp * _sigmoid(-lam)
        dwa_ref[...] += lax.dot_general(xcb, dprb, TN_DIMS, preferred_element_type=F32)
        dwx_ref[...] += lax.dot_general(xcb, dpib, TN_DIMS, preferred_element_type=F32)

    tok, vecs, mat = _lru_specs(seq)
    xs = pl.BlockSpec((seq, LANES), lambda hf, b: (b, COL_LRUX // LANES + hf))
    gs = pl.BlockSpec((seq, LANES), lambda hf, b: (b, COL_LRUG // LANES + hf))
    tshape = jax.ShapeDtypeStruct((proj.shape[0], 2 * LANES), F32)
    return pl.pallas_call(
        body, name=name, grid=(2, nb), in_specs=[xs, gs, vecs, mat, mat, tok], out_specs=[tok, tok, vecs, mat, mat],
        out_shape=[tshape, tshape, jax.ShapeDtypeStruct((SUBLANES, 2 * LANES), F32),
                   jax.ShapeDtypeStruct((2, LANES, LANES), F32), jax.ShapeDtypeStruct((2, LANES, LANES), F32)],
        compiler_params=_params(("parallel", "arbitrary")))(proj, proj, vec, wa, wx, dout)


FFN_ROWS = 512
FFN_COLS = 256


def _with_halo(halo, x, k):
    xx = jnp.concatenate([halo, x], axis=0)
    return pltpu.roll(xx, k, 0)[SUBLANES:, :]


def _ffn_conv(x_ref, halo_ref, cw, pos):
    x, halo = x_ref[...], halo_ref[...]
    x1 = jnp.where(pos >= 1, _with_halo(halo, x, 1), 0.0)
    x2 = jnp.where(pos >= 2, _with_halo(halo, x, 2), 0.0)
    return cw[3:4, :] + cw[0:1, :] * x2 + cw[1:2, :] * x1 + cw[2:3, :] * x, x1, x2


def _ffn_specs(tm, tn, ncol, gate_off):
    prev = lambda i: jnp.maximum(i * (tm // SUBLANES) - 1, 0)
    up = pl.BlockSpec((tm, tn), lambda j, i: (i, j))
    gate = pl.BlockSpec((tm, tn), lambda j, i: (i, j + gate_off))
    up_h = pl.BlockSpec((SUBLANES, tn), lambda j, i: (prev(i), j))
    gate_h = pl.BlockSpec((SUBLANES, tn), lambda j, i: (prev(i), j + gate_off))
    cw_up = pl.BlockSpec((SUBLANES, tn), lambda j, i: (0, j))
    cw_gate = pl.BlockSpec((SUBLANES, tn), lambda j, i: (0, j + gate_off))
    return up, gate, up_h, gate_h, cw_up, cw_gate


def _ffn_act(hf, cw, *, seq, name):
    t, w2 = hf.shape
    w = w2 // 2
    tm, tn = _tile(seq, FFN_ROWS, SUBLANES), _tile(w, FFN_COLS)

    def body(u_ref, g_ref, uh_ref, gh_ref, cu_ref, cg_ref, o_ref):
        pos = (pl.program_id(1) * tm + lax.broadcasted_iota(jnp.int32, (tm, 1), 0)) % seq
        up, _, _ = _ffn_conv(u_ref, uh_ref, cu_ref[...], pos)
        gate, _, _ = _ffn_conv(g_ref, gh_ref, cg_ref[...], pos)
        o_ref[...] = _gelu(gate) * up

    specs = _ffn_specs(tm, tn, w // tn, w // tn)
    return pl.pallas_call(
        body, name=name, grid=(w // tn, t // tm), in_specs=list(specs), out_specs=specs[0],
        out_shape=jax.ShapeDtypeStruct((t, w), F32),
        compiler_params=_params(("parallel", "parallel")))(hf, hf, hf, hf, cw, cw)


def _ffn_act_bwd(hf, cw, dact, *, seq, name):
    t, w2 = hf.shape
    w = w2 // 2
    tm, tn = _tile(seq, FFN_ROWS, SUBLANES), _tile(w, FFN_COLS)

    def body(u_ref, g_ref, uh_ref, gh_ref, cu_ref, cg_ref, da_ref, du_ref, dg_ref, dcu_ref, dcg_ref):
        pos = (pl.program_id(1) * tm + lax.broadcasted_iota(jnp.int32, (tm, 1), 0)) % seq
        up, u1, u2 = _ffn_conv(u_ref, uh_ref, cu_ref[...], pos)
        gate, g1, g2 = _ffn_conv(g_ref, gh_ref, cg_ref[...], pos)
        da = da_ref[...]
        d_up = da * _gelu(gate)
        d_gate = da * up * _gelu_grad(gate)
        du_ref[...] = d_up
        dg_ref[...] = d_gate

        @pl.when(pl.program_id(1) == 0)
        def _():
            dcu_ref[...] = jnp.zeros_like(dcu_ref)
            dcg_ref[...] = jnp.zeros_like(dcg_ref)

        for ref, d, taps in ((dcu_ref, d_up, (u2, u1, u_ref[...])), (dcg_ref, d_gate, (g2, g1, g_ref[...]))):
            for j in range(3):
                ref[j:j + 1, :] += jnp.sum(d * taps[j], axis=0, keepdims=True)
            ref[3:4, :] += jnp.sum(d, axis=0, keepdims=True)

    specs = _ffn_specs(tm, tn, w // tn, w // tn)
    tile, cwt = specs[0], specs[4]
    dhfc_up = pl.BlockSpec((tm, tn), lambda j, i: (i, j))
    outs = pl.pallas_call(
        body, name=name, grid=(w // tn, t // tm), in_specs=list(specs) + [tile],
        out_specs=[dhfc_up, dhfc_up, cwt, cwt],
        out_shape=[jax.ShapeDtypeStruct((t, w), F32), jax.ShapeDtypeStruct((t, w), F32),
                   jax.ShapeDtypeStruct((SUBLANES, w), F32), jax.ShapeDtypeStruct((SUBLANES, w), F32)],
        compiler_params=_params(("parallel", "arbitrary")))(hf, hf, hf, hf, cw, cw, dact)
    return outs


def _conv_transpose(d, cw, *, seq, name):
    t, w = d.shape
    tm, tn = _tile(seq, FFN_ROWS, SUBLANES), _tile(w, FFN_COLS)
    last = t // SUBLANES - 1

    def body(d_ref, nx_ref, cw_ref, o_ref):
        pos = (pl.program_id(1) * tm + lax.broadcasted_iota(jnp.int32, (tm, 1), 0)) % seq
        x, cw_ = d_ref[...], cw_ref[...]
        xx = jnp.concatenate([x, nx_ref[...]], axis=0)
        n = tm + SUBLANES
        x1 = jnp.where(pos < seq - 1, pltpu.roll(xx, n - 1, 0)[:tm, :], 0.0)
        x2 = jnp.where(pos < seq - 2, pltpu.roll(xx, n - 2, 0)[:tm, :], 0.0)
        o_ref[...] = cw_[2:3, :] * x + cw_[1:2, :] * x1 + cw_[0:1, :] * x2

    tile = pl.BlockSpec((tm, tn), lambda j, i: (i, j))
    nxt = pl.BlockSpec((SUBLANES, tn), lambda j, i: (jnp.minimum((i + 1) * (tm // SUBLANES), last), j))
    cws = pl.BlockSpec((SUBLANES, tn), lambda j, i: (0, j))
    return pl.pallas_call(
        body, name=name, grid=(w // tn, t // tm), in_specs=[tile, nxt, cws], out_specs=tile,
        out_shape=jax.ShapeDtypeStruct((t, w), F32), compiler_params=_params(("parallel", "parallel")))(d, d, cw)


def _adamw(w, g, m, v, *, name, rows=256):
    r, c = w.shape
    tr = _tile(r, rows, SUBLANES)

    def body(w_ref, g_ref, m_ref, v_ref, d_ref, nm_ref, nv_ref):
        g_ = g_ref[...]
        nm = ADAM_B1 * m_ref[...] + (1.0 - ADAM_B1) * g_
        nv = ADAM_B2 * v_ref[...] + (1.0 - ADAM_B2) * (g_ * g_)
        m_hat = nm / (1.0 - ADAM_B1 ** ADAM_STEP)
        v_hat = nv / (1.0 - ADAM_B2 ** ADAM_STEP)
        d_ref[...] = -ADAM_LR * (m_hat / (jnp.sqrt(v_hat) + ADAM_EPS) + ADAM_WD * w_ref[...])
        nm_ref[...] = nm
        nv_ref[...] = nv

    spec = pl.BlockSpec((tr, c), lambda i: (i, 0))
    shape = jax.ShapeDtypeStruct((r, c), F32)
    return pl.pallas_call(body, name=name, grid=(r // tr,), in_specs=[spec] * 4, out_specs=[spec] * 3,
                          out_shape=[shape] * 3, compiler_params=_params(("parallel",)))(w, g, m, v)


MESH = pl.DeviceIdType.MESH
ANY = pl.BlockSpec(memory_space=pl.ANY)


def _mesh_pos():
    return lax.axis_index("x"), lax.axis_index("y"), lax.axis_index("c")


def _sibling_pair(buf, *, name, other_half_rows=None):
    def body(src_ref, out_ref, send_sem, recv_sem, local_sem):
        x, y, c = _mesh_pos()
        if other_half_rows is None:
            local = pltpu.make_async_copy(src_ref, out_ref.at[c], local_sem)
            local.start()
            cp = pltpu.make_async_remote_copy(src_ref, out_ref.at[c], send_sem, recv_sem,
                                              device_id=(x, y, 1 - c), device_id_type=MESH)
            cp.start()
            cp.wait()
            local.wait()
        else:
            r = other_half_rows
            cp = pltpu.make_async_remote_copy(src_ref.at[:, pl.ds((1 - c) * r, r), :], out_ref, send_sem, recv_sem,
                                              device_id=(x, y, 1 - c), device_id_type=MESH)
            cp.start()
            cp.wait()

    if other_half_rows is None:
        shape = (2,) + buf.shape
    else:
        shape = (buf.shape[0], other_half_rows, buf.shape[2])
    return pl.pallas_call(
        body, name=name, in_specs=[ANY], out_specs=ANY, out_shape=jax.ShapeDtypeStruct(shape, buf.dtype),
        scratch_shapes=[pltpu.SemaphoreType.DMA, pltpu.SemaphoreType.DMA, pltpu.SemaphoreType.DMA])(buf)


def _chip_exchange(buf, *, name, bcast, own_half_rows=None):
    def body(src_ref, out_ref, send_sems, recv_sems, local_sem):
        x, y, c = _mesh_pos()
        me = 2 * x + y

        def src_for(k):
            if not bcast:
                return src_ref.at[k]
            if own_half_rows is None:
                return src_ref
            return src_ref.at[pl.ds(c * own_half_rows, own_half_rows), :]

        local = pltpu.make_async_copy(src_for(me), out_ref.at[me], local_sem)
        local.start()
        peers = [(1 - x, y), (x, 1 - y), (1 - x, 1 - y)]
        sends = []
        for r, (px, py) in enumerate(peers):
            cp = pltpu.make_async_remote_copy(src_for(2 * px + py), out_ref.at[me], send_sems.at[r], recv_sems.at[r],
                                              device_id=(px, py, c), device_id_type=MESH)
            cp.start()
            sends.append(cp)
        for r, (px, py) in enumerate(peers):
            k = 2 * px + py
            pltpu.make_async_remote_copy(src_for(k), out_ref.at[k], send_sems.at[r], recv_sems.at[r],
                                         device_id=(px, py, c), device_id_type=MESH).wait_recv()
        for cp in sends:
            cp.wait_send()
        local.wait()

    if not bcast:
        shape = buf.shape
    elif own_half_rows is None:
        shape = (4,) + buf.shape
    else:
        shape = (4, own_half_rows) + buf.shape[1:]
    return pl.pallas_call(
        body, name=name, in_specs=[ANY], out_specs=ANY, out_shape=jax.ShapeDtypeStruct(shape, buf.dtype),
        scratch_shapes=[pltpu.SemaphoreType.DMA((3,)), pltpu.SemaphoreType.DMA((3,)), pltpu.SemaphoreType.DMA])(buf)


def _add_own_half(full, recv, c, *, name, rows=384):
    k4, r, n = recv.shape
    tr = _tile(r, rows, 16)
    nblk = r // tr

    def body(c_ref, a_ref, b_ref, o_ref):
        o_ref[...] = (a_ref[...] + b_ref[...]).astype(BF16)

    grid_spec = pltpu.PrefetchScalarGridSpec(
        num_scalar_prefetch=1, grid=(k4, nblk),
        in_specs=[pl.BlockSpec((None, tr, n), lambda k, i, c_ref: (k, c_ref[0] * nblk + i, 0)),
                  pl.BlockSpec((None, tr, n), lambda k, i, c_ref: (k, i, 0))],
        out_specs=pl.BlockSpec((None, tr, n), lambda k, i, c_ref: (k, i, 0)))
    return pl.pallas_call(body, name=name, grid_spec=grid_spec, out_shape=jax.ShapeDtypeStruct(recv.shape, BF16),
                          compiler_params=_params(("parallel", "parallel")))(c.reshape(1), full, recv)


def _sum_slots(buf, *, name, rows=384):
    r, n = buf.shape[-2:]
    lead = buf.shape[:-2]
    k = int(np.prod(lead))
    tr = _tile(r, rows, 16)

    def body(b_ref, o_ref):
        acc = b_ref[0].astype(F32)
        for s in range(1, k):
            acc = acc + b_ref[s].astype(F32)
        o_ref[...] = acc

    return pl.pallas_call(
        body, name=name, grid=(r // tr,), in_specs=[pl.BlockSpec((k, tr, n), lambda i: (0, i, 0))],
        out_specs=pl.BlockSpec((tr, n), lambda i: (i, 0)), out_shape=jax.ShapeDtypeStruct((r, n), F32),
        compiler_params=_params(("parallel",)))(buf.reshape((k, r, n)))


ROW = 1024
SHARDED = (("w_in", 2), ("w_out", 1), ("w_cq", 1), ("w_ck", 1), ("w_cv", 1), ("w_co", 2), ("w_up", 2),
           ("w_down", 1), ("lru_conv_w", 2), ("ffn_conv_w", 2))
CONV_SHARDED = ("lru_conv_w", "ffn_conv_w")
REPLICATED = ("norm_mix_g", "b_forget", "lru_conv_b", "lru_w_a", "lru_b_a", "lru_w_x", "lru_b_x", "lru_lambda",
              "norm_cross_g", "norm_mem_g", "norm_ffn_g", "ffn_conv_b", "rel_bias", "final_norm_g")
WEIGHTS = ('norm_mix_g', 'w_in', 'b_forget', 'lru_conv_w', 'lru_conv_b', 'lru_w_a', 'lru_b_a', 'lru_w_x', 'lru_b_x',
           'lru_lambda', 'w_out', 'norm_cross_g', 'norm_mem_g', 'w_cq', 'w_ck', 'w_cv', 'w_co', 'norm_ffn_g', 'w_up',
           'ffn_conv_w', 'ffn_conv_b', 'w_down', 'rel_bias', 'final_norm_g')
SHARD_ROW_UNIT = 256
SMALL_ROW_UNIT = 8


def _round_up(n, m):
    return -(-n // m) * m


class _Packing:
    def __init__(self, entries, row_unit):
        self.entries, self.off = entries, {}
        o = 0
        for name, shape in entries:
            self.off[name] = o
            o += _round_up(int(np.prod(shape)), ROW)
        self.used = o
        self.rows = _round_up(o // ROW, row_unit)

    def pack(self, arrays, lead=()):
        parts = []
        for name, shape in self.entries:
            n = int(np.prod(shape))
            a = arrays[name].reshape(lead + (n,))
            parts.append(jnp.pad(a, [(0, 0)] * len(lead) + [(0, _round_up(n, ROW) - n)]))
        tail = self.rows * ROW - self.used
        if tail:
            parts.append(jnp.zeros(lead + (tail,), parts[0].dtype))
        return jnp.concatenate(parts, axis=-1).reshape(lead + (self.rows, ROW))

    def unpack(self, flat, lead=()):
        flat = flat.reshape(lead + (self.rows * ROW,))
        out = {}
        for name, shape in self.entries:
            n = int(np.prod(shape))
            out[name] = lax.slice_in_dim(flat, self.off[name], self.off[name] + n, axis=len(lead)).reshape(lead + tuple(shape))
        return out


def _to_shards(g, axis):
    shp = g.shape
    g = g.reshape(shp[:axis] + (4, shp[axis] // 4) + shp[axis + 1:])
    return jnp.moveaxis(g, axis, 0)


def _from_shards(s, axis):
    s = jnp.moveaxis(s, 0, axis)
    shp = s.shape
    return s.reshape(shp[:axis] + (4 * shp[axis + 1],) + shp[axis + 2:])


def _pad_w_in(w):
    return jnp.concatenate([w[:, :1536], w[:, 1540:2820], w[:, 1536:1540],
                            jnp.zeros((w.shape[0], PROJ_W - N_IN), w.dtype)], axis=1)


def _unpad_w_in(wp):
    return jnp.concatenate([wp[:, :1536], wp[:, COL_F:COL_F + 4], wp[:, 1536:COL_F]], axis=1)


def _block_diag(w):
    z = jnp.zeros((HEAD_DIM, HEAD_DIM), w.dtype)
    half = lambda a, b: jnp.concatenate([jnp.concatenate([a, z], 1), jnp.concatenate([z, b], 1)], 0)
    return jnp.stack([half(w[0], w[1]), half(w[2], w[3])])


def _block_diag_grad(d):
    return jnp.stack([d[0, :HEAD_DIM, :HEAD_DIM], d[0, HEAD_DIM:, HEAD_DIM:],
                      d[1, :HEAD_DIM, :HEAD_DIM], d[1, HEAD_DIM:, HEAD_DIM:]])


def _fox_layouts(cum, nb, seq):
    tk = min(ATT_TILE, seq)
    col = cum.reshape(nb, 2, 2, seq).transpose(0, 1, 3, 2)
    row = cum.reshape(nb, 2, 2, seq // tk, tk).transpose(0, 1, 3, 2, 4)
    return col, row


def _layer_params(w, l, nb):
    lru_vec = jnp.concatenate([w["lru_conv_w"][l], w["lru_conv_b"][l][None], w["lru_b_a"][l][None],
                               w["lru_b_x"][l][None], w["lru_lambda"][l][None]], axis=0)
    ffn_cw = jnp.concatenate([w["ffn_conv_w"][l], w["ffn_conv_b"][l][None],
                              jnp.zeros((SUBLANES - 4, 2 * D_FF), F32)], axis=0)
    return dict(
        w_in=_pad_w_in(w["w_in"][l]), lru_vec=lru_vec,
        wa=_block_diag(w["lru_w_a"][l]).astype(BF16), wx=_block_diag(w["lru_w_x"][l]).astype(BF16),
        ffn_cw=ffn_cw, b_rows=jnp.tile(w["b_forget"][l], nb).reshape(nb * N_HEADS, 1))


def _layer_fwd(x, mem, w, lp, l, bias, nb):
    t = x.shape[0]
    seq = t // nb
    tag = f"l{l}"
    sv = dict(x0=x)
    h = _rmsnorm(x, w["norm_mix_g"][l], name=tag + "_norm_mix")
    proj = _mm(h, lp["w_in"], name=tag + "_proj")
    o_sb, ltot = _sb_attn_fwd(proj, nb=nb, name=tag + "_sb_fwd")
    f_rows = proj[:, COL_F:COL_F + N_HEADS].reshape(nb, seq, N_HEADS).transpose(0, 2, 1).reshape(nb * N_HEADS, seq)
    cum_col, cum_row = _fox_layouts(_fox_gate_fwd(f_rows, lp["b_rows"]), nb, seq)
    cf, cd = COL_FOX // LANES, COL_DIL // LANES
    o_fox, lse_fox = _softmax_attn_fwd(proj, proj, proj, nb=nb, qcol=cf, kcol=cf + 2, vcol=cf + 4, mode="fox",
                                       extra=(cum_col, cum_row), name=tag + "_fox_fwd")
    o_dil, lse_dil = _softmax_attn_fwd(proj, proj, proj, nb=nb, qcol=cd, kcol=cd + 2, vcol=cd + 4, mode="dil",
                                       extra=(bias,), name=tag + "_dil_fwd")
    o_lru = _lru_fwd(proj, lp["lru_vec"], lp["wa"], lp["wx"], nb=nb, name=tag + "_lru_fwd")
    mixed = jnp.concatenate([o_sb, o_fox, o_dil, o_lru], axis=1)
    x1 = _mm(mixed, w["w_out"][l], res=x, name=tag + "_out")
    hq = _rmsnorm(x1, w["norm_cross_g"][l], name=tag + "_norm_cross")
    memn = _rmsnorm(mem, w["norm_mem_g"][l], name=tag + "_norm_mem")
    q = _mm(hq, w["w_cq"][l], name=tag + "_cq")
    k = _mm(memn, w["w_ck"][l], name=tag + "_ck")
    v = _mm(memn, w["w_cv"][l], name=tag + "_cv")
    oc, lse_c = _softmax_attn_fwd(q, k, v, nb=nb, qcol=0, kcol=0, vcol=0, mode="cross", name=tag + "_cross_fwd")
    x2 = _mm(oc, w["w_co"][l], res=x1, name=tag + "_co")
    hn = _rmsnorm(x2, w["norm_ffn_g"][l], name=tag + "_norm_ffn")
    hf = _mm(hn, w["w_up"][l], name=tag + "_up")
    act = _ffn_act(hf, lp["ffn_cw"], seq=seq, name=tag + "_ffn_act")
    x3 = _mm(act, w["w_down"][l], res=x2, name=tag + "_down")
    sv.update(h=h, proj=proj, ltot=ltot, f_rows=f_rows, cum_col=cum_col, cum_row=cum_row, o_fox=o_fox,
              lse_fox=lse_fox, o_dil=o_dil, lse_dil=lse_dil, mixed=mixed, x1=x1, hq=hq, memn=memn, q=q, k=k, v=v,
              oc=oc, lse_c=lse_c, x2=x2, hn=hn, hf=hf, act=act)
    return x3, sv


def _layer_bwd(dx3, mem, sv, w, lp, l, bias, nb):
    t = dx3.shape[0]
    seq = t // nb
    tag = f"l{l}"
    g = {}
    g["w_down"] = _mm(sv["act"], dx3, ta=True, name=tag + "_dw_down")
    dact = _mm(dx3, w["w_down"][l], tb=True, name=tag + "_dact")
    d_up, d_gate, dcu, dcg = _ffn_act_bwd(sv["hf"], lp["ffn_cw"], dact, seq=seq, name=tag + "_ffn_act_bwd")
    dcw = jnp.concatenate([dcu, dcg], axis=1)
    g["ffn_conv_w"], g["ffn_conv_b"] = dcw[:3], dcw[3]
    dhf = _conv_transpose(jnp.concatenate([d_up, d_gate], axis=1), lp["ffn_cw"], seq=seq, name=tag + "_conv_t")
    g["w_up"] = _mm(sv["hn"], dhf, ta=True, name=tag + "_dw_up")
    dhn = _mm(dhf, w["w_up"][l], tb=True, name=tag + "_dhn")
    dx2, g["norm_ffn_g"] = _rmsnorm_bwd(dhn, sv["x2"], w["norm_ffn_g"][l], dx3, name=tag + "_norm_ffn_bwd")
    g["w_co"] = _mm(sv["oc"], dx2, ta=True, name=tag + "_dw_co")
    doc = _mm(dx2, w["w_co"][l], tb=True, name=tag + "_doc")
    dq, dk, dv = _softmax_attn_bwd(sv["q"], sv["k"], sv["v"], sv["oc"], sv["lse_c"], doc, nb=nb, qcol=0, kcol=0,
                                   vcol=0, mode="cross", name=tag + "_cross_bwd")
    g["w_cq"] = _mm(sv["hq"], dq, ta=True, name=tag + "_dw_cq")
    g["w_ck"] = _mm(sv["memn"], dk, ta=True, name=tag + "_dw_ck")
    g["w_cv"] = _mm(sv["memn"], dv, ta=True, name=tag + "_dw_cv")
    dhq = _mm(dq, w["w_cq"][l], tb=True, name=tag + "_dhq")
    dmemn = _mm(dv, w["w_cv"][l], tb=True, res=_mm(dk, w["w_ck"][l], tb=True, name=tag + "_dmem_k"),
                name=tag + "_dmem_v")
    _, g["norm_mem_g"] = _rmsnorm_bwd(dmemn, mem, w["norm_mem_g"][l], None, name=tag + "_norm_mem_bwd")
    dx1, g["norm_cross_g"] = _rmsnorm_bwd(dhq, sv["x1"], w["norm_cross_g"][l], dx2, name=tag + "_norm_cross_bwd")
    g["w_out"] = _mm(sv["mixed"], dx1, ta=True, name=tag + "_dw_out")
    dmixed = _mm(dx1, w["w_out"][l], tb=True, name=tag + "_dmixed")
    do = [dmixed[:, i * GROUP_W:(i + 1) * GROUP_W] for i in range(4)]
    proj = sv["proj"]
    d_sb = _sb_attn_bwd(proj, sv["ltot"], do[0], nb=nb, name=tag + "_sb_bwd")
    cf, cd = COL_FOX // LANES, COL_DIL // LANES
    *d_fox, dcum_k, dcum_q = _softmax_attn_bwd(
        proj, proj, proj, sv["o_fox"], sv["lse_fox"], do[1], nb=nb, qcol=cf, kcol=cf + 2, vcol=cf + 4, mode="fox",
        extra=(sv["cum_col"], sv["cum_row"]), name=tag + "_fox_bwd")
    dcum = (dcum_k.transpose(0, 1, 3, 2, 4).reshape(nb * N_HEADS, seq)
            + dcum_q.transpose(0, 1, 3, 2).reshape(nb * N_HEADS, seq))
    df_rows, db = _fox_gate_bwd(dcum, sv["f_rows"], lp["b_rows"])
    g["b_forget"] = db[:N_HEADS, 0]
    df = df_rows.reshape(nb, N_HEADS, seq).transpose(0, 2, 1).reshape(t, N_HEADS)
    *d_dil, dbias = _softmax_attn_bwd(
        proj, proj, proj, sv["o_dil"], sv["lse_dil"], do[2], nb=nb, qcol=cd, kcol=cd + 2, vcol=cd + 4, mode="dil",
        extra=(bias,), name=tag + "_dil_bwd")
    dlx, dlg, dvec, dwa, dwx = _lru_bwd(proj, lp["lru_vec"], lp["wa"], lp["wx"], do[3], nb=nb, name=tag + "_lru_bwd")
    g["lru_conv_w"], g["lru_conv_b"], g["lru_b_a"], g["lru_b_x"], g["lru_lambda"] = (
        dvec[0:4], dvec[4], dvec[5], dvec[6], dvec[7])
    g["lru_w_a"], g["lru_w_x"] = _block_diag_grad(dwa), _block_diag_grad(dwx)
    dproj = jnp.concatenate(list(d_sb) + d_fox + d_dil + [dlx, dlg, df, jnp.zeros((t, PROJ_W - COL_F - N_HEADS), F32)],
                            axis=1)
    g["w_in"] = _unpad_w_in(_mm(sv["h"], dproj, ta=True, name=tag + "_dw_in"))
    dh = _mm(dproj, lp["w_in"], tb=True, name=tag + "_dh")
    dx0, g["norm_mix_g"] = _rmsnorm_bwd(dh, sv["x0"], w["norm_mix_g"][l], dx1, name=tag + "_norm_mix_bwd")
    return dx0, g, dbias


def _local_step(x, mem, w, target, nb):
    depth = w["norm_mix_g"].shape[0]
    seq = x.shape[0] // nb
    bias = _dil_bias(w["rel_bias"], seq)
    lps = [_layer_params(w, l, nb) for l in range(depth)]
    saved = []
    for l in range(depth):
        x, sv = _layer_fwd(x, mem, w, lps[l], l, bias, nb)
        saved.append(sv)
    loss, dx, dg_final = _loss_head(x, w["final_norm_g"], target)
    per_layer, dbias = [None] * depth, None
    for l in reversed(range(depth)):
        dx, per_layer[l], db = _layer_bwd(dx, mem, saved[l], w, lps[l], l, bias, nb)
        dbias = db if dbias is None else dbias + db
    grads = {n: jnp.stack([per_layer[l][n] for l in range(depth)]) for n in per_layer[0]}
    grads["rel_bias"] = _dil_bias_bwd(dbias, seq)
    grads["final_norm_g"] = dg_final
    return loss, dx, grads


INPUTS = ("x", "mem") + WEIGHTS + ("loss_target",) + tuple("m_" + n for n in WEIGHTS) + tuple("v_" + n for n in WEIGHTS)


def kernel(*args):
    a = dict(zip(INPUTS, args, strict=True))
    nb, seq, d = a["x"].shape
    x = a["x"].reshape(nb * seq, d)
    mem = a["mem"].reshape(nb * a["mem"].shape[1], d)
    target = a["loss_target"].reshape(nb * seq, d)
    c = lax.axis_index("c")

    pk = _Packing([(n, a[n].shape) for n, _ in SHARDED], SHARD_ROW_UNIT)
    half_rows = pk.rows // 2
    w_flat = pk.pack({n: a[n] for n, _ in SHARDED})
    halves = _chip_exchange(w_flat.astype(BF16), bcast=True, own_half_rows=half_rows, name="gather_chips")
    both = _sibling_pair(halves, name="gather_sibling")
    gathered = pk.unpack(both.transpose(1, 0, 2, 3).reshape(4, pk.rows, ROW), lead=(4,))
    cpk = _Packing([(n, a[n].shape) for n in CONV_SHARDED], SMALL_ROW_UNIT)
    conv = cpk.unpack(_chip_exchange(cpk.pack({n: a[n] for n in CONV_SHARDED}), bcast=True, name="gather_conv"),
                      lead=(4,))
    w = {n: a[n] for n in REPLICATED}
    for n, axis in SHARDED:
        w[n] = _from_shards(conv[n] if n in CONV_SHARDED else gathered[n], axis)

    loss, dx, grads = _local_step(x, mem, w, target, nb)

    g_flat = pk.pack({n: _to_shards(grads[n], axis) for n, axis in SHARDED}, lead=(4,))
    from_sibling = _sibling_pair(g_flat, other_half_rows=half_rows, name="reduce_sibling")
    partial = _add_own_half(g_flat, from_sibling, c, name="reduce_add")
    mine = _sum_slots(_chip_exchange(partial, bcast=False, name="reduce_chips"), name="reduce_sum")
    g_shard = _sibling_pair(mine, name="reduce_share").reshape(pk.rows, ROW)
    delta, new_m, new_v = _adamw(w_flat, g_shard, pk.pack({n: a["m_" + n] for n, _ in SHARDED}),
                                 pk.pack({n: a["v_" + n] for n, _ in SHARDED}), name="adamw_sharded")
    out = {"grad": pk.unpack(g_shard), "delta": pk.unpack(delta), "m": pk.unpack(new_m), "v": pk.unpack(new_v)}

    spk = _Packing([(n, a[n].shape) for n in REPLICATED] + [("loss", (1,))], SMALL_ROW_UNIT)
    s_flat = spk.pack({**{n: grads[n] for n in REPLICATED}, "loss": loss.reshape(1)})
    s_all = _chip_exchange(_sibling_pair(s_flat, name="small_sibling"), bcast=True, name="small_chips")
    s_sum = _sum_slots(s_all, name="small_sum")
    zero = jnp.zeros((1,), F32)
    s_delta, s_m, s_v = _adamw(
        spk.pack({**{n: a[n] for n in REPLICATED}, "loss": zero}), s_sum,
        spk.pack({**{n: a["m_" + n] for n in REPLICATED}, "loss": zero}),
        spk.pack({**{n: a["v_" + n] for n in REPLICATED}, "loss": zero}), name="adamw_replicated")
    small = {"grad": spk.unpack(s_sum), "delta": spk.unpack(s_delta), "m": spk.unpack(s_m), "v": spk.unpack(s_v)}

    sharded = {n for n, _ in SHARDED}
    pick = lambda kind, n: (out if n in sharded else small)[kind][n]
    return (small["grad"]["loss"].reshape(()), dx.reshape(nb, seq, d),
            *[pick("grad", n) for n in WEIGHTS], *[pick("delta", n) for n in WEIGHTS],
            *[pick("m", n) for n in WEIGHTS], *[pick("v", n) for n in WEIGHTS])
```

```python
import math

import numpy as np
import jax
import jax.numpy as jnp
from jax import lax
from jax.experimental import pallas as pl
from jax.experimental.pallas import tpu as pltpu

F32 = jnp.float32
BF16 = jnp.bfloat16

HEAD_DIM = 64
N_HEADS = 4
N_IN = 2820
D_FF = 2816
LRU_C = 8.0
EPS = 1e-6
NUM_BUCKETS = 32
MAX_DISTANCE = 2048
DILATED_PATTERNS = ((128, 1), (512, 4), (2048, 16))
ADAM_LR, ADAM_B1, ADAM_B2, ADAM_EPS, ADAM_WD, ADAM_STEP = 0.001, 0.9, 0.999, 1e-08, 0.01, 10

LANES = 128
SUBLANES = 8
VMEM_LIMIT = 48 * 1024 * 1024

PROJ_W = 3072
PAIR_W = 3 * LANES
LRU_W = 2 * LANES
COL_LRU = 6 * PAIR_W
COL_F = COL_LRU + 2 * LRU_W
MIX_SB, MIX_FOX, MIX_DIL, MIX_LRU = 0, 1, 2, 3
ORIG_COL = {MIX_SB: 0, MIX_FOX: 768, MIX_DIL: 1540}
ORIG_LRU_X, ORIG_LRU_G = 2308, 2564

ATT_TILE = 256
MASKED = -1e30
SCALE = HEAD_DIM ** -0.5

NT_DIMS = (((1,), (1,)), ((), ()))
TN_DIMS = (((0,), (0,)), ((), ()))

MESH = pl.DeviceIdType.MESH
ANY = pl.BlockSpec(memory_space=pl.ANY)


def _params(sem):
    return pltpu.CompilerParams(dimension_semantics=sem, vmem_limit_bytes=VMEM_LIMIT)


def _tile(n, target, unit=LANES):
    if n <= target:
        return n
    t = (target // unit) * unit
    while t > unit and n % t:
        t -= unit
    assert n % t == 0, (n, target, unit)
    return t


def _mm(a, b, *, ta=False, tb=False, res=None, name, ti=512, tj=512, tc=1408):
    m, kc = (a.shape[1], a.shape[0]) if ta else a.shape
    n = b.shape[0] if tb else b.shape[1]
    assert (b.shape[1] if tb else b.shape[0]) == kc
    ti, tj, tc = _tile(m, ti, LANES if ta else SUBLANES), _tile(n, tj), _tile(kc, tc, SUBLANES if ta and tb else LANES)
    nk = kc // tc
    dims = (((0 if ta else 1,), (1 if tb else 0,)), ((), ()))

    def body(*refs):
        if res is None:
            a_ref, b_ref, o_ref, acc_ref = refs
        else:
            a_ref, b_ref, r_ref, o_ref, acc_ref = refs
        k = pl.program_id(2)

        @pl.when(k == 0)
        def _():
            acc_ref[...] = jnp.zeros_like(acc_ref)

        acc_ref[...] += lax.dot_general(a_ref[...].astype(BF16), b_ref[...].astype(BF16), dims,
                                        preferred_element_type=F32)

        @pl.when(k == nk - 1)
        def _():
            if res is None:
                o_ref[...] = acc_ref[...]
            else:
                o_ref[...] = r_ref[...] + acc_ref[...]

    a_spec = pl.BlockSpec((tc, ti), lambda i, j, k: (k, i)) if ta else pl.BlockSpec((ti, tc), lambda i, j, k: (i, k))
    b_spec = pl.BlockSpec((tj, tc), lambda i, j, k: (j, k)) if tb else pl.BlockSpec((tc, tj), lambda i, j, k: (k, j))
    o_spec = pl.BlockSpec((ti, tj), lambda i, j, k: (i, j))
    in_specs = [a_spec, b_spec] + ([] if res is None else [o_spec])
    args = (a, b) + (() if res is None else (res,))
    return pl.pallas_call(
        body, name=name, grid=(m // ti, n // tj, nk), in_specs=in_specs, out_specs=o_spec,
        out_shape=jax.ShapeDtypeStruct((m, n), F32), scratch_shapes=[pltpu.VMEM((ti, tj), F32)],
        compiler_params=_params(("parallel", "parallel", "arbitrary")))(*args)


def _xhat(x):
    return x * lax.rsqrt(jnp.mean(x * x, axis=-1, keepdims=True) + EPS)


def _norm_bwd_rows(dy, x, g):
    rstd = lax.rsqrt(jnp.mean(x * x, axis=-1, keepdims=True) + EPS)
    xh = x * rstd
    dxh = dy * g
    dx = rstd * (dxh - xh * jnp.mean(dxh * xh, axis=-1, keepdims=True))
    return dx, dy * xh


def _rmsnorm(x, g, *, name, rows=512):
    t, d = x.shape
    tr = _tile(t, rows, SUBLANES)

    def body(x_ref, g_ref, o_ref):
        o_ref[...] = _xhat(x_ref[...]) * g_ref[...]

    return pl.pallas_call(
        body, name=name, grid=(t // tr,),
        in_specs=[pl.BlockSpec((tr, d), lambda i: (i, 0)), pl.BlockSpec((1, d), lambda i: (0, 0))],
        out_specs=pl.BlockSpec((tr, d), lambda i: (i, 0)), out_shape=jax.ShapeDtypeStruct((t, d), F32),
        compiler_params=_params(("parallel",)))(x, g.reshape(1, d))


def _rmsnorm_bwd(dy, x, g, dres, *, name, rows=512):
    t, d = x.shape
    tr = _tile(t, rows, SUBLANES)

    def body(*refs):
        if dres is None:
            dy_ref, x_ref, g_ref, dx_ref, dg_ref = refs
        else:
            dy_ref, x_ref, g_ref, r_ref, dx_ref, dg_ref = refs
        dx, dgr = _norm_bwd_rows(dy_ref[...], x_ref[...], g_ref[...])
        dx_ref[...] = dx if dres is None else r_ref[...] + dx

        @pl.when(pl.program_id(0) == 0)
        def _():
            dg_ref[...] = jnp.zeros_like(dg_ref)

        dg_ref[...] += jnp.sum(dgr, axis=0, keepdims=True)

    row = pl.BlockSpec((tr, d), lambda i: (i, 0))
    vec = pl.BlockSpec((1, d), lambda i: (0, 0))
    in_specs = [row, row, vec] + ([] if dres is None else [row])
    args = (dy, x, g.reshape(1, d)) + (() if dres is None else (dres,))
    dx, dg = pl.pallas_call(
        body, name=name, grid=(t // tr,), in_specs=in_specs, out_specs=[row, vec],
        out_shape=[jax.ShapeDtypeStruct((t, d), F32), jax.ShapeDtypeStruct((1, d), F32)],
        compiler_params=_params(("arbitrary",)))(*args)
    return dx, dg.reshape(d)


def _loss_head(x, g, target, *, rows=512):
    t, d = x.shape
    tr = _tile(t, rows, SUBLANES)

    def body(x_ref, g_ref, t_ref, dx_ref, dg_ref, loss_ref):
        x_, g_ = x_ref[...], g_ref[...]
        err = _xhat(x_) * g_ - t_ref[...]
        dx, dgr = _norm_bwd_rows(err * (1.0 / d), x_, g_)
        dx_ref[...] = dx

        @pl.when(pl.program_id(0) == 0)
        def _():
            dg_ref[...] = jnp.zeros_like(dg_ref)
            loss_ref[...] = jnp.zeros_like(loss_ref)

        dg_ref[...] += jnp.sum(dgr, axis=0, keepdims=True)
        loss_ref[...] += 0.5 * jnp.sum(jnp.mean(err * err, axis=-1, keepdims=True), axis=0, keepdims=True)

    row = pl.BlockSpec((tr, d), lambda i: (i, 0))
    vec = pl.BlockSpec((1, d), lambda i: (0, 0))
    one = pl.BlockSpec((1, 1), lambda i: (0, 0))
    dx, dg, loss = pl.pallas_call(
        body, name="loss_head", grid=(t // tr,), in_specs=[row, vec, row], out_specs=[row, vec, one],
        out_shape=[jax.ShapeDtypeStruct((t, d), F32), jax.ShapeDtypeStruct((1, d), F32),
                   jax.ShapeDtypeStruct((1, 1), F32)],
        compiler_params=_params(("arbitrary",)))(x, g.reshape(1, d), target)
    return loss.reshape(()), dx, dg.reshape(d)


def _head_masks(shape):
    lane = lax.broadcasted_iota(jnp.int32, shape, len(shape) - 1)
    return lane < HEAD_DIM, lane >= HEAD_DIM


def _split_heads(x):
    m0, m1 = _head_masks(x.shape)
    zero = jnp.zeros_like(x)
    return jnp.where(m0, x, zero), jnp.where(m1, x, zero)


def _lane_pair(a0, a1, rows):
    m0, _ = _head_masks((rows, LANES))
    return jnp.where(m0, a0, a1)


def _qkv_readers(refs, packed):
    if packed:
        (r,) = refs
        return tuple((lambda r0, n, s=s: r[pl.ds(r0, n), s * LANES:(s + 1) * LANES]) for s in range(3))
    return tuple((lambda r0, n, ref=ref: ref[pl.ds(r0, n), :]) for ref in refs)


def _pair_spec(seq, col0, width=LANES):
    return pl.BlockSpec((seq, width), lambda p, b: (b, col0 + p))


def _fox_specs(seq, nk, tk):
    return [pl.BlockSpec((None, None, seq, 2), lambda p, b: (b, p, 0, 0)),
            pl.BlockSpec((None, None, nk, 2, tk), lambda p, b: (b, p, 0, 0, 0))]


def _softmax_attn_fwd(src, *, nb, mode, mixer=None, out_buf=None, extra=(), name):
    packed = mode != "cross"
    n_src = 1 if packed else 3
    seq_q = (src if packed else src[0]).shape[0] // nb
    seq_k = seq_q if packed else src[1].shape[0] // nb
    tq, tk = min(ATT_TILE, seq_q), min(ATT_TILE, seq_k)
    nq, nk = seq_q // tq, seq_k // tk
    n_ex = len(extra)

    def body(*refs):
        q_at, k_at, v_at = _qkv_readers(refs[:n_src], packed)
        ex = refs[n_src:n_src + n_ex]
        o_ref, lse_ref = refs[-2:]

        def q_tile(i, _):
            r0 = pl.multiple_of(i * tq, tq)
            qm = _split_heads((q_at(r0, tq) * SCALE).astype(BF16))
            if mode == "fox":
                cq = ex[0][pl.ds(r0, tq), :]
                row = r0 + lax.broadcasted_iota(jnp.int32, (tq, tk), 0)

            def k_tile(j, carry):
                m, l, acc = carry
                c0 = pl.multiple_of(j * tk, tk)
                kt = k_at(c0, tk).astype(BF16)
                vm = _split_heads(v_at(c0, tk).astype(BF16))
                if mode == "fox":
                    ck = ex[1][j]
                    keep = (c0 + lax.broadcasted_iota(jnp.int32, (tq, tk), 1)) <= row
                new_m, new_l, alphas, pv = [], [], [], None
                for h in range(2):
                    s = lax.dot_general(qm[h], kt, NT_DIMS, preferred_element_type=F32)
                    if mode == "fox":
                        s = jnp.where(keep, s + cq[:, h:h + 1] - ck[h:h + 1, :], MASKED)
                    elif mode == "dil":
                        s = s + ex[0][h, i - j]
                    mh = jnp.maximum(m[h], jnp.max(s, axis=-1, keepdims=True))
                    p = jnp.exp(s - mh)
                    alpha = jnp.exp(m[h] - mh)
                    new_m.append(mh)
                    new_l.append(alpha * l[h] + jnp.sum(p, axis=-1, keepdims=True))
                    alphas.append(alpha)
                    d = jnp.dot(p.astype(BF16), vm[h], preferred_element_type=F32)
                    pv = d if pv is None else pv + d
                acc = acc * _lane_pair(alphas[0], alphas[1], tq) + pv
                return tuple(new_m), tuple(new_l), acc

            init = ((jnp.full((tq, 1), MASKED, F32),) * 2, (jnp.zeros((tq, 1), F32),) * 2,
                    jnp.zeros((tq, LANES), F32))
            m, l, acc = lax.fori_loop(0, i + 1 if packed else nk, k_tile, init)
            o_ref[pl.ds(r0, tq), :] = acc / _lane_pair(l[0], l[1], tq)
            lse_ref[pl.ds(r0, tq), :] = _lane_pair(m[0] + jnp.log(l[0]), m[1] + jnp.log(l[1]), tq)
            return 0

        lax.fori_loop(0, nq, q_tile, 0)

    lse_shape = jax.ShapeDtypeStruct((nb * seq_q, 2 * LANES), F32)
    if packed:
        in_specs, args = [_pair_spec(seq_q, 2 * mixer, PAIR_W)], [src]
        in_specs += _fox_specs(seq_q, nk, tk) if mode == "fox" else [
            pl.BlockSpec((None, 2, nq, tq, tk), lambda p, b: (p, 0, 0, 0, 0))]
        args += list(extra) + [out_buf]
        in_specs.append(ANY)
        out_specs = [_pair_spec(seq_q, 2 * mixer), _pair_spec(seq_q, 0)]
        out_shape = [jax.ShapeDtypeStruct(out_buf.shape, F32), lse_shape]
        aliases = {len(args) - 1: 0}
    else:
        in_specs = [_pair_spec(seq_q, 0), _pair_spec(seq_k, 0), _pair_spec(seq_k, 0)]
        args = list(src)
        out_specs = [_pair_spec(seq_q, 0), _pair_spec(seq_q, 0)]
        out_shape = [lse_shape, lse_shape]
        aliases = {}
    return pl.pallas_call(
        body, name=name, grid=(2, nb), in_specs=in_specs, out_specs=out_specs, out_shape=out_shape,
        input_output_aliases=aliases, compiler_params=_params(("parallel", "arbitrary")))(*args)


def _softmax_attn_bwd(src, o, lse, do, *, nb, mode, mixer=None, dbuf=None, extra=(), name):
    packed = mode != "cross"
    n_src = 1 if packed else 3
    seq_q = (src if packed else src[0]).shape[0] // nb
    seq_k = seq_q if packed else src[1].shape[0] // nb
    tq, tk = min(ATT_TILE, seq_q), min(ATT_TILE, seq_k)
    nq, nk = seq_q // tq, seq_k // tk
    n_ex = len(extra)
    n_in = n_src + 3 + n_ex + (1 if packed else 0)

    def body(*refs):
        q_at, k_at, v_at = _qkv_readers(refs[:n_src], packed)
        o_ref, lse_ref, do_ref = refs[n_src:n_src + 3]
        ex = refs[n_src + 3:n_src + 3 + n_ex]
        outs = refs[n_in:]
        if packed:
            d_ref = outs[0]
            dq_w = lambda r0, val: d_ref.__setitem__((pl.ds(r0, tq), slice(0, LANES)), val)
            dk_ref = d_ref.at[:, LANES:2 * LANES]
            dv_ref = d_ref.at[:, 2 * LANES:3 * LANES]
        else:
            dq_ref, dk_ref, dv_ref = outs[:3]
            dq_w = lambda r0, val: dq_ref.__setitem__((pl.ds(r0, tq), slice(None)), val)
        dk_ref[...] = jnp.zeros((seq_k, LANES), F32)
        dv_ref[...] = jnp.zeros((seq_k, LANES), F32)
        if mode == "fox":
            dcum_ref, dcq_ref = outs[-2:]
            dcum_ref[...] = jnp.zeros_like(dcum_ref)
        if mode == "dil":
            dbias_ref = outs[-1]

            @pl.when(pl.program_id(1) == 0)
            def _():
                dbias_ref[...] = jnp.zeros_like(dbias_ref)

        def q_tile(i, _):
            r0 = pl.multiple_of(i * tq, tq)
            qm = _split_heads((q_at(r0, tq) * SCALE).astype(BF16))
            do_f = do_ref[pl.ds(r0, tq), :]
            dom = _split_heads(do_f.astype(BF16))
            dd = _split_heads(do_f * o_ref[pl.ds(r0, tq), :])
            delta = [jnp.sum(dd[h], axis=-1, keepdims=True) for h in range(2)]
            lse_t = lse_ref[pl.ds(r0, tq), :]
            lse_h = [lse_t[:, 0:1], lse_t[:, HEAD_DIM:HEAD_DIM + 1]]
            if mode == "fox":
                cq = ex[0][pl.ds(r0, tq), :]
                row = r0 + lax.broadcasted_iota(jnp.int32, (tq, tk), 0)

            def k_tile(j, carry):
                dq, rs = carry
                c0 = pl.multiple_of(j * tk, tk)
                kt = k_at(c0, tk).astype(BF16)
                vt = v_at(c0, tk).astype(BF16)
                km = _split_heads(kt)
                if mode == "fox":
                    ck = ex[1][j]
                    keep = (c0 + lax.broadcasted_iota(jnp.int32, (tq, tk), 1)) <= row
                dk_t, dv_t, new_rs = None, None, []
                for h in range(2):
                    s = lax.dot_general(qm[h], kt, NT_DIMS, preferred_element_type=F32)
                    if mode == "fox":
                        s = jnp.where(keep, s + cq[:, h:h + 1] - ck[h:h + 1, :], MASKED)
                    elif mode == "dil":
                        s = s + ex[0][h, i - j]
                    p = jnp.exp(s - lse_h[h])
                    dp = lax.dot_general(dom[h], vt, NT_DIMS, preferred_element_type=F32)
                    ds = p * (dp - delta[h])
                    dsb = ds.astype(BF16)
                    dq = dq + jnp.dot(dsb, km[h], preferred_element_type=F32)
                    a = lax.dot_general(dsb, qm[h], TN_DIMS, preferred_element_type=F32)
                    b = lax.dot_general(p.astype(BF16), dom[h], TN_DIMS, preferred_element_type=F32)
                    dk_t = a if dk_t is None else dk_t + a
                    dv_t = b if dv_t is None else dv_t + b
                    if mode == "fox":
                        dcum_ref[j, h:h + 1, :] -= jnp.sum(ds, axis=0, keepdims=True)
                        new_rs.append(rs[h] + jnp.sum(ds, axis=-1, keepdims=True))
                    elif mode == "dil":
                        dbias_ref[h, i - j] += ds
                dk_ref[pl.ds(c0, tk), :] += dk_t
                dv_ref[pl.ds(c0, tk), :] += dv_t
                return dq, (tuple(new_rs) if mode == "fox" else rs)

            zero = (jnp.zeros((tq, 1), F32),) * 2
            dq, rs = lax.fori_loop(0, i + 1 if packed else nk, k_tile, (jnp.zeros((tq, LANES), F32), zero))
            dq_w(r0, dq * SCALE)
            if mode == "fox":
                dcq_ref[pl.ds(r0, tq), :] = jnp.where(lax.broadcasted_iota(jnp.int32, (tq, 2), 1) == 0, rs[0], rs[1])
            return 0

        lax.fori_loop(0, nq, q_tile, 0)

    if packed:
        in_specs = [_pair_spec(seq_q, 2 * mixer, PAIR_W), _pair_spec(seq_q, 2 * mixer), _pair_spec(seq_q, 0),
                    _pair_spec(seq_q, 2 * mixer)]
        args = [src, o, lse, do]
        out_specs = [_pair_spec(seq_q, 2 * mixer, PAIR_W)]
        out_shape = [jax.ShapeDtypeStruct(dbuf.shape, F32)]
        if mode == "fox":
            in_specs += _fox_specs(seq_q, nk, tk)
            out_specs += [_fox_specs(seq_q, nk, tk)[1], _fox_specs(seq_q, nk, tk)[0]]
            out_shape += [jax.ShapeDtypeStruct((nb, 2, nk, 2, tk), F32), jax.ShapeDtypeStruct((nb, 2, seq_q, 2), F32)]
        else:
            tiles = pl.BlockSpec((None, 2, nq, tq, tk), lambda p, b: (p, 0, 0, 0, 0))
            in_specs.append(tiles)
            out_specs.append(tiles)
            out_shape.append(jax.ShapeDtypeStruct((2, 2, nq, tq, tk), F32))
        args += list(extra) + [dbuf]
        in_specs.append(ANY)
        aliases = {len(args) - 1: 0}
    else:
        sq, sk = _pair_spec(seq_q, 0), _pair_spec(seq_k, 0)
        in_specs, args = [sq, sk, sk, sq, sq, sq], list(src) + [o, lse, do]
        out_specs = [sq, sk, sk]
        out_shape = [jax.ShapeDtypeStruct((nb * seq_q, 2 * LANES), F32)] + [
            jax.ShapeDtypeStruct((nb * seq_k, 2 * LANES), F32)] * 2
        aliases = {}
    return pl.pallas_call(
        body, name=name, grid=(2, nb), in_specs=in_specs, out_specs=out_specs, out_shape=out_shape,
        input_output_aliases=aliases, compiler_params=_params(("parallel", "arbitrary")))(*args)


def _log_sigmoid(z):
    return jnp.minimum(z, 0.0) - jnp.log(1.0 + jnp.exp(-jnp.abs(z)))


def _split_bf16(x):
    hi = x.astype(BF16)
    return hi, (x - hi.astype(F32)).astype(BF16)


def _tri(n, fn):
    r = lax.broadcasted_iota(jnp.int32, (n, n), 0)
    c = lax.broadcasted_iota(jnp.int32, (n, n), 1)
    return jnp.where(fn(r, c), 1.0, 0.0).astype(BF16)


def _sb_attn_fwd(proj, out_buf, *, nb, name):
    seq = proj.shape[0] // nb
    tq = tk = min(ATT_TILE, seq)
    nq = seq // tq

    def body(qkv_ref, _, o_ref, lt_ref):
        q_at, k_at, v_at = _qkv_readers((qkv_ref,), True)
        after = _tri(tk, lambda r, c: r > c)

        def q_tile(i, _):
            r0 = pl.multiple_of(i * tq, tq)
            qm = _split_heads((q_at(r0, tq) * SCALE).astype(BF16))
            row = r0 + lax.broadcasted_iota(jnp.int32, (tq, tk), 0)

            def k_tile(jj, carry):
                c, acc = carry
                j = i - jj
                c0 = pl.multiple_of(j * tk, tk)
                kt = k_at(c0, tk).astype(BF16)
                vm = _split_heads(v_at(c0, tk).astype(BF16))
                strict = (c0 + lax.broadcasted_iota(jnp.int32, (tq, tk), 1)) < row
                new_c = []
                for h in range(2):
                    z = lax.dot_general(qm[h], kt, NT_DIMS, preferred_element_type=F32)
                    ls = _log_sigmoid(z)
                    lk = jnp.where(strict, ls - z, 0.0)
                    hi, lo = _split_bf16(lk)
                    sfx = (jnp.dot(hi, after, preferred_element_type=F32)
                           + jnp.dot(lo, after, preferred_element_type=F32))
                    att = jnp.where(strict, jnp.exp(ls + sfx + c[h]), 0.0)
                    acc = acc + jnp.dot(att.astype(BF16), vm[h], preferred_element_type=F32)
                    new_c.append(c[h] + jnp.sum(lk, axis=-1, keepdims=True))
                return tuple(new_c), acc

            init = ((jnp.zeros((tq, 1), F32),) * 2, jnp.zeros((tq, LANES), F32))
            c, acc = lax.fori_loop(0, i + 1, k_tile, init)
            o_ref[pl.ds(r0, tq), :] = acc
            lt_ref[pl.ds(r0, tq), :] = _lane_pair(c[0], c[1], tq)
            return 0

        lax.fori_loop(0, nq, q_tile, 0)

    return pl.pallas_call(
        body, name=name, grid=(2, nb), in_specs=[_pair_spec(seq, 2 * MIX_SB, PAIR_W), ANY],
        out_specs=[_pair_spec(seq, 2 * MIX_SB), _pair_spec(seq, 0)],
        out_shape=[jax.ShapeDtypeStruct(out_buf.shape, F32), jax.ShapeDtypeStruct((nb * seq, 2 * LANES), F32)],
        input_output_aliases={1: 0}, compiler_params=_params(("parallel", "arbitrary")))(proj, out_buf)


def _sb_attn_bwd(proj, ltot, do, dbuf, *, nb, name):
    seq = proj.shape[0] // nb
    tq = tk = min(ATT_TILE, seq)
    nq = seq // tq

    def body(qkv_ref, lt_ref, do_ref, _, d_ref):
        q_at, k_at, v_at = _qkv_readers((qkv_ref,), True)
        upto = _tri(tk, lambda r, c: r <= c)
        before = _tri(tk, lambda r, c: r < c)
        dk_ref = d_ref.at[:, LANES:2 * LANES]
        dv_ref = d_ref.at[:, 2 * LANES:3 * LANES]
        dk_ref[...] = jnp.zeros((seq, LANES), F32)
        dv_ref[...] = jnp.zeros((seq, LANES), F32)

        def q_tile(i, _):
            r0 = pl.multiple_of(i * tq, tq)
            qm = _split_heads((q_at(r0, tq) * SCALE).astype(BF16))
            dom = _split_heads(do_ref[pl.ds(r0, tq), :].astype(BF16))
            lt_t = lt_ref[pl.ds(r0, tq), :]
            lt_h = [lt_t[:, 0:1], lt_t[:, HEAD_DIM:HEAD_DIM + 1]]
            row = r0 + lax.broadcasted_iota(jnp.int32, (tq, tk), 0)

            def k_tile(j, carry):
                pc, qc, dq = carry
                c0 = pl.multiple_of(j * tk, tk)
                kt = k_at(c0, tk).astype(BF16)
                vt = v_at(c0, tk).astype(BF16)
                km = _split_heads(kt)
                strict = (c0 + lax.broadcasted_iota(jnp.int32, (tq, tk), 1)) < row
                new_pc, new_qc, dk_t, dv_t = [], [], None, None
                for h in range(2):
                    z = lax.dot_general(qm[h], kt, NT_DIMS, preferred_element_type=F32)
                    ls = _log_sigmoid(z)
                    lk = jnp.where(strict, ls - z, 0.0)
                    hi, lo = _split_bf16(lk)
                    pin = (jnp.dot(hi, upto, preferred_element_type=F32)
                           + jnp.dot(lo, upto, preferred_element_type=F32))
                    att = jnp.where(strict, jnp.exp(ls + (lt_h[h] - pc[h] - pin)), 0.0)
                    da = lax.dot_general(dom[h], vt, NT_DIMS, preferred_element_type=F32)
                    dg = att * da
                    qx = qc[h] + jnp.dot(dg.astype(BF16), before, preferred_element_type=F32)
                    sig = jnp.exp(ls)
                    dz = jnp.where(strict, dg * (1.0 - sig) - sig * qx, 0.0)
                    dzb = dz.astype(BF16)
                    dq = dq + jnp.dot(dzb, km[h], preferred_element_type=F32)
                    a = lax.dot_general(dzb, qm[h], TN_DIMS, preferred_element_type=F32)
                    b = lax.dot_general(att.astype(BF16), dom[h], TN_DIMS, preferred_element_type=F32)
                    dk_t = a if dk_t is None else dk_t + a
                    dv_t = b if dv_t is None else dv_t + b
                    new_pc.append(pc[h] + jnp.sum(lk, axis=-1, keepdims=True))
                    new_qc.append(qc[h] + jnp.sum(dg, axis=-1, keepdims=True))
                dk_ref[pl.ds(c0, tk), :] += dk_t
                dv_ref[pl.ds(c0, tk), :] += dv_t
                return tuple(new_pc), tuple(new_qc), dq

            zero = (jnp.zeros((tq, 1), F32),) * 2
            _, _, dq = lax.fori_loop(0, i + 1, k_tile, (zero, zero, jnp.zeros((tq, LANES), F32)))
            d_ref[pl.ds(r0, tq), 0:LANES] = dq * SCALE
            return 0

        lax.fori_loop(0, nq, q_tile, 0)

    return pl.pallas_call(
        body, name=name, grid=(2, nb),
        in_specs=[_pair_spec(seq, 2 * MIX_SB, PAIR_W), _pair_spec(seq, 0), _pair_spec(seq, 2 * MIX_SB), ANY],
        out_specs=_pair_spec(seq, 2 * MIX_SB, PAIR_W), out_shape=jax.ShapeDtypeStruct(dbuf.shape, F32),
        input_output_aliases={3: 0}, compiler_params=_params(("parallel", "arbitrary")))(proj, ltot, do, dbuf)


def _lane_scan(x, reverse=False):
    n = x.shape[-1]
    lane = lax.broadcasted_iota(jnp.int32, x.shape, 1)
    k = 1
    while k < n:
        if reverse:
            x = x + jnp.where(lane < n - k, pltpu.roll(x, n - k, 1), 0.0)
        else:
            x = x + jnp.where(lane >= k, pltpu.roll(x, k, 1), 0.0)
        k *= 2
    return x


def _fox_gate_fwd(f_rows, b_rows):
    def body(f_ref, b_ref, o_ref):
        o_ref[...] = _lane_scan(_log_sigmoid(f_ref[...] + b_ref[...]))

    return pl.pallas_call(body, name="fox_gate_fwd", out_shape=jax.ShapeDtypeStruct(f_rows.shape, F32))(f_rows, b_rows)


def _fox_gate_bwd(dcum, f_rows, b_rows):
    def body(d_ref, f_ref, b_ref, df_ref, db_ref):
        z = f_ref[...] + b_ref[...]
        df = _lane_scan(d_ref[...], reverse=True) * jnp.exp(_log_sigmoid(-z))
        df_ref[...] = df
        rs = jnp.sum(df, axis=-1, keepdims=True)
        tot = rs
        for e in range(1, f_rows.shape[0] // N_HEADS):
            tot = tot + pltpu.roll(rs, e * N_HEADS, 0)
        db_ref[...] = tot

    return pl.pallas_call(
        body, name="fox_gate_bwd",
        out_shape=[jax.ShapeDtypeStruct(f_rows.shape, F32), jax.ShapeDtypeStruct((f_rows.shape[0], 1), F32)],
    )(dcum, f_rows, b_rows)


def _dil_tables(seq):
    t = min(ATT_TILE, seq)
    n = seq // t
    a = np.arange(t)
    d = (np.arange(n)[:, None, None] * t + a[None, :, None] - a[None, None, :]).astype(np.int64)
    count = np.zeros(d.shape, np.int64)
    for window, dil in DILATED_PATTERNS:
        count += (d >= 0) & (d % dil == 0) & (d // dil <= window // dil)
    nn = np.maximum(d, 0)
    max_exact = NUM_BUCKETS // 2
    nf = np.maximum(nn, 1).astype(np.float32)
    large = max_exact + (np.log(nf / np.float32(max_exact)) / np.float32(math.log(MAX_DISTANCE / max_exact))
                         * np.float32(NUM_BUCKETS - max_exact)).astype(np.int32)
    bucket = np.where(nn < max_exact, nn, np.minimum(large, NUM_BUCKETS - 1))
    bucket = np.where(count > 0, bucket, -1).astype(np.int32)
    logc = np.where(count > 0, np.log(np.maximum(count, 1)), MASKED).astype(np.float32)
    return bucket, logc


def _dil_bias(rel_bias, seq):
    bucket, logc = _dil_tables(seq)
    n, t, _ = bucket.shape

    def body(rb_ref, bk_ref, lc_ref, o_ref):
        h = pl.program_id(0) * 2 + pl.program_id(1)
        bk = bk_ref[...]
        out = lc_ref[...]
        for b in range(NUM_BUCKETS):
            out = jnp.where(bk == b, out + rb_ref[b, h], out)
        o_ref[...] = out

    full = pl.BlockSpec((n, t, t), lambda p, h: (0, 0, 0))
    return pl.pallas_call(
        body, name="dil_bias", grid=(2, 2),
        in_specs=[pl.BlockSpec(memory_space=pltpu.SMEM), full, full],
        out_specs=pl.BlockSpec((None, None, n, t, t), lambda p, h: (p, h, 0, 0, 0)),
        out_shape=jax.ShapeDtypeStruct((2, 2, n, t, t), F32),
        compiler_params=_params(("parallel", "parallel")))(rel_bias, jnp.asarray(bucket), jnp.asarray(logc))


def _dil_bias_bwd(dbias, seq):
    bucket, _ = _dil_tables(seq)
    n, t, _ = bucket.shape

    def body(d_ref, bk_ref, o_ref):
        bk = bk_ref[...]
        lane = lax.broadcasted_iota(jnp.int32, (1, LANES), 1)
        for b in range(NUM_BUCKETS):
            rowv = jnp.zeros((1, LANES), F32)
            for h in range(N_HEADS):
                s = jnp.sum(jnp.where(bk == b, d_ref[h // 2, h % 2], 0.0))
                rowv = jnp.where(lane == h, s, rowv)
            o_ref[b:b + 1, :] = rowv

    out = pl.pallas_call(body, name="dil_bias_bwd", out_shape=jax.ShapeDtypeStruct((NUM_BUCKETS, LANES), F32),
                         compiler_params=pltpu.CompilerParams(vmem_limit_bytes=VMEM_LIMIT))(dbias, jnp.asarray(bucket))
    return out[:, :N_HEADS]


def _shift_rows(x, k, row, fill=0.0):
    n = x.shape[0]
    if k > 0:
        return jnp.where(row >= k, pltpu.roll(x, k, 0), fill)
    return jnp.where(row < n + k, pltpu.roll(x, n + k, 0), fill)


def _row_scan(a, u, row, reverse=False):
    n = a.shape[0]
    k = 1
    while k < n:
        s = -k if reverse else k
        u = a * _shift_rows(u, s, row) + u
        a = a * _shift_rows(a, s, row, 1.0)
        k *= 2
    return u


def _sigmoid(x):
    return 1.0 / (1.0 + jnp.exp(-x))


def _gelu(g):
    return 0.5 * g * (1.0 + lax.erf(g * (2.0 ** -0.5)))


def _gelu_grad(g):
    return 0.5 * (1.0 + lax.erf(g * (2.0 ** -0.5))) + g * jnp.exp(-0.5 * g * g) * (1.0 / math.sqrt(2.0 * math.pi))


def _neg_expm1(x):
    small = -x * (1.0 + x * (0.5 + x * (1.0 / 6.0 + x * (1.0 / 24.0))))
    return jnp.where(x > -0.03, small, 1.0 - jnp.exp(x))


def _lru_core(x, vec, wa, wx, row):
    xs = [_shift_rows(x, 3 - j, row) if j < 3 else x for j in range(4)]
    xc = vec[4:5, :]
    for j in range(4):
        xc = xc + vec[j:j + 1, :] * xs[j]
    xcb = xc.astype(BF16)
    r = _sigmoid(jnp.dot(xcb, wa, preferred_element_type=F32) + vec[5:6, :])
    ig = _sigmoid(jnp.dot(xcb, wx, preferred_element_type=F32) + vec[6:7, :])
    lam = vec[7:8, :]
    sp = jnp.maximum(-lam, 0.0) - _log_sigmoid(jnp.abs(lam))
    la = -LRU_C * r * sp
    a = jnp.exp(la)
    mult = jnp.sqrt(_neg_expm1(2.0 * la))
    return xs, xc, xcb, r, ig, sp, la, a, mult


def _lru_specs(seq):
    xg = pl.BlockSpec((seq, LRU_W), lambda hf, b: (b, COL_LRU // LRU_W + hf))
    mix = pl.BlockSpec((seq, LANES), lambda hf, b: (b, 2 * MIX_LRU + hf))
    vec = pl.BlockSpec((SUBLANES, LANES), lambda hf, b: (0, hf))
    mat = pl.BlockSpec((None, LANES, LANES), lambda hf, b: (hf, 0, 0))
    return xg, mix, vec, mat


def _lru_fwd(proj, vec, wa, wx, out_buf, *, nb, name):
    seq = proj.shape[0] // nb

    def body(xg_ref, vec_ref, wa_ref, wx_ref, _, o_ref):
        row = lax.broadcasted_iota(jnp.int32, (seq, LANES), 0)
        _, xc, _, _, ig, _, _, a, mult = _lru_core(xg_ref[:, 0:LANES], vec_ref[...], wa_ref[...], wx_ref[...], row)
        h = _row_scan(a, mult * (ig * xc), row)
        o_ref[...] = h * _gelu(xg_ref[:, LANES:LRU_W])

    xg, mix, vecs, mat = _lru_specs(seq)
    return pl.pallas_call(
        body, name=name, grid=(2, nb), in_specs=[xg, vecs, mat, mat, ANY], out_specs=mix,
        out_shape=jax.ShapeDtypeStruct(out_buf.shape, F32), input_output_aliases={4: 0},
        compiler_params=_params(("parallel", "arbitrary")))(proj, vec, wa, wx, out_buf)


def _lru_bwd(proj, vec, wa, wx, dout, dbuf, *, nb, name):
    seq = proj.shape[0] // nb

    def body(xg_ref, vec_ref, wa_ref, wx_ref, do_ref, _, d_ref, dvec_ref, dwa_ref, dwx_ref):
        row = lax.broadcasted_iota(jnp.int32, (seq, LANES), 0)
        vec_, wa_, wx_ = vec_ref[...], wa_ref[...], wx_ref[...]
        xs, xc, xcb, r, ig, sp, la, a, mult = _lru_core(xg_ref[:, 0:LANES], vec_, wa_, wx_, row)
        h = _row_scan(a, mult * (ig * xc), row)
        gate, do = xg_ref[:, LANES:LRU_W], do_ref[...]
        d_ref[:, LANES:LRU_W] = do * h * _gelu_grad(gate)
        dh = do * _gelu(gate)
        gacc = _row_scan(_shift_rows(a, -1, row), dh, row, reverse=True)
        da = gacc * _shift_rows(h, 1, row)
        dmult = gacc * (ig * xc)
        dig = gacc * (mult * xc)
        dxc = gacc * (mult * ig)
        dla = da * a - dmult * (a * a) / mult
        dr = (-LRU_C) * sp * dla
        dsp = jnp.sum((-LRU_C) * r * dla, axis=0, keepdims=True)
        dpr = dr * r * (1.0 - r)
        dpi = dig * ig * (1.0 - ig)
        dprb, dpib = dpr.astype(BF16), dpi.astype(BF16)
        dxc = (dxc + lax.dot_general(dprb, wa_, NT_DIMS, preferred_element_type=F32)
               + lax.dot_general(dpib, wx_, NT_DIMS, preferred_element_type=F32))
        dx = vec_[3:4, :] * dxc
        for j in range(3):
            dx = dx + vec_[j:j + 1, :] * _shift_rows(dxc, -(3 - j), row)
        d_ref[:, 0:LANES] = dx

        @pl.when(pl.program_id(1) == 0)
        def _():
            dvec_ref[...] = jnp.zeros_like(dvec_ref)
            dwa_ref[...] = jnp.zeros_like(dwa_ref)
            dwx_ref[...] = jnp.zeros_like(dwx_ref)

        for j in range(4):
            dvec_ref[j:j + 1, :] += jnp.sum(dxc * xs[j], axis=0, keepdims=True)
        dvec_ref[4:5, :] += jnp.sum(dxc, axis=0, keepdims=True)
        dvec_ref[5:6, :] += jnp.sum(dpr, axis=0, keepdims=True)
        dvec_ref[6:7, :] += jnp.sum(dpi, axis=0, keepdims=True)
        lam = vec_[7:8, :]
        dvec_ref[7:8, :] += -dsp * _sigmoid(-lam)
        dwa_ref[...] += lax.dot_general(xcb, dprb, TN_DIMS, preferred_element_type=F32)
        dwx_ref[...] += lax.dot_general(xcb, dpib, TN_DIMS, preferred_element_type=F32)

    xg, mix, vecs, mat = _lru_specs(seq)
    return pl.pallas_call(
        body, name=name, grid=(2, nb), in_specs=[xg, vecs, mat, mat, mix, ANY], out_specs=[xg, vecs, mat, mat],
        out_shape=[jax.ShapeDtypeStruct(dbuf.shape, F32), jax.ShapeDtypeStruct((SUBLANES, 2 * LANES), F32),
                   jax.ShapeDtypeStruct((2, LANES, LANES), F32), jax.ShapeDtypeStruct((2, LANES, LANES), F32)],
        input_output_aliases={5: 0},
        compiler_params=_params(("parallel", "arbitrary")))(proj, vec, wa, wx, dout, dbuf)


FFN_ROWS = 512
FFN_COLS = 256


def _with_halo(halo, x, k):
    xx = jnp.concatenate([halo, x], axis=0)
    return pltpu.roll(xx, k, 0)[SUBLANES:, :]


def _ffn_conv(x_ref, halo_ref, cw, pos):
    x, halo = x_ref[...], halo_ref[...]
    x1 = jnp.where(pos >= 1, _with_halo(halo, x, 1), 0.0)
    x2 = jnp.where(pos >= 2, _with_halo(halo, x, 2), 0.0)
    return cw[3:4, :] + cw[0:1, :] * x2 + cw[1:2, :] * x1 + cw[2:3, :] * x, x1, x2


def _ffn_specs(tm, tn, gate_off):
    prev = lambda i: jnp.maximum(i * (tm // SUBLANES) - 1, 0)
    up = pl.BlockSpec((tm, tn), lambda j, i: (i, j))
    gate = pl.BlockSpec((tm, tn), lambda j, i: (i, j + gate_off))
    up_h = pl.BlockSpec((SUBLANES, tn), lambda j, i: (prev(i), j))
    gate_h = pl.BlockSpec((SUBLANES, tn), lambda j, i: (prev(i), j + gate_off))
    cw_up = pl.BlockSpec((SUBLANES, tn), lambda j, i: (0, j))
    cw_gate = pl.BlockSpec((SUBLANES, tn), lambda j, i: (0, j + gate_off))
    return up, gate, up_h, gate_h, cw_up, cw_gate


def _ffn_act(hf, cw, *, seq, name):
    t, w2 = hf.shape
    w = w2 // 2
    tm, tn = _tile(seq, FFN_ROWS, SUBLANES), _tile(w, FFN_COLS)

    def body(u_ref, g_ref, uh_ref, gh_ref, cu_ref, cg_ref, o_ref):
        pos = (pl.program_id(1) * tm + lax.broadcasted_iota(jnp.int32, (tm, 1), 0)) % seq
        up, _, _ = _ffn_conv(u_ref, uh_ref, cu_ref[...], pos)
        gate, _, _ = _ffn_conv(g_ref, gh_ref, cg_ref[...], pos)
        o_ref[...] = _gelu(gate) * up

    specs = _ffn_specs(tm, tn, w // tn)
    return pl.pallas_call(
        body, name=name, grid=(w // tn, t // tm), in_specs=list(specs), out_specs=specs[0],
        out_shape=jax.ShapeDtypeStruct((t, w), F32),
        compiler_params=_params(("parallel", "parallel")))(hf, hf, hf, hf, cw, cw)


def _ffn_act_bwd(hf, cw, dact, *, seq, name):
    t, w2 = hf.shape
    w = w2 // 2
    tm, tn = _tile(seq, FFN_ROWS, SUBLANES), _tile(w, FFN_COLS)

    def body(u_ref, g_ref, uh_ref, gh_ref, cu_ref, cg_ref, da_ref, d_ref, dcu_ref, dcg_ref):
        pos = (pl.program_id(1) * tm + lax.broadcasted_iota(jnp.int32, (tm, 1), 0)) % seq
        up, u1, u2 = _ffn_conv(u_ref, uh_ref, cu_ref[...], pos)
        gate, g1, g2 = _ffn_conv(g_ref, gh_ref, cg_ref[...], pos)
        da = da_ref[...]
        d_up = da * _gelu(gate)
        d_gate = da * up * _gelu_grad(gate)
        d_ref[0] = d_up
        d_ref[1] = d_gate

        @pl.when(pl.program_id(1) == 0)
        def _():
            dcu_ref[...] = jnp.zeros_like(dcu_ref)
            dcg_ref[...] = jnp.zeros_like(dcg_ref)

        for ref, d, taps in ((dcu_ref, d_up, (u2, u1, u_ref[...])), (dcg_ref, d_gate, (g2, g1, g_ref[...]))):
            for j in range(3):
                ref[j:j + 1, :] += jnp.sum(d * taps[j], axis=0, keepdims=True)
            ref[3:4, :] += jnp.sum(d, axis=0, keepdims=True)

    specs = _ffn_specs(tm, tn, w // tn)
    tile, cwt = specs[0], specs[4]
    return pl.pallas_call(
        body, name=name, grid=(w // tn, t // tm), in_specs=list(specs) + [tile],
        out_specs=[pl.BlockSpec((2, tm, tn), lambda j, i: (0, i, j)), cwt, cwt],
        out_shape=[jax.ShapeDtypeStruct((2, t, w), F32), jax.ShapeDtypeStruct((SUBLANES, w), F32),
                   jax.ShapeDtypeStruct((SUBLANES, w), F32)],
        compiler_params=_params(("parallel", "arbitrary")))(hf, hf, hf, hf, cw, cw, dact)


def _conv_transpose(d, cw, *, seq, name):
    _, t, w = d.shape
    tm, tn = _tile(seq, FFN_ROWS, SUBLANES), _tile(w, FFN_COLS)
    last = t // SUBLANES - 1
    ncol = w // tn

    def body(d_ref, nx_ref, cw_ref, o_ref):
        pos = (pl.program_id(2) * tm + lax.broadcasted_iota(jnp.int32, (tm, 1), 0)) % seq
        x, cw_ = d_ref[...], cw_ref[...]
        xx = jnp.concatenate([x, nx_ref[...]], axis=0)
        n = tm + SUBLANES
        x1 = jnp.where(pos < seq - 1, pltpu.roll(xx, n - 1, 0)[:tm, :], 0.0)
        x2 = jnp.where(pos < seq - 2, pltpu.roll(xx, n - 2, 0)[:tm, :], 0.0)
        o_ref[...] = cw_[2:3, :] * x + cw_[1:2, :] * x1 + cw_[0:1, :] * x2

    tile = pl.BlockSpec((None, tm, tn), lambda h, j, i: (h, i, j))
    nxt = pl.BlockSpec((None, SUBLANES, tn), lambda h, j, i: (h, jnp.minimum((i + 1) * (tm // SUBLANES), last), j))
    cws = pl.BlockSpec((SUBLANES, tn), lambda h, j, i: (0, h * ncol + j))
    return pl.pallas_call(
        body, name=name, grid=(2, ncol, t // tm), in_specs=[tile, nxt, cws],
        out_specs=pl.BlockSpec((tm, tn), lambda h, j, i: (i, h * ncol + j)),
        out_shape=jax.ShapeDtypeStruct((t, 2 * w), F32),
        compiler_params=_params(("parallel", "parallel", "parallel")))(d, d, cw)


def _adamw(w, g, m, v, *, name, rows=256):
    nl, r, c = w.shape
    tr = _tile(r, rows, SUBLANES)

    def body(w_ref, g_ref, m_ref, v_ref, d_ref, nm_ref, nv_ref):
        g_ = g_ref[...]
        nm = ADAM_B1 * m_ref[...] + (1.0 - ADAM_B1) * g_
        nv = ADAM_B2 * v_ref[...] + (1.0 - ADAM_B2) * (g_ * g_)
        m_hat = nm / (1.0 - ADAM_B1 ** ADAM_STEP)
        v_hat = nv / (1.0 - ADAM_B2 ** ADAM_STEP)
        d_ref[...] = -ADAM_LR * (m_hat / (jnp.sqrt(v_hat) + ADAM_EPS) + ADAM_WD * w_ref[...])
        nm_ref[...] = nm
        nv_ref[...] = nv

    spec = pl.BlockSpec((None, tr, c), lambda l, i: (l, i, 0))
    shape = jax.ShapeDtypeStruct((nl, r, c), F32)
    return pl.pallas_call(body, name=name, grid=(nl, r // tr), in_specs=[spec] * 4, out_specs=[spec] * 3,
                          out_shape=[shape] * 3, compiler_params=_params(("parallel", "parallel")))(w, g, m, v)


def _mesh_pos():
    return lax.axis_index("x"), lax.axis_index("y"), lax.axis_index("c")


def _peers(x, y):
    chips = [(1 - x, y), (x, 1 - y), (1 - x, 1 - y)]
    return [(px, py, 2 * px + py) for px, py in chips]


def _remote(src, dst, send_sems, recv_sems, idx, to):
    return pltpu.make_async_remote_copy(src, dst, send_sems.at[idx], recv_sems.at[idx], device_id=to,
                                        device_id_type=MESH)


def _gather_weights(bufs, *, name):
    n = len(bufs)

    def body(*refs):
        ins, outs = refs[:n], refs[n:2 * n]
        send_sems, recv_sems = refs[2 * n:]
        x, y, c = _mesh_pos()
        me = 2 * x + y
        peers = _peers(x, y)
        started = []
        for i in range(n):
            for r, (px, py, _) in enumerate(peers):
                cp = _remote(ins[i].at[me, c], outs[i].at[me, c], send_sems, recv_sems, (i, r), (px, py, c))
                cp.start()
                started.append(cp)
        for i in range(n):
            for r, (px, py, k) in enumerate(peers):
                _remote(outs[i].at[k, c], outs[i].at[k, c], send_sems, recv_sems, (i, r), (px, py, c)).wait_recv()
                cp = _remote(outs[i].at[k, c], outs[i].at[k, c], send_sems, recv_sems, (i, 3 + r), (x, y, 1 - c))
                cp.start()
                started.append(cp)
        for i in range(n):
            for r, (_, _, k) in enumerate(peers):
                _remote(outs[i].at[k, 1 - c], outs[i].at[k, 1 - c], send_sems, recv_sems, (i, 3 + r),
                        (x, y, 1 - c)).wait_recv()
        for cp in started:
            cp.wait_send()

    return pl.pallas_call(
        body, name=name, in_specs=[ANY] * n, out_specs=[ANY] * n,
        out_shape=[jax.ShapeDtypeStruct(b.shape, b.dtype) for b in bufs],
        input_output_aliases={i: i for i in range(n)},
        scratch_shapes=[pltpu.SemaphoreType.DMA((n, 6)), pltpu.SemaphoreType.DMA((n, 6))])(*bufs)


def _reduce_sibling(gs, *, name):
    n = len(gs)

    def body(*refs):
        ins, outs = refs[:n], refs[n:2 * n]
        send_sems, recv_sems = refs[2 * n:]
        x, y, c = _mesh_pos()
        cps = []
        for i in range(n):
            h = gs[i].shape[1] // 2
            cp = _remote(ins[i].at[:, pl.ds((1 - c) * h, h), :], outs[i], send_sems, recv_sems, i, (x, y, 1 - c))
            cp.start()
            cps.append(cp)
        for cp in cps:
            cp.wait()

    return pl.pallas_call(
        body, name=name, in_specs=[ANY] * n, out_specs=[ANY] * n,
        out_shape=[jax.ShapeDtypeStruct((g.shape[0], g.shape[1] // 2, g.shape[2]), g.dtype) for g in gs],
        scratch_shapes=[pltpu.SemaphoreType.DMA((n,)), pltpu.SemaphoreType.DMA((n,))])(*gs)


def _reduce_chips(ps, *, name):
    n = len(ps)

    def body(*refs):
        ins, outs = refs[:n], refs[n:2 * n]
        send_sems, recv_sems = refs[2 * n:]
        x, y, c = _mesh_pos()
        cps = []
        for i in range(n):
            for r, (px, py, k) in enumerate(_peers(x, y)):
                cp = _remote(ins[i].at[k], outs[i].at[r], send_sems, recv_sems, (i, r), (px, py, c))
                cp.start()
                cps.append(cp)
        for cp in cps:
            cp.wait()

    return pl.pallas_call(
        body, name=name, in_specs=[ANY] * n, out_specs=[ANY] * n,
        out_shape=[jax.ShapeDtypeStruct((3,) + p.shape[1:], p.dtype) for p in ps],
        scratch_shapes=[pltpu.SemaphoreType.DMA((n, 3)), pltpu.SemaphoreType.DMA((n, 3))])(*ps)


def _share_halves(bufs, *, name):
    n = len(bufs)

    def body(*refs):
        ins, outs = refs[:n], refs[n:2 * n]
        send_sems, recv_sems = refs[2 * n:]
        x, y, c = _mesh_pos()
        cps = []
        for i in range(n):
            h = bufs[i].shape[1] // 2
            mine = pl.ds(c * h, h)
            cp = _remote(ins[i].at[:, mine, :], outs[i].at[:, mine, :], send_sems, recv_sems, i, (x, y, 1 - c))
            cp.start()
            cps.append(cp)
        for cp in cps:
            cp.wait()

    return pl.pallas_call(
        body, name=name, in_specs=[ANY] * n, out_specs=[ANY] * n,
        out_shape=[jax.ShapeDtypeStruct(b.shape, b.dtype) for b in bufs],
        input_output_aliases={i: i for i in range(n)},
        scratch_shapes=[pltpu.SemaphoreType.DMA((n,)), pltpu.SemaphoreType.DMA((n,))])(*bufs)


def _add_own_half(full, recv, pos, *, name, rows=256):
    k4, h, n = recv.shape
    tr = _tile(h, rows, 16)
    nblk = h // tr

    def body(pos_ref, a_ref, b_ref, o_ref):
        o_ref[...] = (a_ref[...] + b_ref[...]).astype(BF16)

    grid_spec = pltpu.PrefetchScalarGridSpec(
        num_scalar_prefetch=1, grid=(k4, nblk),
        in_specs=[pl.BlockSpec((None, tr, n), lambda k, i, pos_ref: (k, pos_ref[1] * nblk + i, 0)),
                  pl.BlockSpec((None, tr, n), lambda k, i, pos_ref: (k, i, 0))],
        out_specs=pl.BlockSpec((None, tr, n), lambda k, i, pos_ref: (k, i, 0)))
    return pl.pallas_call(body, name=name, grid_spec=grid_spec, out_shape=jax.ShapeDtypeStruct(recv.shape, BF16),
                          compiler_params=_params(("parallel", "parallel")))(pos, full, recv)


def _sum_into(own, others, buf, pos, layer, *, name, rows=256):
    _, h, n = own.shape
    tr = _tile(h, rows, 16)
    nblk = h // tr

    def body(pos_ref, own_ref, oth_ref, _, o_ref):
        acc = own_ref[...].astype(F32)
        for r in range(3):
            acc = acc + oth_ref[r].astype(F32)
        o_ref[...] = acc

    grid_spec = pltpu.PrefetchScalarGridSpec(
        num_scalar_prefetch=1, grid=(nblk,),
        in_specs=[pl.BlockSpec((None, tr, n), lambda i, pos_ref: (pos_ref[0], i, 0)),
                  pl.BlockSpec((3, tr, n), lambda i, pos_ref: (0, i, 0)), ANY],
        out_specs=pl.BlockSpec((None, tr, n), lambda i, pos_ref: (layer, pos_ref[1] * nblk + i, 0)))
    return pl.pallas_call(body, name=name, grid_spec=grid_spec, out_shape=jax.ShapeDtypeStruct(buf.shape, F32),
                          input_output_aliases={3: 0}, compiler_params=_params(("parallel",)))(pos, own, others, buf)


def _sibling_pair(buf, *, name):
    def body(src_ref, out_ref, send_sem, recv_sem, local_sem):
        x, y, c = _mesh_pos()
        local = pltpu.make_async_copy(src_ref, out_ref.at[c], local_sem)
        local.start()
        cp = pltpu.make_async_remote_copy(src_ref, out_ref.at[c], send_sem, recv_sem, device_id=(x, y, 1 - c),
                                          device_id_type=MESH)
        cp.start()
        cp.wait()
        local.wait()

    return pl.pallas_call(
        body, name=name, in_specs=[ANY], out_specs=ANY, out_shape=jax.ShapeDtypeStruct((2,) + buf.shape, buf.dtype),
        scratch_shapes=[pltpu.SemaphoreType.DMA, pltpu.SemaphoreType.DMA, pltpu.SemaphoreType.DMA])(buf)


def _chip_bcast(buf, *, name):
    def body(src_ref, out_ref, send_sems, recv_sems, local_sem):
        x, y, c = _mesh_pos()
        me = 2 * x + y
        local = pltpu.make_async_copy(src_ref, out_ref.at[me], local_sem)
        local.start()
        sends = []
        for r, (px, py, _) in enumerate(_peers(x, y)):
            cp = _remote(src_ref, out_ref.at[me], send_sems, recv_sems, r, (px, py, c))
            cp.start()
            sends.append(cp)
        for r, (px, py, k) in enumerate(_peers(x, y)):
            _remote(src_ref, out_ref.at[k], send_sems, recv_sems, r, (px, py, c)).wait_recv()
        for cp in sends:
            cp.wait_send()
        local.wait()

    return pl.pallas_call(
        body, name=name, in_specs=[ANY], out_specs=ANY, out_shape=jax.ShapeDtypeStruct((4,) + buf.shape, buf.dtype),
        scratch_shapes=[pltpu.SemaphoreType.DMA((3,)), pltpu.SemaphoreType.DMA((3,)), pltpu.SemaphoreType.DMA])(buf)


def _sum_slots(buf, *, name, rows=384):
    r, n = buf.shape[-2:]
    k = int(np.prod(buf.shape[:-2]))
    tr = _tile(r, rows, SUBLANES)

    def body(b_ref, o_ref):
        acc = b_ref[0]
        for s in range(1, k):
            acc = acc + b_ref[s]
        o_ref[...] = acc

    return pl.pallas_call(
        body, name=name, grid=(r // tr,), in_specs=[pl.BlockSpec((k, tr, n), lambda i: (0, i, 0))],
        out_specs=pl.BlockSpec((tr, n), lambda i: (i, 0)), out_shape=jax.ShapeDtypeStruct((r, n), F32),
        compiler_params=_params(("parallel",)))(buf.reshape((k, r, n)))


ROW = 1024
BIG = (("w_in", 2), ("w_out", 1), ("w_cq", 1), ("w_ck", 1), ("w_cv", 1), ("w_co", 2), ("w_up", 2), ("w_down", 1))
CONV = ("lru_conv_w", "ffn_conv_w")
REPLICATED = ("norm_mix_g", "b_forget", "lru_conv_b", "lru_w_a", "lru_b_a", "lru_w_x", "lru_b_x", "lru_lambda",
              "norm_cross_g", "norm_mem_g", "norm_ffn_g", "ffn_conv_b", "rel_bias", "final_norm_g")
WEIGHTS = ('norm_mix_g', 'w_in', 'b_forget', 'lru_conv_w', 'lru_conv_b', 'lru_w_a', 'lru_b_a', 'lru_w_x', 'lru_b_x',
           'lru_lambda', 'w_out', 'norm_cross_g', 'norm_mem_g', 'w_cq', 'w_ck', 'w_cv', 'w_co', 'norm_ffn_g', 'w_up',
           'ffn_conv_w', 'ffn_conv_b', 'w_down', 'rel_bias', 'final_norm_g')
INPUTS = ("x", "mem") + WEIGHTS + ("loss_target",) + tuple("m_" + n for n in WEIGHTS) + tuple("v_" + n for n in WEIGHTS)


def _round_up(n, m):
    return -(-n // m) * m


class _Packing:
    def __init__(self, entries):
        self.entries, self.off = entries, {}
        o = 0
        for name, shape in entries:
            self.off[name] = o
            o += _round_up(int(np.prod(shape)), ROW)
        self.used = o
        self.rows = _round_up(o // ROW, SUBLANES)

    def pack(self, arrays):
        parts = []
        for name, shape in self.entries:
            n = int(np.prod(shape))
            parts.append(jnp.pad(arrays[name].reshape(n), (0, _round_up(n, ROW) - n)))
        tail = self.rows * ROW - self.used
        if tail:
            parts.append(jnp.zeros((tail,), F32))
        return jnp.concatenate(parts).reshape(self.rows, ROW)

    def unpack(self, flat, lead=()):
        flat = flat.reshape(lead + (self.rows * ROW,))
        out = {}
        for name, shape in self.entries:
            n = int(np.prod(shape))
            out[name] = lax.slice_in_dim(flat, self.off[name], self.off[name] + n, axis=len(lead)).reshape(
                lead + tuple(shape))
        return out


def _to_shards(g, axis):
    r, c = g.shape
    if axis == 1:
        return g.reshape(4, r // 4, c)
    return g.reshape(r, 4, c // 4).transpose(1, 0, 2)


def _from_shards(s, axis):
    _, nl, r, c = s.shape
    if axis == 1:
        return s.transpose(1, 0, 2, 3).reshape(nl, 4 * r, c)
    return s.transpose(1, 2, 0, 3).reshape(nl, r, 4 * c)


def _proj_blocks():
    blocks = []
    for mixer in (MIX_SB, MIX_FOX, MIX_DIL):
        for p in range(2):
            blocks += [ORIG_COL[mixer] + part * 2 * LANES + p * LANES for part in range(3)]
    for hf in range(2):
        blocks += [ORIG_LRU_X + hf * LANES, ORIG_LRU_G + hf * LANES]
    return blocks


def _pad_w_in(w):
    parts = [w[..., s:s + LANES] for s in _proj_blocks()]
    parts += [w[..., 1536:1540], jnp.zeros(w.shape[:-1] + (PROJ_W - COL_F - N_HEADS,), w.dtype)]
    return jnp.concatenate(parts, axis=-1)


def _unpad_w_in(wp):
    blocks = _proj_blocks()
    order = sorted(range(len(blocks)), key=lambda i: blocks[i])
    parts = []
    for i in order:
        if blocks[i] == ORIG_COL[MIX_DIL]:
            parts.append(wp[..., COL_F:COL_F + N_HEADS])
        parts.append(wp[..., i * LANES:(i + 1) * LANES])
    return jnp.concatenate(parts, axis=-1)


def _block_diag(w):
    z = jnp.zeros((HEAD_DIM, HEAD_DIM), w.dtype)
    half = lambda a, b: jnp.concatenate([jnp.concatenate([a, z], 1), jnp.concatenate([z, b], 1)], 0)
    return jnp.stack([half(w[0], w[1]), half(w[2], w[3])])


def _block_diag_grad(d):
    return jnp.stack([d[0, :HEAD_DIM, :HEAD_DIM], d[0, HEAD_DIM:, HEAD_DIM:],
                      d[1, :HEAD_DIM, :HEAD_DIM], d[1, HEAD_DIM:, HEAD_DIM:]])


def _fox_layouts(cum, nb, seq):
    tk = min(ATT_TILE, seq)
    col = cum.reshape(nb, 2, 2, seq).transpose(0, 1, 3, 2)
    row = cum.reshape(nb, 2, 2, seq // tk, tk).transpose(0, 1, 3, 2, 4)
    return col, row


def _layer_params(w, l, nb):
    lru_vec = jnp.concatenate([w["lru_conv_w"][l], w["lru_conv_b"][l][None], w["lru_b_a"][l][None],
                               w["lru_b_x"][l][None], w["lru_lambda"][l][None]], axis=0)
    ffn_cw = jnp.concatenate([w["ffn_conv_w"][l], w["ffn_conv_b"][l][None],
                              jnp.zeros((SUBLANES - 4, 2 * D_FF), F32)], axis=0)
    return dict(
        w_in=w["w_in_padded"][l], lru_vec=lru_vec,
        wa=_block_diag(w["lru_w_a"][l]).astype(BF16), wx=_block_diag(w["lru_w_x"][l]).astype(BF16),
        ffn_cw=ffn_cw, b_rows=jnp.tile(w["b_forget"][l], nb).reshape(nb * N_HEADS, 1))


def _layer_fwd(x, mem, w, lp, l, bias, nb):
    t, d = x.shape
    seq = t // nb
    tag = f"l{l}"
    sv = dict(x0=x)
    h = _rmsnorm(x, w["norm_mix_g"][l], name=tag + "_norm_mix")
    proj = _mm(h, lp["w_in"], name=tag + "_proj")
    mixed, ltot = _sb_attn_fwd(proj, lax.empty((t, d), F32), nb=nb, name=tag + "_sb_fwd")
    f_rows = proj[:, COL_F:COL_F + N_HEADS].reshape(nb, seq, N_HEADS).transpose(0, 2, 1).reshape(nb * N_HEADS, seq)
    cum_col, cum_row = _fox_layouts(_fox_gate_fwd(f_rows, lp["b_rows"]), nb, seq)
    mixed, lse_fox = _softmax_attn_fwd(proj, nb=nb, mode="fox", mixer=MIX_FOX, out_buf=mixed,
                                       extra=(cum_col, cum_row), name=tag + "_fox_fwd")
    mixed, lse_dil = _softmax_attn_fwd(proj, nb=nb, mode="dil", mixer=MIX_DIL, out_buf=mixed, extra=(bias,),
                                       name=tag + "_dil_fwd")
    mixed = _lru_fwd(proj, lp["lru_vec"], lp["wa"], lp["wx"], mixed, nb=nb, name=tag + "_lru_fwd")
    x1 = _mm(mixed, w["w_out"][l], res=x, name=tag + "_out")
    hq = _rmsnorm(x1, w["norm_cross_g"][l], name=tag + "_norm_cross")
    memn = _rmsnorm(mem, w["norm_mem_g"][l], name=tag + "_norm_mem")
    q = _mm(hq, w["w_cq"][l], name=tag + "_cq")
    k = _mm(memn, w["w_ck"][l], name=tag + "_ck")
    v = _mm(memn, w["w_cv"][l], name=tag + "_cv")
    oc, lse_c = _softmax_attn_fwd((q, k, v), nb=nb, mode="cross", name=tag + "_cross_fwd")
    x2 = _mm(oc, w["w_co"][l], res=x1, name=tag + "_co")
    hn = _rmsnorm(x2, w["norm_ffn_g"][l], name=tag + "_norm_ffn")
    hf = _mm(hn, w["w_up"][l], name=tag + "_up")
    act = _ffn_act(hf, lp["ffn_cw"], seq=seq, name=tag + "_ffn_act")
    x3 = _mm(act, w["w_down"][l], res=x2, name=tag + "_down")
    sv.update(h=h, proj=proj, ltot=ltot, f_rows=f_rows, cum_col=cum_col, cum_row=cum_row, lse_fox=lse_fox,
              lse_dil=lse_dil, mixed=mixed, x1=x1, hq=hq, memn=memn, q=q, k=k, v=v, oc=oc, lse_c=lse_c, x2=x2,
              hn=hn, hf=hf, act=act)
    return x3, sv


def _layer_bwd(dx3, mem, sv, w, lp, l, bias, nb):
    t = dx3.shape[0]
    seq = t // nb
    tag = f"l{l}"
    g = {}
    g["w_down"] = _mm(sv["act"], dx3, ta=True, name=tag + "_dw_down")
    dact = _mm(dx3, w["w_down"][l], tb=True, name=tag + "_dact")
    dhfc, dcu, dcg = _ffn_act_bwd(sv["hf"], lp["ffn_cw"], dact, seq=seq, name=tag + "_ffn_act_bwd")
    dcw = jnp.concatenate([dcu, dcg], axis=1)
    g["ffn_conv_w"], g["ffn_conv_b"] = dcw[:3], dcw[3]
    dhf = _conv_transpose(dhfc, lp["ffn_cw"], seq=seq, name=tag + "_conv_t")
    g["w_up"] = _mm(sv["hn"], dhf, ta=True, name=tag + "_dw_up")
    dhn = _mm(dhf, w["w_up"][l], tb=True, name=tag + "_dhn")
    dx2, g["norm_ffn_g"] = _rmsnorm_bwd(dhn, sv["x2"], w["norm_ffn_g"][l], dx3, name=tag + "_norm_ffn_bwd")
    g["w_co"] = _mm(sv["oc"], dx2, ta=True, name=tag + "_dw_co")
    doc = _mm(dx2, w["w_co"][l], tb=True, name=tag + "_doc")
    dq, dk, dv = _softmax_attn_bwd((sv["q"], sv["k"], sv["v"]), sv["oc"], sv["lse_c"], doc, nb=nb, mode="cross",
                                   name=tag + "_cross_bwd")
    g["w_cq"] = _mm(sv["hq"], dq, ta=True, name=tag + "_dw_cq")
    g["w_ck"] = _mm(sv["memn"], dk, ta=True, name=tag + "_dw_ck")
    g["w_cv"] = _mm(sv["memn"], dv, ta=True, name=tag + "_dw_cv")
    dhq = _mm(dq, w["w_cq"][l], tb=True, name=tag + "_dhq")
    dmemn = _mm(dv, w["w_cv"][l], tb=True, res=_mm(dk, w["w_ck"][l], tb=True, name=tag + "_dmem_k"),
                name=tag + "_dmem_v")
    _, g["norm_mem_g"] = _rmsnorm_bwd(dmemn, mem, w["norm_mem_g"][l], None, name=tag + "_norm_mem_bwd")
    dx1, g["norm_cross_g"] = _rmsnorm_bwd(dhq, sv["x1"], w["norm_cross_g"][l], dx2, name=tag + "_norm_cross_bwd")
    mixed, proj = sv["mixed"], sv["proj"]
    g["w_out"] = _mm(mixed, dx1, ta=True, name=tag + "_dw_out")
    dmixed = _mm(dx1, w["w_out"][l], tb=True, name=tag + "_dmixed")
    dproj = _sb_attn_bwd(proj, sv["ltot"], dmixed, lax.empty((t, PROJ_W), F32), nb=nb, name=tag + "_sb_bwd")
    dproj, dcum_k, dcum_q = _softmax_attn_bwd(
        proj, mixed, sv["lse_fox"], dmixed, nb=nb, mode="fox", mixer=MIX_FOX, dbuf=dproj,
        extra=(sv["cum_col"], sv["cum_row"]), name=tag + "_fox_bwd")
    dcum = (dcum_k.transpose(0, 1, 3, 2, 4).reshape(nb * N_HEADS, seq)
            + dcum_q.transpose(0, 1, 3, 2).reshape(nb * N_HEADS, seq))
    df_rows, db = _fox_gate_bwd(dcum, sv["f_rows"], lp["b_rows"])
    g["b_forget"] = db[:N_HEADS, 0]
    df = df_rows.reshape(nb, N_HEADS, seq).transpose(0, 2, 1).reshape(t, N_HEADS)
    dproj, dbias = _softmax_attn_bwd(proj, mixed, sv["lse_dil"], dmixed, nb=nb, mode="dil", mixer=MIX_DIL,
                                     dbuf=dproj, extra=(bias,), name=tag + "_dil_bwd")
    dproj, dvec, dwa, dwx = _lru_bwd(proj, lp["lru_vec"], lp["wa"], lp["wx"], dmixed, dproj, nb=nb,
                                     name=tag + "_lru_bwd")
    g["lru_conv_w"], g["lru_conv_b"], g["lru_b_a"], g["lru_b_x"], g["lru_lambda"] = (
        dvec[0:4], dvec[4], dvec[5], dvec[6], dvec[7])
    g["lru_w_a"], g["lru_w_x"] = _block_diag_grad(dwa), _block_diag_grad(dwx)
    dproj = lax.dynamic_update_slice(dproj, jnp.pad(df, ((0, 0), (0, PROJ_W - COL_F - N_HEADS))), (0, COL_F))
    g["w_in_padded"] = _mm(sv["h"], dproj, ta=True, name=tag + "_dw_in")
    dh = _mm(dproj, lp["w_in"], tb=True, name=tag + "_dh")
    dx0, g["norm_mix_g"] = _rmsnorm_bwd(dh, sv["x0"], w["norm_mix_g"][l], dx1, name=tag + "_norm_mix_bwd")
    return dx0, g, dbias


def _big_grad_shards(g):
    out = []
    for n, axis in BIG:
        full = _unpad_w_in(g["w_in_padded"]) if n == "w_in" else g[n]
        out.append(_to_shards(full, axis))
    return out


def kernel(*args):
    a = dict(zip(INPUTS, args, strict=True))
    nb, seq, d = a["x"].shape
    depth = a["norm_mix_g"].shape[0]
    x = a["x"].reshape(nb * seq, d)
    mem = a["mem"].reshape(nb * a["mem"].shape[1], d)
    target = a["loss_target"].reshape(nb * seq, d)
    cx, cy, c = _mesh_pos()
    chip = 2 * cx + cy
    pos = jnp.stack([chip, c]).astype(jnp.int32)

    slots = []
    for n, _ in BIG:
        own = a[n].astype(BF16)[None]
        slots.append(lax.dynamic_update_slice(lax.empty((4,) + own.shape[1:], BF16), own, (chip,) + (0,) * (own.ndim - 1)))
    gathered = _gather_weights(slots, name="gather_weights")
    w = {n: a[n] for n in REPLICATED}
    for (n, axis), buf in zip(BIG, gathered):
        w[n] = _from_shards(buf, axis)
    w["w_in_padded"] = _pad_w_in(w["w_in"])
    cpk = _Packing([(n, a[n].shape) for n in CONV])
    conv = cpk.unpack(_chip_bcast(cpk.pack({n: a[n] for n in CONV}), name="gather_conv"), lead=(4,))
    for n in CONV:
        w[n] = jnp.moveaxis(conv[n], 0, 2).reshape(a[n].shape[:2] + (4 * a[n].shape[2],))

    bias = _dil_bias(w["rel_bias"], seq)
    lps = [_layer_params(w, l, nb) for l in range(depth)]
    saved = []
    for l in range(depth):
        x, sv = _layer_fwd(x, mem, w, lps[l], l, bias, nb)
        saved.append(sv)
    loss, dx, dg_final = _loss_head(x, w["final_norm_g"], target)
    small_g = [None] * depth
    dbias = None
    g_shard = [lax.empty(a[n].shape, F32) for n, _ in BIG]
    for l in reversed(range(depth)):
        dx, g, db = _layer_bwd(dx, mem, saved[l], w, lps[l], l, bias, nb)
        dbias = db if dbias is None else dbias + db
        small_g[l] = g
        full = _big_grad_shards(g)
        from_sibling = _reduce_sibling(full, name=f"l{l}_reduce_sibling")
        partial = [_add_own_half(f, r, pos, name=f"l{l}_reduce_add_{n}")
                   for f, r, (n, _) in zip(full, from_sibling, BIG)]
        others = _reduce_chips(partial, name=f"l{l}_reduce_chips")
        g_shard = [_sum_into(p, o, buf, pos, l, name=f"l{l}_reduce_sum_{n}")
                   for p, o, buf, (n, _) in zip(partial, others, g_shard, BIG)]
    g_shard = _share_halves(g_shard, name="reduce_share")
    out = {}
    for (n, _), gs in zip(BIG, g_shard):
        delta, new_m, new_v = _adamw(a[n], gs, a["m_" + n], a["v_" + n], name="adamw_" + n)
        out[n] = (gs, delta, new_m, new_v)

    grads = {n: jnp.stack([small_g[l][n] for l in range(depth)]) for n in REPLICATED + CONV
             if n not in ("rel_bias", "final_norm_g")}
    grads["rel_bias"] = _dil_bias_bwd(dbias, seq)
    grads["final_norm_g"] = dg_final
    grads["loss"] = loss.reshape(1)
    spk = _Packing([(n, grads[n].shape) for n in REPLICATED + CONV + ("loss",)])
    s_all = _chip_bcast(_sibling_pair(spk.pack(grads), name="small_sibling"), name="small_chips")
    total = spk.unpack(_sum_slots(s_all, name="small_sum"))
    for n in CONV:
        width = a[n].shape[2]
        total[n] = lax.dynamic_slice_in_dim(total[n], chip * width, width, axis=2)
    apk = _Packing([(n, a[n].shape) for n in REPLICATED + CONV])
    s_out = _adamw(*[apk.pack(src)[None] for src in (
        {n: a[n] for n in REPLICATED + CONV}, total, {n: a["m_" + n] for n in REPLICATED + CONV},
        {n: a["v_" + n] for n in REPLICATED + CONV})], name="adamw_small")
    s_delta, s_m, s_v = [apk.unpack(o[0]) for o in s_out]
    for n in REPLICATED + CONV:
        out[n] = (total[n], s_delta[n], s_m[n], s_v[n])

    return (total["loss"].reshape(()), dx.reshape(nb, seq, d), *[out[n][0] for n in WEIGHTS],
            *[out[n][1] for n in WEIGHTS], *[out[n][2] for n in WEIGHTS], *[out[n][3] for n in WEIGHTS])
```

```python
import math

import numpy as np
import jax
import jax.numpy as jnp
from jax import lax
from jax.experimental import pallas as pl
from jax.experimental.pallas import tpu as pltpu

F32 = jnp.float32
BF16 = jnp.bfloat16

HEAD_DIM = 64
N_HEADS = 4
N_IN = 2820
D_FF = 2816
LRU_C = 8.0
EPS = 1e-6
NUM_BUCKETS = 32
MAX_DISTANCE = 2048
DILATED_PATTERNS = ((128, 1), (512, 4), (2048, 16))
ADAM_LR, ADAM_B1, ADAM_B2, ADAM_EPS, ADAM_WD, ADAM_STEP = 0.001, 0.9, 0.999, 1e-08, 0.01, 10

LANES = 128
SUBLANES = 8
VMEM_LIMIT = 48 * 1024 * 1024

PROJ_W = 3072
PAIR_W = 3 * LANES
LRU_W = 2 * LANES
COL_LRU = 6 * PAIR_W
COL_F = COL_LRU + 2 * LRU_W
MIX_SB, MIX_FOX, MIX_DIL, MIX_LRU = 0, 1, 2, 3
ORIG_COL = {MIX_SB: 0, MIX_FOX: 768, MIX_DIL: 1540}
ORIG_LRU_X, ORIG_LRU_G = 2308, 2564

ATT_TILE = 256
MASKED = -1e30
SCALE = HEAD_DIM ** -0.5

NT_DIMS = (((1,), (1,)), ((), ()))
TN_DIMS = (((0,), (0,)), ((), ()))

MESH = pl.DeviceIdType.MESH
ANY = pl.BlockSpec(memory_space=pl.ANY)


def _params(sem):
    return pltpu.CompilerParams(dimension_semantics=sem, vmem_limit_bytes=VMEM_LIMIT)


def _tile(n, target, unit=LANES):
    if n <= target:
        return n
    t = (target // unit) * unit
    while t > unit and n % t:
        t -= unit
    assert n % t == 0, (n, target, unit)
    return t


def _mm(a, b, *, ta=False, tb=False, res=None, name, ti=1024, tj=1408, tc=1408):
    m, kc = (a.shape[1], a.shape[0]) if ta else a.shape
    n = b.shape[0] if tb else b.shape[1]
    assert (b.shape[1] if tb else b.shape[0]) == kc
    ti, tj, tc = _tile(m, ti, LANES if ta else SUBLANES), _tile(n, tj), _tile(kc, tc, SUBLANES if ta and tb else LANES)
    nk = kc // tc
    dims = (((0 if ta else 1,), (1 if tb else 0,)), ((), ()))

    def body(*refs):
        if res is None:
            a_ref, b_ref, o_ref = refs[:3]
        else:
            a_ref, b_ref, r_ref, o_ref = refs[:4]
        part = lax.dot_general(a_ref[...].astype(BF16), b_ref[...].astype(BF16), dims, preferred_element_type=F32)
        if nk == 1:
            o_ref[...] = part if res is None else r_ref[...] + part
            return
        acc_ref = refs[-1]
        k = pl.program_id(2)

        @pl.when(k == 0)
        def _():
            acc_ref[...] = part

        @pl.when(k > 0)
        def _():
            acc_ref[...] += part

        @pl.when(k == nk - 1)
        def _():
            o_ref[...] = acc_ref[...] if res is None else r_ref[...] + acc_ref[...]

    a_spec = pl.BlockSpec((tc, ti), lambda i, j, k: (k, i)) if ta else pl.BlockSpec((ti, tc), lambda i, j, k: (i, k))
    b_spec = pl.BlockSpec((tj, tc), lambda i, j, k: (j, k)) if tb else pl.BlockSpec((tc, tj), lambda i, j, k: (k, j))
    o_spec = pl.BlockSpec((ti, tj), lambda i, j, k: (i, j))
    in_specs = [a_spec, b_spec] + ([] if res is None else [o_spec])
    args = (a, b) + (() if res is None else (res,))
    return pl.pallas_call(
        body, name=name, grid=(m // ti, n // tj, nk), in_specs=in_specs, out_specs=o_spec,
        out_shape=jax.ShapeDtypeStruct((m, n), F32),
        scratch_shapes=[] if nk == 1 else [pltpu.VMEM((ti, tj), F32)],
        compiler_params=_params(("parallel", "parallel", "arbitrary")))(*args)


def _xhat(x):
    return x * lax.rsqrt(jnp.mean(x * x, axis=-1, keepdims=True) + EPS)


def _norm_bwd_rows(dy, x, g):
    rstd = lax.rsqrt(jnp.mean(x * x, axis=-1, keepdims=True) + EPS)
    xh = x * rstd
    dxh = dy * g
    dx = rstd * (dxh - xh * jnp.mean(dxh * xh, axis=-1, keepdims=True))
    return dx, dy * xh


def _rmsnorm(x, g, *, name, rows=512):
    t, d = x.shape
    tr = _tile(t, rows, 2 * SUBLANES)

    def body(x_ref, g_ref, o_ref):
        o_ref[...] = (_xhat(x_ref[...]) * g_ref[...]).astype(BF16)

    return pl.pallas_call(
        body, name=name, grid=(t // tr,),
        in_specs=[pl.BlockSpec((tr, d), lambda i: (i, 0)), pl.BlockSpec((1, d), lambda i: (0, 0))],
        out_specs=pl.BlockSpec((tr, d), lambda i: (i, 0)), out_shape=jax.ShapeDtypeStruct((t, d), BF16),
        compiler_params=_params(("parallel",)))(x, g.reshape(1, d))


def _rmsnorm_bwd(dy, x, g, dres, *, name, rows=512):
    t, d = x.shape
    tr = _tile(t, rows, SUBLANES)

    def body(*refs):
        if dres is None:
            dy_ref, x_ref, g_ref, dx_ref, dg_ref = refs
        else:
            dy_ref, x_ref, g_ref, r_ref, dx_ref, dg_ref = refs
        dx, dgr = _norm_bwd_rows(dy_ref[...], x_ref[...], g_ref[...])
        dx_ref[...] = dx if dres is None else r_ref[...] + dx

        @pl.when(pl.program_id(0) == 0)
        def _():
            dg_ref[...] = jnp.zeros_like(dg_ref)

        dg_ref[...] += jnp.sum(dgr, axis=0, keepdims=True)

    row = pl.BlockSpec((tr, d), lambda i: (i, 0))
    vec = pl.BlockSpec((1, d), lambda i: (0, 0))
    in_specs = [row, row, vec] + ([] if dres is None else [row])
    args = (dy, x, g.reshape(1, d)) + (() if dres is None else (dres,))
    dx, dg = pl.pallas_call(
        body, name=name, grid=(t // tr,), in_specs=in_specs, out_specs=[row, vec],
        out_shape=[jax.ShapeDtypeStruct((t, d), F32), jax.ShapeDtypeStruct((1, d), F32)],
        compiler_params=_params(("arbitrary",)))(*args)
    return dx, dg.reshape(d)


def _loss_head(x, g, target, *, rows=512):
    t, d = x.shape
    tr = _tile(t, rows, SUBLANES)

    def body(x_ref, g_ref, t_ref, dx_ref, dg_ref, loss_ref):
        x_, g_ = x_ref[...], g_ref[...]
        err = _xhat(x_) * g_ - t_ref[...]
        dx, dgr = _norm_bwd_rows(err * (1.0 / d), x_, g_)
        dx_ref[...] = dx

        @pl.when(pl.program_id(0) == 0)
        def _():
            dg_ref[...] = jnp.zeros_like(dg_ref)
            loss_ref[...] = jnp.zeros_like(loss_ref)

        dg_ref[...] += jnp.sum(dgr, axis=0, keepdims=True)
        loss_ref[...] += 0.5 * jnp.sum(jnp.mean(err * err, axis=-1, keepdims=True), axis=0, keepdims=True)

    row = pl.BlockSpec((tr, d), lambda i: (i, 0))
    vec = pl.BlockSpec((1, d), lambda i: (0, 0))
    one = pl.BlockSpec((1, 1), lambda i: (0, 0))
    dx, dg, loss = pl.pallas_call(
        body, name="loss_head", grid=(t // tr,), in_specs=[row, vec, row], out_specs=[row, vec, one],
        out_shape=[jax.ShapeDtypeStruct((t, d), F32), jax.ShapeDtypeStruct((1, d), F32),
                   jax.ShapeDtypeStruct((1, 1), F32)],
        compiler_params=_params(("arbitrary",)))(x, g.reshape(1, d), target)
    return loss.reshape(()), dx, dg.reshape(d)


def _head_masks(shape):
    lane = lax.broadcasted_iota(jnp.int32, shape, len(shape) - 1)
    return lane < HEAD_DIM, lane >= HEAD_DIM


def _split_heads(x):
    m0, m1 = _head_masks(x.shape)
    zero = jnp.zeros_like(x)
    return jnp.where(m0, x, zero), jnp.where(m1, x, zero)


def _lane_pair(a0, a1, rows):
    m0, _ = _head_masks((rows, LANES))
    return jnp.where(m0, a0, a1)


def _qkv_readers(refs, packed):
    if packed:
        (r,) = refs
        return tuple((lambda r0, n, s=s: r[pl.ds(r0, n), s * LANES:(s + 1) * LANES]) for s in range(3))
    return tuple((lambda r0, n, ref=ref: ref[pl.ds(r0, n), :]) for ref in refs)


def _pair_spec(seq, col0, width=LANES):
    return pl.BlockSpec((seq, width), lambda p, b: (b, col0 + p))


def _fox_specs(seq, nk, tk):
    return [pl.BlockSpec((None, None, seq, 2), lambda p, b: (b, p, 0, 0)),
            pl.BlockSpec((None, None, nk, 2, tk), lambda p, b: (b, p, 0, 0, 0))]


def _softmax_attn_fwd(src, *, nb, mode, mixer=None, out_buf=None, extra=(), name):
    packed = mode != "cross"
    n_src = 1 if packed else 3
    seq_q = (src if packed else src[0]).shape[0] // nb
    seq_k = seq_q if packed else src[1].shape[0] // nb
    tq, tk = min(ATT_TILE, seq_q), min(ATT_TILE, seq_k)
    nq, nk = seq_q // tq, seq_k // tk
    n_ex = len(extra)

    def body(*refs):
        q_at, k_at, v_at = _qkv_readers(refs[:n_src], packed)
        ex = refs[n_src:n_src + n_ex]
        o_ref, lse_ref = refs[-2:]

        def q_tile(i, _):
            r0 = pl.multiple_of(i * tq, tq)
            qm = _split_heads((q_at(r0, tq) * SCALE).astype(BF16))
            if mode == "fox":
                cq = ex[0][pl.ds(r0, tq), :]
                row = r0 + lax.broadcasted_iota(jnp.int32, (tq, tk), 0)

            def k_tile(j, carry, diagonal=False):
                m, l, acc = carry
                c0 = pl.multiple_of(j * tk, tk)
                kt = k_at(c0, tk).astype(BF16)
                vm = _split_heads(v_at(c0, tk).astype(BF16))
                if mode == "fox":
                    ck = ex[1][j]
                new_m, new_l, alphas, pv = [], [], [], None
                for h in range(2):
                    s = lax.dot_general(qm[h], kt, NT_DIMS, preferred_element_type=F32)
                    if mode == "fox":
                        s = s + cq[:, h:h + 1] - ck[h:h + 1, :]
                        if diagonal:
                            s = jnp.where((c0 + lax.broadcasted_iota(jnp.int32, (tq, tk), 1)) <= row, s, MASKED)
                    elif mode == "dil":
                        s = s + ex[0][h, i - j]
                    mh = jnp.maximum(m[h], jnp.max(s, axis=-1, keepdims=True))
                    p = jnp.exp(s - mh)
                    alpha = jnp.exp(m[h] - mh)
                    new_m.append(mh)
                    new_l.append(alpha * l[h] + jnp.sum(p, axis=-1, keepdims=True))
                    alphas.append(alpha)
                    d = jnp.dot(p.astype(BF16), vm[h], preferred_element_type=F32)
                    pv = d if pv is None else pv + d
                acc = acc * _lane_pair(alphas[0], alphas[1], tq) + pv
                return tuple(new_m), tuple(new_l), acc

            init = ((jnp.full((tq, 1), MASKED, F32),) * 2, (jnp.zeros((tq, 1), F32),) * 2,
                    jnp.zeros((tq, LANES), F32))
            if mode == "fox":
                m, l, acc = k_tile(i, lax.fori_loop(0, i, k_tile, init), True)
            else:
                m, l, acc = lax.fori_loop(0, i + 1 if packed else nk, k_tile, init)
            o_ref[pl.ds(r0, tq), :] = acc / _lane_pair(l[0], l[1], tq)
            lse_ref[pl.ds(r0, tq), :] = _lane_pair(m[0] + jnp.log(l[0]), m[1] + jnp.log(l[1]), tq)
            return 0

        lax.fori_loop(0, nq, q_tile, 0)

    lse_shape = jax.ShapeDtypeStruct((nb * seq_q, 2 * LANES), F32)
    if packed:
        in_specs, args = [_pair_spec(seq_q, 2 * mixer, PAIR_W)], [src]
        in_specs += _fox_specs(seq_q, nk, tk) if mode == "fox" else [
            pl.BlockSpec((None, 2, nq, tq, tk), lambda p, b: (p, 0, 0, 0, 0))]
        args += list(extra) + [out_buf]
        in_specs.append(ANY)
        out_specs = [_pair_spec(seq_q, 2 * mixer), _pair_spec(seq_q, 0)]
        out_shape = [jax.ShapeDtypeStruct(out_buf.shape, F32), lse_shape]
        aliases = {len(args) - 1: 0}
    else:
        in_specs = [_pair_spec(seq_q, 0), _pair_spec(seq_k, 0), _pair_spec(seq_k, 0)]
        args = list(src)
        out_specs = [_pair_spec(seq_q, 0), _pair_spec(seq_q, 0)]
        out_shape = [lse_shape, lse_shape]
        aliases = {}
    return pl.pallas_call(
        body, name=name, grid=(2, nb), in_specs=in_specs, out_specs=out_specs, out_shape=out_shape,
        input_output_aliases=aliases, compiler_params=_params(("parallel", "arbitrary")))(*args)


def _softmax_attn_bwd(src, o, lse, do, *, nb, mode, mixer=None, dbuf=None, extra=(), name):
    packed = mode != "cross"
    n_src = 1 if packed else 3
    seq_q = (src if packed else src[0]).shape[0] // nb
    seq_k = seq_q if packed else src[1].shape[0] // nb
    tq, tk = min(ATT_TILE, seq_q), min(ATT_TILE, seq_k)
    nq, nk = seq_q // tq, seq_k // tk
    n_ex = len(extra)
    n_in = n_src + 3 + n_ex + (1 if packed else 0)

    def body(*refs):
        q_at, k_at, v_at = _qkv_readers(refs[:n_src], packed)
        o_ref, lse_ref, do_ref = refs[n_src:n_src + 3]
        ex = refs[n_src + 3:n_src + 3 + n_ex]
        outs = refs[n_in:]
        if packed:
            d_ref = outs[0]
            dq_w = lambda r0, val: d_ref.__setitem__((pl.ds(r0, tq), slice(0, LANES)), val)
            dk_ref = d_ref.at[:, LANES:2 * LANES]
            dv_ref = d_ref.at[:, 2 * LANES:3 * LANES]
        else:
            dq_ref, dk_ref, dv_ref = outs[:3]
            dq_w = lambda r0, val: dq_ref.__setitem__((pl.ds(r0, tq), slice(None)), val)
        dk_ref[...] = jnp.zeros((seq_k, LANES), F32)
        dv_ref[...] = jnp.zeros((seq_k, LANES), F32)
        if mode == "fox":
            dcum_ref, dcq_ref = outs[-2:]
            dcum_ref[...] = jnp.zeros_like(dcum_ref)
        if mode == "dil":
            dbias_ref = outs[-1]

            @pl.when(pl.program_id(1) == 0)
            def _():
                dbias_ref[...] = jnp.zeros_like(dbias_ref)

        def q_tile(i, _):
            r0 = pl.multiple_of(i * tq, tq)
            qm = _split_heads((q_at(r0, tq) * SCALE).astype(BF16))
            do_f = do_ref[pl.ds(r0, tq), :]
            dom = _split_heads(do_f.astype(BF16))
            dd = _split_heads(do_f * o_ref[pl.ds(r0, tq), :])
            delta = [jnp.sum(dd[h], axis=-1, keepdims=True) for h in range(2)]
            lse_t = lse_ref[pl.ds(r0, tq), :]
            lse_h = [lse_t[:, 0:1], lse_t[:, HEAD_DIM:HEAD_DIM + 1]]
            if mode == "fox":
                cq = ex[0][pl.ds(r0, tq), :]
                row = r0 + lax.broadcasted_iota(jnp.int32, (tq, tk), 0)

            def k_tile(j, carry, diagonal=False):
                dq, rs = carry
                c0 = pl.multiple_of(j * tk, tk)
                kt = k_at(c0, tk).astype(BF16)
                vt = v_at(c0, tk).astype(BF16)
                km = _split_heads(kt)
                if mode == "fox":
                    ck = ex[1][j]
                dk_t, dv_t, new_rs = None, None, []
                for h in range(2):
                    s = lax.dot_general(qm[h], kt, NT_DIMS, preferred_element_type=F32)
                    if mode == "fox":
                        s = s + cq[:, h:h + 1] - ck[h:h + 1, :]
                        if diagonal:
                            s = jnp.where((c0 + lax.broadcasted_iota(jnp.int32, (tq, tk), 1)) <= row, s, MASKED)
                    elif mode == "dil":
                        s = s + ex[0][h, i - j]
                    p = jnp.exp(s - lse_h[h])
                    dp = lax.dot_general(dom[h], vt, NT_DIMS, preferred_element_type=F32)
                    ds = p * (dp - delta[h])
                    dsb = ds.astype(BF16)
                    dq = dq + jnp.dot(dsb, km[h], preferred_element_type=F32)
                    a = lax.dot_general(dsb, qm[h], TN_DIMS, preferred_element_type=F32)
                    b = lax.dot_general(p.astype(BF16), dom[h], TN_DIMS, preferred_element_type=F32)
                    dk_t = a if dk_t is None else dk_t + a
                    dv_t = b if dv_t is None else dv_t + b
                    if mode == "fox":
                        dcum_ref[j, h:h + 1, :] -= jnp.sum(ds, axis=0, keepdims=True)
                        new_rs.append(rs[h] + jnp.sum(ds, axis=-1, keepdims=True))
                    elif mode == "dil":
                        dbias_ref[h, i - j] += ds
                dk_ref[pl.ds(c0, tk), :] += dk_t
                dv_ref[pl.ds(c0, tk), :] += dv_t
                return dq, (tuple(new_rs) if mode == "fox" else rs)

            zero = (jnp.zeros((tq, 1), F32),) * 2
            init = (jnp.zeros((tq, LANES), F32), zero)
            if mode == "fox":
                dq, rs = k_tile(i, lax.fori_loop(0, i, k_tile, init), True)
            else:
                dq, rs = lax.fori_loop(0, i + 1 if packed else nk, k_tile, init)
            dq_w(r0, dq * SCALE)
            if mode == "fox":
                dcq_ref[pl.ds(r0, tq), :] = jnp.where(lax.broadcasted_iota(jnp.int32, (tq, 2), 1) == 0, rs[0], rs[1])
            return 0

        lax.fori_loop(0, nq, q_tile, 0)

    if packed:
        in_specs = [_pair_spec(seq_q, 2 * mixer, PAIR_W), _pair_spec(seq_q, 2 * mixer), _pair_spec(seq_q, 0),
                    _pair_spec(seq_q, 2 * mixer)]
        args = [src, o, lse, do]
        out_specs = [_pair_spec(seq_q, 2 * mixer, PAIR_W)]
        out_shape = [jax.ShapeDtypeStruct(dbuf.shape, F32)]
        if mode == "fox":
            in_specs += _fox_specs(seq_q, nk, tk)
            out_specs += [_fox_specs(seq_q, nk, tk)[1], _fox_specs(seq_q, nk, tk)[0]]
            out_shape += [jax.ShapeDtypeStruct((nb, 2, nk, 2, tk), F32), jax.ShapeDtypeStruct((nb, 2, seq_q, 2), F32)]
        else:
            tiles = pl.BlockSpec((None, 2, nq, tq, tk), lambda p, b: (p, 0, 0, 0, 0))
            in_specs.append(tiles)
            out_specs.append(tiles)
            out_shape.append(jax.ShapeDtypeStruct((2, 2, nq, tq, tk), F32))
        args += list(extra) + [dbuf]
        in_specs.append(ANY)
        aliases = {len(args) - 1: 0}
    else:
        sq, sk = _pair_spec(seq_q, 0), _pair_spec(seq_k, 0)
        in_specs, args = [sq, sk, sk, sq, sq, sq], list(src) + [o, lse, do]
        out_specs = [sq, sk, sk]
        out_shape = [jax.ShapeDtypeStruct((nb * seq_q, 2 * LANES), F32)] + [
            jax.ShapeDtypeStruct((nb * seq_k, 2 * LANES), F32)] * 2
        aliases = {}
    return pl.pallas_call(
        body, name=name, grid=(2, nb), in_specs=in_specs, out_specs=out_specs, out_shape=out_shape,
        input_output_aliases=aliases, compiler_params=_params(("parallel", "arbitrary")))(*args)


def _log_sigmoid(z):
    return jnp.minimum(z, 0.0) - jnp.log(1.0 + jnp.exp(-jnp.abs(z)))


def _split_bf16(x):
    hi = x.astype(BF16)
    return hi, (x - hi.astype(F32)).astype(BF16)


def _tri(n, fn):
    r = lax.broadcasted_iota(jnp.int32, (n, n), 0)
    c = lax.broadcasted_iota(jnp.int32, (n, n), 1)
    return jnp.where(fn(r, c), 1.0, 0.0).astype(BF16)


def _sb_attn_fwd(proj, out_buf, *, nb, name):
    seq = proj.shape[0] // nb
    tq = tk = min(ATT_TILE, seq)
    nq = seq // tq

    def body(qkv_ref, _, o_ref, lt_ref):
        q_at, k_at, v_at = _qkv_readers((qkv_ref,), True)
        after = _tri(tk, lambda r, c: r > c)

        def q_tile(i, _):
            r0 = pl.multiple_of(i * tq, tq)
            qm = _split_heads((q_at(r0, tq) * SCALE).astype(BF16))
            row = r0 + lax.broadcasted_iota(jnp.int32, (tq, tk), 0)

            def k_tile(j, carry, diagonal):
                c, acc = carry
                c0 = pl.multiple_of(j * tk, tk)
                kt = k_at(c0, tk).astype(BF16)
                vm = _split_heads(v_at(c0, tk).astype(BF16))
                if diagonal:
                    strict = (c0 + lax.broadcasted_iota(jnp.int32, (tq, tk), 1)) < row
                new_c = []
                for h in range(2):
                    z = lax.dot_general(qm[h], kt, NT_DIMS, preferred_element_type=F32)
                    ls = _log_sigmoid(z)
                    lk = ls - z
                    if diagonal:
                        lk = jnp.where(strict, lk, 0.0)
                    hi, lo = _split_bf16(lk)
                    sfx = (jnp.dot(hi, after, preferred_element_type=F32)
                           + jnp.dot(lo, after, preferred_element_type=F32))
                    att = jnp.exp(ls + sfx + c[h])
                    if diagonal:
                        att = jnp.where(strict, att, 0.0)
                    acc = acc + jnp.dot(att.astype(BF16), vm[h], preferred_element_type=F32)
                    new_c.append(c[h] + jnp.sum(lk, axis=-1, keepdims=True))
                return tuple(new_c), acc

            init = ((jnp.zeros((tq, 1), F32),) * 2, jnp.zeros((tq, LANES), F32))
            c, acc = lax.fori_loop(1, i + 1, lambda jj, cr: k_tile(i - jj, cr, False), k_tile(i, init, True))
            o_ref[pl.ds(r0, tq), :] = acc
            lt_ref[pl.ds(r0, tq), :] = _lane_pair(c[0], c[1], tq)
            return 0

        lax.fori_loop(0, nq, q_tile, 0)

    return pl.pallas_call(
        body, name=name, grid=(2, nb), in_specs=[_pair_spec(seq, 2 * MIX_SB, PAIR_W), ANY],
        out_specs=[_pair_spec(seq, 2 * MIX_SB), _pair_spec(seq, 0)],
        out_shape=[jax.ShapeDtypeStruct(out_buf.shape, F32), jax.ShapeDtypeStruct((nb * seq, 2 * LANES), F32)],
        input_output_aliases={1: 0}, compiler_params=_params(("parallel", "arbitrary")))(proj, out_buf)


def _sb_attn_bwd(proj, ltot, do, dbuf, *, nb, name):
    seq = proj.shape[0] // nb
    tq = tk = min(ATT_TILE, seq)
    nq = seq // tq

    def body(qkv_ref, lt_ref, do_ref, _, d_ref):
        q_at, k_at, v_at = _qkv_readers((qkv_ref,), True)
        upto = _tri(tk, lambda r, c: r <= c)
        before = _tri(tk, lambda r, c: r < c)
        dk_ref = d_ref.at[:, LANES:2 * LANES]
        dv_ref = d_ref.at[:, 2 * LANES:3 * LANES]
        dk_ref[...] = jnp.zeros((seq, LANES), F32)
        dv_ref[...] = jnp.zeros((seq, LANES), F32)

        def q_tile(i, _):
            r0 = pl.multiple_of(i * tq, tq)
            qm = _split_heads((q_at(r0, tq) * SCALE).astype(BF16))
            dom = _split_heads(do_ref[pl.ds(r0, tq), :].astype(BF16))
            lt_t = lt_ref[pl.ds(r0, tq), :]
            lt_h = [lt_t[:, 0:1], lt_t[:, HEAD_DIM:HEAD_DIM + 1]]
            row = r0 + lax.broadcasted_iota(jnp.int32, (tq, tk), 0)

            def k_tile(j, carry, diagonal):
                pc, qc, dq = carry
                c0 = pl.multiple_of(j * tk, tk)
                kt = k_at(c0, tk).astype(BF16)
                vt = v_at(c0, tk).astype(BF16)
                km = _split_heads(kt)
                if diagonal:
                    strict = (c0 + lax.broadcasted_iota(jnp.int32, (tq, tk), 1)) < row
                new_pc, new_qc, dk_t, dv_t = [], [], None, None
                for h in range(2):
                    z = lax.dot_general(qm[h], kt, NT_DIMS, preferred_element_type=F32)
                    ls = _log_sigmoid(z)
                    lk = ls - z
                    if diagonal:
                        lk = jnp.where(strict, lk, 0.0)
                    hi, lo = _split_bf16(lk)
                    pin = (jnp.dot(hi, upto, preferred_element_type=F32)
                           + jnp.dot(lo, upto, preferred_element_type=F32))
                    att = jnp.exp(ls + (lt_h[h] - pc[h] - pin))
                    if diagonal:
                        att = jnp.where(strict, att, 0.0)
                    da = lax.dot_general(dom[h], vt, NT_DIMS, preferred_element_type=F32)
                    dg = att * da
                    qx = qc[h] + jnp.dot(dg.astype(BF16), before, preferred_element_type=F32)
                    sig = jnp.exp(ls)
                    dz = dg * (1.0 - sig) - sig * qx
                    if diagonal:
                        dz = jnp.where(strict, dz, 0.0)
                    dzb = dz.astype(BF16)
                    dq = dq + jnp.dot(dzb, km[h], preferred_element_type=F32)
                    a = lax.dot_general(dzb, qm[h], TN_DIMS, preferred_element_type=F32)
                    b = lax.dot_general(att.astype(BF16), dom[h], TN_DIMS, preferred_element_type=F32)
                    dk_t = a if dk_t is None else dk_t + a
                    dv_t = b if dv_t is None else dv_t + b
                    new_pc.append(pc[h] + jnp.sum(lk, axis=-1, keepdims=True))
                    new_qc.append(qc[h] + jnp.sum(dg, axis=-1, keepdims=True))
                dk_ref[pl.ds(c0, tk), :] += dk_t
                dv_ref[pl.ds(c0, tk), :] += dv_t
                return tuple(new_pc), tuple(new_qc), dq

            zero = (jnp.zeros((tq, 1), F32),) * 2
            carry = lax.fori_loop(0, i, lambda j, cr: k_tile(j, cr, False), (zero, zero, jnp.zeros((tq, LANES), F32)))
            _, _, dq = k_tile(i, carry, True)
            d_ref[pl.ds(r0, tq), 0:LANES] = dq * SCALE
            return 0

        lax.fori_loop(0, nq, q_tile, 0)

    return pl.pallas_call(
        body, name=name, grid=(2, nb),
        in_specs=[_pair_spec(seq, 2 * MIX_SB, PAIR_W), _pair_spec(seq, 0), _pair_spec(seq, 2 * MIX_SB), ANY],
        out_specs=_pair_spec(seq, 2 * MIX_SB, PAIR_W), out_shape=jax.ShapeDtypeStruct(dbuf.shape, F32),
        input_output_aliases={3: 0}, compiler_params=_params(("parallel", "arbitrary")))(proj, ltot, do, dbuf)


def _lane_scan(x, reverse=False):
    n = x.shape[-1]
    lane = lax.broadcasted_iota(jnp.int32, x.shape, 1)
    k = 1
    while k < n:
        if reverse:
            x = x + jnp.where(lane < n - k, pltpu.roll(x, n - k, 1), 0.0)
        else:
            x = x + jnp.where(lane >= k, pltpu.roll(x, k, 1), 0.0)
        k *= 2
    return x


def _fox_gate_fwd(f_rows, b_rows):
    def body(f_ref, b_ref, o_ref):
        o_ref[...] = _lane_scan(_log_sigmoid(f_ref[...] + b_ref[...]))

    return pl.pallas_call(body, name="fox_gate_fwd", out_shape=jax.ShapeDtypeStruct(f_rows.shape, F32))(f_rows, b_rows)


def _fox_gate_bwd(dcum, f_rows, b_rows):
    def body(d_ref, f_ref, b_ref, df_ref, db_ref):
        z = f_ref[...] + b_ref[...]
        df = _lane_scan(d_ref[...], reverse=True) * jnp.exp(_log_sigmoid(-z))
        df_ref[...] = df
        rs = jnp.sum(df, axis=-1, keepdims=True)
        tot = rs
        for e in range(1, f_rows.shape[0] // N_HEADS):
            tot = tot + pltpu.roll(rs, e * N_HEADS, 0)
        db_ref[...] = tot

    return pl.pallas_call(
        body, name="fox_gate_bwd",
        out_shape=[jax.ShapeDtypeStruct(f_rows.shape, F32), jax.ShapeDtypeStruct((f_rows.shape[0], 1), F32)],
    )(dcum, f_rows, b_rows)


def _dil_tables(seq):
    t = min(ATT_TILE, seq)
    n = seq // t
    a = np.arange(t)
    d = (np.arange(n)[:, None, None] * t + a[None, :, None] - a[None, None, :]).astype(np.int64)
    count = np.zeros(d.shape, np.int64)
    for window, dil in DILATED_PATTERNS:
        count += (d >= 0) & (d % dil == 0) & (d // dil <= window // dil)
    nn = np.maximum(d, 0)
    max_exact = NUM_BUCKETS // 2
    nf = np.maximum(nn, 1).astype(np.float32)
    large = max_exact + (np.log(nf / np.float32(max_exact)) / np.float32(math.log(MAX_DISTANCE / max_exact))
                         * np.float32(NUM_BUCKETS - max_exact)).astype(np.int32)
    bucket = np.where(nn < max_exact, nn, np.minimum(large, NUM_BUCKETS - 1))
    bucket = np.where(count > 0, bucket, -1).astype(np.int32)
    logc = np.where(count > 0, np.log(np.maximum(count, 1)), MASKED).astype(np.float32)
    return bucket, logc


def _dil_bias(rel_bias, seq):
    bucket, logc = _dil_tables(seq)
    n, t, _ = bucket.shape

    def body(rb_ref, bk_ref, lc_ref, o_ref):
        h = pl.program_id(0) * 2 + pl.program_id(1)
        bk = bk_ref[...]
        out = lc_ref[...]
        for b in range(NUM_BUCKETS):
            out = jnp.where(bk == b, out + rb_ref[b, h], out)
        o_ref[...] = out

    full = pl.BlockSpec((n, t, t), lambda p, h: (0, 0, 0))
    return pl.pallas_call(
        body, name="dil_bias", grid=(2, 2),
        in_specs=[pl.BlockSpec(memory_space=pltpu.SMEM), full, full],
        out_specs=pl.BlockSpec((None, None, n, t, t), lambda p, h: (p, h, 0, 0, 0)),
        out_shape=jax.ShapeDtypeStruct((2, 2, n, t, t), F32),
        compiler_params=_params(("parallel", "parallel")))(rel_bias, jnp.asarray(bucket), jnp.asarray(logc))


def _dil_bias_bwd(dbias, seq):
    bucket, _ = _dil_tables(seq)
    n, t, _ = bucket.shape

    def body(d_ref, bk_ref, o_ref):
        bk = bk_ref[...]
        lane = lax.broadcasted_iota(jnp.int32, (1, LANES), 1)
        for b in range(NUM_BUCKETS):
            rowv = jnp.zeros((1, LANES), F32)
            for h in range(N_HEADS):
                s = jnp.sum(jnp.where(bk == b, d_ref[h // 2, h % 2], 0.0))
                rowv = jnp.where(lane == h, s, rowv)
            o_ref[b:b + 1, :] = rowv

    out = pl.pallas_call(body, name="dil_bias_bwd", out_shape=jax.ShapeDtypeStruct((NUM_BUCKETS, LANES), F32),
                         compiler_params=pltpu.CompilerParams(vmem_limit_bytes=VMEM_LIMIT))(dbias, jnp.asarray(bucket))
    return out[:, :N_HEADS]


def _shift_rows(x, k, row, fill=0.0):
    n = x.shape[0]
    if k > 0:
        return jnp.where(row >= k, pltpu.roll(x, k, 0), fill)
    return jnp.where(row < n + k, pltpu.roll(x, n + k, 0), fill)


def _row_scan(a, u, row, reverse=False):
    n = a.shape[0]
    k = 1
    while k < n:
        s = -k if reverse else k
        u = a * _shift_rows(u, s, row) + u
        a = a * _shift_rows(a, s, row, 1.0)
        k *= 2
    return u


def _sigmoid(x):
    return 1.0 / (1.0 + jnp.exp(-x))


def _gelu(g):
    return 0.5 * g * (1.0 + lax.erf(g * (2.0 ** -0.5)))


def _gelu_grad(g):
    return 0.5 * (1.0 + lax.erf(g * (2.0 ** -0.5))) + g * jnp.exp(-0.5 * g * g) * (1.0 / math.sqrt(2.0 * math.pi))


def _neg_expm1(x):
    small = -x * (1.0 + x * (0.5 + x * (1.0 / 6.0 + x * (1.0 / 24.0))))
    return jnp.where(x > -0.03, small, 1.0 - jnp.exp(x))


def _lru_core(x, vec, wa, wx, row):
    xs = [_shift_rows(x, 3 - j, row) if j < 3 else x for j in range(4)]
    xc = vec[4:5, :]
    for j in range(4):
        xc = xc + vec[j:j + 1, :] * xs[j]
    xcb = xc.astype(BF16)
    r = _sigmoid(jnp.dot(xcb, wa, preferred_element_type=F32) + vec[5:6, :])
    ig = _sigmoid(jnp.dot(xcb, wx, preferred_element_type=F32) + vec[6:7, :])
    lam = vec[7:8, :]
    sp = jnp.maximum(-lam, 0.0) - _log_sigmoid(jnp.abs(lam))
    la = -LRU_C * r * sp
    a = jnp.exp(la)
    mult = jnp.sqrt(_neg_expm1(2.0 * la))
    return xs, xc, xcb, r, ig, sp, la, a, mult


def _lru_specs(seq):
    xg = pl.BlockSpec((seq, LRU_W), lambda hf, b: (b, COL_LRU // LRU_W + hf))
    mix = pl.BlockSpec((seq, LANES), lambda hf, b: (b, 2 * MIX_LRU + hf))
    vec = pl.BlockSpec((SUBLANES, LANES), lambda hf, b: (0, hf))
    mat = pl.BlockSpec((None, LANES, LANES), lambda hf, b: (hf, 0, 0))
    return xg, mix, vec, mat


def _lru_fwd(proj, vec, wa, wx, out_buf, *, nb, name):
    seq = proj.shape[0] // nb

    def body(xg_ref, vec_ref, wa_ref, wx_ref, _, o_ref):
        row = lax.broadcasted_iota(jnp.int32, (seq, LANES), 0)
        _, xc, _, _, ig, _, _, a, mult = _lru_core(xg_ref[:, 0:LANES], vec_ref[...], wa_ref[...], wx_ref[...], row)
        h = _row_scan(a, mult * (ig * xc), row)
        o_ref[...] = h * _gelu(xg_ref[:, LANES:LRU_W])

    xg, mix, vecs, mat = _lru_specs(seq)
    return pl.pallas_call(
        body, name=name, grid=(2, nb), in_specs=[xg, vecs, mat, mat, ANY], out_specs=mix,
        out_shape=jax.ShapeDtypeStruct(out_buf.shape, F32), input_output_aliases={4: 0},
        compiler_params=_params(("parallel", "arbitrary")))(proj, vec, wa, wx, out_buf)


def _lru_bwd(proj, vec, wa, wx, dout, dbuf, *, nb, name):
    seq = proj.shape[0] // nb

    def body(xg_ref, vec_ref, wa_ref, wx_ref, do_ref, _, d_ref, dvec_ref, dwa_ref, dwx_ref):
        row = lax.broadcasted_iota(jnp.int32, (seq, LANES), 0)
        vec_, wa_, wx_ = vec_ref[...], wa_ref[...], wx_ref[...]
        xs, xc, xcb, r, ig, sp, la, a, mult = _lru_core(xg_ref[:, 0:LANES], vec_, wa_, wx_, row)
        h = _row_scan(a, mult * (ig * xc), row)
        gate, do = xg_ref[:, LANES:LRU_W], do_ref[...]
        d_ref[:, LANES:LRU_W] = do * h * _gelu_grad(gate)
        dh = do * _gelu(gate)
        gacc = _row_scan(_shift_rows(a, -1, row), dh, row, reverse=True)
        da = gacc * _shift_rows(h, 1, row)
        dmult = gacc * (ig * xc)
        dig = gacc * (mult * xc)
        dxc = gacc * (mult * ig)
        dla = da * a - dmult * (a * a) / mult
        dr = (-LRU_C) * sp * dla
        dsp = jnp.sum((-LRU_C) * r * dla, axis=0, keepdims=True)
        dpr = dr * r * (1.0 - r)
        dpi = dig * ig * (1.0 - ig)
        dprb, dpib = dpr.astype(BF16), dpi.astype(BF16)
        dxc = (dxc + lax.dot_general(dprb, wa_, NT_DIMS, preferred_element_type=F32)
               + lax.dot_general(dpib, wx_, NT_DIMS, preferred_element_type=F32))
        dx = vec_[3:4, :] * dxc
        for j in range(3):
            dx = dx + vec_[j:j + 1, :] * _shift_rows(dxc, -(3 - j), row)
        d_ref[:, 0:LANES] = dx

        @pl.when(pl.program_id(1) == 0)
        def _():
            dvec_ref[...] = jnp.zeros_like(dvec_ref)
            dwa_ref[...] = jnp.zeros_like(dwa_ref)
            dwx_ref[...] = jnp.zeros_like(dwx_ref)

        for j in range(4):
            dvec_ref[j:j + 1, :] += jnp.sum(dxc * xs[j], axis=0, keepdims=True)
        dvec_ref[4:5, :] += jnp.sum(dxc, axis=0, keepdims=True)
        dvec_ref[5:6, :] += jnp.sum(dpr, axis=0, keepdims=True)
        dvec_ref[6:7, :] += jnp.sum(dpi, axis=0, keepdims=True)
        lam = vec_[7:8, :]
        dvec_ref[7:8, :] += -dsp * _sigmoid(-lam)
        dwa_ref[...] += lax.dot_general(xcb, dprb, TN_DIMS, preferred_element_type=F32)
        dwx_ref[...] += lax.dot_general(xcb, dpib, TN_DIMS, preferred_element_type=F32)

    xg, mix, vecs, mat = _lru_specs(seq)
    return pl.pallas_call(
        body, name=name, grid=(2, nb), in_specs=[xg, vecs, mat, mat, mix, ANY], out_specs=[xg, vecs, mat, mat],
        out_shape=[jax.ShapeDtypeStruct(dbuf.shape, F32), jax.ShapeDtypeStruct((SUBLANES, 2 * LANES), F32),
                   jax.ShapeDtypeStruct((2, LANES, LANES), F32), jax.ShapeDtypeStruct((2, LANES, LANES), F32)],
        input_output_aliases={5: 0},
        compiler_params=_params(("parallel", "arbitrary")))(proj, vec, wa, wx, dout, dbuf)


FFN_ROWS = 256
FFN_COLS = 1408


def _with_halo(halo, x, k):
    xx = jnp.concatenate([halo, x], axis=0)
    return pltpu.roll(xx, k, 0)[SUBLANES:, :]


def _ffn_conv(x_ref, halo_ref, cw, pos):
    x, halo = x_ref[...], halo_ref[...]
    x1 = jnp.where(pos >= 1, _with_halo(halo, x, 1), 0.0)
    x2 = jnp.where(pos >= 2, _with_halo(halo, x, 2), 0.0)
    return cw[3:4, :] + cw[0:1, :] * x2 + cw[1:2, :] * x1 + cw[2:3, :] * x, x1, x2


def _ffn_specs(tm, tn, gate_off):
    prev = lambda i: jnp.maximum(i * (tm // SUBLANES) - 1, 0)
    up = pl.BlockSpec((tm, tn), lambda j, i: (i, j))
    gate = pl.BlockSpec((tm, tn), lambda j, i: (i, j + gate_off))
    up_h = pl.BlockSpec((SUBLANES, tn), lambda j, i: (prev(i), j))
    gate_h = pl.BlockSpec((SUBLANES, tn), lambda j, i: (prev(i), j + gate_off))
    cw_up = pl.BlockSpec((SUBLANES, tn), lambda j, i: (0, j))
    cw_gate = pl.BlockSpec((SUBLANES, tn), lambda j, i: (0, j + gate_off))
    return up, gate, up_h, gate_h, cw_up, cw_gate


def _ffn_act(hf, cw, *, seq, name):
    t, w2 = hf.shape
    w = w2 // 2
    tm, tn = _tile(seq, FFN_ROWS, SUBLANES), _tile(w, FFN_COLS)

    def body(u_ref, g_ref, uh_ref, gh_ref, cu_ref, cg_ref, o_ref):
        pos = (pl.program_id(1) * tm + lax.broadcasted_iota(jnp.int32, (tm, 1), 0)) % seq
        up, _, _ = _ffn_conv(u_ref, uh_ref, cu_ref[...], pos)
        gate, _, _ = _ffn_conv(g_ref, gh_ref, cg_ref[...], pos)
        o_ref[...] = (_gelu(gate) * up).astype(BF16)

    specs = _ffn_specs(tm, tn, w // tn)
    return pl.pallas_call(
        body, name=name, grid=(w // tn, t // tm), in_specs=list(specs), out_specs=specs[0],
        out_shape=jax.ShapeDtypeStruct((t, w), BF16),
        compiler_params=_params(("parallel", "parallel")))(hf, hf, hf, hf, cw, cw)


def _ffn_act_bwd(hf, cw, dact, *, seq, name):
    t, w2 = hf.shape
    w = w2 // 2
    tm, tn = _tile(seq, FFN_ROWS, SUBLANES), _tile(w, FFN_COLS)

    def body(u_ref, g_ref, uh_ref, gh_ref, cu_ref, cg_ref, da_ref, d_ref, dcu_ref, dcg_ref):
        pos = (pl.program_id(1) * tm + lax.broadcasted_iota(jnp.int32, (tm, 1), 0)) % seq
        up, u1, u2 = _ffn_conv(u_ref, uh_ref, cu_ref[...], pos)
        gate, g1, g2 = _ffn_conv(g_ref, gh_ref, cg_ref[...], pos)
        da = da_ref[...]
        d_up = da * _gelu(gate)
        d_gate = da * up * _gelu_grad(gate)
        d_ref[0] = d_up
        d_ref[1] = d_gate

        @pl.when(pl.program_id(1) == 0)
        def _():
            dcu_ref[...] = jnp.zeros_like(dcu_ref)
            dcg_ref[...] = jnp.zeros_like(dcg_ref)

        for ref, d, taps in ((dcu_ref, d_up, (u2, u1, u_ref[...])), (dcg_ref, d_gate, (g2, g1, g_ref[...]))):
            for j in range(3):
                ref[j:j + 1, :] += jnp.sum(d * taps[j], axis=0, keepdims=True)
            ref[3:4, :] += jnp.sum(d, axis=0, keepdims=True)

    specs = _ffn_specs(tm, tn, w // tn)
    tile, cwt = specs[0], specs[4]
    return pl.pallas_call(
        body, name=name, grid=(w // tn, t // tm), in_specs=list(specs) + [tile],
        out_specs=[pl.BlockSpec((2, tm, tn), lambda j, i: (0, i, j)), cwt, cwt],
        out_shape=[jax.ShapeDtypeStruct((2, t, w), F32), jax.ShapeDtypeStruct((SUBLANES, w), F32),
                   jax.ShapeDtypeStruct((SUBLANES, w), F32)],
        compiler_params=_params(("parallel", "arbitrary")))(hf, hf, hf, hf, cw, cw, dact)


def _conv_transpose(d, cw, *, seq, name):
    _, t, w = d.shape
    tm, tn = _tile(seq, FFN_ROWS, SUBLANES), _tile(w, FFN_COLS)
    last = t // SUBLANES - 1
    ncol = w // tn

    def body(d_ref, nx_ref, cw_ref, o_ref):
        pos = (pl.program_id(2) * tm + lax.broadcasted_iota(jnp.int32, (tm, 1), 0)) % seq
        x, cw_ = d_ref[...], cw_ref[...]
        xx = jnp.concatenate([x, nx_ref[...]], axis=0)
        n = tm + SUBLANES
        x1 = jnp.where(pos < seq - 1, pltpu.roll(xx, n - 1, 0)[:tm, :], 0.0)
        x2 = jnp.where(pos < seq - 2, pltpu.roll(xx, n - 2, 0)[:tm, :], 0.0)
        o_ref[...] = (cw_[2:3, :] * x + cw_[1:2, :] * x1 + cw_[0:1, :] * x2).astype(BF16)

    tile = pl.BlockSpec((None, tm, tn), lambda h, j, i: (h, i, j))
    nxt = pl.BlockSpec((None, SUBLANES, tn), lambda h, j, i: (h, jnp.minimum((i + 1) * (tm // SUBLANES), last), j))
    cws = pl.BlockSpec((SUBLANES, tn), lambda h, j, i: (0, h * ncol + j))
    return pl.pallas_call(
        body, name=name, grid=(2, ncol, t // tm), in_specs=[tile, nxt, cws],
        out_specs=pl.BlockSpec((tm, tn), lambda h, j, i: (i, h * ncol + j)),
        out_shape=jax.ShapeDtypeStruct((t, 2 * w), BF16),
        compiler_params=_params(("parallel", "parallel", "parallel")))(d, d, cw)


def _adamw(w, g, m, v, *, name, rows=256):
    nl, r, c = w.shape
    tr = _tile(r, rows, SUBLANES)

    def body(w_ref, g_ref, m_ref, v_ref, d_ref, nm_ref, nv_ref):
        g_ = g_ref[...]
        nm = ADAM_B1 * m_ref[...] + (1.0 - ADAM_B1) * g_
        nv = ADAM_B2 * v_ref[...] + (1.0 - ADAM_B2) * (g_ * g_)
        m_hat = nm / (1.0 - ADAM_B1 ** ADAM_STEP)
        v_hat = nv / (1.0 - ADAM_B2 ** ADAM_STEP)
        d_ref[...] = -ADAM_LR * (m_hat / (jnp.sqrt(v_hat) + ADAM_EPS) + ADAM_WD * w_ref[...])
        nm_ref[...] = nm
        nv_ref[...] = nv

    spec = pl.BlockSpec((None, tr, c), lambda l, i: (l, i, 0))
    shape = jax.ShapeDtypeStruct((nl, r, c), F32)
    return pl.pallas_call(body, name=name, grid=(nl, r // tr), in_specs=[spec] * 4, out_specs=[spec] * 3,
                          out_shape=[shape] * 3, compiler_params=_params(("parallel", "parallel")))(w, g, m, v)


def _mesh_pos():
    return lax.axis_index("x"), lax.axis_index("y"), lax.axis_index("c")


def _peers(x, y):
    chips = [(1 - x, y), (x, 1 - y), (1 - x, 1 - y)]
    return [(px, py, 2 * px + py) for px, py in chips]


def _remote(src, dst, send_sems, recv_sems, idx, to):
    return pltpu.make_async_remote_copy(src, dst, send_sems.at[idx], recv_sems.at[idx], device_id=to,
                                        device_id_type=MESH)


def _gather_weights(bufs, *, name):
    n = len(bufs)

    def body(*refs):
        ins, outs = refs[:n], refs[n:2 * n]
        send_sems, recv_sems = refs[2 * n:]
        x, y, c = _mesh_pos()
        me = 2 * x + y
        peers = _peers(x, y)
        started = []
        for i in range(n):
            for r, (px, py, _) in enumerate(peers):
                cp = _remote(ins[i].at[me, c], outs[i].at[me, c], send_sems, recv_sems, (i, r), (px, py, c))
                cp.start()
                started.append(cp)
        for i in range(n):
            for r, (px, py, k) in enumerate(peers):
                _remote(outs[i].at[k, c], outs[i].at[k, c], send_sems, recv_sems, (i, r), (px, py, c)).wait_recv()
                cp = _remote(outs[i].at[k, c], outs[i].at[k, c], send_sems, recv_sems, (i, 3 + r), (x, y, 1 - c))
                cp.start()
                started.append(cp)
        for i in range(n):
            for r, (_, _, k) in enumerate(peers):
                _remote(outs[i].at[k, 1 - c], outs[i].at[k, 1 - c], send_sems, recv_sems, (i, 3 + r),
                        (x, y, 1 - c)).wait_recv()
        for cp in started:
            cp.wait_send()

    return pl.pallas_call(
        body, name=name, in_specs=[ANY] * n, out_specs=[ANY] * n,
        out_shape=[jax.ShapeDtypeStruct(b.shape, b.dtype) for b in bufs],
        input_output_aliases={i: i for i in range(n)},
        scratch_shapes=[pltpu.SemaphoreType.DMA((n, 6)), pltpu.SemaphoreType.DMA((n, 6))])(*bufs)


def _reduce_sibling(gs, *, name):
    n = len(gs)

    def body(*refs):
        ins, outs = refs[:n], refs[n:2 * n]
        send_sems, recv_sems = refs[2 * n:]
        x, y, c = _mesh_pos()
        cps = []
        for i in range(n):
            h = gs[i].shape[1] // 2
            cp = _remote(ins[i].at[:, pl.ds((1 - c) * h, h), :], outs[i], send_sems, recv_sems, i, (x, y, 1 - c))
            cp.start()
            cps.append(cp)
        for cp in cps:
            cp.wait()

    return pl.pallas_call(
        body, name=name, in_specs=[ANY] * n, out_specs=[ANY] * n,
        out_shape=[jax.ShapeDtypeStruct((g.shape[0], g.shape[1] // 2, g.shape[2]), g.dtype) for g in gs],
        scratch_shapes=[pltpu.SemaphoreType.DMA((n,)), pltpu.SemaphoreType.DMA((n,))])(*gs)


def _reduce_chips(ps, *, name):
    n = len(ps)

    def body(*refs):
        ins, outs = refs[:n], refs[n:2 * n]
        send_sems, recv_sems = refs[2 * n:]
        x, y, c = _mesh_pos()
        cps = []
        for i in range(n):
            for r, (px, py, k) in enumerate(_peers(x, y)):
                cp = _remote(ins[i].at[k], outs[i].at[r], send_sems, recv_sems, (i, r), (px, py, c))
                cp.start()
                cps.append(cp)
        for cp in cps:
            cp.wait()

    return pl.pallas_call(
        body, name=name, in_specs=[ANY] * n, out_specs=[ANY] * n,
        out_shape=[jax.ShapeDtypeStruct((3,) + p.shape[1:], p.dtype) for p in ps],
        scratch_shapes=[pltpu.SemaphoreType.DMA((n, 3)), pltpu.SemaphoreType.DMA((n, 3))])(*ps)


def _share_halves(bufs, *, name):
    n = len(bufs)

    def body(*refs):
        ins, outs = refs[:n], refs[n:2 * n]
        send_sems, recv_sems = refs[2 * n:]
        x, y, c = _mesh_pos()
        cps = []
        for i in range(n):
            h = bufs[i].shape[1] // 2
            mine = pl.ds(c * h, h)
            cp = _remote(ins[i].at[:, mine, :], outs[i].at[:, mine, :], send_sems, recv_sems, i, (x, y, 1 - c))
            cp.start()
            cps.append(cp)
        for cp in cps:
            cp.wait()

    return pl.pallas_call(
        body, name=name, in_specs=[ANY] * n, out_specs=[ANY] * n,
        out_shape=[jax.ShapeDtypeStruct(b.shape, b.dtype) for b in bufs],
        input_output_aliases={i: i for i in range(n)},
        scratch_shapes=[pltpu.SemaphoreType.DMA((n,)), pltpu.SemaphoreType.DMA((n,))])(*bufs)


def _add_own_half(full, recv, pos, *, name, rows=256):
    k4, h, n = recv.shape
    tr = _tile(h, rows, 16)
    nblk = h // tr

    def body(pos_ref, a_ref, b_ref, o_ref):
        o_ref[...] = (a_ref[...] + b_ref[...]).astype(BF16)

    grid_spec = pltpu.PrefetchScalarGridSpec(
        num_scalar_prefetch=1, grid=(k4, nblk),
        in_specs=[pl.BlockSpec((None, tr, n), lambda k, i, pos_ref: (k, pos_ref[1] * nblk + i, 0)),
                  pl.BlockSpec((None, tr, n), lambda k, i, pos_ref: (k, i, 0))],
        out_specs=pl.BlockSpec((None, tr, n), lambda k, i, pos_ref: (k, i, 0)))
    return pl.pallas_call(body, name=name, grid_spec=grid_spec, out_shape=jax.ShapeDtypeStruct(recv.shape, BF16),
                          compiler_params=_params(("parallel", "parallel")))(pos, full, recv)


def _sum_into(own, others, buf, pos, layer, *, name, rows=256):
    _, h, n = own.shape
    tr = _tile(h, rows, 16)
    nblk = h // tr

    def body(pos_ref, own_ref, oth_ref, _, o_ref):
        acc = own_ref[...].astype(F32)
        for r in range(3):
            acc = acc + oth_ref[r].astype(F32)
        o_ref[...] = acc

    grid_spec = pltpu.PrefetchScalarGridSpec(
        num_scalar_prefetch=1, grid=(nblk,),
        in_specs=[pl.BlockSpec((None, tr, n), lambda i, pos_ref: (pos_ref[0], i, 0)),
                  pl.BlockSpec((3, tr, n), lambda i, pos_ref: (0, i, 0)), ANY],
        out_specs=pl.BlockSpec((None, tr, n), lambda i, pos_ref: (layer, pos_ref[1] * nblk + i, 0)))
    return pl.pallas_call(body, name=name, grid_spec=grid_spec, out_shape=jax.ShapeDtypeStruct(buf.shape, F32),
                          input_output_aliases={3: 0}, compiler_params=_params(("parallel",)))(pos, own, others, buf)


def _sibling_pair(buf, *, name):
    def body(src_ref, out_ref, send_sem, recv_sem, local_sem):
        x, y, c = _mesh_pos()
        local = pltpu.make_async_copy(src_ref, out_ref.at[c], local_sem)
        local.start()
        cp = pltpu.make_async_remote_copy(src_ref, out_ref.at[c], send_sem, recv_sem, device_id=(x, y, 1 - c),
                                          device_id_type=MESH)
        cp.start()
        cp.wait()
        local.wait()

    return pl.pallas_call(
        body, name=name, in_specs=[ANY], out_specs=ANY, out_shape=jax.ShapeDtypeStruct((2,) + buf.shape, buf.dtype),
        scratch_shapes=[pltpu.SemaphoreType.DMA, pltpu.SemaphoreType.DMA, pltpu.SemaphoreType.DMA])(buf)


def _chip_bcast(buf, *, name):
    def body(src_ref, out_ref, send_sems, recv_sems, local_sem):
        x, y, c = _mesh_pos()
        me = 2 * x + y
        local = pltpu.make_async_copy(src_ref, out_ref.at[me], local_sem)
        local.start()
        sends = []
        for r, (px, py, _) in enumerate(_peers(x, y)):
            cp = _remote(src_ref, out_ref.at[me], send_sems, recv_sems, r, (px, py, c))
            cp.start()
            sends.append(cp)
        for r, (px, py, k) in enumerate(_peers(x, y)):
            _remote(src_ref, out_ref.at[k], send_sems, recv_sems, r, (px, py, c)).wait_recv()
        for cp in sends:
            cp.wait_send()
        local.wait()

    return pl.pallas_call(
        body, name=name, in_specs=[ANY], out_specs=ANY, out_shape=jax.ShapeDtypeStruct((4,) + buf.shape, buf.dtype),
        scratch_shapes=[pltpu.SemaphoreType.DMA((3,)), pltpu.SemaphoreType.DMA((3,)), pltpu.SemaphoreType.DMA])(buf)


def _sum_slots(buf, *, name, rows=384):
    r, n = buf.shape[-2:]
    k = int(np.prod(buf.shape[:-2]))
    tr = _tile(r, rows, SUBLANES)

    def body(b_ref, o_ref):
        acc = b_ref[0]
        for s in range(1, k):
            acc = acc + b_ref[s]
        o_ref[...] = acc

    return pl.pallas_call(
        body, name=name, grid=(r // tr,), in_specs=[pl.BlockSpec((k, tr, n), lambda i: (0, i, 0))],
        out_specs=pl.BlockSpec((tr, n), lambda i: (i, 0)), out_shape=jax.ShapeDtypeStruct((r, n), F32),
        compiler_params=_params(("parallel",)))(buf.reshape((k, r, n)))


ROW = 1024
BIG = (("w_in", 2), ("w_out", 1), ("w_cq", 1), ("w_ck", 1), ("w_cv", 1), ("w_co", 2), ("w_up", 2), ("w_down", 1))
CONV = ("lru_conv_w", "ffn_conv_w")
REPLICATED = ("norm_mix_g", "b_forget", "lru_conv_b", "lru_w_a", "lru_b_a", "lru_w_x", "lru_b_x", "lru_lambda",
              "norm_cross_g", "norm_mem_g", "norm_ffn_g", "ffn_conv_b", "rel_bias", "final_norm_g")
WEIGHTS = ('norm_mix_g', 'w_in', 'b_forget', 'lru_conv_w', 'lru_conv_b', 'lru_w_a', 'lru_b_a', 'lru_w_x', 'lru_b_x',
           'lru_lambda', 'w_out', 'norm_cross_g', 'norm_mem_g', 'w_cq', 'w_ck', 'w_cv', 'w_co', 'norm_ffn_g', 'w_up',
           'ffn_conv_w', 'ffn_conv_b', 'w_down', 'rel_bias', 'final_norm_g')
INPUTS = ("x", "mem") + WEIGHTS + ("loss_target",) + tuple("m_" + n for n in WEIGHTS) + tuple("v_" + n for n in WEIGHTS)


def _round_up(n, m):
    return -(-n // m) * m


class _Packing:
    def __init__(self, entries):
        self.entries, self.off = entries, {}
        o = 0
        for name, shape in entries:
            self.off[name] = o
            o += _round_up(int(np.prod(shape)), ROW)
        self.used = o
        self.rows = _round_up(o // ROW, SUBLANES)

    def pack(self, arrays):
        parts = []
        for name, shape in self.entries:
            n = int(np.prod(shape))
            parts.append(jnp.pad(arrays[name].reshape(n), (0, _round_up(n, ROW) - n)))
        tail = self.rows * ROW - self.used
        if tail:
            parts.append(jnp.zeros((tail,), F32))
        return jnp.concatenate(parts).reshape(self.rows, ROW)

    def unpack(self, flat, lead=()):
        flat = flat.reshape(lead + (self.rows * ROW,))
        out = {}
        for name, shape in self.entries:
            n = int(np.prod(shape))
            out[name] = lax.slice_in_dim(flat, self.off[name], self.off[name] + n, axis=len(lead)).reshape(
                lead + tuple(shape))
        return out


def _to_shards(g, axis):
    r, c = g.shape
    if axis == 1:
        return g.reshape(4, r // 4, c)
    return g.reshape(r, 4, c // 4).transpose(1, 0, 2)


def _from_shards(s, axis):
    _, nl, r, c = s.shape
    if axis == 1:
        return s.transpose(1, 0, 2, 3).reshape(nl, 4 * r, c)
    return s.transpose(1, 2, 0, 3).reshape(nl, r, 4 * c)


def _proj_blocks():
    blocks = []
    for mixer in (MIX_SB, MIX_FOX, MIX_DIL):
        for p in range(2):
            blocks += [ORIG_COL[mixer] + part * 2 * LANES + p * LANES for part in range(3)]
    for hf in range(2):
        blocks += [ORIG_LRU_X + hf * LANES, ORIG_LRU_G + hf * LANES]
    return blocks


def _pad_w_in(w):
    parts = [w[..., s:s + LANES] for s in _proj_blocks()]
    parts += [w[..., 1536:1540], jnp.zeros(w.shape[:-1] + (PROJ_W - COL_F - N_HEADS,), w.dtype)]
    return jnp.concatenate(parts, axis=-1)


def _unpad_w_in(wp):
    blocks = _proj_blocks()
    order = sorted(range(len(blocks)), key=lambda i: blocks[i])
    parts = []
    for i in order:
        if blocks[i] == ORIG_COL[MIX_DIL]:
            parts.append(wp[..., COL_F:COL_F + N_HEADS])
        parts.append(wp[..., i * LANES:(i + 1) * LANES])
    return jnp.concatenate(parts, axis=-1)


def _block_diag(w):
    z = jnp.zeros((HEAD_DIM, HEAD_DIM), w.dtype)
    half = lambda a, b: jnp.concatenate([jnp.concatenate([a, z], 1), jnp.concatenate([z, b], 1)], 0)
    return jnp.stack([half(w[0], w[1]), half(w[2], w[3])])


def _block_diag_grad(d):
    return jnp.stack([d[0, :HEAD_DIM, :HEAD_DIM], d[0, HEAD_DIM:, HEAD_DIM:],
                      d[1, :HEAD_DIM, :HEAD_DIM], d[1, HEAD_DIM:, HEAD_DIM:]])


def _fox_layouts(cum, nb, seq):
    tk = min(ATT_TILE, seq)
    col = cum.reshape(nb, 2, 2, seq).transpose(0, 1, 3, 2)
    row = cum.reshape(nb, 2, 2, seq // tk, tk).transpose(0, 1, 3, 2, 4)
    return col, row


def _layer_params(w, l, nb):
    lru_vec = jnp.concatenate([w["lru_conv_w"][l], w["lru_conv_b"][l][None], w["lru_b_a"][l][None],
                               w["lru_b_x"][l][None], w["lru_lambda"][l][None]], axis=0)
    ffn_cw = jnp.concatenate([w["ffn_conv_w"][l], w["ffn_conv_b"][l][None],
                              jnp.zeros((SUBLANES - 4, 2 * D_FF), F32)], axis=0)
    return dict(
        w_in=w["w_in_padded"][l], lru_vec=lru_vec,
        wa=_block_diag(w["lru_w_a"][l]).astype(BF16), wx=_block_diag(w["lru_w_x"][l]).astype(BF16),
        ffn_cw=ffn_cw, b_rows=jnp.tile(w["b_forget"][l], nb).reshape(nb * N_HEADS, 1))


def _layer_fwd(x, mem, w, lp, l, bias, nb):
    t, d = x.shape
    seq = t // nb
    tag = f"l{l}"
    sv = dict(x0=x)
    h = _rmsnorm(x, w["norm_mix_g"][l], name=tag + "_norm_mix")
    proj = _mm(h, lp["w_in"], name=tag + "_proj")
    mixed, ltot = _sb_attn_fwd(proj, lax.empty((t, d), F32), nb=nb, name=tag + "_sb_fwd")
    f_rows = proj[:, COL_F:COL_F + N_HEADS].reshape(nb, seq, N_HEADS).transpose(0, 2, 1).reshape(nb * N_HEADS, seq)
    cum_col, cum_row = _fox_layouts(_fox_gate_fwd(f_rows, lp["b_rows"]), nb, seq)
    mixed, lse_fox = _softmax_attn_fwd(proj, nb=nb, mode="fox", mixer=MIX_FOX, out_buf=mixed,
                                       extra=(cum_col, cum_row), name=tag + "_fox_fwd")
    mixed, lse_dil = _softmax_attn_fwd(proj, nb=nb, mode="dil", mixer=MIX_DIL, out_buf=mixed, extra=(bias,),
                                       name=tag + "_dil_fwd")
    mixed = _lru_fwd(proj, lp["lru_vec"], lp["wa"], lp["wx"], mixed, nb=nb, name=tag + "_lru_fwd")
    x1 = _mm(mixed, w["w_out"][l], res=x, name=tag + "_out")
    hq = _rmsnorm(x1, w["norm_cross_g"][l], name=tag + "_norm_cross")
    memn = _rmsnorm(mem, w["norm_mem_g"][l], name=tag + "_norm_mem")
    q = _mm(hq, w["w_cq"][l], name=tag + "_cq")
    k = _mm(memn, w["w_ck"][l], name=tag + "_ck")
    v = _mm(memn, w["w_cv"][l], name=tag + "_cv")
    oc, lse_c = _softmax_attn_fwd((q, k, v), nb=nb, mode="cross", name=tag + "_cross_fwd")
    x2 = _mm(oc, w["w_co"][l], res=x1, name=tag + "_co")
    hn = _rmsnorm(x2, w["norm_ffn_g"][l], name=tag + "_norm_ffn")
    hf = _mm(hn, w["w_up"][l], name=tag + "_up")
    act = _ffn_act(hf, lp["ffn_cw"], seq=seq, name=tag + "_ffn_act")
    x3 = _mm(act, w["w_down"][l], res=x2, name=tag + "_down")
    sv.update(h=h, proj=proj, ltot=ltot, f_rows=f_rows, cum_col=cum_col, cum_row=cum_row, lse_fox=lse_fox,
              lse_dil=lse_dil, mixed=mixed, x1=x1, hq=hq, memn=memn, q=q, k=k, v=v, oc=oc, lse_c=lse_c, x2=x2,
              hn=hn, hf=hf, act=act)
    return x3, sv


def _layer_bwd(dx3, mem, sv, w, lp, l, bias, nb):
    t = dx3.shape[0]
    seq = t // nb
    tag = f"l{l}"
    g = {}
    g["w_down"] = _mm(sv["act"], dx3, ta=True, name=tag + "_dw_down")
    dact = _mm(dx3, w["w_down"][l], tb=True, name=tag + "_dact")
    dhfc, dcu, dcg = _ffn_act_bwd(sv["hf"], lp["ffn_cw"], dact, seq=seq, name=tag + "_ffn_act_bwd")
    dcw = jnp.concatenate([dcu, dcg], axis=1)
    g["ffn_conv_w"], g["ffn_conv_b"] = dcw[:3], dcw[3]
    dhf = _conv_transpose(dhfc, lp["ffn_cw"], seq=seq, name=tag + "_conv_t")
    g["w_up"] = _mm(sv["hn"], dhf, ta=True, name=tag + "_dw_up")
    dhn = _mm(dhf, w["w_up"][l], tb=True, name=tag + "_dhn")
    dx2, g["norm_ffn_g"] = _rmsnorm_bwd(dhn, sv["x2"], w["norm_ffn_g"][l], dx3, name=tag + "_norm_ffn_bwd")
    g["w_co"] = _mm(sv["oc"], dx2, ta=True, name=tag + "_dw_co")
    doc = _mm(dx2, w["w_co"][l], tb=True, name=tag + "_doc")
    dq, dk, dv = _softmax_attn_bwd((sv["q"], sv["k"], sv["v"]), sv["oc"], sv["lse_c"], doc, nb=nb, mode="cross",
                                   name=tag + "_cross_bwd")
    g["w_cq"] = _mm(sv["hq"], dq, ta=True, name=tag + "_dw_cq")
    g["w_ck"] = _mm(sv["memn"], dk, ta=True, name=tag + "_dw_ck")
    g["w_cv"] = _mm(sv["memn"], dv, ta=True, name=tag + "_dw_cv")
    dhq = _mm(dq, w["w_cq"][l], tb=True, name=tag + "_dhq")
    dmemn = _mm(dv, w["w_cv"][l], tb=True, res=_mm(dk, w["w_ck"][l], tb=True, name=tag + "_dmem_k"),
                name=tag + "_dmem_v")
    _, g["norm_mem_g"] = _rmsnorm_bwd(dmemn, mem, w["norm_mem_g"][l], None, name=tag + "_norm_mem_bwd")
    dx1, g["norm_cross_g"] = _rmsnorm_bwd(dhq, sv["x1"], w["norm_cross_g"][l], dx2, name=tag + "_norm_cross_bwd")
    mixed, proj = sv["mixed"], sv["proj"]
    g["w_out"] = _mm(mixed, dx1, ta=True, name=tag + "_dw_out")
    dmixed = _mm(dx1, w["w_out"][l], tb=True, name=tag + "_dmixed")
    dproj = _sb_attn_bwd(proj, sv["ltot"], dmixed, lax.empty((t, PROJ_W), F32), nb=nb, name=tag + "_sb_bwd")
    dproj, dcum_k, dcum_q = _softmax_attn_bwd(
        proj, mixed, sv["lse_fox"], dmixed, nb=nb, mode="fox", mixer=MIX_FOX, dbuf=dproj,
        extra=(sv["cum_col"], sv["cum_row"]), name=tag + "_fox_bwd")
    dcum = (dcum_k.transpose(0, 1, 3, 2, 4).reshape(nb * N_HEADS, seq)
            + dcum_q.transpose(0, 1, 3, 2).reshape(nb * N_HEADS, seq))
    df_rows, db = _fox_gate_bwd(dcum, sv["f_rows"], lp["b_rows"])
    g["b_forget"] = db[:N_HEADS, 0]
    df = df_rows.reshape(nb, N_HEADS, seq).transpose(0, 2, 1).reshape(t, N_HEADS)
    dproj, dbias = _softmax_attn_bwd(proj, mixed, sv["lse_dil"], dmixed, nb=nb, mode="dil", mixer=MIX_DIL,
                                     dbuf=dproj, extra=(bias,), name=tag + "_dil_bwd")
    dproj, dvec, dwa, dwx = _lru_bwd(proj, lp["lru_vec"], lp["wa"], lp["wx"], dmixed, dproj, nb=nb,
                                     name=tag + "_lru_bwd")
    g["lru_conv_w"], g["lru_conv_b"], g["lru_b_a"], g["lru_b_x"], g["lru_lambda"] = (
        dvec[0:4], dvec[4], dvec[5], dvec[6], dvec[7])
    g["lru_w_a"], g["lru_w_x"] = _block_diag_grad(dwa), _block_diag_grad(dwx)
    dproj = lax.dynamic_update_slice(dproj, jnp.pad(df, ((0, 0), (0, PROJ_W - COL_F - N_HEADS))), (0, COL_F))
    g["w_in_padded"] = _mm(sv["h"], dproj, ta=True, name=tag + "_dw_in")
    dh = _mm(dproj, lp["w_in"], tb=True, name=tag + "_dh")
    dx0, g["norm_mix_g"] = _rmsnorm_bwd(dh, sv["x0"], w["norm_mix_g"][l], dx1, name=tag + "_norm_mix_bwd")
    return dx0, g, dbias


def _big_grad_shards(g):
    out = []
    for n, axis in BIG:
        full = _unpad_w_in(g["w_in_padded"]) if n == "w_in" else g[n]
        out.append(_to_shards(full, axis))
    return out


def kernel(*args):
    a = dict(zip(INPUTS, args, strict=True))
    nb, seq, d = a["x"].shape
    depth = a["norm_mix_g"].shape[0]
    x = a["x"].reshape(nb * seq, d)
    mem = a["mem"].reshape(nb * a["mem"].shape[1], d)
    target = a["loss_target"].reshape(nb * seq, d)
    cx, cy, c = _mesh_pos()
    chip = 2 * cx + cy
    pos = jnp.stack([chip, c]).astype(jnp.int32)

    slots = []
    for n, _ in BIG:
        own = a[n].astype(BF16)[None]
        slots.append(lax.dynamic_update_slice(lax.empty((4,) + own.shape[1:], BF16), own, (chip,) + (0,) * (own.ndim - 1)))
    gathered = _gather_weights(slots, name="gather_weights")
    w = {n: a[n] for n in REPLICATED}
    for (n, axis), buf in zip(BIG, gathered):
        w[n] = _from_shards(buf, axis)
    w["w_in_padded"] = _pad_w_in(w["w_in"])
    cpk = _Packing([(n, a[n].shape) for n in CONV])
    conv = cpk.unpack(_chip_bcast(cpk.pack({n: a[n] for n in CONV}), name="gather_conv"), lead=(4,))
    for n in CONV:
        w[n] = jnp.moveaxis(conv[n], 0, 2).reshape(a[n].shape[:2] + (4 * a[n].shape[2],))

    bias = _dil_bias(w["rel_bias"], seq)
    lps = [_layer_params(w, l, nb) for l in range(depth)]
    saved = []
    for l in range(depth):
        x, sv = _layer_fwd(x, mem, w, lps[l], l, bias, nb)
        saved.append(sv)
    loss, dx, dg_final = _loss_head(x, w["final_norm_g"], target)
    small_g = [None] * depth
    dbias = None
    g_shard = [lax.empty(a[n].shape, F32) for n, _ in BIG]
    for l in reversed(range(depth)):
        dx, g, db = _layer_bwd(dx, mem, saved[l], w, lps[l], l, bias, nb)
        dbias = db if dbias is None else dbias + db
        small_g[l] = g
        full = _big_grad_shards(g)
        from_sibling = _reduce_sibling(full, name=f"l{l}_reduce_sibling")
        partial = [_add_own_half(f, r, pos, name=f"l{l}_reduce_add_{n}")
                   for f, r, (n, _) in zip(full, from_sibling, BIG)]
        others = _reduce_chips(partial, name=f"l{l}_reduce_chips")
        g_shard = [_sum_into(p, o, buf, pos, l, name=f"l{l}_reduce_sum_{n}")
                   for p, o, buf, (n, _) in zip(partial, others, g_shard, BIG)]
    g_shard = _share_halves(g_shard, name="reduce_share")
    out = {}
    for (n, _), gs in zip(BIG, g_shard):
        delta, new_m, new_v = _adamw(a[n], gs, a["m_" + n], a["v_" + n], name="adamw_" + n)
        out[n] = (gs, delta, new_m, new_v)

    grads = {n: jnp.stack([small_g[l][n] for l in range(depth)]) for n in REPLICATED + CONV
             if n not in ("rel_bias", "final_norm_g")}
    grads["rel_bias"] = _dil_bias_bwd(dbias, seq)
    grads["final_norm_g"] = dg_final
    grads["loss"] = loss.reshape(1)
    spk = _Packing([(n, grads[n].shape) for n in REPLICATED + CONV + ("loss",)])
    s_all = _chip_bcast(_sibling_pair(spk.pack(grads), name="small_sibling"), name="small_chips")
    total = spk.unpack(_sum_slots(s_all, name="small_sum"))
    for n in CONV:
        width = a[n].shape[2]
        total[n] = lax.dynamic_slice_in_dim(total[n], chip * width, width, axis=2)
    apk = _Packing([(n, a[n].shape) for n in REPLICATED + CONV])
    s_out = _adamw(*[apk.pack(src)[None] for src in (
        {n: a[n] for n in REPLICATED + CONV}, total, {n: a["m_" + n] for n in REPLICATED + CONV},
        {n: a["v_" + n] for n in REPLICATED + CONV})], name="adamw_small")
    s_delta, s_m, s_v = [apk.unpack(o[0]) for o in s_out]
    for n in REPLICATED + CONV:
        out[n] = (total[n], s_delta[n], s_m[n], s_v[n])

    return (total["loss"].reshape(()), dx.reshape(nb, seq, d), *[out[n][0] for n in WEIGHTS],
            *[out[n][1] for n in WEIGHTS], *[out[n][2] for n in WEIGHTS], *[out[n][3] for n in WEIGHTS])
```

```python
import math

import numpy as np
import jax
import jax.numpy as jnp
from jax import lax
from jax.experimental import pallas as pl
from jax.experimental.pallas import tpu as pltpu

F32 = jnp.float32
BF16 = jnp.bfloat16

HEAD_DIM = 64
N_HEADS = 4
N_IN = 2820
D_FF = 2816
LRU_C = 8.0
EPS = 1e-6
NUM_BUCKETS = 32
MAX_DISTANCE = 2048
DILATED_PATTERNS = ((128, 1), (512, 4), (2048, 16))
ADAM_LR, ADAM_B1, ADAM_B2, ADAM_EPS, ADAM_WD, ADAM_STEP = 0.001, 0.9, 0.999, 1e-08, 0.01, 10

LANES = 128
SUBLANES = 8
VMEM_LIMIT = 48 * 1024 * 1024

PROJ_W = 3072
PAIR_W = 3 * LANES
LRU_W = 2 * LANES
COL_LRU = 6 * PAIR_W
COL_F = COL_LRU + 2 * LRU_W
MIX_SB, MIX_FOX, MIX_DIL, MIX_LRU = 0, 1, 2, 3
ORIG_COL = {MIX_SB: 0, MIX_FOX: 768, MIX_DIL: 1540}
ORIG_LRU_X, ORIG_LRU_G = 2308, 2564

ATT_TILE = 256
MASKED = -1e30
SCALE = HEAD_DIM ** -0.5

NT_DIMS = (((1,), (1,)), ((), ()))
TN_DIMS = (((0,), (0,)), ((), ()))

MESH = pl.DeviceIdType.MESH
ANY = pl.BlockSpec(memory_space=pl.ANY)


def _params(sem):
    return pltpu.CompilerParams(dimension_semantics=sem, vmem_limit_bytes=VMEM_LIMIT)


def _tile(n, target, unit=LANES):
    if n <= target:
        return n
    t = (target // unit) * unit
    while t > unit and n % t:
        t -= unit
    assert n % t == 0, (n, target, unit)
    return t


def _mm(a, b, *, ta=False, tb=False, res=None, col_shards=1, name, ti=1024, tj=1408, tc=1408):
    m, kc = (a.shape[1], a.shape[0]) if ta else a.shape
    n = b.shape[0] if tb else b.shape[1]
    assert (b.shape[1] if tb else b.shape[0]) == kc and n % col_shards == 0
    ti, tj, tc = (_tile(m, ti, LANES if ta else SUBLANES), _tile(n // col_shards, tj),
                  _tile(kc, tc, SUBLANES if ta and tb else LANES))
    per_shard = n // col_shards // tj
    nk = kc // tc
    dims = (((0 if ta else 1,), (1 if tb else 0,)), ((), ()))

    def body(*refs):
        if res is None:
            a_ref, b_ref, o_ref = refs[:3]
        else:
            a_ref, b_ref, r_ref, o_ref = refs[:4]
        part = lax.dot_general(a_ref[...].astype(BF16), b_ref[...].astype(BF16), dims, preferred_element_type=F32)
        if nk == 1:
            o_ref[...] = part if res is None else r_ref[...] + part
            return
        acc_ref = refs[-1]
        k = pl.program_id(2)

        @pl.when(k == 0)
        def _():
            acc_ref[...] = part

        @pl.when(k > 0)
        def _():
            acc_ref[...] += part

        @pl.when(k == nk - 1)
        def _():
            o_ref[...] = acc_ref[...] if res is None else r_ref[...] + acc_ref[...]

    a_spec = pl.BlockSpec((tc, ti), lambda i, j, k: (k, i)) if ta else pl.BlockSpec((ti, tc), lambda i, j, k: (i, k))
    b_spec = pl.BlockSpec((tj, tc), lambda i, j, k: (j, k)) if tb else pl.BlockSpec((tc, tj), lambda i, j, k: (k, j))
    o_spec = pl.BlockSpec((ti, tj), lambda i, j, k: (i, j))
    in_specs = [a_spec, b_spec] + ([] if res is None else [o_spec])
    args = (a, b) + (() if res is None else (res,))
    out_shape = jax.ShapeDtypeStruct((m, n), F32)
    if col_shards > 1:
        assert res is None
        o_spec = pl.BlockSpec((None, ti, tj), lambda i, j, k: (j // per_shard, i, j % per_shard))
        out_shape = jax.ShapeDtypeStruct((col_shards, m, n // col_shards), F32)
    return pl.pallas_call(
        body, name=name, grid=(m // ti, n // tj, nk), in_specs=in_specs, out_specs=o_spec,
        out_shape=out_shape,
        scratch_shapes=[] if nk == 1 else [pltpu.VMEM((ti, tj), F32)],
        compiler_params=_params(("parallel", "parallel", "arbitrary")))(*args)


def _xhat(x):
    return x * lax.rsqrt(jnp.mean(x * x, axis=-1, keepdims=True) + EPS)


def _norm_bwd_rows(dy, x, g):
    rstd = lax.rsqrt(jnp.mean(x * x, axis=-1, keepdims=True) + EPS)
    xh = x * rstd
    dxh = dy * g
    dx = rstd * (dxh - xh * jnp.mean(dxh * xh, axis=-1, keepdims=True))
    return dx, dy * xh


def _rmsnorm(x, g, *, name, rows=512):
    t, d = x.shape
    tr = _tile(t, rows, 2 * SUBLANES)

    def body(x_ref, g_ref, o_ref):
        o_ref[...] = (_xhat(x_ref[...]) * g_ref[...]).astype(BF16)

    return pl.pallas_call(
        body, name=name, grid=(t // tr,),
        in_specs=[pl.BlockSpec((tr, d), lambda i: (i, 0)), pl.BlockSpec((1, d), lambda i: (0, 0))],
        out_specs=pl.BlockSpec((tr, d), lambda i: (i, 0)), out_shape=jax.ShapeDtypeStruct((t, d), BF16),
        compiler_params=_params(("parallel",)))(x, g.reshape(1, d))


def _rmsnorm_bwd(dy, x, g, dres, *, name, rows=512):
    t, d = x.shape
    tr = _tile(t, rows, SUBLANES)

    def body(*refs):
        if dres is None:
            dy_ref, x_ref, g_ref, dx_ref, dg_ref = refs
        else:
            dy_ref, x_ref, g_ref, r_ref, dx_ref, dg_ref = refs
        dx, dgr = _norm_bwd_rows(dy_ref[...], x_ref[...], g_ref[...])
        dx_ref[...] = dx if dres is None else r_ref[...] + dx

        @pl.when(pl.program_id(0) == 0)
        def _():
            dg_ref[...] = jnp.zeros_like(dg_ref)

        dg_ref[...] += jnp.sum(dgr, axis=0, keepdims=True)

    row = pl.BlockSpec((tr, d), lambda i: (i, 0))
    vec = pl.BlockSpec((1, d), lambda i: (0, 0))
    in_specs = [row, row, vec] + ([] if dres is None else [row])
    args = (dy, x, g.reshape(1, d)) + (() if dres is None else (dres,))
    dx, dg = pl.pallas_call(
        body, name=name, grid=(t // tr,), in_specs=in_specs, out_specs=[row, vec],
        out_shape=[jax.ShapeDtypeStruct((t, d), F32), jax.ShapeDtypeStruct((1, d), F32)],
        compiler_params=_params(("arbitrary",)))(*args)
    return dx, dg.reshape(d)


def _loss_head(x, g, target, *, rows=512):
    t, d = x.shape
    tr = _tile(t, rows, SUBLANES)

    def body(x_ref, g_ref, t_ref, dx_ref, dg_ref, loss_ref):
        x_, g_ = x_ref[...], g_ref[...]
        err = _xhat(x_) * g_ - t_ref[...]
        dx, dgr = _norm_bwd_rows(err * (1.0 / d), x_, g_)
        dx_ref[...] = dx

        @pl.when(pl.program_id(0) == 0)
        def _():
            dg_ref[...] = jnp.zeros_like(dg_ref)
            loss_ref[...] = jnp.zeros_like(loss_ref)

        dg_ref[...] += jnp.sum(dgr, axis=0, keepdims=True)
        loss_ref[...] += 0.5 * jnp.sum(jnp.mean(err * err, axis=-1, keepdims=True), axis=0, keepdims=True)

    row = pl.BlockSpec((tr, d), lambda i: (i, 0))
    vec = pl.BlockSpec((1, d), lambda i: (0, 0))
    one = pl.BlockSpec((1, 1), lambda i: (0, 0))
    dx, dg, loss = pl.pallas_call(
        body, name="loss_head", grid=(t // tr,), in_specs=[row, vec, row], out_specs=[row, vec, one],
        out_shape=[jax.ShapeDtypeStruct((t, d), F32), jax.ShapeDtypeStruct((1, d), F32),
                   jax.ShapeDtypeStruct((1, 1), F32)],
        compiler_params=_params(("arbitrary",)))(x, g.reshape(1, d), target)
    return loss.reshape(()), dx, dg.reshape(d)


def _head_masks(shape):
    lane = lax.broadcasted_iota(jnp.int32, shape, len(shape) - 1)
    return lane < HEAD_DIM, lane >= HEAD_DIM


def _split_heads(x):
    m0, m1 = _head_masks(x.shape)
    zero = jnp.zeros_like(x)
    return jnp.where(m0, x, zero), jnp.where(m1, x, zero)


def _lane_pair(a0, a1, rows):
    m0, _ = _head_masks((rows, LANES))
    return jnp.where(m0, a0, a1)


def _qkv_readers(refs, packed):
    if packed:
        (r,) = refs
        return tuple((lambda r0, n, s=s: r[pl.ds(r0, n), s * LANES:(s + 1) * LANES]) for s in range(3))
    return tuple((lambda r0, n, ref=ref: ref[pl.ds(r0, n), :]) for ref in refs)


def _pair_spec(seq, col0, width=LANES):
    return pl.BlockSpec((seq, width), lambda p, b: (b, col0 + p))


def _fox_specs(seq, nk, tk):
    return [pl.BlockSpec((None, None, seq, 2), lambda p, b: (b, p, 0, 0)),
            pl.BlockSpec((None, None, nk, 2, tk), lambda p, b: (b, p, 0, 0, 0))]


def _softmax_attn_fwd(src, *, nb, mode, mixer=None, out_buf=None, extra=(), name):
    packed = mode != "cross"
    n_src = 1 if packed else 3
    seq_q = (src if packed else src[0]).shape[0] // nb
    seq_k = seq_q if packed else src[1].shape[0] // nb
    tq, tk = min(ATT_TILE, seq_q), min(ATT_TILE, seq_k)
    nq, nk = seq_q // tq, seq_k // tk
    n_ex = len(extra)

    def body(*refs):
        q_at, k_at, v_at = _qkv_readers(refs[:n_src], packed)
        ex = refs[n_src:n_src + n_ex]
        o_ref, lse_ref = refs[-2:]

        def q_tile(i, _):
            r0 = pl.multiple_of(i * tq, tq)
            qm = _split_heads((q_at(r0, tq) * SCALE).astype(BF16))
            if mode == "fox":
                cq = ex[0][pl.ds(r0, tq), :]
                row = r0 + lax.broadcasted_iota(jnp.int32, (tq, tk), 0)

            def k_tile(j, carry, diagonal=False):
                m, l, acc = carry
                c0 = pl.multiple_of(j * tk, tk)
                kt = k_at(c0, tk).astype(BF16)
                vm = _split_heads(v_at(c0, tk).astype(BF16))
                if mode == "fox":
                    ck = ex[1][j]
                hs = range(2)
                s = [lax.dot_general(qm[h], kt, NT_DIMS, preferred_element_type=F32) for h in hs]
                if mode == "fox":
                    s = [s[h] + cq[:, h:h + 1] - ck[h:h + 1, :] for h in hs]
                    if diagonal:
                        keep = (c0 + lax.broadcasted_iota(jnp.int32, (tq, tk), 1)) <= row
                        s = [jnp.where(keep, s[h], MASKED) for h in hs]
                elif mode == "dil":
                    s = [s[h] + ex[0][h, i - j] for h in hs]
                new_m = [jnp.maximum(m[h], jnp.max(s[h], axis=-1, keepdims=True)) for h in hs]
                p = [jnp.exp(s[h] - new_m[h]) for h in hs]
                alpha = [jnp.exp(m[h] - new_m[h]) for h in hs]
                new_l = [alpha[h] * l[h] + jnp.sum(p[h], axis=-1, keepdims=True) for h in hs]
                pv = [jnp.dot(p[h].astype(BF16), vm[h], preferred_element_type=F32) for h in hs]
                acc = acc * _lane_pair(alpha[0], alpha[1], tq) + (pv[0] + pv[1])
                return tuple(new_m), tuple(new_l), acc

            init = ((jnp.full((tq, 1), MASKED, F32),) * 2, (jnp.zeros((tq, 1), F32),) * 2,
                    jnp.zeros((tq, LANES), F32))
            if mode == "fox":
                m, l, acc = k_tile(i, lax.fori_loop(0, i, k_tile, init), True)
            else:
                m, l, acc = lax.fori_loop(0, i + 1 if packed else nk, k_tile, init)
            o_ref[pl.ds(r0, tq), :] = acc / _lane_pair(l[0], l[1], tq)
            lse_ref[pl.ds(r0, tq), :] = _lane_pair(m[0] + jnp.log(l[0]), m[1] + jnp.log(l[1]), tq)
            return 0

        lax.fori_loop(0, nq, q_tile, 0)

    lse_shape = jax.ShapeDtypeStruct((nb * seq_q, 2 * LANES), F32)
    if packed:
        in_specs, args = [_pair_spec(seq_q, 2 * mixer, PAIR_W)], [src]
        in_specs += _fox_specs(seq_q, nk, tk) if mode == "fox" else [
            pl.BlockSpec((None, 2, nq, tq, tk), lambda p, b: (p, 0, 0, 0, 0))]
        args += list(extra) + [out_buf]
        in_specs.append(ANY)
        out_specs = [_pair_spec(seq_q, 2 * mixer), _pair_spec(seq_q, 0)]
        out_shape = [jax.ShapeDtypeStruct(out_buf.shape, F32), lse_shape]
        aliases = {len(args) - 1: 0}
    else:
        in_specs = [_pair_spec(seq_q, 0), _pair_spec(seq_k, 0), _pair_spec(seq_k, 0)]
        args = list(src)
        out_specs = [_pair_spec(seq_q, 0), _pair_spec(seq_q, 0)]
        out_shape = [lse_shape, lse_shape]
        aliases = {}
    return pl.pallas_call(
        body, name=name, grid=(2, nb), in_specs=in_specs, out_specs=out_specs, out_shape=out_shape,
        input_output_aliases=aliases, compiler_params=_params(("parallel", "arbitrary")))(*args)


def _softmax_attn_bwd(src, o, lse, do, *, nb, mode, mixer=None, dbuf=None, extra=(), name):
    packed = mode != "cross"
    n_src = 1 if packed else 3
    seq_q = (src if packed else src[0]).shape[0] // nb
    seq_k = seq_q if packed else src[1].shape[0] // nb
    tq, tk = min(ATT_TILE, seq_q), min(ATT_TILE, seq_k)
    nq, nk = seq_q // tq, seq_k // tk
    n_ex = len(extra)
    n_in = n_src + 3 + n_ex + (1 if packed else 0)

    def body(*refs):
        q_at, k_at, v_at = _qkv_readers(refs[:n_src], packed)
        o_ref, lse_ref, do_ref = refs[n_src:n_src + 3]
        ex = refs[n_src + 3:n_src + 3 + n_ex]
        outs = refs[n_in:]
        if packed:
            d_ref = outs[0]
            dq_w = lambda r0, val: d_ref.__setitem__((pl.ds(r0, tq), slice(0, LANES)), val)
            dk_ref = d_ref.at[:, LANES:2 * LANES]
            dv_ref = d_ref.at[:, 2 * LANES:3 * LANES]
        else:
            dq_ref, dk_ref, dv_ref = outs[:3]
            dq_w = lambda r0, val: dq_ref.__setitem__((pl.ds(r0, tq), slice(None)), val)
        dk_ref[...] = jnp.zeros((seq_k, LANES), F32)
        dv_ref[...] = jnp.zeros((seq_k, LANES), F32)
        if mode == "fox":
            dcum_ref, dcq_ref = outs[-2:]
            dcum_ref[...] = jnp.zeros_like(dcum_ref)
        if mode == "dil":
            dbias_ref = outs[-1]

            @pl.when(pl.program_id(1) == 0)
            def _():
                dbias_ref[...] = jnp.zeros_like(dbias_ref)

        def q_tile(i, _):
            r0 = pl.multiple_of(i * tq, tq)
            qm = _split_heads((q_at(r0, tq) * SCALE).astype(BF16))
            do_f = do_ref[pl.ds(r0, tq), :]
            dom = _split_heads(do_f.astype(BF16))
            dd = _split_heads(do_f * o_ref[pl.ds(r0, tq), :])
            delta = [jnp.sum(dd[h], axis=-1, keepdims=True) for h in range(2)]
            lse_t = lse_ref[pl.ds(r0, tq), :]
            lse_h = [lse_t[:, 0:1], lse_t[:, HEAD_DIM:HEAD_DIM + 1]]
            if mode == "fox":
                cq = ex[0][pl.ds(r0, tq), :]
                row = r0 + lax.broadcasted_iota(jnp.int32, (tq, tk), 0)

            def k_tile(j, carry, diagonal=False):
                dq, rs = carry
                c0 = pl.multiple_of(j * tk, tk)
                kt = k_at(c0, tk).astype(BF16)
                vt = v_at(c0, tk).astype(BF16)
                km = _split_heads(kt)
                if mode == "fox":
                    ck = ex[1][j]
                hs = range(2)
                s = [lax.dot_general(qm[h], kt, NT_DIMS, preferred_element_type=F32) for h in hs]
                dp = [lax.dot_general(dom[h], vt, NT_DIMS, preferred_element_type=F32) for h in hs]
                if mode == "fox":
                    s = [s[h] + cq[:, h:h + 1] - ck[h:h + 1, :] for h in hs]
                    if diagonal:
                        keep = (c0 + lax.broadcasted_iota(jnp.int32, (tq, tk), 1)) <= row
                        s = [jnp.where(keep, s[h], MASKED) for h in hs]
                elif mode == "dil":
                    s = [s[h] + ex[0][h, i - j] for h in hs]
                p = [jnp.exp(s[h] - lse_h[h]) for h in hs]
                ds = [p[h] * (dp[h] - delta[h]) for h in hs]
                dsb = [ds[h].astype(BF16) for h in hs]
                pb = [p[h].astype(BF16) for h in hs]
                dq = dq + (jnp.dot(dsb[0], km[0], preferred_element_type=F32)
                           + jnp.dot(dsb[1], km[1], preferred_element_type=F32))
                dk_t = (lax.dot_general(dsb[0], qm[0], TN_DIMS, preferred_element_type=F32)
                        + lax.dot_general(dsb[1], qm[1], TN_DIMS, preferred_element_type=F32))
                dv_t = (lax.dot_general(pb[0], dom[0], TN_DIMS, preferred_element_type=F32)
                        + lax.dot_general(pb[1], dom[1], TN_DIMS, preferred_element_type=F32))
                if mode == "fox":
                    for h in hs:
                        dcum_ref[j, h:h + 1, :] -= jnp.sum(ds[h], axis=0, keepdims=True)
                    rs = tuple(rs[h] + jnp.sum(ds[h], axis=-1, keepdims=True) for h in hs)
                elif mode == "dil":
                    for h in hs:
                        dbias_ref[h, i - j] += ds[h]
                dk_ref[pl.ds(c0, tk), :] += dk_t
                dv_ref[pl.ds(c0, tk), :] += dv_t
                return dq, rs

            zero = (jnp.zeros((tq, 1), F32),) * 2
            init = (jnp.zeros((tq, LANES), F32), zero)
            if mode == "fox":
                dq, rs = k_tile(i, lax.fori_loop(0, i, k_tile, init), True)
            else:
                dq, rs = lax.fori_loop(0, i + 1 if packed else nk, k_tile, init)
            dq_w(r0, dq * SCALE)
            if mode == "fox":
                dcq_ref[pl.ds(r0, tq), :] = jnp.where(lax.broadcasted_iota(jnp.int32, (tq, 2), 1) == 0, rs[0], rs[1])
            return 0

        lax.fori_loop(0, nq, q_tile, 0)

    if packed:
        in_specs = [_pair_spec(seq_q, 2 * mixer, PAIR_W), _pair_spec(seq_q, 2 * mixer), _pair_spec(seq_q, 0),
                    _pair_spec(seq_q, 2 * mixer)]
        args = [src, o, lse, do]
        out_specs = [_pair_spec(seq_q, 2 * mixer, PAIR_W)]
        out_shape = [jax.ShapeDtypeStruct(dbuf.shape, F32)]
        if mode == "fox":
            in_specs += _fox_specs(seq_q, nk, tk)
            out_specs += [_fox_specs(seq_q, nk, tk)[1], _fox_specs(seq_q, nk, tk)[0]]
            out_shape += [jax.ShapeDtypeStruct((nb, 2, nk, 2, tk), F32), jax.ShapeDtypeStruct((nb, 2, seq_q, 2), F32)]
        else:
            tiles = pl.BlockSpec((None, 2, nq, tq, tk), lambda p, b: (p, 0, 0, 0, 0))
            in_specs.append(tiles)
            out_specs.append(tiles)
            out_shape.append(jax.ShapeDtypeStruct((2, 2, nq, tq, tk), F32))
        args += list(extra) + [dbuf]
        in_specs.append(ANY)
        aliases = {len(args) - 1: 0}
    else:
        sq, sk = _pair_spec(seq_q, 0), _pair_spec(seq_k, 0)
        in_specs, args = [sq, sk, sk, sq, sq, sq], list(src) + [o, lse, do]
        out_specs = [sq, sk, sk]
        out_shape = [jax.ShapeDtypeStruct((nb * seq_q, 2 * LANES), F32)] + [
            jax.ShapeDtypeStruct((nb * seq_k, 2 * LANES), F32)] * 2
        aliases = {}
    return pl.pallas_call(
        body, name=name, grid=(2, nb), in_specs=in_specs, out_specs=out_specs, out_shape=out_shape,
        input_output_aliases=aliases, compiler_params=_params(("parallel", "arbitrary")))(*args)


def _log_sigmoid(z):
    return jnp.minimum(z, 0.0) - jnp.log(1.0 + jnp.exp(-jnp.abs(z)))


def _split_bf16(x):
    hi = x.astype(BF16)
    return hi, (x - hi.astype(F32)).astype(BF16)


def _tri(n, fn):
    r = lax.broadcasted_iota(jnp.int32, (n, n), 0)
    c = lax.broadcasted_iota(jnp.int32, (n, n), 1)
    return jnp.where(fn(r, c), 1.0, 0.0).astype(BF16)


def _sb_attn_fwd(proj, out_buf, *, nb, name):
    seq = proj.shape[0] // nb
    tq = tk = min(ATT_TILE, seq)
    nq = seq // tq

    def body(qkv_ref, _, o_ref, lt_ref):
        q_at, k_at, v_at = _qkv_readers((qkv_ref,), True)
        after = _tri(tk, lambda r, c: r > c)

        def q_tile(i, _):
            r0 = pl.multiple_of(i * tq, tq)
            qm = _split_heads((q_at(r0, tq) * SCALE).astype(BF16))
            row = r0 + lax.broadcasted_iota(jnp.int32, (tq, tk), 0)

            def k_tile(j, carry, diagonal):
                c, acc = carry
                c0 = pl.multiple_of(j * tk, tk)
                kt = k_at(c0, tk).astype(BF16)
                vm = _split_heads(v_at(c0, tk).astype(BF16))
                if diagonal:
                    strict = (c0 + lax.broadcasted_iota(jnp.int32, (tq, tk), 1)) < row
                hs = range(2)
                z = [lax.dot_general(qm[h], kt, NT_DIMS, preferred_element_type=F32) for h in hs]
                ls = [_log_sigmoid(z[h]) for h in hs]
                lk = [ls[h] - z[h] for h in hs]
                if diagonal:
                    lk = [jnp.where(strict, lk[h], 0.0) for h in hs]
                parts = [_split_bf16(lk[h]) for h in hs]
                sfx = [jnp.dot(parts[h][0], after, preferred_element_type=F32)
                       + jnp.dot(parts[h][1], after, preferred_element_type=F32) for h in hs]
                att = [jnp.exp(ls[h] + sfx[h] + c[h]) for h in hs]
                if diagonal:
                    att = [jnp.where(strict, att[h], 0.0) for h in hs]
                acc = acc + (jnp.dot(att[0].astype(BF16), vm[0], preferred_element_type=F32)
                             + jnp.dot(att[1].astype(BF16), vm[1], preferred_element_type=F32))
                return tuple(c[h] + jnp.sum(lk[h], axis=-1, keepdims=True) for h in hs), acc

            init = ((jnp.zeros((tq, 1), F32),) * 2, jnp.zeros((tq, LANES), F32))
            c, acc = lax.fori_loop(1, i + 1, lambda jj, cr: k_tile(i - jj, cr, False), k_tile(i, init, True))
            o_ref[pl.ds(r0, tq), :] = acc
            lt_ref[pl.ds(r0, tq), :] = _lane_pair(c[0], c[1], tq)
            return 0

        lax.fori_loop(0, nq, q_tile, 0)

    return pl.pallas_call(
        body, name=name, grid=(2, nb), in_specs=[_pair_spec(seq, 2 * MIX_SB, PAIR_W), ANY],
        out_specs=[_pair_spec(seq, 2 * MIX_SB), _pair_spec(seq, 0)],
        out_shape=[jax.ShapeDtypeStruct(out_buf.shape, F32), jax.ShapeDtypeStruct((nb * seq, 2 * LANES), F32)],
        input_output_aliases={1: 0}, compiler_params=_params(("parallel", "arbitrary")))(proj, out_buf)


def _sb_attn_bwd(proj, ltot, do, dbuf, *, nb, name):
    seq = proj.shape[0] // nb
    tq = tk = min(ATT_TILE, seq)
    nq = seq // tq

    def body(qkv_ref, lt_ref, do_ref, _, d_ref):
        q_at, k_at, v_at = _qkv_readers((qkv_ref,), True)
        upto = _tri(tk, lambda r, c: r <= c)
        before = _tri(tk, lambda r, c: r < c)
        dk_ref = d_ref.at[:, LANES:2 * LANES]
        dv_ref = d_ref.at[:, 2 * LANES:3 * LANES]
        dk_ref[...] = jnp.zeros((seq, LANES), F32)
        dv_ref[...] = jnp.zeros((seq, LANES), F32)

        def q_tile(i, _):
            r0 = pl.multiple_of(i * tq, tq)
            qm = _split_heads((q_at(r0, tq) * SCALE).astype(BF16))
            dom = _split_heads(do_ref[pl.ds(r0, tq), :].astype(BF16))
            lt_t = lt_ref[pl.ds(r0, tq), :]
            lt_h = [lt_t[:, 0:1], lt_t[:, HEAD_DIM:HEAD_DIM + 1]]
            row = r0 + lax.broadcasted_iota(jnp.int32, (tq, tk), 0)

            def k_tile(j, carry, diagonal):
                pc, qc, dq = carry
                c0 = pl.multiple_of(j * tk, tk)
                kt = k_at(c0, tk).astype(BF16)
                vt = v_at(c0, tk).astype(BF16)
                km = _split_heads(kt)
                if diagonal:
                    strict = (c0 + lax.broadcasted_iota(jnp.int32, (tq, tk), 1)) < row
                hs = range(2)
                z = [lax.dot_general(qm[h], kt, NT_DIMS, preferred_element_type=F32) for h in hs]
                da = [lax.dot_general(dom[h], vt, NT_DIMS, preferred_element_type=F32) for h in hs]
                ls = [_log_sigmoid(z[h]) for h in hs]
                lk = [ls[h] - z[h] for h in hs]
                if diagonal:
                    lk = [jnp.where(strict, lk[h], 0.0) for h in hs]
                parts = [_split_bf16(lk[h]) for h in hs]
                pin = [jnp.dot(parts[h][0], upto, preferred_element_type=F32)
                       + jnp.dot(parts[h][1], upto, preferred_element_type=F32) for h in hs]
                att = [jnp.exp(ls[h] + (lt_h[h] - pc[h] - pin[h])) for h in hs]
                if diagonal:
                    att = [jnp.where(strict, att[h], 0.0) for h in hs]
                dg = [att[h] * da[h] for h in hs]
                qx = [qc[h] + jnp.dot(dg[h].astype(BF16), before, preferred_element_type=F32) for h in hs]
                sig = [jnp.exp(ls[h]) for h in hs]
                dz = [dg[h] * (1.0 - sig[h]) - sig[h] * qx[h] for h in hs]
                if diagonal:
                    dz = [jnp.where(strict, dz[h], 0.0) for h in hs]
                dzb = [dz[h].astype(BF16) for h in hs]
                attb = [att[h].astype(BF16) for h in hs]
                dq = dq + (jnp.dot(dzb[0], km[0], preferred_element_type=F32)
                           + jnp.dot(dzb[1], km[1], preferred_element_type=F32))
                dk_ref[pl.ds(c0, tk), :] += (lax.dot_general(dzb[0], qm[0], TN_DIMS, preferred_element_type=F32)
                                             + lax.dot_general(dzb[1], qm[1], TN_DIMS, preferred_element_type=F32))
                dv_ref[pl.ds(c0, tk), :] += (lax.dot_general(attb[0], dom[0], TN_DIMS, preferred_element_type=F32)
                                             + lax.dot_general(attb[1], dom[1], TN_DIMS, preferred_element_type=F32))
                return (tuple(pc[h] + jnp.sum(lk[h], axis=-1, keepdims=True) for h in hs),
                        tuple(qc[h] + jnp.sum(dg[h], axis=-1, keepdims=True) for h in hs), dq)

            zero = (jnp.zeros((tq, 1), F32),) * 2
            carry = lax.fori_loop(0, i, lambda j, cr: k_tile(j, cr, False), (zero, zero, jnp.zeros((tq, LANES), F32)))
            _, _, dq = k_tile(i, carry, True)
            d_ref[pl.ds(r0, tq), 0:LANES] = dq * SCALE
            return 0

        lax.fori_loop(0, nq, q_tile, 0)

    return pl.pallas_call(
        body, name=name, grid=(2, nb),
        in_specs=[_pair_spec(seq, 2 * MIX_SB, PAIR_W), _pair_spec(seq, 0), _pair_spec(seq, 2 * MIX_SB), ANY],
        out_specs=_pair_spec(seq, 2 * MIX_SB, PAIR_W), out_shape=jax.ShapeDtypeStruct(dbuf.shape, F32),
        input_output_aliases={3: 0}, compiler_params=_params(("parallel", "arbitrary")))(proj, ltot, do, dbuf)


def _lane_scan(x, reverse=False):
    n = x.shape[-1]
    lane = lax.broadcasted_iota(jnp.int32, x.shape, 1)
    k = 1
    while k < n:
        if reverse:
            x = x + jnp.where(lane < n - k, pltpu.roll(x, n - k, 1), 0.0)
        else:
            x = x + jnp.where(lane >= k, pltpu.roll(x, k, 1), 0.0)
        k *= 2
    return x


def _fox_gate_fwd(f_rows, b_rows):
    def body(f_ref, b_ref, o_ref):
        o_ref[...] = _lane_scan(_log_sigmoid(f_ref[...] + b_ref[...]))

    return pl.pallas_call(body, name="fox_gate_fwd", out_shape=jax.ShapeDtypeStruct(f_rows.shape, F32))(f_rows, b_rows)


def _fox_gate_bwd(dcum, f_rows, b_rows):
    def body(d_ref, f_ref, b_ref, df_ref, db_ref):
        z = f_ref[...] + b_ref[...]
        df = _lane_scan(d_ref[...], reverse=True) * jnp.exp(_log_sigmoid(-z))
        df_ref[...] = df
        rs = jnp.sum(df, axis=-1, keepdims=True)
        tot = rs
        for e in range(1, f_rows.shape[0] // N_HEADS):
            tot = tot + pltpu.roll(rs, e * N_HEADS, 0)
        db_ref[...] = tot

    return pl.pallas_call(
        body, name="fox_gate_bwd",
        out_shape=[jax.ShapeDtypeStruct(f_rows.shape, F32), jax.ShapeDtypeStruct((f_rows.shape[0], 1), F32)],
    )(dcum, f_rows, b_rows)


def _dil_tables(seq):
    t = min(ATT_TILE, seq)
    n = seq // t
    a = np.arange(t)
    d = (np.arange(n)[:, None, None] * t + a[None, :, None] - a[None, None, :]).astype(np.int64)
    count = np.zeros(d.shape, np.int64)
    for window, dil in DILATED_PATTERNS:
        count += (d >= 0) & (d % dil == 0) & (d // dil <= window // dil)
    nn = np.maximum(d, 0)
    max_exact = NUM_BUCKETS // 2
    nf = np.maximum(nn, 1).astype(np.float32)
    large = max_exact + (np.log(nf / np.float32(max_exact)) / np.float32(math.log(MAX_DISTANCE / max_exact))
                         * np.float32(NUM_BUCKETS - max_exact)).astype(np.int32)
    bucket = np.where(nn < max_exact, nn, np.minimum(large, NUM_BUCKETS - 1))
    bucket = np.where(count > 0, bucket, -1).astype(np.int32)
    logc = np.where(count > 0, np.log(np.maximum(count, 1)), MASKED).astype(np.float32)
    return bucket, logc


def _dil_bias(rel_bias, seq):
    bucket, logc = _dil_tables(seq)
    n, t, _ = bucket.shape

    def body(rb_ref, bk_ref, lc_ref, o_ref):
        h = pl.program_id(0) * 2 + pl.program_id(1)
        bk = bk_ref[...]
        out = lc_ref[...]
        for b in range(NUM_BUCKETS):
            out = jnp.where(bk == b, out + rb_ref[b, h], out)
        o_ref[...] = out

    full = pl.BlockSpec((n, t, t), lambda p, h: (0, 0, 0))
    return pl.pallas_call(
        body, name="dil_bias", grid=(2, 2),
        in_specs=[pl.BlockSpec(memory_space=pltpu.SMEM), full, full],
        out_specs=pl.BlockSpec((None, None, n, t, t), lambda p, h: (p, h, 0, 0, 0)),
        out_shape=jax.ShapeDtypeStruct((2, 2, n, t, t), F32),
        compiler_params=_params(("parallel", "parallel")))(rel_bias, jnp.asarray(bucket), jnp.asarray(logc))


def _dil_bias_bwd(dbias, seq):
    bucket, _ = _dil_tables(seq)
    n, t, _ = bucket.shape

    def body(d_ref, bk_ref, o_ref):
        bk = bk_ref[...]
        lane = lax.broadcasted_iota(jnp.int32, (1, LANES), 1)
        for b in range(NUM_BUCKETS):
            rowv = jnp.zeros((1, LANES), F32)
            for h in range(N_HEADS):
                s = jnp.sum(jnp.where(bk == b, d_ref[h // 2, h % 2], 0.0))
                rowv = jnp.where(lane == h, s, rowv)
            o_ref[b:b + 1, :] = rowv

    out = pl.pallas_call(body, name="dil_bias_bwd", out_shape=jax.ShapeDtypeStruct((NUM_BUCKETS, LANES), F32),
                         compiler_params=pltpu.CompilerParams(vmem_limit_bytes=VMEM_LIMIT))(dbias, jnp.asarray(bucket))
    return out[:, :N_HEADS]


def _shift_rows(x, k, row, fill=0.0):
    n = x.shape[0]
    if k > 0:
        return jnp.where(row >= k, pltpu.roll(x, k, 0), fill)
    return jnp.where(row < n + k, pltpu.roll(x, n + k, 0), fill)


def _row_scan(a, u, row, reverse=False):
    n = a.shape[0]
    k = 1
    while k < n:
        s = -k if reverse else k
        u = a * _shift_rows(u, s, row) + u
        a = a * _shift_rows(a, s, row, 1.0)
        k *= 2
    return u


def _sigmoid(x):
    return 1.0 / (1.0 + jnp.exp(-x))


def _gelu(g):
    return 0.5 * g * (1.0 + lax.erf(g * (2.0 ** -0.5)))


def _gelu_grad(g):
    return 0.5 * (1.0 + lax.erf(g * (2.0 ** -0.5))) + g * jnp.exp(-0.5 * g * g) * (1.0 / math.sqrt(2.0 * math.pi))


def _neg_expm1(x):
    small = -x * (1.0 + x * (0.5 + x * (1.0 / 6.0 + x * (1.0 / 24.0))))
    return jnp.where(x > -0.03, small, 1.0 - jnp.exp(x))


def _lru_core(x, vec, wa, wx, row):
    xs = [_shift_rows(x, 3 - j, row) if j < 3 else x for j in range(4)]
    xc = vec[4:5, :]
    for j in range(4):
        xc = xc + vec[j:j + 1, :] * xs[j]
    xcb = xc.astype(BF16)
    r = _sigmoid(jnp.dot(xcb, wa, preferred_element_type=F32) + vec[5:6, :])
    ig = _sigmoid(jnp.dot(xcb, wx, preferred_element_type=F32) + vec[6:7, :])
    lam = vec[7:8, :]
    sp = jnp.maximum(-lam, 0.0) - _log_sigmoid(jnp.abs(lam))
    la = -LRU_C * r * sp
    a = jnp.exp(la)
    mult = jnp.sqrt(_neg_expm1(2.0 * la))
    return xs, xc, xcb, r, ig, sp, la, a, mult


def _lru_specs(seq):
    xg = pl.BlockSpec((seq, LRU_W), lambda hf, b: (b, COL_LRU // LRU_W + hf))
    mix = pl.BlockSpec((seq, LANES), lambda hf, b: (b, 2 * MIX_LRU + hf))
    vec = pl.BlockSpec((SUBLANES, LANES), lambda hf, b: (0, hf))
    mat = pl.BlockSpec((None, LANES, LANES), lambda hf, b: (hf, 0, 0))
    return xg, mix, vec, mat


def _lru_fwd(proj, vec, wa, wx, out_buf, *, nb, name):
    seq = proj.shape[0] // nb

    def body(xg_ref, vec_ref, wa_ref, wx_ref, _, o_ref):
        row = lax.broadcasted_iota(jnp.int32, (seq, LANES), 0)
        _, xc, _, _, ig, _, _, a, mult = _lru_core(xg_ref[:, 0:LANES], vec_ref[...], wa_ref[...], wx_ref[...], row)
        h = _row_scan(a, mult * (ig * xc), row)
        o_ref[...] = h * _gelu(xg_ref[:, LANES:LRU_W])

    xg, mix, vecs, mat = _lru_specs(seq)
    return pl.pallas_call(
        body, name=name, grid=(2, nb), in_specs=[xg, vecs, mat, mat, ANY], out_specs=mix,
        out_shape=jax.ShapeDtypeStruct(out_buf.shape, F32), input_output_aliases={4: 0},
        compiler_params=_params(("parallel", "arbitrary")))(proj, vec, wa, wx, out_buf)


def _lru_bwd(proj, vec, wa, wx, dout, dbuf, *, nb, name):
    seq = proj.shape[0] // nb

    def body(xg_ref, vec_ref, wa_ref, wx_ref, do_ref, _, d_ref, dvec_ref, dwa_ref, dwx_ref):
        row = lax.broadcasted_iota(jnp.int32, (seq, LANES), 0)
        vec_, wa_, wx_ = vec_ref[...], wa_ref[...], wx_ref[...]
        xs, xc, xcb, r, ig, sp, la, a, mult = _lru_core(xg_ref[:, 0:LANES], vec_, wa_, wx_, row)
        h = _row_scan(a, mult * (ig * xc), row)
        gate, do = xg_ref[:, LANES:LRU_W], do_ref[...]
        d_ref[:, LANES:LRU_W] = do * h * _gelu_grad(gate)
        dh = do * _gelu(gate)
        gacc = _row_scan(_shift_rows(a, -1, row), dh, row, reverse=True)
        da = gacc * _shift_rows(h, 1, row)
        dmult = gacc * (ig * xc)
        dig = gacc * (mult * xc)
        dxc = gacc * (mult * ig)
        dla = da * a - dmult * (a * a) / mult
        dr = (-LRU_C) * sp * dla
        dsp = jnp.sum((-LRU_C) * r * dla, axis=0, keepdims=True)
        dpr = dr * r * (1.0 - r)
        dpi = dig * ig * (1.0 - ig)
        dprb, dpib = dpr.astype(BF16), dpi.astype(BF16)
        dxc = (dxc + lax.dot_general(dprb, wa_, NT_DIMS, preferred_element_type=F32)
               + lax.dot_general(dpib, wx_, NT_DIMS, preferred_element_type=F32))
        dx = vec_[3:4, :] * dxc
        for j in range(3):
            dx = dx + vec_[j:j + 1, :] * _shift_rows(dxc, -(3 - j), row)
        d_ref[:, 0:LANES] = dx

        @pl.when(pl.program_id(1) == 0)
        def _():
            dvec_ref[...] = jnp.zeros_like(dvec_ref)
            dwa_ref[...] = jnp.zeros_like(dwa_ref)
            dwx_ref[...] = jnp.zeros_like(dwx_ref)

        for j in range(4):
            dvec_ref[j:j + 1, :] += jnp.sum(dxc * xs[j], axis=0, keepdims=True)
        dvec_ref[4:5, :] += jnp.sum(dxc, axis=0, keepdims=True)
        dvec_ref[5:6, :] += jnp.sum(dpr, axis=0, keepdims=True)
        dvec_ref[6:7, :] += jnp.sum(dpi, axis=0, keepdims=True)
        lam = vec_[7:8, :]
        dvec_ref[7:8, :] += -dsp * _sigmoid(-lam)
        dwa_ref[...] += lax.dot_general(xcb, dprb, TN_DIMS, preferred_element_type=F32)
        dwx_ref[...] += lax.dot_general(xcb, dpib, TN_DIMS, preferred_element_type=F32)

    xg, mix, vecs, mat = _lru_specs(seq)
    return pl.pallas_call(
        body, name=name, grid=(2, nb), in_specs=[xg, vecs, mat, mat, mix, ANY], out_specs=[xg, vecs, mat, mat],
        out_shape=[jax.ShapeDtypeStruct(dbuf.shape, F32), jax.ShapeDtypeStruct((SUBLANES, 2 * LANES), F32),
                   jax.ShapeDtypeStruct((2, LANES, LANES), F32), jax.ShapeDtypeStruct((2, LANES, LANES), F32)],
        input_output_aliases={5: 0},
        compiler_params=_params(("parallel", "arbitrary")))(proj, vec, wa, wx, dout, dbuf)


FFN_ROWS = 256
FFN_COLS = 1408


def _with_halo(halo, x, k):
    xx = jnp.concatenate([halo, x], axis=0)
    return pltpu.roll(xx, k, 0)[SUBLANES:, :]


def _ffn_conv(x_ref, halo_ref, cw, pos):
    x, halo = x_ref[...], halo_ref[...]
    x1 = jnp.where(pos >= 1, _with_halo(halo, x, 1), 0.0)
    x2 = jnp.where(pos >= 2, _with_halo(halo, x, 2), 0.0)
    return cw[3:4, :] + cw[0:1, :] * x2 + cw[1:2, :] * x1 + cw[2:3, :] * x, x1, x2


def _ffn_specs(tm, tn, gate_off):
    prev = lambda i: jnp.maximum(i * (tm // SUBLANES) - 1, 0)
    up = pl.BlockSpec((tm, tn), lambda j, i: (i, j))
    gate = pl.BlockSpec((tm, tn), lambda j, i: (i, j + gate_off))
    up_h = pl.BlockSpec((SUBLANES, tn), lambda j, i: (prev(i), j))
    gate_h = pl.BlockSpec((SUBLANES, tn), lambda j, i: (prev(i), j + gate_off))
    cw_up = pl.BlockSpec((SUBLANES, tn), lambda j, i: (0, j))
    cw_gate = pl.BlockSpec((SUBLANES, tn), lambda j, i: (0, j + gate_off))
    return up, gate, up_h, gate_h, cw_up, cw_gate


def _ffn_act(hf, cw, *, seq, name):
    t, w2 = hf.shape
    w = w2 // 2
    tm, tn = _tile(seq, FFN_ROWS, SUBLANES), _tile(w, FFN_COLS)

    def body(u_ref, g_ref, uh_ref, gh_ref, cu_ref, cg_ref, o_ref):
        pos = (pl.program_id(1) * tm + lax.broadcasted_iota(jnp.int32, (tm, 1), 0)) % seq
        up, _, _ = _ffn_conv(u_ref, uh_ref, cu_ref[...], pos)
        gate, _, _ = _ffn_conv(g_ref, gh_ref, cg_ref[...], pos)
        o_ref[...] = (_gelu(gate) * up).astype(BF16)

    specs = _ffn_specs(tm, tn, w // tn)
    return pl.pallas_call(
        body, name=name, grid=(w // tn, t // tm), in_specs=list(specs), out_specs=specs[0],
        out_shape=jax.ShapeDtypeStruct((t, w), BF16),
        compiler_params=_params(("parallel", "parallel")))(hf, hf, hf, hf, cw, cw)


def _ffn_act_bwd(hf, cw, dact, *, seq, name):
    t, w2 = hf.shape
    w = w2 // 2
    tm, tn = _tile(seq, FFN_ROWS, SUBLANES), _tile(w, FFN_COLS)

    def body(u_ref, g_ref, uh_ref, gh_ref, cu_ref, cg_ref, da_ref, d_ref, dcu_ref, dcg_ref):
        pos = (pl.program_id(1) * tm + lax.broadcasted_iota(jnp.int32, (tm, 1), 0)) % seq
        up, u1, u2 = _ffn_conv(u_ref, uh_ref, cu_ref[...], pos)
        gate, g1, g2 = _ffn_conv(g_ref, gh_ref, cg_ref[...], pos)
        da = da_ref[...]
        d_up = da * _gelu(gate)
        d_gate = da * up * _gelu_grad(gate)
        d_ref[0] = d_up
        d_ref[1] = d_gate

        @pl.when(pl.program_id(1) == 0)
        def _():
            dcu_ref[...] = jnp.zeros_like(dcu_ref)
            dcg_ref[...] = jnp.zeros_like(dcg_ref)

        for ref, d, taps in ((dcu_ref, d_up, (u2, u1, u_ref[...])), (dcg_ref, d_gate, (g2, g1, g_ref[...]))):
            for j in range(3):
                ref[j:j + 1, :] += jnp.sum(d * taps[j], axis=0, keepdims=True)
            ref[3:4, :] += jnp.sum(d, axis=0, keepdims=True)

    specs = _ffn_specs(tm, tn, w // tn)
    tile, cwt = specs[0], specs[4]
    return pl.pallas_call(
        body, name=name, grid=(w // tn, t // tm), in_specs=list(specs) + [tile],
        out_specs=[pl.BlockSpec((2, tm, tn), lambda j, i: (0, i, j)), cwt, cwt],
        out_shape=[jax.ShapeDtypeStruct((2, t, w), F32), jax.ShapeDtypeStruct((SUBLANES, w), F32),
                   jax.ShapeDtypeStruct((SUBLANES, w), F32)],
        compiler_params=_params(("parallel", "arbitrary")))(hf, hf, hf, hf, cw, cw, dact)


def _conv_transpose(d, cw, *, seq, name):
    _, t, w = d.shape
    tm, tn = _tile(seq, FFN_ROWS, SUBLANES), _tile(w, FFN_COLS)
    last = t // SUBLANES - 1
    ncol = w // tn

    def body(d_ref, nx_ref, cw_ref, o_ref):
        pos = (pl.program_id(2) * tm + lax.broadcasted_iota(jnp.int32, (tm, 1), 0)) % seq
        x, cw_ = d_ref[...], cw_ref[...]
        xx = jnp.concatenate([x, nx_ref[...]], axis=0)
        n = tm + SUBLANES
        x1 = jnp.where(pos < seq - 1, pltpu.roll(xx, n - 1, 0)[:tm, :], 0.0)
        x2 = jnp.where(pos < seq - 2, pltpu.roll(xx, n - 2, 0)[:tm, :], 0.0)
        o_ref[...] = (cw_[2:3, :] * x + cw_[1:2, :] * x1 + cw_[0:1, :] * x2).astype(BF16)

    tile = pl.BlockSpec((None, tm, tn), lambda h, j, i: (h, i, j))
    nxt = pl.BlockSpec((None, SUBLANES, tn), lambda h, j, i: (h, jnp.minimum((i + 1) * (tm // SUBLANES), last), j))
    cws = pl.BlockSpec((SUBLANES, tn), lambda h, j, i: (0, h * ncol + j))
    return pl.pallas_call(
        body, name=name, grid=(2, ncol, t // tm), in_specs=[tile, nxt, cws],
        out_specs=pl.BlockSpec((tm, tn), lambda h, j, i: (i, h * ncol + j)),
        out_shape=jax.ShapeDtypeStruct((t, 2 * w), BF16),
        compiler_params=_params(("parallel", "parallel", "parallel")))(d, d, cw)


def _adamw(w, g, m, v, *, name, rows=256):
    nl, r, c = w.shape
    tr = _tile(r, rows, SUBLANES)

    def body(w_ref, g_ref, m_ref, v_ref, d_ref, nm_ref, nv_ref):
        g_ = g_ref[...]
        nm = ADAM_B1 * m_ref[...] + (1.0 - ADAM_B1) * g_
        nv = ADAM_B2 * v_ref[...] + (1.0 - ADAM_B2) * (g_ * g_)
        m_hat = nm / (1.0 - ADAM_B1 ** ADAM_STEP)
        v_hat = nv / (1.0 - ADAM_B2 ** ADAM_STEP)
        d_ref[...] = -ADAM_LR * (m_hat / (jnp.sqrt(v_hat) + ADAM_EPS) + ADAM_WD * w_ref[...])
        nm_ref[...] = nm
        nv_ref[...] = nv

    spec = pl.BlockSpec((None, tr, c), lambda l, i: (l, i, 0))
    shape = jax.ShapeDtypeStruct((nl, r, c), F32)
    return pl.pallas_call(body, name=name, grid=(nl, r // tr), in_specs=[spec] * 4, out_specs=[spec] * 3,
                          out_shape=[shape] * 3, compiler_params=_params(("parallel", "parallel")))(w, g, m, v)


def _mesh_pos():
    return lax.axis_index("x"), lax.axis_index("y"), lax.axis_index("c")


def _peers(x, y):
    chips = [(1 - x, y), (x, 1 - y), (1 - x, 1 - y)]
    return [(px, py, 2 * px + py) for px, py in chips]


def _remote(src, dst, send_sems, recv_sems, idx, to):
    return pltpu.make_async_remote_copy(src, dst, send_sems.at[idx], recv_sems.at[idx], device_id=to,
                                        device_id_type=MESH)


def _gather_weights(bufs, *, name):
    n = len(bufs)

    def body(*refs):
        ins, outs = refs[:n], refs[n:2 * n]
        send_sems, recv_sems = refs[2 * n:]
        x, y, c = _mesh_pos()
        me = 2 * x + y
        peers = _peers(x, y)
        started = []
        for i in range(n):
            for r, (px, py, _) in enumerate(peers):
                cp = _remote(ins[i].at[me, c], outs[i].at[me, c], send_sems, recv_sems, (i, r), (px, py, c))
                cp.start()
                started.append(cp)
        for i in range(n):
            for r, (px, py, k) in enumerate(peers):
                _remote(outs[i].at[k, c], outs[i].at[k, c], send_sems, recv_sems, (i, r), (px, py, c)).wait_recv()
                cp = _remote(outs[i].at[k, c], outs[i].at[k, c], send_sems, recv_sems, (i, 3 + r), (x, y, 1 - c))
                cp.start()
                started.append(cp)
        for i in range(n):
            for r, (_, _, k) in enumerate(peers):
                _remote(outs[i].at[k, 1 - c], outs[i].at[k, 1 - c], send_sems, recv_sems, (i, 3 + r),
                        (x, y, 1 - c)).wait_recv()
        for cp in started:
            cp.wait_send()

    return pl.pallas_call(
        body, name=name, in_specs=[ANY] * n, out_specs=[ANY] * n,
        out_shape=[jax.ShapeDtypeStruct(b.shape, b.dtype) for b in bufs],
        input_output_aliases={i: i for i in range(n)},
        scratch_shapes=[pltpu.SemaphoreType.DMA((n, 6)), pltpu.SemaphoreType.DMA((n, 6))])(*bufs)


def _reduce_sibling(gs, *, name):
    n = len(gs)

    def body(*refs):
        ins, outs = refs[:n], refs[n:2 * n]
        send_sems, recv_sems = refs[2 * n:]
        x, y, c = _mesh_pos()
        cps = []
        for i in range(n):
            h = gs[i].shape[1] // 2
            cp = _remote(ins[i].at[:, pl.ds((1 - c) * h, h), :], outs[i], send_sems, recv_sems, i, (x, y, 1 - c))
            cp.start()
            cps.append(cp)
        for cp in cps:
            cp.wait()

    return pl.pallas_call(
        body, name=name, in_specs=[ANY] * n, out_specs=[ANY] * n,
        out_shape=[jax.ShapeDtypeStruct((g.shape[0], g.shape[1] // 2, g.shape[2]), g.dtype) for g in gs],
        scratch_shapes=[pltpu.SemaphoreType.DMA((n,)), pltpu.SemaphoreType.DMA((n,))])(*gs)


def _reduce_chips(ps, *, name):
    n = len(ps)

    def body(*refs):
        ins, outs = refs[:n], refs[n:2 * n]
        send_sems, recv_sems = refs[2 * n:]
        x, y, c = _mesh_pos()
        cps = []
        for i in range(n):
            for r, (px, py, k) in enumerate(_peers(x, y)):
                cp = _remote(ins[i].at[k], outs[i].at[r], send_sems, recv_sems, (i, r), (px, py, c))
                cp.start()
                cps.append(cp)
        for cp in cps:
            cp.wait()

    return pl.pallas_call(
        body, name=name, in_specs=[ANY] * n, out_specs=[ANY] * n,
        out_shape=[jax.ShapeDtypeStruct((3,) + p.shape[1:], p.dtype) for p in ps],
        scratch_shapes=[pltpu.SemaphoreType.DMA((n, 3)), pltpu.SemaphoreType.DMA((n, 3))])(*ps)


def _share_halves(bufs, *, name):
    n = len(bufs)

    def body(*refs):
        ins, outs = refs[:n], refs[n:2 * n]
        send_sems, recv_sems = refs[2 * n:]
        x, y, c = _mesh_pos()
        cps = []
        for i in range(n):
            h = bufs[i].shape[1] // 2
            mine = pl.ds(c * h, h)
            cp = _remote(ins[i].at[:, mine, :], outs[i].at[:, mine, :], send_sems, recv_sems, i, (x, y, 1 - c))
            cp.start()
            cps.append(cp)
        for cp in cps:
            cp.wait()

    return pl.pallas_call(
        body, name=name, in_specs=[ANY] * n, out_specs=[ANY] * n,
        out_shape=[jax.ShapeDtypeStruct(b.shape, b.dtype) for b in bufs],
        input_output_aliases={i: i for i in range(n)},
        scratch_shapes=[pltpu.SemaphoreType.DMA((n,)), pltpu.SemaphoreType.DMA((n,))])(*bufs)


def _add_own_half(full, recv, pos, *, name, rows=256):
    k4, h, n = recv.shape
    tr = _tile(h, rows, 16)
    nblk = h // tr

    def body(pos_ref, a_ref, b_ref, o_ref):
        o_ref[...] = (a_ref[...] + b_ref[...]).astype(BF16)

    grid_spec = pltpu.PrefetchScalarGridSpec(
        num_scalar_prefetch=1, grid=(k4, nblk),
        in_specs=[pl.BlockSpec((None, tr, n), lambda k, i, pos_ref: (k, pos_ref[1] * nblk + i, 0)),
                  pl.BlockSpec((None, tr, n), lambda k, i, pos_ref: (k, i, 0))],
        out_specs=pl.BlockSpec((None, tr, n), lambda k, i, pos_ref: (k, i, 0)))
    return pl.pallas_call(body, name=name, grid_spec=grid_spec, out_shape=jax.ShapeDtypeStruct(recv.shape, BF16),
                          compiler_params=_params(("parallel", "parallel")))(pos, full, recv)


def _sum_into(own, others, buf, pos, layer, *, name, rows=256):
    _, h, n = own.shape
    tr = _tile(h, rows, 16)
    nblk = h // tr

    def body(pos_ref, own_ref, oth_ref, _, o_ref):
        acc = own_ref[...].astype(F32)
        for r in range(3):
            acc = acc + oth_ref[r].astype(F32)
        o_ref[...] = acc

    grid_spec = pltpu.PrefetchScalarGridSpec(
        num_scalar_prefetch=1, grid=(nblk,),
        in_specs=[pl.BlockSpec((None, tr, n), lambda i, pos_ref: (pos_ref[0], i, 0)),
                  pl.BlockSpec((3, tr, n), lambda i, pos_ref: (0, i, 0)), ANY],
        out_specs=pl.BlockSpec((None, tr, n), lambda i, pos_ref: (layer, pos_ref[1] * nblk + i, 0)))
    return pl.pallas_call(body, name=name, grid_spec=grid_spec, out_shape=jax.ShapeDtypeStruct(buf.shape, F32),
                          input_output_aliases={3: 0}, compiler_params=_params(("parallel",)))(pos, own, others, buf)


def _sibling_pair(buf, *, name):
    def body(src_ref, out_ref, send_sem, recv_sem, local_sem):
        x, y, c = _mesh_pos()
        local = pltpu.make_async_copy(src_ref, out_ref.at[c], local_sem)
        local.start()
        cp = pltpu.make_async_remote_copy(src_ref, out_ref.at[c], send_sem, recv_sem, device_id=(x, y, 1 - c),
                                          device_id_type=MESH)
        cp.start()
        cp.wait()
        local.wait()

    return pl.pallas_call(
        body, name=name, in_specs=[ANY], out_specs=ANY, out_shape=jax.ShapeDtypeStruct((2,) + buf.shape, buf.dtype),
        scratch_shapes=[pltpu.SemaphoreType.DMA, pltpu.SemaphoreType.DMA, pltpu.SemaphoreType.DMA])(buf)


def _chip_bcast(buf, *, name):
    def body(src_ref, out_ref, send_sems, recv_sems, local_sem):
        x, y, c = _mesh_pos()
        me = 2 * x + y
        local = pltpu.make_async_copy(src_ref, out_ref.at[me], local_sem)
        local.start()
        sends = []
        for r, (px, py, _) in enumerate(_peers(x, y)):
            cp = _remote(src_ref, out_ref.at[me], send_sems, recv_sems, r, (px, py, c))
            cp.start()
            sends.append(cp)
        for r, (px, py, k) in enumerate(_peers(x, y)):
            _remote(src_ref, out_ref.at[k], send_sems, recv_sems, r, (px, py, c)).wait_recv()
        for cp in sends:
            cp.wait_send()
        local.wait()

    return pl.pallas_call(
        body, name=name, in_specs=[ANY], out_specs=ANY, out_shape=jax.ShapeDtypeStruct((4,) + buf.shape, buf.dtype),
        scratch_shapes=[pltpu.SemaphoreType.DMA((3,)), pltpu.SemaphoreType.DMA((3,)), pltpu.SemaphoreType.DMA])(buf)


def _sum_slots(buf, *, name, rows=384):
    r, n = buf.shape[-2:]
    k = int(np.prod(buf.shape[:-2]))
    tr = _tile(r, rows, SUBLANES)

    def body(b_ref, o_ref):
        acc = b_ref[0]
        for s in range(1, k):
            acc = acc + b_ref[s]
        o_ref[...] = acc

    return pl.pallas_call(
        body, name=name, grid=(r // tr,), in_specs=[pl.BlockSpec((k, tr, n), lambda i: (0, i, 0))],
        out_specs=pl.BlockSpec((tr, n), lambda i: (i, 0)), out_shape=jax.ShapeDtypeStruct((r, n), F32),
        compiler_params=_params(("parallel",)))(buf.reshape((k, r, n)))


ROW = 1024
BIG = (("w_in", 2), ("w_out", 1), ("w_cq", 1), ("w_ck", 1), ("w_cv", 1), ("w_co", 2), ("w_up", 2), ("w_down", 1))
CONV = ("lru_conv_w", "ffn_conv_w")
REPLICATED = ("norm_mix_g", "b_forget", "lru_conv_b", "lru_w_a", "lru_b_a", "lru_w_x", "lru_b_x", "lru_lambda",
              "norm_cross_g", "norm_mem_g", "norm_ffn_g", "ffn_conv_b", "rel_bias", "final_norm_g")
WEIGHTS = ('norm_mix_g', 'w_in', 'b_forget', 'lru_conv_w', 'lru_conv_b', 'lru_w_a', 'lru_b_a', 'lru_w_x', 'lru_b_x',
           'lru_lambda', 'w_out', 'norm_cross_g', 'norm_mem_g', 'w_cq', 'w_ck', 'w_cv', 'w_co', 'norm_ffn_g', 'w_up',
           'ffn_conv_w', 'ffn_conv_b', 'w_down', 'rel_bias', 'final_norm_g')
INPUTS = ("x", "mem") + WEIGHTS + ("loss_target",) + tuple("m_" + n for n in WEIGHTS) + tuple("v_" + n for n in WEIGHTS)


def _round_up(n, m):
    return -(-n // m) * m


class _Packing:
    def __init__(self, entries):
        self.entries, self.off = entries, {}
        o = 0
        for name, shape in entries:
            self.off[name] = o
            o += _round_up(int(np.prod(shape)), ROW)
        self.used = o
        self.rows = _round_up(o // ROW, SUBLANES)

    def pack(self, arrays):
        parts = []
        for name, shape in self.entries:
            n = int(np.prod(shape))
            parts.append(jnp.pad(arrays[name].reshape(n), (0, _round_up(n, ROW) - n)))
        tail = self.rows * ROW - self.used
        if tail:
            parts.append(jnp.zeros((tail,), F32))
        return jnp.concatenate(parts).reshape(self.rows, ROW)

    def unpack(self, flat, lead=()):
        flat = flat.reshape(lead + (self.rows * ROW,))
        out = {}
        for name, shape in self.entries:
            n = int(np.prod(shape))
            out[name] = lax.slice_in_dim(flat, self.off[name], self.off[name] + n, axis=len(lead)).reshape(
                lead + tuple(shape))
        return out


def _to_shards(g, axis):
    r, c = g.shape
    if axis == 1:
        return g.reshape(4, r // 4, c)
    return g.reshape(r, 4, c // 4).transpose(1, 0, 2)


def _from_shards(s, axis):
    _, nl, r, c = s.shape
    if axis == 1:
        return s.transpose(1, 0, 2, 3).reshape(nl, 4 * r, c)
    return s.transpose(1, 2, 0, 3).reshape(nl, r, 4 * c)


def _proj_blocks():
    blocks = []
    for mixer in (MIX_SB, MIX_FOX, MIX_DIL):
        for p in range(2):
            blocks += [ORIG_COL[mixer] + part * 2 * LANES + p * LANES for part in range(3)]
    for hf in range(2):
        blocks += [ORIG_LRU_X + hf * LANES, ORIG_LRU_G + hf * LANES]
    return blocks


def _pad_w_in(w):
    parts = [w[..., s:s + LANES] for s in _proj_blocks()]
    parts += [w[..., 1536:1540], jnp.zeros(w.shape[:-1] + (PROJ_W - COL_F - N_HEADS,), w.dtype)]
    return jnp.concatenate(parts, axis=-1)


def _unpad_w_in(wp):
    blocks = _proj_blocks()
    order = sorted(range(len(blocks)), key=lambda i: blocks[i])
    parts = []
    for i in order:
        if blocks[i] == ORIG_COL[MIX_DIL]:
            parts.append(wp[..., COL_F:COL_F + N_HEADS])
        parts.append(wp[..., i * LANES:(i + 1) * LANES])
    return jnp.concatenate(parts, axis=-1)


def _block_diag(w):
    z = jnp.zeros((HEAD_DIM, HEAD_DIM), w.dtype)
    half = lambda a, b: jnp.concatenate([jnp.concatenate([a, z], 1), jnp.concatenate([z, b], 1)], 0)
    return jnp.stack([half(w[0], w[1]), half(w[2], w[3])])


def _block_diag_grad(d):
    return jnp.stack([d[0, :HEAD_DIM, :HEAD_DIM], d[0, HEAD_DIM:, HEAD_DIM:],
                      d[1, :HEAD_DIM, :HEAD_DIM], d[1, HEAD_DIM:, HEAD_DIM:]])


def _fox_layouts(cum, nb, seq):
    tk = min(ATT_TILE, seq)
    col = cum.reshape(nb, 2, 2, seq).transpose(0, 1, 3, 2)
    row = cum.reshape(nb, 2, 2, seq // tk, tk).transpose(0, 1, 3, 2, 4)
    return col, row


def _layer_params(w, l, nb):
    lru_vec = jnp.concatenate([w["lru_conv_w"][l], w["lru_conv_b"][l][None], w["lru_b_a"][l][None],
                               w["lru_b_x"][l][None], w["lru_lambda"][l][None]], axis=0)
    ffn_cw = jnp.concatenate([w["ffn_conv_w"][l], w["ffn_conv_b"][l][None],
                              jnp.zeros((SUBLANES - 4, 2 * D_FF), F32)], axis=0)
    return dict(
        w_in=w["w_in_padded"][l], lru_vec=lru_vec,
        wa=_block_diag(w["lru_w_a"][l]).astype(BF16), wx=_block_diag(w["lru_w_x"][l]).astype(BF16),
        ffn_cw=ffn_cw, b_rows=jnp.tile(w["b_forget"][l], nb).reshape(nb * N_HEADS, 1))


def _layer_fwd(x, mem, w, lp, l, bias, nb):
    t, d = x.shape
    seq = t // nb
    tag = f"l{l}"
    sv = dict(x0=x)
    h = _rmsnorm(x, w["norm_mix_g"][l], name=tag + "_norm_mix")
    proj = _mm(h, lp["w_in"], name=tag + "_proj")
    mixed, ltot = _sb_attn_fwd(proj, lax.empty((t, d), F32), nb=nb, name=tag + "_sb_fwd")
    f_rows = proj[:, COL_F:COL_F + N_HEADS].reshape(nb, seq, N_HEADS).transpose(0, 2, 1).reshape(nb * N_HEADS, seq)
    cum_col, cum_row = _fox_layouts(_fox_gate_fwd(f_rows, lp["b_rows"]), nb, seq)
    mixed, lse_fox = _softmax_attn_fwd(proj, nb=nb, mode="fox", mixer=MIX_FOX, out_buf=mixed,
                                       extra=(cum_col, cum_row), name=tag + "_fox_fwd")
    mixed, lse_dil = _softmax_attn_fwd(proj, nb=nb, mode="dil", mixer=MIX_DIL, out_buf=mixed, extra=(bias,),
                                       name=tag + "_dil_fwd")
    mixed = _lru_fwd(proj, lp["lru_vec"], lp["wa"], lp["wx"], mixed, nb=nb, name=tag + "_lru_fwd")
    x1 = _mm(mixed, w["w_out"][l], res=x, name=tag + "_out")
    hq = _rmsnorm(x1, w["norm_cross_g"][l], name=tag + "_norm_cross")
    memn = _rmsnorm(mem, w["norm_mem_g"][l], name=tag + "_norm_mem")
    q = _mm(hq, w["w_cq"][l], name=tag + "_cq")
    k = _mm(memn, w["w_ck"][l], name=tag + "_ck")
    v = _mm(memn, w["w_cv"][l], name=tag + "_cv")
    oc, lse_c = _softmax_attn_fwd((q, k, v), nb=nb, mode="cross", name=tag + "_cross_fwd")
    x2 = _mm(oc, w["w_co"][l], res=x1, name=tag + "_co")
    hn = _rmsnorm(x2, w["norm_ffn_g"][l], name=tag + "_norm_ffn")
    hf = _mm(hn, w["w_up"][l], name=tag + "_up")
    act = _ffn_act(hf, lp["ffn_cw"], seq=seq, name=tag + "_ffn_act")
    x3 = _mm(act, w["w_down"][l], res=x2, name=tag + "_down")
    sv.update(h=h, proj=proj, ltot=ltot, f_rows=f_rows, cum_col=cum_col, cum_row=cum_row, lse_fox=lse_fox,
              lse_dil=lse_dil, mixed=mixed, x1=x1, hq=hq, memn=memn, q=q, k=k, v=v, oc=oc, lse_c=lse_c, x2=x2,
              hn=hn, hf=hf, act=act)
    return x3, sv


def _layer_bwd(dx3, mem, sv, w, lp, l, bias, nb):
    t = dx3.shape[0]
    seq = t // nb
    tag = f"l{l}"
    g = {}
    g["w_down"] = _mm(sv["act"], dx3, ta=True, name=tag + "_dw_down")
    dact = _mm(dx3, w["w_down"][l], tb=True, name=tag + "_dact")
    dhfc, dcu, dcg = _ffn_act_bwd(sv["hf"], lp["ffn_cw"], dact, seq=seq, name=tag + "_ffn_act_bwd")
    dcw = jnp.concatenate([dcu, dcg], axis=1)
    g["ffn_conv_w"], g["ffn_conv_b"] = dcw[:3], dcw[3]
    dhf = _conv_transpose(dhfc, lp["ffn_cw"], seq=seq, name=tag + "_conv_t")
    g["w_up"] = _mm(sv["hn"], dhf, ta=True, col_shards=4, name=tag + "_dw_up")
    dhn = _mm(dhf, w["w_up"][l], tb=True, name=tag + "_dhn")
    dx2, g["norm_ffn_g"] = _rmsnorm_bwd(dhn, sv["x2"], w["norm_ffn_g"][l], dx3, name=tag + "_norm_ffn_bwd")
    g["w_co"] = _mm(sv["oc"], dx2, ta=True, col_shards=4, name=tag + "_dw_co")
    doc = _mm(dx2, w["w_co"][l], tb=True, name=tag + "_doc")
    dq, dk, dv = _softmax_attn_bwd((sv["q"], sv["k"], sv["v"]), sv["oc"], sv["lse_c"], doc, nb=nb, mode="cross",
                                   name=tag + "_cross_bwd")
    g["w_cq"] = _mm(sv["hq"], dq, ta=True, name=tag + "_dw_cq")
    g["w_ck"] = _mm(sv["memn"], dk, ta=True, name=tag + "_dw_ck")
    g["w_cv"] = _mm(sv["memn"], dv, ta=True, name=tag + "_dw_cv")
    dhq = _mm(dq, w["w_cq"][l], tb=True, name=tag + "_dhq")
    dmemn = _mm(dv, w["w_cv"][l], tb=True, res=_mm(dk, w["w_ck"][l], tb=True, name=tag + "_dmem_k"),
                name=tag + "_dmem_v")
    _, g["norm_mem_g"] = _rmsnorm_bwd(dmemn, mem, w["norm_mem_g"][l], None, name=tag + "_norm_mem_bwd")
    dx1, g["norm_cross_g"] = _rmsnorm_bwd(dhq, sv["x1"], w["norm_cross_g"][l], dx2, name=tag + "_norm_cross_bwd")
    mixed, proj = sv["mixed"], sv["proj"]
    g["w_out"] = _mm(mixed, dx1, ta=True, name=tag + "_dw_out")
    dmixed = _mm(dx1, w["w_out"][l], tb=True, name=tag + "_dmixed")
    dproj = _sb_attn_bwd(proj, sv["ltot"], dmixed, lax.empty((t, PROJ_W), F32), nb=nb, name=tag + "_sb_bwd")
    dproj, dcum_k, dcum_q = _softmax_attn_bwd(
        proj, mixed, sv["lse_fox"], dmixed, nb=nb, mode="fox", mixer=MIX_FOX, dbuf=dproj,
        extra=(sv["cum_col"], sv["cum_row"]), name=tag + "_fox_bwd")
    dcum = (dcum_k.transpose(0, 1, 3, 2, 4).reshape(nb * N_HEADS, seq)
            + dcum_q.transpose(0, 1, 3, 2).reshape(nb * N_HEADS, seq))
    df_rows, db = _fox_gate_bwd(dcum, sv["f_rows"], lp["b_rows"])
    g["b_forget"] = db[:N_HEADS, 0]
    df = df_rows.reshape(nb, N_HEADS, seq).transpose(0, 2, 1).reshape(t, N_HEADS)
    dproj, dbias = _softmax_attn_bwd(proj, mixed, sv["lse_dil"], dmixed, nb=nb, mode="dil", mixer=MIX_DIL,
                                     dbuf=dproj, extra=(bias,), name=tag + "_dil_bwd")
    dproj, dvec, dwa, dwx = _lru_bwd(proj, lp["lru_vec"], lp["wa"], lp["wx"], dmixed, dproj, nb=nb,
                                     name=tag + "_lru_bwd")
    g["lru_conv_w"], g["lru_conv_b"], g["lru_b_a"], g["lru_b_x"], g["lru_lambda"] = (
        dvec[0:4], dvec[4], dvec[5], dvec[6], dvec[7])
    g["lru_w_a"], g["lru_w_x"] = _block_diag_grad(dwa), _block_diag_grad(dwx)
    dproj = lax.dynamic_update_slice(dproj, jnp.pad(df, ((0, 0), (0, PROJ_W - COL_F - N_HEADS))), (0, COL_F))
    g["w_in_padded"] = _mm(sv["h"], dproj, ta=True, name=tag + "_dw_in")
    dh = _mm(dproj, lp["w_in"], tb=True, name=tag + "_dh")
    dx0, g["norm_mix_g"] = _rmsnorm_bwd(dh, sv["x0"], w["norm_mix_g"][l], dx1, name=tag + "_norm_mix_bwd")
    return dx0, g, dbias


def _big_grad_shards(g):
    out = []
    for n, axis in BIG:
        if n in ("w_up", "w_co"):
            out.append(g[n])
        else:
            out.append(_to_shards(_unpad_w_in(g["w_in_padded"]) if n == "w_in" else g[n], axis))
    return out


def kernel(*args):
    a = dict(zip(INPUTS, args, strict=True))
    nb, seq, d = a["x"].shape
    depth = a["norm_mix_g"].shape[0]
    x = a["x"].reshape(nb * seq, d)
    mem = a["mem"].reshape(nb * a["mem"].shape[1], d)
    target = a["loss_target"].reshape(nb * seq, d)
    cx, cy, c = _mesh_pos()
    chip = 2 * cx + cy
    pos = jnp.stack([chip, c]).astype(jnp.int32)

    slots = []
    for n, _ in BIG:
        own = a[n].astype(BF16)[None]
        slots.append(lax.dynamic_update_slice(lax.empty((4,) + own.shape[1:], BF16), own, (chip,) + (0,) * (own.ndim - 1)))
    gathered = _gather_weights(slots, name="gather_weights")
    w = {n: a[n] for n in REPLICATED}
    for (n, axis), buf in zip(BIG, gathered):
        w[n] = _from_shards(buf, axis)
    w["w_in_padded"] = _pad_w_in(w["w_in"])
    cpk = _Packing([(n, a[n].shape) for n in CONV])
    conv = cpk.unpack(_chip_bcast(cpk.pack({n: a[n] for n in CONV}), name="gather_conv"), lead=(4,))
    for n in CONV:
        w[n] = jnp.moveaxis(conv[n], 0, 2).reshape(a[n].shape[:2] + (4 * a[n].shape[2],))

    bias = _dil_bias(w["rel_bias"], seq)
    lps = [_layer_params(w, l, nb) for l in range(depth)]
    saved = []
    for l in range(depth):
        x, sv = _layer_fwd(x, mem, w, lps[l], l, bias, nb)
        saved.append(sv)
    loss, dx, dg_final = _loss_head(x, w["final_norm_g"], target)
    small_g = [None] * depth
    dbias = None
    g_shard = [lax.empty(a[n].shape, F32) for n, _ in BIG]
    for l in reversed(range(depth)):
        dx, g, db = _layer_bwd(dx, mem, saved[l], w, lps[l], l, bias, nb)
        dbias = db if dbias is None else dbias + db
        small_g[l] = g
        full = _big_grad_shards(g)
        from_sibling = _reduce_sibling(full, name=f"l{l}_reduce_sibling")
        partial = [_add_own_half(f, r, pos, name=f"l{l}_reduce_add_{n}")
                   for f, r, (n, _) in zip(full, from_sibling, BIG)]
        others = _reduce_chips(partial, name=f"l{l}_reduce_chips")
        g_shard = [_sum_into(p, o, buf, pos, l, name=f"l{l}_reduce_sum_{n}")
                   for p, o, buf, (n, _) in zip(partial, others, g_shard, BIG)]
    g_shard = _share_halves(g_shard, name="reduce_share")
    out = {}
    for (n, _), gs in zip(BIG, g_shard):
        delta, new_m, new_v = _adamw(a[n], gs, a["m_" + n], a["v_" + n], name="adamw_" + n)
        out[n] = (gs, delta, new_m, new_v)

    grads = {n: jnp.stack([small_g[l][n] for l in range(depth)]) for n in REPLICATED + CONV
             if n not in ("rel_bias", "final_norm_g")}
    grads["rel_bias"] = _dil_bias_bwd(dbias, seq)
    grads["final_norm_g"] = dg_final
    grads["loss"] = loss.reshape(1)
    spk = _Packing([(n, grads[n].shape) for n in REPLICATED + CONV + ("loss",)])
    s_all = _chip_bcast(_sibling_pair(spk.pack(grads), name="small_sibling"), name="small_chips")
    total = spk.unpack(_sum_slots(s_all, name="small_sum"))
    for n in CONV:
        width = a[n].shape[2]
        total[n] = lax.dynamic_slice_in_dim(total[n], chip * width, width, axis=2)
    apk = _Packing([(n, a[n].shape) for n in REPLICATED + CONV])
    s_out = _adamw(*[apk.pack(src)[None] for src in (
        {n: a[n] for n in REPLICATED + CONV}, total, {n: a["m_" + n] for n in REPLICATED + CONV},
        {n: a["v_" + n] for n in REPLICATED + CONV})], name="adamw_small")
    s_delta, s_m, s_v = [apk.unpack(o[0]) for o in s_out]
    for n in REPLICATED + CONV:
        out[n] = (total[n], s_delta[n], s_m[n], s_v[n])

    return (total["loss"].reshape(()), dx.reshape(nb, seq, d), *[out[n][0] for n in WEIGHTS],
            *[out[n][1] for n in WEIGHTS], *[out[n][2] for n in WEIGHTS], *[out[n][3] for n in WEIGHTS])
```

```python
import math

import numpy as np
import jax
import jax.numpy as jnp
from jax import lax
from jax.experimental import pallas as pl
from jax.experimental.pallas import tpu as pltpu

F32 = jnp.float32
BF16 = jnp.bfloat16

HEAD_DIM = 64
N_HEADS = 4
N_IN = 2820
D_FF = 2816
LRU_C = 8.0
EPS = 1e-6
NUM_BUCKETS = 32
MAX_DISTANCE = 2048
DILATED_PATTERNS = ((128, 1), (512, 4), (2048, 16))
ADAM_LR, ADAM_B1, ADAM_B2, ADAM_EPS, ADAM_WD, ADAM_STEP = 0.001, 0.9, 0.999, 1e-08, 0.01, 10

LANES = 128
SUBLANES = 8
VMEM_LIMIT = 48 * 1024 * 1024

PROJ_W = 3072
PAIR_W = 3 * LANES
LRU_W = 2 * LANES
COL_LRU = 6 * PAIR_W
COL_F = COL_LRU + 2 * LRU_W
MIX_SB, MIX_FOX, MIX_DIL, MIX_LRU = 0, 1, 2, 3
ORIG_COL = {MIX_SB: 0, MIX_FOX: 768, MIX_DIL: 1540}
ORIG_LRU_X, ORIG_LRU_G = 2308, 2564

ATT_TILE = 256
MASKED = -1e30
SCALE = HEAD_DIM ** -0.5

NT_DIMS = (((1,), (1,)), ((), ()))
TN_DIMS = (((0,), (0,)), ((), ()))

MESH = pl.DeviceIdType.MESH
ANY = pl.BlockSpec(memory_space=pl.ANY)


def _params(sem):
    return pltpu.CompilerParams(dimension_semantics=sem, vmem_limit_bytes=VMEM_LIMIT)


def _tile(n, target, unit=LANES):
    if n <= target:
        return n
    t = (target // unit) * unit
    while t > unit and n % t:
        t -= unit
    assert n % t == 0, (n, target, unit)
    return t


def _mm(a, b, *, ta=False, tb=False, res=None, col_shards=1, halves=None, norm_g=None, norm_bwd=None, name,
        ti=1024, tj=1408, tc=1408):
    if halves == "a":
        m, kc = a.shape[1], 2 * a.shape[2]
    else:
        m, kc = (a.shape[1], a.shape[0]) if ta else a.shape
    if halves == "b":
        n = 2 * b.shape[2]
        assert b.shape[1] == kc
    else:
        n = b.shape[0] if tb else b.shape[1]
        assert (b.shape[1] if tb else b.shape[0]) == kc
    assert n % col_shards == 0
    n_blk = n // (2 if halves == "b" else col_shards)
    k_blk = kc // 2 if halves == "a" else kc
    ti, tj, tc = (_tile(m, ti, LANES if ta else SUBLANES), _tile(n_blk, tj),
                  _tile(k_blk, tc, SUBLANES if ta and tb else LANES))
    per_shard, per_half_j, per_half_k = n // col_shards // tj, n_blk // tj, k_blk // tc
    nk = kc // tc
    dims = (((0 if ta else 1,), (1 if tb else 0,)), ((), ()))
    rows_whole = norm_g is not None or norm_bwd is not None
    assert not rows_whole or (tj == n and col_shards == 1)
    n_extra = (res is not None) + (norm_g is not None) + (3 if norm_bwd is not None else 0)
    n_out = 2 if rows_whole else 1

    def finish(val, ex, outs):
        if res is not None:
            val = ex[0][...] + val
        if norm_g is not None:
            outs[0][...] = val
            outs[1][...] = (_xhat(val) * ex[-1][...]).astype(BF16)
        elif norm_bwd is not None:
            x_ref, g_ref, r_ref = ex[-3:]
            dx, dgr = _norm_bwd_rows(val, x_ref[...], g_ref[...])
            outs[0][...] = r_ref[...] + dx

            @pl.when(pl.program_id(0) == 0)
            def _():
                outs[1][...] = jnp.zeros_like(outs[1])

            outs[1][...] += jnp.sum(dgr, axis=0, keepdims=True)
        else:
            outs[0][...] = val

    def body(*refs):
        a_ref, b_ref = refs[:2]
        ex = refs[2:2 + n_extra]
        outs = refs[2 + n_extra:2 + n_extra + n_out]
        part = lax.dot_general(a_ref[...].astype(BF16), b_ref[...].astype(BF16), dims, preferred_element_type=F32)
        if nk == 1:
            finish(part, ex, outs)
            return
        acc_ref = refs[-1]
        k = pl.program_id(2)

        @pl.when(k == 0)
        def _():
            acc_ref[...] = part

        @pl.when(k > 0)
        def _():
            acc_ref[...] += part

        @pl.when(k == nk - 1)
        def _():
            finish(acc_ref[...], ex, outs)

    if halves == "a":
        a_spec = pl.BlockSpec((None, ti, tc), lambda i, j, k: (k // per_half_k, i, k % per_half_k))
    elif ta:
        a_spec = pl.BlockSpec((tc, ti), lambda i, j, k: (k, i))
    else:
        a_spec = pl.BlockSpec((ti, tc), lambda i, j, k: (i, k))
    if halves == "b":
        b_spec = pl.BlockSpec((None, tc, tj), lambda i, j, k: (j // per_half_j, k, j % per_half_j))
    elif tb:
        b_spec = pl.BlockSpec((tj, tc), lambda i, j, k: (j, k))
    else:
        b_spec = pl.BlockSpec((tc, tj), lambda i, j, k: (k, j))
    o_spec = pl.BlockSpec((ti, tj), lambda i, j, k: (i, j))
    vec = pl.BlockSpec((1, tj), lambda i, j, k: (0, 0))
    in_specs, args = [a_spec, b_spec], [a, b]
    out_specs, out_shape = [o_spec], [jax.ShapeDtypeStruct((m, n), F32)]
    if res is not None:
        in_specs.append(o_spec)
        args.append(res)
    if norm_g is not None:
        in_specs.append(vec)
        args.append(norm_g.reshape(1, n))
        out_specs.append(o_spec)
        out_shape.append(jax.ShapeDtypeStruct((m, n), BF16))
    if norm_bwd is not None:
        x, g, dres = norm_bwd
        in_specs += [o_spec, vec, o_spec]
        args += [x, g.reshape(1, n), dres]
        out_specs.append(vec)
        out_shape.append(jax.ShapeDtypeStruct((1, n), F32))
    if col_shards > 1:
        assert n_extra == 0
        out_specs = [pl.BlockSpec((None, ti, tj), lambda i, j, k: (j // per_shard, i, j % per_shard))]
        out_shape = [jax.ShapeDtypeStruct((col_shards, m, n // col_shards), F32)]
    sem = ("arbitrary",) * 3 if norm_bwd is not None else ("parallel", "parallel", "arbitrary")
    out = pl.pallas_call(
        body, name=name, grid=(m // ti, n // tj, nk), in_specs=in_specs, out_specs=out_specs, out_shape=out_shape,
        scratch_shapes=[] if nk == 1 else [pltpu.VMEM((ti, tj), F32)], compiler_params=_params(sem))(*args)
    return out if rows_whole else out[0]


def _xhat(x):
    return x * lax.rsqrt(jnp.mean(x * x, axis=-1, keepdims=True) + EPS)


def _norm_bwd_rows(dy, x, g):
    rstd = lax.rsqrt(jnp.mean(x * x, axis=-1, keepdims=True) + EPS)
    xh = x * rstd
    dxh = dy * g
    dx = rstd * (dxh - xh * jnp.mean(dxh * xh, axis=-1, keepdims=True))
    return dx, dy * xh


def _rmsnorm(x, g, *, name, rows=512):
    t, d = x.shape
    tr = _tile(t, rows, 2 * SUBLANES)

    def body(x_ref, g_ref, o_ref):
        o_ref[...] = (_xhat(x_ref[...]) * g_ref[...]).astype(BF16)

    return pl.pallas_call(
        body, name=name, grid=(t // tr,),
        in_specs=[pl.BlockSpec((tr, d), lambda i: (i, 0)), pl.BlockSpec((1, d), lambda i: (0, 0))],
        out_specs=pl.BlockSpec((tr, d), lambda i: (i, 0)), out_shape=jax.ShapeDtypeStruct((t, d), BF16),
        compiler_params=_params(("parallel",)))(x, g.reshape(1, d))


def _rmsnorm_bwd(dy, x, g, dres, *, name, rows=512):
    t, d = x.shape
    tr = _tile(t, rows, SUBLANES)

    def body(*refs):
        if dres is None:
            dy_ref, x_ref, g_ref, dx_ref, dg_ref = refs
        else:
            dy_ref, x_ref, g_ref, r_ref, dx_ref, dg_ref = refs
        dx, dgr = _norm_bwd_rows(dy_ref[...], x_ref[...], g_ref[...])
        dx_ref[...] = dx if dres is None else r_ref[...] + dx

        @pl.when(pl.program_id(0) == 0)
        def _():
            dg_ref[...] = jnp.zeros_like(dg_ref)

        dg_ref[...] += jnp.sum(dgr, axis=0, keepdims=True)

    row = pl.BlockSpec((tr, d), lambda i: (i, 0))
    vec = pl.BlockSpec((1, d), lambda i: (0, 0))
    in_specs = [row, row, vec] + ([] if dres is None else [row])
    args = (dy, x, g.reshape(1, d)) + (() if dres is None else (dres,))
    dx, dg = pl.pallas_call(
        body, name=name, grid=(t // tr,), in_specs=in_specs, out_specs=[row, vec],
        out_shape=[jax.ShapeDtypeStruct((t, d), F32), jax.ShapeDtypeStruct((1, d), F32)],
        compiler_params=_params(("arbitrary",)))(*args)
    return dx, dg.reshape(d)


def _loss_head(x, g, target, *, rows=512):
    t, d = x.shape
    tr = _tile(t, rows, SUBLANES)

    def body(x_ref, g_ref, t_ref, dx_ref, dg_ref, loss_ref):
        x_, g_ = x_ref[...], g_ref[...]
        err = _xhat(x_) * g_ - t_ref[...]
        dx, dgr = _norm_bwd_rows(err * (1.0 / d), x_, g_)
        dx_ref[...] = dx

        @pl.when(pl.program_id(0) == 0)
        def _():
            dg_ref[...] = jnp.zeros_like(dg_ref)
            loss_ref[...] = jnp.zeros_like(loss_ref)

        dg_ref[...] += jnp.sum(dgr, axis=0, keepdims=True)
        loss_ref[...] += 0.5 * jnp.sum(jnp.mean(err * err, axis=-1, keepdims=True), axis=0, keepdims=True)

    row = pl.BlockSpec((tr, d), lambda i: (i, 0))
    vec = pl.BlockSpec((1, d), lambda i: (0, 0))
    one = pl.BlockSpec((1, 1), lambda i: (0, 0))
    dx, dg, loss = pl.pallas_call(
        body, name="loss_head", grid=(t // tr,), in_specs=[row, vec, row], out_specs=[row, vec, one],
        out_shape=[jax.ShapeDtypeStruct((t, d), F32), jax.ShapeDtypeStruct((1, d), F32),
                   jax.ShapeDtypeStruct((1, 1), F32)],
        compiler_params=_params(("arbitrary",)))(x, g.reshape(1, d), target)
    return loss.reshape(()), dx, dg.reshape(d)


def _head_masks(shape):
    lane = lax.broadcasted_iota(jnp.int32, shape, len(shape) - 1)
    return lane < HEAD_DIM, lane >= HEAD_DIM


def _split_heads(x):
    m0, m1 = _head_masks(x.shape)
    zero = jnp.zeros_like(x)
    return jnp.where(m0, x, zero), jnp.where(m1, x, zero)


def _lane_pair(a0, a1, rows):
    m0, _ = _head_masks((rows, LANES))
    return jnp.where(m0, a0, a1)


def _qkv_readers(refs, packed):
    if packed:
        (r,) = refs
        return tuple((lambda r0, n, s=s: r[pl.ds(r0, n), s * LANES:(s + 1) * LANES]) for s in range(3))
    return tuple((lambda r0, n, ref=ref: ref[pl.ds(r0, n), :]) for ref in refs)


def _pair_spec(seq, col0, width=LANES):
    return pl.BlockSpec((seq, width), lambda p, b: (b, col0 + p))


def _fox_specs(seq, nk, tk):
    return [pl.BlockSpec((None, None, seq, 2), lambda p, b: (b, p, 0, 0)),
            pl.BlockSpec((None, None, nk, 2, tk), lambda p, b: (b, p, 0, 0, 0))]


def _softmax_attn_fwd(src, *, nb, mode, mixer=None, out_buf=None, extra=(), name):
    packed = mode != "cross"
    n_src = 1 if packed else 3
    seq_q = (src if packed else src[0]).shape[0] // nb
    seq_k = seq_q if packed else src[1].shape[0] // nb
    tq, tk = min(ATT_TILE, seq_q), min(ATT_TILE, seq_k)
    nq, nk = seq_q // tq, seq_k // tk
    n_ex = len(extra)

    def body(*refs):
        q_at, k_at, v_at = _qkv_readers(refs[:n_src], packed)
        ex = refs[n_src:n_src + n_ex]
        o_ref, lse_ref = refs[-2:]

        def q_tile(i, _):
            r0 = pl.multiple_of(i * tq, tq)
            qm = _split_heads((q_at(r0, tq) * SCALE).astype(BF16))
            if mode == "fox":
                cq = ex[0][pl.ds(r0, tq), :]
                row = r0 + lax.broadcasted_iota(jnp.int32, (tq, tk), 0)

            def k_tile(j, carry, diagonal=False):
                m, l, acc = carry
                c0 = pl.multiple_of(j * tk, tk)
                kt = k_at(c0, tk).astype(BF16)
                vm = _split_heads(v_at(c0, tk).astype(BF16))
                if mode == "fox":
                    ck = ex[1][j]
                hs = range(2)
                s = [lax.dot_general(qm[h], kt, NT_DIMS, preferred_element_type=F32) for h in hs]
                if mode == "fox":
                    s = [s[h] + cq[:, h:h + 1] - ck[h:h + 1, :] for h in hs]
                    if diagonal:
                        keep = (c0 + lax.broadcasted_iota(jnp.int32, (tq, tk), 1)) <= row
                        s = [jnp.where(keep, s[h], MASKED) for h in hs]
                elif mode == "dil":
                    s = [s[h] + ex[0][h, i - j] for h in hs]
                new_m = [jnp.maximum(m[h], jnp.max(s[h], axis=-1, keepdims=True)) for h in hs]
                p = [jnp.exp(s[h] - new_m[h]) for h in hs]
                alpha = [jnp.exp(m[h] - new_m[h]) for h in hs]
                new_l = [alpha[h] * l[h] + jnp.sum(p[h], axis=-1, keepdims=True) for h in hs]
                pv = [jnp.dot(p[h].astype(BF16), vm[h], preferred_element_type=F32) for h in hs]
                acc = acc * _lane_pair(alpha[0], alpha[1], tq) + (pv[0] + pv[1])
                return tuple(new_m), tuple(new_l), acc

            init = ((jnp.full((tq, 1), MASKED, F32),) * 2, (jnp.zeros((tq, 1), F32),) * 2,
                    jnp.zeros((tq, LANES), F32))
            if mode == "fox":
                m, l, acc = k_tile(i, lax.fori_loop(0, i, k_tile, init), True)
            else:
                m, l, acc = lax.fori_loop(0, i + 1 if packed else nk, k_tile, init)
            o_ref[pl.ds(r0, tq), :] = acc / _lane_pair(l[0], l[1], tq)
            lse_ref[pl.ds(r0, tq), :] = _lane_pair(m[0] + jnp.log(l[0]), m[1] + jnp.log(l[1]), tq)
            return 0

        lax.fori_loop(0, nq, q_tile, 0)

    lse_shape = jax.ShapeDtypeStruct((nb * seq_q, 2 * LANES), F32)
    if packed:
        in_specs, args = [_pair_spec(seq_q, 2 * mixer, PAIR_W)], [src]
        in_specs += _fox_specs(seq_q, nk, tk) if mode == "fox" else [
            pl.BlockSpec((None, 2, nq, tq, tk), lambda p, b: (p, 0, 0, 0, 0))]
        args += list(extra) + [out_buf]
        in_specs.append(ANY)
        out_specs = [_pair_spec(seq_q, 2 * mixer), _pair_spec(seq_q, 0)]
        out_shape = [jax.ShapeDtypeStruct(out_buf.shape, F32), lse_shape]
        aliases = {len(args) - 1: 0}
    else:
        in_specs = [_pair_spec(seq_q, 0), _pair_spec(seq_k, 0), _pair_spec(seq_k, 0)]
        args = list(src)
        out_specs = [_pair_spec(seq_q, 0), _pair_spec(seq_q, 0)]
        out_shape = [lse_shape, lse_shape]
        aliases = {}
    return pl.pallas_call(
        body, name=name, grid=(2, nb), in_specs=in_specs, out_specs=out_specs, out_shape=out_shape,
        input_output_aliases=aliases, compiler_params=_params(("parallel", "arbitrary")))(*args)


def _softmax_attn_bwd(src, o, lse, do, *, nb, mode, mixer=None, dbuf=None, extra=(), name):
    packed = mode != "cross"
    n_src = 1 if packed else 3
    seq_q = (src if packed else src[0]).shape[0] // nb
    seq_k = seq_q if packed else src[1].shape[0] // nb
    tq, tk = min(ATT_TILE, seq_q), min(ATT_TILE, seq_k)
    nq, nk = seq_q // tq, seq_k // tk
    n_ex = len(extra)
    n_in = n_src + 3 + n_ex + (1 if packed else 0)

    def body(*refs):
        q_at, k_at, v_at = _qkv_readers(refs[:n_src], packed)
        o_ref, lse_ref, do_ref = refs[n_src:n_src + 3]
        ex = refs[n_src + 3:n_src + 3 + n_ex]
        outs = refs[n_in:]
        if packed:
            d_ref = outs[0]
            dq_w = lambda r0, val: d_ref.__setitem__((pl.ds(r0, tq), slice(0, LANES)), val)
            dk_ref = d_ref.at[:, LANES:2 * LANES]
            dv_ref = d_ref.at[:, 2 * LANES:3 * LANES]
        else:
            dq_ref, dk_ref, dv_ref = outs[:3]
            dq_w = lambda r0, val: dq_ref.__setitem__((pl.ds(r0, tq), slice(None)), val)
        dk_ref[...] = jnp.zeros((seq_k, LANES), F32)
        dv_ref[...] = jnp.zeros((seq_k, LANES), F32)
        if mode == "fox":
            dcum_ref, dcq_ref = outs[-2:]
            dcum_ref[...] = jnp.zeros_like(dcum_ref)
        if mode == "dil":
            dbias_ref = outs[-1]

            @pl.when(pl.program_id(1) == 0)
            def _():
                dbias_ref[...] = jnp.zeros_like(dbias_ref)

        def q_tile(i, _):
            r0 = pl.multiple_of(i * tq, tq)
            qm = _split_heads((q_at(r0, tq) * SCALE).astype(BF16))
            do_f = do_ref[pl.ds(r0, tq), :]
            dom = _split_heads(do_f.astype(BF16))
            dd = _split_heads(do_f * o_ref[pl.ds(r0, tq), :])
            delta = [jnp.sum(dd[h], axis=-1, keepdims=True) for h in range(2)]
            lse_t = lse_ref[pl.ds(r0, tq), :]
            lse_h = [lse_t[:, 0:1], lse_t[:, HEAD_DIM:HEAD_DIM + 1]]
            if mode == "fox":
                cq = ex[0][pl.ds(r0, tq), :]
                row = r0 + lax.broadcasted_iota(jnp.int32, (tq, tk), 0)

            def k_tile(j, carry, diagonal=False):
                dq, rs = carry
                c0 = pl.multiple_of(j * tk, tk)
                kt = k_at(c0, tk).astype(BF16)
                vt = v_at(c0, tk).astype(BF16)
                km = _split_heads(kt)
                if mode == "fox":
                    ck = ex[1][j]
                hs = range(2)
                s = [lax.dot_general(qm[h], kt, NT_DIMS, preferred_element_type=F32) for h in hs]
                dp = [lax.dot_general(dom[h], vt, NT_DIMS, preferred_element_type=F32) for h in hs]
                if mode == "fox":
                    s = [s[h] + cq[:, h:h + 1] - ck[h:h + 1, :] for h in hs]
                    if diagonal:
                        keep = (c0 + lax.broadcasted_iota(jnp.int32, (tq, tk), 1)) <= row
                        s = [jnp.where(keep, s[h], MASKED) for h in hs]
                elif mode == "dil":
                    s = [s[h] + ex[0][h, i - j] for h in hs]
                p = [jnp.exp(s[h] - lse_h[h]) for h in hs]
                ds = [p[h] * (dp[h] - delta[h]) for h in hs]
                dsb = [ds[h].astype(BF16) for h in hs]
                pb = [p[h].astype(BF16) for h in hs]
                dq = dq + (jnp.dot(dsb[0], km[0], preferred_element_type=F32)
                           + jnp.dot(dsb[1], km[1], preferred_element_type=F32))
                dk_t = (lax.dot_general(dsb[0], qm[0], TN_DIMS, preferred_element_type=F32)
                        + lax.dot_general(dsb[1], qm[1], TN_DIMS, preferred_element_type=F32))
                dv_t = (lax.dot_general(pb[0], dom[0], TN_DIMS, preferred_element_type=F32)
                        + lax.dot_general(pb[1], dom[1], TN_DIMS, preferred_element_type=F32))
                if mode == "fox":
                    for h in hs:
                        dcum_ref[j, h:h + 1, :] -= jnp.sum(ds[h], axis=0, keepdims=True)
                    rs = tuple(rs[h] + jnp.sum(ds[h], axis=-1, keepdims=True) for h in hs)
                elif mode == "dil":
                    for h in hs:
                        dbias_ref[h, i - j] += ds[h]
                dk_ref[pl.ds(c0, tk), :] += dk_t
                dv_ref[pl.ds(c0, tk), :] += dv_t
                return dq, rs

            zero = (jnp.zeros((tq, 1), F32),) * 2
            init = (jnp.zeros((tq, LANES), F32), zero)
            if mode == "fox":
                dq, rs = k_tile(i, lax.fori_loop(0, i, k_tile, init), True)
            else:
                dq, rs = lax.fori_loop(0, i + 1 if packed else nk, k_tile, init)
            dq_w(r0, dq * SCALE)
            if mode == "fox":
                dcq_ref[pl.ds(r0, tq), :] = jnp.where(lax.broadcasted_iota(jnp.int32, (tq, 2), 1) == 0, rs[0], rs[1])
            return 0

        lax.fori_loop(0, nq, q_tile, 0)

    if packed:
        in_specs = [_pair_spec(seq_q, 2 * mixer, PAIR_W), _pair_spec(seq_q, 2 * mixer), _pair_spec(seq_q, 0),
                    _pair_spec(seq_q, 2 * mixer)]
        args = [src, o, lse, do]
        out_specs = [_pair_spec(seq_q, 2 * mixer, PAIR_W)]
        out_shape = [jax.ShapeDtypeStruct(dbuf.shape, F32)]
        if mode == "fox":
            in_specs += _fox_specs(seq_q, nk, tk)
            out_specs += [_fox_specs(seq_q, nk, tk)[1], _fox_specs(seq_q, nk, tk)[0]]
            out_shape += [jax.ShapeDtypeStruct((nb, 2, nk, 2, tk), F32), jax.ShapeDtypeStruct((nb, 2, seq_q, 2), F32)]
        else:
            tiles = pl.BlockSpec((None, 2, nq, tq, tk), lambda p, b: (p, 0, 0, 0, 0))
            in_specs.append(tiles)
            out_specs.append(tiles)
            out_shape.append(jax.ShapeDtypeStruct((2, 2, nq, tq, tk), F32))
        args += list(extra) + [dbuf]
        in_specs.append(ANY)
        aliases = {len(args) - 1: 0}
    else:
        sq, sk = _pair_spec(seq_q, 0), _pair_spec(seq_k, 0)
        in_specs, args = [sq, sk, sk, sq, sq, sq], list(src) + [o, lse, do]
        out_specs = [sq, sk, sk]
        out_shape = [jax.ShapeDtypeStruct((nb * seq_q, 2 * LANES), F32)] + [
            jax.ShapeDtypeStruct((nb * seq_k, 2 * LANES), F32)] * 2
        aliases = {}
    return pl.pallas_call(
        body, name=name, grid=(2, nb), in_specs=in_specs, out_specs=out_specs, out_shape=out_shape,
        input_output_aliases=aliases, compiler_params=_params(("parallel", "arbitrary")))(*args)


def _log_sigmoid(z):
    return jnp.minimum(z, 0.0) - jnp.log(1.0 + jnp.exp(-jnp.abs(z)))


def _split_bf16(x):
    hi = x.astype(BF16)
    return hi, (x - hi.astype(F32)).astype(BF16)


def _tri(n, fn):
    r = lax.broadcasted_iota(jnp.int32, (n, n), 0)
    c = lax.broadcasted_iota(jnp.int32, (n, n), 1)
    return jnp.where(fn(r, c), 1.0, 0.0).astype(BF16)


def _sb_attn_fwd(proj, out_buf, *, nb, name):
    seq = proj.shape[0] // nb
    tq = tk = min(ATT_TILE, seq)
    nq = seq // tq

    def body(qkv_ref, _, o_ref, lt_ref):
        q_at, k_at, v_at = _qkv_readers((qkv_ref,), True)
        after = _tri(tk, lambda r, c: r > c)

        def q_tile(i, _):
            r0 = pl.multiple_of(i * tq, tq)
            qm = _split_heads((q_at(r0, tq) * SCALE).astype(BF16))
            row = r0 + lax.broadcasted_iota(jnp.int32, (tq, tk), 0)

            def k_tile(j, carry, diagonal):
                c, acc = carry
                c0 = pl.multiple_of(j * tk, tk)
                kt = k_at(c0, tk).astype(BF16)
                vm = _split_heads(v_at(c0, tk).astype(BF16))
                if diagonal:
                    strict = (c0 + lax.broadcasted_iota(jnp.int32, (tq, tk), 1)) < row
                hs = range(2)
                z = [lax.dot_general(qm[h], kt, NT_DIMS, preferred_element_type=F32) for h in hs]
                ls = [_log_sigmoid(z[h]) for h in hs]
                lk = [ls[h] - z[h] for h in hs]
                if diagonal:
                    lk = [jnp.where(strict, lk[h], 0.0) for h in hs]
                parts = [_split_bf16(lk[h]) for h in hs]
                sfx = [jnp.dot(parts[h][0], after, preferred_element_type=F32)
                       + jnp.dot(parts[h][1], after, preferred_element_type=F32) for h in hs]
                att = [jnp.exp(ls[h] + sfx[h] + c[h]) for h in hs]
                if diagonal:
                    att = [jnp.where(strict, att[h], 0.0) for h in hs]
                acc = acc + (jnp.dot(att[0].astype(BF16), vm[0], preferred_element_type=F32)
                             + jnp.dot(att[1].astype(BF16), vm[1], preferred_element_type=F32))
                return tuple(c[h] + jnp.sum(lk[h], axis=-1, keepdims=True) for h in hs), acc

            init = ((jnp.zeros((tq, 1), F32),) * 2, jnp.zeros((tq, LANES), F32))
            c, acc = lax.fori_loop(1, i + 1, lambda jj, cr: k_tile(i - jj, cr, False), k_tile(i, init, True))
            o_ref[pl.ds(r0, tq), :] = acc
            lt_ref[pl.ds(r0, tq), :] = _lane_pair(c[0], c[1], tq)
            return 0

        lax.fori_loop(0, nq, q_tile, 0)

    return pl.pallas_call(
        body, name=name, grid=(2, nb), in_specs=[_pair_spec(seq, 2 * MIX_SB, PAIR_W), ANY],
        out_specs=[_pair_spec(seq, 2 * MIX_SB), _pair_spec(seq, 0)],
        out_shape=[jax.ShapeDtypeStruct(out_buf.shape, F32), jax.ShapeDtypeStruct((nb * seq, 2 * LANES), F32)],
        input_output_aliases={1: 0}, compiler_params=_params(("parallel", "arbitrary")))(proj, out_buf)


def _sb_attn_bwd(proj, ltot, do, dbuf, *, nb, name):
    seq = proj.shape[0] // nb
    tq = tk = min(ATT_TILE, seq)
    nq = seq // tq

    def body(qkv_ref, lt_ref, do_ref, _, d_ref):
        q_at, k_at, v_at = _qkv_readers((qkv_ref,), True)
        upto = _tri(tk, lambda r, c: r <= c)
        before = _tri(tk, lambda r, c: r < c)
        dk_ref = d_ref.at[:, LANES:2 * LANES]
        dv_ref = d_ref.at[:, 2 * LANES:3 * LANES]
        dk_ref[...] = jnp.zeros((seq, LANES), F32)
        dv_ref[...] = jnp.zeros((seq, LANES), F32)

        def q_tile(i, _):
            r0 = pl.multiple_of(i * tq, tq)
            qm = _split_heads((q_at(r0, tq) * SCALE).astype(BF16))
            dom = _split_heads(do_ref[pl.ds(r0, tq), :].astype(BF16))
            lt_t = lt_ref[pl.ds(r0, tq), :]
            lt_h = [lt_t[:, 0:1], lt_t[:, HEAD_DIM:HEAD_DIM + 1]]
            row = r0 + lax.broadcasted_iota(jnp.int32, (tq, tk), 0)

            def k_tile(j, carry, diagonal):
                pc, qc, dq = carry
                c0 = pl.multiple_of(j * tk, tk)
                kt = k_at(c0, tk).astype(BF16)
                vt = v_at(c0, tk).astype(BF16)
                km = _split_heads(kt)
                if diagonal:
                    strict = (c0 + lax.broadcasted_iota(jnp.int32, (tq, tk), 1)) < row
                hs = range(2)
                z = [lax.dot_general(qm[h], kt, NT_DIMS, preferred_element_type=F32) for h in hs]
                da = [lax.dot_general(dom[h], vt, NT_DIMS, preferred_element_type=F32) for h in hs]
                ls = [_log_sigmoid(z[h]) for h in hs]
                lk = [ls[h] - z[h] for h in hs]
                if diagonal:
                    lk = [jnp.where(strict, lk[h], 0.0) for h in hs]
                parts = [_split_bf16(lk[h]) for h in hs]
                pin = [jnp.dot(parts[h][0], upto, preferred_element_type=F32)
                       + jnp.dot(parts[h][1], upto, preferred_element_type=F32) for h in hs]
                att = [jnp.exp(ls[h] + (lt_h[h] - pc[h] - pin[h])) for h in hs]
                if diagonal:
                    att = [jnp.where(strict, att[h], 0.0) for h in hs]
                dg = [att[h] * da[h] for h in hs]
                qx = [qc[h] + jnp.dot(dg[h].astype(BF16), before, preferred_element_type=F32) for h in hs]
                sig = [jnp.exp(ls[h]) for h in hs]
                dz = [dg[h] * (1.0 - sig[h]) - sig[h] * qx[h] for h in hs]
                if diagonal:
                    dz = [jnp.where(strict, dz[h], 0.0) for h in hs]
                dzb = [dz[h].astype(BF16) for h in hs]
                attb = [att[h].astype(BF16) for h in hs]
                dq = dq + (jnp.dot(dzb[0], km[0], preferred_element_type=F32)
                           + jnp.dot(dzb[1], km[1], preferred_element_type=F32))
                dk_ref[pl.ds(c0, tk), :] += (lax.dot_general(dzb[0], qm[0], TN_DIMS, preferred_element_type=F32)
                                             + lax.dot_general(dzb[1], qm[1], TN_DIMS, preferred_element_type=F32))
                dv_ref[pl.ds(c0, tk), :] += (lax.dot_general(attb[0], dom[0], TN_DIMS, preferred_element_type=F32)
                                             + lax.dot_general(attb[1], dom[1], TN_DIMS, preferred_element_type=F32))
                return (tuple(pc[h] + jnp.sum(lk[h], axis=-1, keepdims=True) for h in hs),
                        tuple(qc[h] + jnp.sum(dg[h], axis=-1, keepdims=True) for h in hs), dq)

            zero = (jnp.zeros((tq, 1), F32),) * 2
            carry = lax.fori_loop(0, i, lambda j, cr: k_tile(j, cr, False), (zero, zero, jnp.zeros((tq, LANES), F32)))
            _, _, dq = k_tile(i, carry, True)
            d_ref[pl.ds(r0, tq), 0:LANES] = dq * SCALE
            return 0

        lax.fori_loop(0, nq, q_tile, 0)

    return pl.pallas_call(
        body, name=name, grid=(2, nb),
        in_specs=[_pair_spec(seq, 2 * MIX_SB, PAIR_W), _pair_spec(seq, 0), _pair_spec(seq, 2 * MIX_SB), ANY],
        out_specs=_pair_spec(seq, 2 * MIX_SB, PAIR_W), out_shape=jax.ShapeDtypeStruct(dbuf.shape, F32),
        input_output_aliases={3: 0}, compiler_params=_params(("parallel", "arbitrary")))(proj, ltot, do, dbuf)


def _lane_scan(x, reverse=False):
    n = x.shape[-1]
    lane = lax.broadcasted_iota(jnp.int32, x.shape, 1)
    k = 1
    while k < n:
        if reverse:
            x = x + jnp.where(lane < n - k, pltpu.roll(x, n - k, 1), 0.0)
        else:
            x = x + jnp.where(lane >= k, pltpu.roll(x, k, 1), 0.0)
        k *= 2
    return x


def _fox_gate_fwd(f_rows, b_rows):
    def body(f_ref, b_ref, o_ref):
        o_ref[...] = _lane_scan(_log_sigmoid(f_ref[...] + b_ref[...]))

    return pl.pallas_call(body, name="fox_gate_fwd", out_shape=jax.ShapeDtypeStruct(f_rows.shape, F32))(f_rows, b_rows)


def _fox_gate_bwd(dcum, f_rows, b_rows):
    def body(d_ref, f_ref, b_ref, df_ref, db_ref):
        z = f_ref[...] + b_ref[...]
        df = _lane_scan(d_ref[...], reverse=True) * jnp.exp(_log_sigmoid(-z))
        df_ref[...] = df
        rs = jnp.sum(df, axis=-1, keepdims=True)
        tot = rs
        for e in range(1, f_rows.shape[0] // N_HEADS):
            tot = tot + pltpu.roll(rs, e * N_HEADS, 0)
        db_ref[...] = tot

    return pl.pallas_call(
        body, name="fox_gate_bwd",
        out_shape=[jax.ShapeDtypeStruct(f_rows.shape, F32), jax.ShapeDtypeStruct((f_rows.shape[0], 1), F32)],
    )(dcum, f_rows, b_rows)


def _dil_tables(seq):
    t = min(ATT_TILE, seq)
    n = seq // t
    a = np.arange(t)
    d = (np.arange(n)[:, None, None] * t + a[None, :, None] - a[None, None, :]).astype(np.int64)
    count = np.zeros(d.shape, np.int64)
    for window, dil in DILATED_PATTERNS:
        count += (d >= 0) & (d % dil == 0) & (d // dil <= window // dil)
    nn = np.maximum(d, 0)
    max_exact = NUM_BUCKETS // 2
    nf = np.maximum(nn, 1).astype(np.float32)
    large = max_exact + (np.log(nf / np.float32(max_exact)) / np.float32(math.log(MAX_DISTANCE / max_exact))
                         * np.float32(NUM_BUCKETS - max_exact)).astype(np.int32)
    bucket = np.where(nn < max_exact, nn, np.minimum(large, NUM_BUCKETS - 1))
    bucket = np.where(count > 0, bucket, -1).astype(np.int32)
    logc = np.where(count > 0, np.log(np.maximum(count, 1)), MASKED).astype(np.float32)
    return bucket, logc


def _dil_bias(rel_bias, seq):
    bucket, logc = _dil_tables(seq)
    n, t, _ = bucket.shape

    def body(rb_ref, bk_ref, lc_ref, o_ref):
        h = pl.program_id(0) * 2 + pl.program_id(1)
        bk = bk_ref[...]
        out = lc_ref[...]
        for b in range(NUM_BUCKETS):
            out = jnp.where(bk == b, out + rb_ref[b, h], out)
        o_ref[...] = out

    full = pl.BlockSpec((n, t, t), lambda p, h: (0, 0, 0))
    return pl.pallas_call(
        body, name="dil_bias", grid=(2, 2),
        in_specs=[pl.BlockSpec(memory_space=pltpu.SMEM), full, full],
        out_specs=pl.BlockSpec((None, None, n, t, t), lambda p, h: (p, h, 0, 0, 0)),
        out_shape=jax.ShapeDtypeStruct((2, 2, n, t, t), F32),
        compiler_params=_params(("parallel", "parallel")))(rel_bias, jnp.asarray(bucket), jnp.asarray(logc))


def _dil_bias_bwd(dbias, seq):
    bucket, _ = _dil_tables(seq)
    n, t, _ = bucket.shape

    def body(d_ref, bk_ref, o_ref):
        bk = bk_ref[...]
        lane = lax.broadcasted_iota(jnp.int32, (1, LANES), 1)
        for b in range(NUM_BUCKETS):
            rowv = jnp.zeros((1, LANES), F32)
            for h in range(N_HEADS):
                s = jnp.sum(jnp.where(bk == b, d_ref[h // 2, h % 2], 0.0))
                rowv = jnp.where(lane == h, s, rowv)
            o_ref[b:b + 1, :] = rowv

    out = pl.pallas_call(body, name="dil_bias_bwd", out_shape=jax.ShapeDtypeStruct((NUM_BUCKETS, LANES), F32),
                         compiler_params=pltpu.CompilerParams(vmem_limit_bytes=VMEM_LIMIT))(dbias, jnp.asarray(bucket))
    return out[:, :N_HEADS]


def _shift_rows(x, k, row, fill=0.0):
    n = x.shape[0]
    if k > 0:
        return jnp.where(row >= k, pltpu.roll(x, k, 0), fill)
    return jnp.where(row < n + k, pltpu.roll(x, n + k, 0), fill)


def _row_scan(a, u, row, reverse=False):
    n = a.shape[0]
    k = 1
    while k < n:
        s = -k if reverse else k
        u = a * _shift_rows(u, s, row) + u
        a = a * _shift_rows(a, s, row, 1.0)
        k *= 2
    return u


def _sigmoid(x):
    return 1.0 / (1.0 + jnp.exp(-x))


def _gelu(g):
    return 0.5 * g * (1.0 + lax.erf(g * (2.0 ** -0.5)))


def _gelu_grad(g):
    return 0.5 * (1.0 + lax.erf(g * (2.0 ** -0.5))) + g * jnp.exp(-0.5 * g * g) * (1.0 / math.sqrt(2.0 * math.pi))


def _neg_expm1(x):
    small = -x * (1.0 + x * (0.5 + x * (1.0 / 6.0 + x * (1.0 / 24.0))))
    return jnp.where(x > -0.03, small, 1.0 - jnp.exp(x))


def _lru_core(x, vec, wa, wx, row):
    xs = [_shift_rows(x, 3 - j, row) if j < 3 else x for j in range(4)]
    xc = vec[4:5, :]
    for j in range(4):
        xc = xc + vec[j:j + 1, :] * xs[j]
    xcb = xc.astype(BF16)
    r = _sigmoid(jnp.dot(xcb, wa, preferred_element_type=F32) + vec[5:6, :])
    ig = _sigmoid(jnp.dot(xcb, wx, preferred_element_type=F32) + vec[6:7, :])
    lam = vec[7:8, :]
    sp = jnp.maximum(-lam, 0.0) - _log_sigmoid(jnp.abs(lam))
    la = -LRU_C * r * sp
    a = jnp.exp(la)
    mult = jnp.sqrt(_neg_expm1(2.0 * la))
    return xs, xc, xcb, r, ig, sp, la, a, mult


def _lru_specs(seq):
    xg = pl.BlockSpec((seq, LRU_W), lambda hf, b: (b, COL_LRU // LRU_W + hf))
    mix = pl.BlockSpec((seq, LANES), lambda hf, b: (b, 2 * MIX_LRU + hf))
    vec = pl.BlockSpec((SUBLANES, LANES), lambda hf, b: (0, hf))
    mat = pl.BlockSpec((None, LANES, LANES), lambda hf, b: (hf, 0, 0))
    return xg, mix, vec, mat


def _lru_fwd(proj, vec, wa, wx, out_buf, *, nb, name):
    seq = proj.shape[0] // nb

    def body(xg_ref, vec_ref, wa_ref, wx_ref, _, o_ref):
        row = lax.broadcasted_iota(jnp.int32, (seq, LANES), 0)
        _, xc, _, _, ig, _, _, a, mult = _lru_core(xg_ref[:, 0:LANES], vec_ref[...], wa_ref[...], wx_ref[...], row)
        h = _row_scan(a, mult * (ig * xc), row)
        o_ref[...] = h * _gelu(xg_ref[:, LANES:LRU_W])

    xg, mix, vecs, mat = _lru_specs(seq)
    return pl.pallas_call(
        body, name=name, grid=(2, nb), in_specs=[xg, vecs, mat, mat, ANY], out_specs=mix,
        out_shape=jax.ShapeDtypeStruct(out_buf.shape, F32), input_output_aliases={4: 0},
        compiler_params=_params(("parallel", "arbitrary")))(proj, vec, wa, wx, out_buf)


def _lru_bwd(proj, vec, wa, wx, dout, dbuf, *, nb, name):
    seq = proj.shape[0] // nb

    def body(xg_ref, vec_ref, wa_ref, wx_ref, do_ref, _, d_ref, dvec_ref, dwa_ref, dwx_ref):
        row = lax.broadcasted_iota(jnp.int32, (seq, LANES), 0)
        vec_, wa_, wx_ = vec_ref[...], wa_ref[...], wx_ref[...]
        xs, xc, xcb, r, ig, sp, la, a, mult = _lru_core(xg_ref[:, 0:LANES], vec_, wa_, wx_, row)
        h = _row_scan(a, mult * (ig * xc), row)
        gate, do = xg_ref[:, LANES:LRU_W], do_ref[...]
        d_ref[:, LANES:LRU_W] = do * h * _gelu_grad(gate)
        dh = do * _gelu(gate)
        gacc = _row_scan(_shift_rows(a, -1, row), dh, row, reverse=True)
        da = gacc * _shift_rows(h, 1, row)
        dmult = gacc * (ig * xc)
        dig = gacc * (mult * xc)
        dxc = gacc * (mult * ig)
        dla = da * a - dmult * (a * a) / mult
        dr = (-LRU_C) * sp * dla
        dsp = jnp.sum((-LRU_C) * r * dla, axis=0, keepdims=True)
        dpr = dr * r * (1.0 - r)
        dpi = dig * ig * (1.0 - ig)
        dprb, dpib = dpr.astype(BF16), dpi.astype(BF16)
        dxc = (dxc + lax.dot_general(dprb, wa_, NT_DIMS, preferred_element_type=F32)
               + lax.dot_general(dpib, wx_, NT_DIMS, preferred_element_type=F32))
        dx = vec_[3:4, :] * dxc
        for j in range(3):
            dx = dx + vec_[j:j + 1, :] * _shift_rows(dxc, -(3 - j), row)
        d_ref[:, 0:LANES] = dx

        @pl.when(pl.program_id(1) == 0)
        def _():
            dvec_ref[...] = jnp.zeros_like(dvec_ref)
            dwa_ref[...] = jnp.zeros_like(dwa_ref)
            dwx_ref[...] = jnp.zeros_like(dwx_ref)

        for j in range(4):
            dvec_ref[j:j + 1, :] += jnp.sum(dxc * xs[j], axis=0, keepdims=True)
        dvec_ref[4:5, :] += jnp.sum(dxc, axis=0, keepdims=True)
        dvec_ref[5:6, :] += jnp.sum(dpr, axis=0, keepdims=True)
        dvec_ref[6:7, :] += jnp.sum(dpi, axis=0, keepdims=True)
        lam = vec_[7:8, :]
        dvec_ref[7:8, :] += -dsp * _sigmoid(-lam)
        dwa_ref[...] += lax.dot_general(xcb, dprb, TN_DIMS, preferred_element_type=F32)
        dwx_ref[...] += lax.dot_general(xcb, dpib, TN_DIMS, preferred_element_type=F32)

    xg, mix, vecs, mat = _lru_specs(seq)
    return pl.pallas_call(
        body, name=name, grid=(2, nb), in_specs=[xg, vecs, mat, mat, mix, ANY], out_specs=[xg, vecs, mat, mat],
        out_shape=[jax.ShapeDtypeStruct(dbuf.shape, F32), jax.ShapeDtypeStruct((SUBLANES, 2 * LANES), F32),
                   jax.ShapeDtypeStruct((2, LANES, LANES), F32), jax.ShapeDtypeStruct((2, LANES, LANES), F32)],
        input_output_aliases={5: 0},
        compiler_params=_params(("parallel", "arbitrary")))(proj, vec, wa, wx, dout, dbuf)


FFN_ROWS = 256
FFN_COLS = 1408


def _with_halo(halo, x, k):
    xx = jnp.concatenate([halo, x], axis=0)
    return pltpu.roll(xx, k, 0)[SUBLANES:, :]


def _ffn_conv(x_ref, halo_ref, cw, pos):
    x, halo = x_ref[...], halo_ref[...]
    x1 = jnp.where(pos >= 1, _with_halo(halo, x, 1), 0.0)
    x2 = jnp.where(pos >= 2, _with_halo(halo, x, 2), 0.0)
    return cw[3:4, :] + cw[0:1, :] * x2 + cw[1:2, :] * x1 + cw[2:3, :] * x, x1, x2


def _ffn_specs(tm, tn, gate_off):
    prev = lambda i: jnp.maximum(i * (tm // SUBLANES) - 1, 0)
    up = pl.BlockSpec((tm, tn), lambda j, i: (i, j))
    gate = pl.BlockSpec((tm, tn), lambda j, i: (i, j + gate_off))
    up_h = pl.BlockSpec((SUBLANES, tn), lambda j, i: (prev(i), j))
    gate_h = pl.BlockSpec((SUBLANES, tn), lambda j, i: (prev(i), j + gate_off))
    cw_up = pl.BlockSpec((SUBLANES, tn), lambda j, i: (0, j))
    cw_gate = pl.BlockSpec((SUBLANES, tn), lambda j, i: (0, j + gate_off))
    return up, gate, up_h, gate_h, cw_up, cw_gate


def _ffn_act(hf, cw, *, seq, name):
    t, w2 = hf.shape
    w = w2 // 2
    tm, tn = _tile(seq, FFN_ROWS, SUBLANES), _tile(w, FFN_COLS)

    def body(u_ref, g_ref, uh_ref, gh_ref, cu_ref, cg_ref, o_ref):
        pos = (pl.program_id(1) * tm + lax.broadcasted_iota(jnp.int32, (tm, 1), 0)) % seq
        up, _, _ = _ffn_conv(u_ref, uh_ref, cu_ref[...], pos)
        gate, _, _ = _ffn_conv(g_ref, gh_ref, cg_ref[...], pos)
        o_ref[...] = (_gelu(gate) * up).astype(BF16)

    specs = _ffn_specs(tm, tn, w // tn)
    return pl.pallas_call(
        body, name=name, grid=(w // tn, t // tm), in_specs=list(specs), out_specs=specs[0],
        out_shape=jax.ShapeDtypeStruct((t, w), BF16),
        compiler_params=_params(("parallel", "parallel")))(hf, hf, hf, hf, cw, cw)


def _ffn_bwd(hf, cw, dact, *, seq, name):
    t, w2 = hf.shape
    w = w2 // 2
    tm, tn = _tile(seq, FFN_ROWS, 2 * SUBLANES), _tile(w, FFN_COLS)
    ext = tm + SUBLANES
    last = t // SUBLANES - 1

    def body(u_ref, g_ref, uh_ref, gh_ref, cu_ref, cg_ref, un_ref, gn_ref, da_ref, dn_ref, d_ref, dcu_ref, dcg_ref):
        pos = (pl.program_id(1) * tm + lax.broadcasted_iota(jnp.int32, (ext, 1), 0)) % seq

        def conv(x_ref, prev_ref, next_ref, cwv):
            xx = jnp.concatenate([prev_ref[...], x_ref[...], next_ref[...]], axis=0)
            x1 = jnp.where(pos >= 1, pltpu.roll(xx, 1, 0)[SUBLANES:, :], 0.0)
            x2 = jnp.where(pos >= 2, pltpu.roll(xx, 2, 0)[SUBLANES:, :], 0.0)
            x0 = xx[SUBLANES:, :]
            return cwv[3:4, :] + cwv[0:1, :] * x2 + cwv[1:2, :] * x1 + cwv[2:3, :] * x0, (x2, x1, x0)

        def back(d, cwv):
            d1 = jnp.where(pos < seq - 1, pltpu.roll(d, ext - 1, 0), 0.0)
            d2 = jnp.where(pos < seq - 2, pltpu.roll(d, ext - 2, 0), 0.0)
            return (cwv[2:3, :] * d + cwv[1:2, :] * d1 + cwv[0:1, :] * d2)[:tm, :].astype(BF16)

        cu, cg = cu_ref[...], cg_ref[...]
        up, u_taps = conv(u_ref, uh_ref, un_ref, cu)
        gate, g_taps = conv(g_ref, gh_ref, gn_ref, cg)
        da = jnp.concatenate([da_ref[...], dn_ref[...]], axis=0)
        cdf = 0.5 * (1.0 + lax.erf(gate * (2.0 ** -0.5)))
        d_up = da * (gate * cdf)
        d_gate = da * up * (cdf + gate * jnp.exp(-0.5 * gate * gate) * (1.0 / math.sqrt(2.0 * math.pi)))
        d_ref[0] = back(d_up, cu)
        d_ref[1] = back(d_gate, cg)

        @pl.when(pl.program_id(1) == 0)
        def _():
            dcu_ref[...] = jnp.zeros_like(dcu_ref)
            dcg_ref[...] = jnp.zeros_like(dcg_ref)

        for ref, d, taps in ((dcu_ref, d_up, u_taps), (dcg_ref, d_gate, g_taps)):
            own = d[:tm, :]
            for j in range(3):
                ref[j:j + 1, :] += jnp.sum(own * taps[j][:tm, :], axis=0, keepdims=True)
            ref[3:4, :] += jnp.sum(own, axis=0, keepdims=True)

    gate_off = w // tn
    specs = _ffn_specs(tm, tn, gate_off)
    tile, cwt = specs[0], specs[4]
    nxt = lambda i: jnp.minimum((i + 1) * (tm // SUBLANES), last)
    up_n = pl.BlockSpec((SUBLANES, tn), lambda j, i: (nxt(i), j))
    gate_n = pl.BlockSpec((SUBLANES, tn), lambda j, i: (nxt(i), j + gate_off))
    return pl.pallas_call(
        body, name=name, grid=(w // tn, t // tm), in_specs=list(specs) + [up_n, gate_n, tile, up_n],
        out_specs=[pl.BlockSpec((2, tm, tn), lambda j, i: (0, i, j)), cwt, cwt],
        out_shape=[jax.ShapeDtypeStruct((2, t, w), BF16), jax.ShapeDtypeStruct((SUBLANES, w), F32),
                   jax.ShapeDtypeStruct((SUBLANES, w), F32)],
        compiler_params=_params(("parallel", "arbitrary")))(hf, hf, hf, hf, cw, cw, hf, hf, dact, dact)


def _adamw(w, g, m, v, *, name, rows=256):
    nl, r, c = w.shape
    tr = _tile(r, rows, SUBLANES)

    def body(w_ref, g_ref, m_ref, v_ref, d_ref, nm_ref, nv_ref):
        g_ = g_ref[...]
        nm = ADAM_B1 * m_ref[...] + (1.0 - ADAM_B1) * g_
        nv = ADAM_B2 * v_ref[...] + (1.0 - ADAM_B2) * (g_ * g_)
        m_hat = nm / (1.0 - ADAM_B1 ** ADAM_STEP)
        v_hat = nv / (1.0 - ADAM_B2 ** ADAM_STEP)
        d_ref[...] = -ADAM_LR * (m_hat / (jnp.sqrt(v_hat) + ADAM_EPS) + ADAM_WD * w_ref[...])
        nm_ref[...] = nm
        nv_ref[...] = nv

    spec = pl.BlockSpec((None, tr, c), lambda l, i: (l, i, 0))
    shape = jax.ShapeDtypeStruct((nl, r, c), F32)
    return pl.pallas_call(body, name=name, grid=(nl, r // tr), in_specs=[spec] * 4, out_specs=[spec] * 3,
                          out_shape=[shape] * 3, compiler_params=_params(("parallel", "parallel")))(w, g, m, v)


def _mesh_pos():
    return lax.axis_index("x"), lax.axis_index("y"), lax.axis_index("c")


def _peers(x, y):
    chips = [(1 - x, y), (x, 1 - y), (1 - x, 1 - y)]
    return [(px, py, 2 * px + py) for px, py in chips]


def _remote(src, dst, send_sems, recv_sems, idx, to):
    return pltpu.make_async_remote_copy(src, dst, send_sems.at[idx], recv_sems.at[idx], device_id=to,
                                        device_id_type=MESH)


def _gather_weights(bufs, *, name):
    n = len(bufs)

    def body(*refs):
        ins, outs = refs[:n], refs[n:2 * n]
        send_sems, recv_sems = refs[2 * n:]
        x, y, c = _mesh_pos()
        me = 2 * x + y
        peers = _peers(x, y)
        started = []
        for i in range(n):
            for r, (px, py, _) in enumerate(peers):
                cp = _remote(ins[i].at[me, c], outs[i].at[me, c], send_sems, recv_sems, (i, r), (px, py, c))
                cp.start()
                started.append(cp)
        for i in range(n):
            for r, (px, py, k) in enumerate(peers):
                _remote(outs[i].at[k, c], outs[i].at[k, c], send_sems, recv_sems, (i, r), (px, py, c)).wait_recv()
                cp = _remote(outs[i].at[k, c], outs[i].at[k, c], send_sems, recv_sems, (i, 3 + r), (x, y, 1 - c))
                cp.start()
                started.append(cp)
        for i in range(n):
            for r, (_, _, k) in enumerate(peers):
                _remote(outs[i].at[k, 1 - c], outs[i].at[k, 1 - c], send_sems, recv_sems, (i, 3 + r),
                        (x, y, 1 - c)).wait_recv()
        for cp in started:
            cp.wait_send()

    return pl.pallas_call(
        body, name=name, in_specs=[ANY] * n, out_specs=[ANY] * n,
        out_shape=[jax.ShapeDtypeStruct(b.shape, b.dtype) for b in bufs],
        input_output_aliases={i: i for i in range(n)},
        scratch_shapes=[pltpu.SemaphoreType.DMA((n, 6)), pltpu.SemaphoreType.DMA((n, 6))])(*bufs)


def _reduce_sibling(gs, *, name):
    n = len(gs)

    def body(*refs):
        ins, outs = refs[:n], refs[n:2 * n]
        send_sems, recv_sems = refs[2 * n:]
        x, y, c = _mesh_pos()
        cps = []
        for i in range(n):
            h = gs[i].shape[1] // 2
            cp = _remote(ins[i].at[:, pl.ds((1 - c) * h, h), :], outs[i], send_sems, recv_sems, i, (x, y, 1 - c))
            cp.start()
            cps.append(cp)
        for cp in cps:
            cp.wait()

    return pl.pallas_call(
        body, name=name, in_specs=[ANY] * n, out_specs=[ANY] * n,
        out_shape=[jax.ShapeDtypeStruct((g.shape[0], g.shape[1] // 2, g.shape[2]), g.dtype) for g in gs],
        scratch_shapes=[pltpu.SemaphoreType.DMA((n,)), pltpu.SemaphoreType.DMA((n,))])(*gs)


def _reduce_chips(ps, *, name):
    n = len(ps)

    def body(*refs):
        ins, outs = refs[:n], refs[n:2 * n]
        send_sems, recv_sems = refs[2 * n:]
        x, y, c = _mesh_pos()
        cps = []
        for i in range(n):
            for r, (px, py, k) in enumerate(_peers(x, y)):
                cp = _remote(ins[i].at[k], outs[i].at[r], send_sems, recv_sems, (i, r), (px, py, c))
                cp.start()
                cps.append(cp)
        for cp in cps:
            cp.wait()

    return pl.pallas_call(
        body, name=name, in_specs=[ANY] * n, out_specs=[ANY] * n,
        out_shape=[jax.ShapeDtypeStruct((3,) + p.shape[1:], p.dtype) for p in ps],
        scratch_shapes=[pltpu.SemaphoreType.DMA((n, 3)), pltpu.SemaphoreType.DMA((n, 3))])(*ps)


def _share_halves(bufs, *, name):
    n = len(bufs)

    def body(*refs):
        ins, outs = refs[:n], refs[n:2 * n]
        send_sems, recv_sems = refs[2 * n:]
        x, y, c = _mesh_pos()
        cps = []
        for i in range(n):
            h = bufs[i].shape[1] // 2
            mine = pl.ds(c * h, h)
            cp = _remote(ins[i].at[:, mine, :], outs[i].at[:, mine, :], send_sems, recv_sems, i, (x, y, 1 - c))
            cp.start()
            cps.append(cp)
        for cp in cps:
            cp.wait()

    return pl.pallas_call(
        body, name=name, in_specs=[ANY] * n, out_specs=[ANY] * n,
        out_shape=[jax.ShapeDtypeStruct(b.shape, b.dtype) for b in bufs],
        input_output_aliases={i: i for i in range(n)},
        scratch_shapes=[pltpu.SemaphoreType.DMA((n,)), pltpu.SemaphoreType.DMA((n,))])(*bufs)


def _add_own_half(full, recv, pos, *, name, rows=256):
    k4, h, n = recv.shape
    tr = _tile(h, rows, 16)
    nblk = h // tr

    def body(pos_ref, a_ref, b_ref, o_ref):
        o_ref[...] = (a_ref[...] + b_ref[...]).astype(BF16)

    grid_spec = pltpu.PrefetchScalarGridSpec(
        num_scalar_prefetch=1, grid=(k4, nblk),
        in_specs=[pl.BlockSpec((None, tr, n), lambda k, i, pos_ref: (k, pos_ref[1] * nblk + i, 0)),
                  pl.BlockSpec((None, tr, n), lambda k, i, pos_ref: (k, i, 0))],
        out_specs=pl.BlockSpec((None, tr, n), lambda k, i, pos_ref: (k, i, 0)))
    return pl.pallas_call(body, name=name, grid_spec=grid_spec, out_shape=jax.ShapeDtypeStruct(recv.shape, BF16),
                          compiler_params=_params(("parallel", "parallel")))(pos, full, recv)


def _sum_into(own, others, buf, pos, layer, *, name, rows=256):
    _, h, n = own.shape
    tr = _tile(h, rows, 16)
    nblk = h // tr

    def body(pos_ref, own_ref, oth_ref, _, o_ref):
        acc = own_ref[...].astype(F32)
        for r in range(3):
            acc = acc + oth_ref[r].astype(F32)
        o_ref[...] = acc

    grid_spec = pltpu.PrefetchScalarGridSpec(
        num_scalar_prefetch=1, grid=(nblk,),
        in_specs=[pl.BlockSpec((None, tr, n), lambda i, pos_ref: (pos_ref[0], i, 0)),
                  pl.BlockSpec((3, tr, n), lambda i, pos_ref: (0, i, 0)), ANY],
        out_specs=pl.BlockSpec((None, tr, n), lambda i, pos_ref: (layer, pos_ref[1] * nblk + i, 0)))
    return pl.pallas_call(body, name=name, grid_spec=grid_spec, out_shape=jax.ShapeDtypeStruct(buf.shape, F32),
                          input_output_aliases={3: 0}, compiler_params=_params(("parallel",)))(pos, own, others, buf)


def _sibling_pair(buf, *, name):
    def body(src_ref, out_ref, send_sem, recv_sem, local_sem):
        x, y, c = _mesh_pos()
        local = pltpu.make_async_copy(src_ref, out_ref.at[c], local_sem)
        local.start()
        cp = pltpu.make_async_remote_copy(src_ref, out_ref.at[c], send_sem, recv_sem, device_id=(x, y, 1 - c),
                                          device_id_type=MESH)
        cp.start()
        cp.wait()
        local.wait()

    return pl.pallas_call(
        body, name=name, in_specs=[ANY], out_specs=ANY, out_shape=jax.ShapeDtypeStruct((2,) + buf.shape, buf.dtype),
        scratch_shapes=[pltpu.SemaphoreType.DMA, pltpu.SemaphoreType.DMA, pltpu.SemaphoreType.DMA])(buf)


def _chip_bcast(buf, *, name):
    def body(src_ref, out_ref, send_sems, recv_sems, local_sem):
        x, y, c = _mesh_pos()
        me = 2 * x + y
        local = pltpu.make_async_copy(src_ref, out_ref.at[me], local_sem)
        local.start()
        sends = []
        for r, (px, py, _) in enumerate(_peers(x, y)):
            cp = _remote(src_ref, out_ref.at[me], send_sems, recv_sems, r, (px, py, c))
            cp.start()
            sends.append(cp)
        for r, (px, py, k) in enumerate(_peers(x, y)):
            _remote(src_ref, out_ref.at[k], send_sems, recv_sems, r, (px, py, c)).wait_recv()
        for cp in sends:
            cp.wait_send()
        local.wait()

    return pl.pallas_call(
        body, name=name, in_specs=[ANY], out_specs=ANY, out_shape=jax.ShapeDtypeStruct((4,) + buf.shape, buf.dtype),
        scratch_shapes=[pltpu.SemaphoreType.DMA((3,)), pltpu.SemaphoreType.DMA((3,)), pltpu.SemaphoreType.DMA])(buf)


def _sum_slots(buf, *, name, rows=384):
    r, n = buf.shape[-2:]
    k = int(np.prod(buf.shape[:-2]))
    tr = _tile(r, rows, SUBLANES)

    def body(b_ref, o_ref):
        acc = b_ref[0]
        for s in range(1, k):
            acc = acc + b_ref[s]
        o_ref[...] = acc

    return pl.pallas_call(
        body, name=name, grid=(r // tr,), in_specs=[pl.BlockSpec((k, tr, n), lambda i: (0, i, 0))],
        out_specs=pl.BlockSpec((tr, n), lambda i: (i, 0)), out_shape=jax.ShapeDtypeStruct((r, n), F32),
        compiler_params=_params(("parallel",)))(buf.reshape((k, r, n)))


ROW = 1024
BIG = (("w_in", 2), ("w_out", 1), ("w_cq", 1), ("w_ck", 1), ("w_cv", 1), ("w_co", 2), ("w_up", 2), ("w_down", 1))
CONV = ("lru_conv_w", "ffn_conv_w")
REPLICATED = ("norm_mix_g", "b_forget", "lru_conv_b", "lru_w_a", "lru_b_a", "lru_w_x", "lru_b_x", "lru_lambda",
              "norm_cross_g", "norm_mem_g", "norm_ffn_g", "ffn_conv_b", "rel_bias", "final_norm_g")
WEIGHTS = ('norm_mix_g', 'w_in', 'b_forget', 'lru_conv_w', 'lru_conv_b', 'lru_w_a', 'lru_b_a', 'lru_w_x', 'lru_b_x',
           'lru_lambda', 'w_out', 'norm_cross_g', 'norm_mem_g', 'w_cq', 'w_ck', 'w_cv', 'w_co', 'norm_ffn_g', 'w_up',
           'ffn_conv_w', 'ffn_conv_b', 'w_down', 'rel_bias', 'final_norm_g')
INPUTS = ("x", "mem") + WEIGHTS + ("loss_target",) + tuple("m_" + n for n in WEIGHTS) + tuple("v_" + n for n in WEIGHTS)


def _round_up(n, m):
    return -(-n // m) * m


class _Packing:
    def __init__(self, entries):
        self.entries, self.off = entries, {}
        o = 0
        for name, shape in entries:
            self.off[name] = o
            o += _round_up(int(np.prod(shape)), ROW)
        self.used = o
        self.rows = _round_up(o // ROW, SUBLANES)

    def pack(self, arrays):
        parts = []
        for name, shape in self.entries:
            n = int(np.prod(shape))
            parts.append(jnp.pad(arrays[name].reshape(n), (0, _round_up(n, ROW) - n)))
        tail = self.rows * ROW - self.used
        if tail:
            parts.append(jnp.zeros((tail,), F32))
        return jnp.concatenate(parts).reshape(self.rows, ROW)

    def unpack(self, flat, lead=()):
        flat = flat.reshape(lead + (self.rows * ROW,))
        out = {}
        for name, shape in self.entries:
            n = int(np.prod(shape))
            out[name] = lax.slice_in_dim(flat, self.off[name], self.off[name] + n, axis=len(lead)).reshape(
                lead + tuple(shape))
        return out


def _to_shards(g, axis):
    r, c = g.shape
    if axis == 1:
        return g.reshape(4, r // 4, c)
    return g.reshape(r, 4, c // 4).transpose(1, 0, 2)


def _from_shards(s, axis):
    _, nl, r, c = s.shape
    if axis == 1:
        return s.transpose(1, 0, 2, 3).reshape(nl, 4 * r, c)
    return s.transpose(1, 2, 0, 3).reshape(nl, r, 4 * c)


def _proj_blocks():
    blocks = []
    for mixer in (MIX_SB, MIX_FOX, MIX_DIL):
        for p in range(2):
            blocks += [ORIG_COL[mixer] + part * 2 * LANES + p * LANES for part in range(3)]
    for hf in range(2):
        blocks += [ORIG_LRU_X + hf * LANES, ORIG_LRU_G + hf * LANES]
    return blocks


def _pad_w_in(w):
    parts = [w[..., s:s + LANES] for s in _proj_blocks()]
    parts += [w[..., 1536:1540], jnp.zeros(w.shape[:-1] + (PROJ_W - COL_F - N_HEADS,), w.dtype)]
    return jnp.concatenate(parts, axis=-1)


def _unpad_w_in(wp):
    blocks = _proj_blocks()
    order = sorted(range(len(blocks)), key=lambda i: blocks[i])
    parts = []
    for i in order:
        if blocks[i] == ORIG_COL[MIX_DIL]:
            parts.append(wp[..., COL_F:COL_F + N_HEADS])
        parts.append(wp[..., i * LANES:(i + 1) * LANES])
    return jnp.concatenate(parts, axis=-1)


def _block_diag(w):
    z = jnp.zeros((HEAD_DIM, HEAD_DIM), w.dtype)
    half = lambda a, b: jnp.concatenate([jnp.concatenate([a, z], 1), jnp.concatenate([z, b], 1)], 0)
    return jnp.stack([half(w[0], w[1]), half(w[2], w[3])])


def _block_diag_grad(d):
    return jnp.stack([d[0, :HEAD_DIM, :HEAD_DIM], d[0, HEAD_DIM:, HEAD_DIM:],
                      d[1, :HEAD_DIM, :HEAD_DIM], d[1, HEAD_DIM:, HEAD_DIM:]])


def _fox_layouts(cum, nb, seq):
    tk = min(ATT_TILE, seq)
    col = cum.reshape(nb, 2, 2, seq).transpose(0, 1, 3, 2)
    row = cum.reshape(nb, 2, 2, seq // tk, tk).transpose(0, 1, 3, 2, 4)
    return col, row


def _layer_params(w, l, nb):
    lru_vec = jnp.concatenate([w["lru_conv_w"][l], w["lru_conv_b"][l][None], w["lru_b_a"][l][None],
                               w["lru_b_x"][l][None], w["lru_lambda"][l][None]], axis=0)
    ffn_cw = jnp.concatenate([w["ffn_conv_w"][l], w["ffn_conv_b"][l][None],
                              jnp.zeros((SUBLANES - 4, 2 * D_FF), F32)], axis=0)
    return dict(
        w_in=w["w_in_padded"][l], lru_vec=lru_vec,
        wa=_block_diag(w["lru_w_a"][l]).astype(BF16), wx=_block_diag(w["lru_w_x"][l]).astype(BF16),
        ffn_cw=ffn_cw, b_rows=jnp.tile(w["b_forget"][l], nb).reshape(nb * N_HEADS, 1))


NORM_ROWS = 512


def _layer_fwd(x, h, mem, w, lp, l, next_g, bias, nb):
    t, d = x.shape
    seq = t // nb
    tag = f"l{l}"
    sv = dict(x0=x)
    proj = _mm(h, lp["w_in"], name=tag + "_proj")
    mixed, ltot = _sb_attn_fwd(proj, lax.empty((t, d), F32), nb=nb, name=tag + "_sb_fwd")
    f_rows = proj[:, COL_F:COL_F + N_HEADS].reshape(nb, seq, N_HEADS).transpose(0, 2, 1).reshape(nb * N_HEADS, seq)
    cum_col, cum_row = _fox_layouts(_fox_gate_fwd(f_rows, lp["b_rows"]), nb, seq)
    mixed, lse_fox = _softmax_attn_fwd(proj, nb=nb, mode="fox", mixer=MIX_FOX, out_buf=mixed,
                                       extra=(cum_col, cum_row), name=tag + "_fox_fwd")
    mixed, lse_dil = _softmax_attn_fwd(proj, nb=nb, mode="dil", mixer=MIX_DIL, out_buf=mixed, extra=(bias,),
                                       name=tag + "_dil_fwd")
    mixed = _lru_fwd(proj, lp["lru_vec"], lp["wa"], lp["wx"], mixed, nb=nb, name=tag + "_lru_fwd")
    x1, hq = _mm(mixed, w["w_out"][l], res=x, norm_g=w["norm_cross_g"][l], ti=NORM_ROWS, name=tag + "_out")
    memn = _rmsnorm(mem, w["norm_mem_g"][l], name=tag + "_norm_mem")
    q = _mm(hq, w["w_cq"][l], name=tag + "_cq")
    k = _mm(memn, w["w_ck"][l], name=tag + "_ck")
    v = _mm(memn, w["w_cv"][l], name=tag + "_cv")
    oc, lse_c = _softmax_attn_fwd((q, k, v), nb=nb, mode="cross", name=tag + "_cross_fwd")
    x2, hn = _mm(oc, w["w_co"][l], res=x1, norm_g=w["norm_ffn_g"][l], ti=NORM_ROWS, name=tag + "_co")
    hf = _mm(hn, w["w_up"][l], name=tag + "_up")
    act = _ffn_act(hf, lp["ffn_cw"], seq=seq, name=tag + "_ffn_act")
    if next_g is None:
        x3, h_next = _mm(act, w["w_down"][l], res=x2, name=tag + "_down"), None
    else:
        x3, h_next = _mm(act, w["w_down"][l], res=x2, norm_g=next_g, ti=NORM_ROWS, name=tag + "_down")
    sv.update(h=h, proj=proj, ltot=ltot, f_rows=f_rows, cum_col=cum_col, cum_row=cum_row, lse_fox=lse_fox,
              lse_dil=lse_dil, mixed=mixed, x1=x1, hq=hq, memn=memn, q=q, k=k, v=v, oc=oc, lse_c=lse_c, x2=x2,
              hn=hn, hf=hf, act=act)
    return x3, h_next, sv


def _layer_bwd(dx3, mem, sv, w, lp, l, bias, nb):
    t = dx3.shape[0]
    seq = t // nb
    tag = f"l{l}"
    g = {}
    g["w_down"] = _mm(sv["act"], dx3, ta=True, name=tag + "_dw_down")
    dact = _mm(dx3, w["w_down"][l], tb=True, name=tag + "_dact")
    dhf, dcu, dcg = _ffn_bwd(sv["hf"], lp["ffn_cw"], dact, seq=seq, name=tag + "_ffn_bwd")
    dcw = jnp.concatenate([dcu, dcg], axis=1)
    g["ffn_conv_w"], g["ffn_conv_b"] = dcw[:3], dcw[3]
    g["w_up"] = _mm(sv["hn"], dhf, ta=True, halves="b", col_shards=4, name=tag + "_dw_up")
    dx2, dg = _mm(dhf, w["w_up"][l], tb=True, halves="a", norm_bwd=(sv["x2"], w["norm_ffn_g"][l], dx3),
                  ti=NORM_ROWS, name=tag + "_dhn")
    g["norm_ffn_g"] = dg.reshape(-1)
    g["w_co"] = _mm(sv["oc"], dx2, ta=True, col_shards=4, name=tag + "_dw_co")
    doc = _mm(dx2, w["w_co"][l], tb=True, name=tag + "_doc")
    dq, dk, dv = _softmax_attn_bwd((sv["q"], sv["k"], sv["v"]), sv["oc"], sv["lse_c"], doc, nb=nb, mode="cross",
                                   name=tag + "_cross_bwd")
    g["w_cq"] = _mm(sv["hq"], dq, ta=True, name=tag + "_dw_cq")
    g["w_ck"] = _mm(sv["memn"], dk, ta=True, name=tag + "_dw_ck")
    g["w_cv"] = _mm(sv["memn"], dv, ta=True, name=tag + "_dw_cv")
    dx1, dg = _mm(dq, w["w_cq"][l], tb=True, norm_bwd=(sv["x1"], w["norm_cross_g"][l], dx2), ti=NORM_ROWS,
                  name=tag + "_dhq")
    g["norm_cross_g"] = dg.reshape(-1)
    dmemn = _mm(dv, w["w_cv"][l], tb=True, res=_mm(dk, w["w_ck"][l], tb=True, name=tag + "_dmem_k"),
                name=tag + "_dmem_v")
    _, g["norm_mem_g"] = _rmsnorm_bwd(dmemn, mem, w["norm_mem_g"][l], None, name=tag + "_norm_mem_bwd")
    mixed, proj = sv["mixed"], sv["proj"]
    g["w_out"] = _mm(mixed, dx1, ta=True, name=tag + "_dw_out")
    dmixed = _mm(dx1, w["w_out"][l], tb=True, name=tag + "_dmixed")
    dproj = _sb_attn_bwd(proj, sv["ltot"], dmixed, lax.empty((t, PROJ_W), F32), nb=nb, name=tag + "_sb_bwd")
    dproj, dcum_k, dcum_q = _softmax_attn_bwd(
        proj, mixed, sv["lse_fox"], dmixed, nb=nb, mode="fox", mixer=MIX_FOX, dbuf=dproj,
        extra=(sv["cum_col"], sv["cum_row"]), name=tag + "_fox_bwd")
    dcum = (dcum_k.transpose(0, 1, 3, 2, 4).reshape(nb * N_HEADS, seq)
            + dcum_q.transpose(0, 1, 3, 2).reshape(nb * N_HEADS, seq))
    df_rows, db = _fox_gate_bwd(dcum, sv["f_rows"], lp["b_rows"])
    g["b_forget"] = db[:N_HEADS, 0]
    df = df_rows.reshape(nb, N_HEADS, seq).transpose(0, 2, 1).reshape(t, N_HEADS)
    dproj, dbias = _softmax_attn_bwd(proj, mixed, sv["lse_dil"], dmixed, nb=nb, mode="dil", mixer=MIX_DIL,
                                     dbuf=dproj, extra=(bias,), name=tag + "_dil_bwd")
    dproj, dvec, dwa, dwx = _lru_bwd(proj, lp["lru_vec"], lp["wa"], lp["wx"], dmixed, dproj, nb=nb,
                                     name=tag + "_lru_bwd")
    g["lru_conv_w"], g["lru_conv_b"], g["lru_b_a"], g["lru_b_x"], g["lru_lambda"] = (
        dvec[0:4], dvec[4], dvec[5], dvec[6], dvec[7])
    g["lru_w_a"], g["lru_w_x"] = _block_diag_grad(dwa), _block_diag_grad(dwx)
    dproj = lax.dynamic_update_slice(dproj, jnp.pad(df, ((0, 0), (0, PROJ_W - COL_F - N_HEADS))), (0, COL_F))
    g["w_in_padded"] = _mm(sv["h"], dproj, ta=True, name=tag + "_dw_in")
    dx0, dg = _mm(dproj, lp["w_in"], tb=True, norm_bwd=(sv["x0"], w["norm_mix_g"][l], dx1), ti=NORM_ROWS,
                  name=tag + "_dh")
    g["norm_mix_g"] = dg.reshape(-1)
    return dx0, g, dbias


def _big_grad_shards(g):
    out = []
    for n, axis in BIG:
        if n in ("w_up", "w_co"):
            out.append(g[n])
        else:
            out.append(_to_shards(_unpad_w_in(g["w_in_padded"]) if n == "w_in" else g[n], axis))
    return out


def kernel(*args):
    a = dict(zip(INPUTS, args, strict=True))
    nb, seq, d = a["x"].shape
    depth = a["norm_mix_g"].shape[0]
    x = a["x"].reshape(nb * seq, d)
    mem = a["mem"].reshape(nb * a["mem"].shape[1], d)
    target = a["loss_target"].reshape(nb * seq, d)
    cx, cy, c = _mesh_pos()
    chip = 2 * cx + cy
    pos = jnp.stack([chip, c]).astype(jnp.int32)

    slots = []
    for n, _ in BIG:
        own = a[n].astype(BF16)[None]
        slots.append(lax.dynamic_update_slice(lax.empty((4,) + own.shape[1:], BF16), own, (chip,) + (0,) * (own.ndim - 1)))
    gathered = _gather_weights(slots, name="gather_weights")
    w = {n: a[n] for n in REPLICATED}
    for (n, axis), buf in zip(BIG, gathered):
        w[n] = _from_shards(buf, axis)
    w["w_in_padded"] = _pad_w_in(w["w_in"])
    cpk = _Packing([(n, a[n].shape) for n in CONV])
    conv = cpk.unpack(_chip_bcast(cpk.pack({n: a[n] for n in CONV}), name="gather_conv"), lead=(4,))
    for n in CONV:
        w[n] = jnp.moveaxis(conv[n], 0, 2).reshape(a[n].shape[:2] + (4 * a[n].shape[2],))

    bias = _dil_bias(w["rel_bias"], seq)
    lps = [_layer_params(w, l, nb) for l in range(depth)]
    saved = []
    h = _rmsnorm(x, w["norm_mix_g"][0], name="l0_norm_mix")
    for l in range(depth):
        next_g = w["norm_mix_g"][l + 1] if l + 1 < depth else None
        x, h, sv = _layer_fwd(x, h, mem, w, lps[l], l, next_g, bias, nb)
        saved.append(sv)
    loss, dx, dg_final = _loss_head(x, w["final_norm_g"], target)
    small_g = [None] * depth
    dbias = None
    g_shard = [lax.empty(a[n].shape, F32) for n, _ in BIG]
    for l in reversed(range(depth)):
        dx, g, db = _layer_bwd(dx, mem, saved[l], w, lps[l], l, bias, nb)
        dbias = db if dbias is None else dbias + db
        small_g[l] = g
        full = _big_grad_shards(g)
        from_sibling = _reduce_sibling(full, name=f"l{l}_reduce_sibling")
        partial = [_add_own_half(f, r, pos, name=f"l{l}_reduce_add_{n}")
                   for f, r, (n, _) in zip(full, from_sibling, BIG)]
        others = _reduce_chips(partial, name=f"l{l}_reduce_chips")
        g_shard = [_sum_into(p, o, buf, pos, l, name=f"l{l}_reduce_sum_{n}")
                   for p, o, buf, (n, _) in zip(partial, others, g_shard, BIG)]
    g_shard = _share_halves(g_shard, name="reduce_share")
    out = {}
    for (n, _), gs in zip(BIG, g_shard):
        delta, new_m, new_v = _adamw(a[n], gs, a["m_" + n], a["v_" + n], name="adamw_" + n)
        out[n] = (gs, delta, new_m, new_v)

    grads = {n: jnp.stack([small_g[l][n] for l in range(depth)]) for n in REPLICATED + CONV
             if n not in ("rel_bias", "final_norm_g")}
    grads["rel_bias"] = _dil_bias_bwd(dbias, seq)
    grads["final_norm_g"] = dg_final
    grads["loss"] = loss.reshape(1)
    spk = _Packing([(n, grads[n].shape) for n in REPLICATED + CONV + ("loss",)])
    s_all = _chip_bcast(_sibling_pair(spk.pack(grads), name="small_sibling"), name="small_chips")
    total = spk.unpack(_sum_slots(s_all, name="small_sum"))
    for n in CONV:
        width = a[n].shape[2]
        total[n] = lax.dynamic_slice_in_dim(total[n], chip * width, width, axis=2)
    apk = _Packing([(n, a[n].shape) for n in REPLICATED + CONV])
    s_out = _adamw(*[apk.pack(src)[None] for src in (
        {n: a[n] for n in REPLICATED + CONV}, total, {n: a["m_" + n] for n in REPLICATED + CONV},
        {n: a["v_" + n] for n in REPLICATED + CONV})], name="adamw_small")
    s_delta, s_m, s_v = [apk.unpack(o[0]) for o in s_out]
    for n in REPLICATED + CONV:
        out[n] = (total[n], s_delta[n], s_m[n], s_v[n])

    return (total["loss"].reshape(()), dx.reshape(nb, seq, d), *[out[n][0] for n in WEIGHTS],
            *[out[n][1] for n in WEIGHTS], *[out[n][2] for n in WEIGHTS], *[out[n][3] for n in WEIGHTS])
```

```python
import math

import numpy as np
import jax
import jax.numpy as jnp
from jax import lax
from jax.experimental import pallas as pl
from jax.experimental.pallas import tpu as pltpu

F32 = jnp.float32
BF16 = jnp.bfloat16

HEAD_DIM = 64
N_HEADS = 4
N_IN = 2820
D_FF = 2816
LRU_C = 8.0
EPS = 1e-6
NUM_BUCKETS = 32
MAX_DISTANCE = 2048
DILATED_PATTERNS = ((128, 1), (512, 4), (2048, 16))
ADAM_LR, ADAM_B1, ADAM_B2, ADAM_EPS, ADAM_WD, ADAM_STEP = 0.001, 0.9, 0.999, 1e-08, 0.01, 10

LANES = 128
SUBLANES = 8
VMEM_LIMIT = 48 * 1024 * 1024

PROJ_W = 3072
PAIR_W = 3 * LANES
LRU_W = 2 * LANES
COL_LRU = 6 * PAIR_W
COL_F = COL_LRU + 2 * LRU_W
MIX_SB, MIX_FOX, MIX_DIL, MIX_LRU = 0, 1, 2, 3
ORIG_COL = {MIX_SB: 0, MIX_FOX: 768, MIX_DIL: 1540}
ORIG_LRU_X, ORIG_LRU_G = 2308, 2564

ATT_TILE = 256
MASKED = -1e30
SCALE = HEAD_DIM ** -0.5

NT_DIMS = (((1,), (1,)), ((), ()))
TN_DIMS = (((0,), (0,)), ((), ()))

MESH = pl.DeviceIdType.MESH
ANY = pl.BlockSpec(memory_space=pl.ANY)


def _params(sem):
    return pltpu.CompilerParams(dimension_semantics=sem, vmem_limit_bytes=VMEM_LIMIT)


def _tile(n, target, unit=LANES):
    if n <= target:
        return n
    t = (target // unit) * unit
    while t > unit and n % t:
        t -= unit
    assert n % t == 0, (n, target, unit)
    return t


def _mm(a, b, *, ta=False, tb=False, res=None, col_shards=1, halves=None, norm_g=None, norm_bwd=None, comm=None,
        name, ti=1024, tj=1408, tc=1408):
    if halves == "a":
        m, kc = a.shape[1], 2 * a.shape[2]
    else:
        m, kc = (a.shape[1], a.shape[0]) if ta else a.shape
    if halves == "b":
        n = 2 * b.shape[2]
        assert b.shape[1] == kc
    else:
        n = b.shape[0] if tb else b.shape[1]
        assert (b.shape[1] if tb else b.shape[0]) == kc
    assert n % col_shards == 0
    n_blk = n // (2 if halves == "b" else col_shards)
    k_blk = kc // 2 if halves == "a" else kc
    ti, tj, tc = (_tile(m, ti, LANES if ta else SUBLANES), _tile(n_blk, tj),
                  _tile(k_blk, tc, SUBLANES if ta and tb else LANES))
    per_shard, per_half_j, per_half_k = n // col_shards // tj, n_blk // tj, k_blk // tc
    nk = kc // tc
    dims = (((0 if ta else 1,), (1 if tb else 0,)), ((), ()))
    rows_whole = norm_g is not None or norm_bwd is not None
    assert not rows_whole or (tj == n and col_shards == 1)
    n_extra = (res is not None) + (norm_g is not None) + (3 if norm_bwd is not None else 0)
    n_out = 2 if rows_whole else 1

    def finish(val, ex, outs):
        if res is not None:
            val = ex[0][...] + val
        if norm_g is not None:
            outs[0][...] = val
            outs[1][...] = (_xhat(val) * ex[-1][...]).astype(BF16)
        elif norm_bwd is not None:
            x_ref, g_ref, r_ref = ex[-3:]
            dx, dgr = _norm_bwd_rows(val, x_ref[...], g_ref[...])
            outs[0][...] = r_ref[...] + dx

            @pl.when(pl.program_id(0) == 0)
            def _():
                outs[1][...] = jnp.zeros_like(outs[1])

            outs[1][...] += jnp.sum(dgr, axis=0, keepdims=True)
        else:
            outs[0][...] = val

    def body(*refs):
        a_ref, b_ref = refs[:2]
        ex = refs[2:2 + n_extra]
        outs = refs[2 + n_extra:2 + n_extra + n_out]
        part = lax.dot_general(a_ref[...].astype(BF16), b_ref[...].astype(BF16), dims, preferred_element_type=F32)
        if nk == 1:
            finish(part, ex, outs)
            return
        acc_ref = refs[-1]
        k = pl.program_id(2)

        @pl.when(k == 0)
        def _():
            acc_ref[...] = part

        @pl.when(k > 0)
        def _():
            acc_ref[...] += part

        @pl.when(k == nk - 1)
        def _():
            finish(acc_ref[...], ex, outs)

    if halves == "a":
        a_spec = pl.BlockSpec((None, ti, tc), lambda i, j, k: (k // per_half_k, i, k % per_half_k))
    elif ta:
        a_spec = pl.BlockSpec((tc, ti), lambda i, j, k: (k, i))
    else:
        a_spec = pl.BlockSpec((ti, tc), lambda i, j, k: (i, k))
    if halves == "b":
        b_spec = pl.BlockSpec((None, tc, tj), lambda i, j, k: (j // per_half_j, k, j % per_half_j))
    elif tb:
        b_spec = pl.BlockSpec((tj, tc), lambda i, j, k: (j, k))
    else:
        b_spec = pl.BlockSpec((tc, tj), lambda i, j, k: (k, j))
    o_spec = pl.BlockSpec((ti, tj), lambda i, j, k: (i, j))
    vec = pl.BlockSpec((1, tj), lambda i, j, k: (0, 0))
    in_specs, args = [a_spec, b_spec], [a, b]
    out_specs, out_shape = [o_spec], [jax.ShapeDtypeStruct((m, n), F32)]
    if res is not None:
        in_specs.append(o_spec)
        args.append(res)
    if norm_g is not None:
        in_specs.append(vec)
        args.append(norm_g.reshape(1, n))
        out_specs.append(o_spec)
        out_shape.append(jax.ShapeDtypeStruct((m, n), BF16))
    if norm_bwd is not None:
        x, g, dres = norm_bwd
        in_specs += [o_spec, vec, o_spec]
        args += [x, g.reshape(1, n), dres]
        out_specs.append(vec)
        out_shape.append(jax.ShapeDtypeStruct((1, n), F32))
    if col_shards > 1:
        assert n_extra == 0
        out_specs = [pl.BlockSpec((None, ti, tj), lambda i, j, k: (j // per_shard, i, j % per_shard))]
        out_shape = [jax.ShapeDtypeStruct((col_shards, m, n // col_shards), F32)]
    sem = ("arbitrary",) * 3 if norm_bwd is not None else ("parallel", "parallel", "arbitrary")
    out, carried = _pallas(body, name=name, grid=(m // ti, n // tj, nk), in_specs=in_specs, out_specs=out_specs,
                           out_shape=out_shape, args=args, scratch=[] if nk == 1 else [pltpu.VMEM((ti, tj), F32)],
                           sem=sem, comm=comm)
    out = out if rows_whole else out[0]
    return out if comm is None else (out, carried)


def _xhat(x):
    return x * lax.rsqrt(jnp.mean(x * x, axis=-1, keepdims=True) + EPS)


def _norm_bwd_rows(dy, x, g):
    rstd = lax.rsqrt(jnp.mean(x * x, axis=-1, keepdims=True) + EPS)
    xh = x * rstd
    dxh = dy * g
    dx = rstd * (dxh - xh * jnp.mean(dxh * xh, axis=-1, keepdims=True))
    return dx, dy * xh


def _rmsnorm(x, g, *, name, rows=512):
    t, d = x.shape
    tr = _tile(t, rows, 2 * SUBLANES)

    def body(x_ref, g_ref, o_ref):
        o_ref[...] = (_xhat(x_ref[...]) * g_ref[...]).astype(BF16)

    return pl.pallas_call(
        body, name=name, grid=(t // tr,),
        in_specs=[pl.BlockSpec((tr, d), lambda i: (i, 0)), pl.BlockSpec((1, d), lambda i: (0, 0))],
        out_specs=pl.BlockSpec((tr, d), lambda i: (i, 0)), out_shape=jax.ShapeDtypeStruct((t, d), BF16),
        compiler_params=_params(("parallel",)))(x, g.reshape(1, d))


def _rmsnorm_bwd(dy, x, g, dres, *, name, rows=512):
    t, d = x.shape
    tr = _tile(t, rows, SUBLANES)

    def body(*refs):
        if dres is None:
            dy_ref, x_ref, g_ref, dx_ref, dg_ref = refs
        else:
            dy_ref, x_ref, g_ref, r_ref, dx_ref, dg_ref = refs
        dx, dgr = _norm_bwd_rows(dy_ref[...], x_ref[...], g_ref[...])
        dx_ref[...] = dx if dres is None else r_ref[...] + dx

        @pl.when(pl.program_id(0) == 0)
        def _():
            dg_ref[...] = jnp.zeros_like(dg_ref)

        dg_ref[...] += jnp.sum(dgr, axis=0, keepdims=True)

    row = pl.BlockSpec((tr, d), lambda i: (i, 0))
    vec = pl.BlockSpec((1, d), lambda i: (0, 0))
    in_specs = [row, row, vec] + ([] if dres is None else [row])
    args = (dy, x, g.reshape(1, d)) + (() if dres is None else (dres,))
    dx, dg = pl.pallas_call(
        body, name=name, grid=(t // tr,), in_specs=in_specs, out_specs=[row, vec],
        out_shape=[jax.ShapeDtypeStruct((t, d), F32), jax.ShapeDtypeStruct((1, d), F32)],
        compiler_params=_params(("arbitrary",)))(*args)
    return dx, dg.reshape(d)


def _loss_head(x, g, target, *, rows=512):
    t, d = x.shape
    tr = _tile(t, rows, SUBLANES)

    def body(x_ref, g_ref, t_ref, dx_ref, dg_ref, loss_ref):
        x_, g_ = x_ref[...], g_ref[...]
        err = _xhat(x_) * g_ - t_ref[...]
        dx, dgr = _norm_bwd_rows(err * (1.0 / d), x_, g_)
        dx_ref[...] = dx

        @pl.when(pl.program_id(0) == 0)
        def _():
            dg_ref[...] = jnp.zeros_like(dg_ref)
            loss_ref[...] = jnp.zeros_like(loss_ref)

        dg_ref[...] += jnp.sum(dgr, axis=0, keepdims=True)
        loss_ref[...] += 0.5 * jnp.sum(jnp.mean(err * err, axis=-1, keepdims=True), axis=0, keepdims=True)

    row = pl.BlockSpec((tr, d), lambda i: (i, 0))
    vec = pl.BlockSpec((1, d), lambda i: (0, 0))
    one = pl.BlockSpec((1, 1), lambda i: (0, 0))
    dx, dg, loss = pl.pallas_call(
        body, name="loss_head", grid=(t // tr,), in_specs=[row, vec, row], out_specs=[row, vec, one],
        out_shape=[jax.ShapeDtypeStruct((t, d), F32), jax.ShapeDtypeStruct((1, d), F32),
                   jax.ShapeDtypeStruct((1, 1), F32)],
        compiler_params=_params(("arbitrary",)))(x, g.reshape(1, d), target)
    return loss.reshape(()), dx, dg.reshape(d)


def _head_masks(shape):
    lane = lax.broadcasted_iota(jnp.int32, shape, len(shape) - 1)
    return lane < HEAD_DIM, lane >= HEAD_DIM


def _split_heads(x):
    m0, m1 = _head_masks(x.shape)
    zero = jnp.zeros_like(x)
    return jnp.where(m0, x, zero), jnp.where(m1, x, zero)


def _lane_pair(a0, a1, rows):
    m0, _ = _head_masks((rows, LANES))
    return jnp.where(m0, a0, a1)


def _qkv_readers(refs, packed):
    if packed:
        (r,) = refs
        return tuple((lambda r0, n, s=s: r[pl.ds(r0, n), s * LANES:(s + 1) * LANES]) for s in range(3))
    return tuple((lambda r0, n, ref=ref: ref[pl.ds(r0, n), :]) for ref in refs)


def _pair_spec(seq, col0, width=LANES):
    return pl.BlockSpec((seq, width), lambda p, b: (b, col0 + p))


def _fox_specs(seq, nk, tk):
    return [pl.BlockSpec((None, None, seq, 2), lambda p, b: (b, p, 0, 0)),
            pl.BlockSpec((None, None, nk, 2, tk), lambda p, b: (b, p, 0, 0, 0))]


def _softmax_attn_fwd(src, *, nb, mode, mixer=None, out_buf=None, extra=(), name, comm=None):
    packed = mode != "cross"
    n_src = 1 if packed else 3
    seq_q = (src if packed else src[0]).shape[0] // nb
    seq_k = seq_q if packed else src[1].shape[0] // nb
    tq, tk = min(ATT_TILE, seq_q), min(ATT_TILE, seq_k)
    nq, nk = seq_q // tq, seq_k // tk
    n_ex = len(extra)

    def body(*refs):
        q_at, k_at, v_at = _qkv_readers(refs[:n_src], packed)
        ex = refs[n_src:n_src + n_ex]
        o_ref, lse_ref = refs[-2:]

        def q_tile(i, _):
            r0 = pl.multiple_of(i * tq, tq)
            qm = _split_heads((q_at(r0, tq) * SCALE).astype(BF16))
            if mode == "fox":
                cq = ex[0][pl.ds(r0, tq), :]
                row = r0 + lax.broadcasted_iota(jnp.int32, (tq, tk), 0)

            def k_tile(j, carry, diagonal=False):
                m, l, acc = carry
                c0 = pl.multiple_of(j * tk, tk)
                kt = k_at(c0, tk).astype(BF16)
                vm = _split_heads(v_at(c0, tk).astype(BF16))
                if mode == "fox":
                    ck = ex[1][j]
                hs = range(2)
                s = [lax.dot_general(qm[h], kt, NT_DIMS, preferred_element_type=F32) for h in hs]
                if mode == "fox":
                    s = [s[h] + cq[:, h:h + 1] - ck[h:h + 1, :] for h in hs]
                    if diagonal:
                        keep = (c0 + lax.broadcasted_iota(jnp.int32, (tq, tk), 1)) <= row
                        s = [jnp.where(keep, s[h], MASKED) for h in hs]
                elif mode == "dil":
                    s = [s[h] + ex[0][h, i - j] for h in hs]
                new_m = [jnp.maximum(m[h], jnp.max(s[h], axis=-1, keepdims=True)) for h in hs]
                p = [jnp.exp(s[h] - new_m[h]) for h in hs]
                alpha = [jnp.exp(m[h] - new_m[h]) for h in hs]
                new_l = [alpha[h] * l[h] + jnp.sum(p[h], axis=-1, keepdims=True) for h in hs]
                pv = [jnp.dot(p[h].astype(BF16), vm[h], preferred_element_type=F32) for h in hs]
                acc = acc * _lane_pair(alpha[0], alpha[1], tq) + (pv[0] + pv[1])
                return tuple(new_m), tuple(new_l), acc

            init = ((jnp.full((tq, 1), MASKED, F32),) * 2, (jnp.zeros((tq, 1), F32),) * 2,
                    jnp.zeros((tq, LANES), F32))
            if mode == "fox":
                m, l, acc = k_tile(i, lax.fori_loop(0, i, k_tile, init), True)
            else:
                m, l, acc = lax.fori_loop(0, i + 1 if packed else nk, k_tile, init)
            o_ref[pl.ds(r0, tq), :] = acc / _lane_pair(l[0], l[1], tq)
            lse_ref[pl.ds(r0, tq), :] = _lane_pair(m[0] + jnp.log(l[0]), m[1] + jnp.log(l[1]), tq)
            return 0

        lax.fori_loop(0, nq, q_tile, 0)

    lse_shape = jax.ShapeDtypeStruct((nb * seq_q, 2 * LANES), F32)
    if packed:
        in_specs, args = [_pair_spec(seq_q, 2 * mixer, PAIR_W)], [src]
        in_specs += _fox_specs(seq_q, nk, tk) if mode == "fox" else [
            pl.BlockSpec((None, 2, nq, tq, tk), lambda p, b: (p, 0, 0, 0, 0))]
        args += list(extra) + [out_buf]
        in_specs.append(ANY)
        out_specs = [_pair_spec(seq_q, 2 * mixer), _pair_spec(seq_q, 0)]
        out_shape = [jax.ShapeDtypeStruct(out_buf.shape, F32), lse_shape]
        aliases = {len(args) - 1: 0}
    else:
        in_specs = [_pair_spec(seq_q, 0), _pair_spec(seq_k, 0), _pair_spec(seq_k, 0)]
        args = list(src)
        out_specs = [_pair_spec(seq_q, 0), _pair_spec(seq_q, 0)]
        out_shape = [lse_shape, lse_shape]
        aliases = {}
    out, carried = _pallas(body, name=name, grid=(2, nb), in_specs=in_specs, out_specs=out_specs, out_shape=out_shape,
                           args=args, aliases=aliases, sem=("parallel", "arbitrary"), comm=comm)
    return out if comm is None else (out, carried)


def _softmax_attn_bwd(src, o, lse, do, *, nb, mode, mixer=None, dbuf=None, extra=(), name):
    packed = mode != "cross"
    n_src = 1 if packed else 3
    seq_q = (src if packed else src[0]).shape[0] // nb
    seq_k = seq_q if packed else src[1].shape[0] // nb
    tq, tk = min(ATT_TILE, seq_q), min(ATT_TILE, seq_k)
    nq, nk = seq_q // tq, seq_k // tk
    n_ex = len(extra)
    n_in = n_src + 3 + n_ex + (1 if packed else 0)

    def body(*refs):
        q_at, k_at, v_at = _qkv_readers(refs[:n_src], packed)
        o_ref, lse_ref, do_ref = refs[n_src:n_src + 3]
        ex = refs[n_src + 3:n_src + 3 + n_ex]
        outs = refs[n_in:]
        if packed:
            d_ref = outs[0]
            dq_w = lambda r0, val: d_ref.__setitem__((pl.ds(r0, tq), slice(0, LANES)), val)
            dk_ref = d_ref.at[:, LANES:2 * LANES]
            dv_ref = d_ref.at[:, 2 * LANES:3 * LANES]
        else:
            dq_ref, dk_ref, dv_ref = outs[:3]
            dq_w = lambda r0, val: dq_ref.__setitem__((pl.ds(r0, tq), slice(None)), val)
        dk_ref[...] = jnp.zeros((seq_k, LANES), F32)
        dv_ref[...] = jnp.zeros((seq_k, LANES), F32)
        if mode == "fox":
            dcum_ref, dcq_ref = outs[-2:]
            dcum_ref[...] = jnp.zeros_like(dcum_ref)
        if mode == "dil":
            dbias_ref = outs[-1]

            @pl.when(pl.program_id(1) == 0)
            def _():
                dbias_ref[...] = jnp.zeros_like(dbias_ref)

        def q_tile(i, _):
            r0 = pl.multiple_of(i * tq, tq)
            qm = _split_heads((q_at(r0, tq) * SCALE).astype(BF16))
            do_f = do_ref[pl.ds(r0, tq), :]
            dom = _split_heads(do_f.astype(BF16))
            dd = _split_heads(do_f * o_ref[pl.ds(r0, tq), :])
            delta = [jnp.sum(dd[h], axis=-1, keepdims=True) for h in range(2)]
            lse_t = lse_ref[pl.ds(r0, tq), :]
            lse_h = [lse_t[:, 0:1], lse_t[:, HEAD_DIM:HEAD_DIM + 1]]
            if mode == "fox":
                cq = ex[0][pl.ds(r0, tq), :]
                row = r0 + lax.broadcasted_iota(jnp.int32, (tq, tk), 0)

            def k_tile(j, carry, diagonal=False):
                dq, rs = carry
                c0 = pl.multiple_of(j * tk, tk)
                kt = k_at(c0, tk).astype(BF16)
                vt = v_at(c0, tk).astype(BF16)
                km = _split_heads(kt)
                if mode == "fox":
                    ck = ex[1][j]
                hs = range(2)
                s = [lax.dot_general(qm[h], kt, NT_DIMS, preferred_element_type=F32) for h in hs]
                dp = [lax.dot_general(dom[h], vt, NT_DIMS, preferred_element_type=F32) for h in hs]
                if mode == "fox":
                    s = [s[h] + cq[:, h:h + 1] - ck[h:h + 1, :] for h in hs]
                    if diagonal:
                        keep = (c0 + lax.broadcasted_iota(jnp.int32, (tq, tk), 1)) <= row
                        s = [jnp.where(keep, s[h], MASKED) for h in hs]
                elif mode == "dil":
                    s = [s[h] + ex[0][h, i - j] for h in hs]
                p = [jnp.exp(s[h] - lse_h[h]) for h in hs]
                ds = [p[h] * (dp[h] - delta[h]) for h in hs]
                dsb = [ds[h].astype(BF16) for h in hs]
                pb = [p[h].astype(BF16) for h in hs]
                dq = dq + (jnp.dot(dsb[0], km[0], preferred_element_type=F32)
                           + jnp.dot(dsb[1], km[1], preferred_element_type=F32))
                dk_t = (lax.dot_general(dsb[0], qm[0], TN_DIMS, preferred_element_type=F32)
                        + lax.dot_general(dsb[1], qm[1], TN_DIMS, preferred_element_type=F32))
                dv_t = (lax.dot_general(pb[0], dom[0], TN_DIMS, preferred_element_type=F32)
                        + lax.dot_general(pb[1], dom[1], TN_DIMS, preferred_element_type=F32))
                if mode == "fox":
                    for h in hs:
                        dcum_ref[j, h:h + 1, :] -= jnp.sum(ds[h], axis=0, keepdims=True)
                    rs = tuple(rs[h] + jnp.sum(ds[h], axis=-1, keepdims=True) for h in hs)
                elif mode == "dil":
                    for h in hs:
                        dbias_ref[h, i - j] += ds[h]
                dk_ref[pl.ds(c0, tk), :] += dk_t
                dv_ref[pl.ds(c0, tk), :] += dv_t
                return dq, rs

            zero = (jnp.zeros((tq, 1), F32),) * 2
            init = (jnp.zeros((tq, LANES), F32), zero)
            if mode == "fox":
                dq, rs = k_tile(i, lax.fori_loop(0, i, k_tile, init), True)
            else:
                dq, rs = lax.fori_loop(0, i + 1 if packed else nk, k_tile, init)
            dq_w(r0, dq * SCALE)
            if mode == "fox":
                dcq_ref[pl.ds(r0, tq), :] = jnp.where(lax.broadcasted_iota(jnp.int32, (tq, 2), 1) == 0, rs[0], rs[1])
            return 0

        lax.fori_loop(0, nq, q_tile, 0)

    if packed:
        in_specs = [_pair_spec(seq_q, 2 * mixer, PAIR_W), _pair_spec(seq_q, 2 * mixer), _pair_spec(seq_q, 0),
                    _pair_spec(seq_q, 2 * mixer)]
        args = [src, o, lse, do]
        out_specs = [_pair_spec(seq_q, 2 * mixer, PAIR_W)]
        out_shape = [jax.ShapeDtypeStruct(dbuf.shape, F32)]
        if mode == "fox":
            in_specs += _fox_specs(seq_q, nk, tk)
            out_specs += [_fox_specs(seq_q, nk, tk)[1], _fox_specs(seq_q, nk, tk)[0]]
            out_shape += [jax.ShapeDtypeStruct((nb, 2, nk, 2, tk), F32), jax.ShapeDtypeStruct((nb, 2, seq_q, 2), F32)]
        else:
            tiles = pl.BlockSpec((None, 2, nq, tq, tk), lambda p, b: (p, 0, 0, 0, 0))
            in_specs.append(tiles)
            out_specs.append(tiles)
            out_shape.append(jax.ShapeDtypeStruct((2, 2, nq, tq, tk), F32))
        args += list(extra) + [dbuf]
        in_specs.append(ANY)
        aliases = {len(args) - 1: 0}
    else:
        sq, sk = _pair_spec(seq_q, 0), _pair_spec(seq_k, 0)
        in_specs, args = [sq, sk, sk, sq, sq, sq], list(src) + [o, lse, do]
        out_specs = [sq, sk, sk]
        out_shape = [jax.ShapeDtypeStruct((nb * seq_q, 2 * LANES), F32)] + [
            jax.ShapeDtypeStruct((nb * seq_k, 2 * LANES), F32)] * 2
        aliases = {}
    return pl.pallas_call(
        body, name=name, grid=(2, nb), in_specs=in_specs, out_specs=out_specs, out_shape=out_shape,
        input_output_aliases=aliases, compiler_params=_params(("parallel", "arbitrary")))(*args)


def _log_sigmoid(z):
    return jnp.minimum(z, 0.0) - jnp.log(1.0 + jnp.exp(-jnp.abs(z)))


def _split_bf16(x):
    hi = x.astype(BF16)
    return hi, (x - hi.astype(F32)).astype(BF16)


def _tri(n, fn):
    r = lax.broadcasted_iota(jnp.int32, (n, n), 0)
    c = lax.broadcasted_iota(jnp.int32, (n, n), 1)
    return jnp.where(fn(r, c), 1.0, 0.0).astype(BF16)


def _sb_attn_fwd(proj, out_buf, *, nb, name, comm=None):
    seq = proj.shape[0] // nb
    tq = tk = min(ATT_TILE, seq)
    nq = seq // tq

    def body(qkv_ref, _, o_ref, lt_ref):
        q_at, k_at, v_at = _qkv_readers((qkv_ref,), True)
        after = _tri(tk, lambda r, c: r > c)

        def q_tile(i, _):
            r0 = pl.multiple_of(i * tq, tq)
            qm = _split_heads((q_at(r0, tq) * SCALE).astype(BF16))
            row = r0 + lax.broadcasted_iota(jnp.int32, (tq, tk), 0)

            def k_tile(j, carry, diagonal):
                c, acc = carry
                c0 = pl.multiple_of(j * tk, tk)
                kt = k_at(c0, tk).astype(BF16)
                vm = _split_heads(v_at(c0, tk).astype(BF16))
                if diagonal:
                    strict = (c0 + lax.broadcasted_iota(jnp.int32, (tq, tk), 1)) < row
                hs = range(2)
                z = [lax.dot_general(qm[h], kt, NT_DIMS, preferred_element_type=F32) for h in hs]
                ls = [_log_sigmoid(z[h]) for h in hs]
                lk = [ls[h] - z[h] for h in hs]
                if diagonal:
                    lk = [jnp.where(strict, lk[h], 0.0) for h in hs]
                parts = [_split_bf16(lk[h]) for h in hs]
                sfx = [jnp.dot(parts[h][0], after, preferred_element_type=F32)
                       + jnp.dot(parts[h][1], after, preferred_element_type=F32) for h in hs]
                att = [jnp.exp(ls[h] + sfx[h] + c[h]) for h in hs]
                if diagonal:
                    att = [jnp.where(strict, att[h], 0.0) for h in hs]
                acc = acc + (jnp.dot(att[0].astype(BF16), vm[0], preferred_element_type=F32)
                             + jnp.dot(att[1].astype(BF16), vm[1], preferred_element_type=F32))
                return tuple(c[h] + jnp.sum(lk[h], axis=-1, keepdims=True) for h in hs), acc

            init = ((jnp.zeros((tq, 1), F32),) * 2, jnp.zeros((tq, LANES), F32))
            c, acc = lax.fori_loop(1, i + 1, lambda jj, cr: k_tile(i - jj, cr, False), k_tile(i, init, True))
            o_ref[pl.ds(r0, tq), :] = acc
            lt_ref[pl.ds(r0, tq), :] = _lane_pair(c[0], c[1], tq)
            return 0

        lax.fori_loop(0, nq, q_tile, 0)

    out, carried = _pallas(
        body, name=name, grid=(2, nb), in_specs=[_pair_spec(seq, 2 * MIX_SB, PAIR_W), ANY],
        out_specs=[_pair_spec(seq, 2 * MIX_SB), _pair_spec(seq, 0)],
        out_shape=[jax.ShapeDtypeStruct(out_buf.shape, F32), jax.ShapeDtypeStruct((nb * seq, 2 * LANES), F32)],
        args=[proj, out_buf], aliases={1: 0}, sem=("parallel", "arbitrary"), comm=comm)
    return out if comm is None else (out, carried)


def _sb_attn_bwd(proj, ltot, do, dbuf, *, nb, name):
    seq = proj.shape[0] // nb
    tq = tk = min(ATT_TILE, seq)
    nq = seq // tq

    def body(qkv_ref, lt_ref, do_ref, _, d_ref):
        q_at, k_at, v_at = _qkv_readers((qkv_ref,), True)
        upto = _tri(tk, lambda r, c: r <= c)
        before = _tri(tk, lambda r, c: r < c)
        dk_ref = d_ref.at[:, LANES:2 * LANES]
        dv_ref = d_ref.at[:, 2 * LANES:3 * LANES]
        dk_ref[...] = jnp.zeros((seq, LANES), F32)
        dv_ref[...] = jnp.zeros((seq, LANES), F32)

        def q_tile(i, _):
            r0 = pl.multiple_of(i * tq, tq)
            qm = _split_heads((q_at(r0, tq) * SCALE).astype(BF16))
            dom = _split_heads(do_ref[pl.ds(r0, tq), :].astype(BF16))
            lt_t = lt_ref[pl.ds(r0, tq), :]
            lt_h = [lt_t[:, 0:1], lt_t[:, HEAD_DIM:HEAD_DIM + 1]]
            row = r0 + lax.broadcasted_iota(jnp.int32, (tq, tk), 0)

            def k_tile(j, carry, diagonal):
                pc, qc, dq = carry
                c0 = pl.multiple_of(j * tk, tk)
                kt = k_at(c0, tk).astype(BF16)
                vt = v_at(c0, tk).astype(BF16)
                km = _split_heads(kt)
                if diagonal:
                    strict = (c0 + lax.broadcasted_iota(jnp.int32, (tq, tk), 1)) < row
                hs = range(2)
                z = [lax.dot_general(qm[h], kt, NT_DIMS, preferred_element_type=F32) for h in hs]
                da = [lax.dot_general(dom[h], vt, NT_DIMS, preferred_element_type=F32) for h in hs]
                ls = [_log_sigmoid(z[h]) for h in hs]
                lk = [ls[h] - z[h] for h in hs]
                if diagonal:
                    lk = [jnp.where(strict, lk[h], 0.0) for h in hs]
                parts = [_split_bf16(lk[h]) for h in hs]
                pin = [jnp.dot(parts[h][0], upto, preferred_element_type=F32)
                       + jnp.dot(parts[h][1], upto, preferred_element_type=F32) for h in hs]
                att = [jnp.exp(ls[h] + (lt_h[h] - pc[h] - pin[h])) for h in hs]
                if diagonal:
                    att = [jnp.where(strict, att[h], 0.0) for h in hs]
                dg = [att[h] * da[h] for h in hs]
                qx = [qc[h] + jnp.dot(dg[h].astype(BF16), before, preferred_element_type=F32) for h in hs]
                sig = [jnp.exp(ls[h]) for h in hs]
                dz = [dg[h] * (1.0 - sig[h]) - sig[h] * qx[h] for h in hs]
                if diagonal:
                    dz = [jnp.where(strict, dz[h], 0.0) for h in hs]
                dzb = [dz[h].astype(BF16) for h in hs]
                attb = [att[h].astype(BF16) for h in hs]
                dq = dq + (jnp.dot(dzb[0], km[0], preferred_element_type=F32)
                           + jnp.dot(dzb[1], km[1], preferred_element_type=F32))
                dk_ref[pl.ds(c0, tk), :] += (lax.dot_general(dzb[0], qm[0], TN_DIMS, preferred_element_type=F32)
                                             + lax.dot_general(dzb[1], qm[1], TN_DIMS, preferred_element_type=F32))
                dv_ref[pl.ds(c0, tk), :] += (lax.dot_general(attb[0], dom[0], TN_DIMS, preferred_element_type=F32)
                                             + lax.dot_general(attb[1], dom[1], TN_DIMS, preferred_element_type=F32))
                return (tuple(pc[h] + jnp.sum(lk[h], axis=-1, keepdims=True) for h in hs),
                        tuple(qc[h] + jnp.sum(dg[h], axis=-1, keepdims=True) for h in hs), dq)

            zero = (jnp.zeros((tq, 1), F32),) * 2
            carry = lax.fori_loop(0, i, lambda j, cr: k_tile(j, cr, False), (zero, zero, jnp.zeros((tq, LANES), F32)))
            _, _, dq = k_tile(i, carry, True)
            d_ref[pl.ds(r0, tq), 0:LANES] = dq * SCALE
            return 0

        lax.fori_loop(0, nq, q_tile, 0)

    return pl.pallas_call(
        body, name=name, grid=(2, nb),
        in_specs=[_pair_spec(seq, 2 * MIX_SB, PAIR_W), _pair_spec(seq, 0), _pair_spec(seq, 2 * MIX_SB), ANY],
        out_specs=_pair_spec(seq, 2 * MIX_SB, PAIR_W), out_shape=jax.ShapeDtypeStruct(dbuf.shape, F32),
        input_output_aliases={3: 0}, compiler_params=_params(("parallel", "arbitrary")))(proj, ltot, do, dbuf)


def _lane_scan(x, reverse=False):
    n = x.shape[-1]
    lane = lax.broadcasted_iota(jnp.int32, x.shape, 1)
    k = 1
    while k < n:
        if reverse:
            x = x + jnp.where(lane < n - k, pltpu.roll(x, n - k, 1), 0.0)
        else:
            x = x + jnp.where(lane >= k, pltpu.roll(x, k, 1), 0.0)
        k *= 2
    return x


def _fox_gate_fwd(f_rows, b_rows):
    def body(f_ref, b_ref, o_ref):
        o_ref[...] = _lane_scan(_log_sigmoid(f_ref[...] + b_ref[...]))

    return pl.pallas_call(body, name="fox_gate_fwd", out_shape=jax.ShapeDtypeStruct(f_rows.shape, F32))(f_rows, b_rows)


def _fox_gate_bwd(dcum, f_rows, b_rows):
    def body(d_ref, f_ref, b_ref, df_ref, db_ref):
        z = f_ref[...] + b_ref[...]
        df = _lane_scan(d_ref[...], reverse=True) * jnp.exp(_log_sigmoid(-z))
        df_ref[...] = df
        rs = jnp.sum(df, axis=-1, keepdims=True)
        tot = rs
        for e in range(1, f_rows.shape[0] // N_HEADS):
            tot = tot + pltpu.roll(rs, e * N_HEADS, 0)
        db_ref[...] = tot

    return pl.pallas_call(
        body, name="fox_gate_bwd",
        out_shape=[jax.ShapeDtypeStruct(f_rows.shape, F32), jax.ShapeDtypeStruct((f_rows.shape[0], 1), F32)],
    )(dcum, f_rows, b_rows)


def _dil_tables(seq):
    t = min(ATT_TILE, seq)
    n = seq // t
    a = np.arange(t)
    d = (np.arange(n)[:, None, None] * t + a[None, :, None] - a[None, None, :]).astype(np.int64)
    count = np.zeros(d.shape, np.int64)
    for window, dil in DILATED_PATTERNS:
        count += (d >= 0) & (d % dil == 0) & (d // dil <= window // dil)
    nn = np.maximum(d, 0)
    max_exact = NUM_BUCKETS // 2
    nf = np.maximum(nn, 1).astype(np.float32)
    large = max_exact + (np.log(nf / np.float32(max_exact)) / np.float32(math.log(MAX_DISTANCE / max_exact))
                         * np.float32(NUM_BUCKETS - max_exact)).astype(np.int32)
    bucket = np.where(nn < max_exact, nn, np.minimum(large, NUM_BUCKETS - 1))
    bucket = np.where(count > 0, bucket, -1).astype(np.int32)
    logc = np.where(count > 0, np.log(np.maximum(count, 1)), MASKED).astype(np.float32)
    return bucket, logc


def _dil_bias(rel_bias, seq):
    bucket, logc = _dil_tables(seq)
    n, t, _ = bucket.shape

    def body(rb_ref, bk_ref, lc_ref, o_ref):
        h = pl.program_id(0) * 2 + pl.program_id(1)
        bk = bk_ref[...]
        out = lc_ref[...]
        for b in range(NUM_BUCKETS):
            out = jnp.where(bk == b, out + rb_ref[b, h], out)
        o_ref[...] = out

    full = pl.BlockSpec((n, t, t), lambda p, h: (0, 0, 0))
    return pl.pallas_call(
        body, name="dil_bias", grid=(2, 2),
        in_specs=[pl.BlockSpec(memory_space=pltpu.SMEM), full, full],
        out_specs=pl.BlockSpec((None, None, n, t, t), lambda p, h: (p, h, 0, 0, 0)),
        out_shape=jax.ShapeDtypeStruct((2, 2, n, t, t), F32),
        compiler_params=_params(("parallel", "parallel")))(rel_bias, jnp.asarray(bucket), jnp.asarray(logc))


def _dil_bias_bwd(dbias, seq):
    bucket, _ = _dil_tables(seq)
    n, t, _ = bucket.shape

    def body(d_ref, bk_ref, o_ref):
        bk = bk_ref[...]
        lane = lax.broadcasted_iota(jnp.int32, (1, LANES), 1)
        for b in range(NUM_BUCKETS):
            rowv = jnp.zeros((1, LANES), F32)
            for h in range(N_HEADS):
                s = jnp.sum(jnp.where(bk == b, d_ref[h // 2, h % 2], 0.0))
                rowv = jnp.where(lane == h, s, rowv)
            o_ref[b:b + 1, :] = rowv

    out = pl.pallas_call(body, name="dil_bias_bwd", out_shape=jax.ShapeDtypeStruct((NUM_BUCKETS, LANES), F32),
                         compiler_params=pltpu.CompilerParams(vmem_limit_bytes=VMEM_LIMIT))(dbias, jnp.asarray(bucket))
    return out[:, :N_HEADS]


def _shift_rows(x, k, row, fill=0.0):
    n = x.shape[0]
    if k > 0:
        return jnp.where(row >= k, pltpu.roll(x, k, 0), fill)
    return jnp.where(row < n + k, pltpu.roll(x, n + k, 0), fill)


def _row_scan(a, u, row, reverse=False):
    n = a.shape[0]
    k = 1
    while k < n:
        s = -k if reverse else k
        u = a * _shift_rows(u, s, row) + u
        a = a * _shift_rows(a, s, row, 1.0)
        k *= 2
    return u


def _sigmoid(x):
    return 1.0 / (1.0 + jnp.exp(-x))


def _gelu(g):
    return 0.5 * g * (1.0 + lax.erf(g * (2.0 ** -0.5)))


def _gelu_grad(g):
    return 0.5 * (1.0 + lax.erf(g * (2.0 ** -0.5))) + g * jnp.exp(-0.5 * g * g) * (1.0 / math.sqrt(2.0 * math.pi))


def _neg_expm1(x):
    small = -x * (1.0 + x * (0.5 + x * (1.0 / 6.0 + x * (1.0 / 24.0))))
    return jnp.where(x > -0.03, small, 1.0 - jnp.exp(x))


def _lru_core(x, vec, wa, wx, row):
    xs = [_shift_rows(x, 3 - j, row) if j < 3 else x for j in range(4)]
    xc = vec[4:5, :]
    for j in range(4):
        xc = xc + vec[j:j + 1, :] * xs[j]
    xcb = xc.astype(BF16)
    r = _sigmoid(jnp.dot(xcb, wa, preferred_element_type=F32) + vec[5:6, :])
    ig = _sigmoid(jnp.dot(xcb, wx, preferred_element_type=F32) + vec[6:7, :])
    lam = vec[7:8, :]
    sp = jnp.maximum(-lam, 0.0) - _log_sigmoid(jnp.abs(lam))
    la = -LRU_C * r * sp
    a = jnp.exp(la)
    mult = jnp.sqrt(_neg_expm1(2.0 * la))
    return xs, xc, xcb, r, ig, sp, la, a, mult


def _lru_specs(seq):
    xg = pl.BlockSpec((seq, LRU_W), lambda hf, b: (b, COL_LRU // LRU_W + hf))
    mix = pl.BlockSpec((seq, LANES), lambda hf, b: (b, 2 * MIX_LRU + hf))
    vec = pl.BlockSpec((SUBLANES, LANES), lambda hf, b: (0, hf))
    mat = pl.BlockSpec((None, LANES, LANES), lambda hf, b: (hf, 0, 0))
    return xg, mix, vec, mat


def _lru_fwd(proj, vec, wa, wx, out_buf, *, nb, name):
    seq = proj.shape[0] // nb

    def body(xg_ref, vec_ref, wa_ref, wx_ref, _, o_ref):
        row = lax.broadcasted_iota(jnp.int32, (seq, LANES), 0)
        _, xc, _, _, ig, _, _, a, mult = _lru_core(xg_ref[:, 0:LANES], vec_ref[...], wa_ref[...], wx_ref[...], row)
        h = _row_scan(a, mult * (ig * xc), row)
        o_ref[...] = h * _gelu(xg_ref[:, LANES:LRU_W])

    xg, mix, vecs, mat = _lru_specs(seq)
    return pl.pallas_call(
        body, name=name, grid=(2, nb), in_specs=[xg, vecs, mat, mat, ANY], out_specs=mix,
        out_shape=jax.ShapeDtypeStruct(out_buf.shape, F32), input_output_aliases={4: 0},
        compiler_params=_params(("parallel", "arbitrary")))(proj, vec, wa, wx, out_buf)


def _lru_bwd(proj, vec, wa, wx, dout, dbuf, *, nb, name):
    seq = proj.shape[0] // nb

    def body(xg_ref, vec_ref, wa_ref, wx_ref, do_ref, _, d_ref, dvec_ref, dwa_ref, dwx_ref):
        row = lax.broadcasted_iota(jnp.int32, (seq, LANES), 0)
        vec_, wa_, wx_ = vec_ref[...], wa_ref[...], wx_ref[...]
        xs, xc, xcb, r, ig, sp, la, a, mult = _lru_core(xg_ref[:, 0:LANES], vec_, wa_, wx_, row)
        h = _row_scan(a, mult * (ig * xc), row)
        gate, do = xg_ref[:, LANES:LRU_W], do_ref[...]
        d_ref[:, LANES:LRU_W] = do * h * _gelu_grad(gate)
        dh = do * _gelu(gate)
        gacc = _row_scan(_shift_rows(a, -1, row), dh, row, reverse=True)
        da = gacc * _shift_rows(h, 1, row)
        dmult = gacc * (ig * xc)
        dig = gacc * (mult * xc)
        dxc = gacc * (mult * ig)
        dla = da * a - dmult * (a * a) / mult
        dr = (-LRU_C) * sp * dla
        dsp = jnp.sum((-LRU_C) * r * dla, axis=0, keepdims=True)
        dpr = dr * r * (1.0 - r)
        dpi = dig * ig * (1.0 - ig)
        dprb, dpib = dpr.astype(BF16), dpi.astype(BF16)
        dxc = (dxc + lax.dot_general(dprb, wa_, NT_DIMS, preferred_element_type=F32)
               + lax.dot_general(dpib, wx_, NT_DIMS, preferred_element_type=F32))
        dx = vec_[3:4, :] * dxc
        for j in range(3):
            dx = dx + vec_[j:j + 1, :] * _shift_rows(dxc, -(3 - j), row)
        d_ref[:, 0:LANES] = dx

        @pl.when(pl.program_id(1) == 0)
        def _():
            dvec_ref[...] = jnp.zeros_like(dvec_ref)
            dwa_ref[...] = jnp.zeros_like(dwa_ref)
            dwx_ref[...] = jnp.zeros_like(dwx_ref)

        for j in range(4):
            dvec_ref[j:j + 1, :] += jnp.sum(dxc * xs[j], axis=0, keepdims=True)
        dvec_ref[4:5, :] += jnp.sum(dxc, axis=0, keepdims=True)
        dvec_ref[5:6, :] += jnp.sum(dpr, axis=0, keepdims=True)
        dvec_ref[6:7, :] += jnp.sum(dpi, axis=0, keepdims=True)
        lam = vec_[7:8, :]
        dvec_ref[7:8, :] += -dsp * _sigmoid(-lam)
        dwa_ref[...] += lax.dot_general(xcb, dprb, TN_DIMS, preferred_element_type=F32)
        dwx_ref[...] += lax.dot_general(xcb, dpib, TN_DIMS, preferred_element_type=F32)

    xg, mix, vecs, mat = _lru_specs(seq)
    return pl.pallas_call(
        body, name=name, grid=(2, nb), in_specs=[xg, vecs, mat, mat, mix, ANY], out_specs=[xg, vecs, mat, mat],
        out_shape=[jax.ShapeDtypeStruct(dbuf.shape, F32), jax.ShapeDtypeStruct((SUBLANES, 2 * LANES), F32),
                   jax.ShapeDtypeStruct((2, LANES, LANES), F32), jax.ShapeDtypeStruct((2, LANES, LANES), F32)],
        input_output_aliases={5: 0},
        compiler_params=_params(("parallel", "arbitrary")))(proj, vec, wa, wx, dout, dbuf)


FFN_ROWS = 256
FFN_COLS = 1408


def _with_halo(halo, x, k):
    xx = jnp.concatenate([halo, x], axis=0)
    return pltpu.roll(xx, k, 0)[SUBLANES:, :]


def _ffn_conv(x_ref, halo_ref, cw, pos):
    x, halo = x_ref[...], halo_ref[...]
    x1 = jnp.where(pos >= 1, _with_halo(halo, x, 1), 0.0)
    x2 = jnp.where(pos >= 2, _with_halo(halo, x, 2), 0.0)
    return cw[3:4, :] + cw[0:1, :] * x2 + cw[1:2, :] * x1 + cw[2:3, :] * x, x1, x2


def _ffn_specs(tm, tn, gate_off):
    prev = lambda i: jnp.maximum(i * (tm // SUBLANES) - 1, 0)
    up = pl.BlockSpec((tm, tn), lambda j, i: (i, j))
    gate = pl.BlockSpec((tm, tn), lambda j, i: (i, j + gate_off))
    up_h = pl.BlockSpec((SUBLANES, tn), lambda j, i: (prev(i), j))
    gate_h = pl.BlockSpec((SUBLANES, tn), lambda j, i: (prev(i), j + gate_off))
    cw_up = pl.BlockSpec((SUBLANES, tn), lambda j, i: (0, j))
    cw_gate = pl.BlockSpec((SUBLANES, tn), lambda j, i: (0, j + gate_off))
    return up, gate, up_h, gate_h, cw_up, cw_gate


def _ffn_act(hf, cw, *, seq, name):
    t, w2 = hf.shape
    w = w2 // 2
    tm, tn = _tile(seq, FFN_ROWS, SUBLANES), _tile(w, FFN_COLS)

    def body(u_ref, g_ref, uh_ref, gh_ref, cu_ref, cg_ref, o_ref):
        pos = (pl.program_id(1) * tm + lax.broadcasted_iota(jnp.int32, (tm, 1), 0)) % seq
        up, _, _ = _ffn_conv(u_ref, uh_ref, cu_ref[...], pos)
        gate, _, _ = _ffn_conv(g_ref, gh_ref, cg_ref[...], pos)
        o_ref[...] = (_gelu(gate) * up).astype(BF16)

    specs = _ffn_specs(tm, tn, w // tn)
    return pl.pallas_call(
        body, name=name, grid=(w // tn, t // tm), in_specs=list(specs), out_specs=specs[0],
        out_shape=jax.ShapeDtypeStruct((t, w), BF16),
        compiler_params=_params(("parallel", "parallel")))(hf, hf, hf, hf, cw, cw)


def _ffn_bwd(hf, cw, dact, *, seq, name, comm=None):
    t, w2 = hf.shape
    w = w2 // 2
    tm, tn = _tile(seq, FFN_ROWS, 2 * SUBLANES), _tile(w, FFN_COLS)
    ext = tm + SUBLANES
    last = t // SUBLANES - 1

    def body(u_ref, g_ref, uh_ref, gh_ref, cu_ref, cg_ref, un_ref, gn_ref, da_ref, dn_ref, d_ref, dcu_ref, dcg_ref):
        pos = (pl.program_id(1) * tm + lax.broadcasted_iota(jnp.int32, (ext, 1), 0)) % seq

        def conv(x_ref, prev_ref, next_ref, cwv):
            xx = jnp.concatenate([prev_ref[...], x_ref[...], next_ref[...]], axis=0)
            x1 = jnp.where(pos >= 1, pltpu.roll(xx, 1, 0)[SUBLANES:, :], 0.0)
            x2 = jnp.where(pos >= 2, pltpu.roll(xx, 2, 0)[SUBLANES:, :], 0.0)
            x0 = xx[SUBLANES:, :]
            return cwv[3:4, :] + cwv[0:1, :] * x2 + cwv[1:2, :] * x1 + cwv[2:3, :] * x0, (x2, x1, x0)

        def back(d, cwv):
            d1 = jnp.where(pos < seq - 1, pltpu.roll(d, ext - 1, 0), 0.0)
            d2 = jnp.where(pos < seq - 2, pltpu.roll(d, ext - 2, 0), 0.0)
            return (cwv[2:3, :] * d + cwv[1:2, :] * d1 + cwv[0:1, :] * d2)[:tm, :].astype(BF16)

        cu, cg = cu_ref[...], cg_ref[...]
        up, u_taps = conv(u_ref, uh_ref, un_ref, cu)
        gate, g_taps = conv(g_ref, gh_ref, gn_ref, cg)
        da = jnp.concatenate([da_ref[...], dn_ref[...]], axis=0)
        cdf = 0.5 * (1.0 + lax.erf(gate * (2.0 ** -0.5)))
        d_up = da * (gate * cdf)
        d_gate = da * up * (cdf + gate * jnp.exp(-0.5 * gate * gate) * (1.0 / math.sqrt(2.0 * math.pi)))
        d_ref[0] = back(d_up, cu)
        d_ref[1] = back(d_gate, cg)

        @pl.when(pl.program_id(1) == 0)
        def _():
            dcu_ref[...] = jnp.zeros_like(dcu_ref)
            dcg_ref[...] = jnp.zeros_like(dcg_ref)

        for ref, d, taps in ((dcu_ref, d_up, u_taps), (dcg_ref, d_gate, g_taps)):
            own = d[:tm, :]
            for j in range(3):
                ref[j:j + 1, :] += jnp.sum(own * taps[j][:tm, :], axis=0, keepdims=True)
            ref[3:4, :] += jnp.sum(own, axis=0, keepdims=True)

    gate_off = w // tn
    specs = _ffn_specs(tm, tn, gate_off)
    tile, cwt = specs[0], specs[4]
    nxt = lambda i: jnp.minimum((i + 1) * (tm // SUBLANES), last)
    up_n = pl.BlockSpec((SUBLANES, tn), lambda j, i: (nxt(i), j))
    gate_n = pl.BlockSpec((SUBLANES, tn), lambda j, i: (nxt(i), j + gate_off))
    out, carried = _pallas(
        body, name=name, grid=(w // tn, t // tm), in_specs=list(specs) + [up_n, gate_n, tile, up_n],
        out_specs=[pl.BlockSpec((2, tm, tn), lambda j, i: (0, i, j)), cwt, cwt],
        out_shape=[jax.ShapeDtypeStruct((2, t, w), BF16), jax.ShapeDtypeStruct((SUBLANES, w), F32),
                   jax.ShapeDtypeStruct((SUBLANES, w), F32)],
        args=[hf, hf, hf, hf, cw, cw, hf, hf, dact, dact], sem=("parallel", "arbitrary"), comm=comm)
    return out if comm is None else (out, carried)


def _adamw(w, g, m, v, *, name, rows=256):
    nl, r, c = w.shape
    tr = _tile(r, rows, SUBLANES)

    def body(w_ref, g_ref, m_ref, v_ref, d_ref, nm_ref, nv_ref):
        g_ = g_ref[...]
        nm = ADAM_B1 * m_ref[...] + (1.0 - ADAM_B1) * g_
        nv = ADAM_B2 * v_ref[...] + (1.0 - ADAM_B2) * (g_ * g_)
        m_hat = nm / (1.0 - ADAM_B1 ** ADAM_STEP)
        v_hat = nv / (1.0 - ADAM_B2 ** ADAM_STEP)
        d_ref[...] = -ADAM_LR * (m_hat / (jnp.sqrt(v_hat) + ADAM_EPS) + ADAM_WD * w_ref[...])
        nm_ref[...] = nm
        nv_ref[...] = nv

    spec = pl.BlockSpec((None, tr, c), lambda l, i: (l, i, 0))
    shape = jax.ShapeDtypeStruct((nl, r, c), F32)
    return pl.pallas_call(body, name=name, grid=(nl, r // tr), in_specs=[spec] * 4, out_specs=[spec] * 3,
                          out_shape=[shape] * 3, compiler_params=_params(("parallel", "parallel")))(w, g, m, v)


def _mesh_pos():
    return lax.axis_index("x"), lax.axis_index("y"), lax.axis_index("c")


def _peers(x, y):
    chips = [(1 - x, y), (x, 1 - y), (1 - x, 1 - y)]
    return [(px, py, 2 * px + py) for px, py in chips]


def _remote(src, dst, send_sems, recv_sems, idx, to):
    return pltpu.make_async_remote_copy(src, dst, send_sems.at[idx], recv_sems.at[idx], device_id=to,
                                        device_id_type=MESH)


class _Comm:
    def __init__(self, operands, out_shape, aliases, sems, copies):
        self.operands, self.out_shape, self.aliases, self.sems, self.copies = operands, out_shape, aliases, sems, copies

    def start(self, ins, outs, sems):
        for send, _ in self.copies(ins, outs, sems):
            send.start()

    def wait(self, ins, outs, sems):
        pairs = self.copies(ins, outs, sems)
        for _, recv in pairs:
            recv.wait_recv()
        for send, _ in pairs:
            send.wait_send()


def _pallas(body, *, name, grid, in_specs, out_specs, out_shape, args, aliases=None, scratch=(), sem, comm=None):
    n_in, n_out = len(in_specs), len(out_specs)
    aliases = dict(aliases or {})
    if comm is None:
        out = pl.pallas_call(body, name=name, grid=grid, in_specs=in_specs, out_specs=out_specs, out_shape=out_shape,
                             input_output_aliases=aliases, scratch_shapes=list(scratch),
                             compiler_params=_params(sem))(*args)
        return list(out), []
    nci, nco, ncs = len(comm.operands), len(comm.out_shape), len(comm.sems)

    def carried(*refs):
        ins, cin = refs[:n_in], refs[n_in:n_in + nci]
        o0 = n_in + nci
        outs, cout = refs[o0:o0 + n_out], refs[o0 + n_out:o0 + n_out + nco]
        s0 = o0 + n_out + nco
        own, csem = refs[s0:len(refs) - ncs], refs[len(refs) - ncs:]
        ids = [pl.program_id(ax) for ax in range(len(grid))]
        first, last = ids[0] == 0, ids[0] == grid[0] - 1
        for i, g in zip(ids[1:], grid[1:]):
            first, last = jnp.logical_and(first, i == 0), jnp.logical_and(last, i == g - 1)

        @pl.when(first)
        def _():
            comm.start(cin, cout, csem)

        body(*ins, *outs, *own)

        @pl.when(last)
        def _():
            comm.wait(cin, cout, csem)

    aliases.update({n_in + i: n_out + j for i, j in comm.aliases.items()})
    out = pl.pallas_call(
        carried, name=name, grid=grid, in_specs=list(in_specs) + [ANY] * nci, out_specs=list(out_specs) + [ANY] * nco,
        out_shape=list(out_shape) + list(comm.out_shape), input_output_aliases=aliases,
        scratch_shapes=list(scratch) + list(comm.sems),
        compiler_params=_params(("arbitrary",) * len(grid)))(*args, *comm.operands)
    return list(out[:n_out]), list(out[n_out:])


def _run_comm(comm, *, name):
    nci, nco = len(comm.operands), len(comm.out_shape)

    def body(*refs):
        ins, outs, sems = refs[:nci], refs[nci:nci + nco], refs[nci + nco:]
        comm.start(ins, outs, sems)
        comm.wait(ins, outs, sems)

    return pl.pallas_call(body, name=name, in_specs=[ANY] * nci, out_specs=[ANY] * nco, out_shape=list(comm.out_shape),
                          input_output_aliases=dict(comm.aliases), scratch_shapes=list(comm.sems))(*comm.operands)


def _pair_sems(*shape):
    return [pltpu.SemaphoreType.DMA(shape), pltpu.SemaphoreType.DMA(shape)]


def _gather_comm(bufs, layer, stage):
    n = len(bufs)

    def copies(ins, outs, sems):
        x, y, c = _mesh_pos()
        me = 2 * x + y
        pairs = []
        for i in range(n):
            h = bufs[i].shape[2] // 2
            mine, other = pl.ds(c * h, h), pl.ds((1 - c) * h, h)
            for r, (px, py, k) in enumerate(_peers(x, y)):
                if stage == 0:
                    send = _remote(ins[i].at[me, layer, mine, :], outs[i].at[me, layer, mine, :], *sems, (i, r), (px, py, c))
                    land = outs[i].at[k, layer, mine, :]
                    recv = _remote(land, land, *sems, (i, r), (px, py, c))
                else:
                    send = _remote(ins[i].at[k, layer, mine, :], outs[i].at[k, layer, mine, :], *sems, (i, r), (x, y, 1 - c))
                    land = outs[i].at[k, layer, other, :]
                    recv = _remote(land, land, *sems, (i, r), (x, y, 1 - c))
                pairs.append((send, recv))
        return pairs

    return _Comm(bufs, [jax.ShapeDtypeStruct(b.shape, b.dtype) for b in bufs], {i: i for i in range(n)},
                 _pair_sems(n, 3), copies)


def _reduce_sibling_comm(gs):
    n = len(gs)

    def copies(ins, outs, sems):
        x, y, c = _mesh_pos()
        pairs = []
        for i in range(n):
            h = gs[i].shape[1] // 2
            cp = _remote(ins[i].at[:, pl.ds((1 - c) * h, h), :], outs[i], *sems, i, (x, y, 1 - c))
            pairs.append((cp, cp))
        return pairs

    return _Comm(gs, [jax.ShapeDtypeStruct((g.shape[0], g.shape[1] // 2, g.shape[2]), g.dtype) for g in gs], {},
                 _pair_sems(n), copies)


def _reduce_chips_comm(ps):
    n = len(ps)

    def copies(ins, outs, sems):
        x, y, c = _mesh_pos()
        pairs = []
        for i in range(n):
            for r, (px, py, k) in enumerate(_peers(x, y)):
                cp = _remote(ins[i].at[k], outs[i].at[r], *sems, (i, r), (px, py, c))
                pairs.append((cp, cp))
        return pairs

    return _Comm(ps, [jax.ShapeDtypeStruct((3,) + p.shape[1:], p.dtype) for p in ps], {}, _pair_sems(n, 3), copies)


def _share_halves(bufs, *, name):
    n = len(bufs)

    def body(*refs):
        ins, outs = refs[:n], refs[n:2 * n]
        send_sems, recv_sems = refs[2 * n:]
        x, y, c = _mesh_pos()
        cps = []
        for i in range(n):
            h = bufs[i].shape[1] // 2
            mine = pl.ds(c * h, h)
            cp = _remote(ins[i].at[:, mine, :], outs[i].at[:, mine, :], send_sems, recv_sems, i, (x, y, 1 - c))
            cp.start()
            cps.append(cp)
        for cp in cps:
            cp.wait()

    return pl.pallas_call(
        body, name=name, in_specs=[ANY] * n, out_specs=[ANY] * n,
        out_shape=[jax.ShapeDtypeStruct(b.shape, b.dtype) for b in bufs],
        input_output_aliases={i: i for i in range(n)},
        scratch_shapes=[pltpu.SemaphoreType.DMA((n,)), pltpu.SemaphoreType.DMA((n,))])(*bufs)


def _add_own_half(full, recv, pos, *, name, rows=256):
    k4, h, n = recv.shape
    tr = _tile(h, rows, 16)
    nblk = h // tr

    def body(pos_ref, a_ref, b_ref, o_ref):
        o_ref[...] = (a_ref[...] + b_ref[...]).astype(BF16)

    grid_spec = pltpu.PrefetchScalarGridSpec(
        num_scalar_prefetch=1, grid=(k4, nblk),
        in_specs=[pl.BlockSpec((None, tr, n), lambda k, i, pos_ref: (k, pos_ref[1] * nblk + i, 0)),
                  pl.BlockSpec((None, tr, n), lambda k, i, pos_ref: (k, i, 0))],
        out_specs=pl.BlockSpec((None, tr, n), lambda k, i, pos_ref: (k, i, 0)))
    return pl.pallas_call(body, name=name, grid_spec=grid_spec, out_shape=jax.ShapeDtypeStruct(recv.shape, BF16),
                          compiler_params=_params(("parallel", "parallel")))(pos, full, recv)


def _sum_into(own, others, buf, pos, layer, *, name, rows=256):
    _, h, n = own.shape
    tr = _tile(h, rows, 16)
    nblk = h // tr

    def body(pos_ref, own_ref, oth_ref, _, o_ref):
        acc = own_ref[...].astype(F32)
        for r in range(3):
            acc = acc + oth_ref[r].astype(F32)
        o_ref[...] = acc

    grid_spec = pltpu.PrefetchScalarGridSpec(
        num_scalar_prefetch=1, grid=(nblk,),
        in_specs=[pl.BlockSpec((None, tr, n), lambda i, pos_ref: (pos_ref[0], i, 0)),
                  pl.BlockSpec((3, tr, n), lambda i, pos_ref: (0, i, 0)), ANY],
        out_specs=pl.BlockSpec((None, tr, n), lambda i, pos_ref: (layer, pos_ref[1] * nblk + i, 0)))
    return pl.pallas_call(body, name=name, grid_spec=grid_spec, out_shape=jax.ShapeDtypeStruct(buf.shape, F32),
                          input_output_aliases={3: 0}, compiler_params=_params(("parallel",)))(pos, own, others, buf)


def _sibling_pair(buf, *, name):
    def body(src_ref, out_ref, send_sem, recv_sem, local_sem):
        x, y, c = _mesh_pos()
        local = pltpu.make_async_copy(src_ref, out_ref.at[c], local_sem)
        local.start()
        cp = pltpu.make_async_remote_copy(src_ref, out_ref.at[c], send_sem, recv_sem, device_id=(x, y, 1 - c),
                                          device_id_type=MESH)
        cp.start()
        cp.wait()
        local.wait()

    return pl.pallas_call(
        body, name=name, in_specs=[ANY], out_specs=ANY, out_shape=jax.ShapeDtypeStruct((2,) + buf.shape, buf.dtype),
        scratch_shapes=[pltpu.SemaphoreType.DMA, pltpu.SemaphoreType.DMA, pltpu.SemaphoreType.DMA])(buf)


def _chip_bcast(buf, *, name):
    def body(src_ref, out_ref, send_sems, recv_sems, local_sem):
        x, y, c = _mesh_pos()
        me = 2 * x + y
        local = pltpu.make_async_copy(src_ref, out_ref.at[me], local_sem)
        local.start()
        sends = []
        for r, (px, py, _) in enumerate(_peers(x, y)):
            cp = _remote(src_ref, out_ref.at[me], send_sems, recv_sems, r, (px, py, c))
            cp.start()
            sends.append(cp)
        for r, (px, py, k) in enumerate(_peers(x, y)):
            _remote(src_ref, out_ref.at[k], send_sems, recv_sems, r, (px, py, c)).wait_recv()
        for cp in sends:
            cp.wait_send()
        local.wait()

    return pl.pallas_call(
        body, name=name, in_specs=[ANY], out_specs=ANY, out_shape=jax.ShapeDtypeStruct((4,) + buf.shape, buf.dtype),
        scratch_shapes=[pltpu.SemaphoreType.DMA((3,)), pltpu.SemaphoreType.DMA((3,)), pltpu.SemaphoreType.DMA])(buf)


def _sum_slots(buf, *, name, rows=384):
    r, n = buf.shape[-2:]
    k = int(np.prod(buf.shape[:-2]))
    tr = _tile(r, rows, SUBLANES)

    def body(b_ref, o_ref):
        acc = b_ref[0]
        for s in range(1, k):
            acc = acc + b_ref[s]
        o_ref[...] = acc

    return pl.pallas_call(
        body, name=name, grid=(r // tr,), in_specs=[pl.BlockSpec((k, tr, n), lambda i: (0, i, 0))],
        out_specs=pl.BlockSpec((tr, n), lambda i: (i, 0)), out_shape=jax.ShapeDtypeStruct((r, n), F32),
        compiler_params=_params(("parallel",)))(buf.reshape((k, r, n)))


ROW = 1024
BIG = (("w_in", 2), ("w_out", 1), ("w_cq", 1), ("w_ck", 1), ("w_cv", 1), ("w_co", 2), ("w_up", 2), ("w_down", 1))
CONV = ("lru_conv_w", "ffn_conv_w")
REPLICATED = ("norm_mix_g", "b_forget", "lru_conv_b", "lru_w_a", "lru_b_a", "lru_w_x", "lru_b_x", "lru_lambda",
              "norm_cross_g", "norm_mem_g", "norm_ffn_g", "ffn_conv_b", "rel_bias", "final_norm_g")
WEIGHTS = ('norm_mix_g', 'w_in', 'b_forget', 'lru_conv_w', 'lru_conv_b', 'lru_w_a', 'lru_b_a', 'lru_w_x', 'lru_b_x',
           'lru_lambda', 'w_out', 'norm_cross_g', 'norm_mem_g', 'w_cq', 'w_ck', 'w_cv', 'w_co', 'norm_ffn_g', 'w_up',
           'ffn_conv_w', 'ffn_conv_b', 'w_down', 'rel_bias', 'final_norm_g')
INPUTS = ("x", "mem") + WEIGHTS + ("loss_target",) + tuple("m_" + n for n in WEIGHTS) + tuple("v_" + n for n in WEIGHTS)


def _round_up(n, m):
    return -(-n // m) * m


class _Packing:
    def __init__(self, entries):
        self.entries, self.off = entries, {}
        o = 0
        for name, shape in entries:
            self.off[name] = o
            o += _round_up(int(np.prod(shape)), ROW)
        self.used = o
        self.rows = _round_up(o // ROW, SUBLANES)

    def pack(self, arrays):
        parts = []
        for name, shape in self.entries:
            n = int(np.prod(shape))
            parts.append(jnp.pad(arrays[name].reshape(n), (0, _round_up(n, ROW) - n)))
        tail = self.rows * ROW - self.used
        if tail:
            parts.append(jnp.zeros((tail,), F32))
        return jnp.concatenate(parts).reshape(self.rows, ROW)

    def unpack(self, flat, lead=()):
        flat = flat.reshape(lead + (self.rows * ROW,))
        out = {}
        for name, shape in self.entries:
            n = int(np.prod(shape))
            out[name] = lax.slice_in_dim(flat, self.off[name], self.off[name] + n, axis=len(lead)).reshape(
                lead + tuple(shape))
        return out


def _to_shards(g, axis):
    r, c = g.shape
    if axis == 1:
        return g.reshape(4, r // 4, c)
    return g.reshape(r, 4, c // 4).transpose(1, 0, 2)


def _from_shards(s, axis):
    _, nl, r, c = s.shape
    if axis == 1:
        return s.transpose(1, 0, 2, 3).reshape(nl, 4 * r, c)
    return s.transpose(1, 2, 0, 3).reshape(nl, r, 4 * c)


def _proj_blocks():
    blocks = []
    for mixer in (MIX_SB, MIX_FOX, MIX_DIL):
        for p in range(2):
            blocks += [ORIG_COL[mixer] + part * 2 * LANES + p * LANES for part in range(3)]
    for hf in range(2):
        blocks += [ORIG_LRU_X + hf * LANES, ORIG_LRU_G + hf * LANES]
    return blocks


def _pad_w_in(w):
    parts = [w[..., s:s + LANES] for s in _proj_blocks()]
    parts += [w[..., 1536:1540], jnp.zeros(w.shape[:-1] + (PROJ_W - COL_F - N_HEADS,), w.dtype)]
    return jnp.concatenate(parts, axis=-1)


def _unpad_w_in(wp):
    blocks = _proj_blocks()
    order = sorted(range(len(blocks)), key=lambda i: blocks[i])
    parts = []
    for i in order:
        if blocks[i] == ORIG_COL[MIX_DIL]:
            parts.append(wp[..., COL_F:COL_F + N_HEADS])
        parts.append(wp[..., i * LANES:(i + 1) * LANES])
    return jnp.concatenate(parts, axis=-1)


def _block_diag(w):
    z = jnp.zeros((HEAD_DIM, HEAD_DIM), w.dtype)
    half = lambda a, b: jnp.concatenate([jnp.concatenate([a, z], 1), jnp.concatenate([z, b], 1)], 0)
    return jnp.stack([half(w[0], w[1]), half(w[2], w[3])])


def _block_diag_grad(d):
    return jnp.stack([d[0, :HEAD_DIM, :HEAD_DIM], d[0, HEAD_DIM:, HEAD_DIM:],
                      d[1, :HEAD_DIM, :HEAD_DIM], d[1, HEAD_DIM:, HEAD_DIM:]])


def _fox_layouts(cum, nb, seq):
    tk = min(ATT_TILE, seq)
    col = cum.reshape(nb, 2, 2, seq).transpose(0, 1, 3, 2)
    row = cum.reshape(nb, 2, 2, seq // tk, tk).transpose(0, 1, 3, 2, 4)
    return col, row


def _layer_params(w, l, nb):
    lru_vec = jnp.concatenate([w["lru_conv_w"][l], w["lru_conv_b"][l][None], w["lru_b_a"][l][None],
                               w["lru_b_x"][l][None], w["lru_lambda"][l][None]], axis=0)
    ffn_cw = jnp.concatenate([w["ffn_conv_w"][l], w["ffn_conv_b"][l][None],
                              jnp.zeros((SUBLANES - 4, 2 * D_FF), F32)], axis=0)
    return dict(
        w_in=w["w_in_padded"][l], lru_vec=lru_vec,
        wa=_block_diag(w["lru_w_a"][l]).astype(BF16), wx=_block_diag(w["lru_w_x"][l]).astype(BF16),
        ffn_cw=ffn_cw, b_rows=jnp.tile(w["b_forget"][l], nb).reshape(nb * N_HEADS, 1))


NORM_ROWS = 512


def _layer_fwd(x, h, mem, w, lp, l, next_g, bias, nb, slots=None):
    t, d = x.shape
    seq = t // nb
    tag = f"l{l}"
    sv = dict(x0=x)
    proj = _mm(h, lp["w_in"], name=tag + "_proj")
    comm = [None, None] if slots is None else [_gather_comm(slots, l + 1, 0), None]
    res = _sb_attn_fwd(proj, lax.empty((t, d), F32), nb=nb, name=tag + "_sb_fwd", comm=comm[0])
    (mixed, ltot), slots = (res, None) if slots is None else res
    f_rows = proj[:, COL_F:COL_F + N_HEADS].reshape(nb, seq, N_HEADS).transpose(0, 2, 1).reshape(nb * N_HEADS, seq)
    cum_col, cum_row = _fox_layouts(_fox_gate_fwd(f_rows, lp["b_rows"]), nb, seq)
    if slots is not None:
        comm[1] = _gather_comm(slots, l + 1, 1)
    res = _softmax_attn_fwd(proj, nb=nb, mode="fox", mixer=MIX_FOX, out_buf=mixed, extra=(cum_col, cum_row),
                            name=tag + "_fox_fwd", comm=comm[1])
    (mixed, lse_fox), slots = (res, None) if slots is None else res
    mixed, lse_dil = _softmax_attn_fwd(proj, nb=nb, mode="dil", mixer=MIX_DIL, out_buf=mixed, extra=(bias,),
                                       name=tag + "_dil_fwd")
    mixed = _lru_fwd(proj, lp["lru_vec"], lp["wa"], lp["wx"], mixed, nb=nb, name=tag + "_lru_fwd")
    x1, hq = _mm(mixed, w["w_out"][l], res=x, norm_g=w["norm_cross_g"][l], ti=NORM_ROWS, name=tag + "_out")
    memn = _rmsnorm(mem, w["norm_mem_g"][l], name=tag + "_norm_mem")
    q = _mm(hq, w["w_cq"][l], name=tag + "_cq")
    k = _mm(memn, w["w_ck"][l], name=tag + "_ck")
    v = _mm(memn, w["w_cv"][l], name=tag + "_cv")
    oc, lse_c = _softmax_attn_fwd((q, k, v), nb=nb, mode="cross", name=tag + "_cross_fwd")
    x2, hn = _mm(oc, w["w_co"][l], res=x1, norm_g=w["norm_ffn_g"][l], ti=NORM_ROWS, name=tag + "_co")
    hf = _mm(hn, w["w_up"][l], name=tag + "_up")
    act = _ffn_act(hf, lp["ffn_cw"], seq=seq, name=tag + "_ffn_act")
    if next_g is None:
        x3, h_next = _mm(act, w["w_down"][l], res=x2, name=tag + "_down"), None
    else:
        x3, h_next = _mm(act, w["w_down"][l], res=x2, norm_g=next_g, ti=NORM_ROWS, name=tag + "_down")
    sv.update(h=h, proj=proj, ltot=ltot, f_rows=f_rows, cum_col=cum_col, cum_row=cum_row, lse_fox=lse_fox,
              lse_dil=lse_dil, mixed=mixed, x1=x1, hq=hq, memn=memn, q=q, k=k, v=v, oc=oc, lse_c=lse_c, x2=x2,
              hn=hn, hf=hf, act=act)
    return x3, h_next, sv, slots


class _PendingReduce:
    def __init__(self, full, pos, layer):
        self.full, self.pos, self.layer = full, pos, layer

    def sibling_comm(self):
        return _reduce_sibling_comm(self.full)

    def add(self, from_sibling):
        self.partial = [_add_own_half(f, r, self.pos, name=f"l{self.layer}_reduce_add_{n}")
                        for f, r, (n, _) in zip(self.full, from_sibling, BIG)]

    def chips_comm(self):
        return _reduce_chips_comm(self.partial)

    def finish(self, others, g_shard):
        return [_sum_into(p, o, buf, self.pos, self.layer, name=f"l{self.layer}_reduce_sum_{n}")
                for p, o, buf, (n, _) in zip(self.partial, others, g_shard, BIG)]


def _layer_bwd(dx3, mem, sv, w, lp, l, bias, nb, pending=None, g_shard=None):
    t = dx3.shape[0]
    seq = t // nb
    tag = f"l{l}"
    g = {}
    if pending is None:
        g["w_down"] = _mm(sv["act"], dx3, ta=True, name=tag + "_dw_down")
    else:
        g["w_down"], from_sibling = _mm(sv["act"], dx3, ta=True, comm=pending.sibling_comm(), name=tag + "_dw_down")
        pending.add(from_sibling)
    dact = _mm(dx3, w["w_down"][l], tb=True, name=tag + "_dact")
    if pending is None:
        dhf, dcu, dcg = _ffn_bwd(sv["hf"], lp["ffn_cw"], dact, seq=seq, name=tag + "_ffn_bwd")
    else:
        (dhf, dcu, dcg), others = _ffn_bwd(sv["hf"], lp["ffn_cw"], dact, seq=seq, name=tag + "_ffn_bwd",
                                           comm=pending.chips_comm())
        g_shard = pending.finish(others, g_shard)
    dcw = jnp.concatenate([dcu, dcg], axis=1)
    g["ffn_conv_w"], g["ffn_conv_b"] = dcw[:3], dcw[3]
    g["w_up"] = _mm(sv["hn"], dhf, ta=True, halves="b", col_shards=4, name=tag + "_dw_up")
    dx2, dg = _mm(dhf, w["w_up"][l], tb=True, halves="a", norm_bwd=(sv["x2"], w["norm_ffn_g"][l], dx3),
                  ti=NORM_ROWS, name=tag + "_dhn")
    g["norm_ffn_g"] = dg.reshape(-1)
    g["w_co"] = _mm(sv["oc"], dx2, ta=True, col_shards=4, name=tag + "_dw_co")
    doc = _mm(dx2, w["w_co"][l], tb=True, name=tag + "_doc")
    dq, dk, dv = _softmax_attn_bwd((sv["q"], sv["k"], sv["v"]), sv["oc"], sv["lse_c"], doc, nb=nb, mode="cross",
                                   name=tag + "_cross_bwd")
    g["w_cq"] = _mm(sv["hq"], dq, ta=True, name=tag + "_dw_cq")
    g["w_ck"] = _mm(sv["memn"], dk, ta=True, name=tag + "_dw_ck")
    g["w_cv"] = _mm(sv["memn"], dv, ta=True, name=tag + "_dw_cv")
    dx1, dg = _mm(dq, w["w_cq"][l], tb=True, norm_bwd=(sv["x1"], w["norm_cross_g"][l], dx2), ti=NORM_ROWS,
                  name=tag + "_dhq")
    g["norm_cross_g"] = dg.reshape(-1)
    dmemn = _mm(dv, w["w_cv"][l], tb=True, res=_mm(dk, w["w_ck"][l], tb=True, name=tag + "_dmem_k"),
                name=tag + "_dmem_v")
    _, g["norm_mem_g"] = _rmsnorm_bwd(dmemn, mem, w["norm_mem_g"][l], None, name=tag + "_norm_mem_bwd")
    mixed, proj = sv["mixed"], sv["proj"]
    g["w_out"] = _mm(mixed, dx1, ta=True, name=tag + "_dw_out")
    dmixed = _mm(dx1, w["w_out"][l], tb=True, name=tag + "_dmixed")
    dproj = _sb_attn_bwd(proj, sv["ltot"], dmixed, lax.empty((t, PROJ_W), F32), nb=nb, name=tag + "_sb_bwd")
    dproj, dcum_k, dcum_q = _softmax_attn_bwd(
        proj, mixed, sv["lse_fox"], dmixed, nb=nb, mode="fox", mixer=MIX_FOX, dbuf=dproj,
        extra=(sv["cum_col"], sv["cum_row"]), name=tag + "_fox_bwd")
    dcum = (dcum_k.transpose(0, 1, 3, 2, 4).reshape(nb * N_HEADS, seq)
            + dcum_q.transpose(0, 1, 3, 2).reshape(nb * N_HEADS, seq))
    df_rows, db = _fox_gate_bwd(dcum, sv["f_rows"], lp["b_rows"])
    g["b_forget"] = db[:N_HEADS, 0]
    df = df_rows.reshape(nb, N_HEADS, seq).transpose(0, 2, 1).reshape(t, N_HEADS)
    dproj, dbias = _softmax_attn_bwd(proj, mixed, sv["lse_dil"], dmixed, nb=nb, mode="dil", mixer=MIX_DIL,
                                     dbuf=dproj, extra=(bias,), name=tag + "_dil_bwd")
    dproj, dvec, dwa, dwx = _lru_bwd(proj, lp["lru_vec"], lp["wa"], lp["wx"], dmixed, dproj, nb=nb,
                                     name=tag + "_lru_bwd")
    g["lru_conv_w"], g["lru_conv_b"], g["lru_b_a"], g["lru_b_x"], g["lru_lambda"] = (
        dvec[0:4], dvec[4], dvec[5], dvec[6], dvec[7])
    g["lru_w_a"], g["lru_w_x"] = _block_diag_grad(dwa), _block_diag_grad(dwx)
    dproj = lax.dynamic_update_slice(dproj, jnp.pad(df, ((0, 0), (0, PROJ_W - COL_F - N_HEADS))), (0, COL_F))
    g["w_in_padded"] = _mm(sv["h"], dproj, ta=True, name=tag + "_dw_in")
    dx0, dg = _mm(dproj, lp["w_in"], tb=True, norm_bwd=(sv["x0"], w["norm_mix_g"][l], dx1), ti=NORM_ROWS,
                  name=tag + "_dh")
    g["norm_mix_g"] = dg.reshape(-1)
    return dx0, g, dbias, g_shard


def _big_grad_shards(g):
    out = []
    for n, axis in BIG:
        if n in ("w_up", "w_co"):
            out.append(g[n])
        else:
            out.append(_to_shards(_unpad_w_in(g["w_in_padded"]) if n == "w_in" else g[n], axis))
    return out


def kernel(*args):
    a = dict(zip(INPUTS, args, strict=True))
    nb, seq, d = a["x"].shape
    depth = a["norm_mix_g"].shape[0]
    x = a["x"].reshape(nb * seq, d)
    mem = a["mem"].reshape(nb * a["mem"].shape[1], d)
    target = a["loss_target"].reshape(nb * seq, d)
    cx, cy, c = _mesh_pos()
    chip = 2 * cx + cy
    pos = jnp.stack([chip, c]).astype(jnp.int32)

    slots = []
    for n, _ in BIG:
        own = a[n].astype(BF16)[None]
        slots.append(lax.dynamic_update_slice(lax.empty((4,) + own.shape[1:], BF16), own, (chip,) + (0,) * (own.ndim - 1)))
    slots = _run_comm(_gather_comm(slots, 0, 0), name="gather_l0_chips")
    slots = _run_comm(_gather_comm(slots, 0, 1), name="gather_l0_sibling")
    w = {n: a[n] for n in REPLICATED}
    w.update({n: {} for n, _ in BIG}, w_in_padded={})

    def take_layer(bufs, l):
        for (n, axis), buf in zip(BIG, bufs):
            w[n][l] = _from_shards(buf[:, l:l + 1], axis)[0]
        w["w_in_padded"][l] = _pad_w_in(w["w_in"][l])

    take_layer(slots, 0)
    cpk = _Packing([(n, a[n].shape) for n in CONV])
    conv = cpk.unpack(_chip_bcast(cpk.pack({n: a[n] for n in CONV}), name="gather_conv"), lead=(4,))
    for n in CONV:
        w[n] = jnp.moveaxis(conv[n], 0, 2).reshape(a[n].shape[:2] + (4 * a[n].shape[2],))

    bias = _dil_bias(w["rel_bias"], seq)
    lps, saved = [], []
    h = _rmsnorm(x, w["norm_mix_g"][0], name="l0_norm_mix")
    for l in range(depth):
        more = l + 1 < depth
        lps.append(_layer_params(w, l, nb))
        x, h, sv, slots = _layer_fwd(x, h, mem, w, lps[l], l, w["norm_mix_g"][l + 1] if more else None, bias, nb,
                                     slots=slots if more else None)
        saved.append(sv)
        if more:
            take_layer(slots, l + 1)
    loss, dx, dg_final = _loss_head(x, w["final_norm_g"], target)
    small_g = [None] * depth
    dbias, pending = None, None
    g_shard = [lax.empty(a[n].shape, F32) for n, _ in BIG]
    for l in reversed(range(depth)):
        dx, g, db, g_shard = _layer_bwd(dx, mem, saved[l], w, lps[l], l, bias, nb, pending=pending, g_shard=g_shard)
        dbias = db if dbias is None else dbias + db
        small_g[l] = g
        pending = _PendingReduce(_big_grad_shards(g), pos, l)
    pending.add(_run_comm(pending.sibling_comm(), name="reduce_sibling"))
    g_shard = pending.finish(_run_comm(pending.chips_comm(), name="reduce_chips"), g_shard)
    g_shard = _share_halves(g_shard, name="reduce_share")
    out = {}
    for (n, _), gs in zip(BIG, g_shard):
        delta, new_m, new_v = _adamw(a[n], gs, a["m_" + n], a["v_" + n], name="adamw_" + n)
        out[n] = (gs, delta, new_m, new_v)

    grads = {n: jnp.stack([small_g[l][n] for l in range(depth)]) for n in REPLICATED + CONV
             if n not in ("rel_bias", "final_norm_g")}
    grads["rel_bias"] = _dil_bias_bwd(dbias, seq)
    grads["final_norm_g"] = dg_final
    grads["loss"] = loss.reshape(1)
    spk = _Packing([(n, grads[n].shape) for n in REPLICATED + CONV + ("loss",)])
    s_all = _chip_bcast(_sibling_pair(spk.pack(grads), name="small_sibling"), name="small_chips")
    total = spk.unpack(_sum_slots(s_all, name="small_sum"))
    for n in CONV:
        width = a[n].shape[2]
        total[n] = lax.dynamic_slice_in_dim(total[n], chip * width, width, axis=2)
    apk = _Packing([(n, a[n].shape) for n in REPLICATED + CONV])
    s_out = _adamw(*[apk.pack(src)[None] for src in (
        {n: a[n] for n in REPLICATED + CONV}, total, {n: a["m_" + n] for n in REPLICATED + CONV},
        {n: a["v_" + n] for n in REPLICATED + CONV})], name="adamw_small")
    s_delta, s_m, s_v = [apk.unpack(o[0]) for o in s_out]
    for n in REPLICATED + CONV:
        out[n] = (total[n], s_delta[n], s_m[n], s_v[n])

    return (total["loss"].reshape(()), dx.reshape(nb, seq, d), *[out[n][0] for n in WEIGHTS],
            *[out[n][1] for n in WEIGHTS], *[out[n][2] for n in WEIGHTS], *[out[n][3] for n in WEIGHTS])
```

```python
import math

import numpy as np
import jax
import jax.numpy as jnp
from jax import lax
from jax.experimental import pallas as pl
from jax.experimental.pallas import tpu as pltpu

F32 = jnp.float32
BF16 = jnp.bfloat16

HEAD_DIM = 64
N_HEADS = 4
N_IN = 2820
D_FF = 2816
LRU_C = 8.0
EPS = 1e-6
NUM_BUCKETS = 32
MAX_DISTANCE = 2048
DILATED_PATTERNS = ((128, 1), (512, 4), (2048, 16))
ADAM_LR, ADAM_B1, ADAM_B2, ADAM_EPS, ADAM_WD, ADAM_STEP = 0.001, 0.9, 0.999, 1e-08, 0.01, 10

LANES = 128
SUBLANES = 8
VMEM_LIMIT = 48 * 1024 * 1024

PROJ_W = 3072
PAIR_W = 3 * LANES
LRU_W = 2 * LANES
COL_LRU = 6 * PAIR_W
COL_F = COL_LRU + 2 * LRU_W
MIX_SB, MIX_FOX, MIX_DIL, MIX_LRU = 0, 1, 2, 3
ORIG_COL = {MIX_SB: 0, MIX_FOX: 768, MIX_DIL: 1540}
ORIG_LRU_X, ORIG_LRU_G = 2308, 2564

ATT_TILE = 256
MASKED = -1e30
SCALE = HEAD_DIM ** -0.5

NT_DIMS = (((1,), (1,)), ((), ()))
TN_DIMS = (((0,), (0,)), ((), ()))

MESH = pl.DeviceIdType.MESH
ANY = pl.BlockSpec(memory_space=pl.ANY)


def _params(sem):
    return pltpu.CompilerParams(dimension_semantics=sem, vmem_limit_bytes=VMEM_LIMIT)


def _tile(n, target, unit=LANES):
    if n <= target:
        return n
    t = (target // unit) * unit
    while t > unit and n % t:
        t -= unit
    assert n % t == 0, (n, target, unit)
    return t


def _mm(a, b, *, ta=False, tb=False, res=None, col_shards=1, halves=None, norm_g=None, norm_bwd=None, comm=None,
        name, ti=1024, tj=1408, tc=1408):
    if halves == "a":
        m, kc = a.shape[1], 2 * a.shape[2]
    else:
        m, kc = (a.shape[1], a.shape[0]) if ta else a.shape
    if halves == "b":
        n = 2 * b.shape[2]
        assert b.shape[1] == kc
    else:
        n = b.shape[0] if tb else b.shape[1]
        assert (b.shape[1] if tb else b.shape[0]) == kc
    assert n % col_shards == 0
    n_blk = n // (2 if halves == "b" else col_shards)
    k_blk = kc // 2 if halves == "a" else kc
    ti, tj, tc = (_tile(m, ti, LANES if ta else SUBLANES), _tile(n_blk, tj),
                  _tile(k_blk, tc, SUBLANES if ta and tb else LANES))
    per_shard, per_half_j, per_half_k = n // col_shards // tj, n_blk // tj, k_blk // tc
    nk = kc // tc
    dims = (((0 if ta else 1,), (1 if tb else 0,)), ((), ()))
    rows_whole = norm_g is not None or norm_bwd is not None
    assert not rows_whole or (tj == n and col_shards == 1)
    n_extra = (res is not None) + (norm_g is not None) + (3 if norm_bwd is not None else 0)
    n_out = 2 if rows_whole else 1

    def finish(val, ex, outs):
        if res is not None:
            val = ex[0][...] + val
        if norm_g is not None:
            outs[0][...] = val
            outs[1][...] = (_xhat(val) * ex[-1][...]).astype(BF16)
        elif norm_bwd is not None:
            x_ref, g_ref, r_ref = ex[-3:]
            dx, dgr = _norm_bwd_rows(val, x_ref[...], g_ref[...])
            outs[0][...] = r_ref[...] + dx

            @pl.when(pl.program_id(0) == 0)
            def _():
                outs[1][...] = jnp.zeros_like(outs[1])

            outs[1][...] += jnp.sum(dgr, axis=0, keepdims=True)
        else:
            outs[0][...] = val

    def body(*refs):
        a_ref, b_ref = refs[:2]
        ex = refs[2:2 + n_extra]
        outs = refs[2 + n_extra:2 + n_extra + n_out]
        part = lax.dot_general(a_ref[...].astype(BF16), b_ref[...].astype(BF16), dims, preferred_element_type=F32)
        if nk == 1:
            finish(part, ex, outs)
            return
        acc_ref = refs[-1]
        k = pl.program_id(2)

        @pl.when(k == 0)
        def _():
            acc_ref[...] = part

        @pl.when(k > 0)
        def _():
            acc_ref[...] += part

        @pl.when(k == nk - 1)
        def _():
            finish(acc_ref[...], ex, outs)

    if halves == "a":
        a_spec = pl.BlockSpec((None, ti, tc), lambda i, j, k: (k // per_half_k, i, k % per_half_k))
    elif ta:
        a_spec = pl.BlockSpec((tc, ti), lambda i, j, k: (k, i))
    else:
        a_spec = pl.BlockSpec((ti, tc), lambda i, j, k: (i, k))
    if halves == "b":
        b_spec = pl.BlockSpec((None, tc, tj), lambda i, j, k: (j // per_half_j, k, j % per_half_j))
    elif tb:
        b_spec = pl.BlockSpec((tj, tc), lambda i, j, k: (j, k))
    else:
        b_spec = pl.BlockSpec((tc, tj), lambda i, j, k: (k, j))
    o_spec = pl.BlockSpec((ti, tj), lambda i, j, k: (i, j))
    vec = pl.BlockSpec((1, tj), lambda i, j, k: (0, 0))
    in_specs, args = [a_spec, b_spec], [a, b]
    out_specs, out_shape = [o_spec], [jax.ShapeDtypeStruct((m, n), F32)]
    if res is not None:
        in_specs.append(o_spec)
        args.append(res)
    if norm_g is not None:
        in_specs.append(vec)
        args.append(norm_g.reshape(1, n))
        out_specs.append(o_spec)
        out_shape.append(jax.ShapeDtypeStruct((m, n), BF16))
    if norm_bwd is not None:
        x, g, dres = norm_bwd
        in_specs += [o_spec, vec, o_spec]
        args += [x, g.reshape(1, n), dres]
        out_specs.append(vec)
        out_shape.append(jax.ShapeDtypeStruct((1, n), F32))
    if col_shards > 1:
        assert n_extra == 0
        out_specs = [pl.BlockSpec((None, ti, tj), lambda i, j, k: (j // per_shard, i, j % per_shard))]
        out_shape = [jax.ShapeDtypeStruct((col_shards, m, n // col_shards), F32)]
    sem = ("arbitrary",) * 3 if norm_bwd is not None else ("parallel", "parallel", "arbitrary")
    out, carried = _pallas(body, name=name, grid=(m // ti, n // tj, nk), in_specs=in_specs, out_specs=out_specs,
                           out_shape=out_shape, args=args, scratch=[] if nk == 1 else [pltpu.VMEM((ti, tj), F32)],
                           sem=sem, comm=comm)
    out = out if rows_whole else out[0]
    return out if comm is None else (out, carried)


def _xhat(x):
    return x * lax.rsqrt(jnp.mean(x * x, axis=-1, keepdims=True) + EPS)


def _norm_bwd_rows(dy, x, g):
    rstd = lax.rsqrt(jnp.mean(x * x, axis=-1, keepdims=True) + EPS)
    xh = x * rstd
    dxh = dy * g
    dx = rstd * (dxh - xh * jnp.mean(dxh * xh, axis=-1, keepdims=True))
    return dx, dy * xh


def _rmsnorm(x, g, *, name, rows=512):
    t, d = x.shape
    tr = _tile(t, rows, 2 * SUBLANES)

    def body(x_ref, g_ref, o_ref):
        o_ref[...] = (_xhat(x_ref[...]) * g_ref[...]).astype(BF16)

    return pl.pallas_call(
        body, name=name, grid=(t // tr,),
        in_specs=[pl.BlockSpec((tr, d), lambda i: (i, 0)), pl.BlockSpec((1, d), lambda i: (0, 0))],
        out_specs=pl.BlockSpec((tr, d), lambda i: (i, 0)), out_shape=jax.ShapeDtypeStruct((t, d), BF16),
        compiler_params=_params(("parallel",)))(x, g.reshape(1, d))


def _rmsnorm_bwd(dy, x, g, dres, *, name, rows=512):
    t, d = x.shape
    tr = _tile(t, rows, SUBLANES)

    def body(*refs):
        if dres is None:
            dy_ref, x_ref, g_ref, dx_ref, dg_ref = refs
        else:
            dy_ref, x_ref, g_ref, r_ref, dx_ref, dg_ref = refs
        dx, dgr = _norm_bwd_rows(dy_ref[...], x_ref[...], g_ref[...])
        dx_ref[...] = dx if dres is None else r_ref[...] + dx

        @pl.when(pl.program_id(0) == 0)
        def _():
            dg_ref[...] = jnp.zeros_like(dg_ref)

        dg_ref[...] += jnp.sum(dgr, axis=0, keepdims=True)

    row = pl.BlockSpec((tr, d), lambda i: (i, 0))
    vec = pl.BlockSpec((1, d), lambda i: (0, 0))
    in_specs = [row, row, vec] + ([] if dres is None else [row])
    args = (dy, x, g.reshape(1, d)) + (() if dres is None else (dres,))
    dx, dg = pl.pallas_call(
        body, name=name, grid=(t // tr,), in_specs=in_specs, out_specs=[row, vec],
        out_shape=[jax.ShapeDtypeStruct((t, d), F32), jax.ShapeDtypeStruct((1, d), F32)],
        compiler_params=_params(("arbitrary",)))(*args)
    return dx, dg.reshape(d)


def _loss_head(x, g, target, *, rows=512):
    t, d = x.shape
    tr = _tile(t, rows, SUBLANES)

    def body(x_ref, g_ref, t_ref, dx_ref, dg_ref, loss_ref):
        x_, g_ = x_ref[...], g_ref[...]
        err = _xhat(x_) * g_ - t_ref[...]
        dx, dgr = _norm_bwd_rows(err * (1.0 / d), x_, g_)
        dx_ref[...] = dx

        @pl.when(pl.program_id(0) == 0)
        def _():
            dg_ref[...] = jnp.zeros_like(dg_ref)
            loss_ref[...] = jnp.zeros_like(loss_ref)

        dg_ref[...] += jnp.sum(dgr, axis=0, keepdims=True)
        loss_ref[...] += 0.5 * jnp.sum(jnp.mean(err * err, axis=-1, keepdims=True), axis=0, keepdims=True)

    row = pl.BlockSpec((tr, d), lambda i: (i, 0))
    vec = pl.BlockSpec((1, d), lambda i: (0, 0))
    one = pl.BlockSpec((1, 1), lambda i: (0, 0))
    dx, dg, loss = pl.pallas_call(
        body, name="loss_head", grid=(t // tr,), in_specs=[row, vec, row], out_specs=[row, vec, one],
        out_shape=[jax.ShapeDtypeStruct((t, d), F32), jax.ShapeDtypeStruct((1, d), F32),
                   jax.ShapeDtypeStruct((1, 1), F32)],
        compiler_params=_params(("arbitrary",)))(x, g.reshape(1, d), target)
    return loss.reshape(()), dx, dg.reshape(d)


def _head_masks(shape):
    lane = lax.broadcasted_iota(jnp.int32, shape, len(shape) - 1)
    return lane < HEAD_DIM, lane >= HEAD_DIM


def _split_heads(x):
    m0, m1 = _head_masks(x.shape)
    zero = jnp.zeros_like(x)
    return jnp.where(m0, x, zero), jnp.where(m1, x, zero)


def _lane_pair(a0, a1, rows):
    m0, _ = _head_masks((rows, LANES))
    return jnp.where(m0, a0, a1)


def _qkv_readers(refs, packed):
    if packed:
        (r,) = refs
        return tuple((lambda r0, n, s=s: r[pl.ds(r0, n), s * LANES:(s + 1) * LANES]) for s in range(3))
    return tuple((lambda r0, n, ref=ref: ref[pl.ds(r0, n), :]) for ref in refs)


def _pair_spec(seq, col0, width=LANES):
    return pl.BlockSpec((seq, width), lambda p, b: (b, col0 + p))


def _fox_specs(seq, nk, tk):
    return [pl.BlockSpec((None, None, seq, 2), lambda p, b: (b, p, 0, 0)),
            pl.BlockSpec((None, None, nk, 2, tk), lambda p, b: (b, p, 0, 0, 0))]


def _softmax_attn_fwd(src, *, nb, mode, mixer=None, out_buf=None, extra=(), name, comm=None):
    packed = mode != "cross"
    n_src = 1 if packed else 3
    seq_q = (src if packed else src[0]).shape[0] // nb
    seq_k = seq_q if packed else src[1].shape[0] // nb
    tq, tk = min(ATT_TILE, seq_q), min(ATT_TILE, seq_k)
    nq, nk = seq_q // tq, seq_k // tk
    n_ex = len(extra)

    def body(*refs):
        q_at, k_at, v_at = _qkv_readers(refs[:n_src], packed)
        ex = refs[n_src:n_src + n_ex]
        o_ref, lse_ref = refs[-2:]

        def q_tile(i, _):
            r0 = pl.multiple_of(i * tq, tq)
            qm = _split_heads((q_at(r0, tq) * SCALE).astype(BF16))
            if mode == "fox":
                cq = ex[0][pl.ds(r0, tq), :]
                row = r0 + lax.broadcasted_iota(jnp.int32, (tq, tk), 0)

            def k_tile(j, carry, diagonal=False):
                m, l, acc = carry
                c0 = pl.multiple_of(j * tk, tk)
                kt = k_at(c0, tk).astype(BF16)
                vm = _split_heads(v_at(c0, tk).astype(BF16))
                if mode == "fox":
                    ck = ex[1][j]
                hs = range(2)
                s = [lax.dot_general(qm[h], kt, NT_DIMS, preferred_element_type=F32) for h in hs]
                if mode == "fox":
                    s = [s[h] + cq[:, h:h + 1] - ck[h:h + 1, :] for h in hs]
                    if diagonal:
                        keep = (c0 + lax.broadcasted_iota(jnp.int32, (tq, tk), 1)) <= row
                        s = [jnp.where(keep, s[h], MASKED) for h in hs]
                elif mode == "dil":
                    s = [s[h] + ex[0][h, i - j] for h in hs]
                new_m = [jnp.maximum(m[h], jnp.max(s[h], axis=-1, keepdims=True)) for h in hs]
                p = [jnp.exp(s[h] - new_m[h]) for h in hs]
                alpha = [jnp.exp(m[h] - new_m[h]) for h in hs]
                new_l = [alpha[h] * l[h] + jnp.sum(p[h], axis=-1, keepdims=True) for h in hs]
                pv = [jnp.dot(p[h].astype(BF16), vm[h], preferred_element_type=F32) for h in hs]
                acc = acc * _lane_pair(alpha[0], alpha[1], tq) + (pv[0] + pv[1])
                return tuple(new_m), tuple(new_l), acc

            init = ((jnp.full((tq, 1), MASKED, F32),) * 2, (jnp.zeros((tq, 1), F32),) * 2,
                    jnp.zeros((tq, LANES), F32))
            if mode == "fox":
                m, l, acc = k_tile(i, lax.fori_loop(0, i, k_tile, init), True)
            else:
                m, l, acc = lax.fori_loop(0, i + 1 if packed else nk, k_tile, init)
            o_ref[pl.ds(r0, tq), :] = acc / _lane_pair(l[0], l[1], tq)
            lse_ref[pl.ds(r0, tq), :] = _lane_pair(m[0] + jnp.log(l[0]), m[1] + jnp.log(l[1]), tq)
            return 0

        lax.fori_loop(0, nq, q_tile, 0)

    lse_shape = jax.ShapeDtypeStruct((nb * seq_q, 2 * LANES), F32)
    if packed:
        in_specs, args = [_pair_spec(seq_q, 2 * mixer, PAIR_W)], [src]
        in_specs += _fox_specs(seq_q, nk, tk) if mode == "fox" else [
            pl.BlockSpec((None, 2, nq, tq, tk), lambda p, b: (p, 0, 0, 0, 0))]
        args += list(extra) + [out_buf]
        in_specs.append(ANY)
        out_specs = [_pair_spec(seq_q, 2 * mixer), _pair_spec(seq_q, 0)]
        out_shape = [jax.ShapeDtypeStruct(out_buf.shape, F32), lse_shape]
        aliases = {len(args) - 1: 0}
    else:
        in_specs = [_pair_spec(seq_q, 0), _pair_spec(seq_k, 0), _pair_spec(seq_k, 0)]
        args = list(src)
        out_specs = [_pair_spec(seq_q, 0), _pair_spec(seq_q, 0)]
        out_shape = [lse_shape, lse_shape]
        aliases = {}
    out, carried = _pallas(body, name=name, grid=(2, nb), in_specs=in_specs, out_specs=out_specs, out_shape=out_shape,
                           args=args, aliases=aliases, sem=("parallel", "arbitrary"), comm=comm)
    return out if comm is None else (out, carried)


def _softmax_attn_bwd(src, o, lse, do, *, nb, mode, mixer=None, dbuf=None, extra=(), name):
    packed = mode != "cross"
    n_src = 1 if packed else 3
    seq_q = (src if packed else src[0]).shape[0] // nb
    seq_k = seq_q if packed else src[1].shape[0] // nb
    tq, tk = min(ATT_TILE, seq_q), min(ATT_TILE, seq_k)
    nq, nk = seq_q // tq, seq_k // tk
    n_ex = len(extra)
    n_in = n_src + 3 + n_ex + (1 if packed else 0)

    def body(*refs):
        q_at, k_at, v_at = _qkv_readers(refs[:n_src], packed)
        o_ref, lse_ref, do_ref = refs[n_src:n_src + 3]
        ex = refs[n_src + 3:n_src + 3 + n_ex]
        outs = refs[n_in:]
        if packed:
            d_ref = outs[0]
            dq_w = lambda r0, val: d_ref.__setitem__((pl.ds(r0, tq), slice(0, LANES)), val)
            dk_ref = d_ref.at[:, LANES:2 * LANES]
            dv_ref = d_ref.at[:, 2 * LANES:3 * LANES]
        else:
            dq_ref, dk_ref, dv_ref = outs[:3]
            dq_w = lambda r0, val: dq_ref.__setitem__((pl.ds(r0, tq), slice(None)), val)
        dk_ref[...] = jnp.zeros((seq_k, LANES), F32)
        dv_ref[...] = jnp.zeros((seq_k, LANES), F32)
        if mode == "fox":
            dcum_ref, dcq_ref = outs[-2:]
            dcum_ref[...] = jnp.zeros_like(dcum_ref)
        if mode == "dil":
            dbias_ref = outs[-1]

            @pl.when(pl.program_id(1) == 0)
            def _():
                dbias_ref[...] = jnp.zeros_like(dbias_ref)

        def q_tile(i, _):
            r0 = pl.multiple_of(i * tq, tq)
            qm = _split_heads((q_at(r0, tq) * SCALE).astype(BF16))
            do_f = do_ref[pl.ds(r0, tq), :]
            dom = _split_heads(do_f.astype(BF16))
            dd = _split_heads(do_f * o_ref[pl.ds(r0, tq), :])
            delta = [jnp.sum(dd[h], axis=-1, keepdims=True) for h in range(2)]
            lse_t = lse_ref[pl.ds(r0, tq), :]
            lse_h = [lse_t[:, 0:1], lse_t[:, HEAD_DIM:HEAD_DIM + 1]]
            if mode == "fox":
                cq = ex[0][pl.ds(r0, tq), :]
                row = r0 + lax.broadcasted_iota(jnp.int32, (tq, tk), 0)

            def k_tile(j, carry, diagonal=False):
                dq, rs = carry
                c0 = pl.multiple_of(j * tk, tk)
                kt = k_at(c0, tk).astype(BF16)
                vt = v_at(c0, tk).astype(BF16)
                km = _split_heads(kt)
                if mode == "fox":
                    ck = ex[1][j]
                hs = range(2)
                s = [lax.dot_general(qm[h], kt, NT_DIMS, preferred_element_type=F32) for h in hs]
                dp = [lax.dot_general(dom[h], vt, NT_DIMS, preferred_element_type=F32) for h in hs]
                if mode == "fox":
                    s = [s[h] + cq[:, h:h + 1] - ck[h:h + 1, :] for h in hs]
                    if diagonal:
                        keep = (c0 + lax.broadcasted_iota(jnp.int32, (tq, tk), 1)) <= row
                        s = [jnp.where(keep, s[h], MASKED) for h in hs]
                elif mode == "dil":
                    s = [s[h] + ex[0][h, i - j] for h in hs]
                p = [jnp.exp(s[h] - lse_h[h]) for h in hs]
                ds = [p[h] * (dp[h] - delta[h]) for h in hs]
                dsb = [ds[h].astype(BF16) for h in hs]
                pb = [p[h].astype(BF16) for h in hs]
                dq = dq + (jnp.dot(dsb[0], km[0], preferred_element_type=F32)
                           + jnp.dot(dsb[1], km[1], preferred_element_type=F32))
                dk_t = (lax.dot_general(dsb[0], qm[0], TN_DIMS, preferred_element_type=F32)
                        + lax.dot_general(dsb[1], qm[1], TN_DIMS, preferred_element_type=F32))
                dv_t = (lax.dot_general(pb[0], dom[0], TN_DIMS, preferred_element_type=F32)
                        + lax.dot_general(pb[1], dom[1], TN_DIMS, preferred_element_type=F32))
                if mode == "fox":
                    for h in hs:
                        dcum_ref[j, h:h + 1, :] -= jnp.sum(ds[h], axis=0, keepdims=True)
                    rs = tuple(rs[h] + jnp.sum(ds[h], axis=-1, keepdims=True) for h in hs)
                elif mode == "dil":
                    for h in hs:
                        dbias_ref[h, i - j] += ds[h]
                dk_ref[pl.ds(c0, tk), :] += dk_t
                dv_ref[pl.ds(c0, tk), :] += dv_t
                return dq, rs

            zero = (jnp.zeros((tq, 1), F32),) * 2
            init = (jnp.zeros((tq, LANES), F32), zero)
            if mode == "fox":
                dq, rs = k_tile(i, lax.fori_loop(0, i, k_tile, init), True)
            else:
                dq, rs = lax.fori_loop(0, i + 1 if packed else nk, k_tile, init)
            dq_w(r0, dq * SCALE)
            if mode == "fox":
                dcq_ref[pl.ds(r0, tq), :] = jnp.where(lax.broadcasted_iota(jnp.int32, (tq, 2), 1) == 0, rs[0], rs[1])
            return 0

        lax.fori_loop(0, nq, q_tile, 0)

    if packed:
        in_specs = [_pair_spec(seq_q, 2 * mixer, PAIR_W), _pair_spec(seq_q, 2 * mixer), _pair_spec(seq_q, 0),
                    _pair_spec(seq_q, 2 * mixer)]
        args = [src, o, lse, do]
        out_specs = [_pair_spec(seq_q, 2 * mixer, PAIR_W)]
        out_shape = [jax.ShapeDtypeStruct(dbuf.shape, F32)]
        if mode == "fox":
            in_specs += _fox_specs(seq_q, nk, tk)
            out_specs += [_fox_specs(seq_q, nk, tk)[1], _fox_specs(seq_q, nk, tk)[0]]
            out_shape += [jax.ShapeDtypeStruct((nb, 2, nk, 2, tk), F32), jax.ShapeDtypeStruct((nb, 2, seq_q, 2), F32)]
        else:
            tiles = pl.BlockSpec((None, 2, nq, tq, tk), lambda p, b: (p, 0, 0, 0, 0))
            in_specs.append(tiles)
            out_specs.append(tiles)
            out_shape.append(jax.ShapeDtypeStruct((2, 2, nq, tq, tk), F32))
        args += list(extra) + [dbuf]
        in_specs.append(ANY)
        aliases = {len(args) - 1: 0}
    else:
        sq, sk = _pair_spec(seq_q, 0), _pair_spec(seq_k, 0)
        in_specs, args = [sq, sk, sk, sq, sq, sq], list(src) + [o, lse, do]
        out_specs = [sq, sk, sk]
        out_shape = [jax.ShapeDtypeStruct((nb * seq_q, 2 * LANES), F32)] + [
            jax.ShapeDtypeStruct((nb * seq_k, 2 * LANES), F32)] * 2
        aliases = {}
    return pl.pallas_call(
        body, name=name, grid=(2, nb), in_specs=in_specs, out_specs=out_specs, out_shape=out_shape,
        input_output_aliases=aliases, compiler_params=_params(("parallel", "arbitrary")))(*args)


def _log_sigmoid(z):
    return jnp.minimum(z, 0.0) - jnp.log(1.0 + jnp.exp(-jnp.abs(z)))


def _split_bf16(x):
    hi = x.astype(BF16)
    return hi, (x - hi.astype(F32)).astype(BF16)


def _tri(n, fn):
    r = lax.broadcasted_iota(jnp.int32, (n, n), 0)
    c = lax.broadcasted_iota(jnp.int32, (n, n), 1)
    return jnp.where(fn(r, c), 1.0, 0.0).astype(BF16)


def _sb_attn_fwd(proj, out_buf, *, nb, name, comm=None):
    seq = proj.shape[0] // nb
    tq = tk = min(ATT_TILE, seq)
    nq = seq // tq

    def body(qkv_ref, _, o_ref, lt_ref):
        q_at, k_at, v_at = _qkv_readers((qkv_ref,), True)
        after = _tri(tk, lambda r, c: r > c)

        def q_tile(i, _):
            r0 = pl.multiple_of(i * tq, tq)
            qm = _split_heads((q_at(r0, tq) * SCALE).astype(BF16))
            row = r0 + lax.broadcasted_iota(jnp.int32, (tq, tk), 0)

            def k_tile(j, carry, diagonal):
                c, acc = carry
                c0 = pl.multiple_of(j * tk, tk)
                kt = k_at(c0, tk).astype(BF16)
                vm = _split_heads(v_at(c0, tk).astype(BF16))
                if diagonal:
                    strict = (c0 + lax.broadcasted_iota(jnp.int32, (tq, tk), 1)) < row
                hs = range(2)
                z = [lax.dot_general(qm[h], kt, NT_DIMS, preferred_element_type=F32) for h in hs]
                ls = [_log_sigmoid(z[h]) for h in hs]
                lk = [ls[h] - z[h] for h in hs]
                if diagonal:
                    lk = [jnp.where(strict, lk[h], 0.0) for h in hs]
                parts = [_split_bf16(lk[h]) for h in hs]
                sfx = [jnp.dot(parts[h][0], after, preferred_element_type=F32)
                       + jnp.dot(parts[h][1], after, preferred_element_type=F32) for h in hs]
                att = [jnp.exp(ls[h] + sfx[h] + c[h]) for h in hs]
                if diagonal:
                    att = [jnp.where(strict, att[h], 0.0) for h in hs]
                acc = acc + (jnp.dot(att[0].astype(BF16), vm[0], preferred_element_type=F32)
                             + jnp.dot(att[1].astype(BF16), vm[1], preferred_element_type=F32))
                return tuple(c[h] + jnp.sum(lk[h], axis=-1, keepdims=True) for h in hs), acc

            init = ((jnp.zeros((tq, 1), F32),) * 2, jnp.zeros((tq, LANES), F32))
            c, acc = lax.fori_loop(1, i + 1, lambda jj, cr: k_tile(i - jj, cr, False), k_tile(i, init, True))
            o_ref[pl.ds(r0, tq), :] = acc
            lt_ref[pl.ds(r0, tq), :] = _lane_pair(c[0], c[1], tq)
            return 0

        lax.fori_loop(0, nq, q_tile, 0)

    out, carried = _pallas(
        body, name=name, grid=(2, nb), in_specs=[_pair_spec(seq, 2 * MIX_SB, PAIR_W), ANY],
        out_specs=[_pair_spec(seq, 2 * MIX_SB), _pair_spec(seq, 0)],
        out_shape=[jax.ShapeDtypeStruct(out_buf.shape, F32), jax.ShapeDtypeStruct((nb * seq, 2 * LANES), F32)],
        args=[proj, out_buf], aliases={1: 0}, sem=("parallel", "arbitrary"), comm=comm)
    return out if comm is None else (out, carried)


def _sb_attn_bwd(proj, ltot, do, dbuf, *, nb, name, comm=None):
    seq = proj.shape[0] // nb
    tq = tk = min(ATT_TILE, seq)
    nq = seq // tq

    def body(qkv_ref, lt_ref, do_ref, _, d_ref):
        q_at, k_at, v_at = _qkv_readers((qkv_ref,), True)
        upto = _tri(tk, lambda r, c: r <= c)
        before = _tri(tk, lambda r, c: r < c)
        dk_ref = d_ref.at[:, LANES:2 * LANES]
        dv_ref = d_ref.at[:, 2 * LANES:3 * LANES]
        dk_ref[...] = jnp.zeros((seq, LANES), F32)
        dv_ref[...] = jnp.zeros((seq, LANES), F32)

        def q_tile(i, _):
            r0 = pl.multiple_of(i * tq, tq)
            qm = _split_heads((q_at(r0, tq) * SCALE).astype(BF16))
            dom = _split_heads(do_ref[pl.ds(r0, tq), :].astype(BF16))
            lt_t = lt_ref[pl.ds(r0, tq), :]
            lt_h = [lt_t[:, 0:1], lt_t[:, HEAD_DIM:HEAD_DIM + 1]]
            row = r0 + lax.broadcasted_iota(jnp.int32, (tq, tk), 0)

            def k_tile(j, carry, diagonal):
                pc, qc, dq = carry
                c0 = pl.multiple_of(j * tk, tk)
                kt = k_at(c0, tk).astype(BF16)
                vt = v_at(c0, tk).astype(BF16)
                km = _split_heads(kt)
                if diagonal:
                    strict = (c0 + lax.broadcasted_iota(jnp.int32, (tq, tk), 1)) < row
                hs = range(2)
                z = [lax.dot_general(qm[h], kt, NT_DIMS, preferred_element_type=F32) for h in hs]
                da = [lax.dot_general(dom[h], vt, NT_DIMS, preferred_element_type=F32) for h in hs]
                ls = [_log_sigmoid(z[h]) for h in hs]
                lk = [ls[h] - z[h] for h in hs]
                if diagonal:
                    lk = [jnp.where(strict, lk[h], 0.0) for h in hs]
                parts = [_split_bf16(lk[h]) for h in hs]
                pin = [jnp.dot(parts[h][0], upto, preferred_element_type=F32)
                       + jnp.dot(parts[h][1], upto, preferred_element_type=F32) for h in hs]
                att = [jnp.exp(ls[h] + (lt_h[h] - pc[h] - pin[h])) for h in hs]
                if diagonal:
                    att = [jnp.where(strict, att[h], 0.0) for h in hs]
                dg = [att[h] * da[h] for h in hs]
                qx = [qc[h] + jnp.dot(dg[h].astype(BF16), before, preferred_element_type=F32) for h in hs]
                sig = [jnp.exp(ls[h]) for h in hs]
                dz = [dg[h] * (1.0 - sig[h]) - sig[h] * qx[h] for h in hs]
                if diagonal:
                    dz = [jnp.where(strict, dz[h], 0.0) for h in hs]
                dzb = [dz[h].astype(BF16) for h in hs]
                attb = [att[h].astype(BF16) for h in hs]
                dq = dq + (jnp.dot(dzb[0], km[0], preferred_element_type=F32)
                           + jnp.dot(dzb[1], km[1], preferred_element_type=F32))
                dk_ref[pl.ds(c0, tk), :] += (lax.dot_general(dzb[0], qm[0], TN_DIMS, preferred_element_type=F32)
                                             + lax.dot_general(dzb[1], qm[1], TN_DIMS, preferred_element_type=F32))
                dv_ref[pl.ds(c0, tk), :] += (lax.dot_general(attb[0], dom[0], TN_DIMS, preferred_element_type=F32)
                                             + lax.dot_general(attb[1], dom[1], TN_DIMS, preferred_element_type=F32))
                return (tuple(pc[h] + jnp.sum(lk[h], axis=-1, keepdims=True) for h in hs),
                        tuple(qc[h] + jnp.sum(dg[h], axis=-1, keepdims=True) for h in hs), dq)

            zero = (jnp.zeros((tq, 1), F32),) * 2
            carry = lax.fori_loop(0, i, lambda j, cr: k_tile(j, cr, False), (zero, zero, jnp.zeros((tq, LANES), F32)))
            _, _, dq = k_tile(i, carry, True)
            d_ref[pl.ds(r0, tq), 0:LANES] = dq * SCALE
            return 0

        lax.fori_loop(0, nq, q_tile, 0)

    out, carried = _pallas(
        body, name=name, grid=(2, nb),
        in_specs=[_pair_spec(seq, 2 * MIX_SB, PAIR_W), _pair_spec(seq, 0), _pair_spec(seq, 2 * MIX_SB), ANY],
        out_specs=[_pair_spec(seq, 2 * MIX_SB, PAIR_W)], out_shape=[jax.ShapeDtypeStruct(dbuf.shape, F32)],
        args=[proj, ltot, do, dbuf], aliases={3: 0}, sem=("parallel", "arbitrary"), comm=comm)
    return out[0] if comm is None else (out[0], carried)


def _lane_scan(x, reverse=False):
    n = x.shape[-1]
    lane = lax.broadcasted_iota(jnp.int32, x.shape, 1)
    k = 1
    while k < n:
        if reverse:
            x = x + jnp.where(lane < n - k, pltpu.roll(x, n - k, 1), 0.0)
        else:
            x = x + jnp.where(lane >= k, pltpu.roll(x, k, 1), 0.0)
        k *= 2
    return x


def _fox_gate_fwd(f_rows, b_rows):
    def body(f_ref, b_ref, o_ref):
        o_ref[...] = _lane_scan(_log_sigmoid(f_ref[...] + b_ref[...]))

    return pl.pallas_call(body, name="fox_gate_fwd", out_shape=jax.ShapeDtypeStruct(f_rows.shape, F32))(f_rows, b_rows)


def _fox_gate_bwd(dcum, f_rows, b_rows):
    def body(d_ref, f_ref, b_ref, df_ref, db_ref):
        z = f_ref[...] + b_ref[...]
        df = _lane_scan(d_ref[...], reverse=True) * jnp.exp(_log_sigmoid(-z))
        df_ref[...] = df
        rs = jnp.sum(df, axis=-1, keepdims=True)
        tot = rs
        for e in range(1, f_rows.shape[0] // N_HEADS):
            tot = tot + pltpu.roll(rs, e * N_HEADS, 0)
        db_ref[...] = tot

    return pl.pallas_call(
        body, name="fox_gate_bwd",
        out_shape=[jax.ShapeDtypeStruct(f_rows.shape, F32), jax.ShapeDtypeStruct((f_rows.shape[0], 1), F32)],
    )(dcum, f_rows, b_rows)


def _dil_tables(seq):
    t = min(ATT_TILE, seq)
    n = seq // t
    a = np.arange(t)
    d = (np.arange(n)[:, None, None] * t + a[None, :, None] - a[None, None, :]).astype(np.int64)
    count = np.zeros(d.shape, np.int64)
    for window, dil in DILATED_PATTERNS:
        count += (d >= 0) & (d % dil == 0) & (d // dil <= window // dil)
    nn = np.maximum(d, 0)
    max_exact = NUM_BUCKETS // 2
    nf = np.maximum(nn, 1).astype(np.float32)
    large = max_exact + (np.log(nf / np.float32(max_exact)) / np.float32(math.log(MAX_DISTANCE / max_exact))
                         * np.float32(NUM_BUCKETS - max_exact)).astype(np.int32)
    bucket = np.where(nn < max_exact, nn, np.minimum(large, NUM_BUCKETS - 1))
    bucket = np.where(count > 0, bucket, -1).astype(np.int32)
    logc = np.where(count > 0, np.log(np.maximum(count, 1)), MASKED).astype(np.float32)
    return bucket, logc


def _dil_bias(rel_bias, seq):
    bucket, logc = _dil_tables(seq)
    n, t, _ = bucket.shape

    def body(rb_ref, bk_ref, lc_ref, o_ref):
        h = pl.program_id(0) * 2 + pl.program_id(1)
        bk = bk_ref[...]
        out = lc_ref[...]
        for b in range(NUM_BUCKETS):
            out = jnp.where(bk == b, out + rb_ref[b, h], out)
        o_ref[...] = out

    full = pl.BlockSpec((n, t, t), lambda p, h: (0, 0, 0))
    return pl.pallas_call(
        body, name="dil_bias", grid=(2, 2),
        in_specs=[pl.BlockSpec(memory_space=pltpu.SMEM), full, full],
        out_specs=pl.BlockSpec((None, None, n, t, t), lambda p, h: (p, h, 0, 0, 0)),
        out_shape=jax.ShapeDtypeStruct((2, 2, n, t, t), F32),
        compiler_params=_params(("parallel", "parallel")))(rel_bias, jnp.asarray(bucket), jnp.asarray(logc))


def _dil_bias_bwd(dbias, seq):
    bucket, _ = _dil_tables(seq)
    n, t, _ = bucket.shape

    def body(d_ref, bk_ref, o_ref):
        bk = bk_ref[...]
        lane = lax.broadcasted_iota(jnp.int32, (1, LANES), 1)
        for b in range(NUM_BUCKETS):
            rowv = jnp.zeros((1, LANES), F32)
            for h in range(N_HEADS):
                s = jnp.sum(jnp.where(bk == b, d_ref[h // 2, h % 2], 0.0))
                rowv = jnp.where(lane == h, s, rowv)
            o_ref[b:b + 1, :] = rowv

    out = pl.pallas_call(body, name="dil_bias_bwd", out_shape=jax.ShapeDtypeStruct((NUM_BUCKETS, LANES), F32),
                         compiler_params=pltpu.CompilerParams(vmem_limit_bytes=VMEM_LIMIT))(dbias, jnp.asarray(bucket))
    return out[:, :N_HEADS]


def _shift_rows(x, k, row, fill=0.0):
    n = x.shape[0]
    if k > 0:
        return jnp.where(row >= k, pltpu.roll(x, k, 0), fill)
    return jnp.where(row < n + k, pltpu.roll(x, n + k, 0), fill)


def _row_scan(a, u, row, reverse=False):
    n = a.shape[0]
    k = 1
    while k < n:
        s = -k if reverse else k
        u = a * _shift_rows(u, s, row) + u
        a = a * _shift_rows(a, s, row, 1.0)
        k *= 2
    return u


def _sigmoid(x):
    return 1.0 / (1.0 + jnp.exp(-x))


def _gelu(g):
    return 0.5 * g * (1.0 + lax.erf(g * (2.0 ** -0.5)))


def _gelu_grad(g):
    return 0.5 * (1.0 + lax.erf(g * (2.0 ** -0.5))) + g * jnp.exp(-0.5 * g * g) * (1.0 / math.sqrt(2.0 * math.pi))


def _neg_expm1(x):
    small = -x * (1.0 + x * (0.5 + x * (1.0 / 6.0 + x * (1.0 / 24.0))))
    return jnp.where(x > -0.03, small, 1.0 - jnp.exp(x))


def _lru_core(x, vec, wa, wx, row):
    xs = [_shift_rows(x, 3 - j, row) if j < 3 else x for j in range(4)]
    xc = vec[4:5, :]
    for j in range(4):
        xc = xc + vec[j:j + 1, :] * xs[j]
    xcb = xc.astype(BF16)
    r = _sigmoid(jnp.dot(xcb, wa, preferred_element_type=F32) + vec[5:6, :])
    ig = _sigmoid(jnp.dot(xcb, wx, preferred_element_type=F32) + vec[6:7, :])
    lam = vec[7:8, :]
    sp = jnp.maximum(-lam, 0.0) - _log_sigmoid(jnp.abs(lam))
    la = -LRU_C * r * sp
    a = jnp.exp(la)
    mult = jnp.sqrt(_neg_expm1(2.0 * la))
    return xs, xc, xcb, r, ig, sp, la, a, mult


def _lru_specs(seq):
    xg = pl.BlockSpec((seq, LRU_W), lambda hf, b: (b, COL_LRU // LRU_W + hf))
    mix = pl.BlockSpec((seq, LANES), lambda hf, b: (b, 2 * MIX_LRU + hf))
    vec = pl.BlockSpec((SUBLANES, LANES), lambda hf, b: (0, hf))
    mat = pl.BlockSpec((None, LANES, LANES), lambda hf, b: (hf, 0, 0))
    return xg, mix, vec, mat


def _lru_fwd(proj, vec, wa, wx, out_buf, *, nb, name):
    seq = proj.shape[0] // nb

    def body(xg_ref, vec_ref, wa_ref, wx_ref, _, o_ref):
        row = lax.broadcasted_iota(jnp.int32, (seq, LANES), 0)
        _, xc, _, _, ig, _, _, a, mult = _lru_core(xg_ref[:, 0:LANES], vec_ref[...], wa_ref[...], wx_ref[...], row)
        h = _row_scan(a, mult * (ig * xc), row)
        o_ref[...] = h * _gelu(xg_ref[:, LANES:LRU_W])

    xg, mix, vecs, mat = _lru_specs(seq)
    return pl.pallas_call(
        body, name=name, grid=(2, nb), in_specs=[xg, vecs, mat, mat, ANY], out_specs=mix,
        out_shape=jax.ShapeDtypeStruct(out_buf.shape, F32), input_output_aliases={4: 0},
        compiler_params=_params(("parallel", "arbitrary")))(proj, vec, wa, wx, out_buf)


def _lru_bwd(proj, vec, wa, wx, dout, dbuf, *, nb, name):
    seq = proj.shape[0] // nb

    def body(xg_ref, vec_ref, wa_ref, wx_ref, do_ref, _, d_ref, dvec_ref, dwa_ref, dwx_ref):
        row = lax.broadcasted_iota(jnp.int32, (seq, LANES), 0)
        vec_, wa_, wx_ = vec_ref[...], wa_ref[...], wx_ref[...]
        xs, xc, xcb, r, ig, sp, la, a, mult = _lru_core(xg_ref[:, 0:LANES], vec_, wa_, wx_, row)
        h = _row_scan(a, mult * (ig * xc), row)
        gate, do = xg_ref[:, LANES:LRU_W], do_ref[...]
        d_ref[:, LANES:LRU_W] = do * h * _gelu_grad(gate)
        dh = do * _gelu(gate)
        gacc = _row_scan(_shift_rows(a, -1, row), dh, row, reverse=True)
        da = gacc * _shift_rows(h, 1, row)
        dmult = gacc * (ig * xc)
        dig = gacc * (mult * xc)
        dxc = gacc * (mult * ig)
        dla = da * a - dmult * (a * a) / mult
        dr = (-LRU_C) * sp * dla
        dsp = jnp.sum((-LRU_C) * r * dla, axis=0, keepdims=True)
        dpr = dr * r * (1.0 - r)
        dpi = dig * ig * (1.0 - ig)
        dprb, dpib = dpr.astype(BF16), dpi.astype(BF16)
        dxc = (dxc + lax.dot_general(dprb, wa_, NT_DIMS, preferred_element_type=F32)
               + lax.dot_general(dpib, wx_, NT_DIMS, preferred_element_type=F32))
        dx = vec_[3:4, :] * dxc
        for j in range(3):
            dx = dx + vec_[j:j + 1, :] * _shift_rows(dxc, -(3 - j), row)
        d_ref[:, 0:LANES] = dx

        @pl.when(pl.program_id(1) == 0)
        def _():
            dvec_ref[...] = jnp.zeros_like(dvec_ref)
            dwa_ref[...] = jnp.zeros_like(dwa_ref)
            dwx_ref[...] = jnp.zeros_like(dwx_ref)

        for j in range(4):
            dvec_ref[j:j + 1, :] += jnp.sum(dxc * xs[j], axis=0, keepdims=True)
        dvec_ref[4:5, :] += jnp.sum(dxc, axis=0, keepdims=True)
        dvec_ref[5:6, :] += jnp.sum(dpr, axis=0, keepdims=True)
        dvec_ref[6:7, :] += jnp.sum(dpi, axis=0, keepdims=True)
        lam = vec_[7:8, :]
        dvec_ref[7:8, :] += -dsp * _sigmoid(-lam)
        dwa_ref[...] += lax.dot_general(xcb, dprb, TN_DIMS, preferred_element_type=F32)
        dwx_ref[...] += lax.dot_general(xcb, dpib, TN_DIMS, preferred_element_type=F32)

    xg, mix, vecs, mat = _lru_specs(seq)
    return pl.pallas_call(
        body, name=name, grid=(2, nb), in_specs=[xg, vecs, mat, mat, mix, ANY], out_specs=[xg, vecs, mat, mat],
        out_shape=[jax.ShapeDtypeStruct(dbuf.shape, F32), jax.ShapeDtypeStruct((SUBLANES, 2 * LANES), F32),
                   jax.ShapeDtypeStruct((2, LANES, LANES), F32), jax.ShapeDtypeStruct((2, LANES, LANES), F32)],
        input_output_aliases={5: 0},
        compiler_params=_params(("parallel", "arbitrary")))(proj, vec, wa, wx, dout, dbuf)


FFN_ROWS = 256
FFN_COLS = 1408


def _with_halo(halo, x, k):
    xx = jnp.concatenate([halo, x], axis=0)
    return pltpu.roll(xx, k, 0)[SUBLANES:, :]


def _ffn_conv(x_ref, halo_ref, cw, pos):
    x, halo = x_ref[...], halo_ref[...]
    x1 = jnp.where(pos >= 1, _with_halo(halo, x, 1), 0.0)
    x2 = jnp.where(pos >= 2, _with_halo(halo, x, 2), 0.0)
    return cw[3:4, :] + cw[0:1, :] * x2 + cw[1:2, :] * x1 + cw[2:3, :] * x, x1, x2


def _ffn_specs(tm, tn, gate_off):
    prev = lambda i: jnp.maximum(i * (tm // SUBLANES) - 1, 0)
    up = pl.BlockSpec((tm, tn), lambda j, i: (i, j))
    gate = pl.BlockSpec((tm, tn), lambda j, i: (i, j + gate_off))
    up_h = pl.BlockSpec((SUBLANES, tn), lambda j, i: (prev(i), j))
    gate_h = pl.BlockSpec((SUBLANES, tn), lambda j, i: (prev(i), j + gate_off))
    cw_up = pl.BlockSpec((SUBLANES, tn), lambda j, i: (0, j))
    cw_gate = pl.BlockSpec((SUBLANES, tn), lambda j, i: (0, j + gate_off))
    return up, gate, up_h, gate_h, cw_up, cw_gate


def _ffn_act(hf, cw, *, seq, name):
    t, w2 = hf.shape
    w = w2 // 2
    tm, tn = _tile(seq, FFN_ROWS, SUBLANES), _tile(w, FFN_COLS)

    def body(u_ref, g_ref, uh_ref, gh_ref, cu_ref, cg_ref, o_ref):
        pos = (pl.program_id(1) * tm + lax.broadcasted_iota(jnp.int32, (tm, 1), 0)) % seq
        up, _, _ = _ffn_conv(u_ref, uh_ref, cu_ref[...], pos)
        gate, _, _ = _ffn_conv(g_ref, gh_ref, cg_ref[...], pos)
        o_ref[...] = (_gelu(gate) * up).astype(BF16)

    specs = _ffn_specs(tm, tn, w // tn)
    return pl.pallas_call(
        body, name=name, grid=(w // tn, t // tm), in_specs=list(specs), out_specs=specs[0],
        out_shape=jax.ShapeDtypeStruct((t, w), BF16),
        compiler_params=_params(("parallel", "parallel")))(hf, hf, hf, hf, cw, cw)


def _ffn_bwd(hf, cw, dact, *, seq, name, comm=None):
    t, w2 = hf.shape
    w = w2 // 2
    tm, tn = _tile(seq, FFN_ROWS, 2 * SUBLANES), _tile(w, FFN_COLS)
    ext = tm + SUBLANES
    last = t // SUBLANES - 1

    def body(u_ref, g_ref, uh_ref, gh_ref, cu_ref, cg_ref, un_ref, gn_ref, da_ref, dn_ref, d_ref, dcu_ref, dcg_ref):
        pos = (pl.program_id(1) * tm + lax.broadcasted_iota(jnp.int32, (ext, 1), 0)) % seq

        def conv(x_ref, prev_ref, next_ref, cwv):
            xx = jnp.concatenate([prev_ref[...], x_ref[...], next_ref[...]], axis=0)
            x1 = jnp.where(pos >= 1, pltpu.roll(xx, 1, 0)[SUBLANES:, :], 0.0)
            x2 = jnp.where(pos >= 2, pltpu.roll(xx, 2, 0)[SUBLANES:, :], 0.0)
            x0 = xx[SUBLANES:, :]
            return cwv[3:4, :] + cwv[0:1, :] * x2 + cwv[1:2, :] * x1 + cwv[2:3, :] * x0, (x2, x1, x0)

        def back(d, cwv):
            d1 = jnp.where(pos < seq - 1, pltpu.roll(d, ext - 1, 0), 0.0)
            d2 = jnp.where(pos < seq - 2, pltpu.roll(d, ext - 2, 0), 0.0)
            return (cwv[2:3, :] * d + cwv[1:2, :] * d1 + cwv[0:1, :] * d2)[:tm, :].astype(BF16)

        cu, cg = cu_ref[...], cg_ref[...]
        up, u_taps = conv(u_ref, uh_ref, un_ref, cu)
        gate, g_taps = conv(g_ref, gh_ref, gn_ref, cg)
        da = jnp.concatenate([da_ref[...], dn_ref[...]], axis=0)
        cdf = 0.5 * (1.0 + lax.erf(gate * (2.0 ** -0.5)))
        d_up = da * (gate * cdf)
        d_gate = da * up * (cdf + gate * jnp.exp(-0.5 * gate * gate) * (1.0 / math.sqrt(2.0 * math.pi)))
        d_ref[0] = back(d_up, cu)
        d_ref[1] = back(d_gate, cg)

        @pl.when(pl.program_id(1) == 0)
        def _():
            dcu_ref[...] = jnp.zeros_like(dcu_ref)
            dcg_ref[...] = jnp.zeros_like(dcg_ref)

        for ref, d, taps in ((dcu_ref, d_up, u_taps), (dcg_ref, d_gate, g_taps)):
            own = d[:tm, :]
            for j in range(3):
                ref[j:j + 1, :] += jnp.sum(own * taps[j][:tm, :], axis=0, keepdims=True)
            ref[3:4, :] += jnp.sum(own, axis=0, keepdims=True)

    gate_off = w // tn
    specs = _ffn_specs(tm, tn, gate_off)
    tile, cwt = specs[0], specs[4]
    nxt = lambda i: jnp.minimum((i + 1) * (tm // SUBLANES), last)
    up_n = pl.BlockSpec((SUBLANES, tn), lambda j, i: (nxt(i), j))
    gate_n = pl.BlockSpec((SUBLANES, tn), lambda j, i: (nxt(i), j + gate_off))
    out, carried = _pallas(
        body, name=name, grid=(w // tn, t // tm), in_specs=list(specs) + [up_n, gate_n, tile, up_n],
        out_specs=[pl.BlockSpec((2, tm, tn), lambda j, i: (0, i, j)), cwt, cwt],
        out_shape=[jax.ShapeDtypeStruct((2, t, w), BF16), jax.ShapeDtypeStruct((SUBLANES, w), F32),
                   jax.ShapeDtypeStruct((SUBLANES, w), F32)],
        args=[hf, hf, hf, hf, cw, cw, hf, hf, dact, dact], sem=("parallel", "arbitrary"), comm=comm)
    return out if comm is None else (out, carried)


def _adamw(w, g, m, v, *, name, rows=256):
    nl, r, c = w.shape
    tr = _tile(r, rows, SUBLANES)

    def body(w_ref, g_ref, m_ref, v_ref, d_ref, nm_ref, nv_ref):
        g_ = g_ref[...]
        nm = ADAM_B1 * m_ref[...] + (1.0 - ADAM_B1) * g_
        nv = ADAM_B2 * v_ref[...] + (1.0 - ADAM_B2) * (g_ * g_)
        m_hat = nm / (1.0 - ADAM_B1 ** ADAM_STEP)
        v_hat = nv / (1.0 - ADAM_B2 ** ADAM_STEP)
        d_ref[...] = -ADAM_LR * (m_hat / (jnp.sqrt(v_hat) + ADAM_EPS) + ADAM_WD * w_ref[...])
        nm_ref[...] = nm
        nv_ref[...] = nv

    spec = pl.BlockSpec((None, tr, c), lambda l, i: (l, i, 0))
    shape = jax.ShapeDtypeStruct((nl, r, c), F32)
    return pl.pallas_call(body, name=name, grid=(nl, r // tr), in_specs=[spec] * 4, out_specs=[spec] * 3,
                          out_shape=[shape] * 3, compiler_params=_params(("parallel", "parallel")))(w, g, m, v)


def _mesh_pos():
    return lax.axis_index("x"), lax.axis_index("y"), lax.axis_index("c")


def _peers(x, y):
    chips = [(1 - x, y), (x, 1 - y), (1 - x, 1 - y)]
    return [(px, py, 2 * px + py) for px, py in chips]


def _remote(src, dst, send_sems, recv_sems, idx, to):
    return pltpu.make_async_remote_copy(src, dst, send_sems.at[idx], recv_sems.at[idx], device_id=to,
                                        device_id_type=MESH)


class _Comm:
    def __init__(self, operands, out_shape, aliases, sems, copies):
        self.operands, self.out_shape, self.aliases, self.sems, self.copies = operands, out_shape, aliases, sems, copies

    def start(self, ins, outs, sems):
        for send, _ in self.copies(ins, outs, sems):
            send.start()

    def wait(self, ins, outs, sems):
        pairs = self.copies(ins, outs, sems)
        for _, recv in pairs:
            recv.wait_recv()
        for send, _ in pairs:
            send.wait_send()


def _pallas(body, *, name, grid, in_specs, out_specs, out_shape, args, aliases=None, scratch=(), sem, comm=None):
    n_in, n_out = len(in_specs), len(out_specs)
    aliases = dict(aliases or {})
    if comm is None:
        out = pl.pallas_call(body, name=name, grid=grid, in_specs=in_specs, out_specs=out_specs, out_shape=out_shape,
                             input_output_aliases=aliases, scratch_shapes=list(scratch),
                             compiler_params=_params(sem))(*args)
        return list(out), []
    nci, nco, ncs = len(comm.operands), len(comm.out_shape), len(comm.sems)

    def carried(*refs):
        ins, cin = refs[:n_in], refs[n_in:n_in + nci]
        o0 = n_in + nci
        outs, cout = refs[o0:o0 + n_out], refs[o0 + n_out:o0 + n_out + nco]
        s0 = o0 + n_out + nco
        own, csem = refs[s0:len(refs) - ncs], refs[len(refs) - ncs:]
        ids = [pl.program_id(ax) for ax in range(len(grid))]
        first, last = ids[0] == 0, ids[0] == grid[0] - 1
        for i, g in zip(ids[1:], grid[1:]):
            first, last = jnp.logical_and(first, i == 0), jnp.logical_and(last, i == g - 1)

        @pl.when(first)
        def _():
            comm.start(cin, cout, csem)

        body(*ins, *outs, *own)

        @pl.when(last)
        def _():
            comm.wait(cin, cout, csem)

    aliases.update({n_in + i: n_out + j for i, j in comm.aliases.items()})
    out = pl.pallas_call(
        carried, name=name, grid=grid, in_specs=list(in_specs) + [ANY] * nci, out_specs=list(out_specs) + [ANY] * nco,
        out_shape=list(out_shape) + list(comm.out_shape), input_output_aliases=aliases,
        scratch_shapes=list(scratch) + list(comm.sems),
        compiler_params=_params(("arbitrary",) * len(grid)))(*args, *comm.operands)
    return list(out[:n_out]), list(out[n_out:])


def _run_comm(comm, *, name):
    nci, nco = len(comm.operands), len(comm.out_shape)

    def body(*refs):
        ins, outs, sems = refs[:nci], refs[nci:nci + nco], refs[nci + nco:]
        comm.start(ins, outs, sems)
        comm.wait(ins, outs, sems)

    return pl.pallas_call(body, name=name, in_specs=[ANY] * nci, out_specs=[ANY] * nco, out_shape=list(comm.out_shape),
                          input_output_aliases=dict(comm.aliases), scratch_shapes=list(comm.sems))(*comm.operands)


def _pair_sems(*shape):
    return [pltpu.SemaphoreType.DMA(shape), pltpu.SemaphoreType.DMA(shape)]


def _gather_comm(bufs, layer, stage):
    n = len(bufs)

    def copies(ins, outs, sems):
        x, y, c = _mesh_pos()
        me = 2 * x + y
        pairs = []
        for i in range(n):
            h = bufs[i].shape[2] // 2
            mine, other = pl.ds(c * h, h), pl.ds((1 - c) * h, h)
            for r, (px, py, k) in enumerate(_peers(x, y)):
                if stage == 0:
                    send = _remote(ins[i].at[me, layer, mine, :], outs[i].at[me, layer, mine, :], *sems, (i, r), (px, py, c))
                    land = outs[i].at[k, layer, mine, :]
                    recv = _remote(land, land, *sems, (i, r), (px, py, c))
                else:
                    send = _remote(ins[i].at[k, layer, mine, :], outs[i].at[k, layer, mine, :], *sems, (i, r), (x, y, 1 - c))
                    land = outs[i].at[k, layer, other, :]
                    recv = _remote(land, land, *sems, (i, r), (x, y, 1 - c))
                pairs.append((send, recv))
        return pairs

    return _Comm(bufs, [jax.ShapeDtypeStruct(b.shape, b.dtype) for b in bufs], {i: i for i in range(n)},
                 _pair_sems(n, 3), copies)


def _reduce_sibling_comm(gs):
    n = len(gs)

    def copies(ins, outs, sems):
        x, y, c = _mesh_pos()
        pairs = []
        for i in range(n):
            h = gs[i].shape[1] // 2
            cp = _remote(ins[i].at[:, pl.ds((1 - c) * h, h), :], outs[i], *sems, i, (x, y, 1 - c))
            pairs.append((cp, cp))
        return pairs

    return _Comm(gs, [jax.ShapeDtypeStruct((g.shape[0], g.shape[1] // 2, g.shape[2]), g.dtype) for g in gs], {},
                 _pair_sems(n), copies)


def _reduce_chips_comm(ps):
    n = len(ps)

    def copies(ins, outs, sems):
        x, y, c = _mesh_pos()
        pairs = []
        for i in range(n):
            for r, (px, py, k) in enumerate(_peers(x, y)):
                cp = _remote(ins[i].at[k], outs[i].at[r], *sems, (i, r), (px, py, c))
                pairs.append((cp, cp))
        return pairs

    return _Comm(ps, [jax.ShapeDtypeStruct((3,) + p.shape[1:], p.dtype) for p in ps], {}, _pair_sems(n, 3), copies)


def _share_halves(bufs, *, name):
    n = len(bufs)

    def body(*refs):
        ins, outs = refs[:n], refs[n:2 * n]
        send_sems, recv_sems = refs[2 * n:]
        x, y, c = _mesh_pos()
        cps = []
        for i in range(n):
            h = bufs[i].shape[1] // 2
            mine = pl.ds(c * h, h)
            cp = _remote(ins[i].at[:, mine, :], outs[i].at[:, mine, :], send_sems, recv_sems, i, (x, y, 1 - c))
            cp.start()
            cps.append(cp)
        for cp in cps:
            cp.wait()

    return pl.pallas_call(
        body, name=name, in_specs=[ANY] * n, out_specs=[ANY] * n,
        out_shape=[jax.ShapeDtypeStruct(b.shape, b.dtype) for b in bufs],
        input_output_aliases={i: i for i in range(n)},
        scratch_shapes=[pltpu.SemaphoreType.DMA((n,)), pltpu.SemaphoreType.DMA((n,))])(*bufs)


def _add_own_half(full, recv, pos, *, name, rows=256):
    k4, h, n = recv.shape
    tr = _tile(h, rows, 16)
    nblk = h // tr

    def body(pos_ref, a_ref, b_ref, o_ref):
        o_ref[...] = (a_ref[...] + b_ref[...]).astype(BF16)

    grid_spec = pltpu.PrefetchScalarGridSpec(
        num_scalar_prefetch=1, grid=(k4, nblk),
        in_specs=[pl.BlockSpec((None, tr, n), lambda k, i, pos_ref: (k, pos_ref[1] * nblk + i, 0)),
                  pl.BlockSpec((None, tr, n), lambda k, i, pos_ref: (k, i, 0))],
        out_specs=pl.BlockSpec((None, tr, n), lambda k, i, pos_ref: (k, i, 0)))
    return pl.pallas_call(body, name=name, grid_spec=grid_spec, out_shape=jax.ShapeDtypeStruct(recv.shape, BF16),
                          compiler_params=_params(("parallel", "parallel")))(pos, full, recv)


def _sum_into(own, others, buf, pos, layer, *, name, rows=256):
    _, h, n = own.shape
    tr = _tile(h, rows, 16)
    nblk = h // tr

    def body(pos_ref, own_ref, oth_ref, _, o_ref):
        acc = own_ref[...].astype(F32)
        for r in range(3):
            acc = acc + oth_ref[r].astype(F32)
        o_ref[...] = acc

    grid_spec = pltpu.PrefetchScalarGridSpec(
        num_scalar_prefetch=1, grid=(nblk,),
        in_specs=[pl.BlockSpec((None, tr, n), lambda i, pos_ref: (pos_ref[0], i, 0)),
                  pl.BlockSpec((3, tr, n), lambda i, pos_ref: (0, i, 0)), ANY],
        out_specs=pl.BlockSpec((None, tr, n), lambda i, pos_ref: (layer, pos_ref[1] * nblk + i, 0)))
    return pl.pallas_call(body, name=name, grid_spec=grid_spec, out_shape=jax.ShapeDtypeStruct(buf.shape, F32),
                          input_output_aliases={3: 0}, compiler_params=_params(("parallel",)))(pos, own, others, buf)


def _sibling_pair(buf, *, name):
    def body(src_ref, out_ref, send_sem, recv_sem, local_sem):
        x, y, c = _mesh_pos()
        local = pltpu.make_async_copy(src_ref, out_ref.at[c], local_sem)
        local.start()
        cp = pltpu.make_async_remote_copy(src_ref, out_ref.at[c], send_sem, recv_sem, device_id=(x, y, 1 - c),
                                          device_id_type=MESH)
        cp.start()
        cp.wait()
        local.wait()

    return pl.pallas_call(
        body, name=name, in_specs=[ANY], out_specs=ANY, out_shape=jax.ShapeDtypeStruct((2,) + buf.shape, buf.dtype),
        scratch_shapes=[pltpu.SemaphoreType.DMA, pltpu.SemaphoreType.DMA, pltpu.SemaphoreType.DMA])(buf)


def _chip_bcast(buf, *, name):
    def body(src_ref, out_ref, send_sems, recv_sems, local_sem):
        x, y, c = _mesh_pos()
        me = 2 * x + y
        local = pltpu.make_async_copy(src_ref, out_ref.at[me], local_sem)
        local.start()
        sends = []
        for r, (px, py, _) in enumerate(_peers(x, y)):
            cp = _remote(src_ref, out_ref.at[me], send_sems, recv_sems, r, (px, py, c))
            cp.start()
            sends.append(cp)
        for r, (px, py, k) in enumerate(_peers(x, y)):
            _remote(src_ref, out_ref.at[k], send_sems, recv_sems, r, (px, py, c)).wait_recv()
        for cp in sends:
            cp.wait_send()
        local.wait()

    return pl.pallas_call(
        body, name=name, in_specs=[ANY], out_specs=ANY, out_shape=jax.ShapeDtypeStruct((4,) + buf.shape, buf.dtype),
        scratch_shapes=[pltpu.SemaphoreType.DMA((3,)), pltpu.SemaphoreType.DMA((3,)), pltpu.SemaphoreType.DMA])(buf)


def _sum_slots(buf, *, name, rows=384):
    r, n = buf.shape[-2:]
    k = int(np.prod(buf.shape[:-2]))
    tr = _tile(r, rows, SUBLANES)

    def body(b_ref, o_ref):
        acc = b_ref[0]
        for s in range(1, k):
            acc = acc + b_ref[s]
        o_ref[...] = acc

    return pl.pallas_call(
        body, name=name, grid=(r // tr,), in_specs=[pl.BlockSpec((k, tr, n), lambda i: (0, i, 0))],
        out_specs=pl.BlockSpec((tr, n), lambda i: (i, 0)), out_shape=jax.ShapeDtypeStruct((r, n), F32),
        compiler_params=_params(("parallel",)))(buf.reshape((k, r, n)))


ROW = 1024
BIG = (("w_in", 2), ("w_out", 1), ("w_cq", 1), ("w_ck", 1), ("w_cv", 1), ("w_co", 2), ("w_up", 2), ("w_down", 1))
CONV = ("lru_conv_w", "ffn_conv_w")
REPLICATED = ("norm_mix_g", "b_forget", "lru_conv_b", "lru_w_a", "lru_b_a", "lru_w_x", "lru_b_x", "lru_lambda",
              "norm_cross_g", "norm_mem_g", "norm_ffn_g", "ffn_conv_b", "rel_bias", "final_norm_g")
WEIGHTS = ('norm_mix_g', 'w_in', 'b_forget', 'lru_conv_w', 'lru_conv_b', 'lru_w_a', 'lru_b_a', 'lru_w_x', 'lru_b_x',
           'lru_lambda', 'w_out', 'norm_cross_g', 'norm_mem_g', 'w_cq', 'w_ck', 'w_cv', 'w_co', 'norm_ffn_g', 'w_up',
           'ffn_conv_w', 'ffn_conv_b', 'w_down', 'rel_bias', 'final_norm_g')
INPUTS = ("x", "mem") + WEIGHTS + ("loss_target",) + tuple("m_" + n for n in WEIGHTS) + tuple("v_" + n for n in WEIGHTS)


def _round_up(n, m):
    return -(-n // m) * m


class _Packing:
    def __init__(self, entries):
        self.entries, self.off = entries, {}
        o = 0
        for name, shape in entries:
            self.off[name] = o
            o += _round_up(int(np.prod(shape)), ROW)
        self.used = o
        self.rows = _round_up(o // ROW, SUBLANES)

    def pack(self, arrays):
        parts = []
        for name, shape in self.entries:
            n = int(np.prod(shape))
            parts.append(jnp.pad(arrays[name].reshape(n), (0, _round_up(n, ROW) - n)))
        tail = self.rows * ROW - self.used
        if tail:
            parts.append(jnp.zeros((tail,), F32))
        return jnp.concatenate(parts).reshape(self.rows, ROW)

    def unpack(self, flat, lead=()):
        flat = flat.reshape(lead + (self.rows * ROW,))
        out = {}
        for name, shape in self.entries:
            n = int(np.prod(shape))
            out[name] = lax.slice_in_dim(flat, self.off[name], self.off[name] + n, axis=len(lead)).reshape(
                lead + tuple(shape))
        return out


def _to_shards(g, axis):
    r, c = g.shape
    if axis == 1:
        return g.reshape(4, r // 4, c)
    return g.reshape(r, 4, c // 4).transpose(1, 0, 2)


def _from_shards(s, axis):
    _, nl, r, c = s.shape
    if axis == 1:
        return s.transpose(1, 0, 2, 3).reshape(nl, 4 * r, c)
    return s.transpose(1, 2, 0, 3).reshape(nl, r, 4 * c)


def _proj_blocks():
    blocks = []
    for mixer in (MIX_SB, MIX_FOX, MIX_DIL):
        for p in range(2):
            blocks += [ORIG_COL[mixer] + part * 2 * LANES + p * LANES for part in range(3)]
    for hf in range(2):
        blocks += [ORIG_LRU_X + hf * LANES, ORIG_LRU_G + hf * LANES]
    return blocks


def _pad_w_in(w):
    parts = [w[..., s:s + LANES] for s in _proj_blocks()]
    parts += [w[..., 1536:1540], jnp.zeros(w.shape[:-1] + (PROJ_W - COL_F - N_HEADS,), w.dtype)]
    return jnp.concatenate(parts, axis=-1)


def _unpad_w_in(wp):
    blocks = _proj_blocks()
    order = sorted(range(len(blocks)), key=lambda i: blocks[i])
    parts = []
    for i in order:
        if blocks[i] == ORIG_COL[MIX_DIL]:
            parts.append(wp[..., COL_F:COL_F + N_HEADS])
        parts.append(wp[..., i * LANES:(i + 1) * LANES])
    return jnp.concatenate(parts, axis=-1)


def _block_diag(w):
    z = jnp.zeros((HEAD_DIM, HEAD_DIM), w.dtype)
    half = lambda a, b: jnp.concatenate([jnp.concatenate([a, z], 1), jnp.concatenate([z, b], 1)], 0)
    return jnp.stack([half(w[0], w[1]), half(w[2], w[3])])


def _block_diag_grad(d):
    return jnp.stack([d[0, :HEAD_DIM, :HEAD_DIM], d[0, HEAD_DIM:, HEAD_DIM:],
                      d[1, :HEAD_DIM, :HEAD_DIM], d[1, HEAD_DIM:, HEAD_DIM:]])


def _fox_layouts(cum, nb, seq):
    tk = min(ATT_TILE, seq)
    col = cum.reshape(nb, 2, 2, seq).transpose(0, 1, 3, 2)
    row = cum.reshape(nb, 2, 2, seq // tk, tk).transpose(0, 1, 3, 2, 4)
    return col, row


def _layer_params(w, l, nb):
    lru_vec = jnp.concatenate([w["lru_conv_w"][l], w["lru_conv_b"][l][None], w["lru_b_a"][l][None],
                               w["lru_b_x"][l][None], w["lru_lambda"][l][None]], axis=0)
    ffn_cw = jnp.concatenate([w["ffn_conv_w"][l], w["ffn_conv_b"][l][None],
                              jnp.zeros((SUBLANES - 4, 2 * D_FF), F32)], axis=0)
    return dict(
        w_in=w["w_in_padded"][l], lru_vec=lru_vec,
        wa=_block_diag(w["lru_w_a"][l]).astype(BF16), wx=_block_diag(w["lru_w_x"][l]).astype(BF16),
        ffn_cw=ffn_cw, b_rows=jnp.tile(w["b_forget"][l], nb).reshape(nb * N_HEADS, 1))


NORM_ROWS = 512


def _layer_fwd(x, h, mem, w, lp, l, next_g, bias, nb, slots=None):
    t, d = x.shape
    seq = t // nb
    tag = f"l{l}"
    sv = dict(x0=x)
    proj = _mm(h, lp["w_in"], name=tag + "_proj")
    comm = [None, None] if slots is None else [_gather_comm(slots, l + 1, 0), None]
    res = _sb_attn_fwd(proj, lax.empty((t, d), F32), nb=nb, name=tag + "_sb_fwd", comm=comm[0])
    (mixed, ltot), slots = (res, None) if slots is None else res
    f_rows = proj[:, COL_F:COL_F + N_HEADS].reshape(nb, seq, N_HEADS).transpose(0, 2, 1).reshape(nb * N_HEADS, seq)
    cum_col, cum_row = _fox_layouts(_fox_gate_fwd(f_rows, lp["b_rows"]), nb, seq)
    if slots is not None:
        comm[1] = _gather_comm(slots, l + 1, 1)
    res = _softmax_attn_fwd(proj, nb=nb, mode="fox", mixer=MIX_FOX, out_buf=mixed, extra=(cum_col, cum_row),
                            name=tag + "_fox_fwd", comm=comm[1])
    (mixed, lse_fox), slots = (res, None) if slots is None else res
    mixed, lse_dil = _softmax_attn_fwd(proj, nb=nb, mode="dil", mixer=MIX_DIL, out_buf=mixed, extra=(bias,),
                                       name=tag + "_dil_fwd")
    mixed = _lru_fwd(proj, lp["lru_vec"], lp["wa"], lp["wx"], mixed, nb=nb, name=tag + "_lru_fwd")
    x1, hq = _mm(mixed, w["w_out"][l], res=x, norm_g=w["norm_cross_g"][l], ti=NORM_ROWS, name=tag + "_out")
    memn = _rmsnorm(mem, w["norm_mem_g"][l], name=tag + "_norm_mem")
    q = _mm(hq, w["w_cq"][l], name=tag + "_cq")
    k = _mm(memn, w["w_ck"][l], name=tag + "_ck")
    v = _mm(memn, w["w_cv"][l], name=tag + "_cv")
    oc, lse_c = _softmax_attn_fwd((q, k, v), nb=nb, mode="cross", name=tag + "_cross_fwd")
    x2, hn = _mm(oc, w["w_co"][l], res=x1, norm_g=w["norm_ffn_g"][l], ti=NORM_ROWS, name=tag + "_co")
    hf = _mm(hn, w["w_up"][l], name=tag + "_up")
    act = _ffn_act(hf, lp["ffn_cw"], seq=seq, name=tag + "_ffn_act")
    if next_g is None:
        x3, h_next = _mm(act, w["w_down"][l], res=x2, name=tag + "_down"), None
    else:
        x3, h_next = _mm(act, w["w_down"][l], res=x2, norm_g=next_g, ti=NORM_ROWS, name=tag + "_down")
    sv.update(h=h, proj=proj, ltot=ltot, f_rows=f_rows, cum_col=cum_col, cum_row=cum_row, lse_fox=lse_fox,
              lse_dil=lse_dil, mixed=mixed, x1=x1, hq=hq, memn=memn, q=q, k=k, v=v, oc=oc, lse_c=lse_c, x2=x2,
              hn=hn, hf=hf, act=act)
    return x3, h_next, sv, slots


class _PendingReduce:
    def __init__(self, full, pos, layer):
        self.names, self.full, self.pos, self.layer = list(full), list(full.values()), pos, layer

    def sibling_comm(self):
        return _reduce_sibling_comm(self.full)

    def add(self, from_sibling):
        self.partial = [_add_own_half(f, r, self.pos, name=f"l{self.layer}_reduce_add_{n}")
                        for f, r, n in zip(self.full, from_sibling, self.names)]

    def chips_comm(self):
        return _reduce_chips_comm(self.partial)

    def finish(self, others, g_shard):
        g_shard = dict(g_shard)
        for p, o, n in zip(self.partial, others, self.names):
            g_shard[n] = _sum_into(p, o, g_shard[n], self.pos, self.layer, name=f"l{self.layer}_reduce_sum_{n}")
        return g_shard


def _layer_bwd(dx3, mem, sv, w, lp, l, bias, nb, pos, pending=None, g_shard=None, reduce_early=False):
    t = dx3.shape[0]
    seq = t // nb
    tag = f"l{l}"
    g = {}
    down_rows = _tile(sv["act"].shape[1], 1408)
    if pending is None:
        g["w_down"] = _mm(sv["act"], dx3, ta=True, ti=down_rows, name=tag + "_dw_down")
    else:
        g["w_down"], from_sibling = _mm(sv["act"], dx3, ta=True, ti=down_rows, comm=pending.sibling_comm(),
                                        name=tag + "_dw_down")
        pending.add(from_sibling)
    dact = _mm(dx3, w["w_down"][l], tb=True, name=tag + "_dact")
    if pending is None:
        dhf, dcu, dcg = _ffn_bwd(sv["hf"], lp["ffn_cw"], dact, seq=seq, name=tag + "_ffn_bwd")
    else:
        (dhf, dcu, dcg), others = _ffn_bwd(sv["hf"], lp["ffn_cw"], dact, seq=seq, name=tag + "_ffn_bwd",
                                           comm=pending.chips_comm())
        g_shard = pending.finish(others, g_shard)
    dcw = jnp.concatenate([dcu, dcg], axis=1)
    g["ffn_conv_w"], g["ffn_conv_b"] = dcw[:3], dcw[3]
    g["w_up"] = _mm(sv["hn"], dhf, ta=True, halves="b", col_shards=4, name=tag + "_dw_up")
    early = _PendingReduce(_big_grad_shards(g, EARLY), pos, l) if reduce_early else None
    res = _mm(dhf, w["w_up"][l], tb=True, halves="a", norm_bwd=(sv["x2"], w["norm_ffn_g"][l], dx3), ti=NORM_ROWS,
              comm=early.sibling_comm() if early else None, name=tag + "_dhn")
    if early:
        res, from_sibling = res
        early.add(from_sibling)
    dx2, dg = res
    g["norm_ffn_g"] = dg.reshape(-1)
    g["w_co"] = _mm(sv["oc"], dx2, ta=True, col_shards=4, name=tag + "_dw_co")
    doc = _mm(dx2, w["w_co"][l], tb=True, name=tag + "_doc")
    dq, dk, dv = _softmax_attn_bwd((sv["q"], sv["k"], sv["v"]), sv["oc"], sv["lse_c"], doc, nb=nb, mode="cross",
                                   name=tag + "_cross_bwd")
    g["w_cq"] = _mm(sv["hq"], dq, ta=True, name=tag + "_dw_cq")
    g["w_ck"] = _mm(sv["memn"], dk, ta=True, name=tag + "_dw_ck")
    g["w_cv"] = _mm(sv["memn"], dv, ta=True, name=tag + "_dw_cv")
    dx1, dg = _mm(dq, w["w_cq"][l], tb=True, norm_bwd=(sv["x1"], w["norm_cross_g"][l], dx2), ti=NORM_ROWS,
                  name=tag + "_dhq")
    g["norm_cross_g"] = dg.reshape(-1)
    dmemn = _mm(dv, w["w_cv"][l], tb=True, res=_mm(dk, w["w_ck"][l], tb=True, name=tag + "_dmem_k"),
                name=tag + "_dmem_v")
    _, g["norm_mem_g"] = _rmsnorm_bwd(dmemn, mem, w["norm_mem_g"][l], None, name=tag + "_norm_mem_bwd")
    mixed, proj = sv["mixed"], sv["proj"]
    g["w_out"] = _mm(mixed, dx1, ta=True, name=tag + "_dw_out")
    dmixed = _mm(dx1, w["w_out"][l], tb=True, name=tag + "_dmixed")
    dproj = _sb_attn_bwd(proj, sv["ltot"], dmixed, lax.empty((t, PROJ_W), F32), nb=nb, name=tag + "_sb_bwd",
                         comm=early.chips_comm() if early else None)
    if early:
        dproj, others = dproj
        g_shard = early.finish(others, g_shard)
    dproj, dcum_k, dcum_q = _softmax_attn_bwd(
        proj, mixed, sv["lse_fox"], dmixed, nb=nb, mode="fox", mixer=MIX_FOX, dbuf=dproj,
        extra=(sv["cum_col"], sv["cum_row"]), name=tag + "_fox_bwd")
    dcum = (dcum_k.transpose(0, 1, 3, 2, 4).reshape(nb * N_HEADS, seq)
            + dcum_q.transpose(0, 1, 3, 2).reshape(nb * N_HEADS, seq))
    df_rows, db = _fox_gate_bwd(dcum, sv["f_rows"], lp["b_rows"])
    g["b_forget"] = db[:N_HEADS, 0]
    df = df_rows.reshape(nb, N_HEADS, seq).transpose(0, 2, 1).reshape(t, N_HEADS)
    dproj, dbias = _softmax_attn_bwd(proj, mixed, sv["lse_dil"], dmixed, nb=nb, mode="dil", mixer=MIX_DIL,
                                     dbuf=dproj, extra=(bias,), name=tag + "_dil_bwd")
    dproj, dvec, dwa, dwx = _lru_bwd(proj, lp["lru_vec"], lp["wa"], lp["wx"], dmixed, dproj, nb=nb,
                                     name=tag + "_lru_bwd")
    g["lru_conv_w"], g["lru_conv_b"], g["lru_b_a"], g["lru_b_x"], g["lru_lambda"] = (
        dvec[0:4], dvec[4], dvec[5], dvec[6], dvec[7])
    g["lru_w_a"], g["lru_w_x"] = _block_diag_grad(dwa), _block_diag_grad(dwx)
    dproj = lax.dynamic_update_slice(dproj, jnp.pad(df, ((0, 0), (0, PROJ_W - COL_F - N_HEADS))), (0, COL_F))
    g["w_in_padded"] = _mm(sv["h"], dproj, ta=True, name=tag + "_dw_in")
    dx0, dg = _mm(dproj, lp["w_in"], tb=True, norm_bwd=(sv["x0"], w["norm_mix_g"][l], dx1), ti=NORM_ROWS,
                  name=tag + "_dh")
    g["norm_mix_g"] = dg.reshape(-1)
    return dx0, g, dbias, g_shard


def _big_grad_shards(g, names):
    out = {}
    for n, axis in BIG:
        if n not in names:
            continue
        if n in ("w_up", "w_co"):
            out[n] = g[n]
        else:
            out[n] = _to_shards(_unpad_w_in(g["w_in_padded"]) if n == "w_in" else g[n], axis)
    return out


EARLY = ("w_down", "w_up")


def kernel(*args):
    a = dict(zip(INPUTS, args, strict=True))
    nb, seq, d = a["x"].shape
    depth = a["norm_mix_g"].shape[0]
    x = a["x"].reshape(nb * seq, d)
    mem = a["mem"].reshape(nb * a["mem"].shape[1], d)
    target = a["loss_target"].reshape(nb * seq, d)
    cx, cy, c = _mesh_pos()
    chip = 2 * cx + cy
    pos = jnp.stack([chip, c]).astype(jnp.int32)

    slots = []
    for n, _ in BIG:
        own = a[n].astype(BF16)[None]
        slots.append(lax.dynamic_update_slice(lax.empty((4,) + own.shape[1:], BF16), own, (chip,) + (0,) * (own.ndim - 1)))
    slots = _run_comm(_gather_comm(slots, 0, 0), name="gather_l0_chips")
    slots = _run_comm(_gather_comm(slots, 0, 1), name="gather_l0_sibling")
    w = {n: a[n] for n in REPLICATED}
    w.update({n: {} for n, _ in BIG}, w_in_padded={})

    def take_layer(bufs, l):
        for (n, axis), buf in zip(BIG, bufs):
            w[n][l] = _from_shards(buf[:, l:l + 1], axis)[0]
        w["w_in_padded"][l] = _pad_w_in(w["w_in"][l])

    take_layer(slots, 0)
    cpk = _Packing([(n, a[n].shape) for n in CONV])
    conv = cpk.unpack(_chip_bcast(cpk.pack({n: a[n] for n in CONV}), name="gather_conv"), lead=(4,))
    for n in CONV:
        w[n] = jnp.moveaxis(conv[n], 0, 2).reshape(a[n].shape[:2] + (4 * a[n].shape[2],))

    bias = _dil_bias(w["rel_bias"], seq)
    lps, saved = [], []
    h = _rmsnorm(x, w["norm_mix_g"][0], name="l0_norm_mix")
    for l in range(depth):
        more = l + 1 < depth
        lps.append(_layer_params(w, l, nb))
        x, h, sv, slots = _layer_fwd(x, h, mem, w, lps[l], l, w["norm_mix_g"][l + 1] if more else None, bias, nb,
                                     slots=slots if more else None)
        saved.append(sv)
        if more:
            take_layer(slots, l + 1)
    loss, dx, dg_final = _loss_head(x, w["final_norm_g"], target)
    small_g = [None] * depth
    dbias, pending = None, None
    g_shard = {n: lax.empty(a[n].shape, F32) for n, _ in BIG}
    for l in reversed(range(depth)):
        bottom = l == 0
        dx, g, db, g_shard = _layer_bwd(dx, mem, saved[l], w, lps[l], l, bias, nb, pos, pending=pending,
                                        g_shard=g_shard, reduce_early=bottom)
        dbias = db if dbias is None else dbias + db
        small_g[l] = g
        left = [n for n, _ in BIG if not (bottom and n in EARLY)]
        pending = _PendingReduce(_big_grad_shards(g, left), pos, l)
    pending.add(_run_comm(pending.sibling_comm(), name="reduce_sibling"))
    g_shard = pending.finish(_run_comm(pending.chips_comm(), name="reduce_chips"), g_shard)
    names = [n for n, _ in BIG]
    g_shard = dict(zip(names, _share_halves([g_shard[n] for n in names], name="reduce_share")))
    out = {}
    for n in names:
        delta, new_m, new_v = _adamw(a[n], g_shard[n], a["m_" + n], a["v_" + n], name="adamw_" + n)
        out[n] = (g_shard[n], delta, new_m, new_v)

    grads = {n: jnp.stack([small_g[l][n] for l in range(depth)]) for n in REPLICATED + CONV
             if n not in ("rel_bias", "final_norm_g")}
    grads["rel_bias"] = _dil_bias_bwd(dbias, seq)
    grads["final_norm_g"] = dg_final
    grads["loss"] = loss.reshape(1)
    spk = _Packing([(n, grads[n].shape) for n in REPLICATED + CONV + ("loss",)])
    s_all = _chip_bcast(_sibling_pair(spk.pack(grads), name="small_sibling"), name="small_chips")
    total = spk.unpack(_sum_slots(s_all, name="small_sum"))
    for n in CONV:
        width = a[n].shape[2]
        total[n] = lax.dynamic_slice_in_dim(total[n], chip * width, width, axis=2)
    apk = _Packing([(n, a[n].shape) for n in REPLICATED + CONV])
    s_out = _adamw(*[apk.pack(src)[None] for src in (
        {n: a[n] for n in REPLICATED + CONV}, total, {n: a["m_" + n] for n in REPLICATED + CONV},
        {n: a["v_" + n] for n in REPLICATED + CONV})], name="adamw_small")
    s_delta, s_m, s_v = [apk.unpack(o[0]) for o in s_out]
    for n in REPLICATED + CONV:
        out[n] = (total[n], s_delta[n], s_m[n], s_v[n])

    return (total["loss"].reshape(()), dx.reshape(nb, seq, d), *[out[n][0] for n in WEIGHTS],
            *[out[n][1] for n in WEIGHTS], *[out[n][2] for n in WEIGHTS], *[out[n][3] for n in WEIGHTS])
```

```python
import math

import numpy as np
import jax
import jax.numpy as jnp
from jax import lax
from jax.experimental import pallas as pl
from jax.experimental.pallas import tpu as pltpu

F32 = jnp.float32
BF16 = jnp.bfloat16

HEAD_DIM = 64
N_HEADS = 4
N_IN = 2820
D_FF = 2816
LRU_C = 8.0
EPS = 1e-6
NUM_BUCKETS = 32
MAX_DISTANCE = 2048
DILATED_PATTERNS = ((128, 1), (512, 4), (2048, 16))
ADAM_LR, ADAM_B1, ADAM_B2, ADAM_EPS, ADAM_WD, ADAM_STEP = 0.001, 0.9, 0.999, 1e-08, 0.01, 10

LANES = 128
SUBLANES = 8
VMEM_LIMIT = 48 * 1024 * 1024

PROJ_W = 3072
PAIR_W = 3 * LANES
LRU_W = 2 * LANES
COL_LRU = 6 * PAIR_W
COL_F = COL_LRU + 2 * LRU_W
MIX_SB, MIX_FOX, MIX_DIL, MIX_LRU = 0, 1, 2, 3
ORIG_COL = {MIX_SB: 0, MIX_FOX: 768, MIX_DIL: 1540}
ORIG_LRU_X, ORIG_LRU_G = 2308, 2564

ATT_TILE = 256
MASKED = -1e30
SCALE = HEAD_DIM ** -0.5

NT_DIMS = (((1,), (1,)), ((), ()))
TN_DIMS = (((0,), (0,)), ((), ()))

MESH = pl.DeviceIdType.MESH
ANY = pl.BlockSpec(memory_space=pl.ANY)


def _params(sem):
    return pltpu.CompilerParams(dimension_semantics=sem, vmem_limit_bytes=VMEM_LIMIT)


def _tile(n, target, unit=LANES):
    if n <= target:
        return n
    t = (target // unit) * unit
    while t > unit and n % t:
        t -= unit
    assert n % t == 0, (n, target, unit)
    return t


def _mm(a, b, *, ta=False, tb=False, res=None, col_shards=1, halves=None, norm_g=None, norm_bwd=None, comm=None,
        name, ti=1024, tj=1408, tc=1408):
    if halves == "a":
        m, kc = a.shape[1], 2 * a.shape[2]
    else:
        m, kc = (a.shape[1], a.shape[0]) if ta else a.shape
    if halves == "b":
        n = 2 * b.shape[2]
        assert b.shape[1] == kc
    else:
        n = b.shape[0] if tb else b.shape[1]
        assert (b.shape[1] if tb else b.shape[0]) == kc
    assert n % col_shards == 0
    n_blk = n // (2 if halves == "b" else col_shards)
    k_blk = kc // 2 if halves == "a" else kc
    ti, tj, tc = (_tile(m, ti, LANES if ta else SUBLANES), _tile(n_blk, tj),
                  _tile(k_blk, tc, SUBLANES if ta and tb else LANES))
    per_shard, per_half_j, per_half_k = n // col_shards // tj, n_blk // tj, k_blk // tc
    nk = kc // tc
    dims = (((0 if ta else 1,), (1 if tb else 0,)), ((), ()))
    rows_whole = norm_g is not None or norm_bwd is not None
    assert not rows_whole or (tj == n and col_shards == 1)
    n_extra = (res is not None) + (norm_g is not None) + (3 if norm_bwd is not None else 0)
    n_out = 2 if rows_whole else 1

    def finish(val, ex, outs):
        if res is not None:
            val = ex[0][...] + val
        if norm_g is not None:
            outs[0][...] = val
            outs[1][...] = (_xhat(val) * ex[-1][...]).astype(BF16)
        elif norm_bwd is not None:
            x_ref, g_ref, r_ref = ex[-3:]
            dx, dgr = _norm_bwd_rows(val, x_ref[...], g_ref[...])
            outs[0][...] = r_ref[...] + dx

            @pl.when(pl.program_id(0) == 0)
            def _():
                outs[1][...] = jnp.zeros_like(outs[1])

            outs[1][...] += jnp.sum(dgr, axis=0, keepdims=True)
        else:
            outs[0][...] = val

    def body(*refs):
        a_ref, b_ref = refs[:2]
        ex = refs[2:2 + n_extra]
        outs = refs[2 + n_extra:2 + n_extra + n_out]
        part = lax.dot_general(a_ref[...].astype(BF16), b_ref[...].astype(BF16), dims, preferred_element_type=F32)
        if nk == 1:
            finish(part, ex, outs)
            return
        acc_ref = refs[-1]
        k = pl.program_id(2)

        @pl.when(k == 0)
        def _():
            acc_ref[...] = part

        @pl.when(k > 0)
        def _():
            acc_ref[...] += part

        @pl.when(k == nk - 1)
        def _():
            finish(acc_ref[...], ex, outs)

    if halves == "a":
        a_spec = pl.BlockSpec((None, ti, tc), lambda i, j, k: (k // per_half_k, i, k % per_half_k))
    elif ta:
        a_spec = pl.BlockSpec((tc, ti), lambda i, j, k: (k, i))
    else:
        a_spec = pl.BlockSpec((ti, tc), lambda i, j, k: (i, k))
    if halves == "b":
        b_spec = pl.BlockSpec((None, tc, tj), lambda i, j, k: (j // per_half_j, k, j % per_half_j))
    elif tb:
        b_spec = pl.BlockSpec((tj, tc), lambda i, j, k: (j, k))
    else:
        b_spec = pl.BlockSpec((tc, tj), lambda i, j, k: (k, j))
    o_spec = pl.BlockSpec((ti, tj), lambda i, j, k: (i, j))
    vec = pl.BlockSpec((1, tj), lambda i, j, k: (0, 0))
    in_specs, args = [a_spec, b_spec], [a, b]
    out_specs, out_shape = [o_spec], [jax.ShapeDtypeStruct((m, n), F32)]
    if res is not None:
        in_specs.append(o_spec)
        args.append(res)
    if norm_g is not None:
        in_specs.append(vec)
        args.append(norm_g.reshape(1, n))
        out_specs.append(o_spec)
        out_shape.append(jax.ShapeDtypeStruct((m, n), BF16))
    if norm_bwd is not None:
        x, g, dres = norm_bwd
        in_specs += [o_spec, vec, o_spec]
        args += [x, g.reshape(1, n), dres]
        out_specs.append(vec)
        out_shape.append(jax.ShapeDtypeStruct((1, n), F32))
    if col_shards > 1:
        assert n_extra == 0
        out_specs = [pl.BlockSpec((None, ti, tj), lambda i, j, k: (j // per_shard, i, j % per_shard))]
        out_shape = [jax.ShapeDtypeStruct((col_shards, m, n // col_shards), F32)]
    sem = ("arbitrary",) * 3 if norm_bwd is not None else ("parallel", "parallel", "arbitrary")
    out, carried = _pallas(body, name=name, grid=(m // ti, n // tj, nk), in_specs=in_specs, out_specs=out_specs,
                           out_shape=out_shape, args=args, scratch=[] if nk == 1 else [pltpu.VMEM((ti, tj), F32)],
                           sem=sem, comm=comm)
    out = out if rows_whole else out[0]
    return out if comm is None else (out, carried)


def _xhat(x):
    return x * lax.rsqrt(jnp.mean(x * x, axis=-1, keepdims=True) + EPS)


def _norm_bwd_rows(dy, x, g):
    rstd = lax.rsqrt(jnp.mean(x * x, axis=-1, keepdims=True) + EPS)
    xh = x * rstd
    dxh = dy * g
    dx = rstd * (dxh - xh * jnp.mean(dxh * xh, axis=-1, keepdims=True))
    return dx, dy * xh


def _rmsnorm(x, g, *, name, rows=512):
    t, d = x.shape
    tr = _tile(t, rows, 2 * SUBLANES)

    def body(x_ref, g_ref, o_ref):
        o_ref[...] = (_xhat(x_ref[...]) * g_ref[...]).astype(BF16)

    return pl.pallas_call(
        body, name=name, grid=(t // tr,),
        in_specs=[pl.BlockSpec((tr, d), lambda i: (i, 0)), pl.BlockSpec((1, d), lambda i: (0, 0))],
        out_specs=pl.BlockSpec((tr, d), lambda i: (i, 0)), out_shape=jax.ShapeDtypeStruct((t, d), BF16),
        compiler_params=_params(("parallel",)))(x, g.reshape(1, d))


def _rmsnorm_bwd(dy, x, g, dres, *, name, rows=512):
    t, d = x.shape
    tr = _tile(t, rows, SUBLANES)

    def body(*refs):
        if dres is None:
            dy_ref, x_ref, g_ref, dx_ref, dg_ref = refs
        else:
            dy_ref, x_ref, g_ref, r_ref, dx_ref, dg_ref = refs
        dx, dgr = _norm_bwd_rows(dy_ref[...], x_ref[...], g_ref[...])
        dx_ref[...] = dx if dres is None else r_ref[...] + dx

        @pl.when(pl.program_id(0) == 0)
        def _():
            dg_ref[...] = jnp.zeros_like(dg_ref)

        dg_ref[...] += jnp.sum(dgr, axis=0, keepdims=True)

    row = pl.BlockSpec((tr, d), lambda i: (i, 0))
    vec = pl.BlockSpec((1, d), lambda i: (0, 0))
    in_specs = [row, row, vec] + ([] if dres is None else [row])
    args = (dy, x, g.reshape(1, d)) + (() if dres is None else (dres,))
    dx, dg = pl.pallas_call(
        body, name=name, grid=(t // tr,), in_specs=in_specs, out_specs=[row, vec],
        out_shape=[jax.ShapeDtypeStruct((t, d), F32), jax.ShapeDtypeStruct((1, d), F32)],
        compiler_params=_params(("arbitrary",)))(*args)
    return dx, dg.reshape(d)


def _loss_head(x, g, target, *, rows=512):
    t, d = x.shape
    tr = _tile(t, rows, SUBLANES)

    def body(x_ref, g_ref, t_ref, dx_ref, dg_ref, loss_ref):
        x_, g_ = x_ref[...], g_ref[...]
        err = _xhat(x_) * g_ - t_ref[...]
        dx, dgr = _norm_bwd_rows(err * (1.0 / d), x_, g_)
        dx_ref[...] = dx

        @pl.when(pl.program_id(0) == 0)
        def _():
            dg_ref[...] = jnp.zeros_like(dg_ref)
            loss_ref[...] = jnp.zeros_like(loss_ref)

        dg_ref[...] += jnp.sum(dgr, axis=0, keepdims=True)
        loss_ref[...] += 0.5 * jnp.sum(jnp.mean(err * err, axis=-1, keepdims=True), axis=0, keepdims=True)

    row = pl.BlockSpec((tr, d), lambda i: (i, 0))
    vec = pl.BlockSpec((1, d), lambda i: (0, 0))
    one = pl.BlockSpec((1, 1), lambda i: (0, 0))
    dx, dg, loss = pl.pallas_call(
        body, name="loss_head", grid=(t // tr,), in_specs=[row, vec, row], out_specs=[row, vec, one],
        out_shape=[jax.ShapeDtypeStruct((t, d), F32), jax.ShapeDtypeStruct((1, d), F32),
                   jax.ShapeDtypeStruct((1, 1), F32)],
        compiler_params=_params(("arbitrary",)))(x, g.reshape(1, d), target)
    return loss.reshape(()), dx, dg.reshape(d)


def _head_masks(shape):
    lane = lax.broadcasted_iota(jnp.int32, shape, len(shape) - 1)
    return lane < HEAD_DIM, lane >= HEAD_DIM


def _split_heads(x):
    m0, m1 = _head_masks(x.shape)
    zero = jnp.zeros_like(x)
    return jnp.where(m0, x, zero), jnp.where(m1, x, zero)


def _lane_pair(a0, a1, rows):
    m0, _ = _head_masks((rows, LANES))
    return jnp.where(m0, a0, a1)


def _qkv_readers(refs, packed):
    if packed:
        (r,) = refs
        return tuple((lambda r0, n, s=s: r[pl.ds(r0, n), s * LANES:(s + 1) * LANES]) for s in range(3))
    return tuple((lambda r0, n, ref=ref: ref[pl.ds(r0, n), :]) for ref in refs)


def _pair_spec(seq, col0, width=LANES):
    return pl.BlockSpec((seq, width), lambda p, b: (b, col0 + p))


def _fox_specs(seq, nk, tk):
    return [pl.BlockSpec((None, None, seq, 2), lambda p, b: (b, p, 0, 0)),
            pl.BlockSpec((None, None, nk, 2, tk), lambda p, b: (b, p, 0, 0, 0))]


def _softmax_attn_fwd(src, *, nb, mode, mixer=None, out_buf=None, extra=(), name, comm=None):
    packed = mode != "cross"
    n_src = 1 if packed else 3
    seq_q = (src if packed else src[0]).shape[0] // nb
    seq_k = seq_q if packed else src[1].shape[0] // nb
    tq, tk = min(ATT_TILE, seq_q), min(ATT_TILE, seq_k)
    nq, nk = seq_q // tq, seq_k // tk
    n_ex = len(extra)

    def body(*refs):
        q_at, k_at, v_at = _qkv_readers(refs[:n_src], packed)
        ex = refs[n_src:n_src + n_ex]
        o_ref, lse_ref = refs[-2:]

        def q_tile(i, _):
            r0 = pl.multiple_of(i * tq, tq)
            qm = _split_heads((q_at(r0, tq) * SCALE).astype(BF16))
            if mode == "fox":
                cq = ex[0][pl.ds(r0, tq), :]
                row = r0 + lax.broadcasted_iota(jnp.int32, (tq, tk), 0)

            def k_tile(j, carry, diagonal=False):
                m, l, acc = carry
                c0 = pl.multiple_of(j * tk, tk)
                kt = k_at(c0, tk).astype(BF16)
                vm = _split_heads(v_at(c0, tk).astype(BF16))
                if mode == "fox":
                    ck = ex[1][j]
                hs = range(2)
                s = [lax.dot_general(qm[h], kt, NT_DIMS, preferred_element_type=F32) for h in hs]
                if mode == "fox":
                    s = [s[h] + cq[:, h:h + 1] - ck[h:h + 1, :] for h in hs]
                    if diagonal:
                        keep = (c0 + lax.broadcasted_iota(jnp.int32, (tq, tk), 1)) <= row
                        s = [jnp.where(keep, s[h], MASKED) for h in hs]
                elif mode == "dil":
                    s = [s[h] + ex[0][h, i - j] for h in hs]
                new_m = [jnp.maximum(m[h], jnp.max(s[h], axis=-1, keepdims=True)) for h in hs]
                p = [jnp.exp(s[h] - new_m[h]) for h in hs]
                alpha = [jnp.exp(m[h] - new_m[h]) for h in hs]
                new_l = [alpha[h] * l[h] + jnp.sum(p[h], axis=-1, keepdims=True) for h in hs]
                pv = [jnp.dot(p[h].astype(BF16), vm[h], preferred_element_type=F32) for h in hs]
                acc = acc * _lane_pair(alpha[0], alpha[1], tq) + (pv[0] + pv[1])
                return tuple(new_m), tuple(new_l), acc

            init = ((jnp.full((tq, 1), MASKED, F32),) * 2, (jnp.zeros((tq, 1), F32),) * 2,
                    jnp.zeros((tq, LANES), F32))
            if mode == "fox":
                m, l, acc = k_tile(i, lax.fori_loop(0, i, k_tile, init), True)
            else:
                m, l, acc = lax.fori_loop(0, i + 1 if packed else nk, k_tile, init)
            o_ref[pl.ds(r0, tq), :] = acc / _lane_pair(l[0], l[1], tq)
            lse_ref[pl.ds(r0, tq), :] = _lane_pair(m[0] + jnp.log(l[0]), m[1] + jnp.log(l[1]), tq)
            return 0

        lax.fori_loop(0, nq, q_tile, 0)

    lse_shape = jax.ShapeDtypeStruct((nb * seq_q, 2 * LANES), F32)
    if packed:
        in_specs, args = [_pair_spec(seq_q, 2 * mixer, PAIR_W)], [src]
        in_specs += _fox_specs(seq_q, nk, tk) if mode == "fox" else [
            pl.BlockSpec((None, 2, nq, tq, tk), lambda p, b: (p, 0, 0, 0, 0))]
        args += list(extra) + [out_buf]
        in_specs.append(ANY)
        out_specs = [_pair_spec(seq_q, 2 * mixer), _pair_spec(seq_q, 0)]
        out_shape = [jax.ShapeDtypeStruct(out_buf.shape, F32), lse_shape]
        aliases = {len(args) - 1: 0}
    else:
        in_specs = [_pair_spec(seq_q, 0), _pair_spec(seq_k, 0), _pair_spec(seq_k, 0)]
        args = list(src)
        out_specs = [_pair_spec(seq_q, 0), _pair_spec(seq_q, 0)]
        out_shape = [lse_shape, lse_shape]
        aliases = {}
    out, carried = _pallas(body, name=name, grid=(2, nb), in_specs=in_specs, out_specs=out_specs, out_shape=out_shape,
                           args=args, aliases=aliases, sem=("parallel", "arbitrary"), comm=comm)
    return out if comm is None else (out, carried)


def _softmax_attn_bwd(src, o, lse, do, *, nb, mode, mixer=None, dbuf=None, extra=(), name):
    packed = mode != "cross"
    n_src = 1 if packed else 3
    seq_q = (src if packed else src[0]).shape[0] // nb
    seq_k = seq_q if packed else src[1].shape[0] // nb
    tq, tk = min(ATT_TILE, seq_q), min(ATT_TILE, seq_k)
    nq, nk = seq_q // tq, seq_k // tk
    n_ex = len(extra)
    n_in = n_src + 3 + n_ex + (1 if packed else 0)

    def body(*refs):
        q_at, k_at, v_at = _qkv_readers(refs[:n_src], packed)
        o_ref, lse_ref, do_ref = refs[n_src:n_src + 3]
        ex = refs[n_src + 3:n_src + 3 + n_ex]
        outs = refs[n_in:]
        if packed:
            d_ref = outs[0]
            dq_w = lambda r0, val: d_ref.__setitem__((pl.ds(r0, tq), slice(0, LANES)), val)
            dk_ref = d_ref.at[:, LANES:2 * LANES]
            dv_ref = d_ref.at[:, 2 * LANES:3 * LANES]
        else:
            dq_ref, dk_ref, dv_ref = outs[:3]
            dq_w = lambda r0, val: dq_ref.__setitem__((pl.ds(r0, tq), slice(None)), val)
        dk_ref[...] = jnp.zeros((seq_k, LANES), F32)
        dv_ref[...] = jnp.zeros((seq_k, LANES), F32)
        if mode == "fox":
            dcum_ref, dcq_ref = outs[-2:]
            dcum_ref[...] = jnp.zeros_like(dcum_ref)
        if mode == "dil":
            dbias_ref = outs[-1]

            @pl.when(pl.program_id(1) == 0)
            def _():
                dbias_ref[...] = jnp.zeros_like(dbias_ref)

        def q_tile(i, _):
            r0 = pl.multiple_of(i * tq, tq)
            qm = _split_heads((q_at(r0, tq) * SCALE).astype(BF16))
            do_f = do_ref[pl.ds(r0, tq), :]
            dom = _split_heads(do_f.astype(BF16))
            dd = _split_heads(do_f * o_ref[pl.ds(r0, tq), :])
            delta = [jnp.sum(dd[h], axis=-1, keepdims=True) for h in range(2)]
            lse_t = lse_ref[pl.ds(r0, tq), :]
            lse_h = [lse_t[:, 0:1], lse_t[:, HEAD_DIM:HEAD_DIM + 1]]
            if mode == "fox":
                cq = ex[0][pl.ds(r0, tq), :]
                row = r0 + lax.broadcasted_iota(jnp.int32, (tq, tk), 0)

            def k_tile(j, carry, diagonal=False):
                dq, rs = carry
                c0 = pl.multiple_of(j * tk, tk)
                kt = k_at(c0, tk).astype(BF16)
                vt = v_at(c0, tk).astype(BF16)
                km = _split_heads(kt)
                if mode == "fox":
                    ck = ex[1][j]
                hs = range(2)
                s = [lax.dot_general(qm[h], kt, NT_DIMS, preferred_element_type=F32) for h in hs]
                dp = [lax.dot_general(dom[h], vt, NT_DIMS, preferred_element_type=F32) for h in hs]
                if mode == "fox":
                    s = [s[h] + cq[:, h:h + 1] - ck[h:h + 1, :] for h in hs]
                    if diagonal:
                        keep = (c0 + lax.broadcasted_iota(jnp.int32, (tq, tk), 1)) <= row
                        s = [jnp.where(keep, s[h], MASKED) for h in hs]
                elif mode == "dil":
                    s = [s[h] + ex[0][h, i - j] for h in hs]
                p = [jnp.exp(s[h] - lse_h[h]) for h in hs]
                ds = [p[h] * (dp[h] - delta[h]) for h in hs]
                dsb = [ds[h].astype(BF16) for h in hs]
                pb = [p[h].astype(BF16) for h in hs]
                dq = dq + (jnp.dot(dsb[0], km[0], preferred_element_type=F32)
                           + jnp.dot(dsb[1], km[1], preferred_element_type=F32))
                dk_t = (lax.dot_general(dsb[0], qm[0], TN_DIMS, preferred_element_type=F32)
                        + lax.dot_general(dsb[1], qm[1], TN_DIMS, preferred_element_type=F32))
                dv_t = (lax.dot_general(pb[0], dom[0], TN_DIMS, preferred_element_type=F32)
                        + lax.dot_general(pb[1], dom[1], TN_DIMS, preferred_element_type=F32))
                if mode == "fox":
                    for h in hs:
                        dcum_ref[j, h:h + 1, :] -= jnp.sum(ds[h], axis=0, keepdims=True)
                    rs = tuple(rs[h] + jnp.sum(ds[h], axis=-1, keepdims=True) for h in hs)
                elif mode == "dil":
                    for h in hs:
                        dbias_ref[h, i - j] += ds[h]
                dk_ref[pl.ds(c0, tk), :] += dk_t
                dv_ref[pl.ds(c0, tk), :] += dv_t
                return dq, rs

            zero = (jnp.zeros((tq, 1), F32),) * 2
            init = (jnp.zeros((tq, LANES), F32), zero)
            if mode == "fox":
                dq, rs = k_tile(i, lax.fori_loop(0, i, k_tile, init), True)
            else:
                dq, rs = lax.fori_loop(0, i + 1 if packed else nk, k_tile, init)
            dq_w(r0, dq * SCALE)
            if mode == "fox":
                dcq_ref[pl.ds(r0, tq), :] = jnp.where(lax.broadcasted_iota(jnp.int32, (tq, 2), 1) == 0, rs[0], rs[1])
            return 0

        lax.fori_loop(0, nq, q_tile, 0)

    if packed:
        in_specs = [_pair_spec(seq_q, 2 * mixer, PAIR_W), _pair_spec(seq_q, 2 * mixer), _pair_spec(seq_q, 0),
                    _pair_spec(seq_q, 2 * mixer)]
        args = [src, o, lse, do]
        out_specs = [_pair_spec(seq_q, 2 * mixer, PAIR_W)]
        out_shape = [jax.ShapeDtypeStruct(dbuf.shape, F32)]
        if mode == "fox":
            in_specs += _fox_specs(seq_q, nk, tk)
            out_specs += [_fox_specs(seq_q, nk, tk)[1], _fox_specs(seq_q, nk, tk)[0]]
            out_shape += [jax.ShapeDtypeStruct((nb, 2, nk, 2, tk), F32), jax.ShapeDtypeStruct((nb, 2, seq_q, 2), F32)]
        else:
            tiles = pl.BlockSpec((None, 2, nq, tq, tk), lambda p, b: (p, 0, 0, 0, 0))
            in_specs.append(tiles)
            out_specs.append(tiles)
            out_shape.append(jax.ShapeDtypeStruct((2, 2, nq, tq, tk), F32))
        args += list(extra) + [dbuf]
        in_specs.append(ANY)
        aliases = {len(args) - 1: 0}
    else:
        sq, sk = _pair_spec(seq_q, 0), _pair_spec(seq_k, 0)
        in_specs, args = [sq, sk, sk, sq, sq, sq], list(src) + [o, lse, do]
        out_specs = [sq, sk, sk]
        out_shape = [jax.ShapeDtypeStruct((nb * seq_q, 2 * LANES), F32)] + [
            jax.ShapeDtypeStruct((nb * seq_k, 2 * LANES), F32)] * 2
        aliases = {}
    return pl.pallas_call(
        body, name=name, grid=(2, nb), in_specs=in_specs, out_specs=out_specs, out_shape=out_shape,
        input_output_aliases=aliases, compiler_params=_params(("parallel", "arbitrary")))(*args)


def _log_sigmoid(z):
    return jnp.minimum(z, 0.0) - jnp.log(1.0 + jnp.exp(-jnp.abs(z)))


def _split_bf16(x):
    hi = x.astype(BF16)
    return hi, (x - hi.astype(F32)).astype(BF16)


def _tri(n, fn):
    r = lax.broadcasted_iota(jnp.int32, (n, n), 0)
    c = lax.broadcasted_iota(jnp.int32, (n, n), 1)
    return jnp.where(fn(r, c), 1.0, 0.0).astype(BF16)


def _sb_attn_fwd(proj, out_buf, *, nb, name, comm=None):
    seq = proj.shape[0] // nb
    tq = tk = min(ATT_TILE, seq)
    nq = seq // tq

    def body(qkv_ref, _, o_ref, lt_ref):
        q_at, k_at, v_at = _qkv_readers((qkv_ref,), True)
        after = _tri(tk, lambda r, c: r > c)

        def q_tile(i, _):
            r0 = pl.multiple_of(i * tq, tq)
            qm = _split_heads((q_at(r0, tq) * SCALE).astype(BF16))
            row = r0 + lax.broadcasted_iota(jnp.int32, (tq, tk), 0)

            def k_tile(j, carry, diagonal):
                c, acc = carry
                c0 = pl.multiple_of(j * tk, tk)
                kt = k_at(c0, tk).astype(BF16)
                vm = _split_heads(v_at(c0, tk).astype(BF16))
                if diagonal:
                    strict = (c0 + lax.broadcasted_iota(jnp.int32, (tq, tk), 1)) < row
                hs = range(2)
                z = [lax.dot_general(qm[h], kt, NT_DIMS, preferred_element_type=F32) for h in hs]
                ls = [_log_sigmoid(z[h]) for h in hs]
                lk = [ls[h] - z[h] for h in hs]
                if diagonal:
                    lk = [jnp.where(strict, lk[h], 0.0) for h in hs]
                parts = [_split_bf16(lk[h]) for h in hs]
                sfx = [jnp.dot(parts[h][0], after, preferred_element_type=F32)
                       + jnp.dot(parts[h][1], after, preferred_element_type=F32) for h in hs]
                att = [jnp.exp(ls[h] + sfx[h] + c[h]) for h in hs]
                if diagonal:
                    att = [jnp.where(strict, att[h], 0.0) for h in hs]
                acc = acc + (jnp.dot(att[0].astype(BF16), vm[0], preferred_element_type=F32)
                             + jnp.dot(att[1].astype(BF16), vm[1], preferred_element_type=F32))
                return tuple(c[h] + jnp.sum(lk[h], axis=-1, keepdims=True) for h in hs), acc

            init = ((jnp.zeros((tq, 1), F32),) * 2, jnp.zeros((tq, LANES), F32))
            c, acc = lax.fori_loop(1, i + 1, lambda jj, cr: k_tile(i - jj, cr, False), k_tile(i, init, True))
            o_ref[pl.ds(r0, tq), :] = acc
            lt_ref[pl.ds(r0, tq), :] = _lane_pair(c[0], c[1], tq)
            return 0

        lax.fori_loop(0, nq, q_tile, 0)

    out, carried = _pallas(
        body, name=name, grid=(2, nb), in_specs=[_pair_spec(seq, 2 * MIX_SB, PAIR_W), ANY],
        out_specs=[_pair_spec(seq, 2 * MIX_SB), _pair_spec(seq, 0)],
        out_shape=[jax.ShapeDtypeStruct(out_buf.shape, F32), jax.ShapeDtypeStruct((nb * seq, 2 * LANES), F32)],
        args=[proj, out_buf], aliases={1: 0}, sem=("parallel", "arbitrary"), comm=comm)
    return out if comm is None else (out, carried)


def _sb_attn_bwd(proj, ltot, do, dbuf, *, nb, name, comm=None):
    seq = proj.shape[0] // nb
    tq = tk = min(ATT_TILE, seq)
    nq = seq // tq

    def body(qkv_ref, lt_ref, do_ref, _, d_ref):
        q_at, k_at, v_at = _qkv_readers((qkv_ref,), True)
        upto = _tri(tk, lambda r, c: r <= c)
        before = _tri(tk, lambda r, c: r < c)
        dk_ref = d_ref.at[:, LANES:2 * LANES]
        dv_ref = d_ref.at[:, 2 * LANES:3 * LANES]
        dk_ref[...] = jnp.zeros((seq, LANES), F32)
        dv_ref[...] = jnp.zeros((seq, LANES), F32)

        def q_tile(i, _):
            r0 = pl.multiple_of(i * tq, tq)
            qm = _split_heads((q_at(r0, tq) * SCALE).astype(BF16))
            dom = _split_heads(do_ref[pl.ds(r0, tq), :].astype(BF16))
            lt_t = lt_ref[pl.ds(r0, tq), :]
            lt_h = [lt_t[:, 0:1], lt_t[:, HEAD_DIM:HEAD_DIM + 1]]
            row = r0 + lax.broadcasted_iota(jnp.int32, (tq, tk), 0)

            def k_tile(j, carry, diagonal):
                pc, qc, dq = carry
                c0 = pl.multiple_of(j * tk, tk)
                kt = k_at(c0, tk).astype(BF16)
                vt = v_at(c0, tk).astype(BF16)
                km = _split_heads(kt)
                if diagonal:
                    strict = (c0 + lax.broadcasted_iota(jnp.int32, (tq, tk), 1)) < row
                hs = range(2)
                z = [lax.dot_general(qm[h], kt, NT_DIMS, preferred_element_type=F32) for h in hs]
                da = [lax.dot_general(dom[h], vt, NT_DIMS, preferred_element_type=F32) for h in hs]
                ls = [_log_sigmoid(z[h]) for h in hs]
                lk = [ls[h] - z[h] for h in hs]
                if diagonal:
                    lk = [jnp.where(strict, lk[h], 0.0) for h in hs]
                parts = [_split_bf16(lk[h]) for h in hs]
                pin = [jnp.dot(parts[h][0], upto, preferred_element_type=F32)
                       + jnp.dot(parts[h][1], upto, preferred_element_type=F32) for h in hs]
                att = [jnp.exp(ls[h] + (lt_h[h] - pc[h] - pin[h])) for h in hs]
                if diagonal:
                    att = [jnp.where(strict, att[h], 0.0) for h in hs]
                dg = [att[h] * da[h] for h in hs]
                qx = [qc[h] + jnp.dot(dg[h].astype(BF16), before, preferred_element_type=F32) for h in hs]
                sig = [jnp.exp(ls[h]) for h in hs]
                dz = [dg[h] * (1.0 - sig[h]) - sig[h] * qx[h] for h in hs]
                if diagonal:
                    dz = [jnp.where(strict, dz[h], 0.0) for h in hs]
                dzb = [dz[h].astype(BF16) for h in hs]
                attb = [att[h].astype(BF16) for h in hs]
                dq = dq + (jnp.dot(dzb[0], km[0], preferred_element_type=F32)
                           + jnp.dot(dzb[1], km[1], preferred_element_type=F32))
                dk_ref[pl.ds(c0, tk), :] += (lax.dot_general(dzb[0], qm[0], TN_DIMS, preferred_element_type=F32)
                                             + lax.dot_general(dzb[1], qm[1], TN_DIMS, preferred_element_type=F32))
                dv_ref[pl.ds(c0, tk), :] += (lax.dot_general(attb[0], dom[0], TN_DIMS, preferred_element_type=F32)
                                             + lax.dot_general(attb[1], dom[1], TN_DIMS, preferred_element_type=F32))
                return (tuple(pc[h] + jnp.sum(lk[h], axis=-1, keepdims=True) for h in hs),
                        tuple(qc[h] + jnp.sum(dg[h], axis=-1, keepdims=True) for h in hs), dq)

            zero = (jnp.zeros((tq, 1), F32),) * 2
            carry = lax.fori_loop(0, i, lambda j, cr: k_tile(j, cr, False), (zero, zero, jnp.zeros((tq, LANES), F32)))
            _, _, dq = k_tile(i, carry, True)
            d_ref[pl.ds(r0, tq), 0:LANES] = dq * SCALE
            return 0

        lax.fori_loop(0, nq, q_tile, 0)

    out, carried = _pallas(
        body, name=name, grid=(2, nb),
        in_specs=[_pair_spec(seq, 2 * MIX_SB, PAIR_W), _pair_spec(seq, 0), _pair_spec(seq, 2 * MIX_SB), ANY],
        out_specs=[_pair_spec(seq, 2 * MIX_SB, PAIR_W)], out_shape=[jax.ShapeDtypeStruct(dbuf.shape, F32)],
        args=[proj, ltot, do, dbuf], aliases={3: 0}, sem=("parallel", "arbitrary"), comm=comm)
    return out[0] if comm is None else (out[0], carried)


def _lane_scan(x, reverse=False):
    n = x.shape[-1]
    lane = lax.broadcasted_iota(jnp.int32, x.shape, 1)
    k = 1
    while k < n:
        if reverse:
            x = x + jnp.where(lane < n - k, pltpu.roll(x, n - k, 1), 0.0)
        else:
            x = x + jnp.where(lane >= k, pltpu.roll(x, k, 1), 0.0)
        k *= 2
    return x


def _fox_gate_fwd(f_rows, b_rows):
    def body(f_ref, b_ref, o_ref):
        o_ref[...] = _lane_scan(_log_sigmoid(f_ref[...] + b_ref[...]))

    return pl.pallas_call(body, name="fox_gate_fwd", out_shape=jax.ShapeDtypeStruct(f_rows.shape, F32))(f_rows, b_rows)


def _fox_gate_bwd(dcum, f_rows, b_rows):
    def body(d_ref, f_ref, b_ref, df_ref, db_ref):
        z = f_ref[...] + b_ref[...]
        df = _lane_scan(d_ref[...], reverse=True) * jnp.exp(_log_sigmoid(-z))
        df_ref[...] = df
        rs = jnp.sum(df, axis=-1, keepdims=True)
        tot = rs
        for e in range(1, f_rows.shape[0] // N_HEADS):
            tot = tot + pltpu.roll(rs, e * N_HEADS, 0)
        db_ref[...] = tot

    return pl.pallas_call(
        body, name="fox_gate_bwd",
        out_shape=[jax.ShapeDtypeStruct(f_rows.shape, F32), jax.ShapeDtypeStruct((f_rows.shape[0], 1), F32)],
    )(dcum, f_rows, b_rows)


def _dil_tables(seq):
    t = min(ATT_TILE, seq)
    n = seq // t
    a = np.arange(t)
    d = (np.arange(n)[:, None, None] * t + a[None, :, None] - a[None, None, :]).astype(np.int64)
    count = np.zeros(d.shape, np.int64)
    for window, dil in DILATED_PATTERNS:
        count += (d >= 0) & (d % dil == 0) & (d // dil <= window // dil)
    nn = np.maximum(d, 0)
    max_exact = NUM_BUCKETS // 2
    nf = np.maximum(nn, 1).astype(np.float32)
    large = max_exact + (np.log(nf / np.float32(max_exact)) / np.float32(math.log(MAX_DISTANCE / max_exact))
                         * np.float32(NUM_BUCKETS - max_exact)).astype(np.int32)
    bucket = np.where(nn < max_exact, nn, np.minimum(large, NUM_BUCKETS - 1))
    bucket = np.where(count > 0, bucket, -1).astype(np.int32)
    logc = np.where(count > 0, np.log(np.maximum(count, 1)), MASKED).astype(np.float32)
    return bucket, logc


def _dil_bias(rel_bias, seq):
    bucket, logc = _dil_tables(seq)
    n, t, _ = bucket.shape

    def body(rb_ref, bk_ref, lc_ref, o_ref):
        h = pl.program_id(0) * 2 + pl.program_id(1)
        bk = bk_ref[...]
        out = lc_ref[...]
        for b in range(NUM_BUCKETS):
            out = jnp.where(bk == b, out + rb_ref[b, h], out)
        o_ref[...] = out

    full = pl.BlockSpec((n, t, t), lambda p, h: (0, 0, 0))
    return pl.pallas_call(
        body, name="dil_bias", grid=(2, 2),
        in_specs=[pl.BlockSpec(memory_space=pltpu.SMEM), full, full],
        out_specs=pl.BlockSpec((None, None, n, t, t), lambda p, h: (p, h, 0, 0, 0)),
        out_shape=jax.ShapeDtypeStruct((2, 2, n, t, t), F32),
        compiler_params=_params(("parallel", "parallel")))(rel_bias, jnp.asarray(bucket), jnp.asarray(logc))


def _dil_bias_bwd(dbias, seq):
    bucket, _ = _dil_tables(seq)
    n, t, _ = bucket.shape

    def body(d_ref, bk_ref, o_ref):
        bk = bk_ref[...]
        lane = lax.broadcasted_iota(jnp.int32, (1, LANES), 1)
        for b in range(NUM_BUCKETS):
            rowv = jnp.zeros((1, LANES), F32)
            for h in range(N_HEADS):
                s = jnp.sum(jnp.where(bk == b, d_ref[h // 2, h % 2], 0.0))
                rowv = jnp.where(lane == h, s, rowv)
            o_ref[b:b + 1, :] = rowv

    out = pl.pallas_call(body, name="dil_bias_bwd", out_shape=jax.ShapeDtypeStruct((NUM_BUCKETS, LANES), F32),
                         compiler_params=pltpu.CompilerParams(vmem_limit_bytes=VMEM_LIMIT))(dbias, jnp.asarray(bucket))
    return out[:, :N_HEADS]


def _shift_rows(x, k, row, fill=0.0):
    n = x.shape[0]
    if k > 0:
        return jnp.where(row >= k, pltpu.roll(x, k, 0), fill)
    return jnp.where(row < n + k, pltpu.roll(x, n + k, 0), fill)


def _row_scan(a, u, row, reverse=False):
    n = a.shape[0]
    k = 1
    while k < n:
        s = -k if reverse else k
        u = a * _shift_rows(u, s, row) + u
        a = a * _shift_rows(a, s, row, 1.0)
        k *= 2
    return u


def _sigmoid(x):
    return 1.0 / (1.0 + jnp.exp(-x))


def _gelu(g):
    return 0.5 * g * (1.0 + lax.erf(g * (2.0 ** -0.5)))


def _gelu_grad(g):
    return 0.5 * (1.0 + lax.erf(g * (2.0 ** -0.5))) + g * jnp.exp(-0.5 * g * g) * (1.0 / math.sqrt(2.0 * math.pi))


def _neg_expm1(x):
    small = -x * (1.0 + x * (0.5 + x * (1.0 / 6.0 + x * (1.0 / 24.0))))
    return jnp.where(x > -0.03, small, 1.0 - jnp.exp(x))


def _lru_core(x, vec, wa, wx, row):
    xs = [_shift_rows(x, 3 - j, row) if j < 3 else x for j in range(4)]
    xc = vec[4:5, :]
    for j in range(4):
        xc = xc + vec[j:j + 1, :] * xs[j]
    xcb = xc.astype(BF16)
    r = _sigmoid(jnp.dot(xcb, wa, preferred_element_type=F32) + vec[5:6, :])
    ig = _sigmoid(jnp.dot(xcb, wx, preferred_element_type=F32) + vec[6:7, :])
    lam = vec[7:8, :]
    sp = jnp.maximum(-lam, 0.0) - _log_sigmoid(jnp.abs(lam))
    la = -LRU_C * r * sp
    a = jnp.exp(la)
    mult = jnp.sqrt(_neg_expm1(2.0 * la))
    return xs, xc, xcb, r, ig, sp, la, a, mult


def _lru_specs(seq):
    xg = pl.BlockSpec((seq, LRU_W), lambda hf, b: (b, COL_LRU // LRU_W + hf))
    mix = pl.BlockSpec((seq, LANES), lambda hf, b: (b, 2 * MIX_LRU + hf))
    vec = pl.BlockSpec((SUBLANES, LANES), lambda hf, b: (0, hf))
    mat = pl.BlockSpec((None, LANES, LANES), lambda hf, b: (hf, 0, 0))
    return xg, mix, vec, mat


def _lru_fwd(proj, vec, wa, wx, out_buf, *, nb, name):
    seq = proj.shape[0] // nb

    def body(xg_ref, vec_ref, wa_ref, wx_ref, _, o_ref):
        row = lax.broadcasted_iota(jnp.int32, (seq, LANES), 0)
        _, xc, _, _, ig, _, _, a, mult = _lru_core(xg_ref[:, 0:LANES], vec_ref[...], wa_ref[...], wx_ref[...], row)
        h = _row_scan(a, mult * (ig * xc), row)
        o_ref[...] = h * _gelu(xg_ref[:, LANES:LRU_W])

    xg, mix, vecs, mat = _lru_specs(seq)
    return pl.pallas_call(
        body, name=name, grid=(2, nb), in_specs=[xg, vecs, mat, mat, ANY], out_specs=mix,
        out_shape=jax.ShapeDtypeStruct(out_buf.shape, F32), input_output_aliases={4: 0},
        compiler_params=_params(("parallel", "arbitrary")))(proj, vec, wa, wx, out_buf)


def _lru_bwd(proj, vec, wa, wx, dout, dbuf, *, nb, name):
    seq = proj.shape[0] // nb

    def body(xg_ref, vec_ref, wa_ref, wx_ref, do_ref, _, d_ref, dvec_ref, dwa_ref, dwx_ref):
        row = lax.broadcasted_iota(jnp.int32, (seq, LANES), 0)
        vec_, wa_, wx_ = vec_ref[...], wa_ref[...], wx_ref[...]
        xs, xc, xcb, r, ig, sp, la, a, mult = _lru_core(xg_ref[:, 0:LANES], vec_, wa_, wx_, row)
        h = _row_scan(a, mult * (ig * xc), row)
        gate, do = xg_ref[:, LANES:LRU_W], do_ref[...]
        d_ref[:, LANES:LRU_W] = do * h * _gelu_grad(gate)
        dh = do * _gelu(gate)
        gacc = _row_scan(_shift_rows(a, -1, row), dh, row, reverse=True)
        da = gacc * _shift_rows(h, 1, row)
        dmult = gacc * (ig * xc)
        dig = gacc * (mult * xc)
        dxc = gacc * (mult * ig)
        dla = da * a - dmult * (a * a) / mult
        dr = (-LRU_C) * sp * dla
        dsp = jnp.sum((-LRU_C) * r * dla, axis=0, keepdims=True)
        dpr = dr * r * (1.0 - r)
        dpi = dig * ig * (1.0 - ig)
        dprb, dpib = dpr.astype(BF16), dpi.astype(BF16)
        dxc = (dxc + lax.dot_general(dprb, wa_, NT_DIMS, preferred_element_type=F32)
               + lax.dot_general(dpib, wx_, NT_DIMS, preferred_element_type=F32))
        dx = vec_[3:4, :] * dxc
        for j in range(3):
            dx = dx + vec_[j:j + 1, :] * _shift_rows(dxc, -(3 - j), row)
        d_ref[:, 0:LANES] = dx

        @pl.when(pl.program_id(1) == 0)
        def _():
            dvec_ref[...] = jnp.zeros_like(dvec_ref)
            dwa_ref[...] = jnp.zeros_like(dwa_ref)
            dwx_ref[...] = jnp.zeros_like(dwx_ref)

        for j in range(4):
            dvec_ref[j:j + 1, :] += jnp.sum(dxc * xs[j], axis=0, keepdims=True)
        dvec_ref[4:5, :] += jnp.sum(dxc, axis=0, keepdims=True)
        dvec_ref[5:6, :] += jnp.sum(dpr, axis=0, keepdims=True)
        dvec_ref[6:7, :] += jnp.sum(dpi, axis=0, keepdims=True)
        lam = vec_[7:8, :]
        dvec_ref[7:8, :] += -dsp * _sigmoid(-lam)
        dwa_ref[...] += lax.dot_general(xcb, dprb, TN_DIMS, preferred_element_type=F32)
        dwx_ref[...] += lax.dot_general(xcb, dpib, TN_DIMS, preferred_element_type=F32)

    xg, mix, vecs, mat = _lru_specs(seq)
    return pl.pallas_call(
        body, name=name, grid=(2, nb), in_specs=[xg, vecs, mat, mat, mix, ANY], out_specs=[xg, vecs, mat, mat],
        out_shape=[jax.ShapeDtypeStruct(dbuf.shape, F32), jax.ShapeDtypeStruct((SUBLANES, 2 * LANES), F32),
                   jax.ShapeDtypeStruct((2, LANES, LANES), F32), jax.ShapeDtypeStruct((2, LANES, LANES), F32)],
        input_output_aliases={5: 0},
        compiler_params=_params(("parallel", "arbitrary")))(proj, vec, wa, wx, dout, dbuf)


FFN_ROWS = 256
FFN_COLS = 1408


def _with_halo(halo, x, k):
    xx = jnp.concatenate([halo, x], axis=0)
    return pltpu.roll(xx, k, 0)[SUBLANES:, :]


def _ffn_conv(x_ref, halo_ref, cw, pos):
    x, halo = x_ref[...], halo_ref[...]
    x1 = jnp.where(pos >= 1, _with_halo(halo, x, 1), 0.0)
    x2 = jnp.where(pos >= 2, _with_halo(halo, x, 2), 0.0)
    return cw[3:4, :] + cw[0:1, :] * x2 + cw[1:2, :] * x1 + cw[2:3, :] * x, x1, x2


def _ffn_specs(tm, tn, gate_off):
    prev = lambda i: jnp.maximum(i * (tm // SUBLANES) - 1, 0)
    up = pl.BlockSpec((tm, tn), lambda j, i: (i, j))
    gate = pl.BlockSpec((tm, tn), lambda j, i: (i, j + gate_off))
    up_h = pl.BlockSpec((SUBLANES, tn), lambda j, i: (prev(i), j))
    gate_h = pl.BlockSpec((SUBLANES, tn), lambda j, i: (prev(i), j + gate_off))
    cw_up = pl.BlockSpec((SUBLANES, tn), lambda j, i: (0, j))
    cw_gate = pl.BlockSpec((SUBLANES, tn), lambda j, i: (0, j + gate_off))
    return up, gate, up_h, gate_h, cw_up, cw_gate


def _ffn_act(hf, cw, *, seq, name):
    t, w2 = hf.shape
    w = w2 // 2
    tm, tn = _tile(seq, FFN_ROWS, SUBLANES), _tile(w, FFN_COLS)

    def body(u_ref, g_ref, uh_ref, gh_ref, cu_ref, cg_ref, o_ref):
        pos = (pl.program_id(1) * tm + lax.broadcasted_iota(jnp.int32, (tm, 1), 0)) % seq
        up, _, _ = _ffn_conv(u_ref, uh_ref, cu_ref[...], pos)
        gate, _, _ = _ffn_conv(g_ref, gh_ref, cg_ref[...], pos)
        o_ref[...] = (_gelu(gate) * up).astype(BF16)

    specs = _ffn_specs(tm, tn, w // tn)
    return pl.pallas_call(
        body, name=name, grid=(w // tn, t // tm), in_specs=list(specs), out_specs=specs[0],
        out_shape=jax.ShapeDtypeStruct((t, w), BF16),
        compiler_params=_params(("parallel", "parallel")))(hf, hf, hf, hf, cw, cw)


def _ffn_bwd(hf, cw, dact, *, seq, name, comm=None):
    t, w2 = hf.shape
    w = w2 // 2
    tm, tn = _tile(seq, FFN_ROWS, 2 * SUBLANES), _tile(w, FFN_COLS)
    ext = tm + SUBLANES
    last = t // SUBLANES - 1

    def body(u_ref, g_ref, uh_ref, gh_ref, cu_ref, cg_ref, un_ref, gn_ref, da_ref, dn_ref, d_ref, dcu_ref, dcg_ref):
        pos = (pl.program_id(1) * tm + lax.broadcasted_iota(jnp.int32, (ext, 1), 0)) % seq

        def conv(x_ref, prev_ref, next_ref, cwv):
            xx = jnp.concatenate([prev_ref[...], x_ref[...], next_ref[...]], axis=0)
            x1 = jnp.where(pos >= 1, pltpu.roll(xx, 1, 0)[SUBLANES:, :], 0.0)
            x2 = jnp.where(pos >= 2, pltpu.roll(xx, 2, 0)[SUBLANES:, :], 0.0)
            x0 = xx[SUBLANES:, :]
            return cwv[3:4, :] + cwv[0:1, :] * x2 + cwv[1:2, :] * x1 + cwv[2:3, :] * x0, (x2, x1, x0)

        def back(d, cwv):
            d1 = jnp.where(pos < seq - 1, pltpu.roll(d, ext - 1, 0), 0.0)
            d2 = jnp.where(pos < seq - 2, pltpu.roll(d, ext - 2, 0), 0.0)
            return (cwv[2:3, :] * d + cwv[1:2, :] * d1 + cwv[0:1, :] * d2)[:tm, :].astype(BF16)

        cu, cg = cu_ref[...], cg_ref[...]
        up, u_taps = conv(u_ref, uh_ref, un_ref, cu)
        gate, g_taps = conv(g_ref, gh_ref, gn_ref, cg)
        da = jnp.concatenate([da_ref[...], dn_ref[...]], axis=0)
        cdf = 0.5 * (1.0 + lax.erf(gate * (2.0 ** -0.5)))
        d_up = da * (gate * cdf)
        d_gate = da * up * (cdf + gate * jnp.exp(-0.5 * gate * gate) * (1.0 / math.sqrt(2.0 * math.pi)))
        d_ref[0] = back(d_up, cu)
        d_ref[1] = back(d_gate, cg)

        @pl.when(pl.program_id(1) == 0)
        def _():
            dcu_ref[...] = jnp.zeros_like(dcu_ref)
            dcg_ref[...] = jnp.zeros_like(dcg_ref)

        for ref, d, taps in ((dcu_ref, d_up, u_taps), (dcg_ref, d_gate, g_taps)):
            own = d[:tm, :]
            for j in range(3):
                ref[j:j + 1, :] += jnp.sum(own * taps[j][:tm, :], axis=0, keepdims=True)
            ref[3:4, :] += jnp.sum(own, axis=0, keepdims=True)

    gate_off = w // tn
    specs = _ffn_specs(tm, tn, gate_off)
    tile, cwt = specs[0], specs[4]
    nxt = lambda i: jnp.minimum((i + 1) * (tm // SUBLANES), last)
    up_n = pl.BlockSpec((SUBLANES, tn), lambda j, i: (nxt(i), j))
    gate_n = pl.BlockSpec((SUBLANES, tn), lambda j, i: (nxt(i), j + gate_off))
    out, carried = _pallas(
        body, name=name, grid=(w // tn, t // tm), in_specs=list(specs) + [up_n, gate_n, tile, up_n],
        out_specs=[pl.BlockSpec((2, tm, tn), lambda j, i: (0, i, j)), cwt, cwt],
        out_shape=[jax.ShapeDtypeStruct((2, t, w), BF16), jax.ShapeDtypeStruct((SUBLANES, w), F32),
                   jax.ShapeDtypeStruct((SUBLANES, w), F32)],
        args=[hf, hf, hf, hf, cw, cw, hf, hf, dact, dact], sem=("parallel", "arbitrary"), comm=comm)
    return out if comm is None else (out, carried)


def _adamw(w, g, m, v, *, name, rows=256):
    nl, r, c = w.shape
    tr = _tile(r, rows, SUBLANES)

    def body(w_ref, g_ref, m_ref, v_ref, d_ref, nm_ref, nv_ref):
        g_ = g_ref[...]
        nm = ADAM_B1 * m_ref[...] + (1.0 - ADAM_B1) * g_
        nv = ADAM_B2 * v_ref[...] + (1.0 - ADAM_B2) * (g_ * g_)
        m_hat = nm / (1.0 - ADAM_B1 ** ADAM_STEP)
        v_hat = nv / (1.0 - ADAM_B2 ** ADAM_STEP)
        d_ref[...] = -ADAM_LR * (m_hat / (jnp.sqrt(v_hat) + ADAM_EPS) + ADAM_WD * w_ref[...])
        nm_ref[...] = nm
        nv_ref[...] = nv

    spec = pl.BlockSpec((None, tr, c), lambda l, i: (l, i, 0))
    shape = jax.ShapeDtypeStruct((nl, r, c), F32)
    return pl.pallas_call(body, name=name, grid=(nl, r // tr), in_specs=[spec] * 4, out_specs=[spec] * 3,
                          out_shape=[shape] * 3, compiler_params=_params(("parallel", "parallel")))(w, g, m, v)


def _mesh_pos():
    return lax.axis_index("x"), lax.axis_index("y"), lax.axis_index("c")


def _peers(x, y):
    chips = [(1 - x, y), (x, 1 - y), (1 - x, 1 - y)]
    return [(px, py, 2 * px + py) for px, py in chips]


def _remote(src, dst, send_sems, recv_sems, idx, to):
    return pltpu.make_async_remote_copy(src, dst, send_sems.at[idx], recv_sems.at[idx], device_id=to,
                                        device_id_type=MESH)


class _Comm:
    def __init__(self, operands, out_shape, aliases, sems, copies):
        self.operands, self.out_shape, self.aliases, self.sems, self.copies = operands, out_shape, aliases, sems, copies

    def start(self, ins, outs, sems):
        for send, _ in self.copies(ins, outs, sems):
            send.start()

    def wait(self, ins, outs, sems):
        pairs = self.copies(ins, outs, sems)
        for _, recv in pairs:
            recv.wait_recv()
        for send, _ in pairs:
            send.wait_send()


def _pallas(body, *, name, grid, in_specs, out_specs, out_shape, args, aliases=None, scratch=(), sem, comm=None):
    n_in, n_out = len(in_specs), len(out_specs)
    aliases = dict(aliases or {})
    if comm is None:
        out = pl.pallas_call(body, name=name, grid=grid, in_specs=in_specs, out_specs=out_specs, out_shape=out_shape,
                             input_output_aliases=aliases, scratch_shapes=list(scratch),
                             compiler_params=_params(sem))(*args)
        return list(out), []
    nci, nco, ncs = len(comm.operands), len(comm.out_shape), len(comm.sems)

    def carried(*refs):
        ins, cin = refs[:n_in], refs[n_in:n_in + nci]
        o0 = n_in + nci
        outs, cout = refs[o0:o0 + n_out], refs[o0 + n_out:o0 + n_out + nco]
        s0 = o0 + n_out + nco
        own, csem = refs[s0:len(refs) - ncs], refs[len(refs) - ncs:]
        ids = [pl.program_id(ax) for ax in range(len(grid))]
        first, last = ids[0] == 0, ids[0] == grid[0] - 1
        for i, g in zip(ids[1:], grid[1:]):
            first, last = jnp.logical_and(first, i == 0), jnp.logical_and(last, i == g - 1)

        @pl.when(first)
        def _():
            comm.start(cin, cout, csem)

        body(*ins, *outs, *own)

        @pl.when(last)
        def _():
            comm.wait(cin, cout, csem)

    aliases.update({n_in + i: n_out + j for i, j in comm.aliases.items()})
    out = pl.pallas_call(
        carried, name=name, grid=grid, in_specs=list(in_specs) + [ANY] * nci, out_specs=list(out_specs) + [ANY] * nco,
        out_shape=list(out_shape) + list(comm.out_shape), input_output_aliases=aliases,
        scratch_shapes=list(scratch) + list(comm.sems),
        compiler_params=_params(("arbitrary",) * len(grid)))(*args, *comm.operands)
    return list(out[:n_out]), list(out[n_out:])


def _run_comm(comm, *, name):
    nci, nco = len(comm.operands), len(comm.out_shape)

    def body(*refs):
        ins, outs, sems = refs[:nci], refs[nci:nci + nco], refs[nci + nco:]
        comm.start(ins, outs, sems)
        comm.wait(ins, outs, sems)

    return pl.pallas_call(body, name=name, in_specs=[ANY] * nci, out_specs=[ANY] * nco, out_shape=list(comm.out_shape),
                          input_output_aliases=dict(comm.aliases), scratch_shapes=list(comm.sems))(*comm.operands)


def _pair_sems(*shape):
    return [pltpu.SemaphoreType.DMA(shape), pltpu.SemaphoreType.DMA(shape)]


def _gather_comm(bufs, layer, stage):
    n = len(bufs)

    def copies(ins, outs, sems):
        x, y, c = _mesh_pos()
        me = 2 * x + y
        pairs = []
        for i in range(n):
            h = bufs[i].shape[2] // 2
            mine, other = pl.ds(c * h, h), pl.ds((1 - c) * h, h)
            for r, (px, py, k) in enumerate(_peers(x, y)):
                if stage == 0:
                    send = _remote(ins[i].at[me, layer, mine, :], outs[i].at[me, layer, mine, :], *sems, (i, r), (px, py, c))
                    land = outs[i].at[k, layer, mine, :]
                    recv = _remote(land, land, *sems, (i, r), (px, py, c))
                else:
                    send = _remote(ins[i].at[k, layer, mine, :], outs[i].at[k, layer, mine, :], *sems, (i, r), (x, y, 1 - c))
                    land = outs[i].at[k, layer, other, :]
                    recv = _remote(land, land, *sems, (i, r), (x, y, 1 - c))
                pairs.append((send, recv))
        return pairs

    return _Comm(bufs, [jax.ShapeDtypeStruct(b.shape, b.dtype) for b in bufs], {i: i for i in range(n)},
                 _pair_sems(n, 3), copies)


def _reduce_sibling_comm(gs):
    n = len(gs)

    def copies(ins, outs, sems):
        x, y, c = _mesh_pos()
        pairs = []
        for i in range(n):
            h = gs[i].shape[1] // 2
            cp = _remote(ins[i].at[:, pl.ds((1 - c) * h, h), :], outs[i], *sems, i, (x, y, 1 - c))
            pairs.append((cp, cp))
        return pairs

    return _Comm(gs, [jax.ShapeDtypeStruct((g.shape[0], g.shape[1] // 2, g.shape[2]), g.dtype) for g in gs], {},
                 _pair_sems(n), copies)


def _reduce_chips_comm(ps):
    n = len(ps)

    def copies(ins, outs, sems):
        x, y, c = _mesh_pos()
        pairs = []
        for i in range(n):
            for r, (px, py, k) in enumerate(_peers(x, y)):
                cp = _remote(ins[i].at[k], outs[i].at[r], *sems, (i, r), (px, py, c))
                pairs.append((cp, cp))
        return pairs

    return _Comm(ps, [jax.ShapeDtypeStruct((3,) + p.shape[1:], p.dtype) for p in ps], {}, _pair_sems(n, 3), copies)


def _share_halves(bufs, *, name):
    n = len(bufs)

    def body(*refs):
        ins, outs = refs[:n], refs[n:2 * n]
        send_sems, recv_sems = refs[2 * n:]
        x, y, c = _mesh_pos()
        cps = []
        for i in range(n):
            h = bufs[i].shape[1] // 2
            mine = pl.ds(c * h, h)
            cp = _remote(ins[i].at[:, mine, :], outs[i].at[:, mine, :], send_sems, recv_sems, i, (x, y, 1 - c))
            cp.start()
            cps.append(cp)
        for cp in cps:
            cp.wait()

    return pl.pallas_call(
        body, name=name, in_specs=[ANY] * n, out_specs=[ANY] * n,
        out_shape=[jax.ShapeDtypeStruct(b.shape, b.dtype) for b in bufs],
        input_output_aliases={i: i for i in range(n)},
        scratch_shapes=[pltpu.SemaphoreType.DMA((n,)), pltpu.SemaphoreType.DMA((n,))])(*bufs)


def _add_own_half(full, recv, pos, *, name, rows=256):
    k4, h, n = recv.shape
    tr = _tile(h, rows, 16)
    nblk = h // tr

    def body(pos_ref, a_ref, b_ref, o_ref):
        o_ref[...] = (a_ref[...] + b_ref[...]).astype(BF16)

    grid_spec = pltpu.PrefetchScalarGridSpec(
        num_scalar_prefetch=1, grid=(k4, nblk),
        in_specs=[pl.BlockSpec((None, tr, n), lambda k, i, pos_ref: (k, pos_ref[1] * nblk + i, 0)),
                  pl.BlockSpec((None, tr, n), lambda k, i, pos_ref: (k, i, 0))],
        out_specs=pl.BlockSpec((None, tr, n), lambda k, i, pos_ref: (k, i, 0)))
    return pl.pallas_call(body, name=name, grid_spec=grid_spec, out_shape=jax.ShapeDtypeStruct(recv.shape, BF16),
                          compiler_params=_params(("parallel", "parallel")))(pos, full, recv)


def _sum_into(own, others, buf, pos, layer, *, name, rows=256):
    _, h, n = own.shape
    tr = _tile(h, rows, 16)
    nblk = h // tr

    def body(pos_ref, own_ref, oth_ref, _, o_ref):
        acc = own_ref[...].astype(F32)
        for r in range(3):
            acc = acc + oth_ref[r].astype(F32)
        o_ref[...] = acc

    grid_spec = pltpu.PrefetchScalarGridSpec(
        num_scalar_prefetch=1, grid=(nblk,),
        in_specs=[pl.BlockSpec((None, tr, n), lambda i, pos_ref: (pos_ref[0], i, 0)),
                  pl.BlockSpec((3, tr, n), lambda i, pos_ref: (0, i, 0)), ANY],
        out_specs=pl.BlockSpec((None, tr, n), lambda i, pos_ref: (layer, pos_ref[1] * nblk + i, 0)))
    return pl.pallas_call(body, name=name, grid_spec=grid_spec, out_shape=jax.ShapeDtypeStruct(buf.shape, F32),
                          input_output_aliases={3: 0}, compiler_params=_params(("parallel",)))(pos, own, others, buf)


def _sibling_pair(buf, *, name):
    def body(src_ref, out_ref, send_sem, recv_sem, local_sem):
        x, y, c = _mesh_pos()
        local = pltpu.make_async_copy(src_ref, out_ref.at[c], local_sem)
        local.start()
        cp = pltpu.make_async_remote_copy(src_ref, out_ref.at[c], send_sem, recv_sem, device_id=(x, y, 1 - c),
                                          device_id_type=MESH)
        cp.start()
        cp.wait()
        local.wait()

    return pl.pallas_call(
        body, name=name, in_specs=[ANY], out_specs=ANY, out_shape=jax.ShapeDtypeStruct((2,) + buf.shape, buf.dtype),
        scratch_shapes=[pltpu.SemaphoreType.DMA, pltpu.SemaphoreType.DMA, pltpu.SemaphoreType.DMA])(buf)


def _chip_bcast(buf, *, name):
    def body(src_ref, out_ref, send_sems, recv_sems, local_sem):
        x, y, c = _mesh_pos()
        me = 2 * x + y
        local = pltpu.make_async_copy(src_ref, out_ref.at[me], local_sem)
        local.start()
        sends = []
        for r, (px, py, _) in enumerate(_peers(x, y)):
            cp = _remote(src_ref, out_ref.at[me], send_sems, recv_sems, r, (px, py, c))
            cp.start()
            sends.append(cp)
        for r, (px, py, k) in enumerate(_peers(x, y)):
            _remote(src_ref, out_ref.at[k], send_sems, recv_sems, r, (px, py, c)).wait_recv()
        for cp in sends:
            cp.wait_send()
        local.wait()

    return pl.pallas_call(
        body, name=name, in_specs=[ANY], out_specs=ANY, out_shape=jax.ShapeDtypeStruct((4,) + buf.shape, buf.dtype),
        scratch_shapes=[pltpu.SemaphoreType.DMA((3,)), pltpu.SemaphoreType.DMA((3,)), pltpu.SemaphoreType.DMA])(buf)


def _sum_slots(buf, *, name, rows=384):
    r, n = buf.shape[-2:]
    k = int(np.prod(buf.shape[:-2]))
    tr = _tile(r, rows, SUBLANES)

    def body(b_ref, o_ref):
        acc = b_ref[0]
        for s in range(1, k):
            acc = acc + b_ref[s]
        o_ref[...] = acc

    return pl.pallas_call(
        body, name=name, grid=(r // tr,), in_specs=[pl.BlockSpec((k, tr, n), lambda i: (0, i, 0))],
        out_specs=pl.BlockSpec((tr, n), lambda i: (i, 0)), out_shape=jax.ShapeDtypeStruct((r, n), F32),
        compiler_params=_params(("parallel",)))(buf.reshape((k, r, n)))


ROW = 1024
BIG = (("w_in", 2), ("w_out", 1), ("w_cq", 1), ("w_ck", 1), ("w_cv", 1), ("w_co", 2), ("w_up", 2), ("w_down", 1))
CONV = ("lru_conv_w", "ffn_conv_w")
REPLICATED = ("norm_mix_g", "b_forget", "lru_conv_b", "lru_w_a", "lru_b_a", "lru_w_x", "lru_b_x", "lru_lambda",
              "norm_cross_g", "norm_mem_g", "norm_ffn_g", "ffn_conv_b", "rel_bias", "final_norm_g")
WEIGHTS = ('norm_mix_g', 'w_in', 'b_forget', 'lru_conv_w', 'lru_conv_b', 'lru_w_a', 'lru_b_a', 'lru_w_x', 'lru_b_x',
           'lru_lambda', 'w_out', 'norm_cross_g', 'norm_mem_g', 'w_cq', 'w_ck', 'w_cv', 'w_co', 'norm_ffn_g', 'w_up',
           'ffn_conv_w', 'ffn_conv_b', 'w_down', 'rel_bias', 'final_norm_g')
INPUTS = ("x", "mem") + WEIGHTS + ("loss_target",) + tuple("m_" + n for n in WEIGHTS) + tuple("v_" + n for n in WEIGHTS)


def _round_up(n, m):
    return -(-n // m) * m


class _Packing:
    def __init__(self, entries):
        self.entries, self.off = entries, {}
        o = 0
        for name, shape in entries:
            self.off[name] = o
            o += _round_up(int(np.prod(shape)), ROW)
        self.used = o
        self.rows = _round_up(o // ROW, SUBLANES)

    def pack(self, arrays):
        parts = []
        for name, shape in self.entries:
            n = int(np.prod(shape))
            parts.append(jnp.pad(arrays[name].reshape(n), (0, _round_up(n, ROW) - n)))
        tail = self.rows * ROW - self.used
        if tail:
            parts.append(jnp.zeros((tail,), F32))
        return jnp.concatenate(parts).reshape(self.rows, ROW)

    def unpack(self, flat, lead=()):
        flat = flat.reshape(lead + (self.rows * ROW,))
        out = {}
        for name, shape in self.entries:
            n = int(np.prod(shape))
            out[name] = lax.slice_in_dim(flat, self.off[name], self.off[name] + n, axis=len(lead)).reshape(
                lead + tuple(shape))
        return out


def _to_shards(g, axis):
    r, c = g.shape
    if axis == 1:
        return g.reshape(4, r // 4, c)
    return g.reshape(r, 4, c // 4).transpose(1, 0, 2)


def _from_shards(s, axis):
    _, nl, r, c = s.shape
    if axis == 1:
        return s.transpose(1, 0, 2, 3).reshape(nl, 4 * r, c)
    return s.transpose(1, 2, 0, 3).reshape(nl, r, 4 * c)


def _proj_blocks():
    blocks = []
    for mixer in (MIX_SB, MIX_FOX, MIX_DIL):
        for p in range(2):
            blocks += [ORIG_COL[mixer] + part * 2 * LANES + p * LANES for part in range(3)]
    for hf in range(2):
        blocks += [ORIG_LRU_X + hf * LANES, ORIG_LRU_G + hf * LANES]
    return blocks


def _pad_w_in(w):
    parts = [w[..., s:s + LANES] for s in _proj_blocks()]
    parts += [w[..., 1536:1540], jnp.zeros(w.shape[:-1] + (PROJ_W - COL_F - N_HEADS,), w.dtype)]
    return jnp.concatenate(parts, axis=-1)


def _unpad_w_in(wp):
    blocks = _proj_blocks()
    order = sorted(range(len(blocks)), key=lambda i: blocks[i])
    parts = []
    for i in order:
        if blocks[i] == ORIG_COL[MIX_DIL]:
            parts.append(wp[..., COL_F:COL_F + N_HEADS])
        parts.append(wp[..., i * LANES:(i + 1) * LANES])
    return jnp.concatenate(parts, axis=-1)


def _block_diag(w):
    z = jnp.zeros((HEAD_DIM, HEAD_DIM), w.dtype)
    half = lambda a, b: jnp.concatenate([jnp.concatenate([a, z], 1), jnp.concatenate([z, b], 1)], 0)
    return jnp.stack([half(w[0], w[1]), half(w[2], w[3])])


def _block_diag_grad(d):
    return jnp.stack([d[0, :HEAD_DIM, :HEAD_DIM], d[0, HEAD_DIM:, HEAD_DIM:],
                      d[1, :HEAD_DIM, :HEAD_DIM], d[1, HEAD_DIM:, HEAD_DIM:]])


def _fox_layouts(cum, nb, seq):
    tk = min(ATT_TILE, seq)
    col = cum.reshape(nb, 2, 2, seq).transpose(0, 1, 3, 2)
    row = cum.reshape(nb, 2, 2, seq // tk, tk).transpose(0, 1, 3, 2, 4)
    return col, row


def _layer_params(w, l, nb):
    lru_vec = jnp.concatenate([w["lru_conv_w"][l], w["lru_conv_b"][l][None], w["lru_b_a"][l][None],
                               w["lru_b_x"][l][None], w["lru_lambda"][l][None]], axis=0)
    ffn_cw = jnp.concatenate([w["ffn_conv_w"][l], w["ffn_conv_b"][l][None],
                              jnp.zeros((SUBLANES - 4, 2 * D_FF), F32)], axis=0)
    return dict(
        w_in=w["w_in_padded"][l], lru_vec=lru_vec,
        wa=_block_diag(w["lru_w_a"][l]).astype(BF16), wx=_block_diag(w["lru_w_x"][l]).astype(BF16),
        ffn_cw=ffn_cw, b_rows=jnp.tile(w["b_forget"][l], nb).reshape(nb * N_HEADS, 1))


NORM_ROWS = 512


def _merge_comms(comms):
    if len(comms) == 1:
        return comms[0]
    operands, out_shape, aliases, sems, spans = [], [], {}, [], []
    for cm in comms:
        aliases.update({len(operands) + i: len(out_shape) + j for i, j in cm.aliases.items()})
        spans.append((len(operands), len(out_shape), len(sems)))
        operands += list(cm.operands)
        out_shape += list(cm.out_shape)
        sems += list(cm.sems)

    def copies(ins, outs, sm):
        pairs = []
        for cm, (i0, o0, s0) in zip(comms, spans):
            pairs += cm.copies(ins[i0:i0 + len(cm.operands)], outs[o0:o0 + len(cm.out_shape)], sm[s0:s0 + len(cm.sems)])
        return pairs

    return _Comm(operands, out_shape, aliases, sems, copies)


GATHER_FIRST = ("w_in",)
GATHER_MID = ("w_out", "w_cq", "w_ck", "w_cv", "w_co")
GATHER_LAST = ("w_up", "w_down")


class _WeightGather:
    def __init__(self, slots, w, depth):
        self.slots, self.w, self.depth = slots, w, depth

    def plan(self, l, key):
        nxt = l + 1 if l + 1 < self.depth else None
        early = GATHER_FIRST + GATHER_MID
        if l == 0:
            table = {"proj": [(GATHER_MID, 0, 0)],
                     "sb_fwd": [(GATHER_MID, 0, 1), (GATHER_LAST, 0, 0)],
                     "fox_fwd": [(GATHER_LAST, 0, 1)] + ([(early, nxt, 0)] if nxt else []),
                     "dil_fwd": [(early, nxt, 1), (GATHER_LAST, nxt, 0)] if nxt else [],
                     "out": [(GATHER_LAST, nxt, 1)] if nxt else []}
        else:
            everything = early + GATHER_LAST
            table = {"sb_fwd": [(everything, nxt, 0)], "fox_fwd": [(everything, nxt, 1)]} if nxt else {}
        return table.get(key, [])

    def comm(self, l, key):
        entries = self.plan(l, key)
        if not entries:
            return None
        return _merge_comms([_gather_comm([self.slots[n] for n in names], layer, stage)
                             for names, layer, stage in entries])

    def done(self, l, key, landed):
        landed = list(landed)
        for names, layer, stage in self.plan(l, key):
            for n in names:
                self.slots[n] = landed.pop(0)
            if stage == 1:
                self.take(names, layer)

    def take(self, names, layer):
        for n, axis in BIG:
            if n in names:
                self.w[n][layer] = _from_shards(self.slots[n][:, layer:layer + 1], axis)[0]
        if "w_in" in names:
            self.w["w_in_padded"][layer] = _pad_w_in(self.w["w_in"][layer])

    def first(self):
        for stage in (0, 1):
            got = _run_comm(_gather_comm([self.slots[n] for n in GATHER_FIRST], 0, stage), name=f"gather_first_{stage}")
            self.slots.update(zip(GATHER_FIRST, got))
        self.take(GATHER_FIRST, 0)


def _layer_fwd(x, h, mem, w, lp, l, next_g, bias, nb, gather):
    t, d = x.shape
    seq = t // nb
    tag = f"l{l}"
    sv = dict(x0=x)

    def carrying(key, fn):
        comm = gather.comm(l, key)
        res = fn(comm)
        if comm is not None:
            res, landed = res
            gather.done(l, key, landed)
        return res

    proj = carrying("proj", lambda cm: _mm(h, lp["w_in"], name=tag + "_proj", comm=cm))
    mixed, ltot = carrying("sb_fwd", lambda cm: _sb_attn_fwd(proj, lax.empty((t, d), F32), nb=nb,
                                                             name=tag + "_sb_fwd", comm=cm))
    f_rows = proj[:, COL_F:COL_F + N_HEADS].reshape(nb, seq, N_HEADS).transpose(0, 2, 1).reshape(nb * N_HEADS, seq)
    cum_col, cum_row = _fox_layouts(_fox_gate_fwd(f_rows, lp["b_rows"]), nb, seq)
    mixed, lse_fox = carrying("fox_fwd", lambda cm: _softmax_attn_fwd(
        proj, nb=nb, mode="fox", mixer=MIX_FOX, out_buf=mixed, extra=(cum_col, cum_row), name=tag + "_fox_fwd", comm=cm))
    mixed, lse_dil = carrying("dil_fwd", lambda cm: _softmax_attn_fwd(
        proj, nb=nb, mode="dil", mixer=MIX_DIL, out_buf=mixed, extra=(bias,), name=tag + "_dil_fwd", comm=cm))
    mixed = _lru_fwd(proj, lp["lru_vec"], lp["wa"], lp["wx"], mixed, nb=nb, name=tag + "_lru_fwd")
    x1, hq = carrying("out", lambda cm: _mm(mixed, w["w_out"][l], res=x, norm_g=w["norm_cross_g"][l], ti=NORM_ROWS,
                                            name=tag + "_out", comm=cm))
    memn = _rmsnorm(mem, w["norm_mem_g"][l], name=tag + "_norm_mem")
    q = _mm(hq, w["w_cq"][l], name=tag + "_cq")
    k = _mm(memn, w["w_ck"][l], name=tag + "_ck")
    v = _mm(memn, w["w_cv"][l], name=tag + "_cv")
    oc, lse_c = _softmax_attn_fwd((q, k, v), nb=nb, mode="cross", name=tag + "_cross_fwd")
    x2, hn = _mm(oc, w["w_co"][l], res=x1, norm_g=w["norm_ffn_g"][l], ti=NORM_ROWS, name=tag + "_co")
    hf = _mm(hn, w["w_up"][l], name=tag + "_up")
    act = _ffn_act(hf, lp["ffn_cw"], seq=seq, name=tag + "_ffn_act")
    if next_g is None:
        x3, h_next = _mm(act, w["w_down"][l], res=x2, name=tag + "_down"), None
    else:
        x3, h_next = _mm(act, w["w_down"][l], res=x2, norm_g=next_g, ti=NORM_ROWS, name=tag + "_down")
    sv.update(h=h, proj=proj, ltot=ltot, f_rows=f_rows, cum_col=cum_col, cum_row=cum_row, lse_fox=lse_fox,
              lse_dil=lse_dil, mixed=mixed, x1=x1, hq=hq, memn=memn, q=q, k=k, v=v, oc=oc, lse_c=lse_c, x2=x2,
              hn=hn, hf=hf, act=act)
    return x3, h_next, sv


class _PendingReduce:
    def __init__(self, full, pos, layer):
        self.names, self.full, self.pos, self.layer = list(full), list(full.values()), pos, layer

    def sibling_comm(self):
        return _reduce_sibling_comm(self.full)

    def add(self, from_sibling):
        self.partial = [_add_own_half(f, r, self.pos, name=f"l{self.layer}_reduce_add_{n}")
                        for f, r, n in zip(self.full, from_sibling, self.names)]

    def chips_comm(self):
        return _reduce_chips_comm(self.partial)

    def finish(self, others, g_shard):
        g_shard = dict(g_shard)
        for p, o, n in zip(self.partial, others, self.names):
            g_shard[n] = _sum_into(p, o, g_shard[n], self.pos, self.layer, name=f"l{self.layer}_reduce_sum_{n}")
        return g_shard


def _layer_bwd(dx3, mem, sv, w, lp, l, bias, nb, pos, pending=None, g_shard=None, reduce_early=False):
    t = dx3.shape[0]
    seq = t // nb
    tag = f"l{l}"
    g = {}
    down_rows = _tile(sv["act"].shape[1], 1408)
    if pending is None:
        g["w_down"] = _mm(sv["act"], dx3, ta=True, ti=down_rows, name=tag + "_dw_down")
    else:
        g["w_down"], from_sibling = _mm(sv["act"], dx3, ta=True, ti=down_rows, comm=pending.sibling_comm(),
                                        name=tag + "_dw_down")
        pending.add(from_sibling)
    dact = _mm(dx3, w["w_down"][l], tb=True, name=tag + "_dact")
    if pending is None:
        dhf, dcu, dcg = _ffn_bwd(sv["hf"], lp["ffn_cw"], dact, seq=seq, name=tag + "_ffn_bwd")
    else:
        (dhf, dcu, dcg), others = _ffn_bwd(sv["hf"], lp["ffn_cw"], dact, seq=seq, name=tag + "_ffn_bwd",
                                           comm=pending.chips_comm())
        g_shard = pending.finish(others, g_shard)
    dcw = jnp.concatenate([dcu, dcg], axis=1)
    g["ffn_conv_w"], g["ffn_conv_b"] = dcw[:3], dcw[3]
    g["w_up"] = _mm(sv["hn"], dhf, ta=True, halves="b", col_shards=4, name=tag + "_dw_up")
    early = _PendingReduce(_big_grad_shards(g, EARLY), pos, l) if reduce_early else None
    res = _mm(dhf, w["w_up"][l], tb=True, halves="a", norm_bwd=(sv["x2"], w["norm_ffn_g"][l], dx3), ti=NORM_ROWS,
              comm=early.sibling_comm() if early else None, name=tag + "_dhn")
    if early:
        res, from_sibling = res
        early.add(from_sibling)
    dx2, dg = res
    g["norm_ffn_g"] = dg.reshape(-1)
    g["w_co"] = _mm(sv["oc"], dx2, ta=True, col_shards=4, name=tag + "_dw_co")
    doc = _mm(dx2, w["w_co"][l], tb=True, name=tag + "_doc")
    dq, dk, dv = _softmax_attn_bwd((sv["q"], sv["k"], sv["v"]), sv["oc"], sv["lse_c"], doc, nb=nb, mode="cross",
                                   name=tag + "_cross_bwd")
    g["w_cq"] = _mm(sv["hq"], dq, ta=True, name=tag + "_dw_cq")
    g["w_ck"] = _mm(sv["memn"], dk, ta=True, name=tag + "_dw_ck")
    g["w_cv"] = _mm(sv["memn"], dv, ta=True, name=tag + "_dw_cv")
    dx1, dg = _mm(dq, w["w_cq"][l], tb=True, norm_bwd=(sv["x1"], w["norm_cross_g"][l], dx2), ti=NORM_ROWS,
                  name=tag + "_dhq")
    g["norm_cross_g"] = dg.reshape(-1)
    dmemn = _mm(dv, w["w_cv"][l], tb=True, res=_mm(dk, w["w_ck"][l], tb=True, name=tag + "_dmem_k"),
                name=tag + "_dmem_v")
    _, g["norm_mem_g"] = _rmsnorm_bwd(dmemn, mem, w["norm_mem_g"][l], None, name=tag + "_norm_mem_bwd")
    mixed, proj = sv["mixed"], sv["proj"]
    g["w_out"] = _mm(mixed, dx1, ta=True, name=tag + "_dw_out")
    dmixed = _mm(dx1, w["w_out"][l], tb=True, name=tag + "_dmixed")
    dproj = _sb_attn_bwd(proj, sv["ltot"], dmixed, lax.empty((t, PROJ_W), F32), nb=nb, name=tag + "_sb_bwd",
                         comm=early.chips_comm() if early else None)
    if early:
        dproj, others = dproj
        g_shard = early.finish(others, g_shard)
    dproj, dcum_k, dcum_q = _softmax_attn_bwd(
        proj, mixed, sv["lse_fox"], dmixed, nb=nb, mode="fox", mixer=MIX_FOX, dbuf=dproj,
        extra=(sv["cum_col"], sv["cum_row"]), name=tag + "_fox_bwd")
    dcum = (dcum_k.transpose(0, 1, 3, 2, 4).reshape(nb * N_HEADS, seq)
            + dcum_q.transpose(0, 1, 3, 2).reshape(nb * N_HEADS, seq))
    df_rows, db = _fox_gate_bwd(dcum, sv["f_rows"], lp["b_rows"])
    g["b_forget"] = db[:N_HEADS, 0]
    df = df_rows.reshape(nb, N_HEADS, seq).transpose(0, 2, 1).reshape(t, N_HEADS)
    dproj, dbias = _softmax_attn_bwd(proj, mixed, sv["lse_dil"], dmixed, nb=nb, mode="dil", mixer=MIX_DIL,
                                     dbuf=dproj, extra=(bias,), name=tag + "_dil_bwd")
    dproj, dvec, dwa, dwx = _lru_bwd(proj, lp["lru_vec"], lp["wa"], lp["wx"], dmixed, dproj, nb=nb,
                                     name=tag + "_lru_bwd")
    g["lru_conv_w"], g["lru_conv_b"], g["lru_b_a"], g["lru_b_x"], g["lru_lambda"] = (
        dvec[0:4], dvec[4], dvec[5], dvec[6], dvec[7])
    g["lru_w_a"], g["lru_w_x"] = _block_diag_grad(dwa), _block_diag_grad(dwx)
    dproj = lax.dynamic_update_slice(dproj, jnp.pad(df, ((0, 0), (0, PROJ_W - COL_F - N_HEADS))), (0, COL_F))
    g["w_in_padded"] = _mm(sv["h"], dproj, ta=True, name=tag + "_dw_in")
    dx0, dg = _mm(dproj, lp["w_in"], tb=True, norm_bwd=(sv["x0"], w["norm_mix_g"][l], dx1), ti=NORM_ROWS,
                  name=tag + "_dh")
    g["norm_mix_g"] = dg.reshape(-1)
    return dx0, g, dbias, g_shard


def _big_grad_shards(g, names):
    out = {}
    for n, axis in BIG:
        if n not in names:
            continue
        if n in ("w_up", "w_co"):
            out[n] = g[n]
        else:
            out[n] = _to_shards(_unpad_w_in(g["w_in_padded"]) if n == "w_in" else g[n], axis)
    return out


EARLY = ("w_down", "w_up")


def kernel(*args):
    a = dict(zip(INPUTS, args, strict=True))
    nb, seq, d = a["x"].shape
    depth = a["norm_mix_g"].shape[0]
    x = a["x"].reshape(nb * seq, d)
    mem = a["mem"].reshape(nb * a["mem"].shape[1], d)
    target = a["loss_target"].reshape(nb * seq, d)
    cx, cy, c = _mesh_pos()
    chip = 2 * cx + cy
    pos = jnp.stack([chip, c]).astype(jnp.int32)

    slots = {}
    for n, _ in BIG:
        own = a[n].astype(BF16)[None]
        slots[n] = lax.dynamic_update_slice(lax.empty((4,) + own.shape[1:], BF16), own, (chip,) + (0,) * (own.ndim - 1))
    w = {n: a[n] for n in REPLICATED}
    w.update({n: {} for n, _ in BIG}, w_in_padded={})
    gather = _WeightGather(slots, w, depth)
    gather.first()
    cpk = _Packing([(n, a[n].shape) for n in CONV])
    conv = cpk.unpack(_chip_bcast(cpk.pack({n: a[n] for n in CONV}), name="gather_conv"), lead=(4,))
    for n in CONV:
        w[n] = jnp.moveaxis(conv[n], 0, 2).reshape(a[n].shape[:2] + (4 * a[n].shape[2],))

    bias = _dil_bias(w["rel_bias"], seq)
    lps, saved = [], []
    h = _rmsnorm(x, w["norm_mix_g"][0], name="l0_norm_mix")
    for l in range(depth):
        lps.append(_layer_params(w, l, nb))
        x, h, sv = _layer_fwd(x, h, mem, w, lps[l], l, w["norm_mix_g"][l + 1] if l + 1 < depth else None, bias, nb,
                              gather)
        saved.append(sv)
    loss, dx, dg_final = _loss_head(x, w["final_norm_g"], target)
    small_g = [None] * depth
    dbias, pending = None, None
    g_shard = {n: lax.empty(a[n].shape, F32) for n, _ in BIG}
    for l in reversed(range(depth)):
        bottom = l == 0
        dx, g, db, g_shard = _layer_bwd(dx, mem, saved[l], w, lps[l], l, bias, nb, pos, pending=pending,
                                        g_shard=g_shard, reduce_early=bottom)
        dbias = db if dbias is None else dbias + db
        small_g[l] = g
        left = [n for n, _ in BIG if not (bottom and n in EARLY)]
        pending = _PendingReduce(_big_grad_shards(g, left), pos, l)
    pending.add(_run_comm(pending.sibling_comm(), name="reduce_sibling"))
    g_shard = pending.finish(_run_comm(pending.chips_comm(), name="reduce_chips"), g_shard)
    names = [n for n, _ in BIG]
    g_shard = dict(zip(names, _share_halves([g_shard[n] for n in names], name="reduce_share")))
    out = {}
    for n in names:
        delta, new_m, new_v = _adamw(a[n], g_shard[n], a["m_" + n], a["v_" + n], name="adamw_" + n)
        out[n] = (g_shard[n], delta, new_m, new_v)

    grads = {n: jnp.stack([small_g[l][n] for l in range(depth)]) for n in REPLICATED + CONV
             if n not in ("rel_bias", "final_norm_g")}
    grads["rel_bias"] = _dil_bias_bwd(dbias, seq)
    grads["final_norm_g"] = dg_final
    grads["loss"] = loss.reshape(1)
    spk = _Packing([(n, grads[n].shape) for n in REPLICATED + CONV + ("loss",)])
    s_all = _chip_bcast(_sibling_pair(spk.pack(grads), name="small_sibling"), name="small_chips")
    total = spk.unpack(_sum_slots(s_all, name="small_sum"))
    for n in CONV:
        width = a[n].shape[2]
        total[n] = lax.dynamic_slice_in_dim(total[n], chip * width, width, axis=2)
    apk = _Packing([(n, a[n].shape) for n in REPLICATED + CONV])
    s_out = _adamw(*[apk.pack(src)[None] for src in (
        {n: a[n] for n in REPLICATED + CONV}, total, {n: a["m_" + n] for n in REPLICATED + CONV},
        {n: a["v_" + n] for n in REPLICATED + CONV})], name="adamw_small")
    s_delta, s_m, s_v = [apk.unpack(o[0]) for o in s_out]
    for n in REPLICATED + CONV:
        out[n] = (total[n], s_delta[n], s_m[n], s_v[n])

    return (total["loss"].reshape(()), dx.reshape(nb, seq, d), *[out[n][0] for n in WEIGHTS],
            *[out[n][1] for n in WEIGHTS], *[out[n][2] for n in WEIGHTS], *[out[n][3] for n in WEIGHTS])
```

```python
import math

import numpy as np
import jax
import jax.numpy as jnp
from jax import lax
from jax.experimental import pallas as pl
from jax.experimental.pallas import tpu as pltpu

F32 = jnp.float32
BF16 = jnp.bfloat16

HEAD_DIM = 64
N_HEADS = 4
N_IN = 2820
D_FF = 2816
LRU_C = 8.0
EPS = 1e-6
NUM_BUCKETS = 32
MAX_DISTANCE = 2048
DILATED_PATTERNS = ((128, 1), (512, 4), (2048, 16))
ADAM_LR, ADAM_B1, ADAM_B2, ADAM_EPS, ADAM_WD, ADAM_STEP = 0.001, 0.9, 0.999, 1e-08, 0.01, 10

LANES = 128
SUBLANES = 8
VMEM_LIMIT = 48 * 1024 * 1024

PROJ_W = 3072
PAIR_W = 3 * LANES
LRU_W = 2 * LANES
COL_LRU = 6 * PAIR_W
COL_F = COL_LRU + 2 * LRU_W
MIX_SB, MIX_FOX, MIX_DIL, MIX_LRU = 0, 1, 2, 3
ORIG_COL = {MIX_SB: 0, MIX_FOX: 768, MIX_DIL: 1540}
ORIG_LRU_X, ORIG_LRU_G = 2308, 2564

ATT_TILE = 256
MASKED = -1e30
SCALE = HEAD_DIM ** -0.5

NT_DIMS = (((1,), (1,)), ((), ()))
TN_DIMS = (((0,), (0,)), ((), ()))

MESH = pl.DeviceIdType.MESH
ANY = pl.BlockSpec(memory_space=pl.ANY)


def _params(sem):
    return pltpu.CompilerParams(dimension_semantics=sem, vmem_limit_bytes=VMEM_LIMIT)


def _tile(n, target, unit=LANES):
    if n <= target:
        return n
    t = (target // unit) * unit
    while t > unit and n % t:
        t -= unit
    assert n % t == 0, (n, target, unit)
    return t


def _mm(a, b, *, ta=False, tb=False, res=None, col_shards=1, halves=None, norm_g=None, norm_bwd=None, comm=None,
        name, ti=1024, tj=1408, tc=1408):
    if halves == "a":
        m, kc = a.shape[1], 2 * a.shape[2]
    else:
        m, kc = (a.shape[1], a.shape[0]) if ta else a.shape
    if halves == "b":
        n = 2 * b.shape[2]
        assert b.shape[1] == kc
    else:
        n = b.shape[0] if tb else b.shape[1]
        assert (b.shape[1] if tb else b.shape[0]) == kc
    assert n % col_shards == 0
    n_blk = n // (2 if halves == "b" else col_shards)
    k_blk = kc // 2 if halves == "a" else kc
    ti, tj, tc = (_tile(m, ti, LANES if ta else SUBLANES), _tile(n_blk, tj),
                  _tile(k_blk, tc, SUBLANES if ta and tb else LANES))
    per_shard, per_half_j, per_half_k = n // col_shards // tj, n_blk // tj, k_blk // tc
    nk = kc // tc
    dims = (((0 if ta else 1,), (1 if tb else 0,)), ((), ()))
    rows_whole = norm_g is not None or norm_bwd is not None
    assert not rows_whole or (tj == n and col_shards == 1)
    n_extra = (res is not None) + (norm_g is not None) + (3 if norm_bwd is not None else 0)
    n_out = 2 if rows_whole else 1

    def finish(val, ex, outs):
        if res is not None:
            val = ex[0][...] + val
        if norm_g is not None:
            outs[0][...] = val
            outs[1][...] = (_xhat(val) * ex[-1][...]).astype(BF16)
        elif norm_bwd is not None:
            x_ref, g_ref, r_ref = ex[-3:]
            dx, dgr = _norm_bwd_rows(val, x_ref[...], g_ref[...])
            outs[0][...] = r_ref[...] + dx

            @pl.when(pl.program_id(0) == 0)
            def _():
                outs[1][...] = jnp.zeros_like(outs[1])

            outs[1][...] += jnp.sum(dgr, axis=0, keepdims=True)
        else:
            outs[0][...] = val

    def body(*refs):
        a_ref, b_ref = refs[:2]
        ex = refs[2:2 + n_extra]
        outs = refs[2 + n_extra:2 + n_extra + n_out]
        part = lax.dot_general(a_ref[...].astype(BF16), b_ref[...].astype(BF16), dims, preferred_element_type=F32)
        if nk == 1:
            finish(part, ex, outs)
            return
        acc_ref = refs[-1]
        k = pl.program_id(2)

        @pl.when(k == 0)
        def _():
            acc_ref[...] = part

        @pl.when(k > 0)
        def _():
            acc_ref[...] += part

        @pl.when(k == nk - 1)
        def _():
            finish(acc_ref[...], ex, outs)

    if halves == "a":
        a_spec = pl.BlockSpec((None, ti, tc), lambda i, j, k: (k // per_half_k, i, k % per_half_k))
    elif ta:
        a_spec = pl.BlockSpec((tc, ti), lambda i, j, k: (k, i))
    else:
        a_spec = pl.BlockSpec((ti, tc), lambda i, j, k: (i, k))
    if halves == "b":
        b_spec = pl.BlockSpec((None, tc, tj), lambda i, j, k: (j // per_half_j, k, j % per_half_j))
    elif tb:
        b_spec = pl.BlockSpec((tj, tc), lambda i, j, k: (j, k))
    else:
        b_spec = pl.BlockSpec((tc, tj), lambda i, j, k: (k, j))
    o_spec = pl.BlockSpec((ti, tj), lambda i, j, k: (i, j))
    vec = pl.BlockSpec((1, tj), lambda i, j, k: (0, 0))
    in_specs, args = [a_spec, b_spec], [a, b]
    out_specs, out_shape = [o_spec], [jax.ShapeDtypeStruct((m, n), F32)]
    if res is not None:
        in_specs.append(o_spec)
        args.append(res)
    if norm_g is not None:
        in_specs.append(vec)
        args.append(norm_g.reshape(1, n))
        out_specs.append(o_spec)
        out_shape.append(jax.ShapeDtypeStruct((m, n), BF16))
    if norm_bwd is not None:
        x, g, dres = norm_bwd
        in_specs += [o_spec, vec, o_spec]
        args += [x, g.reshape(1, n), dres]
        out_specs.append(vec)
        out_shape.append(jax.ShapeDtypeStruct((1, n), F32))
    if col_shards > 1:
        assert n_extra == 0
        out_specs = [pl.BlockSpec((None, ti, tj), lambda i, j, k: (j // per_shard, i, j % per_shard))]
        out_shape = [jax.ShapeDtypeStruct((col_shards, m, n // col_shards), F32)]
    sem = ("arbitrary",) * 3 if norm_bwd is not None else ("parallel", "parallel", "arbitrary")
    out, carried = _pallas(body, name=name, grid=(m // ti, n // tj, nk), in_specs=in_specs, out_specs=out_specs,
                           out_shape=out_shape, args=args, scratch=[] if nk == 1 else [pltpu.VMEM((ti, tj), F32)],
                           sem=sem, comm=comm)
    out = out if rows_whole else out[0]
    return out if comm is None else (out, carried)


def _xhat(x):
    return x * lax.rsqrt(jnp.mean(x * x, axis=-1, keepdims=True) + EPS)


def _norm_bwd_rows(dy, x, g):
    rstd = lax.rsqrt(jnp.mean(x * x, axis=-1, keepdims=True) + EPS)
    xh = x * rstd
    dxh = dy * g
    dx = rstd * (dxh - xh * jnp.mean(dxh * xh, axis=-1, keepdims=True))
    return dx, dy * xh


def _rmsnorm(x, g, *, name, rows=512):
    t, d = x.shape
    tr = _tile(t, rows, 2 * SUBLANES)

    def body(x_ref, g_ref, o_ref):
        o_ref[...] = (_xhat(x_ref[...]) * g_ref[...]).astype(BF16)

    return pl.pallas_call(
        body, name=name, grid=(t // tr,),
        in_specs=[pl.BlockSpec((tr, d), lambda i: (i, 0)), pl.BlockSpec((1, d), lambda i: (0, 0))],
        out_specs=pl.BlockSpec((tr, d), lambda i: (i, 0)), out_shape=jax.ShapeDtypeStruct((t, d), BF16),
        compiler_params=_params(("parallel",)))(x, g.reshape(1, d))


def _rmsnorm_bwd(dy, x, g, dres, *, name, rows=512):
    t, d = x.shape
    tr = _tile(t, rows, SUBLANES)

    def body(*refs):
        if dres is None:
            dy_ref, x_ref, g_ref, dx_ref, dg_ref = refs
        else:
            dy_ref, x_ref, g_ref, r_ref, dx_ref, dg_ref = refs
        dx, dgr = _norm_bwd_rows(dy_ref[...], x_ref[...], g_ref[...])
        dx_ref[...] = dx if dres is None else r_ref[...] + dx

        @pl.when(pl.program_id(0) == 0)
        def _():
            dg_ref[...] = jnp.zeros_like(dg_ref)

        dg_ref[...] += jnp.sum(dgr, axis=0, keepdims=True)

    row = pl.BlockSpec((tr, d), lambda i: (i, 0))
    vec = pl.BlockSpec((1, d), lambda i: (0, 0))
    in_specs = [row, row, vec] + ([] if dres is None else [row])
    args = (dy, x, g.reshape(1, d)) + (() if dres is None else (dres,))
    dx, dg = pl.pallas_call(
        body, name=name, grid=(t // tr,), in_specs=in_specs, out_specs=[row, vec],
        out_shape=[jax.ShapeDtypeStruct((t, d), F32), jax.ShapeDtypeStruct((1, d), F32)],
        compiler_params=_params(("arbitrary",)))(*args)
    return dx, dg.reshape(d)


def _loss_head(x, g, target, *, rows=512):
    t, d = x.shape
    tr = _tile(t, rows, SUBLANES)

    def body(x_ref, g_ref, t_ref, dx_ref, dg_ref, loss_ref):
        x_, g_ = x_ref[...], g_ref[...]
        err = _xhat(x_) * g_ - t_ref[...]
        dx, dgr = _norm_bwd_rows(err * (1.0 / d), x_, g_)
        dx_ref[...] = dx

        @pl.when(pl.program_id(0) == 0)
        def _():
            dg_ref[...] = jnp.zeros_like(dg_ref)
            loss_ref[...] = jnp.zeros_like(loss_ref)

        dg_ref[...] += jnp.sum(dgr, axis=0, keepdims=True)
        loss_ref[...] += 0.5 * jnp.sum(jnp.mean(err * err, axis=-1, keepdims=True), axis=0, keepdims=True)

    row = pl.BlockSpec((tr, d), lambda i: (i, 0))
    vec = pl.BlockSpec((1, d), lambda i: (0, 0))
    one = pl.BlockSpec((1, 1), lambda i: (0, 0))
    dx, dg, loss = pl.pallas_call(
        body, name="loss_head", grid=(t // tr,), in_specs=[row, vec, row], out_specs=[row, vec, one],
        out_shape=[jax.ShapeDtypeStruct((t, d), F32), jax.ShapeDtypeStruct((1, d), F32),
                   jax.ShapeDtypeStruct((1, 1), F32)],
        compiler_params=_params(("arbitrary",)))(x, g.reshape(1, d), target)
    return loss.reshape(()), dx, dg.reshape(d)


def _head_masks(shape):
    lane = lax.broadcasted_iota(jnp.int32, shape, len(shape) - 1)
    return lane < HEAD_DIM, lane >= HEAD_DIM


def _split_heads(x):
    m0, m1 = _head_masks(x.shape)
    zero = jnp.zeros_like(x)
    return jnp.where(m0, x, zero), jnp.where(m1, x, zero)


def _lane_pair(a0, a1, rows):
    m0, _ = _head_masks((rows, LANES))
    return jnp.where(m0, a0, a1)


def _qkv_readers(refs, packed):
    if packed:
        (r,) = refs
        return tuple((lambda r0, n, s=s: r[pl.ds(r0, n), s * LANES:(s + 1) * LANES]) for s in range(3))
    return tuple((lambda r0, n, ref=ref: ref[pl.ds(r0, n), :]) for ref in refs)


def _pair_spec(seq, col0, width=LANES):
    return pl.BlockSpec((seq, width), lambda p, b: (b, col0 + p))


def _fox_specs(seq, nk, tk):
    return [pl.BlockSpec((None, None, seq, 2), lambda p, b: (b, p, 0, 0)),
            pl.BlockSpec((None, None, nk, 2, tk), lambda p, b: (b, p, 0, 0, 0))]


def _softmax_attn_fwd(src, *, nb, mode, mixer=None, out_buf=None, extra=(), name, comm=None):
    packed = mode != "cross"
    n_src = 1 if packed else 3
    seq_q = (src if packed else src[0]).shape[0] // nb
    seq_k = seq_q if packed else src[1].shape[0] // nb
    tq, tk = min(ATT_TILE, seq_q), min(ATT_TILE, seq_k)
    nq, nk = seq_q // tq, seq_k // tk
    n_ex = len(extra)

    def body(*refs):
        q_at, k_at, v_at = _qkv_readers(refs[:n_src], packed)
        ex = refs[n_src:n_src + n_ex]
        o_ref, lse_ref = refs[-2:]

        def q_tile(i, _):
            r0 = pl.multiple_of(i * tq, tq)
            qm = _split_heads((q_at(r0, tq) * SCALE).astype(BF16))
            if mode == "fox":
                cq = ex[0][pl.ds(r0, tq), :]
                row = r0 + lax.broadcasted_iota(jnp.int32, (tq, tk), 0)

            def k_tile(j, carry, diagonal=False):
                m, l, acc = carry
                c0 = pl.multiple_of(j * tk, tk)
                kt = k_at(c0, tk).astype(BF16)
                vm = _split_heads(v_at(c0, tk).astype(BF16))
                if mode == "fox":
                    ck = ex[1][j]
                hs = range(2)
                s = [lax.dot_general(qm[h], kt, NT_DIMS, preferred_element_type=F32) for h in hs]
                if mode == "fox":
                    s = [s[h] + cq[:, h:h + 1] - ck[h:h + 1, :] for h in hs]
                    if diagonal:
                        keep = (c0 + lax.broadcasted_iota(jnp.int32, (tq, tk), 1)) <= row
                        s = [jnp.where(keep, s[h], MASKED) for h in hs]
                elif mode == "dil":
                    s = [s[h] + ex[0][h, i - j] for h in hs]
                new_m = [jnp.maximum(m[h], jnp.max(s[h], axis=-1, keepdims=True)) for h in hs]
                p = [jnp.exp(s[h] - new_m[h]) for h in hs]
                alpha = [jnp.exp(m[h] - new_m[h]) for h in hs]
                new_l = [alpha[h] * l[h] + jnp.sum(p[h], axis=-1, keepdims=True) for h in hs]
                pv = [jnp.dot(p[h].astype(BF16), vm[h], preferred_element_type=F32) for h in hs]
                acc = acc * _lane_pair(alpha[0], alpha[1], tq) + (pv[0] + pv[1])
                return tuple(new_m), tuple(new_l), acc

            init = ((jnp.full((tq, 1), MASKED, F32),) * 2, (jnp.zeros((tq, 1), F32),) * 2,
                    jnp.zeros((tq, LANES), F32))
            if mode == "fox":
                m, l, acc = k_tile(i, lax.fori_loop(0, i, k_tile, init), True)
            else:
                m, l, acc = lax.fori_loop(0, i + 1 if packed else nk, k_tile, init)
            o_ref[pl.ds(r0, tq), :] = acc / _lane_pair(l[0], l[1], tq)
            lse_ref[pl.ds(r0, tq), :] = _lane_pair(m[0] + jnp.log(l[0]), m[1] + jnp.log(l[1]), tq)
            return 0

        lax.fori_loop(0, nq, q_tile, 0)

    lse_shape = jax.ShapeDtypeStruct((nb * seq_q, 2 * LANES), F32)
    if packed:
        in_specs, args = [_pair_spec(seq_q, 2 * mixer, PAIR_W)], [src]
        in_specs += _fox_specs(seq_q, nk, tk) if mode == "fox" else [
            pl.BlockSpec((None, 2, nq, tq, tk), lambda p, b: (p, 0, 0, 0, 0))]
        args += list(extra) + [out_buf]
        in_specs.append(ANY)
        out_specs = [_pair_spec(seq_q, 2 * mixer), _pair_spec(seq_q, 0)]
        out_shape = [jax.ShapeDtypeStruct(out_buf.shape, F32), lse_shape]
        aliases = {len(args) - 1: 0}
    else:
        in_specs = [_pair_spec(seq_q, 0), _pair_spec(seq_k, 0), _pair_spec(seq_k, 0)]
        args = list(src)
        out_specs = [_pair_spec(seq_q, 0), _pair_spec(seq_q, 0)]
        out_shape = [lse_shape, lse_shape]
        aliases = {}
    out, carried = _pallas(body, name=name, grid=(2, nb), in_specs=in_specs, out_specs=out_specs, out_shape=out_shape,
                           args=args, aliases=aliases, sem=("parallel", "arbitrary"), comm=comm)
    return out if comm is None else (out, carried)


def _softmax_attn_bwd(src, o, lse, do, *, nb, mode, mixer=None, dbuf=None, extra=(), name):
    packed = mode != "cross"
    n_src = 1 if packed else 3
    seq_q = (src if packed else src[0]).shape[0] // nb
    seq_k = seq_q if packed else src[1].shape[0] // nb
    tq, tk = min(ATT_TILE, seq_q), min(ATT_TILE, seq_k)
    nq, nk = seq_q // tq, seq_k // tk
    n_ex = len(extra)
    n_in = n_src + 3 + n_ex + (1 if packed else 0)

    def body(*refs):
        q_at, k_at, v_at = _qkv_readers(refs[:n_src], packed)
        o_ref, lse_ref, do_ref = refs[n_src:n_src + 3]
        ex = refs[n_src + 3:n_src + 3 + n_ex]
        outs = refs[n_in:]
        if packed:
            d_ref = outs[0]
            dq_w = lambda r0, val: d_ref.__setitem__((pl.ds(r0, tq), slice(0, LANES)), val)
            dk_ref = d_ref.at[:, LANES:2 * LANES]
            dv_ref = d_ref.at[:, 2 * LANES:3 * LANES]
        else:
            dq_ref, dk_ref, dv_ref = outs[:3]
            dq_w = lambda r0, val: dq_ref.__setitem__((pl.ds(r0, tq), slice(None)), val)
        dk_ref[...] = jnp.zeros((seq_k, LANES), F32)
        dv_ref[...] = jnp.zeros((seq_k, LANES), F32)
        if mode == "fox":
            dcum_ref, dcq_ref = outs[-2:]
            dcum_ref[...] = jnp.zeros_like(dcum_ref)
        if mode == "dil":
            dbias_ref = outs[-1]

            @pl.when(pl.program_id(1) == 0)
            def _():
                dbias_ref[...] = jnp.zeros_like(dbias_ref)

        def q_tile(i, _):
            r0 = pl.multiple_of(i * tq, tq)
            qm = _split_heads((q_at(r0, tq) * SCALE).astype(BF16))
            do_f = do_ref[pl.ds(r0, tq), :]
            dom = _split_heads(do_f.astype(BF16))
            dd = _split_heads(do_f * o_ref[pl.ds(r0, tq), :])
            delta = [jnp.sum(dd[h], axis=-1, keepdims=True) for h in range(2)]
            lse_t = lse_ref[pl.ds(r0, tq), :]
            lse_h = [lse_t[:, 0:1], lse_t[:, HEAD_DIM:HEAD_DIM + 1]]
            if mode == "fox":
                cq = ex[0][pl.ds(r0, tq), :]
                row = r0 + lax.broadcasted_iota(jnp.int32, (tq, tk), 0)

            def k_tile(j, carry, diagonal=False):
                dq, rs = carry
                c0 = pl.multiple_of(j * tk, tk)
                kt = k_at(c0, tk).astype(BF16)
                vt = v_at(c0, tk).astype(BF16)
                km = _split_heads(kt)
                if mode == "fox":
                    ck = ex[1][j]
                hs = range(2)
                s = [lax.dot_general(qm[h], kt, NT_DIMS, preferred_element_type=F32) for h in hs]
                dp = [lax.dot_general(dom[h], vt, NT_DIMS, preferred_element_type=F32) for h in hs]
                if mode == "fox":
                    s = [s[h] + cq[:, h:h + 1] - ck[h:h + 1, :] for h in hs]
                    if diagonal:
                        keep = (c0 + lax.broadcasted_iota(jnp.int32, (tq, tk), 1)) <= row
                        s = [jnp.where(keep, s[h], MASKED) for h in hs]
                elif mode == "dil":
                    s = [s[h] + ex[0][h, i - j] for h in hs]
                p = [jnp.exp(s[h] - lse_h[h]) for h in hs]
                ds = [p[h] * (dp[h] - delta[h]) for h in hs]
                dsb = [ds[h].astype(BF16) for h in hs]
                pb = [p[h].astype(BF16) for h in hs]
                dq = dq + (jnp.dot(dsb[0], km[0], preferred_element_type=F32)
                           + jnp.dot(dsb[1], km[1], preferred_element_type=F32))
                dk_t = (lax.dot_general(dsb[0], qm[0], TN_DIMS, preferred_element_type=F32)
                        + lax.dot_general(dsb[1], qm[1], TN_DIMS, preferred_element_type=F32))
                dv_t = (lax.dot_general(pb[0], dom[0], TN_DIMS, preferred_element_type=F32)
                        + lax.dot_general(pb[1], dom[1], TN_DIMS, preferred_element_type=F32))
                if mode == "fox":
                    for h in hs:
                        dcum_ref[j, h:h + 1, :] -= jnp.sum(ds[h], axis=0, keepdims=True)
                    rs = tuple(rs[h] + jnp.sum(ds[h], axis=-1, keepdims=True) for h in hs)
                elif mode == "dil":
                    for h in hs:
                        dbias_ref[h, i - j] += ds[h]
                dk_ref[pl.ds(c0, tk), :] += dk_t
                dv_ref[pl.ds(c0, tk), :] += dv_t
                return dq, rs

            zero = (jnp.zeros((tq, 1), F32),) * 2
            init = (jnp.zeros((tq, LANES), F32), zero)
            if mode == "fox":
                dq, rs = k_tile(i, lax.fori_loop(0, i, k_tile, init), True)
            else:
                dq, rs = lax.fori_loop(0, i + 1 if packed else nk, k_tile, init)
            dq_w(r0, dq * SCALE)
            if mode == "fox":
                dcq_ref[pl.ds(r0, tq), :] = jnp.where(lax.broadcasted_iota(jnp.int32, (tq, 2), 1) == 0, rs[0], rs[1])
            return 0

        lax.fori_loop(0, nq, q_tile, 0)

    if packed:
        in_specs = [_pair_spec(seq_q, 2 * mixer, PAIR_W), _pair_spec(seq_q, 2 * mixer), _pair_spec(seq_q, 0),
                    _pair_spec(seq_q, 2 * mixer)]
        args = [src, o, lse, do]
        out_specs = [_pair_spec(seq_q, 2 * mixer, PAIR_W)]
        out_shape = [jax.ShapeDtypeStruct(dbuf.shape, F32)]
        if mode == "fox":
            in_specs += _fox_specs(seq_q, nk, tk)
            out_specs += [_fox_specs(seq_q, nk, tk)[1], _fox_specs(seq_q, nk, tk)[0]]
            out_shape += [jax.ShapeDtypeStruct((nb, 2, nk, 2, tk), F32), jax.ShapeDtypeStruct((nb, 2, seq_q, 2), F32)]
        else:
            tiles = pl.BlockSpec((None, 2, nq, tq, tk), lambda p, b: (p, 0, 0, 0, 0))
            in_specs.append(tiles)
            out_specs.append(tiles)
            out_shape.append(jax.ShapeDtypeStruct((2, 2, nq, tq, tk), F32))
        args += list(extra) + [dbuf]
        in_specs.append(ANY)
        aliases = {len(args) - 1: 0}
    else:
        sq, sk = _pair_spec(seq_q, 0), _pair_spec(seq_k, 0)
        in_specs, args = [sq, sk, sk, sq, sq, sq], list(src) + [o, lse, do]
        out_specs = [sq, sk, sk]
        out_shape = [jax.ShapeDtypeStruct((nb * seq_q, 2 * LANES), F32)] + [
            jax.ShapeDtypeStruct((nb * seq_k, 2 * LANES), F32)] * 2
        aliases = {}
    return pl.pallas_call(
        body, name=name, grid=(2, nb), in_specs=in_specs, out_specs=out_specs, out_shape=out_shape,
        input_output_aliases=aliases, compiler_params=_params(("parallel", "arbitrary")))(*args)


def _log_sigmoid(z):
    return jnp.minimum(z, 0.0) - jnp.log(1.0 + jnp.exp(-jnp.abs(z)))


def _split_bf16(x):
    hi = x.astype(BF16)
    return hi, (x - hi.astype(F32)).astype(BF16)


def _tri(n, fn):
    r = lax.broadcasted_iota(jnp.int32, (n, n), 0)
    c = lax.broadcasted_iota(jnp.int32, (n, n), 1)
    return jnp.where(fn(r, c), 1.0, 0.0).astype(BF16)


def _sb_attn_fwd(proj, out_buf, *, nb, name, comm=None):
    seq = proj.shape[0] // nb
    tq = tk = min(ATT_TILE, seq)
    nq = seq // tq

    def body(qkv_ref, _, o_ref, lt_ref):
        q_at, k_at, v_at = _qkv_readers((qkv_ref,), True)
        after = _tri(tk, lambda r, c: r > c)

        def q_tile(i, _):
            r0 = pl.multiple_of(i * tq, tq)
            qm = _split_heads((q_at(r0, tq) * SCALE).astype(BF16))
            row = r0 + lax.broadcasted_iota(jnp.int32, (tq, tk), 0)

            def k_tile(j, carry, diagonal):
                c, acc = carry
                c0 = pl.multiple_of(j * tk, tk)
                kt = k_at(c0, tk).astype(BF16)
                vm = _split_heads(v_at(c0, tk).astype(BF16))
                if diagonal:
                    strict = (c0 + lax.broadcasted_iota(jnp.int32, (tq, tk), 1)) < row
                hs = range(2)
                z = [lax.dot_general(qm[h], kt, NT_DIMS, preferred_element_type=F32) for h in hs]
                ls = [_log_sigmoid(z[h]) for h in hs]
                lk = [ls[h] - z[h] for h in hs]
                if diagonal:
                    lk = [jnp.where(strict, lk[h], 0.0) for h in hs]
                parts = [_split_bf16(lk[h]) for h in hs]
                sfx = [jnp.dot(parts[h][0], after, preferred_element_type=F32)
                       + jnp.dot(parts[h][1], after, preferred_element_type=F32) for h in hs]
                att = [jnp.exp(ls[h] + sfx[h] + c[h]) for h in hs]
                if diagonal:
                    att = [jnp.where(strict, att[h], 0.0) for h in hs]
                acc = acc + (jnp.dot(att[0].astype(BF16), vm[0], preferred_element_type=F32)
                             + jnp.dot(att[1].astype(BF16), vm[1], preferred_element_type=F32))
                return tuple(c[h] + jnp.sum(lk[h], axis=-1, keepdims=True) for h in hs), acc

            init = ((jnp.zeros((tq, 1), F32),) * 2, jnp.zeros((tq, LANES), F32))
            c, acc = lax.fori_loop(1, i + 1, lambda jj, cr: k_tile(i - jj, cr, False), k_tile(i, init, True))
            o_ref[pl.ds(r0, tq), :] = acc
            lt_ref[pl.ds(r0, tq), :] = _lane_pair(c[0], c[1], tq)
            return 0

        lax.fori_loop(0, nq, q_tile, 0)

    out, carried = _pallas(
        body, name=name, grid=(2, nb), in_specs=[_pair_spec(seq, 2 * MIX_SB, PAIR_W), ANY],
        out_specs=[_pair_spec(seq, 2 * MIX_SB), _pair_spec(seq, 0)],
        out_shape=[jax.ShapeDtypeStruct(out_buf.shape, F32), jax.ShapeDtypeStruct((nb * seq, 2 * LANES), F32)],
        args=[proj, out_buf], aliases={1: 0}, sem=("parallel", "arbitrary"), comm=comm)
    return out if comm is None else (out, carried)


def _sb_attn_bwd(proj, ltot, do, dbuf, *, nb, name, comm=None):
    seq = proj.shape[0] // nb
    tq = tk = min(ATT_TILE, seq)
    nq = seq // tq

    def body(qkv_ref, lt_ref, do_ref, _, d_ref):
        q_at, k_at, v_at = _qkv_readers((qkv_ref,), True)
        upto = _tri(tk, lambda r, c: r <= c)
        before = _tri(tk, lambda r, c: r < c)
        dk_ref = d_ref.at[:, LANES:2 * LANES]
        dv_ref = d_ref.at[:, 2 * LANES:3 * LANES]
        dk_ref[...] = jnp.zeros((seq, LANES), F32)
        dv_ref[...] = jnp.zeros((seq, LANES), F32)

        def q_tile(i, _):
            r0 = pl.multiple_of(i * tq, tq)
            qm = _split_heads((q_at(r0, tq) * SCALE).astype(BF16))
            dom = _split_heads(do_ref[pl.ds(r0, tq), :].astype(BF16))
            lt_t = lt_ref[pl.ds(r0, tq), :]
            lt_h = [lt_t[:, 0:1], lt_t[:, HEAD_DIM:HEAD_DIM + 1]]
            row = r0 + lax.broadcasted_iota(jnp.int32, (tq, tk), 0)

            def k_tile(j, carry, diagonal):
                pc, qc, dq = carry
                c0 = pl.multiple_of(j * tk, tk)
                kt = k_at(c0, tk).astype(BF16)
                vt = v_at(c0, tk).astype(BF16)
                km = _split_heads(kt)
                if diagonal:
                    strict = (c0 + lax.broadcasted_iota(jnp.int32, (tq, tk), 1)) < row
                hs = range(2)
                z = [lax.dot_general(qm[h], kt, NT_DIMS, preferred_element_type=F32) for h in hs]
                da = [lax.dot_general(dom[h], vt, NT_DIMS, preferred_element_type=F32) for h in hs]
                ls = [_log_sigmoid(z[h]) for h in hs]
                lk = [ls[h] - z[h] for h in hs]
                if diagonal:
                    lk = [jnp.where(strict, lk[h], 0.0) for h in hs]
                parts = [_split_bf16(lk[h]) for h in hs]
                pin = [jnp.dot(parts[h][0], upto, preferred_element_type=F32)
                       + jnp.dot(parts[h][1], upto, preferred_element_type=F32) for h in hs]
                att = [jnp.exp(ls[h] + (lt_h[h] - pc[h] - pin[h])) for h in hs]
                if diagonal:
                    att = [jnp.where(strict, att[h], 0.0) for h in hs]
                dg = [att[h] * da[h] for h in hs]
                qx = [qc[h] + jnp.dot(dg[h].astype(BF16), before, preferred_element_type=F32) for h in hs]
                sig = [jnp.exp(ls[h]) for h in hs]
                dz = [dg[h] * (1.0 - sig[h]) - sig[h] * qx[h] for h in hs]
                if diagonal:
                    dz = [jnp.where(strict, dz[h], 0.0) for h in hs]
                dzb = [dz[h].astype(BF16) for h in hs]
                attb = [att[h].astype(BF16) for h in hs]
                dq = dq + (jnp.dot(dzb[0], km[0], preferred_element_type=F32)
                           + jnp.dot(dzb[1], km[1], preferred_element_type=F32))
                dk_ref[pl.ds(c0, tk), :] += (lax.dot_general(dzb[0], qm[0], TN_DIMS, preferred_element_type=F32)
                                             + lax.dot_general(dzb[1], qm[1], TN_DIMS, preferred_element_type=F32))
                dv_ref[pl.ds(c0, tk), :] += (lax.dot_general(attb[0], dom[0], TN_DIMS, preferred_element_type=F32)
                                             + lax.dot_general(attb[1], dom[1], TN_DIMS, preferred_element_type=F32))
                return (tuple(pc[h] + jnp.sum(lk[h], axis=-1, keepdims=True) for h in hs),
                        tuple(qc[h] + jnp.sum(dg[h], axis=-1, keepdims=True) for h in hs), dq)

            zero = (jnp.zeros((tq, 1), F32),) * 2
            carry = lax.fori_loop(0, i, lambda j, cr: k_tile(j, cr, False), (zero, zero, jnp.zeros((tq, LANES), F32)))
            _, _, dq = k_tile(i, carry, True)
            d_ref[pl.ds(r0, tq), 0:LANES] = dq * SCALE
            return 0

        lax.fori_loop(0, nq, q_tile, 0)

    out, carried = _pallas(
        body, name=name, grid=(2, nb),
        in_specs=[_pair_spec(seq, 2 * MIX_SB, PAIR_W), _pair_spec(seq, 0), _pair_spec(seq, 2 * MIX_SB), ANY],
        out_specs=[_pair_spec(seq, 2 * MIX_SB, PAIR_W)], out_shape=[jax.ShapeDtypeStruct(dbuf.shape, F32)],
        args=[proj, ltot, do, dbuf], aliases={3: 0}, sem=("parallel", "arbitrary"), comm=comm)
    return out[0] if comm is None else (out[0], carried)


def _lane_scan(x, reverse=False):
    n = x.shape[-1]
    lane = lax.broadcasted_iota(jnp.int32, x.shape, 1)
    k = 1
    while k < n:
        if reverse:
            x = x + jnp.where(lane < n - k, pltpu.roll(x, n - k, 1), 0.0)
        else:
            x = x + jnp.where(lane >= k, pltpu.roll(x, k, 1), 0.0)
        k *= 2
    return x


def _fox_gate_fwd(f_rows, b_rows):
    def body(f_ref, b_ref, o_ref):
        o_ref[...] = _lane_scan(_log_sigmoid(f_ref[...] + b_ref[...]))

    return pl.pallas_call(body, name="fox_gate_fwd", out_shape=jax.ShapeDtypeStruct(f_rows.shape, F32))(f_rows, b_rows)


def _fox_gate_bwd(dcum, f_rows, b_rows):
    def body(d_ref, f_ref, b_ref, df_ref, db_ref):
        z = f_ref[...] + b_ref[...]
        df = _lane_scan(d_ref[...], reverse=True) * jnp.exp(_log_sigmoid(-z))
        df_ref[...] = df
        rs = jnp.sum(df, axis=-1, keepdims=True)
        tot = rs
        for e in range(1, f_rows.shape[0] // N_HEADS):
            tot = tot + pltpu.roll(rs, e * N_HEADS, 0)
        db_ref[...] = tot

    return pl.pallas_call(
        body, name="fox_gate_bwd",
        out_shape=[jax.ShapeDtypeStruct(f_rows.shape, F32), jax.ShapeDtypeStruct((f_rows.shape[0], 1), F32)],
    )(dcum, f_rows, b_rows)


def _dil_tables(seq):
    t = min(ATT_TILE, seq)
    n = seq // t
    a = np.arange(t)
    d = (np.arange(n)[:, None, None] * t + a[None, :, None] - a[None, None, :]).astype(np.int64)
    count = np.zeros(d.shape, np.int64)
    for window, dil in DILATED_PATTERNS:
        count += (d >= 0) & (d % dil == 0) & (d // dil <= window // dil)
    nn = np.maximum(d, 0)
    max_exact = NUM_BUCKETS // 2
    nf = np.maximum(nn, 1).astype(np.float32)
    large = max_exact + (np.log(nf / np.float32(max_exact)) / np.float32(math.log(MAX_DISTANCE / max_exact))
                         * np.float32(NUM_BUCKETS - max_exact)).astype(np.int32)
    bucket = np.where(nn < max_exact, nn, np.minimum(large, NUM_BUCKETS - 1))
    bucket = np.where(count > 0, bucket, -1).astype(np.int32)
    logc = np.where(count > 0, np.log(np.maximum(count, 1)), MASKED).astype(np.float32)
    return bucket, logc


def _dil_bias(rel_bias, seq):
    bucket, logc = _dil_tables(seq)
    n, t, _ = bucket.shape

    def body(rb_ref, bk_ref, lc_ref, o_ref):
        h = pl.program_id(0) * 2 + pl.program_id(1)
        bk = bk_ref[...]
        out = lc_ref[...]
        for b in range(NUM_BUCKETS):
            out = jnp.where(bk == b, out + rb_ref[b, h], out)
        o_ref[...] = out

    full = pl.BlockSpec((n, t, t), lambda p, h: (0, 0, 0))
    return pl.pallas_call(
        body, name="dil_bias", grid=(2, 2),
        in_specs=[pl.BlockSpec(memory_space=pltpu.SMEM), full, full],
        out_specs=pl.BlockSpec((None, None, n, t, t), lambda p, h: (p, h, 0, 0, 0)),
        out_shape=jax.ShapeDtypeStruct((2, 2, n, t, t), F32),
        compiler_params=_params(("parallel", "parallel")))(rel_bias, jnp.asarray(bucket), jnp.asarray(logc))


def _dil_bias_bwd(dbias, seq):
    bucket, _ = _dil_tables(seq)
    n, t, _ = bucket.shape

    def body(d_ref, bk_ref, o_ref):
        bk = bk_ref[...]
        lane = lax.broadcasted_iota(jnp.int32, (1, LANES), 1)
        for b in range(NUM_BUCKETS):
            rowv = jnp.zeros((1, LANES), F32)
            for h in range(N_HEADS):
                s = jnp.sum(jnp.where(bk == b, d_ref[h // 2, h % 2], 0.0))
                rowv = jnp.where(lane == h, s, rowv)
            o_ref[b:b + 1, :] = rowv

    out = pl.pallas_call(body, name="dil_bias_bwd", out_shape=jax.ShapeDtypeStruct((NUM_BUCKETS, LANES), F32),
                         compiler_params=pltpu.CompilerParams(vmem_limit_bytes=VMEM_LIMIT))(dbias, jnp.asarray(bucket))
    return out[:, :N_HEADS]


def _shift_rows(x, k, row, fill=0.0):
    n = x.shape[0]
    if k > 0:
        return jnp.where(row >= k, pltpu.roll(x, k, 0), fill)
    return jnp.where(row < n + k, pltpu.roll(x, n + k, 0), fill)


def _row_scan(a, u, row, reverse=False):
    n = a.shape[0]
    k = 1
    while k < n:
        s = -k if reverse else k
        u = a * _shift_rows(u, s, row) + u
        a = a * _shift_rows(a, s, row, 1.0)
        k *= 2
    return u


def _sigmoid(x):
    return 1.0 / (1.0 + jnp.exp(-x))


def _gelu(g):
    return 0.5 * g * (1.0 + lax.erf(g * (2.0 ** -0.5)))


def _gelu_grad(g):
    return 0.5 * (1.0 + lax.erf(g * (2.0 ** -0.5))) + g * jnp.exp(-0.5 * g * g) * (1.0 / math.sqrt(2.0 * math.pi))


def _neg_expm1(x):
    small = -x * (1.0 + x * (0.5 + x * (1.0 / 6.0 + x * (1.0 / 24.0))))
    return jnp.where(x > -0.03, small, 1.0 - jnp.exp(x))


def _lru_core(x, vec, wa, wx, row):
    xs = [_shift_rows(x, 3 - j, row) if j < 3 else x for j in range(4)]
    xc = vec[4:5, :]
    for j in range(4):
        xc = xc + vec[j:j + 1, :] * xs[j]
    xcb = xc.astype(BF16)
    r = _sigmoid(jnp.dot(xcb, wa, preferred_element_type=F32) + vec[5:6, :])
    ig = _sigmoid(jnp.dot(xcb, wx, preferred_element_type=F32) + vec[6:7, :])
    lam = vec[7:8, :]
    sp = jnp.maximum(-lam, 0.0) - _log_sigmoid(jnp.abs(lam))
    la = -LRU_C * r * sp
    a = jnp.exp(la)
    mult = jnp.sqrt(_neg_expm1(2.0 * la))
    return xs, xc, xcb, r, ig, sp, la, a, mult


def _lru_specs(seq):
    xg = pl.BlockSpec((seq, LRU_W), lambda hf, b: (b, COL_LRU // LRU_W + hf))
    mix = pl.BlockSpec((seq, LANES), lambda hf, b: (b, 2 * MIX_LRU + hf))
    vec = pl.BlockSpec((SUBLANES, LANES), lambda hf, b: (0, hf))
    mat = pl.BlockSpec((None, LANES, LANES), lambda hf, b: (hf, 0, 0))
    return xg, mix, vec, mat


def _lru_fwd(proj, vec, wa, wx, out_buf, *, nb, name):
    seq = proj.shape[0] // nb

    def body(xg_ref, vec_ref, wa_ref, wx_ref, _, o_ref):
        row = lax.broadcasted_iota(jnp.int32, (seq, LANES), 0)
        _, xc, _, _, ig, _, _, a, mult = _lru_core(xg_ref[:, 0:LANES], vec_ref[...], wa_ref[...], wx_ref[...], row)
        h = _row_scan(a, mult * (ig * xc), row)
        o_ref[...] = h * _gelu(xg_ref[:, LANES:LRU_W])

    xg, mix, vecs, mat = _lru_specs(seq)
    return pl.pallas_call(
        body, name=name, grid=(2, nb), in_specs=[xg, vecs, mat, mat, ANY], out_specs=mix,
        out_shape=jax.ShapeDtypeStruct(out_buf.shape, F32), input_output_aliases={4: 0},
        compiler_params=_params(("parallel", "arbitrary")))(proj, vec, wa, wx, out_buf)


def _lru_bwd(proj, vec, wa, wx, dout, dbuf, *, nb, name):
    seq = proj.shape[0] // nb

    def body(xg_ref, vec_ref, wa_ref, wx_ref, do_ref, _, d_ref, dvec_ref, dwa_ref, dwx_ref):
        row = lax.broadcasted_iota(jnp.int32, (seq, LANES), 0)
        vec_, wa_, wx_ = vec_ref[...], wa_ref[...], wx_ref[...]
        xs, xc, xcb, r, ig, sp, la, a, mult = _lru_core(xg_ref[:, 0:LANES], vec_, wa_, wx_, row)
        h = _row_scan(a, mult * (ig * xc), row)
        gate, do = xg_ref[:, LANES:LRU_W], do_ref[...]
        d_ref[:, LANES:LRU_W] = do * h * _gelu_grad(gate)
        dh = do * _gelu(gate)
        gacc = _row_scan(_shift_rows(a, -1, row), dh, row, reverse=True)
        da = gacc * _shift_rows(h, 1, row)
        dmult = gacc * (ig * xc)
        dig = gacc * (mult * xc)
        dxc = gacc * (mult * ig)
        dla = da * a - dmult * (a * a) / mult
        dr = (-LRU_C) * sp * dla
        dsp = jnp.sum((-LRU_C) * r * dla, axis=0, keepdims=True)
        dpr = dr * r * (1.0 - r)
        dpi = dig * ig * (1.0 - ig)
        dprb, dpib = dpr.astype(BF16), dpi.astype(BF16)
        dxc = (dxc + lax.dot_general(dprb, wa_, NT_DIMS, preferred_element_type=F32)
               + lax.dot_general(dpib, wx_, NT_DIMS, preferred_element_type=F32))
        dx = vec_[3:4, :] * dxc
        for j in range(3):
            dx = dx + vec_[j:j + 1, :] * _shift_rows(dxc, -(3 - j), row)
        d_ref[:, 0:LANES] = dx

        @pl.when(pl.program_id(1) == 0)
        def _():
            dvec_ref[...] = jnp.zeros_like(dvec_ref)
            dwa_ref[...] = jnp.zeros_like(dwa_ref)
            dwx_ref[...] = jnp.zeros_like(dwx_ref)

        for j in range(4):
            dvec_ref[j:j + 1, :] += jnp.sum(dxc * xs[j], axis=0, keepdims=True)
        dvec_ref[4:5, :] += jnp.sum(dxc, axis=0, keepdims=True)
        dvec_ref[5:6, :] += jnp.sum(dpr, axis=0, keepdims=True)
        dvec_ref[6:7, :] += jnp.sum(dpi, axis=0, keepdims=True)
        lam = vec_[7:8, :]
        dvec_ref[7:8, :] += -dsp * _sigmoid(-lam)
        dwa_ref[...] += lax.dot_general(xcb, dprb, TN_DIMS, preferred_element_type=F32)
        dwx_ref[...] += lax.dot_general(xcb, dpib, TN_DIMS, preferred_element_type=F32)

    xg, mix, vecs, mat = _lru_specs(seq)
    return pl.pallas_call(
        body, name=name, grid=(2, nb), in_specs=[xg, vecs, mat, mat, mix, ANY], out_specs=[xg, vecs, mat, mat],
        out_shape=[jax.ShapeDtypeStruct(dbuf.shape, F32), jax.ShapeDtypeStruct((SUBLANES, 2 * LANES), F32),
                   jax.ShapeDtypeStruct((2, LANES, LANES), F32), jax.ShapeDtypeStruct((2, LANES, LANES), F32)],
        input_output_aliases={5: 0},
        compiler_params=_params(("parallel", "arbitrary")))(proj, vec, wa, wx, dout, dbuf)


FFN_ROWS = 256
FFN_COLS = 1408


def _with_halo(halo, x, k):
    xx = jnp.concatenate([halo, x], axis=0)
    return pltpu.roll(xx, k, 0)[SUBLANES:, :]


def _ffn_conv(x_ref, halo_ref, cw, pos):
    x, halo = x_ref[...], halo_ref[...]
    x1 = jnp.where(pos >= 1, _with_halo(halo, x, 1), 0.0)
    x2 = jnp.where(pos >= 2, _with_halo(halo, x, 2), 0.0)
    return cw[3:4, :] + cw[0:1, :] * x2 + cw[1:2, :] * x1 + cw[2:3, :] * x, x1, x2


def _ffn_specs(tm, tn, gate_off):
    prev = lambda i: jnp.maximum(i * (tm // SUBLANES) - 1, 0)
    up = pl.BlockSpec((tm, tn), lambda j, i: (i, j))
    gate = pl.BlockSpec((tm, tn), lambda j, i: (i, j + gate_off))
    up_h = pl.BlockSpec((SUBLANES, tn), lambda j, i: (prev(i), j))
    gate_h = pl.BlockSpec((SUBLANES, tn), lambda j, i: (prev(i), j + gate_off))
    cw_up = pl.BlockSpec((SUBLANES, tn), lambda j, i: (0, j))
    cw_gate = pl.BlockSpec((SUBLANES, tn), lambda j, i: (0, j + gate_off))
    return up, gate, up_h, gate_h, cw_up, cw_gate


def _ffn_act(hf, cw, *, seq, name):
    t, w2 = hf.shape
    w = w2 // 2
    tm, tn = _tile(seq, FFN_ROWS, SUBLANES), _tile(w, FFN_COLS)

    def body(u_ref, g_ref, uh_ref, gh_ref, cu_ref, cg_ref, o_ref):
        pos = (pl.program_id(1) * tm + lax.broadcasted_iota(jnp.int32, (tm, 1), 0)) % seq
        up, _, _ = _ffn_conv(u_ref, uh_ref, cu_ref[...], pos)
        gate, _, _ = _ffn_conv(g_ref, gh_ref, cg_ref[...], pos)
        o_ref[...] = (_gelu(gate) * up).astype(BF16)

    specs = _ffn_specs(tm, tn, w // tn)
    return pl.pallas_call(
        body, name=name, grid=(w // tn, t // tm), in_specs=list(specs), out_specs=specs[0],
        out_shape=jax.ShapeDtypeStruct((t, w), BF16),
        compiler_params=_params(("parallel", "parallel")))(hf, hf, hf, hf, cw, cw)


def _ffn_bwd(hf, cw, dact, *, seq, name, comm=None):
    t, w2 = hf.shape
    w = w2 // 2
    tm, tn = _tile(seq, FFN_ROWS, 2 * SUBLANES), _tile(w, FFN_COLS)
    ext = tm + SUBLANES
    last = t // SUBLANES - 1

    def body(u_ref, g_ref, uh_ref, gh_ref, cu_ref, cg_ref, un_ref, gn_ref, da_ref, dn_ref, d_ref, dcu_ref, dcg_ref):
        pos = (pl.program_id(1) * tm + lax.broadcasted_iota(jnp.int32, (ext, 1), 0)) % seq

        def conv(x_ref, prev_ref, next_ref, cwv):
            xx = jnp.concatenate([prev_ref[...], x_ref[...], next_ref[...]], axis=0)
            x1 = jnp.where(pos >= 1, pltpu.roll(xx, 1, 0)[SUBLANES:, :], 0.0)
            x2 = jnp.where(pos >= 2, pltpu.roll(xx, 2, 0)[SUBLANES:, :], 0.0)
            x0 = xx[SUBLANES:, :]
            return cwv[3:4, :] + cwv[0:1, :] * x2 + cwv[1:2, :] * x1 + cwv[2:3, :] * x0, (x2, x1, x0)

        def back(d, cwv):
            d1 = jnp.where(pos < seq - 1, pltpu.roll(d, ext - 1, 0), 0.0)
            d2 = jnp.where(pos < seq - 2, pltpu.roll(d, ext - 2, 0), 0.0)
            return (cwv[2:3, :] * d + cwv[1:2, :] * d1 + cwv[0:1, :] * d2)[:tm, :].astype(BF16)

        cu, cg = cu_ref[...], cg_ref[...]
        up, u_taps = conv(u_ref, uh_ref, un_ref, cu)
        gate, g_taps = conv(g_ref, gh_ref, gn_ref, cg)
        da = jnp.concatenate([da_ref[...], dn_ref[...]], axis=0)
        cdf = 0.5 * (1.0 + lax.erf(gate * (2.0 ** -0.5)))
        d_up = da * (gate * cdf)
        d_gate = da * up * (cdf + gate * jnp.exp(-0.5 * gate * gate) * (1.0 / math.sqrt(2.0 * math.pi)))
        d_ref[0] = back(d_up, cu)
        d_ref[1] = back(d_gate, cg)

        @pl.when(pl.program_id(1) == 0)
        def _():
            dcu_ref[...] = jnp.zeros_like(dcu_ref)
            dcg_ref[...] = jnp.zeros_like(dcg_ref)

        for ref, d, taps in ((dcu_ref, d_up, u_taps), (dcg_ref, d_gate, g_taps)):
            own = d[:tm, :]
            for j in range(3):
                ref[j:j + 1, :] += jnp.sum(own * taps[j][:tm, :], axis=0, keepdims=True)
            ref[3:4, :] += jnp.sum(own, axis=0, keepdims=True)

    gate_off = w // tn
    specs = _ffn_specs(tm, tn, gate_off)
    tile, cwt = specs[0], specs[4]
    nxt = lambda i: jnp.minimum((i + 1) * (tm // SUBLANES), last)
    up_n = pl.BlockSpec((SUBLANES, tn), lambda j, i: (nxt(i), j))
    gate_n = pl.BlockSpec((SUBLANES, tn), lambda j, i: (nxt(i), j + gate_off))
    out, carried = _pallas(
        body, name=name, grid=(w // tn, t // tm), in_specs=list(specs) + [up_n, gate_n, tile, up_n],
        out_specs=[pl.BlockSpec((2, tm, tn), lambda j, i: (0, i, j)), cwt, cwt],
        out_shape=[jax.ShapeDtypeStruct((2, t, w), BF16), jax.ShapeDtypeStruct((SUBLANES, w), F32),
                   jax.ShapeDtypeStruct((SUBLANES, w), F32)],
        args=[hf, hf, hf, hf, cw, cw, hf, hf, dact, dact], sem=("parallel", "arbitrary"), comm=comm)
    return out if comm is None else (out, carried)


def _adamw(w, g, m, v, *, name, rows=256):
    nl, r, c = w.shape
    tr = _tile(r, rows, SUBLANES)

    def body(w_ref, g_ref, m_ref, v_ref, d_ref, nm_ref, nv_ref):
        g_ = g_ref[...]
        nm = ADAM_B1 * m_ref[...] + (1.0 - ADAM_B1) * g_
        nv = ADAM_B2 * v_ref[...] + (1.0 - ADAM_B2) * (g_ * g_)
        m_hat = nm / (1.0 - ADAM_B1 ** ADAM_STEP)
        v_hat = nv / (1.0 - ADAM_B2 ** ADAM_STEP)
        d_ref[...] = -ADAM_LR * (m_hat / (jnp.sqrt(v_hat) + ADAM_EPS) + ADAM_WD * w_ref[...])
        nm_ref[...] = nm
        nv_ref[...] = nv

    spec = pl.BlockSpec((None, tr, c), lambda l, i: (l, i, 0))
    shape = jax.ShapeDtypeStruct((nl, r, c), F32)
    return pl.pallas_call(body, name=name, grid=(nl, r // tr), in_specs=[spec] * 4, out_specs=[spec] * 3,
                          out_shape=[shape] * 3, compiler_params=_params(("parallel", "parallel")))(w, g, m, v)


def _mesh_pos():
    return lax.axis_index("x"), lax.axis_index("y"), lax.axis_index("c")


def _peers(x, y):
    chips = [(1 - x, y), (x, 1 - y), (1 - x, 1 - y)]
    return [(px, py, 2 * px + py) for px, py in chips]


def _remote(src, dst, send_sems, recv_sems, idx, to):
    return pltpu.make_async_remote_copy(src, dst, send_sems.at[idx], recv_sems.at[idx], device_id=to,
                                        device_id_type=MESH)


class _Comm:
    def __init__(self, operands, out_shape, aliases, sems, copies):
        self.operands, self.out_shape, self.aliases, self.sems, self.copies = operands, out_shape, aliases, sems, copies

    def start(self, ins, outs, sems):
        for send, _ in self.copies(ins, outs, sems):
            send.start()

    def wait(self, ins, outs, sems):
        pairs = self.copies(ins, outs, sems)
        for _, recv in pairs:
            recv.wait_recv()
        for send, _ in pairs:
            send.wait_send()


def _pallas(body, *, name, grid, in_specs, out_specs, out_shape, args, aliases=None, scratch=(), sem, comm=None):
    n_in, n_out = len(in_specs), len(out_specs)
    aliases = dict(aliases or {})
    if comm is None:
        out = pl.pallas_call(body, name=name, grid=grid, in_specs=in_specs, out_specs=out_specs, out_shape=out_shape,
                             input_output_aliases=aliases, scratch_shapes=list(scratch),
                             compiler_params=_params(sem))(*args)
        return list(out), []
    nci, nco, ncs = len(comm.operands), len(comm.out_shape), len(comm.sems)

    def carried(*refs):
        ins, cin = refs[:n_in], refs[n_in:n_in + nci]
        o0 = n_in + nci
        outs, cout = refs[o0:o0 + n_out], refs[o0 + n_out:o0 + n_out + nco]
        s0 = o0 + n_out + nco
        own, csem = refs[s0:len(refs) - ncs], refs[len(refs) - ncs:]
        ids = [pl.program_id(ax) for ax in range(len(grid))]
        first, last = ids[0] == 0, ids[0] == grid[0] - 1
        for i, g in zip(ids[1:], grid[1:]):
            first, last = jnp.logical_and(first, i == 0), jnp.logical_and(last, i == g - 1)

        @pl.when(first)
        def _():
            comm.start(cin, cout, csem)

        body(*ins, *outs, *own)

        @pl.when(last)
        def _():
            comm.wait(cin, cout, csem)

    aliases.update({n_in + i: n_out + j for i, j in comm.aliases.items()})
    out = pl.pallas_call(
        carried, name=name, grid=grid, in_specs=list(in_specs) + [ANY] * nci, out_specs=list(out_specs) + [ANY] * nco,
        out_shape=list(out_shape) + list(comm.out_shape), input_output_aliases=aliases,
        scratch_shapes=list(scratch) + list(comm.sems),
        compiler_params=_params(("arbitrary",) * len(grid)))(*args, *comm.operands)
    return list(out[:n_out]), list(out[n_out:])


def _run_comm(comm, *, name):
    nci, nco = len(comm.operands), len(comm.out_shape)

    def body(*refs):
        ins, outs, sems = refs[:nci], refs[nci:nci + nco], refs[nci + nco:]
        comm.start(ins, outs, sems)
        comm.wait(ins, outs, sems)

    return pl.pallas_call(body, name=name, in_specs=[ANY] * nci, out_specs=[ANY] * nco, out_shape=list(comm.out_shape),
                          input_output_aliases=dict(comm.aliases), scratch_shapes=list(comm.sems))(*comm.operands)


def _pair_sems(*shape):
    return [pltpu.SemaphoreType.DMA(shape), pltpu.SemaphoreType.DMA(shape)]


def _gather_comm(bufs, layer, stage):
    n = len(bufs)

    def copies(ins, outs, sems):
        x, y, c = _mesh_pos()
        me = 2 * x + y
        pairs = []
        for i in range(n):
            h = bufs[i].shape[2] // 2
            mine, other = pl.ds(c * h, h), pl.ds((1 - c) * h, h)
            for r, (px, py, k) in enumerate(_peers(x, y)):
                if stage == 0:
                    send = _remote(ins[i].at[layer, me, mine, :], outs[i].at[layer, me, mine, :], *sems, (i, r), (px, py, c))
                    land = outs[i].at[layer, k, mine, :]
                    recv = _remote(land, land, *sems, (i, r), (px, py, c))
                else:
                    send = _remote(ins[i].at[layer, k, mine, :], outs[i].at[layer, k, mine, :], *sems, (i, r), (x, y, 1 - c))
                    land = outs[i].at[layer, k, other, :]
                    recv = _remote(land, land, *sems, (i, r), (x, y, 1 - c))
                pairs.append((send, recv))
        return pairs

    return _Comm(bufs, [jax.ShapeDtypeStruct(b.shape, b.dtype) for b in bufs], {i: i for i in range(n)},
                 _pair_sems(n, 3), copies)


def _reduce_sibling_comm(gs):
    n = len(gs)

    def copies(ins, outs, sems):
        x, y, c = _mesh_pos()
        pairs = []
        for i in range(n):
            h = gs[i].shape[1] // 2
            cp = _remote(ins[i].at[:, pl.ds((1 - c) * h, h), :], outs[i], *sems, i, (x, y, 1 - c))
            pairs.append((cp, cp))
        return pairs

    return _Comm(gs, [jax.ShapeDtypeStruct((g.shape[0], g.shape[1] // 2, g.shape[2]), g.dtype) for g in gs], {},
                 _pair_sems(n), copies)


def _reduce_chips_comm(ps):
    n = len(ps)

    def copies(ins, outs, sems):
        x, y, c = _mesh_pos()
        pairs = []
        for i in range(n):
            for r, (px, py, k) in enumerate(_peers(x, y)):
                cp = _remote(ins[i].at[k], outs[i].at[r], *sems, (i, r), (px, py, c))
                pairs.append((cp, cp))
        return pairs

    return _Comm(ps, [jax.ShapeDtypeStruct((3,) + p.shape[1:], p.dtype) for p in ps], {}, _pair_sems(n, 3), copies)


def _share_halves(bufs, *, name):
    n = len(bufs)

    def body(*refs):
        ins, outs = refs[:n], refs[n:2 * n]
        send_sems, recv_sems = refs[2 * n:]
        x, y, c = _mesh_pos()
        cps = []
        for i in range(n):
            h = bufs[i].shape[1] // 2
            mine = pl.ds(c * h, h)
            cp = _remote(ins[i].at[:, mine, :], outs[i].at[:, mine, :], send_sems, recv_sems, i, (x, y, 1 - c))
            cp.start()
            cps.append(cp)
        for cp in cps:
            cp.wait()

    return pl.pallas_call(
        body, name=name, in_specs=[ANY] * n, out_specs=[ANY] * n,
        out_shape=[jax.ShapeDtypeStruct(b.shape, b.dtype) for b in bufs],
        input_output_aliases={i: i for i in range(n)},
        scratch_shapes=[pltpu.SemaphoreType.DMA((n,)), pltpu.SemaphoreType.DMA((n,))])(*bufs)


def _add_own_half(full, recv, pos, *, name, rows=256):
    k4, h, n = recv.shape
    tr = _tile(h, rows, 16)
    nblk = h // tr

    def body(pos_ref, a_ref, b_ref, o_ref):
        o_ref[...] = (a_ref[...] + b_ref[...]).astype(BF16)

    grid_spec = pltpu.PrefetchScalarGridSpec(
        num_scalar_prefetch=1, grid=(k4, nblk),
        in_specs=[pl.BlockSpec((None, tr, n), lambda k, i, pos_ref: (k, pos_ref[1] * nblk + i, 0)),
                  pl.BlockSpec((None, tr, n), lambda k, i, pos_ref: (k, i, 0))],
        out_specs=pl.BlockSpec((None, tr, n), lambda k, i, pos_ref: (k, i, 0)))
    return pl.pallas_call(body, name=name, grid_spec=grid_spec, out_shape=jax.ShapeDtypeStruct(recv.shape, BF16),
                          compiler_params=_params(("parallel", "parallel")))(pos, full, recv)


def _sum_into(own, others, buf, pos, layer, *, name, rows=256):
    _, h, n = own.shape
    tr = _tile(h, rows, 16)
    nblk = h // tr

    def body(pos_ref, own_ref, oth_ref, _, o_ref):
        acc = own_ref[...].astype(F32)
        for r in range(3):
            acc = acc + oth_ref[r].astype(F32)
        o_ref[...] = acc

    grid_spec = pltpu.PrefetchScalarGridSpec(
        num_scalar_prefetch=1, grid=(nblk,),
        in_specs=[pl.BlockSpec((None, tr, n), lambda i, pos_ref: (pos_ref[0], i, 0)),
                  pl.BlockSpec((3, tr, n), lambda i, pos_ref: (0, i, 0)), ANY],
        out_specs=pl.BlockSpec((None, tr, n), lambda i, pos_ref: (layer, pos_ref[1] * nblk + i, 0)))
    return pl.pallas_call(body, name=name, grid_spec=grid_spec, out_shape=jax.ShapeDtypeStruct(buf.shape, F32),
                          input_output_aliases={3: 0}, compiler_params=_params(("parallel",)))(pos, own, others, buf)


def _sibling_pair(buf, *, name):
    def body(src_ref, out_ref, send_sem, recv_sem, local_sem):
        x, y, c = _mesh_pos()
        local = pltpu.make_async_copy(src_ref, out_ref.at[c], local_sem)
        local.start()
        cp = pltpu.make_async_remote_copy(src_ref, out_ref.at[c], send_sem, recv_sem, device_id=(x, y, 1 - c),
                                          device_id_type=MESH)
        cp.start()
        cp.wait()
        local.wait()

    return pl.pallas_call(
        body, name=name, in_specs=[ANY], out_specs=ANY, out_shape=jax.ShapeDtypeStruct((2,) + buf.shape, buf.dtype),
        scratch_shapes=[pltpu.SemaphoreType.DMA, pltpu.SemaphoreType.DMA, pltpu.SemaphoreType.DMA])(buf)


def _chip_bcast(buf, *, name):
    def body(src_ref, out_ref, send_sems, recv_sems, local_sem):
        x, y, c = _mesh_pos()
        me = 2 * x + y
        local = pltpu.make_async_copy(src_ref, out_ref.at[me], local_sem)
        local.start()
        sends = []
        for r, (px, py, _) in enumerate(_peers(x, y)):
            cp = _remote(src_ref, out_ref.at[me], send_sems, recv_sems, r, (px, py, c))
            cp.start()
            sends.append(cp)
        for r, (px, py, k) in enumerate(_peers(x, y)):
            _remote(src_ref, out_ref.at[k], send_sems, recv_sems, r, (px, py, c)).wait_recv()
        for cp in sends:
            cp.wait_send()
        local.wait()

    return pl.pallas_call(
        body, name=name, in_specs=[ANY], out_specs=ANY, out_shape=jax.ShapeDtypeStruct((4,) + buf.shape, buf.dtype),
        scratch_shapes=[pltpu.SemaphoreType.DMA((3,)), pltpu.SemaphoreType.DMA((3,)), pltpu.SemaphoreType.DMA])(buf)


def _sum_slots(buf, *, name, rows=384):
    r, n = buf.shape[-2:]
    k = int(np.prod(buf.shape[:-2]))
    tr = _tile(r, rows, SUBLANES)

    def body(b_ref, o_ref):
        acc = b_ref[0]
        for s in range(1, k):
            acc = acc + b_ref[s]
        o_ref[...] = acc

    return pl.pallas_call(
        body, name=name, grid=(r // tr,), in_specs=[pl.BlockSpec((k, tr, n), lambda i: (0, i, 0))],
        out_specs=pl.BlockSpec((tr, n), lambda i: (i, 0)), out_shape=jax.ShapeDtypeStruct((r, n), F32),
        compiler_params=_params(("parallel",)))(buf.reshape((k, r, n)))


ROW = 1024
BIG = (("w_in", 2), ("w_out", 1), ("w_cq", 1), ("w_ck", 1), ("w_cv", 1), ("w_co", 2), ("w_up", 2), ("w_down", 1))
CONV = ("lru_conv_w", "ffn_conv_w")
REPLICATED = ("norm_mix_g", "b_forget", "lru_conv_b", "lru_w_a", "lru_b_a", "lru_w_x", "lru_b_x", "lru_lambda",
              "norm_cross_g", "norm_mem_g", "norm_ffn_g", "ffn_conv_b", "rel_bias", "final_norm_g")
WEIGHTS = ('norm_mix_g', 'w_in', 'b_forget', 'lru_conv_w', 'lru_conv_b', 'lru_w_a', 'lru_b_a', 'lru_w_x', 'lru_b_x',
           'lru_lambda', 'w_out', 'norm_cross_g', 'norm_mem_g', 'w_cq', 'w_ck', 'w_cv', 'w_co', 'norm_ffn_g', 'w_up',
           'ffn_conv_w', 'ffn_conv_b', 'w_down', 'rel_bias', 'final_norm_g')
INPUTS = ("x", "mem") + WEIGHTS + ("loss_target",) + tuple("m_" + n for n in WEIGHTS) + tuple("v_" + n for n in WEIGHTS)


def _round_up(n, m):
    return -(-n // m) * m


class _Packing:
    def __init__(self, entries):
        self.entries, self.off = entries, {}
        o = 0
        for name, shape in entries:
            self.off[name] = o
            o += _round_up(int(np.prod(shape)), ROW)
        self.used = o
        self.rows = _round_up(o // ROW, SUBLANES)

    def pack(self, arrays):
        parts = []
        for name, shape in self.entries:
            n = int(np.prod(shape))
            parts.append(jnp.pad(arrays[name].reshape(n), (0, _round_up(n, ROW) - n)))
        tail = self.rows * ROW - self.used
        if tail:
            parts.append(jnp.zeros((tail,), F32))
        return jnp.concatenate(parts).reshape(self.rows, ROW)

    def unpack(self, flat, lead=()):
        out = {}
        for name, shape in self.entries:
            n = int(np.prod(shape))
            r0, nr = self.off[name] // ROW, _round_up(n, ROW) // ROW
            rows = lax.slice_in_dim(flat, r0, r0 + nr, axis=len(lead)).reshape(lead + (nr * ROW,))
            out[name] = lax.slice_in_dim(rows, 0, n, axis=len(lead)).reshape(lead + tuple(shape))
        return out


def _to_shards(g, axis):
    r, c = g.shape
    if axis == 1:
        return g.reshape(4, r // 4, c)
    return g.reshape(r, 4, c // 4).transpose(1, 0, 2)


def _from_shards(s, axis):
    _, r, c = s.shape
    if axis == 1:
        return s.reshape(4 * r, c)
    return s.transpose(1, 0, 2).reshape(r, 4 * c)


def _proj_blocks():
    blocks = []
    for mixer in (MIX_SB, MIX_FOX, MIX_DIL):
        for p in range(2):
            blocks += [ORIG_COL[mixer] + part * 2 * LANES + p * LANES for part in range(3)]
    for hf in range(2):
        blocks += [ORIG_LRU_X + hf * LANES, ORIG_LRU_G + hf * LANES]
    return blocks


def _pad_w_in(w):
    parts = [w[..., s:s + LANES] for s in _proj_blocks()]
    parts += [w[..., 1536:1540], jnp.zeros(w.shape[:-1] + (PROJ_W - COL_F - N_HEADS,), w.dtype)]
    return jnp.concatenate(parts, axis=-1)


def _unpad_w_in(wp):
    blocks = _proj_blocks()
    order = sorted(range(len(blocks)), key=lambda i: blocks[i])
    parts = []
    for i in order:
        if blocks[i] == ORIG_COL[MIX_DIL]:
            parts.append(wp[..., COL_F:COL_F + N_HEADS])
        parts.append(wp[..., i * LANES:(i + 1) * LANES])
    return jnp.concatenate(parts, axis=-1)


def _block_diag(w):
    z = jnp.zeros((HEAD_DIM, HEAD_DIM), w.dtype)
    half = lambda a, b: jnp.concatenate([jnp.concatenate([a, z], 1), jnp.concatenate([z, b], 1)], 0)
    return jnp.stack([half(w[0], w[1]), half(w[2], w[3])])


def _block_diag_grad(d):
    return jnp.stack([d[0, :HEAD_DIM, :HEAD_DIM], d[0, HEAD_DIM:, HEAD_DIM:],
                      d[1, :HEAD_DIM, :HEAD_DIM], d[1, HEAD_DIM:, HEAD_DIM:]])


def _fox_layouts(cum, nb, seq):
    tk = min(ATT_TILE, seq)
    col = cum.reshape(nb, 2, 2, seq).transpose(0, 1, 3, 2)
    row = cum.reshape(nb, 2, 2, seq // tk, tk).transpose(0, 1, 3, 2, 4)
    return col, row


def _layer_params(w, l, nb):
    lru_vec = jnp.concatenate([w["lru_conv_w"][l], w["lru_conv_b"][l][None], w["lru_b_a"][l][None],
                               w["lru_b_x"][l][None], w["lru_lambda"][l][None]], axis=0)
    ffn_cw = jnp.concatenate([w["ffn_conv_w"][l], w["ffn_conv_b"][l][None],
                              jnp.zeros((SUBLANES - 4, 2 * D_FF), F32)], axis=0)
    return dict(
        w_in=w["w_in_padded"][l], lru_vec=lru_vec,
        wa=_block_diag(w["lru_w_a"][l]).astype(BF16), wx=_block_diag(w["lru_w_x"][l]).astype(BF16),
        ffn_cw=ffn_cw, b_rows=jnp.tile(w["b_forget"][l], nb).reshape(nb * N_HEADS, 1))


NORM_ROWS = 512


def _merge_comms(comms):
    if len(comms) == 1:
        return comms[0]
    operands, out_shape, aliases, sems, spans = [], [], {}, [], []
    for cm in comms:
        aliases.update({len(operands) + i: len(out_shape) + j for i, j in cm.aliases.items()})
        spans.append((len(operands), len(out_shape), len(sems)))
        operands += list(cm.operands)
        out_shape += list(cm.out_shape)
        sems += list(cm.sems)

    def copies(ins, outs, sm):
        pairs = []
        for cm, (i0, o0, s0) in zip(comms, spans):
            pairs += cm.copies(ins[i0:i0 + len(cm.operands)], outs[o0:o0 + len(cm.out_shape)], sm[s0:s0 + len(cm.sems)])
        return pairs

    return _Comm(operands, out_shape, aliases, sems, copies)


GATHER_FIRST = ("w_in",)
GATHER_MID = ("w_out", "w_cq", "w_ck", "w_cv", "w_co")
GATHER_LAST = ("w_up", "w_down")


class _WeightGather:
    def __init__(self, slots, w, depth):
        self.slots, self.w, self.depth = slots, w, depth

    def plan(self, l, key):
        nxt = l + 1 if l + 1 < self.depth else None
        early = GATHER_FIRST + GATHER_MID
        if l == 0:
            table = {"proj": [(GATHER_MID, 0, 0)],
                     "sb_fwd": [(GATHER_MID, 0, 1), (GATHER_LAST, 0, 0)],
                     "fox_fwd": [(GATHER_LAST, 0, 1)] + ([(early, nxt, 0)] if nxt else []),
                     "dil_fwd": [(early, nxt, 1), (GATHER_LAST, nxt, 0)] if nxt else [],
                     "out": [(GATHER_LAST, nxt, 1)] if nxt else []}
        else:
            everything = early + GATHER_LAST
            table = {"sb_fwd": [(everything, nxt, 0)], "fox_fwd": [(everything, nxt, 1)]} if nxt else {}
        return table.get(key, [])

    def comm(self, l, key):
        entries = self.plan(l, key)
        if not entries:
            return None
        return _merge_comms([_gather_comm([self.slots[n] for n in names], layer, stage)
                             for names, layer, stage in entries])

    def done(self, l, key, landed):
        landed = list(landed)
        for names, layer, stage in self.plan(l, key):
            for n in names:
                self.slots[n] = landed.pop(0)
            if stage == 1:
                self.take(names, layer)

    def take(self, names, layer):
        for n, axis in BIG:
            if n in names:
                self.w[n][layer] = _from_shards(self.slots[n][layer], axis)
        if "w_in" in names:
            self.w["w_in_padded"][layer] = _pad_w_in(self.w["w_in"][layer])

    def first(self):
        for stage in (0, 1):
            got = _run_comm(_gather_comm([self.slots[n] for n in GATHER_FIRST], 0, stage), name=f"gather_first_{stage}")
            self.slots.update(zip(GATHER_FIRST, got))
        self.take(GATHER_FIRST, 0)


def _layer_fwd(x, h, mem, w, lp, l, next_g, bias, nb, gather):
    t, d = x.shape
    seq = t // nb
    tag = f"l{l}"
    sv = dict(x0=x)

    def carrying(key, fn):
        comm = gather.comm(l, key)
        res = fn(comm)
        if comm is not None:
            res, landed = res
            gather.done(l, key, landed)
        return res

    proj = carrying("proj", lambda cm: _mm(h, lp["w_in"], name=tag + "_proj", comm=cm))
    mixed, ltot = carrying("sb_fwd", lambda cm: _sb_attn_fwd(proj, lax.empty((t, d), F32), nb=nb,
                                                             name=tag + "_sb_fwd", comm=cm))
    f_rows = proj[:, COL_F:COL_F + N_HEADS].reshape(nb, seq, N_HEADS).transpose(0, 2, 1).reshape(nb * N_HEADS, seq)
    cum_col, cum_row = _fox_layouts(_fox_gate_fwd(f_rows, lp["b_rows"]), nb, seq)
    mixed, lse_fox = carrying("fox_fwd", lambda cm: _softmax_attn_fwd(
        proj, nb=nb, mode="fox", mixer=MIX_FOX, out_buf=mixed, extra=(cum_col, cum_row), name=tag + "_fox_fwd", comm=cm))
    mixed, lse_dil = carrying("dil_fwd", lambda cm: _softmax_attn_fwd(
        proj, nb=nb, mode="dil", mixer=MIX_DIL, out_buf=mixed, extra=(bias,), name=tag + "_dil_fwd", comm=cm))
    mixed = _lru_fwd(proj, lp["lru_vec"], lp["wa"], lp["wx"], mixed, nb=nb, name=tag + "_lru_fwd")
    x1, hq = carrying("out", lambda cm: _mm(mixed, w["w_out"][l], res=x, norm_g=w["norm_cross_g"][l], ti=NORM_ROWS,
                                            name=tag + "_out", comm=cm))
    memn = _rmsnorm(mem, w["norm_mem_g"][l], name=tag + "_norm_mem")
    q = _mm(hq, w["w_cq"][l], name=tag + "_cq")
    k = _mm(memn, w["w_ck"][l], name=tag + "_ck")
    v = _mm(memn, w["w_cv"][l], name=tag + "_cv")
    oc, lse_c = _softmax_attn_fwd((q, k, v), nb=nb, mode="cross", name=tag + "_cross_fwd")
    x2, hn = _mm(oc, w["w_co"][l], res=x1, norm_g=w["norm_ffn_g"][l], ti=NORM_ROWS, name=tag + "_co")
    hf = _mm(hn, w["w_up"][l], name=tag + "_up")
    act = _ffn_act(hf, lp["ffn_cw"], seq=seq, name=tag + "_ffn_act")
    if next_g is None:
        x3, h_next = _mm(act, w["w_down"][l], res=x2, name=tag + "_down"), None
    else:
        x3, h_next = _mm(act, w["w_down"][l], res=x2, norm_g=next_g, ti=NORM_ROWS, name=tag + "_down")
    sv.update(h=h, proj=proj, ltot=ltot, f_rows=f_rows, cum_col=cum_col, cum_row=cum_row, lse_fox=lse_fox,
              lse_dil=lse_dil, mixed=mixed, x1=x1, hq=hq, memn=memn, q=q, k=k, v=v, oc=oc, lse_c=lse_c, x2=x2,
              hn=hn, hf=hf, act=act)
    return x3, h_next, sv


class _PendingReduce:
    def __init__(self, full, pos, layer):
        self.names, self.full, self.pos, self.layer = list(full), list(full.values()), pos, layer

    def sibling_comm(self):
        return _reduce_sibling_comm(self.full)

    def add(self, from_sibling):
        self.partial = [_add_own_half(f, r, self.pos, name=f"l{self.layer}_reduce_add_{n}")
                        for f, r, n in zip(self.full, from_sibling, self.names)]

    def chips_comm(self):
        return _reduce_chips_comm(self.partial)

    def finish(self, others, g_shard):
        g_shard = dict(g_shard)
        for p, o, n in zip(self.partial, others, self.names):
            g_shard[n] = _sum_into(p, o, g_shard[n], self.pos, self.layer, name=f"l{self.layer}_reduce_sum_{n}")
        return g_shard


def _layer_bwd(dx3, mem, sv, w, lp, l, bias, nb, pos, pending=None, g_shard=None, reduce_early=False):
    t = dx3.shape[0]
    seq = t // nb
    tag = f"l{l}"
    g = {}
    down_rows = _tile(sv["act"].shape[1], 1408)
    if pending is None:
        g["w_down"] = _mm(sv["act"], dx3, ta=True, ti=down_rows, name=tag + "_dw_down")
    else:
        g["w_down"], from_sibling = _mm(sv["act"], dx3, ta=True, ti=down_rows, comm=pending.sibling_comm(),
                                        name=tag + "_dw_down")
        pending.add(from_sibling)
    dact = _mm(dx3, w["w_down"][l], tb=True, name=tag + "_dact")
    if pending is None:
        dhf, dcu, dcg = _ffn_bwd(sv["hf"], lp["ffn_cw"], dact, seq=seq, name=tag + "_ffn_bwd")
    else:
        (dhf, dcu, dcg), others = _ffn_bwd(sv["hf"], lp["ffn_cw"], dact, seq=seq, name=tag + "_ffn_bwd",
                                           comm=pending.chips_comm())
        g_shard = pending.finish(others, g_shard)
    dcw = jnp.concatenate([dcu, dcg], axis=1)
    g["ffn_conv_w"], g["ffn_conv_b"] = dcw[:3], dcw[3]
    g["w_up"] = _mm(sv["hn"], dhf, ta=True, halves="b", col_shards=4, name=tag + "_dw_up")
    early = _PendingReduce(_big_grad_shards(g, EARLY), pos, l) if reduce_early else None
    res = _mm(dhf, w["w_up"][l], tb=True, halves="a", norm_bwd=(sv["x2"], w["norm_ffn_g"][l], dx3), ti=NORM_ROWS,
              comm=early.sibling_comm() if early else None, name=tag + "_dhn")
    if early:
        res, from_sibling = res
        early.add(from_sibling)
    dx2, dg = res
    g["norm_ffn_g"] = dg.reshape(-1)
    g["w_co"] = _mm(sv["oc"], dx2, ta=True, col_shards=4, name=tag + "_dw_co")
    doc = _mm(dx2, w["w_co"][l], tb=True, name=tag + "_doc")
    dq, dk, dv = _softmax_attn_bwd((sv["q"], sv["k"], sv["v"]), sv["oc"], sv["lse_c"], doc, nb=nb, mode="cross",
                                   name=tag + "_cross_bwd")
    g["w_cq"] = _mm(sv["hq"], dq, ta=True, name=tag + "_dw_cq")
    g["w_ck"] = _mm(sv["memn"], dk, ta=True, name=tag + "_dw_ck")
    g["w_cv"] = _mm(sv["memn"], dv, ta=True, name=tag + "_dw_cv")
    dx1, dg = _mm(dq, w["w_cq"][l], tb=True, norm_bwd=(sv["x1"], w["norm_cross_g"][l], dx2), ti=NORM_ROWS,
                  name=tag + "_dhq")
    g["norm_cross_g"] = dg.reshape(-1)
    dmemn = _mm(dv, w["w_cv"][l], tb=True, res=_mm(dk, w["w_ck"][l], tb=True, name=tag + "_dmem_k"),
                name=tag + "_dmem_v")
    _, g["norm_mem_g"] = _rmsnorm_bwd(dmemn, mem, w["norm_mem_g"][l], None, name=tag + "_norm_mem_bwd")
    mixed, proj = sv["mixed"], sv["proj"]
    g["w_out"] = _mm(mixed, dx1, ta=True, name=tag + "_dw_out")
    dmixed = _mm(dx1, w["w_out"][l], tb=True, name=tag + "_dmixed")
    dproj = _sb_attn_bwd(proj, sv["ltot"], dmixed, lax.empty((t, PROJ_W), F32), nb=nb, name=tag + "_sb_bwd",
                         comm=early.chips_comm() if early else None)
    if early:
        dproj, others = dproj
        g_shard = early.finish(others, g_shard)
    dproj, dcum_k, dcum_q = _softmax_attn_bwd(
        proj, mixed, sv["lse_fox"], dmixed, nb=nb, mode="fox", mixer=MIX_FOX, dbuf=dproj,
        extra=(sv["cum_col"], sv["cum_row"]), name=tag + "_fox_bwd")
    dcum = (dcum_k.transpose(0, 1, 3, 2, 4).reshape(nb * N_HEADS, seq)
            + dcum_q.transpose(0, 1, 3, 2).reshape(nb * N_HEADS, seq))
    df_rows, db = _fox_gate_bwd(dcum, sv["f_rows"], lp["b_rows"])
    g["b_forget"] = db[:N_HEADS, 0]
    df = df_rows.reshape(nb, N_HEADS, seq).transpose(0, 2, 1).reshape(t, N_HEADS)
    dproj, dbias = _softmax_attn_bwd(proj, mixed, sv["lse_dil"], dmixed, nb=nb, mode="dil", mixer=MIX_DIL,
                                     dbuf=dproj, extra=(bias,), name=tag + "_dil_bwd")
    dproj, dvec, dwa, dwx = _lru_bwd(proj, lp["lru_vec"], lp["wa"], lp["wx"], dmixed, dproj, nb=nb,
                                     name=tag + "_lru_bwd")
    g["lru_conv_w"], g["lru_conv_b"], g["lru_b_a"], g["lru_b_x"], g["lru_lambda"] = (
        dvec[0:4], dvec[4], dvec[5], dvec[6], dvec[7])
    g["lru_w_a"], g["lru_w_x"] = _block_diag_grad(dwa), _block_diag_grad(dwx)
    dproj = lax.dynamic_update_slice(dproj, jnp.pad(df, ((0, 0), (0, PROJ_W - COL_F - N_HEADS))), (0, COL_F))
    g["w_in_padded"] = _mm(sv["h"], dproj, ta=True, name=tag + "_dw_in")
    dx0, dg = _mm(dproj, lp["w_in"], tb=True, norm_bwd=(sv["x0"], w["norm_mix_g"][l], dx1), ti=NORM_ROWS,
                  name=tag + "_dh")
    g["norm_mix_g"] = dg.reshape(-1)
    return dx0, g, dbias, g_shard


def _big_grad_shards(g, names):
    out = {}
    for n, axis in BIG:
        if n not in names:
            continue
        if n in ("w_up", "w_co"):
            out[n] = g[n]
        else:
            out[n] = _to_shards(_unpad_w_in(g["w_in_padded"]) if n == "w_in" else g[n], axis)
    return out


EARLY = ("w_down", "w_up")


def kernel(*args):
    a = dict(zip(INPUTS, args, strict=True))
    nb, seq, d = a["x"].shape
    depth = a["norm_mix_g"].shape[0]
    x = a["x"].reshape(nb * seq, d)
    mem = a["mem"].reshape(nb * a["mem"].shape[1], d)
    target = a["loss_target"].reshape(nb * seq, d)
    cx, cy, c = _mesh_pos()
    chip = 2 * cx + cy
    pos = jnp.stack([chip, c]).astype(jnp.int32)

    slots = {}
    for n, _ in BIG:
        own = a[n].astype(BF16)[:, None]
        slots[n] = lax.dynamic_update_slice(lax.empty((depth, 4) + own.shape[2:], BF16), own, (0, chip, 0, 0))
    w = {n: a[n] for n in REPLICATED}
    w.update({n: {} for n, _ in BIG}, w_in_padded={})
    gather = _WeightGather(slots, w, depth)
    gather.first()
    cpk = _Packing([(n, a[n].shape) for n in CONV])
    conv = cpk.unpack(_chip_bcast(cpk.pack({n: a[n] for n in CONV}), name="gather_conv"), lead=(4,))
    for n in CONV:
        w[n] = jnp.moveaxis(conv[n], 0, 2).reshape(a[n].shape[:2] + (4 * a[n].shape[2],))

    bias = _dil_bias(w["rel_bias"], seq)
    lps, saved = [], []
    h = _rmsnorm(x, w["norm_mix_g"][0], name="l0_norm_mix")
    for l in range(depth):
        lps.append(_layer_params(w, l, nb))
        x, h, sv = _layer_fwd(x, h, mem, w, lps[l], l, w["norm_mix_g"][l + 1] if l + 1 < depth else None, bias, nb,
                              gather)
        saved.append(sv)
    loss, dx, dg_final = _loss_head(x, w["final_norm_g"], target)
    small_g = [None] * depth
    dbias, pending = None, None
    g_shard = {n: lax.empty(a[n].shape, F32) for n, _ in BIG}
    for l in reversed(range(depth)):
        bottom = l == 0
        dx, g, db, g_shard = _layer_bwd(dx, mem, saved[l], w, lps[l], l, bias, nb, pos, pending=pending,
                                        g_shard=g_shard, reduce_early=bottom)
        dbias = db if dbias is None else dbias + db
        small_g[l] = g
        left = [n for n, _ in BIG if not (bottom and n in EARLY)]
        pending = _PendingReduce(_big_grad_shards(g, left), pos, l)
    pending.add(_run_comm(pending.sibling_comm(), name="reduce_sibling"))
    g_shard = pending.finish(_run_comm(pending.chips_comm(), name="reduce_chips"), g_shard)
    names = [n for n, _ in BIG]
    g_shard = dict(zip(names, _share_halves([g_shard[n] for n in names], name="reduce_share")))
    out = {}
    for n in names:
        delta, new_m, new_v = _adamw(a[n], g_shard[n], a["m_" + n], a["v_" + n], name="adamw_" + n)
        out[n] = (g_shard[n], delta, new_m, new_v)

    grads = {n: jnp.stack([small_g[l][n] for l in range(depth)]) for n in REPLICATED + CONV
             if n not in ("rel_bias", "final_norm_g")}
    grads["rel_bias"] = _dil_bias_bwd(dbias, seq)
    grads["final_norm_g"] = dg_final
    grads["loss"] = loss.reshape(1)
    spk = _Packing([(n, grads[n].shape) for n in REPLICATED + CONV + ("loss",)])
    s_all = _chip_bcast(_sibling_pair(spk.pack(grads), name="small_sibling"), name="small_chips")
    total = spk.unpack(_sum_slots(s_all, name="small_sum"))
    for n in CONV:
        width = a[n].shape[2]
        total[n] = lax.dynamic_slice_in_dim(total[n], chip * width, width, axis=2)
    apk = _Packing([(n, a[n].shape) for n in REPLICATED + CONV])
    s_out = _adamw(*[apk.pack(src)[None] for src in (
        {n: a[n] for n in REPLICATED + CONV}, total, {n: a["m_" + n] for n in REPLICATED + CONV},
        {n: a["v_" + n] for n in REPLICATED + CONV})], name="adamw_small")
    s_delta, s_m, s_v = [apk.unpack(o[0]) for o in s_out]
    for n in REPLICATED + CONV:
        out[n] = (total[n], s_delta[n], s_m[n], s_v[n])

    return (total["loss"].reshape(()), dx.reshape(nb, seq, d), *[out[n][0] for n in WEIGHTS],
            *[out[n][1] for n in WEIGHTS], *[out[n][2] for n in WEIGHTS], *[out[n][3] for n in WEIGHTS])
```

```python
import math

import numpy as np
import jax
import jax.numpy as jnp
from jax import lax
from jax.experimental import pallas as pl
from jax.experimental.pallas import tpu as pltpu

F32 = jnp.float32
BF16 = jnp.bfloat16

HEAD_DIM = 64
N_HEADS = 4
N_IN = 2820
D_FF = 2816
LRU_C = 8.0
EPS = 1e-6
NUM_BUCKETS = 32
MAX_DISTANCE = 2048
DILATED_PATTERNS = ((128, 1), (512, 4), (2048, 16))
ADAM_LR, ADAM_B1, ADAM_B2, ADAM_EPS, ADAM_WD, ADAM_STEP = 0.001, 0.9, 0.999, 1e-08, 0.01, 10

LANES = 128
SUBLANES = 8
VMEM_LIMIT = 48 * 1024 * 1024

PROJ_W = 3072
PAIR_W = 3 * LANES
LRU_W = 2 * LANES
COL_LRU = 6 * PAIR_W
COL_F = COL_LRU + 2 * LRU_W
MIX_SB, MIX_FOX, MIX_DIL, MIX_LRU = 0, 1, 2, 3
ORIG_COL = {MIX_SB: 0, MIX_FOX: 768, MIX_DIL: 1540}
ORIG_LRU_X, ORIG_LRU_G = 2308, 2564

ATT_TILE = 256
MASKED = -1e30
SCALE = HEAD_DIM ** -0.5

NT_DIMS = (((1,), (1,)), ((), ()))
TN_DIMS = (((0,), (0,)), ((), ()))

MESH = pl.DeviceIdType.MESH
ANY = pl.BlockSpec(memory_space=pl.ANY)


def _params(sem):
    return pltpu.CompilerParams(dimension_semantics=sem, vmem_limit_bytes=VMEM_LIMIT)


def _tile(n, target, unit=LANES):
    if n <= target:
        return n
    t = (target // unit) * unit
    while t > unit and n % t:
        t -= unit
    assert n % t == 0, (n, target, unit)
    return t


def _mm(a, b, *, ta=False, tb=False, res=None, col_shards=1, halves=None, norm_g=None, norm_bwd=None, comm=None,
        name, ti=1024, tj=1408, tc=1408):
    if halves == "a":
        m, kc = a.shape[1], 2 * a.shape[2]
    else:
        m, kc = (a.shape[1], a.shape[0]) if ta else a.shape
    if halves == "b":
        n = 2 * b.shape[2]
        assert b.shape[1] == kc
    else:
        n = b.shape[0] if tb else b.shape[1]
        assert (b.shape[1] if tb else b.shape[0]) == kc
    assert n % col_shards == 0
    n_blk = n // (2 if halves == "b" else col_shards)
    k_blk = kc // 2 if halves == "a" else kc
    ti, tj, tc = (_tile(m, ti, LANES if ta else SUBLANES), _tile(n_blk, tj),
                  _tile(k_blk, tc, SUBLANES if ta and tb else LANES))
    per_shard, per_half_j, per_half_k = n // col_shards // tj, n_blk // tj, k_blk // tc
    nk = kc // tc
    dims = (((0 if ta else 1,), (1 if tb else 0,)), ((), ()))
    rows_whole = norm_g is not None or norm_bwd is not None
    assert not rows_whole or (tj == n and col_shards == 1)
    n_extra = (res is not None) + (norm_g is not None) + (3 if norm_bwd is not None else 0)
    n_out = 2 if rows_whole else 1

    def finish(val, ex, outs):
        if res is not None:
            val = ex[0][...] + val
        if norm_g is not None:
            outs[0][...] = val
            outs[1][...] = (_xhat(val) * ex[-1][...]).astype(BF16)
        elif norm_bwd is not None:
            x_ref, g_ref, r_ref = ex[-3:]
            dx, dgr = _norm_bwd_rows(val, x_ref[...], g_ref[...])
            outs[0][...] = r_ref[...] + dx

            @pl.when(pl.program_id(0) == 0)
            def _():
                outs[1][...] = jnp.zeros_like(outs[1])

            outs[1][...] += jnp.sum(dgr, axis=0, keepdims=True)
        else:
            outs[0][...] = val

    def body(*refs):
        a_ref, b_ref = refs[:2]
        ex = refs[2:2 + n_extra]
        outs = refs[2 + n_extra:2 + n_extra + n_out]
        part = lax.dot_general(a_ref[...].astype(BF16), b_ref[...].astype(BF16), dims, preferred_element_type=F32)
        if nk == 1:
            finish(part, ex, outs)
            return
        acc_ref = refs[-1]
        k = pl.program_id(2)

        @pl.when(k == 0)
        def _():
            acc_ref[...] = part

        @pl.when(k > 0)
        def _():
            acc_ref[...] += part

        @pl.when(k == nk - 1)
        def _():
            finish(acc_ref[...], ex, outs)

    if halves == "a":
        a_spec = pl.BlockSpec((None, ti, tc), lambda i, j, k: (k // per_half_k, i, k % per_half_k))
    elif ta:
        a_spec = pl.BlockSpec((tc, ti), lambda i, j, k: (k, i))
    else:
        a_spec = pl.BlockSpec((ti, tc), lambda i, j, k: (i, k))
    if halves == "b":
        b_spec = pl.BlockSpec((None, tc, tj), lambda i, j, k: (j // per_half_j, k, j % per_half_j))
    elif tb:
        b_spec = pl.BlockSpec((tj, tc), lambda i, j, k: (j, k))
    else:
        b_spec = pl.BlockSpec((tc, tj), lambda i, j, k: (k, j))
    o_spec = pl.BlockSpec((ti, tj), lambda i, j, k: (i, j))
    vec = pl.BlockSpec((1, tj), lambda i, j, k: (0, 0))
    in_specs, args = [a_spec, b_spec], [a, b]
    out_specs, out_shape = [o_spec], [jax.ShapeDtypeStruct((m, n), F32)]
    if res is not None:
        in_specs.append(o_spec)
        args.append(res)
    if norm_g is not None:
        in_specs.append(vec)
        args.append(norm_g.reshape(1, n))
        out_specs.append(o_spec)
        out_shape.append(jax.ShapeDtypeStruct((m, n), BF16))
    if norm_bwd is not None:
        x, g, dres = norm_bwd
        in_specs += [o_spec, vec, o_spec]
        args += [x, g.reshape(1, n), dres]
        out_specs.append(vec)
        out_shape.append(jax.ShapeDtypeStruct((1, n), F32))
    if col_shards > 1:
        assert n_extra == 0
        out_specs = [pl.BlockSpec((None, ti, tj), lambda i, j, k: (j // per_shard, i, j % per_shard))]
        out_shape = [jax.ShapeDtypeStruct((col_shards, m, n // col_shards), F32)]
    sem = ("arbitrary",) * 3 if norm_bwd is not None else ("parallel", "parallel", "arbitrary")
    out, carried = _pallas(body, name=name, grid=(m // ti, n // tj, nk), in_specs=in_specs, out_specs=out_specs,
                           out_shape=out_shape, args=args, scratch=[] if nk == 1 else [pltpu.VMEM((ti, tj), F32)],
                           sem=sem, comm=comm)
    out = out if rows_whole else out[0]
    return out if comm is None else (out, carried)


def _xhat(x):
    return x * lax.rsqrt(jnp.mean(x * x, axis=-1, keepdims=True) + EPS)


def _norm_bwd_rows(dy, x, g):
    rstd = lax.rsqrt(jnp.mean(x * x, axis=-1, keepdims=True) + EPS)
    xh = x * rstd
    dxh = dy * g
    dx = rstd * (dxh - xh * jnp.mean(dxh * xh, axis=-1, keepdims=True))
    return dx, dy * xh


def _rmsnorm(x, g, *, name, rows=512):
    t, d = x.shape
    tr = _tile(t, rows, 2 * SUBLANES)

    def body(x_ref, g_ref, o_ref):
        o_ref[...] = (_xhat(x_ref[...]) * g_ref[...]).astype(BF16)

    return pl.pallas_call(
        body, name=name, grid=(t // tr,),
        in_specs=[pl.BlockSpec((tr, d), lambda i: (i, 0)), pl.BlockSpec((1, d), lambda i: (0, 0))],
        out_specs=pl.BlockSpec((tr, d), lambda i: (i, 0)), out_shape=jax.ShapeDtypeStruct((t, d), BF16),
        compiler_params=_params(("parallel",)))(x, g.reshape(1, d))


def _rmsnorm_bwd(dy, x, g, dres, *, name, rows=512):
    t, d = x.shape
    tr = _tile(t, rows, SUBLANES)

    def body(*refs):
        if dres is None:
            dy_ref, x_ref, g_ref, dx_ref, dg_ref = refs
        else:
            dy_ref, x_ref, g_ref, r_ref, dx_ref, dg_ref = refs
        dx, dgr = _norm_bwd_rows(dy_ref[...], x_ref[...], g_ref[...])
        dx_ref[...] = dx if dres is None else r_ref[...] + dx

        @pl.when(pl.program_id(0) == 0)
        def _():
            dg_ref[...] = jnp.zeros_like(dg_ref)

        dg_ref[...] += jnp.sum(dgr, axis=0, keepdims=True)

    row = pl.BlockSpec((tr, d), lambda i: (i, 0))
    vec = pl.BlockSpec((1, d), lambda i: (0, 0))
    in_specs = [row, row, vec] + ([] if dres is None else [row])
    args = (dy, x, g.reshape(1, d)) + (() if dres is None else (dres,))
    dx, dg = pl.pallas_call(
        body, name=name, grid=(t // tr,), in_specs=in_specs, out_specs=[row, vec],
        out_shape=[jax.ShapeDtypeStruct((t, d), F32), jax.ShapeDtypeStruct((1, d), F32)],
        compiler_params=_params(("arbitrary",)))(*args)
    return dx, dg.reshape(d)


def _loss_head(x, g, target, *, rows=512):
    t, d = x.shape
    tr = _tile(t, rows, SUBLANES)

    def body(x_ref, g_ref, t_ref, dx_ref, dg_ref, loss_ref):
        x_, g_ = x_ref[...], g_ref[...]
        err = _xhat(x_) * g_ - t_ref[...]
        dx, dgr = _norm_bwd_rows(err * (1.0 / d), x_, g_)
        dx_ref[...] = dx

        @pl.when(pl.program_id(0) == 0)
        def _():
            dg_ref[...] = jnp.zeros_like(dg_ref)
            loss_ref[...] = jnp.zeros_like(loss_ref)

        dg_ref[...] += jnp.sum(dgr, axis=0, keepdims=True)
        loss_ref[...] += 0.5 * jnp.sum(jnp.mean(err * err, axis=-1, keepdims=True), axis=0, keepdims=True)

    row = pl.BlockSpec((tr, d), lambda i: (i, 0))
    vec = pl.BlockSpec((1, d), lambda i: (0, 0))
    one = pl.BlockSpec((1, 1), lambda i: (0, 0))
    dx, dg, loss = pl.pallas_call(
        body, name="loss_head", grid=(t // tr,), in_specs=[row, vec, row], out_specs=[row, vec, one],
        out_shape=[jax.ShapeDtypeStruct((t, d), F32), jax.ShapeDtypeStruct((1, d), F32),
                   jax.ShapeDtypeStruct((1, 1), F32)],
        compiler_params=_params(("arbitrary",)))(x, g.reshape(1, d), target)
    return loss.reshape(()), dx, dg.reshape(d)


def _head_masks(shape):
    lane = lax.broadcasted_iota(jnp.int32, shape, len(shape) - 1)
    return lane < HEAD_DIM, lane >= HEAD_DIM


def _split_heads(x):
    m0, m1 = _head_masks(x.shape)
    zero = jnp.zeros_like(x)
    return jnp.where(m0, x, zero), jnp.where(m1, x, zero)


def _lane_pair(a0, a1, rows):
    m0, _ = _head_masks((rows, LANES))
    return jnp.where(m0, a0, a1)


def _qkv_readers(refs, packed):
    if packed:
        (r,) = refs
        return tuple((lambda r0, n, s=s: r[pl.ds(r0, n), s * LANES:(s + 1) * LANES]) for s in range(3))
    return tuple((lambda r0, n, ref=ref: ref[pl.ds(r0, n), :]) for ref in refs)


def _pair_spec(seq, col0, width=LANES):
    return pl.BlockSpec((seq, width), lambda p, b: (b, col0 + p))


def _fox_specs(seq, nk, tk):
    return [pl.BlockSpec((None, None, seq, 2), lambda p, b: (b, p, 0, 0)),
            pl.BlockSpec((None, None, nk, 2, tk), lambda p, b: (b, p, 0, 0, 0))]


def _softmax_attn_fwd(src, *, nb, mode, mixer=None, out_buf=None, extra=(), name, comm=None):
    packed = mode != "cross"
    n_src = 1 if packed else 3
    seq_q = (src if packed else src[0]).shape[0] // nb
    seq_k = seq_q if packed else src[1].shape[0] // nb
    tq, tk = min(ATT_TILE, seq_q), min(ATT_TILE, seq_k)
    nq, nk = seq_q // tq, seq_k // tk
    n_ex = len(extra)

    def body(*refs):
        q_at, k_at, v_at = _qkv_readers(refs[:n_src], packed)
        ex = refs[n_src:n_src + n_ex]
        o_ref, lse_ref = refs[-2:]

        def q_tile(i, _):
            r0 = pl.multiple_of(i * tq, tq)
            qm = _split_heads((q_at(r0, tq) * SCALE).astype(BF16))
            if mode == "fox":
                cq = ex[0][pl.ds(r0, tq), :]
                row = r0 + lax.broadcasted_iota(jnp.int32, (tq, tk), 0)

            def k_tile(j, carry, diagonal=False):
                m, l, acc = carry
                c0 = pl.multiple_of(j * tk, tk)
                kt = k_at(c0, tk).astype(BF16)
                vm = _split_heads(v_at(c0, tk).astype(BF16))
                if mode == "fox":
                    ck = ex[1][j]
                hs = range(2)
                s = [lax.dot_general(qm[h], kt, NT_DIMS, preferred_element_type=F32) for h in hs]
                if mode == "fox":
                    s = [s[h] + cq[:, h:h + 1] - ck[h:h + 1, :] for h in hs]
                    if diagonal:
                        keep = (c0 + lax.broadcasted_iota(jnp.int32, (tq, tk), 1)) <= row
                        s = [jnp.where(keep, s[h], MASKED) for h in hs]
                elif mode == "dil":
                    s = [s[h] + ex[0][h, i - j] for h in hs]
                new_m = [jnp.maximum(m[h], jnp.max(s[h], axis=-1, keepdims=True)) for h in hs]
                p = [jnp.exp(s[h] - new_m[h]) for h in hs]
                alpha = [jnp.exp(m[h] - new_m[h]) for h in hs]
                new_l = [alpha[h] * l[h] + jnp.sum(p[h], axis=-1, keepdims=True) for h in hs]
                pv = [jnp.dot(p[h].astype(BF16), vm[h], preferred_element_type=F32) for h in hs]
                acc = acc * _lane_pair(alpha[0], alpha[1], tq) + (pv[0] + pv[1])
                return tuple(new_m), tuple(new_l), acc

            init = ((jnp.full((tq, 1), MASKED, F32),) * 2, (jnp.zeros((tq, 1), F32),) * 2,
                    jnp.zeros((tq, LANES), F32))
            if mode == "fox":
                m, l, acc = k_tile(i, lax.fori_loop(0, i, k_tile, init), True)
            else:
                m, l, acc = lax.fori_loop(0, i + 1 if packed else nk, k_tile, init)
            o_ref[pl.ds(r0, tq), :] = acc / _lane_pair(l[0], l[1], tq)
            lse_ref[pl.ds(r0, tq), :] = _lane_pair(m[0] + jnp.log(l[0]), m[1] + jnp.log(l[1]), tq)
            return 0

        lax.fori_loop(0, nq, q_tile, 0)

    lse_shape = jax.ShapeDtypeStruct((nb * seq_q, 2 * LANES), F32)
    if packed:
        in_specs, args = [_pair_spec(seq_q, 2 * mixer, PAIR_W)], [src]
        in_specs += _fox_specs(seq_q, nk, tk) if mode == "fox" else [
            pl.BlockSpec((None, 2, nq, tq, tk), lambda p, b: (p, 0, 0, 0, 0))]
        args += list(extra) + [out_buf]
        in_specs.append(ANY)
        out_specs = [_pair_spec(seq_q, 2 * mixer), _pair_spec(seq_q, 0)]
        out_shape = [jax.ShapeDtypeStruct(out_buf.shape, F32), lse_shape]
        aliases = {len(args) - 1: 0}
    else:
        in_specs = [_pair_spec(seq_q, 0), _pair_spec(seq_k, 0), _pair_spec(seq_k, 0)]
        args = list(src)
        out_specs = [_pair_spec(seq_q, 0), _pair_spec(seq_q, 0)]
        out_shape = [lse_shape, lse_shape]
        aliases = {}
    out, carried = _pallas(body, name=name, grid=(2, nb), in_specs=in_specs, out_specs=out_specs, out_shape=out_shape,
                           args=args, aliases=aliases, sem=("parallel", "arbitrary"), comm=comm)
    return out if comm is None else (out, carried)


def _softmax_attn_bwd(src, o, lse, do, *, nb, mode, mixer=None, dbuf=None, extra=(), name):
    packed = mode != "cross"
    n_src = 1 if packed else 3
    seq_q = (src if packed else src[0]).shape[0] // nb
    seq_k = seq_q if packed else src[1].shape[0] // nb
    tq, tk = min(ATT_TILE, seq_q), min(ATT_TILE, seq_k)
    nq, nk = seq_q // tq, seq_k // tk
    n_ex = len(extra)
    n_in = n_src + 3 + n_ex + (1 if packed else 0)

    def body(*refs):
        q_at, k_at, v_at = _qkv_readers(refs[:n_src], packed)
        o_ref, lse_ref, do_ref = refs[n_src:n_src + 3]
        ex = refs[n_src + 3:n_src + 3 + n_ex]
        outs = refs[n_in:]
        if packed:
            d_ref = outs[0]
            dq_w = lambda r0, val: d_ref.__setitem__((pl.ds(r0, tq), slice(0, LANES)), val)
            dk_ref = d_ref.at[:, LANES:2 * LANES]
            dv_ref = d_ref.at[:, 2 * LANES:3 * LANES]
        else:
            dq_ref, dk_ref, dv_ref = outs[:3]
            dq_w = lambda r0, val: dq_ref.__setitem__((pl.ds(r0, tq), slice(None)), val)
        dk_ref[...] = jnp.zeros((seq_k, LANES), F32)
        dv_ref[...] = jnp.zeros((seq_k, LANES), F32)
        if mode == "fox":
            dcum_ref, dcq_ref = outs[-2:]
            dcum_ref[...] = jnp.zeros_like(dcum_ref)
        if mode == "dil":
            dbias_ref = outs[-1]

            @pl.when(pl.program_id(1) == 0)
            def _():
                dbias_ref[...] = jnp.zeros_like(dbias_ref)

        def q_tile(i, _):
            r0 = pl.multiple_of(i * tq, tq)
            qm = _split_heads((q_at(r0, tq) * SCALE).astype(BF16))
            do_f = do_ref[pl.ds(r0, tq), :]
            dom = _split_heads(do_f.astype(BF16))
            dd = _split_heads(do_f * o_ref[pl.ds(r0, tq), :])
            delta = [jnp.sum(dd[h], axis=-1, keepdims=True) for h in range(2)]
            lse_t = lse_ref[pl.ds(r0, tq), :]
            lse_h = [lse_t[:, 0:1], lse_t[:, HEAD_DIM:HEAD_DIM + 1]]
            if mode == "fox":
                cq = ex[0][pl.ds(r0, tq), :]
                row = r0 + lax.broadcasted_iota(jnp.int32, (tq, tk), 0)

            def k_tile(j, carry, diagonal=False):
                dq, rs = carry
                c0 = pl.multiple_of(j * tk, tk)
                kt = k_at(c0, tk).astype(BF16)
                vt = v_at(c0, tk).astype(BF16)
                km = _split_heads(kt)
                if mode == "fox":
                    ck = ex[1][j]
                hs = range(2)
                s = [lax.dot_general(qm[h], kt, NT_DIMS, preferred_element_type=F32) for h in hs]
                dp = [lax.dot_general(dom[h], vt, NT_DIMS, preferred_element_type=F32) for h in hs]
                if mode == "fox":
                    s = [s[h] + cq[:, h:h + 1] - ck[h:h + 1, :] for h in hs]
                    if diagonal:
                        keep = (c0 + lax.broadcasted_iota(jnp.int32, (tq, tk), 1)) <= row
                        s = [jnp.where(keep, s[h], MASKED) for h in hs]
                elif mode == "dil":
                    s = [s[h] + ex[0][h, i - j] for h in hs]
                p = [jnp.exp(s[h] - lse_h[h]) for h in hs]
                ds = [p[h] * (dp[h] - delta[h]) for h in hs]
                dsb = [ds[h].astype(BF16) for h in hs]
                pb = [p[h].astype(BF16) for h in hs]
                dq = dq + (jnp.dot(dsb[0], km[0], preferred_element_type=F32)
                           + jnp.dot(dsb[1], km[1], preferred_element_type=F32))
                dk_t = (lax.dot_general(dsb[0], qm[0], TN_DIMS, preferred_element_type=F32)
                        + lax.dot_general(dsb[1], qm[1], TN_DIMS, preferred_element_type=F32))
                dv_t = (lax.dot_general(pb[0], dom[0], TN_DIMS, preferred_element_type=F32)
                        + lax.dot_general(pb[1], dom[1], TN_DIMS, preferred_element_type=F32))
                if mode == "fox":
                    for h in hs:
                        dcum_ref[j, h:h + 1, :] -= jnp.sum(ds[h], axis=0, keepdims=True)
                    rs = tuple(rs[h] + jnp.sum(ds[h], axis=-1, keepdims=True) for h in hs)
                elif mode == "dil":
                    for h in hs:
                        dbias_ref[h, i - j] += ds[h]
                dk_ref[pl.ds(c0, tk), :] += dk_t
                dv_ref[pl.ds(c0, tk), :] += dv_t
                return dq, rs

            zero = (jnp.zeros((tq, 1), F32),) * 2
            init = (jnp.zeros((tq, LANES), F32), zero)
            if mode == "fox":
                dq, rs = k_tile(i, lax.fori_loop(0, i, k_tile, init), True)
            else:
                dq, rs = lax.fori_loop(0, i + 1 if packed else nk, k_tile, init)
            dq_w(r0, dq * SCALE)
            if mode == "fox":
                dcq_ref[pl.ds(r0, tq), :] = jnp.where(lax.broadcasted_iota(jnp.int32, (tq, 2), 1) == 0, rs[0], rs[1])
            return 0

        lax.fori_loop(0, nq, q_tile, 0)

    if packed:
        in_specs = [_pair_spec(seq_q, 2 * mixer, PAIR_W), _pair_spec(seq_q, 2 * mixer), _pair_spec(seq_q, 0),
                    _pair_spec(seq_q, 2 * mixer)]
        args = [src, o, lse, do]
        out_specs = [_pair_spec(seq_q, 2 * mixer, PAIR_W)]
        out_shape = [jax.ShapeDtypeStruct(dbuf.shape, F32)]
        if mode == "fox":
            in_specs += _fox_specs(seq_q, nk, tk)
            out_specs += [_fox_specs(seq_q, nk, tk)[1], _fox_specs(seq_q, nk, tk)[0]]
            out_shape += [jax.ShapeDtypeStruct((nb, 2, nk, 2, tk), F32), jax.ShapeDtypeStruct((nb, 2, seq_q, 2), F32)]
        else:
            tiles = pl.BlockSpec((None, 2, nq, tq, tk), lambda p, b: (p, 0, 0, 0, 0))
            in_specs.append(tiles)
            out_specs.append(tiles)
            out_shape.append(jax.ShapeDtypeStruct((2, 2, nq, tq, tk), F32))
        args += list(extra) + [dbuf]
        in_specs.append(ANY)
        aliases = {len(args) - 1: 0}
    else:
        sq, sk = _pair_spec(seq_q, 0), _pair_spec(seq_k, 0)
        in_specs, args = [sq, sk, sk, sq, sq, sq], list(src) + [o, lse, do]
        out_specs = [sq, sk, sk]
        out_shape = [jax.ShapeDtypeStruct((nb * seq_q, 2 * LANES), F32)] + [
            jax.ShapeDtypeStruct((nb * seq_k, 2 * LANES), F32)] * 2
        aliases = {}
    return pl.pallas_call(
        body, name=name, grid=(2, nb), in_specs=in_specs, out_specs=out_specs, out_shape=out_shape,
        input_output_aliases=aliases, compiler_params=_params(("parallel", "arbitrary")))(*args)


def _log_sigmoid(z):
    return jnp.minimum(z, 0.0) - jnp.log(1.0 + jnp.exp(-jnp.abs(z)))


def _split_bf16(x):
    hi = x.astype(BF16)
    return hi, (x - hi.astype(F32)).astype(BF16)


def _tri(n, fn):
    r = lax.broadcasted_iota(jnp.int32, (n, n), 0)
    c = lax.broadcasted_iota(jnp.int32, (n, n), 1)
    return jnp.where(fn(r, c), 1.0, 0.0).astype(BF16)


def _sb_attn_fwd(proj, out_buf, *, nb, name, comm=None):
    seq = proj.shape[0] // nb
    tq = tk = min(ATT_TILE, seq)
    nq = seq // tq

    def body(qkv_ref, _, o_ref, lt_ref):
        rd = [_qkv_readers((qkv_ref.at[:, pr * PAIR_W:(pr + 1) * PAIR_W],), True) for pr in range(2)]
        after = _tri(tk, lambda r, c: r > c)
        ch = [(pr, h) for pr in range(2) for h in range(2)]

        def q_tile(i, _):
            r0 = pl.multiple_of(i * tq, tq)
            qm = [_split_heads((rd[pr][0](r0, tq) * SCALE).astype(BF16)) for pr in range(2)]
            row = r0 + lax.broadcasted_iota(jnp.int32, (tq, tk), 0)

            def k_tile(j, carry, diagonal):
                c, acc = carry
                c0 = pl.multiple_of(j * tk, tk)
                kt = [rd[pr][1](c0, tk).astype(BF16) for pr in range(2)]
                vm = [_split_heads(rd[pr][2](c0, tk).astype(BF16)) for pr in range(2)]
                if diagonal:
                    strict = (c0 + lax.broadcasted_iota(jnp.int32, (tq, tk), 1)) < row
                ns = range(len(ch))
                z = [lax.dot_general(qm[pr][h], kt[pr], NT_DIMS, preferred_element_type=F32) for pr, h in ch]
                ls = [_log_sigmoid(z[n]) for n in ns]
                lk = [ls[n] - z[n] for n in ns]
                if diagonal:
                    lk = [jnp.where(strict, lk[n], 0.0) for n in ns]
                parts = [_split_bf16(lk[n]) for n in ns]
                sfx = [jnp.dot(parts[n][0], after, preferred_element_type=F32)
                       + jnp.dot(parts[n][1], after, preferred_element_type=F32) for n in ns]
                att = [jnp.exp(ls[n] + sfx[n] + c[n]) for n in ns]
                if diagonal:
                    att = [jnp.where(strict, att[n], 0.0) for n in ns]
                acc = tuple(acc[pr] + (jnp.dot(att[2 * pr].astype(BF16), vm[pr][0], preferred_element_type=F32)
                                       + jnp.dot(att[2 * pr + 1].astype(BF16), vm[pr][1], preferred_element_type=F32))
                            for pr in range(2))
                return tuple(c[n] + jnp.sum(lk[n], axis=-1, keepdims=True) for n in ns), acc

            init = ((jnp.zeros((tq, 1), F32),) * 4, (jnp.zeros((tq, LANES), F32),) * 2)
            c, acc = lax.fori_loop(1, i + 1, lambda jj, cr: k_tile(i - jj, cr, False), k_tile(i, init, True))
            for pr in range(2):
                o_ref[pl.ds(r0, tq), pr * LANES:(pr + 1) * LANES] = acc[pr]
                lt_ref[pl.ds(r0, tq), pr * LANES:(pr + 1) * LANES] = _lane_pair(c[2 * pr], c[2 * pr + 1], tq)
            return 0

        lax.fori_loop(0, nq, q_tile, 0)

    both = lambda width, col: pl.BlockSpec((seq, 2 * width), lambda b: (b, col))
    out, carried = _pallas(
        body, name=name, grid=(nb,), in_specs=[both(PAIR_W, MIX_SB), ANY],
        out_specs=[both(LANES, MIX_SB), both(LANES, 0)],
        out_shape=[jax.ShapeDtypeStruct(out_buf.shape, F32), jax.ShapeDtypeStruct((nb * seq, 2 * LANES), F32)],
        args=[proj, out_buf], aliases={1: 0}, sem=("arbitrary",), comm=comm)
    return out if comm is None else (out, carried)


def _sb_attn_bwd(proj, ltot, do, dbuf, *, nb, name, comm=None):
    seq = proj.shape[0] // nb
    tq = tk = min(ATT_TILE, seq)
    nq = seq // tq

    def body(qkv_ref, lt_ref, do_ref, _, d_ref):
        rd = [_qkv_readers((qkv_ref.at[:, pr * PAIR_W:(pr + 1) * PAIR_W],), True) for pr in range(2)]
        upto = _tri(tk, lambda r, c: r <= c)
        before = _tri(tk, lambda r, c: r < c)
        dk_ref = [d_ref.at[:, pr * PAIR_W + LANES:pr * PAIR_W + 2 * LANES] for pr in range(2)]
        dv_ref = [d_ref.at[:, pr * PAIR_W + 2 * LANES:(pr + 1) * PAIR_W] for pr in range(2)]
        for ref in dk_ref + dv_ref:
            ref[...] = jnp.zeros((seq, LANES), F32)
        ch = [(pr, h) for pr in range(2) for h in range(2)]

        def q_tile(i, _):
            r0 = pl.multiple_of(i * tq, tq)
            qm = [_split_heads((rd[pr][0](r0, tq) * SCALE).astype(BF16)) for pr in range(2)]
            dom = [_split_heads(do_ref[pl.ds(r0, tq), pr * LANES:(pr + 1) * LANES].astype(BF16)) for pr in range(2)]
            lt_t = lt_ref[pl.ds(r0, tq), :]
            lt_h = [lt_t[:, pr * LANES + h * HEAD_DIM:pr * LANES + h * HEAD_DIM + 1] for pr, h in ch]
            row = r0 + lax.broadcasted_iota(jnp.int32, (tq, tk), 0)

            def k_tile(j, carry, diagonal):
                pc, qc, dq = carry
                c0 = pl.multiple_of(j * tk, tk)
                kt = [rd[pr][1](c0, tk).astype(BF16) for pr in range(2)]
                vt = [rd[pr][2](c0, tk).astype(BF16) for pr in range(2)]
                km = [_split_heads(kt[pr]) for pr in range(2)]
                if diagonal:
                    strict = (c0 + lax.broadcasted_iota(jnp.int32, (tq, tk), 1)) < row
                ns = range(len(ch))
                z = [lax.dot_general(qm[pr][h], kt[pr], NT_DIMS, preferred_element_type=F32) for pr, h in ch]
                da = [lax.dot_general(dom[pr][h], vt[pr], NT_DIMS, preferred_element_type=F32) for pr, h in ch]
                ls = [_log_sigmoid(z[n]) for n in ns]
                lk = [ls[n] - z[n] for n in ns]
                if diagonal:
                    lk = [jnp.where(strict, lk[n], 0.0) for n in ns]
                parts = [_split_bf16(lk[n]) for n in ns]
                pin = [jnp.dot(parts[n][0], upto, preferred_element_type=F32)
                       + jnp.dot(parts[n][1], upto, preferred_element_type=F32) for n in ns]
                att = [jnp.exp(ls[n] + (lt_h[n] - pc[n] - pin[n])) for n in ns]
                if diagonal:
                    att = [jnp.where(strict, att[n], 0.0) for n in ns]
                dg = [att[n] * da[n] for n in ns]
                qx = [qc[n] + jnp.dot(dg[n].astype(BF16), before, preferred_element_type=F32) for n in ns]
                sig = [jnp.exp(ls[n]) for n in ns]
                dz = [dg[n] * (1.0 - sig[n]) - sig[n] * qx[n] for n in ns]
                if diagonal:
                    dz = [jnp.where(strict, dz[n], 0.0) for n in ns]
                dzb = [dz[n].astype(BF16) for n in ns]
                attb = [att[n].astype(BF16) for n in ns]
                new_dq = []
                for pr in range(2):
                    a, b = 2 * pr, 2 * pr + 1
                    new_dq.append(dq[pr] + (jnp.dot(dzb[a], km[pr][0], preferred_element_type=F32)
                                            + jnp.dot(dzb[b], km[pr][1], preferred_element_type=F32)))
                    dk_ref[pr][pl.ds(c0, tk), :] += (
                        lax.dot_general(dzb[a], qm[pr][0], TN_DIMS, preferred_element_type=F32)
                        + lax.dot_general(dzb[b], qm[pr][1], TN_DIMS, preferred_element_type=F32))
                    dv_ref[pr][pl.ds(c0, tk), :] += (
                        lax.dot_general(attb[a], dom[pr][0], TN_DIMS, preferred_element_type=F32)
                        + lax.dot_general(attb[b], dom[pr][1], TN_DIMS, preferred_element_type=F32))
                return (tuple(pc[n] + jnp.sum(lk[n], axis=-1, keepdims=True) for n in ns),
                        tuple(qc[n] + jnp.sum(dg[n], axis=-1, keepdims=True) for n in ns), tuple(new_dq))

            zero = (jnp.zeros((tq, 1), F32),) * 4
            init = (zero, zero, (jnp.zeros((tq, LANES), F32),) * 2)
            carry = lax.fori_loop(0, i, lambda j, cr: k_tile(j, cr, False), init)
            _, _, dq = k_tile(i, carry, True)
            for pr in range(2):
                d_ref[pl.ds(r0, tq), pr * PAIR_W:pr * PAIR_W + LANES] = dq[pr] * SCALE
            return 0

        lax.fori_loop(0, nq, q_tile, 0)

    both = lambda width, col: pl.BlockSpec((seq, 2 * width), lambda b: (b, col))
    out, carried = _pallas(
        body, name=name, grid=(nb,), in_specs=[both(PAIR_W, MIX_SB), both(LANES, 0), both(LANES, MIX_SB), ANY],
        out_specs=[both(PAIR_W, MIX_SB)], out_shape=[jax.ShapeDtypeStruct(dbuf.shape, F32)],
        args=[proj, ltot, do, dbuf], aliases={3: 0}, sem=("arbitrary",), comm=comm)
    return out[0] if comm is None else (out[0], carried)


def _lane_scan(x, reverse=False):
    n = x.shape[-1]
    lane = lax.broadcasted_iota(jnp.int32, x.shape, 1)
    k = 1
    while k < n:
        if reverse:
            x = x + jnp.where(lane < n - k, pltpu.roll(x, n - k, 1), 0.0)
        else:
            x = x + jnp.where(lane >= k, pltpu.roll(x, k, 1), 0.0)
        k *= 2
    return x


def _fox_gate_fwd(f_rows, b_rows):
    def body(f_ref, b_ref, o_ref):
        o_ref[...] = _lane_scan(_log_sigmoid(f_ref[...] + b_ref[...]))

    return pl.pallas_call(body, name="fox_gate_fwd", out_shape=jax.ShapeDtypeStruct(f_rows.shape, F32))(f_rows, b_rows)


def _fox_gate_bwd(dcum, f_rows, b_rows):
    def body(d_ref, f_ref, b_ref, df_ref, db_ref):
        z = f_ref[...] + b_ref[...]
        df = _lane_scan(d_ref[...], reverse=True) * jnp.exp(_log_sigmoid(-z))
        df_ref[...] = df
        rs = jnp.sum(df, axis=-1, keepdims=True)
        tot = rs
        for e in range(1, f_rows.shape[0] // N_HEADS):
            tot = tot + pltpu.roll(rs, e * N_HEADS, 0)
        db_ref[...] = tot

    return pl.pallas_call(
        body, name="fox_gate_bwd",
        out_shape=[jax.ShapeDtypeStruct(f_rows.shape, F32), jax.ShapeDtypeStruct((f_rows.shape[0], 1), F32)],
    )(dcum, f_rows, b_rows)


def _dil_tables(seq):
    t = min(ATT_TILE, seq)
    n = seq // t
    a = np.arange(t)
    d = (np.arange(n)[:, None, None] * t + a[None, :, None] - a[None, None, :]).astype(np.int64)
    count = np.zeros(d.shape, np.int64)
    for window, dil in DILATED_PATTERNS:
        count += (d >= 0) & (d % dil == 0) & (d // dil <= window // dil)
    nn = np.maximum(d, 0)
    max_exact = NUM_BUCKETS // 2
    nf = np.maximum(nn, 1).astype(np.float32)
    large = max_exact + (np.log(nf / np.float32(max_exact)) / np.float32(math.log(MAX_DISTANCE / max_exact))
                         * np.float32(NUM_BUCKETS - max_exact)).astype(np.int32)
    bucket = np.where(nn < max_exact, nn, np.minimum(large, NUM_BUCKETS - 1))
    bucket = np.where(count > 0, bucket, -1).astype(np.int32)
    logc = np.where(count > 0, np.log(np.maximum(count, 1)), MASKED).astype(np.float32)
    return bucket, logc


def _dil_bias(rel_bias, seq):
    bucket, logc = _dil_tables(seq)
    n, t, _ = bucket.shape

    def body(rb_ref, bk_ref, lc_ref, o_ref):
        h = pl.program_id(0) * 2 + pl.program_id(1)
        bk = bk_ref[...]
        out = lc_ref[...]
        for b in range(NUM_BUCKETS):
            out = jnp.where(bk == b, out + rb_ref[b, h], out)
        o_ref[...] = out

    full = pl.BlockSpec((n, t, t), lambda p, h: (0, 0, 0))
    return pl.pallas_call(
        body, name="dil_bias", grid=(2, 2),
        in_specs=[pl.BlockSpec(memory_space=pltpu.SMEM), full, full],
        out_specs=pl.BlockSpec((None, None, n, t, t), lambda p, h: (p, h, 0, 0, 0)),
        out_shape=jax.ShapeDtypeStruct((2, 2, n, t, t), F32),
        compiler_params=_params(("parallel", "parallel")))(rel_bias, jnp.asarray(bucket), jnp.asarray(logc))


def _dil_bias_bwd(dbias, seq):
    bucket, _ = _dil_tables(seq)
    n, t, _ = bucket.shape

    def body(d_ref, bk_ref, o_ref):
        bk = bk_ref[...]
        lane = lax.broadcasted_iota(jnp.int32, (1, LANES), 1)
        for b in range(NUM_BUCKETS):
            rowv = jnp.zeros((1, LANES), F32)
            for h in range(N_HEADS):
                s = jnp.sum(jnp.where(bk == b, d_ref[h // 2, h % 2], 0.0))
                rowv = jnp.where(lane == h, s, rowv)
            o_ref[b:b + 1, :] = rowv

    out = pl.pallas_call(body, name="dil_bias_bwd", out_shape=jax.ShapeDtypeStruct((NUM_BUCKETS, LANES), F32),
                         compiler_params=pltpu.CompilerParams(vmem_limit_bytes=VMEM_LIMIT))(dbias, jnp.asarray(bucket))
    return out[:, :N_HEADS]


def _shift_rows(x, k, row, fill=0.0):
    n = x.shape[0]
    if k > 0:
        return jnp.where(row >= k, pltpu.roll(x, k, 0), fill)
    return jnp.where(row < n + k, pltpu.roll(x, n + k, 0), fill)


def _row_scan(a, u, row, reverse=False):
    n = a.shape[0]
    k = 1
    while k < n:
        s = -k if reverse else k
        u = a * _shift_rows(u, s, row) + u
        a = a * _shift_rows(a, s, row, 1.0)
        k *= 2
    return u


def _sigmoid(x):
    return 1.0 / (1.0 + jnp.exp(-x))


def _gelu(g):
    return 0.5 * g * (1.0 + lax.erf(g * (2.0 ** -0.5)))


def _gelu_grad(g):
    return 0.5 * (1.0 + lax.erf(g * (2.0 ** -0.5))) + g * jnp.exp(-0.5 * g * g) * (1.0 / math.sqrt(2.0 * math.pi))


def _neg_expm1(x):
    small = -x * (1.0 + x * (0.5 + x * (1.0 / 6.0 + x * (1.0 / 24.0))))
    return jnp.where(x > -0.03, small, 1.0 - jnp.exp(x))


def _lru_core(x, vec, wa, wx, row):
    xs = [_shift_rows(x, 3 - j, row) if j < 3 else x for j in range(4)]
    xc = vec[4:5, :]
    for j in range(4):
        xc = xc + vec[j:j + 1, :] * xs[j]
    xcb = xc.astype(BF16)
    r = _sigmoid(jnp.dot(xcb, wa, preferred_element_type=F32) + vec[5:6, :])
    ig = _sigmoid(jnp.dot(xcb, wx, preferred_element_type=F32) + vec[6:7, :])
    lam = vec[7:8, :]
    sp = jnp.maximum(-lam, 0.0) - _log_sigmoid(jnp.abs(lam))
    la = -LRU_C * r * sp
    a = jnp.exp(la)
    mult = jnp.sqrt(_neg_expm1(2.0 * la))
    return xs, xc, xcb, r, ig, sp, la, a, mult


def _lru_specs(seq):
    xg = pl.BlockSpec((seq, LRU_W), lambda hf, b: (b, COL_LRU // LRU_W + hf))
    mix = pl.BlockSpec((seq, LANES), lambda hf, b: (b, 2 * MIX_LRU + hf))
    vec = pl.BlockSpec((SUBLANES, LANES), lambda hf, b: (0, hf))
    mat = pl.BlockSpec((None, LANES, LANES), lambda hf, b: (hf, 0, 0))
    return xg, mix, vec, mat


def _lru_fwd(proj, vec, wa, wx, out_buf, *, nb, name):
    seq = proj.shape[0] // nb

    def body(xg_ref, vec_ref, wa_ref, wx_ref, _, o_ref):
        row = lax.broadcasted_iota(jnp.int32, (seq, LANES), 0)
        _, xc, _, _, ig, _, _, a, mult = _lru_core(xg_ref[:, 0:LANES], vec_ref[...], wa_ref[...], wx_ref[...], row)
        h = _row_scan(a, mult * (ig * xc), row)
        o_ref[...] = h * _gelu(xg_ref[:, LANES:LRU_W])

    xg, mix, vecs, mat = _lru_specs(seq)
    return pl.pallas_call(
        body, name=name, grid=(2, nb), in_specs=[xg, vecs, mat, mat, ANY], out_specs=mix,
        out_shape=jax.ShapeDtypeStruct(out_buf.shape, F32), input_output_aliases={4: 0},
        compiler_params=_params(("parallel", "arbitrary")))(proj, vec, wa, wx, out_buf)


def _lru_bwd(proj, vec, wa, wx, dout, dbuf, *, nb, name):
    seq = proj.shape[0] // nb

    def body(xg_ref, vec_ref, wa_ref, wx_ref, do_ref, _, d_ref, dvec_ref, dwa_ref, dwx_ref):
        row = lax.broadcasted_iota(jnp.int32, (seq, LANES), 0)
        vec_, wa_, wx_ = vec_ref[...], wa_ref[...], wx_ref[...]
        xs, xc, xcb, r, ig, sp, la, a, mult = _lru_core(xg_ref[:, 0:LANES], vec_, wa_, wx_, row)
        h = _row_scan(a, mult * (ig * xc), row)
        gate, do = xg_ref[:, LANES:LRU_W], do_ref[...]
        d_ref[:, LANES:LRU_W] = do * h * _gelu_grad(gate)
        dh = do * _gelu(gate)
        gacc = _row_scan(_shift_rows(a, -1, row), dh, row, reverse=True)
        da = gacc * _shift_rows(h, 1, row)
        dmult = gacc * (ig * xc)
        dig = gacc * (mult * xc)
        dxc = gacc * (mult * ig)
        dla = da * a - dmult * (a * a) / mult
        dr = (-LRU_C) * sp * dla
        dsp = jnp.sum((-LRU_C) * r * dla, axis=0, keepdims=True)
        dpr = dr * r * (1.0 - r)
        dpi = dig * ig * (1.0 - ig)
        dprb, dpib = dpr.astype(BF16), dpi.astype(BF16)
        dxc = (dxc + lax.dot_general(dprb, wa_, NT_DIMS, preferred_element_type=F32)
               + lax.dot_general(dpib, wx_, NT_DIMS, preferred_element_type=F32))
        dx = vec_[3:4, :] * dxc
        for j in range(3):
            dx = dx + vec_[j:j + 1, :] * _shift_rows(dxc, -(3 - j), row)
        d_ref[:, 0:LANES] = dx

        @pl.when(pl.program_id(1) == 0)
        def _():
            dvec_ref[...] = jnp.zeros_like(dvec_ref)
            dwa_ref[...] = jnp.zeros_like(dwa_ref)
            dwx_ref[...] = jnp.zeros_like(dwx_ref)

        for j in range(4):
            dvec_ref[j:j + 1, :] += jnp.sum(dxc * xs[j], axis=0, keepdims=True)
        dvec_ref[4:5, :] += jnp.sum(dxc, axis=0, keepdims=True)
        dvec_ref[5:6, :] += jnp.sum(dpr, axis=0, keepdims=True)
        dvec_ref[6:7, :] += jnp.sum(dpi, axis=0, keepdims=True)
        lam = vec_[7:8, :]
        dvec_ref[7:8, :] += -dsp * _sigmoid(-lam)
        dwa_ref[...] += lax.dot_general(xcb, dprb, TN_DIMS, preferred_element_type=F32)
        dwx_ref[...] += lax.dot_general(xcb, dpib, TN_DIMS, preferred_element_type=F32)

    xg, mix, vecs, mat = _lru_specs(seq)
    return pl.pallas_call(
        body, name=name, grid=(2, nb), in_specs=[xg, vecs, mat, mat, mix, ANY], out_specs=[xg, vecs, mat, mat],
        out_shape=[jax.ShapeDtypeStruct(dbuf.shape, F32), jax.ShapeDtypeStruct((SUBLANES, 2 * LANES), F32),
                   jax.ShapeDtypeStruct((2, LANES, LANES), F32), jax.ShapeDtypeStruct((2, LANES, LANES), F32)],
        input_output_aliases={5: 0},
        compiler_params=_params(("parallel", "arbitrary")))(proj, vec, wa, wx, dout, dbuf)


FFN_ROWS = 256
FFN_COLS = 1408


def _with_halo(halo, x, k):
    xx = jnp.concatenate([halo, x], axis=0)
    return pltpu.roll(xx, k, 0)[SUBLANES:, :]


def _ffn_conv(x_ref, halo_ref, cw, pos):
    x, halo = x_ref[...], halo_ref[...]
    x1 = jnp.where(pos >= 1, _with_halo(halo, x, 1), 0.0)
    x2 = jnp.where(pos >= 2, _with_halo(halo, x, 2), 0.0)
    return cw[3:4, :] + cw[0:1, :] * x2 + cw[1:2, :] * x1 + cw[2:3, :] * x, x1, x2


def _ffn_specs(tm, tn, gate_off):
    prev = lambda i: jnp.maximum(i * (tm // SUBLANES) - 1, 0)
    up = pl.BlockSpec((tm, tn), lambda j, i: (i, j))
    gate = pl.BlockSpec((tm, tn), lambda j, i: (i, j + gate_off))
    up_h = pl.BlockSpec((SUBLANES, tn), lambda j, i: (prev(i), j))
    gate_h = pl.BlockSpec((SUBLANES, tn), lambda j, i: (prev(i), j + gate_off))
    cw_up = pl.BlockSpec((SUBLANES, tn), lambda j, i: (0, j))
    cw_gate = pl.BlockSpec((SUBLANES, tn), lambda j, i: (0, j + gate_off))
    return up, gate, up_h, gate_h, cw_up, cw_gate


def _ffn_act(hf, cw, *, seq, name):
    t, w2 = hf.shape
    w = w2 // 2
    tm, tn = _tile(seq, FFN_ROWS, SUBLANES), _tile(w, FFN_COLS)

    def body(u_ref, g_ref, uh_ref, gh_ref, cu_ref, cg_ref, o_ref):
        pos = (pl.program_id(1) * tm + lax.broadcasted_iota(jnp.int32, (tm, 1), 0)) % seq
        up, _, _ = _ffn_conv(u_ref, uh_ref, cu_ref[...], pos)
        gate, _, _ = _ffn_conv(g_ref, gh_ref, cg_ref[...], pos)
        o_ref[...] = (_gelu(gate) * up).astype(BF16)

    specs = _ffn_specs(tm, tn, w // tn)
    return pl.pallas_call(
        body, name=name, grid=(w // tn, t // tm), in_specs=list(specs), out_specs=specs[0],
        out_shape=jax.ShapeDtypeStruct((t, w), BF16),
        compiler_params=_params(("parallel", "parallel")))(hf, hf, hf, hf, cw, cw)


def _ffn_bwd(hf, cw, dact, *, seq, name, comm=None):
    t, w2 = hf.shape
    w = w2 // 2
    tm, tn = _tile(seq, FFN_ROWS, 2 * SUBLANES), _tile(w, FFN_COLS)
    ext = tm + SUBLANES
    last = t // SUBLANES - 1

    def body(u_ref, g_ref, uh_ref, gh_ref, cu_ref, cg_ref, un_ref, gn_ref, da_ref, dn_ref, d_ref, dcu_ref, dcg_ref):
        pos = (pl.program_id(1) * tm + lax.broadcasted_iota(jnp.int32, (ext, 1), 0)) % seq

        def conv(x_ref, prev_ref, next_ref, cwv):
            xx = jnp.concatenate([prev_ref[...], x_ref[...], next_ref[...]], axis=0)
            x1 = jnp.where(pos >= 1, pltpu.roll(xx, 1, 0)[SUBLANES:, :], 0.0)
            x2 = jnp.where(pos >= 2, pltpu.roll(xx, 2, 0)[SUBLANES:, :], 0.0)
            x0 = xx[SUBLANES:, :]
            return cwv[3:4, :] + cwv[0:1, :] * x2 + cwv[1:2, :] * x1 + cwv[2:3, :] * x0, (x2, x1, x0)

        def back(d, cwv):
            d1 = jnp.where(pos < seq - 1, pltpu.roll(d, ext - 1, 0), 0.0)
            d2 = jnp.where(pos < seq - 2, pltpu.roll(d, ext - 2, 0), 0.0)
            return (cwv[2:3, :] * d + cwv[1:2, :] * d1 + cwv[0:1, :] * d2)[:tm, :].astype(BF16)

        cu, cg = cu_ref[...], cg_ref[...]
        up, u_taps = conv(u_ref, uh_ref, un_ref, cu)
        gate, g_taps = conv(g_ref, gh_ref, gn_ref, cg)
        da = jnp.concatenate([da_ref[...], dn_ref[...]], axis=0)
        cdf = 0.5 * (1.0 + lax.erf(gate * (2.0 ** -0.5)))
        d_up = da * (gate * cdf)
        d_gate = da * up * (cdf + gate * jnp.exp(-0.5 * gate * gate) * (1.0 / math.sqrt(2.0 * math.pi)))
        d_ref[0] = back(d_up, cu)
        d_ref[1] = back(d_gate, cg)

        @pl.when(pl.program_id(1) == 0)
        def _():
            dcu_ref[...] = jnp.zeros_like(dcu_ref)
            dcg_ref[...] = jnp.zeros_like(dcg_ref)

        for ref, d, taps in ((dcu_ref, d_up, u_taps), (dcg_ref, d_gate, g_taps)):
            own = d[:tm, :]
            for j in range(3):
                ref[j:j + 1, :] += jnp.sum(own * taps[j][:tm, :], axis=0, keepdims=True)
            ref[3:4, :] += jnp.sum(own, axis=0, keepdims=True)

    gate_off = w // tn
    specs = _ffn_specs(tm, tn, gate_off)
    tile, cwt = specs[0], specs[4]
    nxt = lambda i: jnp.minimum((i + 1) * (tm // SUBLANES), last)
    up_n = pl.BlockSpec((SUBLANES, tn), lambda j, i: (nxt(i), j))
    gate_n = pl.BlockSpec((SUBLANES, tn), lambda j, i: (nxt(i), j + gate_off))
    out, carried = _pallas(
        body, name=name, grid=(w // tn, t // tm), in_specs=list(specs) + [up_n, gate_n, tile, up_n],
        out_specs=[pl.BlockSpec((2, tm, tn), lambda j, i: (0, i, j)), cwt, cwt],
        out_shape=[jax.ShapeDtypeStruct((2, t, w), BF16), jax.ShapeDtypeStruct((SUBLANES, w), F32),
                   jax.ShapeDtypeStruct((SUBLANES, w), F32)],
        args=[hf, hf, hf, hf, cw, cw, hf, hf, dact, dact], sem=("parallel", "arbitrary"), comm=comm)
    return out if comm is None else (out, carried)


def _adamw(w, g, m, v, *, name, rows=256):
    nl, r, c = w.shape
    tr = _tile(r, rows, SUBLANES)

    def body(w_ref, g_ref, m_ref, v_ref, d_ref, nm_ref, nv_ref):
        g_ = g_ref[...]
        nm = ADAM_B1 * m_ref[...] + (1.0 - ADAM_B1) * g_
        nv = ADAM_B2 * v_ref[...] + (1.0 - ADAM_B2) * (g_ * g_)
        m_hat = nm / (1.0 - ADAM_B1 ** ADAM_STEP)
        v_hat = nv / (1.0 - ADAM_B2 ** ADAM_STEP)
        d_ref[...] = -ADAM_LR * (m_hat / (jnp.sqrt(v_hat) + ADAM_EPS) + ADAM_WD * w_ref[...])
        nm_ref[...] = nm
        nv_ref[...] = nv

    spec = pl.BlockSpec((None, tr, c), lambda l, i: (l, i, 0))
    shape = jax.ShapeDtypeStruct((nl, r, c), F32)
    return pl.pallas_call(body, name=name, grid=(nl, r // tr), in_specs=[spec] * 4, out_specs=[spec] * 3,
                          out_shape=[shape] * 3, compiler_params=_params(("parallel", "parallel")))(w, g, m, v)


def _mesh_pos():
    return lax.axis_index("x"), lax.axis_index("y"), lax.axis_index("c")


def _peers(x, y):
    chips = [(1 - x, y), (x, 1 - y), (1 - x, 1 - y)]
    return [(px, py, 2 * px + py) for px, py in chips]


def _remote(src, dst, send_sems, recv_sems, idx, to):
    return pltpu.make_async_remote_copy(src, dst, send_sems.at[idx], recv_sems.at[idx], device_id=to,
                                        device_id_type=MESH)


class _Comm:
    def __init__(self, operands, out_shape, aliases, sems, copies):
        self.operands, self.out_shape, self.aliases, self.sems, self.copies = operands, out_shape, aliases, sems, copies

    def start(self, ins, outs, sems):
        for send, _ in self.copies(ins, outs, sems):
            send.start()

    def wait(self, ins, outs, sems):
        pairs = self.copies(ins, outs, sems)
        for _, recv in pairs:
            recv.wait_recv()
        for send, _ in pairs:
            send.wait_send()


def _pallas(body, *, name, grid, in_specs, out_specs, out_shape, args, aliases=None, scratch=(), sem, comm=None):
    n_in, n_out = len(in_specs), len(out_specs)
    aliases = dict(aliases or {})
    if comm is None:
        out = pl.pallas_call(body, name=name, grid=grid, in_specs=in_specs, out_specs=out_specs, out_shape=out_shape,
                             input_output_aliases=aliases, scratch_shapes=list(scratch),
                             compiler_params=_params(sem))(*args)
        return list(out), []
    nci, nco, ncs = len(comm.operands), len(comm.out_shape), len(comm.sems)

    def carried(*refs):
        ins, cin = refs[:n_in], refs[n_in:n_in + nci]
        o0 = n_in + nci
        outs, cout = refs[o0:o0 + n_out], refs[o0 + n_out:o0 + n_out + nco]
        s0 = o0 + n_out + nco
        own, csem = refs[s0:len(refs) - ncs], refs[len(refs) - ncs:]
        ids = [pl.program_id(ax) for ax in range(len(grid))]
        first, last = ids[0] == 0, ids[0] == grid[0] - 1
        for i, g in zip(ids[1:], grid[1:]):
            first, last = jnp.logical_and(first, i == 0), jnp.logical_and(last, i == g - 1)

        @pl.when(first)
        def _():
            comm.start(cin, cout, csem)

        body(*ins, *outs, *own)

        @pl.when(last)
        def _():
            comm.wait(cin, cout, csem)

    aliases.update({n_in + i: n_out + j for i, j in comm.aliases.items()})
    out = pl.pallas_call(
        carried, name=name, grid=grid, in_specs=list(in_specs) + [ANY] * nci, out_specs=list(out_specs) + [ANY] * nco,
        out_shape=list(out_shape) + list(comm.out_shape), input_output_aliases=aliases,
        scratch_shapes=list(scratch) + list(comm.sems),
        compiler_params=_params(("arbitrary",) * len(grid)))(*args, *comm.operands)
    return list(out[:n_out]), list(out[n_out:])


def _run_comm(comm, *, name):
    nci, nco = len(comm.operands), len(comm.out_shape)

    def body(*refs):
        ins, outs, sems = refs[:nci], refs[nci:nci + nco], refs[nci + nco:]
        comm.start(ins, outs, sems)
        comm.wait(ins, outs, sems)

    return pl.pallas_call(body, name=name, in_specs=[ANY] * nci, out_specs=[ANY] * nco, out_shape=list(comm.out_shape),
                          input_output_aliases=dict(comm.aliases), scratch_shapes=list(comm.sems))(*comm.operands)


def _pair_sems(*shape):
    return [pltpu.SemaphoreType.DMA(shape), pltpu.SemaphoreType.DMA(shape)]


def _gather_comm(bufs, layer, stage):
    n = len(bufs)

    def copies(ins, outs, sems):
        x, y, c = _mesh_pos()
        me = 2 * x + y
        pairs = []
        for i in range(n):
            h = bufs[i].shape[2] // 2
            mine, other = pl.ds(c * h, h), pl.ds((1 - c) * h, h)
            for r, (px, py, k) in enumerate(_peers(x, y)):
                if stage == 0:
                    send = _remote(ins[i].at[layer, me, mine, :], outs[i].at[layer, me, mine, :], *sems, (i, r), (px, py, c))
                    land = outs[i].at[layer, k, mine, :]
                    recv = _remote(land, land, *sems, (i, r), (px, py, c))
                else:
                    send = _remote(ins[i].at[layer, k, mine, :], outs[i].at[layer, k, mine, :], *sems, (i, r), (x, y, 1 - c))
                    land = outs[i].at[layer, k, other, :]
                    recv = _remote(land, land, *sems, (i, r), (x, y, 1 - c))
                pairs.append((send, recv))
        return pairs

    return _Comm(bufs, [jax.ShapeDtypeStruct(b.shape, b.dtype) for b in bufs], {i: i for i in range(n)},
                 _pair_sems(n, 3), copies)


def _reduce_sibling_comm(gs):
    n = len(gs)

    def copies(ins, outs, sems):
        x, y, c = _mesh_pos()
        pairs = []
        for i in range(n):
            h = gs[i].shape[1] // 2
            cp = _remote(ins[i].at[:, pl.ds((1 - c) * h, h), :], outs[i], *sems, i, (x, y, 1 - c))
            pairs.append((cp, cp))
        return pairs

    return _Comm(gs, [jax.ShapeDtypeStruct((g.shape[0], g.shape[1] // 2, g.shape[2]), g.dtype) for g in gs], {},
                 _pair_sems(n), copies)


def _reduce_chips_comm(ps):
    n = len(ps)

    def copies(ins, outs, sems):
        x, y, c = _mesh_pos()
        pairs = []
        for i in range(n):
            for r, (px, py, k) in enumerate(_peers(x, y)):
                cp = _remote(ins[i].at[k], outs[i].at[r], *sems, (i, r), (px, py, c))
                pairs.append((cp, cp))
        return pairs

    return _Comm(ps, [jax.ShapeDtypeStruct((3,) + p.shape[1:], p.dtype) for p in ps], {}, _pair_sems(n, 3), copies)


def _share_halves(bufs, *, name):
    n = len(bufs)

    def body(*refs):
        ins, outs = refs[:n], refs[n:2 * n]
        send_sems, recv_sems = refs[2 * n:]
        x, y, c = _mesh_pos()
        cps = []
        for i in range(n):
            h = bufs[i].shape[1] // 2
            mine = pl.ds(c * h, h)
            cp = _remote(ins[i].at[:, mine, :], outs[i].at[:, mine, :], send_sems, recv_sems, i, (x, y, 1 - c))
            cp.start()
            cps.append(cp)
        for cp in cps:
            cp.wait()

    return pl.pallas_call(
        body, name=name, in_specs=[ANY] * n, out_specs=[ANY] * n,
        out_shape=[jax.ShapeDtypeStruct(b.shape, b.dtype) for b in bufs],
        input_output_aliases={i: i for i in range(n)},
        scratch_shapes=[pltpu.SemaphoreType.DMA((n,)), pltpu.SemaphoreType.DMA((n,))])(*bufs)


def _add_own_half(full, recv, pos, *, name, rows=256):
    k4, h, n = recv.shape
    tr = _tile(h, rows, 16)
    nblk = h // tr

    def body(pos_ref, a_ref, b_ref, o_ref):
        o_ref[...] = (a_ref[...] + b_ref[...]).astype(BF16)

    grid_spec = pltpu.PrefetchScalarGridSpec(
        num_scalar_prefetch=1, grid=(k4, nblk),
        in_specs=[pl.BlockSpec((None, tr, n), lambda k, i, pos_ref: (k, pos_ref[1] * nblk + i, 0)),
                  pl.BlockSpec((None, tr, n), lambda k, i, pos_ref: (k, i, 0))],
        out_specs=pl.BlockSpec((None, tr, n), lambda k, i, pos_ref: (k, i, 0)))
    return pl.pallas_call(body, name=name, grid_spec=grid_spec, out_shape=jax.ShapeDtypeStruct(recv.shape, BF16),
                          compiler_params=_params(("parallel", "parallel")))(pos, full, recv)


def _sum_into(own, others, buf, pos, layer, *, name, rows=256):
    _, h, n = own.shape
    tr = _tile(h, rows, 16)
    nblk = h // tr

    def body(pos_ref, own_ref, oth_ref, _, o_ref):
        acc = own_ref[...].astype(F32)
        for r in range(3):
            acc = acc + oth_ref[r].astype(F32)
        o_ref[...] = acc

    grid_spec = pltpu.PrefetchScalarGridSpec(
        num_scalar_prefetch=1, grid=(nblk,),
        in_specs=[pl.BlockSpec((None, tr, n), lambda i, pos_ref: (pos_ref[0], i, 0)),
                  pl.BlockSpec((3, tr, n), lambda i, pos_ref: (0, i, 0)), ANY],
        out_specs=pl.BlockSpec((None, tr, n), lambda i, pos_ref: (layer, pos_ref[1] * nblk + i, 0)))
    return pl.pallas_call(body, name=name, grid_spec=grid_spec, out_shape=jax.ShapeDtypeStruct(buf.shape, F32),
                          input_output_aliases={3: 0}, compiler_params=_params(("parallel",)))(pos, own, others, buf)


def _sibling_pair(buf, *, name):
    def body(src_ref, out_ref, send_sem, recv_sem, local_sem):
        x, y, c = _mesh_pos()
        local = pltpu.make_async_copy(src_ref, out_ref.at[c], local_sem)
        local.start()
        cp = pltpu.make_async_remote_copy(src_ref, out_ref.at[c], send_sem, recv_sem, device_id=(x, y, 1 - c),
                                          device_id_type=MESH)
        cp.start()
        cp.wait()
        local.wait()

    return pl.pallas_call(
        body, name=name, in_specs=[ANY], out_specs=ANY, out_shape=jax.ShapeDtypeStruct((2,) + buf.shape, buf.dtype),
        scratch_shapes=[pltpu.SemaphoreType.DMA, pltpu.SemaphoreType.DMA, pltpu.SemaphoreType.DMA])(buf)


def _chip_bcast(buf, *, name):
    def body(src_ref, out_ref, send_sems, recv_sems, local_sem):
        x, y, c = _mesh_pos()
        me = 2 * x + y
        local = pltpu.make_async_copy(src_ref, out_ref.at[me], local_sem)
        local.start()
        sends = []
        for r, (px, py, _) in enumerate(_peers(x, y)):
            cp = _remote(src_ref, out_ref.at[me], send_sems, recv_sems, r, (px, py, c))
            cp.start()
            sends.append(cp)
        for r, (px, py, k) in enumerate(_peers(x, y)):
            _remote(src_ref, out_ref.at[k], send_sems, recv_sems, r, (px, py, c)).wait_recv()
        for cp in sends:
            cp.wait_send()
        local.wait()

    return pl.pallas_call(
        body, name=name, in_specs=[ANY], out_specs=ANY, out_shape=jax.ShapeDtypeStruct((4,) + buf.shape, buf.dtype),
        scratch_shapes=[pltpu.SemaphoreType.DMA((3,)), pltpu.SemaphoreType.DMA((3,)), pltpu.SemaphoreType.DMA])(buf)


def _sum_slots(buf, *, name, rows=384):
    r, n = buf.shape[-2:]
    k = int(np.prod(buf.shape[:-2]))
    tr = _tile(r, rows, SUBLANES)

    def body(b_ref, o_ref):
        acc = b_ref[0]
        for s in range(1, k):
            acc = acc + b_ref[s]
        o_ref[...] = acc

    return pl.pallas_call(
        body, name=name, grid=(r // tr,), in_specs=[pl.BlockSpec((k, tr, n), lambda i: (0, i, 0))],
        out_specs=pl.BlockSpec((tr, n), lambda i: (i, 0)), out_shape=jax.ShapeDtypeStruct((r, n), F32),
        compiler_params=_params(("parallel",)))(buf.reshape((k, r, n)))


ROW = 1024
BIG = (("w_in", 2), ("w_out", 1), ("w_cq", 1), ("w_ck", 1), ("w_cv", 1), ("w_co", 2), ("w_up", 2), ("w_down", 1))
CONV = ("lru_conv_w", "ffn_conv_w")
REPLICATED = ("norm_mix_g", "b_forget", "lru_conv_b", "lru_w_a", "lru_b_a", "lru_w_x", "lru_b_x", "lru_lambda",
              "norm_cross_g", "norm_mem_g", "norm_ffn_g", "ffn_conv_b", "rel_bias", "final_norm_g")
WEIGHTS = ('norm_mix_g', 'w_in', 'b_forget', 'lru_conv_w', 'lru_conv_b', 'lru_w_a', 'lru_b_a', 'lru_w_x', 'lru_b_x',
           'lru_lambda', 'w_out', 'norm_cross_g', 'norm_mem_g', 'w_cq', 'w_ck', 'w_cv', 'w_co', 'norm_ffn_g', 'w_up',
           'ffn_conv_w', 'ffn_conv_b', 'w_down', 'rel_bias', 'final_norm_g')
INPUTS = ("x", "mem") + WEIGHTS + ("loss_target",) + tuple("m_" + n for n in WEIGHTS) + tuple("v_" + n for n in WEIGHTS)


def _round_up(n, m):
    return -(-n // m) * m


class _Packing:
    def __init__(self, entries):
        self.entries, self.off = entries, {}
        o = 0
        for name, shape in entries:
            self.off[name] = o
            o += _round_up(int(np.prod(shape)), ROW)
        self.used = o
        self.rows = _round_up(o // ROW, SUBLANES)

    def pack(self, arrays):
        parts = []
        for name, shape in self.entries:
            n = int(np.prod(shape))
            parts.append(jnp.pad(arrays[name].reshape(n), (0, _round_up(n, ROW) - n)))
        tail = self.rows * ROW - self.used
        if tail:
            parts.append(jnp.zeros((tail,), F32))
        return jnp.concatenate(parts).reshape(self.rows, ROW)

    def unpack(self, flat, lead=()):
        out = {}
        for name, shape in self.entries:
            n = int(np.prod(shape))
            r0, nr = self.off[name] // ROW, _round_up(n, ROW) // ROW
            rows = lax.slice_in_dim(flat, r0, r0 + nr, axis=len(lead)).reshape(lead + (nr * ROW,))
            out[name] = lax.slice_in_dim(rows, 0, n, axis=len(lead)).reshape(lead + tuple(shape))
        return out


def _to_shards(g, axis):
    r, c = g.shape
    if axis == 1:
        return g.reshape(4, r // 4, c)
    return g.reshape(r, 4, c // 4).transpose(1, 0, 2)


def _from_shards(s, axis):
    _, r, c = s.shape
    if axis == 1:
        return s.reshape(4 * r, c)
    return s.transpose(1, 0, 2).reshape(r, 4 * c)


def _proj_blocks():
    blocks = []
    for mixer in (MIX_SB, MIX_FOX, MIX_DIL):
        for p in range(2):
            blocks += [ORIG_COL[mixer] + part * 2 * LANES + p * LANES for part in range(3)]
    for hf in range(2):
        blocks += [ORIG_LRU_X + hf * LANES, ORIG_LRU_G + hf * LANES]
    return blocks


def _pad_w_in(w):
    parts = [w[..., s:s + LANES] for s in _proj_blocks()]
    parts += [w[..., 1536:1540], jnp.zeros(w.shape[:-1] + (PROJ_W - COL_F - N_HEADS,), w.dtype)]
    return jnp.concatenate(parts, axis=-1)


def _unpad_w_in(wp):
    blocks = _proj_blocks()
    order = sorted(range(len(blocks)), key=lambda i: blocks[i])
    parts = []
    for i in order:
        if blocks[i] == ORIG_COL[MIX_DIL]:
            parts.append(wp[..., COL_F:COL_F + N_HEADS])
        parts.append(wp[..., i * LANES:(i + 1) * LANES])
    return jnp.concatenate(parts, axis=-1)


def _block_diag(w):
    z = jnp.zeros((HEAD_DIM, HEAD_DIM), w.dtype)
    half = lambda a, b: jnp.concatenate([jnp.concatenate([a, z], 1), jnp.concatenate([z, b], 1)], 0)
    return jnp.stack([half(w[0], w[1]), half(w[2], w[3])])


def _block_diag_grad(d):
    return jnp.stack([d[0, :HEAD_DIM, :HEAD_DIM], d[0, HEAD_DIM:, HEAD_DIM:],
                      d[1, :HEAD_DIM, :HEAD_DIM], d[1, HEAD_DIM:, HEAD_DIM:]])


def _fox_layouts(cum, nb, seq):
    tk = min(ATT_TILE, seq)
    col = cum.reshape(nb, 2, 2, seq).transpose(0, 1, 3, 2)
    row = cum.reshape(nb, 2, 2, seq // tk, tk).transpose(0, 1, 3, 2, 4)
    return col, row


def _layer_params(w, l, nb):
    lru_vec = jnp.concatenate([w["lru_conv_w"][l], w["lru_conv_b"][l][None], w["lru_b_a"][l][None],
                               w["lru_b_x"][l][None], w["lru_lambda"][l][None]], axis=0)
    ffn_cw = jnp.concatenate([w["ffn_conv_w"][l], w["ffn_conv_b"][l][None],
                              jnp.zeros((SUBLANES - 4, 2 * D_FF), F32)], axis=0)
    return dict(
        w_in=w["w_in_padded"][l], lru_vec=lru_vec,
        wa=_block_diag(w["lru_w_a"][l]).astype(BF16), wx=_block_diag(w["lru_w_x"][l]).astype(BF16),
        ffn_cw=ffn_cw, b_rows=jnp.tile(w["b_forget"][l], nb).reshape(nb * N_HEADS, 1))


NORM_ROWS = 512


def _merge_comms(comms):
    if len(comms) == 1:
        return comms[0]
    operands, out_shape, aliases, sems, spans = [], [], {}, [], []
    for cm in comms:
        aliases.update({len(operands) + i: len(out_shape) + j for i, j in cm.aliases.items()})
        spans.append((len(operands), len(out_shape), len(sems)))
        operands += list(cm.operands)
        out_shape += list(cm.out_shape)
        sems += list(cm.sems)

    def copies(ins, outs, sm):
        pairs = []
        for cm, (i0, o0, s0) in zip(comms, spans):
            pairs += cm.copies(ins[i0:i0 + len(cm.operands)], outs[o0:o0 + len(cm.out_shape)], sm[s0:s0 + len(cm.sems)])
        return pairs

    return _Comm(operands, out_shape, aliases, sems, copies)


GATHER_FIRST = ("w_in",)
GATHER_MID = ("w_out", "w_cq", "w_ck", "w_cv", "w_co")
GATHER_LAST = ("w_up", "w_down")


class _WeightGather:
    def __init__(self, slots, w, depth):
        self.slots, self.w, self.depth = slots, w, depth

    def plan(self, l, key):
        nxt = l + 1 if l + 1 < self.depth else None
        early = GATHER_FIRST + GATHER_MID
        if l == 0:
            table = {"proj": [(GATHER_MID, 0, 0)],
                     "sb_fwd": [(GATHER_MID, 0, 1), (GATHER_LAST, 0, 0)],
                     "fox_fwd": [(GATHER_LAST, 0, 1)] + ([(early, nxt, 0)] if nxt else []),
                     "dil_fwd": [(early, nxt, 1), (GATHER_LAST, nxt, 0)] if nxt else [],
                     "out": [(GATHER_LAST, nxt, 1)] if nxt else []}
        else:
            everything = early + GATHER_LAST
            table = {"sb_fwd": [(everything, nxt, 0)], "fox_fwd": [(everything, nxt, 1)]} if nxt else {}
        return table.get(key, [])

    def comm(self, l, key):
        entries = self.plan(l, key)
        if not entries:
            return None
        return _merge_comms([_gather_comm([self.slots[n] for n in names], layer, stage)
                             for names, layer, stage in entries])

    def done(self, l, key, landed):
        landed = list(landed)
        for names, layer, stage in self.plan(l, key):
            for n in names:
                self.slots[n] = landed.pop(0)
            if stage == 1:
                self.take(names, layer)

    def take(self, names, layer):
        for n, axis in BIG:
            if n in names:
                self.w[n][layer] = _from_shards(self.slots[n][layer], axis)
        if "w_in" in names:
            self.w["w_in_padded"][layer] = _pad_w_in(self.w["w_in"][layer])

    def first(self):
        for stage in (0, 1):
            got = _run_comm(_gather_comm([self.slots[n] for n in GATHER_FIRST], 0, stage), name=f"gather_first_{stage}")
            self.slots.update(zip(GATHER_FIRST, got))
        self.take(GATHER_FIRST, 0)


def _layer_fwd(x, h, mem, w, lp, l, next_g, bias, nb, gather):
    t, d = x.shape
    seq = t // nb
    tag = f"l{l}"
    sv = dict(x0=x)

    def carrying(key, fn):
        comm = gather.comm(l, key)
        res = fn(comm)
        if comm is not None:
            res, landed = res
            gather.done(l, key, landed)
        return res

    proj = carrying("proj", lambda cm: _mm(h, lp["w_in"], name=tag + "_proj", comm=cm))
    mixed, ltot = carrying("sb_fwd", lambda cm: _sb_attn_fwd(proj, lax.empty((t, d), F32), nb=nb,
                                                             name=tag + "_sb_fwd", comm=cm))
    f_rows = proj[:, COL_F:COL_F + N_HEADS].reshape(nb, seq, N_HEADS).transpose(0, 2, 1).reshape(nb * N_HEADS, seq)
    cum_col, cum_row = _fox_layouts(_fox_gate_fwd(f_rows, lp["b_rows"]), nb, seq)
    mixed, lse_fox = carrying("fox_fwd", lambda cm: _softmax_attn_fwd(
        proj, nb=nb, mode="fox", mixer=MIX_FOX, out_buf=mixed, extra=(cum_col, cum_row), name=tag + "_fox_fwd", comm=cm))
    mixed, lse_dil = carrying("dil_fwd", lambda cm: _softmax_attn_fwd(
        proj, nb=nb, mode="dil", mixer=MIX_DIL, out_buf=mixed, extra=(bias,), name=tag + "_dil_fwd", comm=cm))
    mixed = _lru_fwd(proj, lp["lru_vec"], lp["wa"], lp["wx"], mixed, nb=nb, name=tag + "_lru_fwd")
    x1, hq = carrying("out", lambda cm: _mm(mixed, w["w_out"][l], res=x, norm_g=w["norm_cross_g"][l], ti=NORM_ROWS,
                                            name=tag + "_out", comm=cm))
    memn = _rmsnorm(mem, w["norm_mem_g"][l], name=tag + "_norm_mem")
    q = _mm(hq, w["w_cq"][l], name=tag + "_cq")
    k = _mm(memn, w["w_ck"][l], name=tag + "_ck")
    v = _mm(memn, w["w_cv"][l], name=tag + "_cv")
    oc, lse_c = _softmax_attn_fwd((q, k, v), nb=nb, mode="cross", name=tag + "_cross_fwd")
    x2, hn = _mm(oc, w["w_co"][l], res=x1, norm_g=w["norm_ffn_g"][l], ti=NORM_ROWS, name=tag + "_co")
    hf = _mm(hn, w["w_up"][l], name=tag + "_up")
    act = _ffn_act(hf, lp["ffn_cw"], seq=seq, name=tag + "_ffn_act")
    if next_g is None:
        x3, h_next = _mm(act, w["w_down"][l], res=x2, name=tag + "_down"), None
    else:
        x3, h_next = _mm(act, w["w_down"][l], res=x2, norm_g=next_g, ti=NORM_ROWS, name=tag + "_down")
    sv.update(h=h, proj=proj, ltot=ltot, f_rows=f_rows, cum_col=cum_col, cum_row=cum_row, lse_fox=lse_fox,
              lse_dil=lse_dil, mixed=mixed, x1=x1, hq=hq, memn=memn, q=q, k=k, v=v, oc=oc, lse_c=lse_c, x2=x2,
              hn=hn, hf=hf, act=act)
    return x3, h_next, sv


class _PendingReduce:
    def __init__(self, full, pos, layer):
        self.names, self.full, self.pos, self.layer = list(full), list(full.values()), pos, layer

    def sibling_comm(self):
        return _reduce_sibling_comm(self.full)

    def add(self, from_sibling):
        self.partial = [_add_own_half(f, r, self.pos, name=f"l{self.layer}_reduce_add_{n}")
                        for f, r, n in zip(self.full, from_sibling, self.names)]

    def chips_comm(self):
        return _reduce_chips_comm(self.partial)

    def finish(self, others, g_shard):
        g_shard = dict(g_shard)
        for p, o, n in zip(self.partial, others, self.names):
            g_shard[n] = _sum_into(p, o, g_shard[n], self.pos, self.layer, name=f"l{self.layer}_reduce_sum_{n}")
        return g_shard


def _layer_bwd(dx3, mem, sv, w, lp, l, bias, nb, pos, pending=None, g_shard=None, reduce_early=False):
    t = dx3.shape[0]
    seq = t // nb
    tag = f"l{l}"
    g = {}
    down_rows = _tile(sv["act"].shape[1], 1408)
    if pending is None:
        g["w_down"] = _mm(sv["act"], dx3, ta=True, ti=down_rows, name=tag + "_dw_down")
    else:
        g["w_down"], from_sibling = _mm(sv["act"], dx3, ta=True, ti=down_rows, comm=pending.sibling_comm(),
                                        name=tag + "_dw_down")
        pending.add(from_sibling)
    dact = _mm(dx3, w["w_down"][l], tb=True, name=tag + "_dact")
    if pending is None:
        dhf, dcu, dcg = _ffn_bwd(sv["hf"], lp["ffn_cw"], dact, seq=seq, name=tag + "_ffn_bwd")
    else:
        (dhf, dcu, dcg), others = _ffn_bwd(sv["hf"], lp["ffn_cw"], dact, seq=seq, name=tag + "_ffn_bwd",
                                           comm=pending.chips_comm())
        g_shard = pending.finish(others, g_shard)
    dcw = jnp.concatenate([dcu, dcg], axis=1)
    g["ffn_conv_w"], g["ffn_conv_b"] = dcw[:3], dcw[3]
    g["w_up"] = _mm(sv["hn"], dhf, ta=True, halves="b", col_shards=4, name=tag + "_dw_up")
    early = _PendingReduce(_big_grad_shards(g, EARLY), pos, l) if reduce_early else None
    res = _mm(dhf, w["w_up"][l], tb=True, halves="a", norm_bwd=(sv["x2"], w["norm_ffn_g"][l], dx3), ti=NORM_ROWS,
              comm=early.sibling_comm() if early else None, name=tag + "_dhn")
    if early:
        res, from_sibling = res
        early.add(from_sibling)
    dx2, dg = res
    g["norm_ffn_g"] = dg.reshape(-1)
    g["w_co"] = _mm(sv["oc"], dx2, ta=True, col_shards=4, name=tag + "_dw_co")
    doc = _mm(dx2, w["w_co"][l], tb=True, name=tag + "_doc")
    dq, dk, dv = _softmax_attn_bwd((sv["q"], sv["k"], sv["v"]), sv["oc"], sv["lse_c"], doc, nb=nb, mode="cross",
                                   name=tag + "_cross_bwd")
    g["w_cq"] = _mm(sv["hq"], dq, ta=True, name=tag + "_dw_cq")
    g["w_ck"] = _mm(sv["memn"], dk, ta=True, name=tag + "_dw_ck")
    g["w_cv"] = _mm(sv["memn"], dv, ta=True, name=tag + "_dw_cv")
    dx1, dg = _mm(dq, w["w_cq"][l], tb=True, norm_bwd=(sv["x1"], w["norm_cross_g"][l], dx2), ti=NORM_ROWS,
                  name=tag + "_dhq")
    g["norm_cross_g"] = dg.reshape(-1)
    dmemn = _mm(dv, w["w_cv"][l], tb=True, res=_mm(dk, w["w_ck"][l], tb=True, name=tag + "_dmem_k"),
                name=tag + "_dmem_v")
    _, g["norm_mem_g"] = _rmsnorm_bwd(dmemn, mem, w["norm_mem_g"][l], None, name=tag + "_norm_mem_bwd")
    mixed, proj = sv["mixed"], sv["proj"]
    g["w_out"] = _mm(mixed, dx1, ta=True, name=tag + "_dw_out")
    dmixed = _mm(dx1, w["w_out"][l], tb=True, name=tag + "_dmixed")
    dproj = _sb_attn_bwd(proj, sv["ltot"], dmixed, lax.empty((t, PROJ_W), F32), nb=nb, name=tag + "_sb_bwd",
                         comm=early.chips_comm() if early else None)
    if early:
        dproj, others = dproj
        g_shard = early.finish(others, g_shard)
    dproj, dcum_k, dcum_q = _softmax_attn_bwd(
        proj, mixed, sv["lse_fox"], dmixed, nb=nb, mode="fox", mixer=MIX_FOX, dbuf=dproj,
        extra=(sv["cum_col"], sv["cum_row"]), name=tag + "_fox_bwd")
    dcum = (dcum_k.transpose(0, 1, 3, 2, 4).reshape(nb * N_HEADS, seq)
            + dcum_q.transpose(0, 1, 3, 2).reshape(nb * N_HEADS, seq))
    df_rows, db = _fox_gate_bwd(dcum, sv["f_rows"], lp["b_rows"])
    g["b_forget"] = db[:N_HEADS, 0]
    df = df_rows.reshape(nb, N_HEADS, seq).transpose(0, 2, 1).reshape(t, N_HEADS)
    dproj, dbias = _softmax_attn_bwd(proj, mixed, sv["lse_dil"], dmixed, nb=nb, mode="dil", mixer=MIX_DIL,
                                     dbuf=dproj, extra=(bias,), name=tag + "_dil_bwd")
    dproj, dvec, dwa, dwx = _lru_bwd(proj, lp["lru_vec"], lp["wa"], lp["wx"], dmixed, dproj, nb=nb,
                                     name=tag + "_lru_bwd")
    g["lru_conv_w"], g["lru_conv_b"], g["lru_b_a"], g["lru_b_x"], g["lru_lambda"] = (
        dvec[0:4], dvec[4], dvec[5], dvec[6], dvec[7])
    g["lru_w_a"], g["lru_w_x"] = _block_diag_grad(dwa), _block_diag_grad(dwx)
    dproj = lax.dynamic_update_slice(dproj, jnp.pad(df, ((0, 0), (0, PROJ_W - COL_F - N_HEADS))), (0, COL_F))
    g["w_in_padded"] = _mm(sv["h"], dproj, ta=True, name=tag + "_dw_in")
    dx0, dg = _mm(dproj, lp["w_in"], tb=True, norm_bwd=(sv["x0"], w["norm_mix_g"][l], dx1), ti=NORM_ROWS,
                  name=tag + "_dh")
    g["norm_mix_g"] = dg.reshape(-1)
    return dx0, g, dbias, g_shard


def _big_grad_shards(g, names):
    out = {}
    for n, axis in BIG:
        if n not in names:
            continue
        if n in ("w_up", "w_co"):
            out[n] = g[n]
        else:
            out[n] = _to_shards(_unpad_w_in(g["w_in_padded"]) if n == "w_in" else g[n], axis)
    return out


EARLY = ("w_down", "w_up")


def kernel(*args):
    a = dict(zip(INPUTS, args, strict=True))
    nb, seq, d = a["x"].shape
    depth = a["norm_mix_g"].shape[0]
    x = a["x"].reshape(nb * seq, d)
    mem = a["mem"].reshape(nb * a["mem"].shape[1], d)
    target = a["loss_target"].reshape(nb * seq, d)
    cx, cy, c = _mesh_pos()
    chip = 2 * cx + cy
    pos = jnp.stack([chip, c]).astype(jnp.int32)

    slots = {}
    for n, _ in BIG:
        own = a[n].astype(BF16)[:, None]
        slots[n] = lax.dynamic_update_slice(lax.empty((depth, 4) + own.shape[2:], BF16), own, (0, chip, 0, 0))
    w = {n: a[n] for n in REPLICATED}
    w.update({n: {} for n, _ in BIG}, w_in_padded={})
    gather = _WeightGather(slots, w, depth)
    gather.first()
    cpk = _Packing([(n, a[n].shape) for n in CONV])
    conv = cpk.unpack(_chip_bcast(cpk.pack({n: a[n] for n in CONV}), name="gather_conv"), lead=(4,))
    for n in CONV:
        w[n] = jnp.moveaxis(conv[n], 0, 2).reshape(a[n].shape[:2] + (4 * a[n].shape[2],))

    bias = _dil_bias(w["rel_bias"], seq)
    lps, saved = [], []
    h = _rmsnorm(x, w["norm_mix_g"][0], name="l0_norm_mix")
    for l in range(depth):
        lps.append(_layer_params(w, l, nb))
        x, h, sv = _layer_fwd(x, h, mem, w, lps[l], l, w["norm_mix_g"][l + 1] if l + 1 < depth else None, bias, nb,
                              gather)
        saved.append(sv)
    loss, dx, dg_final = _loss_head(x, w["final_norm_g"], target)
    small_g = [None] * depth
    dbias, pending = None, None
    g_shard = {n: lax.empty(a[n].shape, F32) for n, _ in BIG}
    for l in reversed(range(depth)):
        bottom = l == 0
        dx, g, db, g_shard = _layer_bwd(dx, mem, saved[l], w, lps[l], l, bias, nb, pos, pending=pending,
                                        g_shard=g_shard, reduce_early=bottom)
        dbias = db if dbias is None else dbias + db
        small_g[l] = g
        left = [n for n, _ in BIG if not (bottom and n in EARLY)]
        pending = _PendingReduce(_big_grad_shards(g, left), pos, l)
    pending.add(_run_comm(pending.sibling_comm(), name="reduce_sibling"))
    g_shard = pending.finish(_run_comm(pending.chips_comm(), name="reduce_chips"), g_shard)
    names = [n for n, _ in BIG]
    g_shard = dict(zip(names, _share_halves([g_shard[n] for n in names], name="reduce_share")))
    out = {}
    for n in names:
        delta, new_m, new_v = _adamw(a[n], g_shard[n], a["m_" + n], a["v_" + n], name="adamw_" + n)
        out[n] = (g_shard[n], delta, new_m, new_v)

    grads = {n: jnp.stack([small_g[l][n] for l in range(depth)]) for n in REPLICATED + CONV
             if n not in ("rel_bias", "final_norm_g")}
    grads["rel_bias"] = _dil_bias_bwd(dbias, seq)
    grads["final_norm_g"] = dg_final
    grads["loss"] = loss.reshape(1)
    spk = _Packing([(n, grads[n].shape) for n in REPLICATED + CONV + ("loss",)])
    s_all = _chip_bcast(_sibling_pair(spk.pack(grads), name="small_sibling"), name="small_chips")
    total = spk.unpack(_sum_slots(s_all, name="small_sum"))
    for n in CONV:
        width = a[n].shape[2]
        total[n] = lax.dynamic_slice_in_dim(total[n], chip * width, width, axis=2)
    apk = _Packing([(n, a[n].shape) for n in REPLICATED + CONV])
    s_out = _adamw(*[apk.pack(src)[None] for src in (
        {n: a[n] for n in REPLICATED + CONV}, total, {n: a["m_" + n] for n in REPLICATED + CONV},
        {n: a["v_" + n] for n in REPLICATED + CONV})], name="adamw_small")
    s_delta, s_m, s_v = [apk.unpack(o[0]) for o in s_out]
    for n in REPLICATED + CONV:
        out[n] = (total[n], s_delta[n], s_m[n], s_v[n])

    return (total["loss"].reshape(()), dx.reshape(nb, seq, d), *[out[n][0] for n in WEIGHTS],
            *[out[n][1] for n in WEIGHTS], *[out[n][2] for n in WEIGHTS], *[out[n][3] for n in WEIGHTS])
```

```python
import math

import numpy as np
import jax
import jax.numpy as jnp
from jax import lax
from jax.experimental import pallas as pl
from jax.experimental.pallas import tpu as pltpu

F32 = jnp.float32
BF16 = jnp.bfloat16

HEAD_DIM = 64
N_HEADS = 4
N_IN = 2820
D_FF = 2816
LRU_C = 8.0
EPS = 1e-6
NUM_BUCKETS = 32
MAX_DISTANCE = 2048
DILATED_PATTERNS = ((128, 1), (512, 4), (2048, 16))
ADAM_LR, ADAM_B1, ADAM_B2, ADAM_EPS, ADAM_WD, ADAM_STEP = 0.001, 0.9, 0.999, 1e-08, 0.01, 10

LANES = 128
SUBLANES = 8
VMEM_LIMIT = 48 * 1024 * 1024

PROJ_W = 3072
PAIR_W = 3 * LANES
LRU_W = 2 * LANES
COL_LRU = 6 * PAIR_W
COL_F = COL_LRU + 2 * LRU_W
MIX_SB, MIX_FOX, MIX_DIL, MIX_LRU = 0, 1, 2, 3
ORIG_COL = {MIX_SB: 0, MIX_FOX: 768, MIX_DIL: 1540}
ORIG_LRU_X, ORIG_LRU_G = 2308, 2564

ATT_TILE = 256
MASKED = -1e30
SCALE = HEAD_DIM ** -0.5

NT_DIMS = (((1,), (1,)), ((), ()))
TN_DIMS = (((0,), (0,)), ((), ()))

MESH = pl.DeviceIdType.MESH
ANY = pl.BlockSpec(memory_space=pl.ANY)


def _params(sem):
    return pltpu.CompilerParams(dimension_semantics=sem, vmem_limit_bytes=VMEM_LIMIT)


def _tile(n, target, unit=LANES):
    if n <= target:
        return n
    t = (target // unit) * unit
    while t > unit and n % t:
        t -= unit
    assert n % t == 0, (n, target, unit)
    return t


def _mm(a, b, *, ta=False, tb=False, res=None, col_shards=1, halves=None, norm_g=None, norm_bwd=None, comm=None,
        name, ti=1024, tj=1408, tc=1408):
    if halves == "a":
        m, kc = a.shape[1], 2 * a.shape[2]
    else:
        m, kc = (a.shape[1], a.shape[0]) if ta else a.shape
    if halves == "b":
        n = 2 * b.shape[2]
        assert b.shape[1] == kc
    else:
        n = b.shape[0] if tb else b.shape[1]
        assert (b.shape[1] if tb else b.shape[0]) == kc
    assert n % col_shards == 0
    n_blk = n // (2 if halves == "b" else col_shards)
    k_blk = kc // 2 if halves == "a" else kc
    ti, tj, tc = (_tile(m, ti, LANES if ta else SUBLANES), _tile(n_blk, tj),
                  _tile(k_blk, tc, SUBLANES if ta and tb else LANES))
    per_shard, per_half_j, per_half_k = n // col_shards // tj, n_blk // tj, k_blk // tc
    nk = kc // tc
    dims = (((0 if ta else 1,), (1 if tb else 0,)), ((), ()))
    rows_whole = norm_g is not None or norm_bwd is not None
    assert not rows_whole or (tj == n and col_shards == 1)
    n_extra = (res is not None) + (norm_g is not None) + (3 if norm_bwd is not None else 0)
    n_out = 2 if rows_whole else 1

    def finish(val, ex, outs):
        if res is not None:
            val = ex[0][...] + val
        if norm_g is not None:
            outs[0][...] = val
            outs[1][...] = (_xhat(val) * ex[-1][...]).astype(BF16)
        elif norm_bwd is not None:
            x_ref, g_ref, r_ref = ex[-3:]
            dx, dgr = _norm_bwd_rows(val, x_ref[...], g_ref[...])
            outs[0][...] = r_ref[...] + dx

            @pl.when(pl.program_id(0) == 0)
            def _():
                outs[1][...] = jnp.zeros_like(outs[1])

            outs[1][...] += jnp.sum(dgr, axis=0, keepdims=True)
        else:
            outs[0][...] = val

    def body(*refs):
        a_ref, b_ref = refs[:2]
        ex = refs[2:2 + n_extra]
        outs = refs[2 + n_extra:2 + n_extra + n_out]
        part = lax.dot_general(a_ref[...].astype(BF16), b_ref[...].astype(BF16), dims, preferred_element_type=F32)
        if nk == 1:
            finish(part, ex, outs)
            return
        acc_ref = refs[-1]
        k = pl.program_id(2)

        @pl.when(k == 0)
        def _():
            acc_ref[...] = part

        @pl.when(k > 0)
        def _():
            acc_ref[...] += part

        @pl.when(k == nk - 1)
        def _():
            finish(acc_ref[...], ex, outs)

    if halves == "a":
        a_spec = pl.BlockSpec((None, ti, tc), lambda i, j, k: (k // per_half_k, i, k % per_half_k))
    elif ta:
        a_spec = pl.BlockSpec((tc, ti), lambda i, j, k: (k, i))
    else:
        a_spec = pl.BlockSpec((ti, tc), lambda i, j, k: (i, k))
    if halves == "b":
        b_spec = pl.BlockSpec((None, tc, tj), lambda i, j, k: (j // per_half_j, k, j % per_half_j))
    elif tb:
        b_spec = pl.BlockSpec((tj, tc), lambda i, j, k: (j, k))
    else:
        b_spec = pl.BlockSpec((tc, tj), lambda i, j, k: (k, j))
    o_spec = pl.BlockSpec((ti, tj), lambda i, j, k: (i, j))
    vec = pl.BlockSpec((1, tj), lambda i, j, k: (0, 0))
    in_specs, args = [a_spec, b_spec], [a, b]
    out_specs, out_shape = [o_spec], [jax.ShapeDtypeStruct((m, n), F32)]
    if res is not None:
        in_specs.append(o_spec)
        args.append(res)
    if norm_g is not None:
        in_specs.append(vec)
        args.append(norm_g.reshape(1, n))
        out_specs.append(o_spec)
        out_shape.append(jax.ShapeDtypeStruct((m, n), BF16))
    if norm_bwd is not None:
        x, g, dres = norm_bwd
        in_specs += [o_spec, vec, o_spec]
        args += [x, g.reshape(1, n), dres]
        out_specs.append(vec)
        out_shape.append(jax.ShapeDtypeStruct((1, n), F32))
    if col_shards > 1:
        assert n_extra == 0
        out_specs = [pl.BlockSpec((None, ti, tj), lambda i, j, k: (j // per_shard, i, j % per_shard))]
        out_shape = [jax.ShapeDtypeStruct((col_shards, m, n // col_shards), F32)]
    sem = ("arbitrary",) * 3 if norm_bwd is not None else ("parallel", "parallel", "arbitrary")
    out, carried = _pallas(body, name=name, grid=(m // ti, n // tj, nk), in_specs=in_specs, out_specs=out_specs,
                           out_shape=out_shape, args=args, scratch=[] if nk == 1 else [pltpu.VMEM((ti, tj), F32)],
                           sem=sem, comm=comm)
    out = out if rows_whole else out[0]
    return out if comm is None else (out, carried)


def _xhat(x):
    return x * lax.rsqrt(jnp.mean(x * x, axis=-1, keepdims=True) + EPS)


def _norm_bwd_rows(dy, x, g):
    rstd = lax.rsqrt(jnp.mean(x * x, axis=-1, keepdims=True) + EPS)
    xh = x * rstd
    dxh = dy * g
    dx = rstd * (dxh - xh * jnp.mean(dxh * xh, axis=-1, keepdims=True))
    return dx, dy * xh


def _rmsnorm(x, g, *, name, rows=512):
    t, d = x.shape
    tr = _tile(t, rows, 2 * SUBLANES)

    def body(x_ref, g_ref, o_ref):
        o_ref[...] = (_xhat(x_ref[...]) * g_ref[...]).astype(BF16)

    return pl.pallas_call(
        body, name=name, grid=(t // tr,),
        in_specs=[pl.BlockSpec((tr, d), lambda i: (i, 0)), pl.BlockSpec((1, d), lambda i: (0, 0))],
        out_specs=pl.BlockSpec((tr, d), lambda i: (i, 0)), out_shape=jax.ShapeDtypeStruct((t, d), BF16),
        compiler_params=_params(("parallel",)))(x, g.reshape(1, d))


def _rmsnorm_bwd(dy, x, g, dres, *, name, rows=512):
    t, d = x.shape
    tr = _tile(t, rows, SUBLANES)

    def body(*refs):
        if dres is None:
            dy_ref, x_ref, g_ref, dx_ref, dg_ref = refs
        else:
            dy_ref, x_ref, g_ref, r_ref, dx_ref, dg_ref = refs
        dx, dgr = _norm_bwd_rows(dy_ref[...], x_ref[...], g_ref[...])
        dx_ref[...] = dx if dres is None else r_ref[...] + dx

        @pl.when(pl.program_id(0) == 0)
        def _():
            dg_ref[...] = jnp.zeros_like(dg_ref)

        dg_ref[...] += jnp.sum(dgr, axis=0, keepdims=True)

    row = pl.BlockSpec((tr, d), lambda i: (i, 0))
    vec = pl.BlockSpec((1, d), lambda i: (0, 0))
    in_specs = [row, row, vec] + ([] if dres is None else [row])
    args = (dy, x, g.reshape(1, d)) + (() if dres is None else (dres,))
    dx, dg = pl.pallas_call(
        body, name=name, grid=(t // tr,), in_specs=in_specs, out_specs=[row, vec],
        out_shape=[jax.ShapeDtypeStruct((t, d), F32), jax.ShapeDtypeStruct((1, d), F32)],
        compiler_params=_params(("arbitrary",)))(*args)
    return dx, dg.reshape(d)


def _loss_head(x, g, target, *, rows=512):
    t, d = x.shape
    tr = _tile(t, rows, SUBLANES)

    def body(x_ref, g_ref, t_ref, dx_ref, dg_ref, loss_ref):
        x_, g_ = x_ref[...], g_ref[...]
        err = _xhat(x_) * g_ - t_ref[...]
        dx, dgr = _norm_bwd_rows(err * (1.0 / d), x_, g_)
        dx_ref[...] = dx

        @pl.when(pl.program_id(0) == 0)
        def _():
            dg_ref[...] = jnp.zeros_like(dg_ref)
            loss_ref[...] = jnp.zeros_like(loss_ref)

        dg_ref[...] += jnp.sum(dgr, axis=0, keepdims=True)
        loss_ref[...] += 0.5 * jnp.sum(jnp.mean(err * err, axis=-1, keepdims=True), axis=0, keepdims=True)

    row = pl.BlockSpec((tr, d), lambda i: (i, 0))
    vec = pl.BlockSpec((1, d), lambda i: (0, 0))
    one = pl.BlockSpec((1, 1), lambda i: (0, 0))
    dx, dg, loss = pl.pallas_call(
        body, name="loss_head", grid=(t // tr,), in_specs=[row, vec, row], out_specs=[row, vec, one],
        out_shape=[jax.ShapeDtypeStruct((t, d), F32), jax.ShapeDtypeStruct((1, d), F32),
                   jax.ShapeDtypeStruct((1, 1), F32)],
        compiler_params=_params(("arbitrary",)))(x, g.reshape(1, d), target)
    return loss.reshape(()), dx, dg.reshape(d)


def _head_masks(shape):
    lane = lax.broadcasted_iota(jnp.int32, shape, len(shape) - 1)
    return lane < HEAD_DIM, lane >= HEAD_DIM


def _split_heads(x):
    m0, m1 = _head_masks(x.shape)
    zero = jnp.zeros_like(x)
    return jnp.where(m0, x, zero), jnp.where(m1, x, zero)


def _lane_pair(a0, a1, rows):
    m0, _ = _head_masks((rows, LANES))
    return jnp.where(m0, a0, a1)


def _qkv_readers(refs, packed):
    if packed:
        (r,) = refs
        return tuple((lambda r0, n, s=s: r[pl.ds(r0, n), s * LANES:(s + 1) * LANES]) for s in range(3))
    return tuple((lambda r0, n, ref=ref: ref[pl.ds(r0, n), :]) for ref in refs)


def _pair_spec(seq, col0, width=LANES):
    return pl.BlockSpec((seq, width), lambda p, b: (b, col0 + p))


def _fox_specs(seq, nk, tk):
    return [pl.BlockSpec((None, None, seq, 2), lambda p, b: (b, p, 0, 0)),
            pl.BlockSpec((None, None, nk, 2, tk), lambda p, b: (b, p, 0, 0, 0))]


def _softmax_attn_fwd(src, *, nb, mode, mixer=None, out_buf=None, extra=(), name, comm=None):
    packed = mode != "cross"
    n_src = 1 if packed else 3
    seq_q = (src if packed else src[0]).shape[0] // nb
    seq_k = seq_q if packed else src[1].shape[0] // nb
    tq, tk = min(ATT_TILE, seq_q), min(ATT_TILE, seq_k)
    nq, nk = seq_q // tq, seq_k // tk
    n_ex = len(extra)

    def body(*refs):
        q_at, k_at, v_at = _qkv_readers(refs[:n_src], packed)
        ex = refs[n_src:n_src + n_ex]
        o_ref, lse_ref = refs[-2:]

        def q_tile(i, _):
            r0 = pl.multiple_of(i * tq, tq)
            qm = _split_heads((q_at(r0, tq) * SCALE).astype(BF16))
            if mode == "fox":
                cq = ex[0][pl.ds(r0, tq), :]
                row = r0 + lax.broadcasted_iota(jnp.int32, (tq, tk), 0)

            def k_tile(j, carry, diagonal=False):
                m, l, acc = carry
                c0 = pl.multiple_of(j * tk, tk)
                kt = k_at(c0, tk).astype(BF16)
                vm = _split_heads(v_at(c0, tk).astype(BF16))
                if mode == "fox":
                    ck = ex[1][j]
                hs = range(2)
                s = [lax.dot_general(qm[h], kt, NT_DIMS, preferred_element_type=F32) for h in hs]
                if mode == "fox":
                    s = [s[h] + cq[:, h:h + 1] - ck[h:h + 1, :] for h in hs]
                    if diagonal:
                        keep = (c0 + lax.broadcasted_iota(jnp.int32, (tq, tk), 1)) <= row
                        s = [jnp.where(keep, s[h], MASKED) for h in hs]
                elif mode == "dil":
                    s = [s[h] + ex[0][h, i - j] for h in hs]
                new_m = [jnp.maximum(m[h], jnp.max(s[h], axis=-1, keepdims=True)) for h in hs]
                p = [jnp.exp(s[h] - new_m[h]) for h in hs]
                alpha = [jnp.exp(m[h] - new_m[h]) for h in hs]
                new_l = [alpha[h] * l[h] + jnp.sum(p[h], axis=-1, keepdims=True) for h in hs]
                pv = [jnp.dot(p[h].astype(BF16), vm[h], preferred_element_type=F32) for h in hs]
                acc = acc * _lane_pair(alpha[0], alpha[1], tq) + (pv[0] + pv[1])
                return tuple(new_m), tuple(new_l), acc

            init = ((jnp.full((tq, 1), MASKED, F32),) * 2, (jnp.zeros((tq, 1), F32),) * 2,
                    jnp.zeros((tq, LANES), F32))
            if mode == "fox":
                m, l, acc = k_tile(i, lax.fori_loop(0, i, k_tile, init), True)
            else:
                m, l, acc = lax.fori_loop(0, i + 1 if packed else nk, k_tile, init)
            o_ref[pl.ds(r0, tq), :] = acc / _lane_pair(l[0], l[1], tq)
            lse_ref[pl.ds(r0, tq), :] = _lane_pair(m[0] + jnp.log(l[0]), m[1] + jnp.log(l[1]), tq)
            return 0

        lax.fori_loop(0, nq, q_tile, 0)

    lse_shape = jax.ShapeDtypeStruct((nb * seq_q, 2 * LANES), F32)
    if packed:
        in_specs, args = [_pair_spec(seq_q, 2 * mixer, PAIR_W)], [src]
        in_specs += _fox_specs(seq_q, nk, tk) if mode == "fox" else [
            pl.BlockSpec((None, 2, nq, tq, tk), lambda p, b: (p, 0, 0, 0, 0))]
        args += list(extra) + [out_buf]
        in_specs.append(ANY)
        out_specs = [_pair_spec(seq_q, 2 * mixer), _pair_spec(seq_q, 0)]
        out_shape = [jax.ShapeDtypeStruct(out_buf.shape, F32), lse_shape]
        aliases = {len(args) - 1: 0}
    else:
        in_specs = [_pair_spec(seq_q, 0), _pair_spec(seq_k, 0), _pair_spec(seq_k, 0)]
        args = list(src)
        out_specs = [_pair_spec(seq_q, 0), _pair_spec(seq_q, 0)]
        out_shape = [lse_shape, lse_shape]
        aliases = {}
    out, carried = _pallas(body, name=name, grid=(2, nb), in_specs=in_specs, out_specs=out_specs, out_shape=out_shape,
                           args=args, aliases=aliases, sem=("parallel", "arbitrary"), comm=comm)
    return out if comm is None else (out, carried)


def _softmax_attn_bwd(src, o, lse, do, *, nb, mode, mixer=None, dbuf=None, extra=(), name, comm=None):
    packed = mode != "cross"
    n_src = 1 if packed else 3
    seq_q = (src if packed else src[0]).shape[0] // nb
    seq_k = seq_q if packed else src[1].shape[0] // nb
    tq, tk = min(ATT_TILE, seq_q), min(ATT_TILE, seq_k)
    nq, nk = seq_q // tq, seq_k // tk
    n_ex = len(extra)
    n_in = n_src + 3 + n_ex + (1 if packed else 0)

    def body(*refs):
        q_at, k_at, v_at = _qkv_readers(refs[:n_src], packed)
        o_ref, lse_ref, do_ref = refs[n_src:n_src + 3]
        ex = refs[n_src + 3:n_src + 3 + n_ex]
        outs = refs[n_in:]
        if packed:
            d_ref = outs[0]
            dq_w = lambda r0, val: d_ref.__setitem__((pl.ds(r0, tq), slice(0, LANES)), val)
            dk_ref = d_ref.at[:, LANES:2 * LANES]
            dv_ref = d_ref.at[:, 2 * LANES:3 * LANES]
        else:
            dq_ref, dk_ref, dv_ref = outs[:3]
            dq_w = lambda r0, val: dq_ref.__setitem__((pl.ds(r0, tq), slice(None)), val)
        dk_ref[...] = jnp.zeros((seq_k, LANES), F32)
        dv_ref[...] = jnp.zeros((seq_k, LANES), F32)
        if mode == "fox":
            dcum_ref, dcq_ref = outs[-2:]
            dcum_ref[...] = jnp.zeros_like(dcum_ref)
        if mode == "dil":
            dbias_ref = outs[-1]

            @pl.when(pl.program_id(1) == 0)
            def _():
                dbias_ref[...] = jnp.zeros_like(dbias_ref)

        def q_tile(i, _):
            r0 = pl.multiple_of(i * tq, tq)
            qm = _split_heads((q_at(r0, tq) * SCALE).astype(BF16))
            do_f = do_ref[pl.ds(r0, tq), :]
            dom = _split_heads(do_f.astype(BF16))
            dd = _split_heads(do_f * o_ref[pl.ds(r0, tq), :])
            delta = [jnp.sum(dd[h], axis=-1, keepdims=True) for h in range(2)]
            lse_t = lse_ref[pl.ds(r0, tq), :]
            lse_h = [lse_t[:, 0:1], lse_t[:, HEAD_DIM:HEAD_DIM + 1]]
            if mode == "fox":
                cq = ex[0][pl.ds(r0, tq), :]
                row = r0 + lax.broadcasted_iota(jnp.int32, (tq, tk), 0)

            def k_tile(j, carry, diagonal=False):
                dq, rs = carry
                c0 = pl.multiple_of(j * tk, tk)
                kt = k_at(c0, tk).astype(BF16)
                vt = v_at(c0, tk).astype(BF16)
                km = _split_heads(kt)
                if mode == "fox":
                    ck = ex[1][j]
                hs = range(2)
                s = [lax.dot_general(qm[h], kt, NT_DIMS, preferred_element_type=F32) for h in hs]
                dp = [lax.dot_general(dom[h], vt, NT_DIMS, preferred_element_type=F32) for h in hs]
                if mode == "fox":
                    s = [s[h] + cq[:, h:h + 1] - ck[h:h + 1, :] for h in hs]
                    if diagonal:
                        keep = (c0 + lax.broadcasted_iota(jnp.int32, (tq, tk), 1)) <= row
                        s = [jnp.where(keep, s[h], MASKED) for h in hs]
                elif mode == "dil":
                    s = [s[h] + ex[0][h, i - j] for h in hs]
                p = [jnp.exp(s[h] - lse_h[h]) for h in hs]
                ds = [p[h] * (dp[h] - delta[h]) for h in hs]
                dsb = [ds[h].astype(BF16) for h in hs]
                pb = [p[h].astype(BF16) for h in hs]
                dq = dq + (jnp.dot(dsb[0], km[0], preferred_element_type=F32)
                           + jnp.dot(dsb[1], km[1], preferred_element_type=F32))
                dk_t = (lax.dot_general(dsb[0], qm[0], TN_DIMS, preferred_element_type=F32)
                        + lax.dot_general(dsb[1], qm[1], TN_DIMS, preferred_element_type=F32))
                dv_t = (lax.dot_general(pb[0], dom[0], TN_DIMS, preferred_element_type=F32)
                        + lax.dot_general(pb[1], dom[1], TN_DIMS, preferred_element_type=F32))
                if mode == "fox":
                    for h in hs:
                        dcum_ref[j, h:h + 1, :] -= jnp.sum(ds[h], axis=0, keepdims=True)
                    rs = tuple(rs[h] + jnp.sum(ds[h], axis=-1, keepdims=True) for h in hs)
                elif mode == "dil":
                    for h in hs:
                        dbias_ref[h, i - j] += ds[h]
                dk_ref[pl.ds(c0, tk), :] += dk_t
                dv_ref[pl.ds(c0, tk), :] += dv_t
                return dq, rs

            zero = (jnp.zeros((tq, 1), F32),) * 2
            init = (jnp.zeros((tq, LANES), F32), zero)
            if mode == "fox":
                dq, rs = k_tile(i, lax.fori_loop(0, i, k_tile, init), True)
            else:
                dq, rs = lax.fori_loop(0, i + 1 if packed else nk, k_tile, init)
            dq_w(r0, dq * SCALE)
            if mode == "fox":
                dcq_ref[pl.ds(r0, tq), :] = jnp.where(lax.broadcasted_iota(jnp.int32, (tq, 2), 1) == 0, rs[0], rs[1])
            return 0

        lax.fori_loop(0, nq, q_tile, 0)

    if packed:
        in_specs = [_pair_spec(seq_q, 2 * mixer, PAIR_W), _pair_spec(seq_q, 2 * mixer), _pair_spec(seq_q, 0),
                    _pair_spec(seq_q, 2 * mixer)]
        args = [src, o, lse, do]
        out_specs = [_pair_spec(seq_q, 2 * mixer, PAIR_W)]
        out_shape = [jax.ShapeDtypeStruct(dbuf.shape, F32)]
        if mode == "fox":
            in_specs += _fox_specs(seq_q, nk, tk)
            out_specs += [_fox_specs(seq_q, nk, tk)[1], _fox_specs(seq_q, nk, tk)[0]]
            out_shape += [jax.ShapeDtypeStruct((nb, 2, nk, 2, tk), F32), jax.ShapeDtypeStruct((nb, 2, seq_q, 2), F32)]
        else:
            tiles = pl.BlockSpec((None, 2, nq, tq, tk), lambda p, b: (p, 0, 0, 0, 0))
            in_specs.append(tiles)
            out_specs.append(tiles)
            out_shape.append(jax.ShapeDtypeStruct((2, 2, nq, tq, tk), F32))
        args += list(extra) + [dbuf]
        in_specs.append(ANY)
        aliases = {len(args) - 1: 0}
    else:
        sq, sk = _pair_spec(seq_q, 0), _pair_spec(seq_k, 0)
        in_specs, args = [sq, sk, sk, sq, sq, sq], list(src) + [o, lse, do]
        out_specs = [sq, sk, sk]
        out_shape = [jax.ShapeDtypeStruct((nb * seq_q, 2 * LANES), F32)] + [
            jax.ShapeDtypeStruct((nb * seq_k, 2 * LANES), F32)] * 2
        aliases = {}
    out, carried = _pallas(body, name=name, grid=(2, nb), in_specs=in_specs, out_specs=out_specs, out_shape=out_shape,
                           args=args, aliases=aliases, sem=("parallel", "arbitrary"), comm=comm)
    return out if comm is None else (out, carried)


def _log_sigmoid(z):
    return jnp.minimum(z, 0.0) - jnp.log(1.0 + jnp.exp(-jnp.abs(z)))


def _split_bf16(x):
    hi = x.astype(BF16)
    return hi, (x - hi.astype(F32)).astype(BF16)


def _tri(n, fn):
    r = lax.broadcasted_iota(jnp.int32, (n, n), 0)
    c = lax.broadcasted_iota(jnp.int32, (n, n), 1)
    return jnp.where(fn(r, c), 1.0, 0.0).astype(BF16)


def _sb_attn_fwd(proj, out_buf, *, nb, name, comm=None):
    seq = proj.shape[0] // nb
    tq = tk = min(ATT_TILE, seq)
    nq = seq // tq

    def body(qkv_ref, _, o_ref, lt_ref):
        rd = [_qkv_readers((qkv_ref.at[:, pr * PAIR_W:(pr + 1) * PAIR_W],), True) for pr in range(2)]
        after = _tri(tk, lambda r, c: r > c)
        ch = [(pr, h) for pr in range(2) for h in range(2)]

        def q_tile(i, _):
            r0 = pl.multiple_of(i * tq, tq)
            qm = [_split_heads((rd[pr][0](r0, tq) * SCALE).astype(BF16)) for pr in range(2)]
            row = r0 + lax.broadcasted_iota(jnp.int32, (tq, tk), 0)

            def k_tile(j, carry, diagonal):
                c, acc = carry
                c0 = pl.multiple_of(j * tk, tk)
                kt = [rd[pr][1](c0, tk).astype(BF16) for pr in range(2)]
                vm = [_split_heads(rd[pr][2](c0, tk).astype(BF16)) for pr in range(2)]
                if diagonal:
                    strict = (c0 + lax.broadcasted_iota(jnp.int32, (tq, tk), 1)) < row
                ns = range(len(ch))
                z = [lax.dot_general(qm[pr][h], kt[pr], NT_DIMS, preferred_element_type=F32) for pr, h in ch]
                ls = [_log_sigmoid(z[n]) for n in ns]
                lk = [ls[n] - z[n] for n in ns]
                if diagonal:
                    lk = [jnp.where(strict, lk[n], 0.0) for n in ns]
                parts = [_split_bf16(lk[n]) for n in ns]
                sfx = [jnp.dot(parts[n][0], after, preferred_element_type=F32)
                       + jnp.dot(parts[n][1], after, preferred_element_type=F32) for n in ns]
                att = [jnp.exp(ls[n] + sfx[n] + c[n]) for n in ns]
                if diagonal:
                    att = [jnp.where(strict, att[n], 0.0) for n in ns]
                acc = tuple(acc[pr] + (jnp.dot(att[2 * pr].astype(BF16), vm[pr][0], preferred_element_type=F32)
                                       + jnp.dot(att[2 * pr + 1].astype(BF16), vm[pr][1], preferred_element_type=F32))
                            for pr in range(2))
                return tuple(c[n] + jnp.sum(lk[n], axis=-1, keepdims=True) for n in ns), acc

            init = ((jnp.zeros((tq, 1), F32),) * 4, (jnp.zeros((tq, LANES), F32),) * 2)
            c, acc = lax.fori_loop(1, i + 1, lambda jj, cr: k_tile(i - jj, cr, False), k_tile(i, init, True))
            for pr in range(2):
                o_ref[pl.ds(r0, tq), pr * LANES:(pr + 1) * LANES] = acc[pr]
                lt_ref[pl.ds(r0, tq), pr * LANES:(pr + 1) * LANES] = _lane_pair(c[2 * pr], c[2 * pr + 1], tq)
            return 0

        lax.fori_loop(0, nq, q_tile, 0)

    both = lambda width, col: pl.BlockSpec((seq, 2 * width), lambda b: (b, col))
    out, carried = _pallas(
        body, name=name, grid=(nb,), in_specs=[both(PAIR_W, MIX_SB), ANY],
        out_specs=[both(LANES, MIX_SB), both(LANES, 0)],
        out_shape=[jax.ShapeDtypeStruct(out_buf.shape, F32), jax.ShapeDtypeStruct((nb * seq, 2 * LANES), F32)],
        args=[proj, out_buf], aliases={1: 0}, sem=("arbitrary",), comm=comm)
    return out if comm is None else (out, carried)


def _sb_attn_bwd(proj, ltot, do, dbuf, *, nb, name, comm=None):
    seq = proj.shape[0] // nb
    tq = tk = min(ATT_TILE, seq)
    nq = seq // tq

    def body(qkv_ref, lt_ref, do_ref, _, d_ref):
        rd = [_qkv_readers((qkv_ref.at[:, pr * PAIR_W:(pr + 1) * PAIR_W],), True) for pr in range(2)]
        upto = _tri(tk, lambda r, c: r <= c)
        before = _tri(tk, lambda r, c: r < c)
        dk_ref = [d_ref.at[:, pr * PAIR_W + LANES:pr * PAIR_W + 2 * LANES] for pr in range(2)]
        dv_ref = [d_ref.at[:, pr * PAIR_W + 2 * LANES:(pr + 1) * PAIR_W] for pr in range(2)]
        for ref in dk_ref + dv_ref:
            ref[...] = jnp.zeros((seq, LANES), F32)
        ch = [(pr, h) for pr in range(2) for h in range(2)]

        def q_tile(i, _):
            r0 = pl.multiple_of(i * tq, tq)
            qm = [_split_heads((rd[pr][0](r0, tq) * SCALE).astype(BF16)) for pr in range(2)]
            dom = [_split_heads(do_ref[pl.ds(r0, tq), pr * LANES:(pr + 1) * LANES].astype(BF16)) for pr in range(2)]
            lt_t = lt_ref[pl.ds(r0, tq), :]
            lt_h = [lt_t[:, pr * LANES + h * HEAD_DIM:pr * LANES + h * HEAD_DIM + 1] for pr, h in ch]
            row = r0 + lax.broadcasted_iota(jnp.int32, (tq, tk), 0)

            def k_tile(j, carry, diagonal):
                pc, qc, dq = carry
                c0 = pl.multiple_of(j * tk, tk)
                kt = [rd[pr][1](c0, tk).astype(BF16) for pr in range(2)]
                vt = [rd[pr][2](c0, tk).astype(BF16) for pr in range(2)]
                km = [_split_heads(kt[pr]) for pr in range(2)]
                if diagonal:
                    strict = (c0 + lax.broadcasted_iota(jnp.int32, (tq, tk), 1)) < row
                ns = range(len(ch))
                z = [lax.dot_general(qm[pr][h], kt[pr], NT_DIMS, preferred_element_type=F32) for pr, h in ch]
                da = [lax.dot_general(dom[pr][h], vt[pr], NT_DIMS, preferred_element_type=F32) for pr, h in ch]
                ls = [_log_sigmoid(z[n]) for n in ns]
                lk = [ls[n] - z[n] for n in ns]
                if diagonal:
                    lk = [jnp.where(strict, lk[n], 0.0) for n in ns]
                parts = [_split_bf16(lk[n]) for n in ns]
                pin = [jnp.dot(parts[n][0], upto, preferred_element_type=F32)
                       + jnp.dot(parts[n][1], upto, preferred_element_type=F32) for n in ns]
                att = [jnp.exp(ls[n] + (lt_h[n] - pc[n] - pin[n])) for n in ns]
                if diagonal:
                    att = [jnp.where(strict, att[n], 0.0) for n in ns]
                dg = [att[n] * da[n] for n in ns]
                qx = [qc[n] + jnp.dot(dg[n].astype(BF16), before, preferred_element_type=F32) for n in ns]
                sig = [jnp.exp(ls[n]) for n in ns]
                dz = [dg[n] * (1.0 - sig[n]) - sig[n] * qx[n] for n in ns]
                if diagonal:
                    dz = [jnp.where(strict, dz[n], 0.0) for n in ns]
                dzb = [dz[n].astype(BF16) for n in ns]
                attb = [att[n].astype(BF16) for n in ns]
                new_dq = []
                for pr in range(2):
                    a, b = 2 * pr, 2 * pr + 1
                    new_dq.append(dq[pr] + (jnp.dot(dzb[a], km[pr][0], preferred_element_type=F32)
                                            + jnp.dot(dzb[b], km[pr][1], preferred_element_type=F32)))
                    dk_ref[pr][pl.ds(c0, tk), :] += (
                        lax.dot_general(dzb[a], qm[pr][0], TN_DIMS, preferred_element_type=F32)
                        + lax.dot_general(dzb[b], qm[pr][1], TN_DIMS, preferred_element_type=F32))
                    dv_ref[pr][pl.ds(c0, tk), :] += (
                        lax.dot_general(attb[a], dom[pr][0], TN_DIMS, preferred_element_type=F32)
                        + lax.dot_general(attb[b], dom[pr][1], TN_DIMS, preferred_element_type=F32))
                return (tuple(pc[n] + jnp.sum(lk[n], axis=-1, keepdims=True) for n in ns),
                        tuple(qc[n] + jnp.sum(dg[n], axis=-1, keepdims=True) for n in ns), tuple(new_dq))

            zero = (jnp.zeros((tq, 1), F32),) * 4
            init = (zero, zero, (jnp.zeros((tq, LANES), F32),) * 2)
            carry = lax.fori_loop(0, i, lambda j, cr: k_tile(j, cr, False), init)
            _, _, dq = k_tile(i, carry, True)
            for pr in range(2):
                d_ref[pl.ds(r0, tq), pr * PAIR_W:pr * PAIR_W + LANES] = dq[pr] * SCALE
            return 0

        lax.fori_loop(0, nq, q_tile, 0)

    both = lambda width, col: pl.BlockSpec((seq, 2 * width), lambda b: (b, col))
    out, carried = _pallas(
        body, name=name, grid=(nb,), in_specs=[both(PAIR_W, MIX_SB), both(LANES, 0), both(LANES, MIX_SB), ANY],
        out_specs=[both(PAIR_W, MIX_SB)], out_shape=[jax.ShapeDtypeStruct(dbuf.shape, F32)],
        args=[proj, ltot, do, dbuf], aliases={3: 0}, sem=("arbitrary",), comm=comm)
    return out[0] if comm is None else (out[0], carried)


def _lane_scan(x, reverse=False):
    n = x.shape[-1]
    lane = lax.broadcasted_iota(jnp.int32, x.shape, 1)
    k = 1
    while k < n:
        if reverse:
            x = x + jnp.where(lane < n - k, pltpu.roll(x, n - k, 1), 0.0)
        else:
            x = x + jnp.where(lane >= k, pltpu.roll(x, k, 1), 0.0)
        k *= 2
    return x


def _fox_gate_fwd(f_rows, b_rows):
    def body(f_ref, b_ref, o_ref):
        o_ref[...] = _lane_scan(_log_sigmoid(f_ref[...] + b_ref[...]))

    return pl.pallas_call(body, name="fox_gate_fwd", out_shape=jax.ShapeDtypeStruct(f_rows.shape, F32))(f_rows, b_rows)


def _fox_gate_bwd(dcum, f_rows, b_rows):
    def body(d_ref, f_ref, b_ref, df_ref, db_ref):
        z = f_ref[...] + b_ref[...]
        df = _lane_scan(d_ref[...], reverse=True) * jnp.exp(_log_sigmoid(-z))
        df_ref[...] = df
        rs = jnp.sum(df, axis=-1, keepdims=True)
        tot = rs
        for e in range(1, f_rows.shape[0] // N_HEADS):
            tot = tot + pltpu.roll(rs, e * N_HEADS, 0)
        db_ref[...] = tot

    return pl.pallas_call(
        body, name="fox_gate_bwd",
        out_shape=[jax.ShapeDtypeStruct(f_rows.shape, F32), jax.ShapeDtypeStruct((f_rows.shape[0], 1), F32)],
    )(dcum, f_rows, b_rows)


def _dil_tables(seq):
    t = min(ATT_TILE, seq)
    n = seq // t
    a = np.arange(t)
    d = (np.arange(n)[:, None, None] * t + a[None, :, None] - a[None, None, :]).astype(np.int64)
    count = np.zeros(d.shape, np.int64)
    for window, dil in DILATED_PATTERNS:
        count += (d >= 0) & (d % dil == 0) & (d // dil <= window // dil)
    nn = np.maximum(d, 0)
    max_exact = NUM_BUCKETS // 2
    nf = np.maximum(nn, 1).astype(np.float32)
    large = max_exact + (np.log(nf / np.float32(max_exact)) / np.float32(math.log(MAX_DISTANCE / max_exact))
                         * np.float32(NUM_BUCKETS - max_exact)).astype(np.int32)
    bucket = np.where(nn < max_exact, nn, np.minimum(large, NUM_BUCKETS - 1))
    bucket = np.where(count > 0, bucket, -1).astype(np.int32)
    logc = np.where(count > 0, np.log(np.maximum(count, 1)), MASKED).astype(np.float32)
    return bucket, logc


def _dil_bias(rel_bias, seq):
    bucket, logc = _dil_tables(seq)
    n, t, _ = bucket.shape

    def body(rb_ref, bk_ref, lc_ref, o_ref):
        h = pl.program_id(0) * 2 + pl.program_id(1)
        bk = bk_ref[...]
        out = lc_ref[...]
        for b in range(NUM_BUCKETS):
            out = jnp.where(bk == b, out + rb_ref[b, h], out)
        o_ref[...] = out

    full = pl.BlockSpec((n, t, t), lambda p, h: (0, 0, 0))
    return pl.pallas_call(
        body, name="dil_bias", grid=(2, 2),
        in_specs=[pl.BlockSpec(memory_space=pltpu.SMEM), full, full],
        out_specs=pl.BlockSpec((None, None, n, t, t), lambda p, h: (p, h, 0, 0, 0)),
        out_shape=jax.ShapeDtypeStruct((2, 2, n, t, t), F32),
        compiler_params=_params(("parallel", "parallel")))(rel_bias, jnp.asarray(bucket), jnp.asarray(logc))


def _dil_bias_bwd(dbias, seq):
    bucket, _ = _dil_tables(seq)
    n, t, _ = bucket.shape

    def body(d_ref, bk_ref, o_ref):
        bk = bk_ref[...]
        lane = lax.broadcasted_iota(jnp.int32, (1, LANES), 1)
        for b in range(NUM_BUCKETS):
            rowv = jnp.zeros((1, LANES), F32)
            for h in range(N_HEADS):
                s = jnp.sum(jnp.where(bk == b, d_ref[h // 2, h % 2], 0.0))
                rowv = jnp.where(lane == h, s, rowv)
            o_ref[b:b + 1, :] = rowv

    out = pl.pallas_call(body, name="dil_bias_bwd", out_shape=jax.ShapeDtypeStruct((NUM_BUCKETS, LANES), F32),
                         compiler_params=pltpu.CompilerParams(vmem_limit_bytes=VMEM_LIMIT))(dbias, jnp.asarray(bucket))
    return out[:, :N_HEADS]


def _shift_rows(x, k, row, fill=0.0):
    n = x.shape[0]
    if k > 0:
        return jnp.where(row >= k, pltpu.roll(x, k, 0), fill)
    return jnp.where(row < n + k, pltpu.roll(x, n + k, 0), fill)


def _row_scan(a, u, row, reverse=False):
    n = a.shape[0]
    k = 1
    while k < n:
        s = -k if reverse else k
        u = a * _shift_rows(u, s, row) + u
        a = a * _shift_rows(a, s, row, 1.0)
        k *= 2
    return u


def _sigmoid(x):
    return 1.0 / (1.0 + jnp.exp(-x))


def _gelu(g):
    return 0.5 * g * (1.0 + lax.erf(g * (2.0 ** -0.5)))


def _gelu_grad(g):
    return 0.5 * (1.0 + lax.erf(g * (2.0 ** -0.5))) + g * jnp.exp(-0.5 * g * g) * (1.0 / math.sqrt(2.0 * math.pi))


def _neg_expm1(x):
    small = -x * (1.0 + x * (0.5 + x * (1.0 / 6.0 + x * (1.0 / 24.0))))
    return jnp.where(x > -0.03, small, 1.0 - jnp.exp(x))


def _lru_core(x, vec, wa, wx, row):
    xs = [_shift_rows(x, 3 - j, row) if j < 3 else x for j in range(4)]
    xc = vec[4:5, :]
    for j in range(4):
        xc = xc + vec[j:j + 1, :] * xs[j]
    xcb = xc.astype(BF16)
    r = _sigmoid(jnp.dot(xcb, wa, preferred_element_type=F32) + vec[5:6, :])
    ig = _sigmoid(jnp.dot(xcb, wx, preferred_element_type=F32) + vec[6:7, :])
    lam = vec[7:8, :]
    sp = jnp.maximum(-lam, 0.0) - _log_sigmoid(jnp.abs(lam))
    la = -LRU_C * r * sp
    a = jnp.exp(la)
    mult = jnp.sqrt(_neg_expm1(2.0 * la))
    return xs, xc, xcb, r, ig, sp, la, a, mult


def _lru_specs(seq):
    xg = pl.BlockSpec((seq, LRU_W), lambda hf, b: (b, COL_LRU // LRU_W + hf))
    mix = pl.BlockSpec((seq, LANES), lambda hf, b: (b, 2 * MIX_LRU + hf))
    vec = pl.BlockSpec((SUBLANES, LANES), lambda hf, b: (0, hf))
    mat = pl.BlockSpec((None, LANES, LANES), lambda hf, b: (hf, 0, 0))
    return xg, mix, vec, mat


def _lru_fwd(proj, vec, wa, wx, out_buf, *, nb, name):
    seq = proj.shape[0] // nb

    def body(xg_ref, vec_ref, wa_ref, wx_ref, _, o_ref):
        row = lax.broadcasted_iota(jnp.int32, (seq, LANES), 0)
        _, xc, _, _, ig, _, _, a, mult = _lru_core(xg_ref[:, 0:LANES], vec_ref[...], wa_ref[...], wx_ref[...], row)
        h = _row_scan(a, mult * (ig * xc), row)
        o_ref[...] = h * _gelu(xg_ref[:, LANES:LRU_W])

    xg, mix, vecs, mat = _lru_specs(seq)
    return pl.pallas_call(
        body, name=name, grid=(2, nb), in_specs=[xg, vecs, mat, mat, ANY], out_specs=mix,
        out_shape=jax.ShapeDtypeStruct(out_buf.shape, F32), input_output_aliases={4: 0},
        compiler_params=_params(("parallel", "arbitrary")))(proj, vec, wa, wx, out_buf)


def _lru_bwd(proj, vec, wa, wx, dout, dbuf, *, nb, name):
    seq = proj.shape[0] // nb

    def body(xg_ref, vec_ref, wa_ref, wx_ref, do_ref, _, d_ref, dvec_ref, dwa_ref, dwx_ref):
        row = lax.broadcasted_iota(jnp.int32, (seq, LANES), 0)
        vec_, wa_, wx_ = vec_ref[...], wa_ref[...], wx_ref[...]
        xs, xc, xcb, r, ig, sp, la, a, mult = _lru_core(xg_ref[:, 0:LANES], vec_, wa_, wx_, row)
        h = _row_scan(a, mult * (ig * xc), row)
        gate, do = xg_ref[:, LANES:LRU_W], do_ref[...]
        d_ref[:, LANES:LRU_W] = do * h * _gelu_grad(gate)
        dh = do * _gelu(gate)
        gacc = _row_scan(_shift_rows(a, -1, row), dh, row, reverse=True)
        da = gacc * _shift_rows(h, 1, row)
        dmult = gacc * (ig * xc)
        dig = gacc * (mult * xc)
        dxc = gacc * (mult * ig)
        dla = da * a - dmult * (a * a) / mult
        dr = (-LRU_C) * sp * dla
        dsp = jnp.sum((-LRU_C) * r * dla, axis=0, keepdims=True)
        dpr = dr * r * (1.0 - r)
        dpi = dig * ig * (1.0 - ig)
        dprb, dpib = dpr.astype(BF16), dpi.astype(BF16)
        dxc = (dxc + lax.dot_general(dprb, wa_, NT_DIMS, preferred_element_type=F32)
               + lax.dot_general(dpib, wx_, NT_DIMS, preferred_element_type=F32))
        dx = vec_[3:4, :] * dxc
        for j in range(3):
            dx = dx + vec_[j:j + 1, :] * _shift_rows(dxc, -(3 - j), row)
        d_ref[:, 0:LANES] = dx

        @pl.when(pl.program_id(1) == 0)
        def _():
            dvec_ref[...] = jnp.zeros_like(dvec_ref)
            dwa_ref[...] = jnp.zeros_like(dwa_ref)
            dwx_ref[...] = jnp.zeros_like(dwx_ref)

        for j in range(4):
            dvec_ref[j:j + 1, :] += jnp.sum(dxc * xs[j], axis=0, keepdims=True)
        dvec_ref[4:5, :] += jnp.sum(dxc, axis=0, keepdims=True)
        dvec_ref[5:6, :] += jnp.sum(dpr, axis=0, keepdims=True)
        dvec_ref[6:7, :] += jnp.sum(dpi, axis=0, keepdims=True)
        lam = vec_[7:8, :]
        dvec_ref[7:8, :] += -dsp * _sigmoid(-lam)
        dwa_ref[...] += lax.dot_general(xcb, dprb, TN_DIMS, preferred_element_type=F32)
        dwx_ref[...] += lax.dot_general(xcb, dpib, TN_DIMS, preferred_element_type=F32)

    xg, mix, vecs, mat = _lru_specs(seq)
    return pl.pallas_call(
        body, name=name, grid=(2, nb), in_specs=[xg, vecs, mat, mat, mix, ANY], out_specs=[xg, vecs, mat, mat],
        out_shape=[jax.ShapeDtypeStruct(dbuf.shape, F32), jax.ShapeDtypeStruct((SUBLANES, 2 * LANES), F32),
                   jax.ShapeDtypeStruct((2, LANES, LANES), F32), jax.ShapeDtypeStruct((2, LANES, LANES), F32)],
        input_output_aliases={5: 0},
        compiler_params=_params(("parallel", "arbitrary")))(proj, vec, wa, wx, dout, dbuf)


FFN_ROWS = 256
FFN_COLS = 1408


def _with_halo(halo, x, k):
    xx = jnp.concatenate([halo, x], axis=0)
    return pltpu.roll(xx, k, 0)[SUBLANES:, :]


def _ffn_conv(x_ref, halo_ref, cw, pos):
    x, halo = x_ref[...], halo_ref[...]
    x1 = jnp.where(pos >= 1, _with_halo(halo, x, 1), 0.0)
    x2 = jnp.where(pos >= 2, _with_halo(halo, x, 2), 0.0)
    return cw[3:4, :] + cw[0:1, :] * x2 + cw[1:2, :] * x1 + cw[2:3, :] * x, x1, x2


def _ffn_specs(tm, tn, gate_off):
    prev = lambda i: jnp.maximum(i * (tm // SUBLANES) - 1, 0)
    up = pl.BlockSpec((tm, tn), lambda j, i: (i, j))
    gate = pl.BlockSpec((tm, tn), lambda j, i: (i, j + gate_off))
    up_h = pl.BlockSpec((SUBLANES, tn), lambda j, i: (prev(i), j))
    gate_h = pl.BlockSpec((SUBLANES, tn), lambda j, i: (prev(i), j + gate_off))
    cw_up = pl.BlockSpec((SUBLANES, tn), lambda j, i: (0, j))
    cw_gate = pl.BlockSpec((SUBLANES, tn), lambda j, i: (0, j + gate_off))
    return up, gate, up_h, gate_h, cw_up, cw_gate


def _ffn_act(hf, cw, *, seq, name):
    t, w2 = hf.shape
    w = w2 // 2
    tm, tn = _tile(seq, FFN_ROWS, SUBLANES), _tile(w, FFN_COLS)

    def body(u_ref, g_ref, uh_ref, gh_ref, cu_ref, cg_ref, o_ref):
        pos = (pl.program_id(1) * tm + lax.broadcasted_iota(jnp.int32, (tm, 1), 0)) % seq
        up, _, _ = _ffn_conv(u_ref, uh_ref, cu_ref[...], pos)
        gate, _, _ = _ffn_conv(g_ref, gh_ref, cg_ref[...], pos)
        o_ref[...] = (_gelu(gate) * up).astype(BF16)

    specs = _ffn_specs(tm, tn, w // tn)
    return pl.pallas_call(
        body, name=name, grid=(w // tn, t // tm), in_specs=list(specs), out_specs=specs[0],
        out_shape=jax.ShapeDtypeStruct((t, w), BF16),
        compiler_params=_params(("parallel", "parallel")))(hf, hf, hf, hf, cw, cw)


def _ffn_bwd(hf, cw, dact, *, seq, name, comm=None):
    t, w2 = hf.shape
    w = w2 // 2
    tm, tn = _tile(seq, FFN_ROWS, 2 * SUBLANES), _tile(w, FFN_COLS)
    ext = tm + SUBLANES
    last = t // SUBLANES - 1

    def body(u_ref, g_ref, uh_ref, gh_ref, cu_ref, cg_ref, un_ref, gn_ref, da_ref, dn_ref, d_ref, dcu_ref, dcg_ref):
        pos = (pl.program_id(1) * tm + lax.broadcasted_iota(jnp.int32, (ext, 1), 0)) % seq

        def conv(x_ref, prev_ref, next_ref, cwv):
            xx = jnp.concatenate([prev_ref[...], x_ref[...], next_ref[...]], axis=0)
            x1 = jnp.where(pos >= 1, pltpu.roll(xx, 1, 0)[SUBLANES:, :], 0.0)
            x2 = jnp.where(pos >= 2, pltpu.roll(xx, 2, 0)[SUBLANES:, :], 0.0)
            x0 = xx[SUBLANES:, :]
            return cwv[3:4, :] + cwv[0:1, :] * x2 + cwv[1:2, :] * x1 + cwv[2:3, :] * x0, (x2, x1, x0)

        def back(d, cwv):
            d1 = jnp.where(pos < seq - 1, pltpu.roll(d, ext - 1, 0), 0.0)
            d2 = jnp.where(pos < seq - 2, pltpu.roll(d, ext - 2, 0), 0.0)
            return (cwv[2:3, :] * d + cwv[1:2, :] * d1 + cwv[0:1, :] * d2)[:tm, :].astype(BF16)

        cu, cg = cu_ref[...], cg_ref[...]
        up, u_taps = conv(u_ref, uh_ref, un_ref, cu)
        gate, g_taps = conv(g_ref, gh_ref, gn_ref, cg)
        da = jnp.concatenate([da_ref[...], dn_ref[...]], axis=0)
        cdf = 0.5 * (1.0 + lax.erf(gate * (2.0 ** -0.5)))
        d_up = da * (gate * cdf)
        d_gate = da * up * (cdf + gate * jnp.exp(-0.5 * gate * gate) * (1.0 / math.sqrt(2.0 * math.pi)))
        d_ref[0] = back(d_up, cu)
        d_ref[1] = back(d_gate, cg)

        @pl.when(pl.program_id(1) == 0)
        def _():
            dcu_ref[...] = jnp.zeros_like(dcu_ref)
            dcg_ref[...] = jnp.zeros_like(dcg_ref)

        for ref, d, taps in ((dcu_ref, d_up, u_taps), (dcg_ref, d_gate, g_taps)):
            own = d[:tm, :]
            for j in range(3):
                ref[j:j + 1, :] += jnp.sum(own * taps[j][:tm, :], axis=0, keepdims=True)
            ref[3:4, :] += jnp.sum(own, axis=0, keepdims=True)

    gate_off = w // tn
    specs = _ffn_specs(tm, tn, gate_off)
    tile, cwt = specs[0], specs[4]
    nxt = lambda i: jnp.minimum((i + 1) * (tm // SUBLANES), last)
    up_n = pl.BlockSpec((SUBLANES, tn), lambda j, i: (nxt(i), j))
    gate_n = pl.BlockSpec((SUBLANES, tn), lambda j, i: (nxt(i), j + gate_off))
    out, carried = _pallas(
        body, name=name, grid=(w // tn, t // tm), in_specs=list(specs) + [up_n, gate_n, tile, up_n],
        out_specs=[pl.BlockSpec((2, tm, tn), lambda j, i: (0, i, j)), cwt, cwt],
        out_shape=[jax.ShapeDtypeStruct((2, t, w), BF16), jax.ShapeDtypeStruct((SUBLANES, w), F32),
                   jax.ShapeDtypeStruct((SUBLANES, w), F32)],
        args=[hf, hf, hf, hf, cw, cw, hf, hf, dact, dact], sem=("parallel", "arbitrary"), comm=comm)
    return out if comm is None else (out, carried)


def _adamw(w, g, m, v, *, name, rows=256):
    nl, r, c = w.shape
    tr = _tile(r, rows, SUBLANES)

    def body(w_ref, g_ref, m_ref, v_ref, d_ref, nm_ref, nv_ref):
        g_ = g_ref[...]
        nm = ADAM_B1 * m_ref[...] + (1.0 - ADAM_B1) * g_
        nv = ADAM_B2 * v_ref[...] + (1.0 - ADAM_B2) * (g_ * g_)
        m_hat = nm / (1.0 - ADAM_B1 ** ADAM_STEP)
        v_hat = nv / (1.0 - ADAM_B2 ** ADAM_STEP)
        d_ref[...] = -ADAM_LR * (m_hat / (jnp.sqrt(v_hat) + ADAM_EPS) + ADAM_WD * w_ref[...])
        nm_ref[...] = nm
        nv_ref[...] = nv

    spec = pl.BlockSpec((None, tr, c), lambda l, i: (l, i, 0))
    shape = jax.ShapeDtypeStruct((nl, r, c), F32)
    return pl.pallas_call(body, name=name, grid=(nl, r // tr), in_specs=[spec] * 4, out_specs=[spec] * 3,
                          out_shape=[shape] * 3, compiler_params=_params(("parallel", "parallel")))(w, g, m, v)


def _mesh_pos():
    return lax.axis_index("x"), lax.axis_index("y"), lax.axis_index("c")


def _peers(x, y):
    chips = [(1 - x, y), (x, 1 - y), (1 - x, 1 - y)]
    return [(px, py, 2 * px + py) for px, py in chips]


def _remote(src, dst, send_sems, recv_sems, idx, to):
    return pltpu.make_async_remote_copy(src, dst, send_sems.at[idx], recv_sems.at[idx], device_id=to,
                                        device_id_type=MESH)


class _Comm:
    def __init__(self, operands, out_shape, aliases, sems, copies):
        self.operands, self.out_shape, self.aliases, self.sems, self.copies = operands, out_shape, aliases, sems, copies

    def start(self, ins, outs, sems):
        for send, _ in self.copies(ins, outs, sems):
            send.start()

    def wait(self, ins, outs, sems):
        pairs = self.copies(ins, outs, sems)
        for _, recv in pairs:
            recv.wait_recv()
        for send, _ in pairs:
            send.wait_send()


def _pallas(body, *, name, grid, in_specs, out_specs, out_shape, args, aliases=None, scratch=(), sem, comm=None):
    n_in, n_out = len(in_specs), len(out_specs)
    aliases = dict(aliases or {})
    if comm is None:
        out = pl.pallas_call(body, name=name, grid=grid, in_specs=in_specs, out_specs=out_specs, out_shape=out_shape,
                             input_output_aliases=aliases, scratch_shapes=list(scratch),
                             compiler_params=_params(sem))(*args)
        return list(out), []
    nci, nco, ncs = len(comm.operands), len(comm.out_shape), len(comm.sems)

    def carried(*refs):
        ins, cin = refs[:n_in], refs[n_in:n_in + nci]
        o0 = n_in + nci
        outs, cout = refs[o0:o0 + n_out], refs[o0 + n_out:o0 + n_out + nco]
        s0 = o0 + n_out + nco
        own, csem = refs[s0:len(refs) - ncs], refs[len(refs) - ncs:]
        ids = [pl.program_id(ax) for ax in range(len(grid))]
        first, last = ids[0] == 0, ids[0] == grid[0] - 1
        for i, g in zip(ids[1:], grid[1:]):
            first, last = jnp.logical_and(first, i == 0), jnp.logical_and(last, i == g - 1)

        @pl.when(first)
        def _():
            comm.start(cin, cout, csem)

        body(*ins, *outs, *own)

        @pl.when(last)
        def _():
            comm.wait(cin, cout, csem)

    aliases.update({n_in + i: n_out + j for i, j in comm.aliases.items()})
    out = pl.pallas_call(
        carried, name=name, grid=grid, in_specs=list(in_specs) + [ANY] * nci, out_specs=list(out_specs) + [ANY] * nco,
        out_shape=list(out_shape) + list(comm.out_shape), input_output_aliases=aliases,
        scratch_shapes=list(scratch) + list(comm.sems),
        compiler_params=_params(("arbitrary",) * len(grid)))(*args, *comm.operands)
    return list(out[:n_out]), list(out[n_out:])


def _run_comm(comm, *, name):
    nci, nco = len(comm.operands), len(comm.out_shape)

    def body(*refs):
        ins, outs, sems = refs[:nci], refs[nci:nci + nco], refs[nci + nco:]
        comm.start(ins, outs, sems)
        comm.wait(ins, outs, sems)

    return pl.pallas_call(body, name=name, in_specs=[ANY] * nci, out_specs=[ANY] * nco, out_shape=list(comm.out_shape),
                          input_output_aliases=dict(comm.aliases), scratch_shapes=list(comm.sems))(*comm.operands)


def _pair_sems(*shape):
    return [pltpu.SemaphoreType.DMA(shape), pltpu.SemaphoreType.DMA(shape)]


def _gather_comm(bufs, layer, stage):
    n = len(bufs)

    def copies(ins, outs, sems):
        x, y, c = _mesh_pos()
        me = 2 * x + y
        pairs = []
        for i in range(n):
            h = bufs[i].shape[2] // 2
            mine, other = pl.ds(c * h, h), pl.ds((1 - c) * h, h)
            for r, (px, py, k) in enumerate(_peers(x, y)):
                if stage == 0:
                    send = _remote(ins[i].at[layer, me, mine, :], outs[i].at[layer, me, mine, :], *sems, (i, r), (px, py, c))
                    land = outs[i].at[layer, k, mine, :]
                    recv = _remote(land, land, *sems, (i, r), (px, py, c))
                else:
                    send = _remote(ins[i].at[layer, k, mine, :], outs[i].at[layer, k, mine, :], *sems, (i, r), (x, y, 1 - c))
                    land = outs[i].at[layer, k, other, :]
                    recv = _remote(land, land, *sems, (i, r), (x, y, 1 - c))
                pairs.append((send, recv))
        return pairs

    return _Comm(bufs, [jax.ShapeDtypeStruct(b.shape, b.dtype) for b in bufs], {i: i for i in range(n)},
                 _pair_sems(n, 3), copies)


def _reduce_sibling_comm(gs):
    n = len(gs)

    def copies(ins, outs, sems):
        x, y, c = _mesh_pos()
        pairs = []
        for i in range(n):
            h = gs[i].shape[1] // 2
            cp = _remote(ins[i].at[:, pl.ds((1 - c) * h, h), :], outs[i], *sems, i, (x, y, 1 - c))
            pairs.append((cp, cp))
        return pairs

    return _Comm(gs, [jax.ShapeDtypeStruct((g.shape[0], g.shape[1] // 2, g.shape[2]), g.dtype) for g in gs], {},
                 _pair_sems(n), copies)


def _reduce_chips_comm(ps):
    n = len(ps)

    def copies(ins, outs, sems):
        x, y, c = _mesh_pos()
        pairs = []
        for i in range(n):
            for r, (px, py, k) in enumerate(_peers(x, y)):
                cp = _remote(ins[i].at[k], outs[i].at[r], *sems, (i, r), (px, py, c))
                pairs.append((cp, cp))
        return pairs

    return _Comm(ps, [jax.ShapeDtypeStruct((3,) + p.shape[1:], p.dtype) for p in ps], {}, _pair_sems(n, 3), copies)


def _share_halves(bufs, *, name):
    n = len(bufs)

    def body(*refs):
        ins, outs = refs[:n], refs[n:2 * n]
        send_sems, recv_sems = refs[2 * n:]
        x, y, c = _mesh_pos()
        cps = []
        for i in range(n):
            h = bufs[i].shape[1] // 2
            mine = pl.ds(c * h, h)
            cp = _remote(ins[i].at[:, mine, :], outs[i].at[:, mine, :], send_sems, recv_sems, i, (x, y, 1 - c))
            cp.start()
            cps.append(cp)
        for cp in cps:
            cp.wait()

    return pl.pallas_call(
        body, name=name, in_specs=[ANY] * n, out_specs=[ANY] * n,
        out_shape=[jax.ShapeDtypeStruct(b.shape, b.dtype) for b in bufs],
        input_output_aliases={i: i for i in range(n)},
        scratch_shapes=[pltpu.SemaphoreType.DMA((n,)), pltpu.SemaphoreType.DMA((n,))])(*bufs)


def _add_own_half(full, recv, pos, *, name, rows=256):
    k4, h, n = recv.shape
    tr = _tile(h, rows, 16)
    nblk = h // tr

    def body(pos_ref, a_ref, b_ref, o_ref):
        o_ref[...] = (a_ref[...] + b_ref[...]).astype(BF16)

    grid_spec = pltpu.PrefetchScalarGridSpec(
        num_scalar_prefetch=1, grid=(k4, nblk),
        in_specs=[pl.BlockSpec((None, tr, n), lambda k, i, pos_ref: (k, pos_ref[1] * nblk + i, 0)),
                  pl.BlockSpec((None, tr, n), lambda k, i, pos_ref: (k, i, 0))],
        out_specs=pl.BlockSpec((None, tr, n), lambda k, i, pos_ref: (k, i, 0)))
    return pl.pallas_call(body, name=name, grid_spec=grid_spec, out_shape=jax.ShapeDtypeStruct(recv.shape, BF16),
                          compiler_params=_params(("parallel", "parallel")))(pos, full, recv)


def _sum_into(own, others, buf, pos, layer, *, name, rows=256):
    _, h, n = own.shape
    tr = _tile(h, rows, 16)
    nblk = h // tr

    def body(pos_ref, own_ref, oth_ref, _, o_ref):
        acc = own_ref[...].astype(F32)
        for r in range(3):
            acc = acc + oth_ref[r].astype(F32)
        o_ref[...] = acc

    grid_spec = pltpu.PrefetchScalarGridSpec(
        num_scalar_prefetch=1, grid=(nblk,),
        in_specs=[pl.BlockSpec((None, tr, n), lambda i, pos_ref: (pos_ref[0], i, 0)),
                  pl.BlockSpec((3, tr, n), lambda i, pos_ref: (0, i, 0)), ANY],
        out_specs=pl.BlockSpec((None, tr, n), lambda i, pos_ref: (layer, pos_ref[1] * nblk + i, 0)))
    return pl.pallas_call(body, name=name, grid_spec=grid_spec, out_shape=jax.ShapeDtypeStruct(buf.shape, F32),
                          input_output_aliases={3: 0}, compiler_params=_params(("parallel",)))(pos, own, others, buf)


def _pair_comm(buf):
    def copies(ins, outs, sems):
        x, y, c = _mesh_pos()
        land = outs[0].at[1 - c]
        return [(_remote(ins[0].at[c], outs[0].at[c], *sems, 0, (x, y, 1 - c)),
                 _remote(land, land, *sems, 0, (x, y, 1 - c)))]

    return _Comm([buf], [jax.ShapeDtypeStruct(buf.shape, buf.dtype)], {0: 0}, _pair_sems(1), copies)


def _quad_comm(buf):
    def copies(ins, outs, sems):
        x, y, c = _mesh_pos()
        me = 2 * x + y
        pairs = []
        for r, (px, py, k) in enumerate(_peers(x, y)):
            land = outs[0].at[k]
            pairs.append((_remote(ins[0].at[me], outs[0].at[me], *sems, r, (px, py, c)),
                          _remote(land, land, *sems, r, (px, py, c))))
        return pairs

    return _Comm([buf], [jax.ShapeDtypeStruct(buf.shape, buf.dtype)], {0: 0}, _pair_sems(3), copies)


def _chip_bcast(buf, *, name):
    def body(src_ref, out_ref, send_sems, recv_sems, local_sem):
        x, y, c = _mesh_pos()
        me = 2 * x + y
        local = pltpu.make_async_copy(src_ref, out_ref.at[me], local_sem)
        local.start()
        sends = []
        for r, (px, py, _) in enumerate(_peers(x, y)):
            cp = _remote(src_ref, out_ref.at[me], send_sems, recv_sems, r, (px, py, c))
            cp.start()
            sends.append(cp)
        for r, (px, py, k) in enumerate(_peers(x, y)):
            _remote(src_ref, out_ref.at[k], send_sems, recv_sems, r, (px, py, c)).wait_recv()
        for cp in sends:
            cp.wait_send()
        local.wait()

    return pl.pallas_call(
        body, name=name, in_specs=[ANY], out_specs=ANY, out_shape=jax.ShapeDtypeStruct((4,) + buf.shape, buf.dtype),
        scratch_shapes=[pltpu.SemaphoreType.DMA((3,)), pltpu.SemaphoreType.DMA((3,)), pltpu.SemaphoreType.DMA])(buf)


def _sum_slots(buf, *, name, rows=384):
    r, n = buf.shape[-2:]
    k = int(np.prod(buf.shape[:-2]))
    tr = _tile(r, rows, SUBLANES)

    def body(b_ref, o_ref):
        acc = b_ref[0]
        for s in range(1, k):
            acc = acc + b_ref[s]
        o_ref[...] = acc

    return pl.pallas_call(
        body, name=name, grid=(r // tr,), in_specs=[pl.BlockSpec((k, tr, n), lambda i: (0, i, 0))],
        out_specs=pl.BlockSpec((tr, n), lambda i: (i, 0)), out_shape=jax.ShapeDtypeStruct((r, n), F32),
        compiler_params=_params(("parallel",)))(buf.reshape((k, r, n)))


ROW = 1024
BIG = (("w_in", 2), ("w_out", 1), ("w_cq", 1), ("w_ck", 1), ("w_cv", 1), ("w_co", 2), ("w_up", 2), ("w_down", 1))
CONV = ("lru_conv_w", "ffn_conv_w")
REPLICATED = ("norm_mix_g", "b_forget", "lru_conv_b", "lru_w_a", "lru_b_a", "lru_w_x", "lru_b_x", "lru_lambda",
              "norm_cross_g", "norm_mem_g", "norm_ffn_g", "ffn_conv_b", "rel_bias", "final_norm_g")
WEIGHTS = ('norm_mix_g', 'w_in', 'b_forget', 'lru_conv_w', 'lru_conv_b', 'lru_w_a', 'lru_b_a', 'lru_w_x', 'lru_b_x',
           'lru_lambda', 'w_out', 'norm_cross_g', 'norm_mem_g', 'w_cq', 'w_ck', 'w_cv', 'w_co', 'norm_ffn_g', 'w_up',
           'ffn_conv_w', 'ffn_conv_b', 'w_down', 'rel_bias', 'final_norm_g')
INPUTS = ("x", "mem") + WEIGHTS + ("loss_target",) + tuple("m_" + n for n in WEIGHTS) + tuple("v_" + n for n in WEIGHTS)


def _round_up(n, m):
    return -(-n // m) * m


class _Packing:
    def __init__(self, entries):
        self.entries, self.off = entries, {}
        o = 0
        for name, shape in entries:
            self.off[name] = o
            o += _round_up(int(np.prod(shape)), ROW)
        self.used = o
        self.rows = _round_up(o // ROW, SUBLANES)

    def pack(self, arrays):
        parts = []
        for name, shape in self.entries:
            n = int(np.prod(shape))
            parts.append(jnp.pad(arrays[name].reshape(n), (0, _round_up(n, ROW) - n)))
        tail = self.rows * ROW - self.used
        if tail:
            parts.append(jnp.zeros((tail,), F32))
        return jnp.concatenate(parts).reshape(self.rows, ROW)

    def unpack(self, flat, lead=()):
        out = {}
        for name, shape in self.entries:
            n = int(np.prod(shape))
            r0, nr = self.off[name] // ROW, _round_up(n, ROW) // ROW
            rows = lax.slice_in_dim(flat, r0, r0 + nr, axis=len(lead)).reshape(lead + (nr * ROW,))
            out[name] = lax.slice_in_dim(rows, 0, n, axis=len(lead)).reshape(lead + tuple(shape))
        return out


def _to_shards(g, axis):
    r, c = g.shape
    if axis == 1:
        return g.reshape(4, r // 4, c)
    return g.reshape(r, 4, c // 4).transpose(1, 0, 2)


def _from_shards(s, axis):
    _, r, c = s.shape
    if axis == 1:
        return s.reshape(4 * r, c)
    return s.transpose(1, 0, 2).reshape(r, 4 * c)


def _proj_blocks():
    blocks = []
    for mixer in (MIX_SB, MIX_FOX, MIX_DIL):
        for p in range(2):
            blocks += [ORIG_COL[mixer] + part * 2 * LANES + p * LANES for part in range(3)]
    for hf in range(2):
        blocks += [ORIG_LRU_X + hf * LANES, ORIG_LRU_G + hf * LANES]
    return blocks


def _pad_w_in(w):
    parts = [w[..., s:s + LANES] for s in _proj_blocks()]
    parts += [w[..., 1536:1540], jnp.zeros(w.shape[:-1] + (PROJ_W - COL_F - N_HEADS,), w.dtype)]
    return jnp.concatenate(parts, axis=-1)


def _unpad_w_in(wp):
    blocks = _proj_blocks()
    order = sorted(range(len(blocks)), key=lambda i: blocks[i])
    parts = []
    for i in order:
        if blocks[i] == ORIG_COL[MIX_DIL]:
            parts.append(wp[..., COL_F:COL_F + N_HEADS])
        parts.append(wp[..., i * LANES:(i + 1) * LANES])
    return jnp.concatenate(parts, axis=-1)


def _block_diag(w):
    z = jnp.zeros((HEAD_DIM, HEAD_DIM), w.dtype)
    half = lambda a, b: jnp.concatenate([jnp.concatenate([a, z], 1), jnp.concatenate([z, b], 1)], 0)
    return jnp.stack([half(w[0], w[1]), half(w[2], w[3])])


def _block_diag_grad(d):
    return jnp.stack([d[0, :HEAD_DIM, :HEAD_DIM], d[0, HEAD_DIM:, HEAD_DIM:],
                      d[1, :HEAD_DIM, :HEAD_DIM], d[1, HEAD_DIM:, HEAD_DIM:]])


def _fox_layouts(cum, nb, seq):
    tk = min(ATT_TILE, seq)
    col = cum.reshape(nb, 2, 2, seq).transpose(0, 1, 3, 2)
    row = cum.reshape(nb, 2, 2, seq // tk, tk).transpose(0, 1, 3, 2, 4)
    return col, row


def _layer_params(w, l, nb):
    lru_vec = jnp.concatenate([w["lru_conv_w"][l], w["lru_conv_b"][l][None], w["lru_b_a"][l][None],
                               w["lru_b_x"][l][None], w["lru_lambda"][l][None]], axis=0)
    ffn_cw = jnp.concatenate([w["ffn_conv_w"][l], w["ffn_conv_b"][l][None],
                              jnp.zeros((SUBLANES - 4, 2 * D_FF), F32)], axis=0)
    return dict(
        w_in=w["w_in_padded"][l], lru_vec=lru_vec,
        wa=_block_diag(w["lru_w_a"][l]).astype(BF16), wx=_block_diag(w["lru_w_x"][l]).astype(BF16),
        ffn_cw=ffn_cw, b_rows=jnp.tile(w["b_forget"][l], nb).reshape(nb * N_HEADS, 1))


NORM_ROWS = 512


def _merge_comms(comms):
    if len(comms) == 1:
        return comms[0]
    operands, out_shape, aliases, sems, spans = [], [], {}, [], []
    for cm in comms:
        aliases.update({len(operands) + i: len(out_shape) + j for i, j in cm.aliases.items()})
        spans.append((len(operands), len(out_shape), len(sems)))
        operands += list(cm.operands)
        out_shape += list(cm.out_shape)
        sems += list(cm.sems)

    def copies(ins, outs, sm):
        pairs = []
        for cm, (i0, o0, s0) in zip(comms, spans):
            pairs += cm.copies(ins[i0:i0 + len(cm.operands)], outs[o0:o0 + len(cm.out_shape)], sm[s0:s0 + len(cm.sems)])
        return pairs

    return _Comm(operands, out_shape, aliases, sems, copies)


GATHER_FIRST = ("w_in",)
GATHER_MID = ("w_out", "w_cq", "w_ck", "w_cv", "w_co")
GATHER_LAST = ("w_up", "w_down")


class _WeightGather:
    def __init__(self, slots, w, depth):
        self.slots, self.w, self.depth = slots, w, depth

    def plan(self, l, key):
        nxt = l + 1 if l + 1 < self.depth else None
        early = GATHER_FIRST + GATHER_MID
        if l == 0:
            table = {"proj": [(GATHER_MID, 0, 0)],
                     "sb_fwd": [(GATHER_MID, 0, 1), (GATHER_LAST, 0, 0)],
                     "fox_fwd": [(GATHER_LAST, 0, 1)] + ([(early, nxt, 0)] if nxt else []),
                     "dil_fwd": [(early, nxt, 1), (GATHER_LAST, nxt, 0)] if nxt else [],
                     "out": [(GATHER_LAST, nxt, 1)] if nxt else []}
        else:
            everything = early + GATHER_LAST
            table = {"sb_fwd": [(everything, nxt, 0)], "fox_fwd": [(everything, nxt, 1)]} if nxt else {}
        return table.get(key, [])

    def comm(self, l, key):
        entries = self.plan(l, key)
        if not entries:
            return None
        return _merge_comms([_gather_comm([self.slots[n] for n in names], layer, stage)
                             for names, layer, stage in entries])

    def done(self, l, key, landed):
        landed = list(landed)
        for names, layer, stage in self.plan(l, key):
            for n in names:
                self.slots[n] = landed.pop(0)
            if stage == 1:
                self.take(names, layer)

    def take(self, names, layer):
        for n, axis in BIG:
            if n in names:
                self.w[n][layer] = _from_shards(self.slots[n][layer], axis)
        if "w_in" in names:
            self.w["w_in_padded"][layer] = _pad_w_in(self.w["w_in"][layer])

    def first(self):
        for stage in (0, 1):
            got = _run_comm(_gather_comm([self.slots[n] for n in GATHER_FIRST], 0, stage), name=f"gather_first_{stage}")
            self.slots.update(zip(GATHER_FIRST, got))
        self.take(GATHER_FIRST, 0)


def _layer_fwd(x, h, mem, w, lp, l, next_g, bias, nb, gather):
    t, d = x.shape
    seq = t // nb
    tag = f"l{l}"
    sv = dict(x0=x)

    def carrying(key, fn):
        comm = gather.comm(l, key)
        res = fn(comm)
        if comm is not None:
            res, landed = res
            gather.done(l, key, landed)
        return res

    proj = carrying("proj", lambda cm: _mm(h, lp["w_in"], name=tag + "_proj", comm=cm))
    mixed, ltot = carrying("sb_fwd", lambda cm: _sb_attn_fwd(proj, lax.empty((t, d), F32), nb=nb,
                                                             name=tag + "_sb_fwd", comm=cm))
    f_rows = proj[:, COL_F:COL_F + N_HEADS].reshape(nb, seq, N_HEADS).transpose(0, 2, 1).reshape(nb * N_HEADS, seq)
    cum_col, cum_row = _fox_layouts(_fox_gate_fwd(f_rows, lp["b_rows"]), nb, seq)
    mixed, lse_fox = carrying("fox_fwd", lambda cm: _softmax_attn_fwd(
        proj, nb=nb, mode="fox", mixer=MIX_FOX, out_buf=mixed, extra=(cum_col, cum_row), name=tag + "_fox_fwd", comm=cm))
    mixed, lse_dil = carrying("dil_fwd", lambda cm: _softmax_attn_fwd(
        proj, nb=nb, mode="dil", mixer=MIX_DIL, out_buf=mixed, extra=(bias,), name=tag + "_dil_fwd", comm=cm))
    mixed = _lru_fwd(proj, lp["lru_vec"], lp["wa"], lp["wx"], mixed, nb=nb, name=tag + "_lru_fwd")
    x1, hq = carrying("out", lambda cm: _mm(mixed, w["w_out"][l], res=x, norm_g=w["norm_cross_g"][l], ti=NORM_ROWS,
                                            name=tag + "_out", comm=cm))
    memn = _rmsnorm(mem, w["norm_mem_g"][l], name=tag + "_norm_mem")
    q = _mm(hq, w["w_cq"][l], name=tag + "_cq")
    k = _mm(memn, w["w_ck"][l], name=tag + "_ck")
    v = _mm(memn, w["w_cv"][l], name=tag + "_cv")
    oc, lse_c = _softmax_attn_fwd((q, k, v), nb=nb, mode="cross", name=tag + "_cross_fwd")
    x2, hn = _mm(oc, w["w_co"][l], res=x1, norm_g=w["norm_ffn_g"][l], ti=NORM_ROWS, name=tag + "_co")
    hf = _mm(hn, w["w_up"][l], name=tag + "_up")
    act = _ffn_act(hf, lp["ffn_cw"], seq=seq, name=tag + "_ffn_act")
    if next_g is None:
        x3, h_next = _mm(act, w["w_down"][l], res=x2, name=tag + "_down"), None
    else:
        x3, h_next = _mm(act, w["w_down"][l], res=x2, norm_g=next_g, ti=NORM_ROWS, name=tag + "_down")
    sv.update(h=h, proj=proj, ltot=ltot, f_rows=f_rows, cum_col=cum_col, cum_row=cum_row, lse_fox=lse_fox,
              lse_dil=lse_dil, mixed=mixed, x1=x1, hq=hq, memn=memn, q=q, k=k, v=v, oc=oc, lse_c=lse_c, x2=x2,
              hn=hn, hf=hf, act=act)
    return x3, h_next, sv


class _PendingReduce:
    def __init__(self, full, pos, layer):
        self.names, self.full, self.pos, self.layer = list(full), list(full.values()), pos, layer

    def sibling_comm(self):
        return _reduce_sibling_comm(self.full)

    def add(self, from_sibling):
        self.partial = [_add_own_half(f, r, self.pos, name=f"l{self.layer}_reduce_add_{n}")
                        for f, r, n in zip(self.full, from_sibling, self.names)]

    def chips_comm(self):
        return _reduce_chips_comm(self.partial)

    def finish(self, others, g_shard):
        g_shard = dict(g_shard)
        for p, o, n in zip(self.partial, others, self.names):
            g_shard[n] = _sum_into(p, o, g_shard[n], self.pos, self.layer, name=f"l{self.layer}_reduce_sum_{n}")
        return g_shard


def _layer_bwd(dx3, mem, sv, w, lp, l, bias, nb, pos, pending=None, g_shard=None, reduce_early=False):
    t = dx3.shape[0]
    seq = t // nb
    tag = f"l{l}"
    g = {}
    down_rows = _tile(sv["act"].shape[1], 1408)
    if pending is None:
        g["w_down"] = _mm(sv["act"], dx3, ta=True, ti=down_rows, name=tag + "_dw_down")
    else:
        g["w_down"], from_sibling = _mm(sv["act"], dx3, ta=True, ti=down_rows, comm=pending.sibling_comm(),
                                        name=tag + "_dw_down")
        pending.add(from_sibling)
    dact = _mm(dx3, w["w_down"][l], tb=True, name=tag + "_dact")
    if pending is None:
        dhf, dcu, dcg = _ffn_bwd(sv["hf"], lp["ffn_cw"], dact, seq=seq, name=tag + "_ffn_bwd")
    else:
        (dhf, dcu, dcg), others = _ffn_bwd(sv["hf"], lp["ffn_cw"], dact, seq=seq, name=tag + "_ffn_bwd",
                                           comm=pending.chips_comm())
        g_shard = pending.finish(others, g_shard)
    dcw = jnp.concatenate([dcu, dcg], axis=1)
    g["ffn_conv_w"], g["ffn_conv_b"] = dcw[:3], dcw[3]
    g["w_up"] = _mm(sv["hn"], dhf, ta=True, halves="b", col_shards=4, name=tag + "_dw_up")
    early = _PendingReduce(_big_grad_shards(g, EARLY), pos, l) if reduce_early else None
    res = _mm(dhf, w["w_up"][l], tb=True, halves="a", norm_bwd=(sv["x2"], w["norm_ffn_g"][l], dx3), ti=NORM_ROWS,
              comm=early.sibling_comm() if early else None, name=tag + "_dhn")
    if early:
        res, from_sibling = res
        early.add(from_sibling)
    dx2, dg = res
    g["norm_ffn_g"] = dg.reshape(-1)
    g["w_co"] = _mm(sv["oc"], dx2, ta=True, col_shards=4, name=tag + "_dw_co")
    doc = _mm(dx2, w["w_co"][l], tb=True, name=tag + "_doc")
    dq, dk, dv = _softmax_attn_bwd((sv["q"], sv["k"], sv["v"]), sv["oc"], sv["lse_c"], doc, nb=nb, mode="cross",
                                   name=tag + "_cross_bwd")
    g["w_cq"] = _mm(sv["hq"], dq, ta=True, name=tag + "_dw_cq")
    g["w_ck"] = _mm(sv["memn"], dk, ta=True, name=tag + "_dw_ck")
    g["w_cv"] = _mm(sv["memn"], dv, ta=True, name=tag + "_dw_cv")
    dx1, dg = _mm(dq, w["w_cq"][l], tb=True, norm_bwd=(sv["x1"], w["norm_cross_g"][l], dx2), ti=NORM_ROWS,
                  name=tag + "_dhq")
    g["norm_cross_g"] = dg.reshape(-1)
    dmemn = _mm(dv, w["w_cv"][l], tb=True, res=_mm(dk, w["w_ck"][l], tb=True, name=tag + "_dmem_k"),
                name=tag + "_dmem_v")
    _, g["norm_mem_g"] = _rmsnorm_bwd(dmemn, mem, w["norm_mem_g"][l], None, name=tag + "_norm_mem_bwd")
    mixed, proj = sv["mixed"], sv["proj"]
    g["w_out"] = _mm(mixed, dx1, ta=True, name=tag + "_dw_out")
    mid = _PendingReduce(_big_grad_shards(g, GATHER_MID), pos, l) if reduce_early else None
    dmixed = _mm(dx1, w["w_out"][l], tb=True, name=tag + "_dmixed", comm=mid.sibling_comm() if mid else None)
    if mid:
        dmixed, from_sibling = dmixed
        mid.add(from_sibling)
    dproj = _sb_attn_bwd(proj, sv["ltot"], dmixed, lax.empty((t, PROJ_W), F32), nb=nb, name=tag + "_sb_bwd",
                         comm=early.chips_comm() if early else None)
    if early:
        dproj, others = dproj
        g_shard = early.finish(others, g_shard)
    res = _softmax_attn_bwd(
        proj, mixed, sv["lse_fox"], dmixed, nb=nb, mode="fox", mixer=MIX_FOX, dbuf=dproj,
        extra=(sv["cum_col"], sv["cum_row"]), name=tag + "_fox_bwd", comm=mid.chips_comm() if mid else None)
    if mid:
        res, others = res
        g_shard = mid.finish(others, g_shard)
    dproj, dcum_k, dcum_q = res
    dcum = (dcum_k.transpose(0, 1, 3, 2, 4).reshape(nb * N_HEADS, seq)
            + dcum_q.transpose(0, 1, 3, 2).reshape(nb * N_HEADS, seq))
    df_rows, db = _fox_gate_bwd(dcum, sv["f_rows"], lp["b_rows"])
    g["b_forget"] = db[:N_HEADS, 0]
    df = df_rows.reshape(nb, N_HEADS, seq).transpose(0, 2, 1).reshape(t, N_HEADS)
    dproj, dbias = _softmax_attn_bwd(proj, mixed, sv["lse_dil"], dmixed, nb=nb, mode="dil", mixer=MIX_DIL,
                                     dbuf=dproj, extra=(bias,), name=tag + "_dil_bwd")
    dproj, dvec, dwa, dwx = _lru_bwd(proj, lp["lru_vec"], lp["wa"], lp["wx"], dmixed, dproj, nb=nb,
                                     name=tag + "_lru_bwd")
    g["lru_conv_w"], g["lru_conv_b"], g["lru_b_a"], g["lru_b_x"], g["lru_lambda"] = (
        dvec[0:4], dvec[4], dvec[5], dvec[6], dvec[7])
    g["lru_w_a"], g["lru_w_x"] = _block_diag_grad(dwa), _block_diag_grad(dwx)
    dproj = lax.dynamic_update_slice(dproj, jnp.pad(df, ((0, 0), (0, PROJ_W - COL_F - N_HEADS))), (0, COL_F))
    g["w_in_padded"] = _mm(sv["h"], dproj, ta=True, name=tag + "_dw_in")
    dx0, dg = _mm(dproj, lp["w_in"], tb=True, norm_bwd=(sv["x0"], w["norm_mix_g"][l], dx1), ti=NORM_ROWS,
                  name=tag + "_dh")
    g["norm_mix_g"] = dg.reshape(-1)
    return dx0, g, dbias, g_shard


def _big_grad_shards(g, names):
    out = {}
    for n, axis in BIG:
        if n not in names:
            continue
        if n in ("w_up", "w_co"):
            out[n] = g[n]
        else:
            out[n] = _to_shards(_unpad_w_in(g["w_in_padded"]) if n == "w_in" else g[n], axis)
    return out


EARLY = ("w_down", "w_up")


def kernel(*args):
    a = dict(zip(INPUTS, args, strict=True))
    nb, seq, d = a["x"].shape
    depth = a["norm_mix_g"].shape[0]
    x = a["x"].reshape(nb * seq, d)
    mem = a["mem"].reshape(nb * a["mem"].shape[1], d)
    target = a["loss_target"].reshape(nb * seq, d)
    cx, cy, c = _mesh_pos()
    chip = 2 * cx + cy
    pos = jnp.stack([chip, c]).astype(jnp.int32)

    slots = {}
    for n, _ in BIG:
        own = a[n].astype(BF16)[:, None]
        slots[n] = lax.dynamic_update_slice(lax.empty((depth, 4) + own.shape[2:], BF16), own, (0, chip, 0, 0))
    w = {n: a[n] for n in REPLICATED}
    w.update({n: {} for n, _ in BIG}, w_in_padded={})
    gather = _WeightGather(slots, w, depth)
    gather.first()
    cpk = _Packing([(n, a[n].shape) for n in CONV])
    conv = cpk.unpack(_chip_bcast(cpk.pack({n: a[n] for n in CONV}), name="gather_conv"), lead=(4,))
    for n in CONV:
        w[n] = jnp.moveaxis(conv[n], 0, 2).reshape(a[n].shape[:2] + (4 * a[n].shape[2],))

    bias = _dil_bias(w["rel_bias"], seq)
    lps, saved = [], []
    h = _rmsnorm(x, w["norm_mix_g"][0], name="l0_norm_mix")
    for l in range(depth):
        lps.append(_layer_params(w, l, nb))
        x, h, sv = _layer_fwd(x, h, mem, w, lps[l], l, w["norm_mix_g"][l + 1] if l + 1 < depth else None, bias, nb,
                              gather)
        saved.append(sv)
    loss, dx, dg_final = _loss_head(x, w["final_norm_g"], target)
    small_g = [None] * depth
    dbias, pending = None, None
    g_shard = {n: lax.empty(a[n].shape, F32) for n, _ in BIG}
    for l in reversed(range(depth)):
        bottom = l == 0
        dx, g, db, g_shard = _layer_bwd(dx, mem, saved[l], w, lps[l], l, bias, nb, pos, pending=pending,
                                        g_shard=g_shard, reduce_early=bottom)
        dbias = db if dbias is None else dbias + db
        small_g[l] = g
        left = [n for n, _ in BIG if not (bottom and n in EARLY + GATHER_MID)]
        pending = _PendingReduce(_big_grad_shards(g, left), pos, l)

    grads = {n: jnp.stack([small_g[l][n] for l in range(depth)]) for n in REPLICATED + CONV
             if n not in ("rel_bias", "final_norm_g")}
    grads["rel_bias"] = _dil_bias_bwd(dbias, seq)
    grads["final_norm_g"] = dg_final
    grads["loss"] = loss.reshape(1)
    spk = _Packing([(n, grads[n].shape) for n in REPLICATED + CONV + ("loss",)])
    s_flat = spk.pack(grads)
    pair = lax.dynamic_update_slice(lax.empty((2,) + s_flat.shape, F32), s_flat[None], (c, 0, 0))
    *from_sibling, pair = _run_comm(_merge_comms([pending.sibling_comm(), _pair_comm(pair)]), name="tail_sibling")
    pending.add(from_sibling)
    quad = lax.dynamic_update_slice(lax.empty((4,) + pair.shape, F32), pair[None], (chip, 0, 0, 0))
    *others, quad = _run_comm(_merge_comms([pending.chips_comm(), _quad_comm(quad)]), name="tail_chips")
    g_shard = pending.finish(others, g_shard)
    names = [n for n, _ in BIG]
    g_shard = dict(zip(names, _share_halves([g_shard[n] for n in names], name="reduce_share")))
    out = {}
    for n in names:
        delta, new_m, new_v = _adamw(a[n], g_shard[n], a["m_" + n], a["v_" + n], name="adamw_" + n)
        out[n] = (g_shard[n], delta, new_m, new_v)
    total = spk.unpack(_sum_slots(quad, name="small_sum"))
    for n in CONV:
        width = a[n].shape[2]
        total[n] = lax.dynamic_slice_in_dim(total[n], chip * width, width, axis=2)
    apk = _Packing([(n, a[n].shape) for n in REPLICATED + CONV])
    s_out = _adamw(*[apk.pack(src)[None] for src in (
        {n: a[n] for n in REPLICATED + CONV}, total, {n: a["m_" + n] for n in REPLICATED + CONV},
        {n: a["v_" + n] for n in REPLICATED + CONV})], name="adamw_small")
    s_delta, s_m, s_v = [apk.unpack(o[0]) for o in s_out]
    for n in REPLICATED + CONV:
        out[n] = (total[n], s_delta[n], s_m[n], s_v[n])

    return (total["loss"].reshape(()), dx.reshape(nb, seq, d), *[out[n][0] for n in WEIGHTS],
            *[out[n][1] for n in WEIGHTS], *[out[n][2] for n in WEIGHTS], *[out[n][3] for n in WEIGHTS])
```

```python
import math

import numpy as np
import jax
import jax.numpy as jnp
from jax import lax
from jax.experimental import pallas as pl
from jax.experimental.pallas import tpu as pltpu

F32 = jnp.float32
BF16 = jnp.bfloat16

HEAD_DIM = 64
N_HEADS = 4
N_IN = 2820
D_FF = 2816
LRU_C = 8.0
EPS = 1e-6
NUM_BUCKETS = 32
MAX_DISTANCE = 2048
DILATED_PATTERNS = ((128, 1), (512, 4), (2048, 16))
ADAM_LR, ADAM_B1, ADAM_B2, ADAM_EPS, ADAM_WD, ADAM_STEP = 0.001, 0.9, 0.999, 1e-08, 0.01, 10

LANES = 128
SUBLANES = 8
VMEM_LIMIT = 48 * 1024 * 1024

PROJ_W = 3072
PAIR_W = 3 * LANES
LRU_W = 2 * LANES
COL_LRU = 6 * PAIR_W
COL_F = COL_LRU + 2 * LRU_W
MIX_SB, MIX_FOX, MIX_DIL, MIX_LRU = 0, 1, 2, 3
ORIG_COL = {MIX_SB: 0, MIX_FOX: 768, MIX_DIL: 1540}
ORIG_LRU_X, ORIG_LRU_G = 2308, 2564

ATT_TILE = 256
MASKED = -1e30
SCALE = HEAD_DIM ** -0.5

NT_DIMS = (((1,), (1,)), ((), ()))
TN_DIMS = (((0,), (0,)), ((), ()))

MESH = pl.DeviceIdType.MESH
ANY = pl.BlockSpec(memory_space=pl.ANY)


def _params(sem):
    return pltpu.CompilerParams(dimension_semantics=sem, vmem_limit_bytes=VMEM_LIMIT)


def _tile(n, target, unit=LANES):
    if n <= target:
        return n
    t = (target // unit) * unit
    while t > unit and n % t:
        t -= unit
    assert n % t == 0, (n, target, unit)
    return t


def _mm(a, b, *, ta=False, tb=False, res=None, col_shards=1, halves=None, norm_g=None, norm_bwd=None, comm=None,
        name, ti=1024, tj=1408, tc=1408):
    if halves == "a":
        m, kc = a.shape[1], 2 * a.shape[2]
    else:
        m, kc = (a.shape[1], a.shape[0]) if ta else a.shape
    if halves == "b":
        n = 2 * b.shape[2]
        assert b.shape[1] == kc
    else:
        n = b.shape[0] if tb else b.shape[1]
        assert (b.shape[1] if tb else b.shape[0]) == kc
    assert n % col_shards == 0
    n_blk = n // (2 if halves == "b" else col_shards)
    k_blk = kc // 2 if halves == "a" else kc
    ti, tj, tc = (_tile(m, ti, LANES if ta else SUBLANES), _tile(n_blk, tj),
                  _tile(k_blk, tc, SUBLANES if ta and tb else LANES))
    per_shard, per_half_j, per_half_k = n // col_shards // tj, n_blk // tj, k_blk // tc
    nk = kc // tc
    dims = (((0 if ta else 1,), (1 if tb else 0,)), ((), ()))
    rows_whole = norm_g is not None or norm_bwd is not None
    assert not rows_whole or (tj == n and col_shards == 1)
    n_extra = (res is not None) + (norm_g is not None) + (3 if norm_bwd is not None else 0)
    n_out = 2 if rows_whole else 1

    def finish(val, ex, outs):
        if res is not None:
            val = ex[0][...] + val
        if norm_g is not None:
            outs[0][...] = val
            outs[1][...] = (_xhat(val) * ex[-1][...]).astype(BF16)
        elif norm_bwd is not None:
            x_ref, g_ref, r_ref = ex[-3:]
            dx, dgr = _norm_bwd_rows(val, x_ref[...], g_ref[...])
            outs[0][...] = r_ref[...] + dx

            @pl.when(pl.program_id(0) == 0)
            def _():
                outs[1][...] = jnp.zeros_like(outs[1])

            outs[1][...] += jnp.sum(dgr, axis=0, keepdims=True)
        else:
            outs[0][...] = val

    def body(*refs):
        a_ref, b_ref = refs[:2]
        ex = refs[2:2 + n_extra]
        outs = refs[2 + n_extra:2 + n_extra + n_out]
        part = lax.dot_general(a_ref[...].astype(BF16), b_ref[...].astype(BF16), dims, preferred_element_type=F32)
        if nk == 1:
            finish(part, ex, outs)
            return
        acc_ref = refs[-1]
        k = pl.program_id(2)

        @pl.when(k == 0)
        def _():
            acc_ref[...] = part

        @pl.when(k > 0)
        def _():
            acc_ref[...] += part

        @pl.when(k == nk - 1)
        def _():
            finish(acc_ref[...], ex, outs)

    if halves == "a":
        a_spec = pl.BlockSpec((None, ti, tc), lambda i, j, k: (k // per_half_k, i, k % per_half_k))
    elif ta:
        a_spec = pl.BlockSpec((tc, ti), lambda i, j, k: (k, i))
    else:
        a_spec = pl.BlockSpec((ti, tc), lambda i, j, k: (i, k))
    if halves == "b":
        b_spec = pl.BlockSpec((None, tc, tj), lambda i, j, k: (j // per_half_j, k, j % per_half_j))
    elif tb:
        b_spec = pl.BlockSpec((tj, tc), lambda i, j, k: (j, k))
    else:
        b_spec = pl.BlockSpec((tc, tj), lambda i, j, k: (k, j))
    o_spec = pl.BlockSpec((ti, tj), lambda i, j, k: (i, j))
    vec = pl.BlockSpec((1, tj), lambda i, j, k: (0, 0))
    in_specs, args = [a_spec, b_spec], [a, b]
    out_specs, out_shape = [o_spec], [jax.ShapeDtypeStruct((m, n), F32)]
    if res is not None:
        in_specs.append(o_spec)
        args.append(res)
    if norm_g is not None:
        in_specs.append(vec)
        args.append(norm_g.reshape(1, n))
        out_specs.append(o_spec)
        out_shape.append(jax.ShapeDtypeStruct((m, n), BF16))
    if norm_bwd is not None:
        x, g, dres = norm_bwd
        in_specs += [o_spec, vec, o_spec]
        args += [x, g.reshape(1, n), dres]
        out_specs.append(vec)
        out_shape.append(jax.ShapeDtypeStruct((1, n), F32))
    if col_shards > 1:
        assert n_extra == 0
        out_specs = [pl.BlockSpec((None, ti, tj), lambda i, j, k: (j // per_shard, i, j % per_shard))]
        out_shape = [jax.ShapeDtypeStruct((col_shards, m, n // col_shards), F32)]
    sem = ("arbitrary",) * 3 if norm_bwd is not None else ("parallel", "parallel", "arbitrary")
    out, carried = _pallas(body, name=name, grid=(m // ti, n // tj, nk), in_specs=in_specs, out_specs=out_specs,
                           out_shape=out_shape, args=args, scratch=[] if nk == 1 else [pltpu.VMEM((ti, tj), F32)],
                           sem=sem, comm=comm)
    out = out if rows_whole else out[0]
    return out if comm is None else (out, carried)


def _xhat(x):
    return x * lax.rsqrt(jnp.mean(x * x, axis=-1, keepdims=True) + EPS)


def _norm_bwd_rows(dy, x, g):
    rstd = lax.rsqrt(jnp.mean(x * x, axis=-1, keepdims=True) + EPS)
    xh = x * rstd
    dxh = dy * g
    dx = rstd * (dxh - xh * jnp.mean(dxh * xh, axis=-1, keepdims=True))
    return dx, dy * xh


def _rmsnorm(x, g, *, name, rows=512, comm=None):
    t, d = x.shape
    tr = _tile(t, rows, 2 * SUBLANES)

    def body(x_ref, g_ref, o_ref):
        o_ref[...] = (_xhat(x_ref[...]) * g_ref[...]).astype(BF16)

    out, carried = _pallas(
        body, name=name, grid=(t // tr,),
        in_specs=[pl.BlockSpec((tr, d), lambda i: (i, 0)), pl.BlockSpec((1, d), lambda i: (0, 0))],
        out_specs=[pl.BlockSpec((tr, d), lambda i: (i, 0))], out_shape=[jax.ShapeDtypeStruct((t, d), BF16)],
        args=[x, g.reshape(1, d)], sem=("parallel",), comm=comm)
    return out[0] if comm is None else (out[0], carried)


def _rmsnorm_bwd(dy, x, g, dres, *, name, rows=512):
    t, d = x.shape
    tr = _tile(t, rows, SUBLANES)

    def body(*refs):
        if dres is None:
            dy_ref, x_ref, g_ref, dx_ref, dg_ref = refs
        else:
            dy_ref, x_ref, g_ref, r_ref, dx_ref, dg_ref = refs
        dx, dgr = _norm_bwd_rows(dy_ref[...], x_ref[...], g_ref[...])
        dx_ref[...] = dx if dres is None else r_ref[...] + dx

        @pl.when(pl.program_id(0) == 0)
        def _():
            dg_ref[...] = jnp.zeros_like(dg_ref)

        dg_ref[...] += jnp.sum(dgr, axis=0, keepdims=True)

    row = pl.BlockSpec((tr, d), lambda i: (i, 0))
    vec = pl.BlockSpec((1, d), lambda i: (0, 0))
    in_specs = [row, row, vec] + ([] if dres is None else [row])
    args = (dy, x, g.reshape(1, d)) + (() if dres is None else (dres,))
    dx, dg = pl.pallas_call(
        body, name=name, grid=(t // tr,), in_specs=in_specs, out_specs=[row, vec],
        out_shape=[jax.ShapeDtypeStruct((t, d), F32), jax.ShapeDtypeStruct((1, d), F32)],
        compiler_params=_params(("arbitrary",)))(*args)
    return dx, dg.reshape(d)


def _loss_head(x, g, target, *, rows=512):
    t, d = x.shape
    tr = _tile(t, rows, SUBLANES)

    def body(x_ref, g_ref, t_ref, dx_ref, dg_ref, loss_ref):
        x_, g_ = x_ref[...], g_ref[...]
        err = _xhat(x_) * g_ - t_ref[...]
        dx, dgr = _norm_bwd_rows(err * (1.0 / d), x_, g_)
        dx_ref[...] = dx

        @pl.when(pl.program_id(0) == 0)
        def _():
            dg_ref[...] = jnp.zeros_like(dg_ref)
            loss_ref[...] = jnp.zeros_like(loss_ref)

        dg_ref[...] += jnp.sum(dgr, axis=0, keepdims=True)
        loss_ref[...] += 0.5 * jnp.sum(jnp.mean(err * err, axis=-1, keepdims=True), axis=0, keepdims=True)

    row = pl.BlockSpec((tr, d), lambda i: (i, 0))
    vec = pl.BlockSpec((1, d), lambda i: (0, 0))
    one = pl.BlockSpec((1, 1), lambda i: (0, 0))
    dx, dg, loss = pl.pallas_call(
        body, name="loss_head", grid=(t // tr,), in_specs=[row, vec, row], out_specs=[row, vec, one],
        out_shape=[jax.ShapeDtypeStruct((t, d), F32), jax.ShapeDtypeStruct((1, d), F32),
                   jax.ShapeDtypeStruct((1, 1), F32)],
        compiler_params=_params(("arbitrary",)))(x, g.reshape(1, d), target)
    return loss.reshape(()), dx, dg.reshape(d)


def _head_masks(shape):
    lane = lax.broadcasted_iota(jnp.int32, shape, len(shape) - 1)
    return lane < HEAD_DIM, lane >= HEAD_DIM


def _split_heads(x):
    m0, m1 = _head_masks(x.shape)
    zero = jnp.zeros_like(x)
    return jnp.where(m0, x, zero), jnp.where(m1, x, zero)


def _lane_pair(a0, a1, rows):
    m0, _ = _head_masks((rows, LANES))
    return jnp.where(m0, a0, a1)


def _qkv_readers(refs, packed):
    if packed:
        (r,) = refs
        return tuple((lambda r0, n, s=s: r[pl.ds(r0, n), s * LANES:(s + 1) * LANES]) for s in range(3))
    return tuple((lambda r0, n, ref=ref: ref[pl.ds(r0, n), :]) for ref in refs)


def _pair_spec(seq, col0, width=LANES):
    return pl.BlockSpec((seq, width), lambda p, b: (b, col0 + p))


def _fox_specs(seq, nk, tk):
    return [pl.BlockSpec((None, None, seq, 2), lambda p, b: (b, p, 0, 0)),
            pl.BlockSpec((None, None, nk, 2, tk), lambda p, b: (b, p, 0, 0, 0))]


def _softmax_attn_fwd(src, *, nb, mode, mixer=None, out_buf=None, extra=(), name, comm=None):
    packed = mode != "cross"
    n_src = 1 if packed else 3
    seq_q = (src if packed else src[0]).shape[0] // nb
    seq_k = seq_q if packed else src[1].shape[0] // nb
    tq, tk = min(ATT_TILE, seq_q), min(ATT_TILE, seq_k)
    nq, nk = seq_q // tq, seq_k // tk
    n_ex = len(extra)

    def body(*refs):
        q_at, k_at, v_at = _qkv_readers(refs[:n_src], packed)
        ex = refs[n_src:n_src + n_ex]
        o_ref, lse_ref = refs[-2:]

        def q_tile(i, _):
            r0 = pl.multiple_of(i * tq, tq)
            qm = _split_heads((q_at(r0, tq) * SCALE).astype(BF16))
            if mode == "fox":
                cq = ex[0][pl.ds(r0, tq), :]
                row = r0 + lax.broadcasted_iota(jnp.int32, (tq, tk), 0)

            def k_tile(j, carry, diagonal=False):
                m, l, acc = carry
                c0 = pl.multiple_of(j * tk, tk)
                kt = k_at(c0, tk).astype(BF16)
                vm = _split_heads(v_at(c0, tk).astype(BF16))
                if mode == "fox":
                    ck = ex[1][j]
                hs = range(2)
                s = [lax.dot_general(qm[h], kt, NT_DIMS, preferred_element_type=F32) for h in hs]
                if mode == "fox":
                    s = [s[h] + cq[:, h:h + 1] - ck[h:h + 1, :] for h in hs]
                    if diagonal:
                        keep = (c0 + lax.broadcasted_iota(jnp.int32, (tq, tk), 1)) <= row
                        s = [jnp.where(keep, s[h], MASKED) for h in hs]
                elif mode == "dil":
                    s = [s[h] + ex[0][h, i - j] for h in hs]
                new_m = [jnp.maximum(m[h], jnp.max(s[h], axis=-1, keepdims=True)) for h in hs]
                p = [jnp.exp(s[h] - new_m[h]) for h in hs]
                alpha = [jnp.exp(m[h] - new_m[h]) for h in hs]
                new_l = [alpha[h] * l[h] + jnp.sum(p[h], axis=-1, keepdims=True) for h in hs]
                pv = [jnp.dot(p[h].astype(BF16), vm[h], preferred_element_type=F32) for h in hs]
                acc = acc * _lane_pair(alpha[0], alpha[1], tq) + (pv[0] + pv[1])
                return tuple(new_m), tuple(new_l), acc

            init = ((jnp.full((tq, 1), MASKED, F32),) * 2, (jnp.zeros((tq, 1), F32),) * 2,
                    jnp.zeros((tq, LANES), F32))
            if mode == "fox":
                m, l, acc = k_tile(i, lax.fori_loop(0, i, k_tile, init), True)
            else:
                m, l, acc = lax.fori_loop(0, i + 1 if packed else nk, k_tile, init)
            o_ref[pl.ds(r0, tq), :] = acc / _lane_pair(l[0], l[1], tq)
            lse_ref[pl.ds(r0, tq), :] = _lane_pair(m[0] + jnp.log(l[0]), m[1] + jnp.log(l[1]), tq)
            return 0

        lax.fori_loop(0, nq, q_tile, 0)

    lse_shape = jax.ShapeDtypeStruct((nb * seq_q, 2 * LANES), F32)
    if packed:
        in_specs, args = [_pair_spec(seq_q, 2 * mixer, PAIR_W)], [src]
        in_specs += _fox_specs(seq_q, nk, tk) if mode == "fox" else [
            pl.BlockSpec((None, 2, nq, tq, tk), lambda p, b: (p, 0, 0, 0, 0))]
        args += list(extra) + [out_buf]
        in_specs.append(ANY)
        out_specs = [_pair_spec(seq_q, 2 * mixer), _pair_spec(seq_q, 0)]
        out_shape = [jax.ShapeDtypeStruct(out_buf.shape, F32), lse_shape]
        aliases = {len(args) - 1: 0}
    else:
        in_specs = [_pair_spec(seq_q, 0), _pair_spec(seq_k, 0), _pair_spec(seq_k, 0)]
        args = list(src)
        out_specs = [_pair_spec(seq_q, 0), _pair_spec(seq_q, 0)]
        out_shape = [lse_shape, lse_shape]
        aliases = {}
    out, carried = _pallas(body, name=name, grid=(2, nb), in_specs=in_specs, out_specs=out_specs, out_shape=out_shape,
                           args=args, aliases=aliases, sem=("parallel", "arbitrary"), comm=comm)
    return out if comm is None else (out, carried)


def _softmax_attn_bwd(src, o, lse, do, *, nb, mode, mixer=None, dbuf=None, extra=(), name, comm=None):
    packed = mode != "cross"
    n_src = 1 if packed else 3
    seq_q = (src if packed else src[0]).shape[0] // nb
    seq_k = seq_q if packed else src[1].shape[0] // nb
    tq, tk = min(ATT_TILE, seq_q), min(ATT_TILE, seq_k)
    nq, nk = seq_q // tq, seq_k // tk
    n_ex = len(extra)
    n_in = n_src + 3 + n_ex + (1 if packed else 0)

    def body(*refs):
        q_at, k_at, v_at = _qkv_readers(refs[:n_src], packed)
        o_ref, lse_ref, do_ref = refs[n_src:n_src + 3]
        ex = refs[n_src + 3:n_src + 3 + n_ex]
        outs = refs[n_in:]
        if packed:
            d_ref = outs[0]
            dq_w = lambda r0, val: d_ref.__setitem__((pl.ds(r0, tq), slice(0, LANES)), val)
            dk_ref = d_ref.at[:, LANES:2 * LANES]
            dv_ref = d_ref.at[:, 2 * LANES:3 * LANES]
        else:
            dq_ref, dk_ref, dv_ref = outs[:3]
            dq_w = lambda r0, val: dq_ref.__setitem__((pl.ds(r0, tq), slice(None)), val)
        dk_ref[...] = jnp.zeros((seq_k, LANES), F32)
        dv_ref[...] = jnp.zeros((seq_k, LANES), F32)
        if mode == "fox":
            dcum_ref, dcq_ref = outs[-2:]
            dcum_ref[...] = jnp.zeros_like(dcum_ref)
        if mode == "dil":
            dbias_ref = outs[-1]

            @pl.when(pl.program_id(1) == 0)
            def _():
                dbias_ref[...] = jnp.zeros_like(dbias_ref)

        def q_tile(i, _):
            r0 = pl.multiple_of(i * tq, tq)
            qm = _split_heads((q_at(r0, tq) * SCALE).astype(BF16))
            do_f = do_ref[pl.ds(r0, tq), :]
            dom = _split_heads(do_f.astype(BF16))
            dd = _split_heads(do_f * o_ref[pl.ds(r0, tq), :])
            delta = [jnp.sum(dd[h], axis=-1, keepdims=True) for h in range(2)]
            lse_t = lse_ref[pl.ds(r0, tq), :]
            lse_h = [lse_t[:, 0:1], lse_t[:, HEAD_DIM:HEAD_DIM + 1]]
            if mode == "fox":
                cq = ex[0][pl.ds(r0, tq), :]
                row = r0 + lax.broadcasted_iota(jnp.int32, (tq, tk), 0)

            def k_tile(j, carry, diagonal=False):
                dq, rs = carry
                c0 = pl.multiple_of(j * tk, tk)
                kt = k_at(c0, tk).astype(BF16)
                vt = v_at(c0, tk).astype(BF16)
                km = _split_heads(kt)
                if mode == "fox":
                    ck = ex[1][j]
                hs = range(2)
                s = [lax.dot_general(qm[h], kt, NT_DIMS, preferred_element_type=F32) for h in hs]
                dp = [lax.dot_general(dom[h], vt, NT_DIMS, preferred_element_type=F32) for h in hs]
                if mode == "fox":
                    s = [s[h] + cq[:, h:h + 1] - ck[h:h + 1, :] for h in hs]
                    if diagonal:
                        keep = (c0 + lax.broadcasted_iota(jnp.int32, (tq, tk), 1)) <= row
                        s = [jnp.where(keep, s[h], MASKED) for h in hs]
                elif mode == "dil":
                    s = [s[h] + ex[0][h, i - j] for h in hs]
                p = [jnp.exp(s[h] - lse_h[h]) for h in hs]
                ds = [p[h] * (dp[h] - delta[h]) for h in hs]
                dsb = [ds[h].astype(BF16) for h in hs]
                pb = [p[h].astype(BF16) for h in hs]
                dq = dq + (jnp.dot(dsb[0], km[0], preferred_element_type=F32)
                           + jnp.dot(dsb[1], km[1], preferred_element_type=F32))
                dk_t = (lax.dot_general(dsb[0], qm[0], TN_DIMS, preferred_element_type=F32)
                        + lax.dot_general(dsb[1], qm[1], TN_DIMS, preferred_element_type=F32))
                dv_t = (lax.dot_general(pb[0], dom[0], TN_DIMS, preferred_element_type=F32)
                        + lax.dot_general(pb[1], dom[1], TN_DIMS, preferred_element_type=F32))
                if mode == "fox":
                    for h in hs:
                        dcum_ref[j, h:h + 1, :] -= jnp.sum(ds[h], axis=0, keepdims=True)
                    rs = tuple(rs[h] + jnp.sum(ds[h], axis=-1, keepdims=True) for h in hs)
                elif mode == "dil":
                    for h in hs:
                        dbias_ref[h, i - j] += ds[h]
                dk_ref[pl.ds(c0, tk), :] += dk_t
                dv_ref[pl.ds(c0, tk), :] += dv_t
                return dq, rs

            zero = (jnp.zeros((tq, 1), F32),) * 2
            init = (jnp.zeros((tq, LANES), F32), zero)
            if mode == "fox":
                dq, rs = k_tile(i, lax.fori_loop(0, i, k_tile, init), True)
            else:
                dq, rs = lax.fori_loop(0, i + 1 if packed else nk, k_tile, init)
            dq_w(r0, dq * SCALE)
            if mode == "fox":
                dcq_ref[pl.ds(r0, tq), :] = jnp.where(lax.broadcasted_iota(jnp.int32, (tq, 2), 1) == 0, rs[0], rs[1])
            return 0

        lax.fori_loop(0, nq, q_tile, 0)

    if packed:
        in_specs = [_pair_spec(seq_q, 2 * mixer, PAIR_W), _pair_spec(seq_q, 2 * mixer), _pair_spec(seq_q, 0),
                    _pair_spec(seq_q, 2 * mixer)]
        args = [src, o, lse, do]
        out_specs = [_pair_spec(seq_q, 2 * mixer, PAIR_W)]
        out_shape = [jax.ShapeDtypeStruct(dbuf.shape, F32)]
        if mode == "fox":
            in_specs += _fox_specs(seq_q, nk, tk)
            out_specs += [_fox_specs(seq_q, nk, tk)[1], _fox_specs(seq_q, nk, tk)[0]]
            out_shape += [jax.ShapeDtypeStruct((nb, 2, nk, 2, tk), F32), jax.ShapeDtypeStruct((nb, 2, seq_q, 2), F32)]
        else:
            tiles = pl.BlockSpec((None, 2, nq, tq, tk), lambda p, b: (p, 0, 0, 0, 0))
            in_specs.append(tiles)
            out_specs.append(tiles)
            out_shape.append(jax.ShapeDtypeStruct((2, 2, nq, tq, tk), F32))
        args += list(extra) + [dbuf]
        in_specs.append(ANY)
        aliases = {len(args) - 1: 0}
    else:
        sq, sk = _pair_spec(seq_q, 0), _pair_spec(seq_k, 0)
        in_specs, args = [sq, sk, sk, sq, sq, sq], list(src) + [o, lse, do]
        out_specs = [sq, sk, sk]
        out_shape = [jax.ShapeDtypeStruct((nb * seq_q, 2 * LANES), F32)] + [
            jax.ShapeDtypeStruct((nb * seq_k, 2 * LANES), F32)] * 2
        aliases = {}
    out, carried = _pallas(body, name=name, grid=(2, nb), in_specs=in_specs, out_specs=out_specs, out_shape=out_shape,
                           args=args, aliases=aliases, sem=("parallel", "arbitrary"), comm=comm)
    return out if comm is None else (out, carried)


def _log_sigmoid(z):
    return jnp.minimum(z, 0.0) - jnp.log(1.0 + jnp.exp(-jnp.abs(z)))


def _split_bf16(x):
    hi = x.astype(BF16)
    return hi, (x - hi.astype(F32)).astype(BF16)


def _tri(n, fn):
    r = lax.broadcasted_iota(jnp.int32, (n, n), 0)
    c = lax.broadcasted_iota(jnp.int32, (n, n), 1)
    return jnp.where(fn(r, c), 1.0, 0.0).astype(BF16)


def _sb_attn_fwd(proj, out_buf, *, nb, name, comm=None):
    seq = proj.shape[0] // nb
    tq = tk = min(ATT_TILE, seq)
    nq = seq // tq

    def body(qkv_ref, _, o_ref, lt_ref):
        rd = [_qkv_readers((qkv_ref.at[:, pr * PAIR_W:(pr + 1) * PAIR_W],), True) for pr in range(2)]
        after = _tri(tk, lambda r, c: r > c)
        ch = [(pr, h) for pr in range(2) for h in range(2)]

        def q_tile(i, _):
            r0 = pl.multiple_of(i * tq, tq)
            qm = [_split_heads((rd[pr][0](r0, tq) * SCALE).astype(BF16)) for pr in range(2)]
            row = r0 + lax.broadcasted_iota(jnp.int32, (tq, tk), 0)

            def k_tile(j, carry, diagonal):
                c, acc = carry
                c0 = pl.multiple_of(j * tk, tk)
                kt = [rd[pr][1](c0, tk).astype(BF16) for pr in range(2)]
                vm = [_split_heads(rd[pr][2](c0, tk).astype(BF16)) for pr in range(2)]
                if diagonal:
                    strict = (c0 + lax.broadcasted_iota(jnp.int32, (tq, tk), 1)) < row
                ns = range(len(ch))
                z = [lax.dot_general(qm[pr][h], kt[pr], NT_DIMS, preferred_element_type=F32) for pr, h in ch]
                ls = [_log_sigmoid(z[n]) for n in ns]
                lk = [ls[n] - z[n] for n in ns]
                if diagonal:
                    lk = [jnp.where(strict, lk[n], 0.0) for n in ns]
                parts = [_split_bf16(lk[n]) for n in ns]
                sfx = [jnp.dot(parts[n][0], after, preferred_element_type=F32)
                       + jnp.dot(parts[n][1], after, preferred_element_type=F32) for n in ns]
                att = [jnp.exp(ls[n] + sfx[n] + c[n]) for n in ns]
                if diagonal:
                    att = [jnp.where(strict, att[n], 0.0) for n in ns]
                acc = tuple(acc[pr] + (jnp.dot(att[2 * pr].astype(BF16), vm[pr][0], preferred_element_type=F32)
                                       + jnp.dot(att[2 * pr + 1].astype(BF16), vm[pr][1], preferred_element_type=F32))
                            for pr in range(2))
                return tuple(c[n] + jnp.sum(lk[n], axis=-1, keepdims=True) for n in ns), acc

            init = ((jnp.zeros((tq, 1), F32),) * 4, (jnp.zeros((tq, LANES), F32),) * 2)
            c, acc = lax.fori_loop(1, i + 1, lambda jj, cr: k_tile(i - jj, cr, False), k_tile(i, init, True))
            for pr in range(2):
                o_ref[pl.ds(r0, tq), pr * LANES:(pr + 1) * LANES] = acc[pr]
                lt_ref[pl.ds(r0, tq), pr * LANES:(pr + 1) * LANES] = _lane_pair(c[2 * pr], c[2 * pr + 1], tq)
            return 0

        lax.fori_loop(0, nq, q_tile, 0)

    both = lambda width, col: pl.BlockSpec((seq, 2 * width), lambda b: (b, col))
    out, carried = _pallas(
        body, name=name, grid=(nb,), in_specs=[both(PAIR_W, MIX_SB), ANY],
        out_specs=[both(LANES, MIX_SB), both(LANES, 0)],
        out_shape=[jax.ShapeDtypeStruct(out_buf.shape, F32), jax.ShapeDtypeStruct((nb * seq, 2 * LANES), F32)],
        args=[proj, out_buf], aliases={1: 0}, sem=("arbitrary",), comm=comm)
    return out if comm is None else (out, carried)


def _sb_attn_bwd(proj, ltot, do, dbuf, *, nb, name, comm=None):
    seq = proj.shape[0] // nb
    tq = tk = min(ATT_TILE, seq)
    nq = seq // tq

    def body(qkv_ref, lt_ref, do_ref, _, d_ref):
        rd = [_qkv_readers((qkv_ref.at[:, pr * PAIR_W:(pr + 1) * PAIR_W],), True) for pr in range(2)]
        upto = _tri(tk, lambda r, c: r <= c)
        before = _tri(tk, lambda r, c: r < c)
        dk_ref = [d_ref.at[:, pr * PAIR_W + LANES:pr * PAIR_W + 2 * LANES] for pr in range(2)]
        dv_ref = [d_ref.at[:, pr * PAIR_W + 2 * LANES:(pr + 1) * PAIR_W] for pr in range(2)]
        for ref in dk_ref + dv_ref:
            ref[...] = jnp.zeros((seq, LANES), F32)
        ch = [(pr, h) for pr in range(2) for h in range(2)]

        def q_tile(i, _):
            r0 = pl.multiple_of(i * tq, tq)
            qm = [_split_heads((rd[pr][0](r0, tq) * SCALE).astype(BF16)) for pr in range(2)]
            dom = [_split_heads(do_ref[pl.ds(r0, tq), pr * LANES:(pr + 1) * LANES].astype(BF16)) for pr in range(2)]
            lt_t = lt_ref[pl.ds(r0, tq), :]
            lt_h = [lt_t[:, pr * LANES + h * HEAD_DIM:pr * LANES + h * HEAD_DIM + 1] for pr, h in ch]
            row = r0 + lax.broadcasted_iota(jnp.int32, (tq, tk), 0)

            def k_tile(j, carry, diagonal):
                pc, qc, dq = carry
                c0 = pl.multiple_of(j * tk, tk)
                kt = [rd[pr][1](c0, tk).astype(BF16) for pr in range(2)]
                vt = [rd[pr][2](c0, tk).astype(BF16) for pr in range(2)]
                km = [_split_heads(kt[pr]) for pr in range(2)]
                if diagonal:
                    strict = (c0 + lax.broadcasted_iota(jnp.int32, (tq, tk), 1)) < row
                ns = range(len(ch))
                z = [lax.dot_general(qm[pr][h], kt[pr], NT_DIMS, preferred_element_type=F32) for pr, h in ch]
                da = [lax.dot_general(dom[pr][h], vt[pr], NT_DIMS, preferred_element_type=F32) for pr, h in ch]
                ls = [_log_sigmoid(z[n]) for n in ns]
                lk = [ls[n] - z[n] for n in ns]
                if diagonal:
                    lk = [jnp.where(strict, lk[n], 0.0) for n in ns]
                parts = [_split_bf16(lk[n]) for n in ns]
                pin = [jnp.dot(parts[n][0], upto, preferred_element_type=F32)
                       + jnp.dot(parts[n][1], upto, preferred_element_type=F32) for n in ns]
                att = [jnp.exp(ls[n] + (lt_h[n] - pc[n] - pin[n])) for n in ns]
                if diagonal:
                    att = [jnp.where(strict, att[n], 0.0) for n in ns]
                dg = [att[n] * da[n] for n in ns]
                qx = [qc[n] + jnp.dot(dg[n].astype(BF16), before, preferred_element_type=F32) for n in ns]
                sig = [jnp.exp(ls[n]) for n in ns]
                dz = [dg[n] * (1.0 - sig[n]) - sig[n] * qx[n] for n in ns]
                if diagonal:
                    dz = [jnp.where(strict, dz[n], 0.0) for n in ns]
                dzb = [dz[n].astype(BF16) for n in ns]
                attb = [att[n].astype(BF16) for n in ns]
                new_dq = []
                for pr in range(2):
                    a, b = 2 * pr, 2 * pr + 1
                    new_dq.append(dq[pr] + (jnp.dot(dzb[a], km[pr][0], preferred_element_type=F32)
                                            + jnp.dot(dzb[b], km[pr][1], preferred_element_type=F32)))
                    dk_ref[pr][pl.ds(c0, tk), :] += (
                        lax.dot_general(dzb[a], qm[pr][0], TN_DIMS, preferred_element_type=F32)
                        + lax.dot_general(dzb[b], qm[pr][1], TN_DIMS, preferred_element_type=F32))
                    dv_ref[pr][pl.ds(c0, tk), :] += (
                        lax.dot_general(attb[a], dom[pr][0], TN_DIMS, preferred_element_type=F32)
                        + lax.dot_general(attb[b], dom[pr][1], TN_DIMS, preferred_element_type=F32))
                return (tuple(pc[n] + jnp.sum(lk[n], axis=-1, keepdims=True) for n in ns),
                        tuple(qc[n] + jnp.sum(dg[n], axis=-1, keepdims=True) for n in ns), tuple(new_dq))

            zero = (jnp.zeros((tq, 1), F32),) * 4
            init = (zero, zero, (jnp.zeros((tq, LANES), F32),) * 2)
            carry = lax.fori_loop(0, i, lambda j, cr: k_tile(j, cr, False), init)
            _, _, dq = k_tile(i, carry, True)
            for pr in range(2):
                d_ref[pl.ds(r0, tq), pr * PAIR_W:pr * PAIR_W + LANES] = dq[pr] * SCALE
            return 0

        lax.fori_loop(0, nq, q_tile, 0)

    both = lambda width, col: pl.BlockSpec((seq, 2 * width), lambda b: (b, col))
    out, carried = _pallas(
        body, name=name, grid=(nb,), in_specs=[both(PAIR_W, MIX_SB), both(LANES, 0), both(LANES, MIX_SB), ANY],
        out_specs=[both(PAIR_W, MIX_SB)], out_shape=[jax.ShapeDtypeStruct(dbuf.shape, F32)],
        args=[proj, ltot, do, dbuf], aliases={3: 0}, sem=("arbitrary",), comm=comm)
    return out[0] if comm is None else (out[0], carried)


def _lane_scan(x, reverse=False):
    n = x.shape[-1]
    lane = lax.broadcasted_iota(jnp.int32, x.shape, 1)
    k = 1
    while k < n:
        if reverse:
            x = x + jnp.where(lane < n - k, pltpu.roll(x, n - k, 1), 0.0)
        else:
            x = x + jnp.where(lane >= k, pltpu.roll(x, k, 1), 0.0)
        k *= 2
    return x


def _fox_gate_fwd(f_rows, b_rows):
    def body(f_ref, b_ref, o_ref):
        o_ref[...] = _lane_scan(_log_sigmoid(f_ref[...] + b_ref[...]))

    return pl.pallas_call(body, name="fox_gate_fwd", out_shape=jax.ShapeDtypeStruct(f_rows.shape, F32))(f_rows, b_rows)


def _fox_gate_bwd(dcum, f_rows, b_rows):
    def body(d_ref, f_ref, b_ref, df_ref, db_ref):
        z = f_ref[...] + b_ref[...]
        df = _lane_scan(d_ref[...], reverse=True) * jnp.exp(_log_sigmoid(-z))
        df_ref[...] = df
        rs = jnp.sum(df, axis=-1, keepdims=True)
        tot = rs
        for e in range(1, f_rows.shape[0] // N_HEADS):
            tot = tot + pltpu.roll(rs, e * N_HEADS, 0)
        db_ref[...] = tot

    return pl.pallas_call(
        body, name="fox_gate_bwd",
        out_shape=[jax.ShapeDtypeStruct(f_rows.shape, F32), jax.ShapeDtypeStruct((f_rows.shape[0], 1), F32)],
    )(dcum, f_rows, b_rows)


def _dil_tables(seq):
    t = min(ATT_TILE, seq)
    n = seq // t
    a = np.arange(t)
    d = (np.arange(n)[:, None, None] * t + a[None, :, None] - a[None, None, :]).astype(np.int64)
    count = np.zeros(d.shape, np.int64)
    for window, dil in DILATED_PATTERNS:
        count += (d >= 0) & (d % dil == 0) & (d // dil <= window // dil)
    nn = np.maximum(d, 0)
    max_exact = NUM_BUCKETS // 2
    nf = np.maximum(nn, 1).astype(np.float32)
    large = max_exact + (np.log(nf / np.float32(max_exact)) / np.float32(math.log(MAX_DISTANCE / max_exact))
                         * np.float32(NUM_BUCKETS - max_exact)).astype(np.int32)
    bucket = np.where(nn < max_exact, nn, np.minimum(large, NUM_BUCKETS - 1))
    bucket = np.where(count > 0, bucket, -1).astype(np.int32)
    logc = np.where(count > 0, np.log(np.maximum(count, 1)), MASKED).astype(np.float32)
    return bucket, logc


def _dil_bias(rel_bias, seq, comm=None):
    bucket, logc = _dil_tables(seq)
    n, t, _ = bucket.shape

    def body(rb_ref, bk_ref, lc_ref, o_ref):
        h = pl.program_id(0) * 2 + pl.program_id(1)
        bk = bk_ref[...]
        out = lc_ref[...]
        for b in range(NUM_BUCKETS):
            out = jnp.where(bk == b, out + rb_ref[b, h], out)
        o_ref[...] = out

    full = pl.BlockSpec((n, t, t), lambda p, h: (0, 0, 0))
    out, carried = _pallas(
        body, name="dil_bias", grid=(2, 2),
        in_specs=[pl.BlockSpec(memory_space=pltpu.SMEM), full, full],
        out_specs=[pl.BlockSpec((None, None, n, t, t), lambda p, h: (p, h, 0, 0, 0))],
        out_shape=[jax.ShapeDtypeStruct((2, 2, n, t, t), F32)],
        args=[rel_bias, jnp.asarray(bucket), jnp.asarray(logc)], sem=("parallel", "parallel"), comm=comm)
    return out[0] if comm is None else (out[0], carried)


def _dil_bias_bwd(dbias, seq):
    bucket, _ = _dil_tables(seq)
    n, t, _ = bucket.shape

    def body(d_ref, bk_ref, o_ref):
        bk = bk_ref[...]
        lane = lax.broadcasted_iota(jnp.int32, (1, LANES), 1)
        for b in range(NUM_BUCKETS):
            rowv = jnp.zeros((1, LANES), F32)
            for h in range(N_HEADS):
                s = jnp.sum(jnp.where(bk == b, d_ref[h // 2, h % 2], 0.0))
                rowv = jnp.where(lane == h, s, rowv)
            o_ref[b:b + 1, :] = rowv

    out = pl.pallas_call(body, name="dil_bias_bwd", out_shape=jax.ShapeDtypeStruct((NUM_BUCKETS, LANES), F32),
                         compiler_params=pltpu.CompilerParams(vmem_limit_bytes=VMEM_LIMIT))(dbias, jnp.asarray(bucket))
    return out[:, :N_HEADS]


def _shift_rows(x, k, row, fill=0.0):
    n = x.shape[0]
    if k > 0:
        return jnp.where(row >= k, pltpu.roll(x, k, 0), fill)
    return jnp.where(row < n + k, pltpu.roll(x, n + k, 0), fill)


def _row_scan(a, u, row, reverse=False):
    n = a.shape[0]
    k = 1
    while k < n:
        s = -k if reverse else k
        u = a * _shift_rows(u, s, row) + u
        a = a * _shift_rows(a, s, row, 1.0)
        k *= 2
    return u


def _sigmoid(x):
    return 1.0 / (1.0 + jnp.exp(-x))


def _gelu(g):
    return 0.5 * g * (1.0 + lax.erf(g * (2.0 ** -0.5)))


def _gelu_grad(g):
    return 0.5 * (1.0 + lax.erf(g * (2.0 ** -0.5))) + g * jnp.exp(-0.5 * g * g) * (1.0 / math.sqrt(2.0 * math.pi))


def _neg_expm1(x):
    small = -x * (1.0 + x * (0.5 + x * (1.0 / 6.0 + x * (1.0 / 24.0))))
    return jnp.where(x > -0.03, small, 1.0 - jnp.exp(x))


def _lru_core(x, vec, wa, wx, row):
    xs = [_shift_rows(x, 3 - j, row) if j < 3 else x for j in range(4)]
    xc = vec[4:5, :]
    for j in range(4):
        xc = xc + vec[j:j + 1, :] * xs[j]
    xcb = xc.astype(BF16)
    r = _sigmoid(jnp.dot(xcb, wa, preferred_element_type=F32) + vec[5:6, :])
    ig = _sigmoid(jnp.dot(xcb, wx, preferred_element_type=F32) + vec[6:7, :])
    lam = vec[7:8, :]
    sp = jnp.maximum(-lam, 0.0) - _log_sigmoid(jnp.abs(lam))
    la = -LRU_C * r * sp
    a = jnp.exp(la)
    mult = jnp.sqrt(_neg_expm1(2.0 * la))
    return xs, xc, xcb, r, ig, sp, la, a, mult


def _lru_specs(seq):
    xg = pl.BlockSpec((seq, LRU_W), lambda hf, b: (b, COL_LRU // LRU_W + hf))
    mix = pl.BlockSpec((seq, LANES), lambda hf, b: (b, 2 * MIX_LRU + hf))
    vec = pl.BlockSpec((SUBLANES, LANES), lambda hf, b: (0, hf))
    mat = pl.BlockSpec((None, LANES, LANES), lambda hf, b: (hf, 0, 0))
    return xg, mix, vec, mat


def _lru_fwd(proj, vec, wa, wx, out_buf, *, nb, name):
    seq = proj.shape[0] // nb

    def body(xg_ref, vec_ref, wa_ref, wx_ref, _, o_ref):
        row = lax.broadcasted_iota(jnp.int32, (seq, LANES), 0)
        _, xc, _, _, ig, _, _, a, mult = _lru_core(xg_ref[:, 0:LANES], vec_ref[...], wa_ref[...], wx_ref[...], row)
        h = _row_scan(a, mult * (ig * xc), row)
        o_ref[...] = h * _gelu(xg_ref[:, LANES:LRU_W])

    xg, mix, vecs, mat = _lru_specs(seq)
    return pl.pallas_call(
        body, name=name, grid=(2, nb), in_specs=[xg, vecs, mat, mat, ANY], out_specs=mix,
        out_shape=jax.ShapeDtypeStruct(out_buf.shape, F32), input_output_aliases={4: 0},
        compiler_params=_params(("parallel", "arbitrary")))(proj, vec, wa, wx, out_buf)


def _lru_bwd(proj, vec, wa, wx, dout, dbuf, *, nb, name):
    seq = proj.shape[0] // nb

    def body(xg_ref, vec_ref, wa_ref, wx_ref, do_ref, _, d_ref, dvec_ref, dwa_ref, dwx_ref):
        row = lax.broadcasted_iota(jnp.int32, (seq, LANES), 0)
        vec_, wa_, wx_ = vec_ref[...], wa_ref[...], wx_ref[...]
        xs, xc, xcb, r, ig, sp, la, a, mult = _lru_core(xg_ref[:, 0:LANES], vec_, wa_, wx_, row)
        h = _row_scan(a, mult * (ig * xc), row)
        gate, do = xg_ref[:, LANES:LRU_W], do_ref[...]
        d_ref[:, LANES:LRU_W] = do * h * _gelu_grad(gate)
        dh = do * _gelu(gate)
        gacc = _row_scan(_shift_rows(a, -1, row), dh, row, reverse=True)
        da = gacc * _shift_rows(h, 1, row)
        dmult = gacc * (ig * xc)
        dig = gacc * (mult * xc)
        dxc = gacc * (mult * ig)
        dla = da * a - dmult * (a * a) / mult
        dr = (-LRU_C) * sp * dla
        dsp = jnp.sum((-LRU_C) * r * dla, axis=0, keepdims=True)
        dpr = dr * r * (1.0 - r)
        dpi = dig * ig * (1.0 - ig)
        dprb, dpib = dpr.astype(BF16), dpi.astype(BF16)
        dxc = (dxc + lax.dot_general(dprb, wa_, NT_DIMS, preferred_element_type=F32)
               + lax.dot_general(dpib, wx_, NT_DIMS, preferred_element_type=F32))
        dx = vec_[3:4, :] * dxc
        for j in range(3):
            dx = dx + vec_[j:j + 1, :] * _shift_rows(dxc, -(3 - j), row)
        d_ref[:, 0:LANES] = dx

        @pl.when(pl.program_id(1) == 0)
        def _():
            dvec_ref[...] = jnp.zeros_like(dvec_ref)
            dwa_ref[...] = jnp.zeros_like(dwa_ref)
            dwx_ref[...] = jnp.zeros_like(dwx_ref)

        for j in range(4):
            dvec_ref[j:j + 1, :] += jnp.sum(dxc * xs[j], axis=0, keepdims=True)
        dvec_ref[4:5, :] += jnp.sum(dxc, axis=0, keepdims=True)
        dvec_ref[5:6, :] += jnp.sum(dpr, axis=0, keepdims=True)
        dvec_ref[6:7, :] += jnp.sum(dpi, axis=0, keepdims=True)
        lam = vec_[7:8, :]
        dvec_ref[7:8, :] += -dsp * _sigmoid(-lam)
        dwa_ref[...] += lax.dot_general(xcb, dprb, TN_DIMS, preferred_element_type=F32)
        dwx_ref[...] += lax.dot_general(xcb, dpib, TN_DIMS, preferred_element_type=F32)

    xg, mix, vecs, mat = _lru_specs(seq)
    return pl.pallas_call(
        body, name=name, grid=(2, nb), in_specs=[xg, vecs, mat, mat, mix, ANY], out_specs=[xg, vecs, mat, mat],
        out_shape=[jax.ShapeDtypeStruct(dbuf.shape, F32), jax.ShapeDtypeStruct((SUBLANES, 2 * LANES), F32),
                   jax.ShapeDtypeStruct((2, LANES, LANES), F32), jax.ShapeDtypeStruct((2, LANES, LANES), F32)],
        input_output_aliases={5: 0},
        compiler_params=_params(("parallel", "arbitrary")))(proj, vec, wa, wx, dout, dbuf)


FFN_ROWS = 256
FFN_COLS = 1408


def _with_halo(halo, x, k):
    xx = jnp.concatenate([halo, x], axis=0)
    return pltpu.roll(xx, k, 0)[SUBLANES:, :]


def _ffn_conv(x_ref, halo_ref, cw, pos):
    x, halo = x_ref[...], halo_ref[...]
    x1 = jnp.where(pos >= 1, _with_halo(halo, x, 1), 0.0)
    x2 = jnp.where(pos >= 2, _with_halo(halo, x, 2), 0.0)
    return cw[3:4, :] + cw[0:1, :] * x2 + cw[1:2, :] * x1 + cw[2:3, :] * x, x1, x2


def _ffn_specs(tm, tn, gate_off):
    prev = lambda i: jnp.maximum(i * (tm // SUBLANES) - 1, 0)
    up = pl.BlockSpec((tm, tn), lambda j, i: (i, j))
    gate = pl.BlockSpec((tm, tn), lambda j, i: (i, j + gate_off))
    up_h = pl.BlockSpec((SUBLANES, tn), lambda j, i: (prev(i), j))
    gate_h = pl.BlockSpec((SUBLANES, tn), lambda j, i: (prev(i), j + gate_off))
    cw_up = pl.BlockSpec((SUBLANES, tn), lambda j, i: (0, j))
    cw_gate = pl.BlockSpec((SUBLANES, tn), lambda j, i: (0, j + gate_off))
    return up, gate, up_h, gate_h, cw_up, cw_gate


def _ffn_act(hf, cw, *, seq, name):
    t, w2 = hf.shape
    w = w2 // 2
    tm, tn = _tile(seq, FFN_ROWS, SUBLANES), _tile(w, FFN_COLS)

    def body(u_ref, g_ref, uh_ref, gh_ref, cu_ref, cg_ref, o_ref):
        pos = (pl.program_id(1) * tm + lax.broadcasted_iota(jnp.int32, (tm, 1), 0)) % seq
        up, _, _ = _ffn_conv(u_ref, uh_ref, cu_ref[...], pos)
        gate, _, _ = _ffn_conv(g_ref, gh_ref, cg_ref[...], pos)
        o_ref[...] = (_gelu(gate) * up).astype(BF16)

    specs = _ffn_specs(tm, tn, w // tn)
    return pl.pallas_call(
        body, name=name, grid=(w // tn, t // tm), in_specs=list(specs), out_specs=specs[0],
        out_shape=jax.ShapeDtypeStruct((t, w), BF16),
        compiler_params=_params(("parallel", "parallel")))(hf, hf, hf, hf, cw, cw)


def _ffn_bwd(hf, cw, dact, *, seq, name, comm=None):
    t, w2 = hf.shape
    w = w2 // 2
    tm, tn = _tile(seq, FFN_ROWS, 2 * SUBLANES), _tile(w, FFN_COLS)
    ext = tm + SUBLANES
    last = t // SUBLANES - 1

    def body(u_ref, g_ref, uh_ref, gh_ref, cu_ref, cg_ref, un_ref, gn_ref, da_ref, dn_ref, d_ref, dcu_ref, dcg_ref):
        pos = (pl.program_id(1) * tm + lax.broadcasted_iota(jnp.int32, (ext, 1), 0)) % seq

        def conv(x_ref, prev_ref, next_ref, cwv):
            xx = jnp.concatenate([prev_ref[...], x_ref[...], next_ref[...]], axis=0)
            x1 = jnp.where(pos >= 1, pltpu.roll(xx, 1, 0)[SUBLANES:, :], 0.0)
            x2 = jnp.where(pos >= 2, pltpu.roll(xx, 2, 0)[SUBLANES:, :], 0.0)
            x0 = xx[SUBLANES:, :]
            return cwv[3:4, :] + cwv[0:1, :] * x2 + cwv[1:2, :] * x1 + cwv[2:3, :] * x0, (x2, x1, x0)

        def back(d, cwv):
            d1 = jnp.where(pos < seq - 1, pltpu.roll(d, ext - 1, 0), 0.0)
            d2 = jnp.where(pos < seq - 2, pltpu.roll(d, ext - 2, 0), 0.0)
            return (cwv[2:3, :] * d + cwv[1:2, :] * d1 + cwv[0:1, :] * d2)[:tm, :].astype(BF16)

        cu, cg = cu_ref[...], cg_ref[...]
        up, u_taps = conv(u_ref, uh_ref, un_ref, cu)
        gate, g_taps = conv(g_ref, gh_ref, gn_ref, cg)
        da = jnp.concatenate([da_ref[...], dn_ref[...]], axis=0)
        cdf = 0.5 * (1.0 + lax.erf(gate * (2.0 ** -0.5)))
        d_up = da * (gate * cdf)
        d_gate = da * up * (cdf + gate * jnp.exp(-0.5 * gate * gate) * (1.0 / math.sqrt(2.0 * math.pi)))
        d_ref[0] = back(d_up, cu)
        d_ref[1] = back(d_gate, cg)

        @pl.when(pl.program_id(1) == 0)
        def _():
            dcu_ref[...] = jnp.zeros_like(dcu_ref)
            dcg_ref[...] = jnp.zeros_like(dcg_ref)

        for ref, d, taps in ((dcu_ref, d_up, u_taps), (dcg_ref, d_gate, g_taps)):
            own = d[:tm, :]
            for j in range(3):
                ref[j:j + 1, :] += jnp.sum(own * taps[j][:tm, :], axis=0, keepdims=True)
            ref[3:4, :] += jnp.sum(own, axis=0, keepdims=True)

    gate_off = w // tn
    specs = _ffn_specs(tm, tn, gate_off)
    tile, cwt = specs[0], specs[4]
    nxt = lambda i: jnp.minimum((i + 1) * (tm // SUBLANES), last)
    up_n = pl.BlockSpec((SUBLANES, tn), lambda j, i: (nxt(i), j))
    gate_n = pl.BlockSpec((SUBLANES, tn), lambda j, i: (nxt(i), j + gate_off))
    out, carried = _pallas(
        body, name=name, grid=(w // tn, t // tm), in_specs=list(specs) + [up_n, gate_n, tile, up_n],
        out_specs=[pl.BlockSpec((2, tm, tn), lambda j, i: (0, i, j)), cwt, cwt],
        out_shape=[jax.ShapeDtypeStruct((2, t, w), BF16), jax.ShapeDtypeStruct((SUBLANES, w), F32),
                   jax.ShapeDtypeStruct((SUBLANES, w), F32)],
        args=[hf, hf, hf, hf, cw, cw, hf, hf, dact, dact], sem=("parallel", "arbitrary"), comm=comm)
    return out if comm is None else (out, carried)


def _adamw(w, g, m, v, *, name, rows=256, lead=None):
    nl, r, c = w.shape
    tr = _tile(r, rows, SUBLANES)

    def body(w_ref, g_ref, m_ref, v_ref, d_ref, nm_ref, nv_ref):
        g_ = g_ref[...]
        nm = ADAM_B1 * m_ref[...] + (1.0 - ADAM_B1) * g_
        nv = ADAM_B2 * v_ref[...] + (1.0 - ADAM_B2) * (g_ * g_)
        m_hat = nm / (1.0 - ADAM_B1 ** ADAM_STEP)
        v_hat = nv / (1.0 - ADAM_B2 ** ADAM_STEP)
        d_ref[...] = -ADAM_LR * (m_hat / (jnp.sqrt(v_hat) + ADAM_EPS) + ADAM_WD * w_ref[...])
        nm_ref[...] = nm
        nv_ref[...] = nv

    shape = jax.ShapeDtypeStruct((nl, r, c), F32)
    if lead is None:
        spec, grid = pl.BlockSpec((None, tr, c), lambda l, i: (l, i, 0)), (nl, r // tr)
    else:
        spec, grid = pl.BlockSpec((lead, r, c), lambda i: (i, 0, 0)), (nl // lead,)
    return pl.pallas_call(body, name=name, grid=grid, in_specs=[spec] * 4, out_specs=[spec] * 3,
                          out_shape=[shape] * 3, compiler_params=_params(("parallel",) * len(grid)))(w, g, m, v)


def _mesh_pos():
    return lax.axis_index("x"), lax.axis_index("y"), lax.axis_index("c")


def _peers(x, y):
    chips = [(1 - x, y), (x, 1 - y), (1 - x, 1 - y)]
    return [(px, py, 2 * px + py) for px, py in chips]


def _remote(src, dst, send_sems, recv_sems, idx, to):
    return pltpu.make_async_remote_copy(src, dst, send_sems.at[idx], recv_sems.at[idx], device_id=to,
                                        device_id_type=MESH)


class _Comm:
    def __init__(self, operands, out_shape, aliases, sems, copies):
        self.operands, self.out_shape, self.aliases, self.sems, self.copies = operands, out_shape, aliases, sems, copies

    def start(self, ins, outs, sems):
        for send, _ in self.copies(ins, outs, sems):
            send.start()

    def wait(self, ins, outs, sems):
        pairs = self.copies(ins, outs, sems)
        for _, recv in pairs:
            recv.wait_recv()
        for send, _ in pairs:
            send.wait_send()


def _pallas(body, *, name, grid, in_specs, out_specs, out_shape, args, aliases=None, scratch=(), sem, comm=None):
    n_in, n_out = len(in_specs), len(out_specs)
    aliases = dict(aliases or {})
    if comm is None:
        out = pl.pallas_call(body, name=name, grid=grid, in_specs=in_specs, out_specs=out_specs, out_shape=out_shape,
                             input_output_aliases=aliases, scratch_shapes=list(scratch),
                             compiler_params=_params(sem))(*args)
        return list(out), []
    nci, nco, ncs = len(comm.operands), len(comm.out_shape), len(comm.sems)

    def carried(*refs):
        ins, cin = refs[:n_in], refs[n_in:n_in + nci]
        o0 = n_in + nci
        outs, cout = refs[o0:o0 + n_out], refs[o0 + n_out:o0 + n_out + nco]
        s0 = o0 + n_out + nco
        own, csem = refs[s0:len(refs) - ncs], refs[len(refs) - ncs:]
        ids = [pl.program_id(ax) for ax in range(len(grid))]
        first, last = ids[0] == 0, ids[0] == grid[0] - 1
        for i, g in zip(ids[1:], grid[1:]):
            first, last = jnp.logical_and(first, i == 0), jnp.logical_and(last, i == g - 1)

        @pl.when(first)
        def _():
            comm.start(cin, cout, csem)

        body(*ins, *outs, *own)

        @pl.when(last)
        def _():
            comm.wait(cin, cout, csem)

    aliases.update({n_in + i: n_out + j for i, j in comm.aliases.items()})
    out = pl.pallas_call(
        carried, name=name, grid=grid, in_specs=list(in_specs) + [ANY] * nci, out_specs=list(out_specs) + [ANY] * nco,
        out_shape=list(out_shape) + list(comm.out_shape), input_output_aliases=aliases,
        scratch_shapes=list(scratch) + list(comm.sems),
        compiler_params=_params(("arbitrary",) * len(grid)))(*args, *comm.operands)
    return list(out[:n_out]), list(out[n_out:])


def _run_comm(comm, *, name):
    nci, nco = len(comm.operands), len(comm.out_shape)

    def body(*refs):
        ins, outs, sems = refs[:nci], refs[nci:nci + nco], refs[nci + nco:]
        comm.start(ins, outs, sems)
        comm.wait(ins, outs, sems)

    return pl.pallas_call(body, name=name, in_specs=[ANY] * nci, out_specs=[ANY] * nco, out_shape=list(comm.out_shape),
                          input_output_aliases=dict(comm.aliases), scratch_shapes=list(comm.sems))(*comm.operands)


def _pair_sems(*shape):
    return [pltpu.SemaphoreType.DMA(shape), pltpu.SemaphoreType.DMA(shape)]


def _gather_comm(bufs, layer, stage):
    n = len(bufs)

    def copies(ins, outs, sems):
        x, y, c = _mesh_pos()
        me = 2 * x + y
        pairs = []
        for i in range(n):
            h = bufs[i].shape[2] // 2
            mine, other = pl.ds(c * h, h), pl.ds((1 - c) * h, h)
            for r, (px, py, k) in enumerate(_peers(x, y)):
                if stage == 0:
                    send = _remote(ins[i].at[layer, me, mine, :], outs[i].at[layer, me, mine, :], *sems, (i, r), (px, py, c))
                    land = outs[i].at[layer, k, mine, :]
                    recv = _remote(land, land, *sems, (i, r), (px, py, c))
                else:
                    send = _remote(ins[i].at[layer, k, mine, :], outs[i].at[layer, k, mine, :], *sems, (i, r), (x, y, 1 - c))
                    land = outs[i].at[layer, k, other, :]
                    recv = _remote(land, land, *sems, (i, r), (x, y, 1 - c))
                pairs.append((send, recv))
        return pairs

    return _Comm(bufs, [jax.ShapeDtypeStruct(b.shape, b.dtype) for b in bufs], {i: i for i in range(n)},
                 _pair_sems(n, 3), copies)


def _reduce_sibling_comm(gs):
    n = len(gs)

    def copies(ins, outs, sems):
        x, y, c = _mesh_pos()
        pairs = []
        for i in range(n):
            h = gs[i].shape[1] // 2
            cp = _remote(ins[i].at[:, pl.ds((1 - c) * h, h), :], outs[i], *sems, i, (x, y, 1 - c))
            pairs.append((cp, cp))
        return pairs

    return _Comm(gs, [jax.ShapeDtypeStruct((g.shape[0], g.shape[1] // 2, g.shape[2]), g.dtype) for g in gs], {},
                 _pair_sems(n), copies)


def _reduce_chips_comm(ps):
    n = len(ps)

    def copies(ins, outs, sems):
        x, y, c = _mesh_pos()
        pairs = []
        for i in range(n):
            for r, (px, py, k) in enumerate(_peers(x, y)):
                cp = _remote(ins[i].at[k], outs[i].at[r], *sems, (i, r), (px, py, c))
                pairs.append((cp, cp))
        return pairs

    return _Comm(ps, [jax.ShapeDtypeStruct((3,) + p.shape[1:], p.dtype) for p in ps], {}, _pair_sems(n, 3), copies)


def _share_halves(bufs, *, name):
    n = len(bufs)

    def body(*refs):
        ins, outs = refs[:n], refs[n:2 * n]
        send_sems, recv_sems = refs[2 * n:]
        x, y, c = _mesh_pos()
        cps = []
        for i in range(n):
            h = bufs[i].shape[1] // 2
            mine = pl.ds(c * h, h)
            cp = _remote(ins[i].at[:, mine, :], outs[i].at[:, mine, :], send_sems, recv_sems, i, (x, y, 1 - c))
            cp.start()
            cps.append(cp)
        for cp in cps:
            cp.wait()

    return pl.pallas_call(
        body, name=name, in_specs=[ANY] * n, out_specs=[ANY] * n,
        out_shape=[jax.ShapeDtypeStruct(b.shape, b.dtype) for b in bufs],
        input_output_aliases={i: i for i in range(n)},
        scratch_shapes=[pltpu.SemaphoreType.DMA((n,)), pltpu.SemaphoreType.DMA((n,))])(*bufs)


def _add_own_half(full, recv, pos, *, name, rows=256):
    k4, h, n = recv.shape
    tr = _tile(h, rows, 16)
    nblk = h // tr

    def body(pos_ref, a_ref, b_ref, o_ref):
        o_ref[...] = (a_ref[...] + b_ref[...]).astype(BF16)

    grid_spec = pltpu.PrefetchScalarGridSpec(
        num_scalar_prefetch=1, grid=(k4, nblk),
        in_specs=[pl.BlockSpec((None, tr, n), lambda k, i, pos_ref: (k, pos_ref[1] * nblk + i, 0)),
                  pl.BlockSpec((None, tr, n), lambda k, i, pos_ref: (k, i, 0))],
        out_specs=pl.BlockSpec((None, tr, n), lambda k, i, pos_ref: (k, i, 0)))
    return pl.pallas_call(body, name=name, grid_spec=grid_spec, out_shape=jax.ShapeDtypeStruct(recv.shape, BF16),
                          compiler_params=_params(("parallel", "parallel")))(pos, full, recv)


def _sum_into(own, others, buf, pos, layer, *, name, rows=256):
    _, h, n = own.shape
    tr = _tile(h, rows, 16)
    nblk = h // tr

    def body(pos_ref, own_ref, oth_ref, _, o_ref):
        acc = own_ref[...].astype(F32)
        for r in range(3):
            acc = acc + oth_ref[r].astype(F32)
        o_ref[...] = acc

    grid_spec = pltpu.PrefetchScalarGridSpec(
        num_scalar_prefetch=1, grid=(nblk,),
        in_specs=[pl.BlockSpec((None, tr, n), lambda i, pos_ref: (pos_ref[0], i, 0)),
                  pl.BlockSpec((3, tr, n), lambda i, pos_ref: (0, i, 0)), ANY],
        out_specs=pl.BlockSpec((None, tr, n), lambda i, pos_ref: (layer, pos_ref[1] * nblk + i, 0)))
    return pl.pallas_call(body, name=name, grid_spec=grid_spec, out_shape=jax.ShapeDtypeStruct(buf.shape, F32),
                          input_output_aliases={3: 0}, compiler_params=_params(("parallel",)))(pos, own, others, buf)


def _pair_comm(buf):
    def copies(ins, outs, sems):
        x, y, c = _mesh_pos()
        land = outs[0].at[1 - c]
        return [(_remote(ins[0].at[c], outs[0].at[c], *sems, 0, (x, y, 1 - c)),
                 _remote(land, land, *sems, 0, (x, y, 1 - c)))]

    return _Comm([buf], [jax.ShapeDtypeStruct(buf.shape, buf.dtype)], {0: 0}, _pair_sems(1), copies)


def _quad_comm(buf):
    def copies(ins, outs, sems):
        x, y, c = _mesh_pos()
        me = 2 * x + y
        pairs = []
        for r, (px, py, k) in enumerate(_peers(x, y)):
            land = outs[0].at[k]
            pairs.append((_remote(ins[0].at[me], outs[0].at[me], *sems, r, (px, py, c)),
                          _remote(land, land, *sems, r, (px, py, c))))
        return pairs

    return _Comm([buf], [jax.ShapeDtypeStruct(buf.shape, buf.dtype)], {0: 0}, _pair_sems(3), copies)


def _chip_bcast(buf, *, name):
    def body(src_ref, out_ref, send_sems, recv_sems, local_sem):
        x, y, c = _mesh_pos()
        me = 2 * x + y
        local = pltpu.make_async_copy(src_ref, out_ref.at[me], local_sem)
        local.start()
        sends = []
        for r, (px, py, _) in enumerate(_peers(x, y)):
            cp = _remote(src_ref, out_ref.at[me], send_sems, recv_sems, r, (px, py, c))
            cp.start()
            sends.append(cp)
        for r, (px, py, k) in enumerate(_peers(x, y)):
            _remote(src_ref, out_ref.at[k], send_sems, recv_sems, r, (px, py, c)).wait_recv()
        for cp in sends:
            cp.wait_send()
        local.wait()

    return pl.pallas_call(
        body, name=name, in_specs=[ANY], out_specs=ANY, out_shape=jax.ShapeDtypeStruct((4,) + buf.shape, buf.dtype),
        scratch_shapes=[pltpu.SemaphoreType.DMA((3,)), pltpu.SemaphoreType.DMA((3,)), pltpu.SemaphoreType.DMA])(buf)


def _sum_slots(buf, *, name, rows=384):
    r, n = buf.shape[-2:]
    k = int(np.prod(buf.shape[:-2]))
    tr = _tile(r, rows, SUBLANES)

    def body(b_ref, o_ref):
        acc = b_ref[0]
        for s in range(1, k):
            acc = acc + b_ref[s]
        o_ref[...] = acc

    return pl.pallas_call(
        body, name=name, grid=(r // tr,), in_specs=[pl.BlockSpec((k, tr, n), lambda i: (0, i, 0))],
        out_specs=pl.BlockSpec((tr, n), lambda i: (i, 0)), out_shape=jax.ShapeDtypeStruct((r, n), F32),
        compiler_params=_params(("parallel",)))(buf.reshape((k, r, n)))


ROW = 1024
BIG = (("w_in", 2), ("w_out", 1), ("w_cq", 1), ("w_ck", 1), ("w_cv", 1), ("w_co", 2), ("w_up", 2), ("w_down", 1))
CONV = ("lru_conv_w", "ffn_conv_w")
REPLICATED = ("norm_mix_g", "b_forget", "lru_conv_b", "lru_w_a", "lru_b_a", "lru_w_x", "lru_b_x", "lru_lambda",
              "norm_cross_g", "norm_mem_g", "norm_ffn_g", "ffn_conv_b", "rel_bias", "final_norm_g")
WEIGHTS = ('norm_mix_g', 'w_in', 'b_forget', 'lru_conv_w', 'lru_conv_b', 'lru_w_a', 'lru_b_a', 'lru_w_x', 'lru_b_x',
           'lru_lambda', 'w_out', 'norm_cross_g', 'norm_mem_g', 'w_cq', 'w_ck', 'w_cv', 'w_co', 'norm_ffn_g', 'w_up',
           'ffn_conv_w', 'ffn_conv_b', 'w_down', 'rel_bias', 'final_norm_g')
INPUTS = ("x", "mem") + WEIGHTS + ("loss_target",) + tuple("m_" + n for n in WEIGHTS) + tuple("v_" + n for n in WEIGHTS)


def _round_up(n, m):
    return -(-n // m) * m


class _Packing:
    def __init__(self, entries):
        self.entries, self.off = entries, {}
        o = 0
        for name, shape in entries:
            self.off[name] = o
            o += _round_up(int(np.prod(shape)), ROW)
        self.used = o
        self.rows = _round_up(o // ROW, SUBLANES)

    def pack(self, arrays):
        parts = []
        for name, shape in self.entries:
            n = int(np.prod(shape))
            parts.append(jnp.pad(arrays[name].reshape(n), (0, _round_up(n, ROW) - n)))
        tail = self.rows * ROW - self.used
        if tail:
            parts.append(jnp.zeros((tail,), F32))
        return jnp.concatenate(parts).reshape(self.rows, ROW)

    def unpack(self, flat, lead=()):
        out = {}
        for name, shape in self.entries:
            n = int(np.prod(shape))
            r0, nr = self.off[name] // ROW, _round_up(n, ROW) // ROW
            rows = lax.slice_in_dim(flat, r0, r0 + nr, axis=len(lead)).reshape(lead + (nr * ROW,))
            out[name] = lax.slice_in_dim(rows, 0, n, axis=len(lead)).reshape(lead + tuple(shape))
        return out


def _to_shards(g, axis):
    r, c = g.shape
    if axis == 1:
        return g.reshape(4, r // 4, c)
    return g.reshape(r, 4, c // 4).transpose(1, 0, 2)


def _from_shards(s, axis):
    _, r, c = s.shape
    if axis == 1:
        return s.reshape(4 * r, c)
    return s.transpose(1, 0, 2).reshape(r, 4 * c)


def _proj_blocks():
    blocks = []
    for mixer in (MIX_SB, MIX_FOX, MIX_DIL):
        for p in range(2):
            blocks += [ORIG_COL[mixer] + part * 2 * LANES + p * LANES for part in range(3)]
    for hf in range(2):
        blocks += [ORIG_LRU_X + hf * LANES, ORIG_LRU_G + hf * LANES]
    return blocks


def _pad_w_in(w):
    parts = [w[..., s:s + LANES] for s in _proj_blocks()]
    parts += [w[..., 1536:1540], jnp.zeros(w.shape[:-1] + (PROJ_W - COL_F - N_HEADS,), w.dtype)]
    return jnp.concatenate(parts, axis=-1)


def _unpad_w_in(wp):
    blocks = _proj_blocks()
    order = sorted(range(len(blocks)), key=lambda i: blocks[i])
    parts = []
    for i in order:
        if blocks[i] == ORIG_COL[MIX_DIL]:
            parts.append(wp[..., COL_F:COL_F + N_HEADS])
        parts.append(wp[..., i * LANES:(i + 1) * LANES])
    return jnp.concatenate(parts, axis=-1)


def _block_diag(w):
    z = jnp.zeros((HEAD_DIM, HEAD_DIM), w.dtype)
    half = lambda a, b: jnp.concatenate([jnp.concatenate([a, z], 1), jnp.concatenate([z, b], 1)], 0)
    return jnp.stack([half(w[0], w[1]), half(w[2], w[3])])


def _block_diag_grad(d):
    return jnp.stack([d[0, :HEAD_DIM, :HEAD_DIM], d[0, HEAD_DIM:, HEAD_DIM:],
                      d[1, :HEAD_DIM, :HEAD_DIM], d[1, HEAD_DIM:, HEAD_DIM:]])


def _fox_layouts(cum, nb, seq):
    tk = min(ATT_TILE, seq)
    col = cum.reshape(nb, 2, 2, seq).transpose(0, 1, 3, 2)
    row = cum.reshape(nb, 2, 2, seq // tk, tk).transpose(0, 1, 3, 2, 4)
    return col, row


def _layer_params(w, l, nb):
    lru_vec = jnp.concatenate([w["lru_conv_w"][l], w["lru_conv_b"][l][None], w["lru_b_a"][l][None],
                               w["lru_b_x"][l][None], w["lru_lambda"][l][None]], axis=0)
    ffn_cw = jnp.concatenate([w["ffn_conv_w"][l], w["ffn_conv_b"][l][None],
                              jnp.zeros((SUBLANES - 4, 2 * D_FF), F32)], axis=0)
    return dict(
        w_in=w["w_in_padded"][l], lru_vec=lru_vec,
        wa=_block_diag(w["lru_w_a"][l]).astype(BF16), wx=_block_diag(w["lru_w_x"][l]).astype(BF16),
        ffn_cw=ffn_cw, b_rows=jnp.tile(w["b_forget"][l], nb).reshape(nb * N_HEADS, 1))


NORM_ROWS = 512


def _merge_comms(comms):
    if len(comms) == 1:
        return comms[0]
    operands, out_shape, aliases, sems, spans = [], [], {}, [], []
    for cm in comms:
        aliases.update({len(operands) + i: len(out_shape) + j for i, j in cm.aliases.items()})
        spans.append((len(operands), len(out_shape), len(sems)))
        operands += list(cm.operands)
        out_shape += list(cm.out_shape)
        sems += list(cm.sems)

    def copies(ins, outs, sm):
        pairs = []
        for cm, (i0, o0, s0) in zip(comms, spans):
            pairs += cm.copies(ins[i0:i0 + len(cm.operands)], outs[o0:o0 + len(cm.out_shape)], sm[s0:s0 + len(cm.sems)])
        return pairs

    return _Comm(operands, out_shape, aliases, sems, copies)


GATHER_FIRST = ("w_in",)
GATHER_MID = ("w_out", "w_cq", "w_ck", "w_cv", "w_co")
GATHER_LAST = ("w_up", "w_down")


class _WeightGather:
    def __init__(self, slots, w, depth):
        self.slots, self.w, self.depth = slots, w, depth

    def plan(self, l, key):
        nxt = l + 1 if l + 1 < self.depth else None
        early = GATHER_FIRST + GATHER_MID
        if l == 0:
            table = {"proj": [(GATHER_MID, 0, 0)],
                     "sb_fwd": [(GATHER_MID, 0, 1), (GATHER_LAST, 0, 0)],
                     "fox_fwd": [(GATHER_LAST, 0, 1)] + ([(early, nxt, 0)] if nxt else []),
                     "dil_fwd": [(early, nxt, 1), (GATHER_LAST, nxt, 0)] if nxt else [],
                     "out": [(GATHER_LAST, nxt, 1)] if nxt else []}
        else:
            everything = early + GATHER_LAST
            table = {"sb_fwd": [(everything, nxt, 0)], "fox_fwd": [(everything, nxt, 1)]} if nxt else {}
        return table.get(key, [])

    def comm(self, l, key):
        entries = self.plan(l, key)
        if not entries:
            return None
        return _merge_comms([_gather_comm([self.slots[n] for n in names], layer, stage)
                             for names, layer, stage in entries])

    def done(self, l, key, landed):
        landed = list(landed)
        for names, layer, stage in self.plan(l, key):
            for n in names:
                self.slots[n] = landed.pop(0)
            if stage == 1:
                self.take(names, layer)

    def take(self, names, layer):
        for n, axis in BIG:
            if n in names:
                self.w[n][layer] = _from_shards(self.slots[n][layer], axis)
        if "w_in" in names:
            self.w["w_in_padded"][layer] = _pad_w_in(self.w["w_in"][layer])

    def first_comm(self, stage):
        return _gather_comm([self.slots[n] for n in GATHER_FIRST], 0, stage)

    def first_done(self, stage, landed):
        self.slots.update(zip(GATHER_FIRST, landed))
        if stage == 1:
            self.take(GATHER_FIRST, 0)


def _layer_fwd(x, h, mem, w, lp, l, next_g, bias, nb, gather):
    t, d = x.shape
    seq = t // nb
    tag = f"l{l}"
    sv = dict(x0=x)

    def carrying(key, fn):
        comm = gather.comm(l, key)
        res = fn(comm)
        if comm is not None:
            res, landed = res
            gather.done(l, key, landed)
        return res

    proj = carrying("proj", lambda cm: _mm(h, lp["w_in"], name=tag + "_proj", comm=cm))
    mixed, ltot = carrying("sb_fwd", lambda cm: _sb_attn_fwd(proj, lax.empty((t, d), F32), nb=nb,
                                                             name=tag + "_sb_fwd", comm=cm))
    f_rows = proj[:, COL_F:COL_F + N_HEADS].reshape(nb, seq, N_HEADS).transpose(0, 2, 1).reshape(nb * N_HEADS, seq)
    cum_col, cum_row = _fox_layouts(_fox_gate_fwd(f_rows, lp["b_rows"]), nb, seq)
    mixed, lse_fox = carrying("fox_fwd", lambda cm: _softmax_attn_fwd(
        proj, nb=nb, mode="fox", mixer=MIX_FOX, out_buf=mixed, extra=(cum_col, cum_row), name=tag + "_fox_fwd", comm=cm))
    mixed, lse_dil = carrying("dil_fwd", lambda cm: _softmax_attn_fwd(
        proj, nb=nb, mode="dil", mixer=MIX_DIL, out_buf=mixed, extra=(bias,), name=tag + "_dil_fwd", comm=cm))
    mixed = _lru_fwd(proj, lp["lru_vec"], lp["wa"], lp["wx"], mixed, nb=nb, name=tag + "_lru_fwd")
    x1, hq = carrying("out", lambda cm: _mm(mixed, w["w_out"][l], res=x, norm_g=w["norm_cross_g"][l], ti=NORM_ROWS,
                                            name=tag + "_out", comm=cm))
    memn = _rmsnorm(mem, w["norm_mem_g"][l], name=tag + "_norm_mem")
    q = _mm(hq, w["w_cq"][l], name=tag + "_cq")
    k = _mm(memn, w["w_ck"][l], name=tag + "_ck")
    v = _mm(memn, w["w_cv"][l], name=tag + "_cv")
    oc, lse_c = _softmax_attn_fwd((q, k, v), nb=nb, mode="cross", name=tag + "_cross_fwd")
    x2, hn = _mm(oc, w["w_co"][l], res=x1, norm_g=w["norm_ffn_g"][l], ti=NORM_ROWS, name=tag + "_co")
    hf = _mm(hn, w["w_up"][l], name=tag + "_up")
    act = _ffn_act(hf, lp["ffn_cw"], seq=seq, name=tag + "_ffn_act")
    if next_g is None:
        x3, h_next = _mm(act, w["w_down"][l], res=x2, name=tag + "_down"), None
    else:
        x3, h_next = _mm(act, w["w_down"][l], res=x2, norm_g=next_g, ti=NORM_ROWS, name=tag + "_down")
    sv.update(h=h, proj=proj, ltot=ltot, f_rows=f_rows, cum_col=cum_col, cum_row=cum_row, lse_fox=lse_fox,
              lse_dil=lse_dil, mixed=mixed, x1=x1, hq=hq, memn=memn, q=q, k=k, v=v, oc=oc, lse_c=lse_c, x2=x2,
              hn=hn, hf=hf, act=act)
    return x3, h_next, sv


class _PendingReduce:
    def __init__(self, full, pos, layer):
        self.names, self.full, self.pos, self.layer = list(full), list(full.values()), pos, layer

    def sibling_comm(self):
        return _reduce_sibling_comm(self.full)

    def add(self, from_sibling):
        self.partial = [_add_own_half(f, r, self.pos, name=f"l{self.layer}_reduce_add_{n}")
                        for f, r, n in zip(self.full, from_sibling, self.names)]

    def chips_comm(self):
        return _reduce_chips_comm(self.partial)

    def finish(self, others, g_shard):
        g_shard = dict(g_shard)
        for p, o, n in zip(self.partial, others, self.names):
            g_shard[n] = _sum_into(p, o, g_shard[n], self.pos, self.layer, name=f"l{self.layer}_reduce_sum_{n}")
        return g_shard


def _layer_bwd(dx3, mem, sv, w, lp, l, bias, nb, pos, pending=None, g_shard=None, reduce_early=False):
    t = dx3.shape[0]
    seq = t // nb
    tag = f"l{l}"
    g = {}
    down_rows = _tile(sv["act"].shape[1], 1408)
    if pending is None:
        g["w_down"] = _mm(sv["act"], dx3, ta=True, ti=down_rows, name=tag + "_dw_down")
    else:
        g["w_down"], from_sibling = _mm(sv["act"], dx3, ta=True, ti=down_rows, comm=pending.sibling_comm(),
                                        name=tag + "_dw_down")
        pending.add(from_sibling)
    dact = _mm(dx3, w["w_down"][l], tb=True, name=tag + "_dact")
    if pending is None:
        dhf, dcu, dcg = _ffn_bwd(sv["hf"], lp["ffn_cw"], dact, seq=seq, name=tag + "_ffn_bwd")
    else:
        (dhf, dcu, dcg), others = _ffn_bwd(sv["hf"], lp["ffn_cw"], dact, seq=seq, name=tag + "_ffn_bwd",
                                           comm=pending.chips_comm())
        g_shard = pending.finish(others, g_shard)
    dcw = jnp.concatenate([dcu, dcg], axis=1)
    g["ffn_conv_w"], g["ffn_conv_b"] = dcw[:3], dcw[3]
    g["w_up"] = _mm(sv["hn"], dhf, ta=True, halves="b", col_shards=4, name=tag + "_dw_up")
    early = _PendingReduce(_big_grad_shards(g, EARLY), pos, l) if reduce_early else None
    res = _mm(dhf, w["w_up"][l], tb=True, halves="a", norm_bwd=(sv["x2"], w["norm_ffn_g"][l], dx3), ti=NORM_ROWS,
              comm=early.sibling_comm() if early else None, name=tag + "_dhn")
    if early:
        res, from_sibling = res
        early.add(from_sibling)
    dx2, dg = res
    g["norm_ffn_g"] = dg.reshape(-1)
    g["w_co"] = _mm(sv["oc"], dx2, ta=True, col_shards=4, name=tag + "_dw_co")
    doc = _mm(dx2, w["w_co"][l], tb=True, name=tag + "_doc")
    dq, dk, dv = _softmax_attn_bwd((sv["q"], sv["k"], sv["v"]), sv["oc"], sv["lse_c"], doc, nb=nb, mode="cross",
                                   name=tag + "_cross_bwd")
    g["w_cq"] = _mm(sv["hq"], dq, ta=True, name=tag + "_dw_cq")
    g["w_ck"] = _mm(sv["memn"], dk, ta=True, name=tag + "_dw_ck")
    g["w_cv"] = _mm(sv["memn"], dv, ta=True, name=tag + "_dw_cv")
    dx1, dg = _mm(dq, w["w_cq"][l], tb=True, norm_bwd=(sv["x1"], w["norm_cross_g"][l], dx2), ti=NORM_ROWS,
                  name=tag + "_dhq")
    g["norm_cross_g"] = dg.reshape(-1)
    dmemn = _mm(dv, w["w_cv"][l], tb=True, res=_mm(dk, w["w_ck"][l], tb=True, name=tag + "_dmem_k"),
                name=tag + "_dmem_v")
    _, g["norm_mem_g"] = _rmsnorm_bwd(dmemn, mem, w["norm_mem_g"][l], None, name=tag + "_norm_mem_bwd")
    mixed, proj = sv["mixed"], sv["proj"]
    g["w_out"] = _mm(mixed, dx1, ta=True, name=tag + "_dw_out")
    mid = _PendingReduce(_big_grad_shards(g, GATHER_MID), pos, l) if reduce_early else None
    dmixed = _mm(dx1, w["w_out"][l], tb=True, name=tag + "_dmixed", comm=mid.sibling_comm() if mid else None)
    if mid:
        dmixed, from_sibling = dmixed
        mid.add(from_sibling)
    dproj = _sb_attn_bwd(proj, sv["ltot"], dmixed, lax.empty((t, PROJ_W), F32), nb=nb, name=tag + "_sb_bwd",
                         comm=early.chips_comm() if early else None)
    if early:
        dproj, others = dproj
        g_shard = early.finish(others, g_shard)
    res = _softmax_attn_bwd(
        proj, mixed, sv["lse_fox"], dmixed, nb=nb, mode="fox", mixer=MIX_FOX, dbuf=dproj,
        extra=(sv["cum_col"], sv["cum_row"]), name=tag + "_fox_bwd", comm=mid.chips_comm() if mid else None)
    if mid:
        res, others = res
        g_shard = mid.finish(others, g_shard)
    dproj, dcum_k, dcum_q = res
    dcum = (dcum_k.transpose(0, 1, 3, 2, 4).reshape(nb * N_HEADS, seq)
            + dcum_q.transpose(0, 1, 3, 2).reshape(nb * N_HEADS, seq))
    df_rows, db = _fox_gate_bwd(dcum, sv["f_rows"], lp["b_rows"])
    g["b_forget"] = db[:N_HEADS, 0]
    df = df_rows.reshape(nb, N_HEADS, seq).transpose(0, 2, 1).reshape(t, N_HEADS)
    dproj, dbias = _softmax_attn_bwd(proj, mixed, sv["lse_dil"], dmixed, nb=nb, mode="dil", mixer=MIX_DIL,
                                     dbuf=dproj, extra=(bias,), name=tag + "_dil_bwd")
    dproj, dvec, dwa, dwx = _lru_bwd(proj, lp["lru_vec"], lp["wa"], lp["wx"], dmixed, dproj, nb=nb,
                                     name=tag + "_lru_bwd")
    g["lru_conv_w"], g["lru_conv_b"], g["lru_b_a"], g["lru_b_x"], g["lru_lambda"] = (
        dvec[0:4], dvec[4], dvec[5], dvec[6], dvec[7])
    g["lru_w_a"], g["lru_w_x"] = _block_diag_grad(dwa), _block_diag_grad(dwx)
    dproj = lax.dynamic_update_slice(dproj, jnp.pad(df, ((0, 0), (0, PROJ_W - COL_F - N_HEADS))), (0, COL_F))
    g["w_in_padded"] = _mm(sv["h"], dproj, ta=True, name=tag + "_dw_in")
    dx0, dg = _mm(dproj, lp["w_in"], tb=True, norm_bwd=(sv["x0"], w["norm_mix_g"][l], dx1), ti=NORM_ROWS,
                  name=tag + "_dh")
    g["norm_mix_g"] = dg.reshape(-1)
    return dx0, g, dbias, g_shard


def _big_grad_shards(g, names):
    out = {}
    for n, axis in BIG:
        if n not in names:
            continue
        if n in ("w_up", "w_co"):
            out[n] = g[n]
        else:
            out[n] = _to_shards(_unpad_w_in(g["w_in_padded"]) if n == "w_in" else g[n], axis)
    return out


EARLY = ("w_down", "w_up")


def kernel(*args):
    a = dict(zip(INPUTS, args, strict=True))
    nb, seq, d = a["x"].shape
    depth = a["norm_mix_g"].shape[0]
    x = a["x"].reshape(nb * seq, d)
    mem = a["mem"].reshape(nb * a["mem"].shape[1], d)
    target = a["loss_target"].reshape(nb * seq, d)
    cx, cy, c = _mesh_pos()
    chip = 2 * cx + cy
    pos = jnp.stack([chip, c]).astype(jnp.int32)

    slots = {}
    for n, _ in BIG:
        own = a[n].astype(BF16)[:, None]
        slots[n] = lax.dynamic_update_slice(lax.empty((depth, 4) + own.shape[2:], BF16), own, (0, chip, 0, 0))
    w = {n: a[n] for n in REPLICATED}
    w.update({n: {} for n, _ in BIG}, w_in_padded={})
    gather = _WeightGather(slots, w, depth)
    bias, landed = _dil_bias(w["rel_bias"], seq, comm=gather.first_comm(0))
    gather.first_done(0, landed)
    h, landed = _rmsnorm(x, w["norm_mix_g"][0], name="l0_norm_mix", comm=gather.first_comm(1))
    gather.first_done(1, landed)
    cpk = _Packing([(n, a[n].shape) for n in CONV])
    conv = cpk.unpack(_chip_bcast(cpk.pack({n: a[n] for n in CONV}), name="gather_conv"), lead=(4,))
    for n in CONV:
        w[n] = jnp.moveaxis(conv[n], 0, 2).reshape(a[n].shape[:2] + (4 * a[n].shape[2],))

    lps, saved = [], []
    for l in range(depth):
        lps.append(_layer_params(w, l, nb))
        x, h, sv = _layer_fwd(x, h, mem, w, lps[l], l, w["norm_mix_g"][l + 1] if l + 1 < depth else None, bias, nb,
                              gather)
        saved.append(sv)
    loss, dx, dg_final = _loss_head(x, w["final_norm_g"], target)
    small_g = [None] * depth
    dbias, pending = None, None
    g_shard = {n: lax.empty(a[n].shape, F32) for n, _ in BIG}
    for l in reversed(range(depth)):
        bottom = l == 0
        dx, g, db, g_shard = _layer_bwd(dx, mem, saved[l], w, lps[l], l, bias, nb, pos, pending=pending,
                                        g_shard=g_shard, reduce_early=bottom)
        dbias = db if dbias is None else dbias + db
        small_g[l] = g
        left = [n for n, _ in BIG if not (bottom and n in EARLY + GATHER_MID)]
        pending = _PendingReduce(_big_grad_shards(g, left), pos, l)

    grads = {n: jnp.stack([small_g[l][n] for l in range(depth)]) for n in REPLICATED + CONV
             if n not in ("rel_bias", "final_norm_g")}
    grads["rel_bias"] = _dil_bias_bwd(dbias, seq)
    grads["final_norm_g"] = dg_final
    grads["loss"] = loss.reshape(1)
    spk = _Packing([(n, grads[n].shape) for n in REPLICATED + CONV + ("loss",)])
    s_flat = spk.pack(grads)
    pair = lax.dynamic_update_slice(lax.empty((2,) + s_flat.shape, F32), s_flat[None], (c, 0, 0))
    *from_sibling, pair = _run_comm(_merge_comms([pending.sibling_comm(), _pair_comm(pair)]), name="tail_sibling")
    pending.add(from_sibling)
    quad = lax.dynamic_update_slice(lax.empty((4,) + pair.shape, F32), pair[None], (chip, 0, 0, 0))
    *others, quad = _run_comm(_merge_comms([pending.chips_comm(), _quad_comm(quad)]), name="tail_chips")
    g_shard = pending.finish(others, g_shard)
    names = [n for n, _ in BIG]
    g_shard = dict(zip(names, _share_halves([g_shard[n] for n in names], name="reduce_share")))
    out = {}
    for n in names:
        operands = (a[n], g_shard[n], a["m_" + n], a["v_" + n])
        cols = a[n].shape[2]
        if cols % LANES:
            lead = max(b for b in range(1, 65) if cols % b == 0)
            res = _adamw(*[z.transpose(2, 0, 1) for z in operands], name="adamw_" + n, lead=lead)
            delta, new_m, new_v = [z.transpose(1, 2, 0) for z in res]
        else:
            delta, new_m, new_v = _adamw(*operands, name="adamw_" + n)
        out[n] = (g_shard[n], delta, new_m, new_v)
    total = spk.unpack(_sum_slots(quad, name="small_sum"))
    for n in CONV:
        width = a[n].shape[2]
        total[n] = lax.dynamic_slice_in_dim(total[n], chip * width, width, axis=2)
    apk = _Packing([(n, a[n].shape) for n in REPLICATED + CONV])
    s_out = _adamw(*[apk.pack(src)[None] for src in (
        {n: a[n] for n in REPLICATED + CONV}, total, {n: a["m_" + n] for n in REPLICATED + CONV},
        {n: a["v_" + n] for n in REPLICATED + CONV})], name="adamw_small")
    s_delta, s_m, s_v = [apk.unpack(o[0]) for o in s_out]
    for n in REPLICATED + CONV:
        out[n] = (total[n], s_delta[n], s_m[n], s_v[n])

    return (total["loss"].reshape(()), dx.reshape(nb, seq, d), *[out[n][0] for n in WEIGHTS],
            *[out[n][1] for n in WEIGHTS], *[out[n][2] for n in WEIGHTS], *[out[n][3] for n in WEIGHTS])
```

```python
import math

import numpy as np
import jax
import jax.numpy as jnp
from jax import lax
from jax.experimental import pallas as pl
from jax.experimental.pallas import tpu as pltpu

F32 = jnp.float32
BF16 = jnp.bfloat16

HEAD_DIM = 64
N_HEADS = 4
N_IN = 2820
D_FF = 2816
LRU_C = 8.0
EPS = 1e-6
NUM_BUCKETS = 32
MAX_DISTANCE = 2048
DILATED_PATTERNS = ((128, 1), (512, 4), (2048, 16))
ADAM_LR, ADAM_B1, ADAM_B2, ADAM_EPS, ADAM_WD, ADAM_STEP = 0.001, 0.9, 0.999, 1e-08, 0.01, 10

LANES = 128
SUBLANES = 8
VMEM_LIMIT = 48 * 1024 * 1024

PROJ_W = 3072
PAIR_W = 3 * LANES
LRU_W = 2 * LANES
COL_LRU = 6 * PAIR_W
COL_F = COL_LRU + 2 * LRU_W
MIX_SB, MIX_FOX, MIX_DIL, MIX_LRU = 0, 1, 2, 3
ORIG_COL = {MIX_SB: 0, MIX_FOX: 768, MIX_DIL: 1540}
ORIG_LRU_X, ORIG_LRU_G = 2308, 2564

ATT_TILE = 256
MASKED = -1e30
SCALE = HEAD_DIM ** -0.5

NT_DIMS = (((1,), (1,)), ((), ()))
TN_DIMS = (((0,), (0,)), ((), ()))

MESH = pl.DeviceIdType.MESH
ANY = pl.BlockSpec(memory_space=pl.ANY)


def _params(sem):
    return pltpu.CompilerParams(dimension_semantics=sem, vmem_limit_bytes=VMEM_LIMIT)


def _tile(n, target, unit=LANES):
    if n <= target:
        return n
    t = (target // unit) * unit
    while t > unit and n % t:
        t -= unit
    assert n % t == 0, (n, target, unit)
    return t


def _mm(a, b, *, ta=False, tb=False, res=None, col_shards=1, halves=None, b_chunks=1, norm_g=None, norm_bwd=None,
        comm=None, name, ti=1024, tj=1408, tc=1408):
    if halves == "a":
        m, kc = a.shape[1], 2 * a.shape[2]
    else:
        m, kc = (a.shape[1], a.shape[0]) if ta else a.shape
    n_blk, k_blk = None, kc // 2 if halves == "a" else kc
    if halves == "b":
        n = 2 * b.shape[2]
        assert b.shape[1] == kc
        n_blk = n // 2
    elif b_chunks > 1:
        n = b.shape[1] if tb else b_chunks * b.shape[2]
        assert (b_chunks * b.shape[2] if tb else b.shape[1]) == kc
        if tb:
            k_blk = min(k_blk, kc // b_chunks)
        else:
            n_blk = n // b_chunks
    else:
        n = b.shape[0] if tb else b.shape[1]
        assert (b.shape[1] if tb else b.shape[0]) == kc
    assert n % col_shards == 0
    n_blk = min(n_blk or n, n // col_shards)
    ti, tj, tc = (_tile(m, ti, LANES if ta else SUBLANES), _tile(n_blk, tj),
                  _tile(k_blk, tc, SUBLANES if ta and tb else LANES))
    per_shard, per_half_j, per_half_k = n // col_shards // tj, n // 2 // tj, kc // 2 // tc
    per_chunk = (kc if tb else n) // b_chunks // (tc if tb else tj)
    nk = kc // tc
    dims = (((0 if ta else 1,), (1 if tb else 0,)), ((), ()))
    rows_whole = norm_g is not None or norm_bwd is not None
    assert not rows_whole or (tj == n and col_shards == 1)
    n_extra = (res is not None) + (norm_g is not None) + (3 if norm_bwd is not None else 0)
    n_out = 2 if rows_whole else 1

    def finish(val, ex, outs):
        if res is not None:
            val = ex[0][...] + val
        if norm_g is not None:
            outs[0][...] = val
            outs[1][...] = (_xhat(val) * ex[-1][...]).astype(BF16)
        elif norm_bwd is not None:
            x_ref, g_ref, r_ref = ex[-3:]
            dx, dgr = _norm_bwd_rows(val, x_ref[...], g_ref[...])
            outs[0][...] = r_ref[...] + dx

            @pl.when(pl.program_id(0) == 0)
            def _():
                outs[1][...] = jnp.zeros_like(outs[1])

            outs[1][...] += jnp.sum(dgr, axis=0, keepdims=True)
        else:
            outs[0][...] = val

    def body(*refs):
        a_ref, b_ref = refs[:2]
        ex = refs[2:2 + n_extra]
        outs = refs[2 + n_extra:2 + n_extra + n_out]
        part = lax.dot_general(a_ref[...].astype(BF16), b_ref[...].astype(BF16), dims, preferred_element_type=F32)
        if nk == 1:
            finish(part, ex, outs)
            return
        acc_ref = refs[-1]
        k = pl.program_id(2)

        @pl.when(k == 0)
        def _():
            acc_ref[...] = part

        @pl.when(k > 0)
        def _():
            acc_ref[...] += part

        @pl.when(k == nk - 1)
        def _():
            finish(acc_ref[...], ex, outs)

    if halves == "a":
        a_spec = pl.BlockSpec((None, ti, tc), lambda i, j, k: (k // per_half_k, i, k % per_half_k))
    elif ta:
        a_spec = pl.BlockSpec((tc, ti), lambda i, j, k: (k, i))
    else:
        a_spec = pl.BlockSpec((ti, tc), lambda i, j, k: (i, k))
    if halves == "b":
        b_spec = pl.BlockSpec((None, tc, tj), lambda i, j, k: (j // per_half_j, k, j % per_half_j))
    elif b_chunks > 1 and tb:
        b_spec = pl.BlockSpec((None, tj, tc), lambda i, j, k: (k // per_chunk, j, k % per_chunk))
    elif b_chunks > 1:
        b_spec = pl.BlockSpec((None, tc, tj), lambda i, j, k: (j // per_chunk, k, j % per_chunk))
    elif tb:
        b_spec = pl.BlockSpec((tj, tc), lambda i, j, k: (j, k))
    else:
        b_spec = pl.BlockSpec((tc, tj), lambda i, j, k: (k, j))
    o_spec = pl.BlockSpec((ti, tj), lambda i, j, k: (i, j))
    vec = pl.BlockSpec((1, tj), lambda i, j, k: (0, 0))
    in_specs, args = [a_spec, b_spec], [a, b]
    out_specs, out_shape = [o_spec], [jax.ShapeDtypeStruct((m, n), F32)]
    if res is not None:
        in_specs.append(o_spec)
        args.append(res)
    if norm_g is not None:
        in_specs.append(vec)
        args.append(norm_g.reshape(1, n))
        out_specs.append(o_spec)
        out_shape.append(jax.ShapeDtypeStruct((m, n), BF16))
    if norm_bwd is not None:
        x, g, dres = norm_bwd
        in_specs += [o_spec, vec, o_spec]
        args += [x, g.reshape(1, n), dres]
        out_specs.append(vec)
        out_shape.append(jax.ShapeDtypeStruct((1, n), F32))
    if col_shards > 1:
        assert n_extra == 0
        out_specs = [pl.BlockSpec((None, ti, tj), lambda i, j, k: (j // per_shard, i, j % per_shard))]
        out_shape = [jax.ShapeDtypeStruct((col_shards, m, n // col_shards), F32)]
    sem = ("arbitrary",) * 3 if norm_bwd is not None else ("parallel", "parallel", "arbitrary")
    out, carried = _pallas(body, name=name, grid=(m // ti, n // tj, nk), in_specs=in_specs, out_specs=out_specs,
                           out_shape=out_shape, args=args, scratch=[] if nk == 1 else [pltpu.VMEM((ti, tj), F32)],
                           sem=sem, comm=comm)
    out = out if rows_whole else out[0]
    return out if comm is None else (out, carried)


def _xhat(x):
    return x * lax.rsqrt(jnp.mean(x * x, axis=-1, keepdims=True) + EPS)


def _norm_bwd_rows(dy, x, g):
    rstd = lax.rsqrt(jnp.mean(x * x, axis=-1, keepdims=True) + EPS)
    xh = x * rstd
    dxh = dy * g
    dx = rstd * (dxh - xh * jnp.mean(dxh * xh, axis=-1, keepdims=True))
    return dx, dy * xh


def _rmsnorm(x, g, *, name, rows=512, comm=None):
    t, d = x.shape
    tr = _tile(t, rows, 2 * SUBLANES)

    def body(x_ref, g_ref, o_ref):
        o_ref[...] = (_xhat(x_ref[...]) * g_ref[...]).astype(BF16)

    out, carried = _pallas(
        body, name=name, grid=(t // tr,),
        in_specs=[pl.BlockSpec((tr, d), lambda i: (i, 0)), pl.BlockSpec((1, d), lambda i: (0, 0))],
        out_specs=[pl.BlockSpec((tr, d), lambda i: (i, 0))], out_shape=[jax.ShapeDtypeStruct((t, d), BF16)],
        args=[x, g.reshape(1, d)], sem=("parallel",), comm=comm)
    return out[0] if comm is None else (out[0], carried)


def _rmsnorm_bwd(dy, x, g, dres, *, name, rows=512):
    t, d = x.shape
    tr = _tile(t, rows, SUBLANES)

    def body(*refs):
        if dres is None:
            dy_ref, x_ref, g_ref, dx_ref, dg_ref = refs
        else:
            dy_ref, x_ref, g_ref, r_ref, dx_ref, dg_ref = refs
        dx, dgr = _norm_bwd_rows(dy_ref[...], x_ref[...], g_ref[...])
        dx_ref[...] = dx if dres is None else r_ref[...] + dx

        @pl.when(pl.program_id(0) == 0)
        def _():
            dg_ref[...] = jnp.zeros_like(dg_ref)

        dg_ref[...] += jnp.sum(dgr, axis=0, keepdims=True)

    row = pl.BlockSpec((tr, d), lambda i: (i, 0))
    vec = pl.BlockSpec((1, d), lambda i: (0, 0))
    in_specs = [row, row, vec] + ([] if dres is None else [row])
    args = (dy, x, g.reshape(1, d)) + (() if dres is None else (dres,))
    dx, dg = pl.pallas_call(
        body, name=name, grid=(t // tr,), in_specs=in_specs, out_specs=[row, vec],
        out_shape=[jax.ShapeDtypeStruct((t, d), F32), jax.ShapeDtypeStruct((1, d), F32)],
        compiler_params=_params(("arbitrary",)))(*args)
    return dx, dg.reshape(d)


def _loss_head(x, g, target, *, rows=512):
    t, d = x.shape
    tr = _tile(t, rows, SUBLANES)

    def body(x_ref, g_ref, t_ref, dx_ref, dg_ref, loss_ref):
        x_, g_ = x_ref[...], g_ref[...]
        err = _xhat(x_) * g_ - t_ref[...]
        dx, dgr = _norm_bwd_rows(err * (1.0 / d), x_, g_)
        dx_ref[...] = dx

        @pl.when(pl.program_id(0) == 0)
        def _():
            dg_ref[...] = jnp.zeros_like(dg_ref)
            loss_ref[...] = jnp.zeros_like(loss_ref)

        dg_ref[...] += jnp.sum(dgr, axis=0, keepdims=True)
        loss_ref[...] += 0.5 * jnp.sum(jnp.mean(err * err, axis=-1, keepdims=True), axis=0, keepdims=True)

    row = pl.BlockSpec((tr, d), lambda i: (i, 0))
    vec = pl.BlockSpec((1, d), lambda i: (0, 0))
    one = pl.BlockSpec((1, 1), lambda i: (0, 0))
    dx, dg, loss = pl.pallas_call(
        body, name="loss_head", grid=(t // tr,), in_specs=[row, vec, row], out_specs=[row, vec, one],
        out_shape=[jax.ShapeDtypeStruct((t, d), F32), jax.ShapeDtypeStruct((1, d), F32),
                   jax.ShapeDtypeStruct((1, 1), F32)],
        compiler_params=_params(("arbitrary",)))(x, g.reshape(1, d), target)
    return loss.reshape(()), dx, dg.reshape(d)


def _head_masks(shape):
    lane = lax.broadcasted_iota(jnp.int32, shape, len(shape) - 1)
    return lane < HEAD_DIM, lane >= HEAD_DIM


def _split_heads(x):
    m0, m1 = _head_masks(x.shape)
    zero = jnp.zeros_like(x)
    return jnp.where(m0, x, zero), jnp.where(m1, x, zero)


def _lane_pair(a0, a1, rows):
    m0, _ = _head_masks((rows, LANES))
    return jnp.where(m0, a0, a1)


def _qkv_readers(refs, packed):
    if packed:
        (r,) = refs
        return tuple((lambda r0, n, s=s: r[pl.ds(r0, n), s * LANES:(s + 1) * LANES]) for s in range(3))
    return tuple((lambda r0, n, ref=ref: ref[pl.ds(r0, n), :]) for ref in refs)


def _pair_spec(seq, col0, width=LANES):
    return pl.BlockSpec((seq, width), lambda p, b: (b, col0 + p))


def _fox_specs(seq, nk, tk):
    return [pl.BlockSpec((None, None, seq, 2), lambda p, b: (b, p, 0, 0)),
            pl.BlockSpec((None, None, nk, 2, tk), lambda p, b: (b, p, 0, 0, 0))]


def _softmax_attn_fwd(src, *, nb, mode, mixer=None, out_buf=None, extra=(), name, comm=None):
    packed = mode != "cross"
    n_src = 1 if packed else 3
    seq_q = (src if packed else src[0]).shape[0] // nb
    seq_k = seq_q if packed else src[1].shape[0] // nb
    tq, tk = min(ATT_TILE, seq_q), min(ATT_TILE, seq_k)
    nq, nk = seq_q // tq, seq_k // tk
    n_ex = len(extra)

    def body(*refs):
        q_at, k_at, v_at = _qkv_readers(refs[:n_src], packed)
        ex = refs[n_src:n_src + n_ex]
        o_ref, lse_ref = refs[-2:]

        def q_tile(i, _):
            r0 = pl.multiple_of(i * tq, tq)
            qm = _split_heads((q_at(r0, tq) * SCALE).astype(BF16))
            if mode == "fox":
                cq = ex[0][pl.ds(r0, tq), :]
                row = r0 + lax.broadcasted_iota(jnp.int32, (tq, tk), 0)

            def k_tile(j, carry, diagonal=False):
                m, l, acc = carry
                c0 = pl.multiple_of(j * tk, tk)
                kt = k_at(c0, tk).astype(BF16)
                vm = _split_heads(v_at(c0, tk).astype(BF16))
                if mode == "fox":
                    ck = ex[1][j]
                hs = range(2)
                s = [lax.dot_general(qm[h], kt, NT_DIMS, preferred_element_type=F32) for h in hs]
                if mode == "fox":
                    s = [s[h] + cq[:, h:h + 1] - ck[h:h + 1, :] for h in hs]
                    if diagonal:
                        keep = (c0 + lax.broadcasted_iota(jnp.int32, (tq, tk), 1)) <= row
                        s = [jnp.where(keep, s[h], MASKED) for h in hs]
                elif mode == "dil":
                    s = [s[h] + ex[0][h, i - j] for h in hs]
                new_m = [jnp.maximum(m[h], jnp.max(s[h], axis=-1, keepdims=True)) for h in hs]
                p = [jnp.exp(s[h] - new_m[h]) for h in hs]
                alpha = [jnp.exp(m[h] - new_m[h]) for h in hs]
                new_l = [alpha[h] * l[h] + jnp.sum(p[h], axis=-1, keepdims=True) for h in hs]
                pv = [jnp.dot(p[h].astype(BF16), vm[h], preferred_element_type=F32) for h in hs]
                acc = acc * _lane_pair(alpha[0], alpha[1], tq) + (pv[0] + pv[1])
                return tuple(new_m), tuple(new_l), acc

            init = ((jnp.full((tq, 1), MASKED, F32),) * 2, (jnp.zeros((tq, 1), F32),) * 2,
                    jnp.zeros((tq, LANES), F32))
            if mode == "fox":
                m, l, acc = k_tile(i, lax.fori_loop(0, i, k_tile, init), True)
            else:
                m, l, acc = lax.fori_loop(0, i + 1 if packed else nk, k_tile, init)
            o_ref[pl.ds(r0, tq), :] = acc / _lane_pair(l[0], l[1], tq)
            lse_ref[pl.ds(r0, tq), :] = _lane_pair(m[0] + jnp.log(l[0]), m[1] + jnp.log(l[1]), tq)
            return 0

        lax.fori_loop(0, nq, q_tile, 0)

    lse_shape = jax.ShapeDtypeStruct((nb * seq_q, 2 * LANES), F32)
    if packed:
        in_specs, args = [_pair_spec(seq_q, 2 * mixer, PAIR_W)], [src]
        in_specs += _fox_specs(seq_q, nk, tk) if mode == "fox" else [
            pl.BlockSpec((None, 2, nq, tq, tk), lambda p, b: (p, 0, 0, 0, 0))]
        args += list(extra) + [out_buf]
        in_specs.append(ANY)
        out_specs = [_pair_spec(seq_q, 2 * mixer), _pair_spec(seq_q, 0)]
        out_shape = [jax.ShapeDtypeStruct(out_buf.shape, F32), lse_shape]
        aliases = {len(args) - 1: 0}
    else:
        in_specs = [_pair_spec(seq_q, 0), _pair_spec(seq_k, 0), _pair_spec(seq_k, 0)]
        args = list(src)
        out_specs = [_pair_spec(seq_q, 0), _pair_spec(seq_q, 0)]
        out_shape = [lse_shape, lse_shape]
        aliases = {}
    out, carried = _pallas(body, name=name, grid=(2, nb), in_specs=in_specs, out_specs=out_specs, out_shape=out_shape,
                           args=args, aliases=aliases, sem=("parallel", "arbitrary"), comm=comm)
    return out if comm is None else (out, carried)


def _softmax_attn_bwd(src, o, lse, do, *, nb, mode, mixer=None, dbuf=None, extra=(), name, comm=None):
    packed = mode != "cross"
    n_src = 1 if packed else 3
    seq_q = (src if packed else src[0]).shape[0] // nb
    seq_k = seq_q if packed else src[1].shape[0] // nb
    tq, tk = min(ATT_TILE, seq_q), min(ATT_TILE, seq_k)
    nq, nk = seq_q // tq, seq_k // tk
    n_ex = len(extra)
    n_in = n_src + 3 + n_ex + (1 if packed else 0)

    def body(*refs):
        q_at, k_at, v_at = _qkv_readers(refs[:n_src], packed)
        o_ref, lse_ref, do_ref = refs[n_src:n_src + 3]
        ex = refs[n_src + 3:n_src + 3 + n_ex]
        outs = refs[n_in:]
        if packed:
            d_ref = outs[0]
            dq_w = lambda r0, val: d_ref.__setitem__((pl.ds(r0, tq), slice(0, LANES)), val)
            dk_ref = d_ref.at[:, LANES:2 * LANES]
            dv_ref = d_ref.at[:, 2 * LANES:3 * LANES]
        else:
            dq_ref, dk_ref, dv_ref = outs[:3]
            dq_w = lambda r0, val: dq_ref.__setitem__((pl.ds(r0, tq), slice(None)), val)
        dk_ref[...] = jnp.zeros((seq_k, LANES), F32)
        dv_ref[...] = jnp.zeros((seq_k, LANES), F32)
        if mode == "fox":
            dcum_ref, dcq_ref = outs[-2:]
            dcum_ref[...] = jnp.zeros_like(dcum_ref)
        if mode == "dil":
            dbias_ref = outs[-1]

            @pl.when(pl.program_id(1) == 0)
            def _():
                dbias_ref[...] = jnp.zeros_like(dbias_ref)

        def q_tile(i, _):
            r0 = pl.multiple_of(i * tq, tq)
            qm = _split_heads((q_at(r0, tq) * SCALE).astype(BF16))
            do_f = do_ref[pl.ds(r0, tq), :]
            dom = _split_heads(do_f.astype(BF16))
            dd = _split_heads(do_f * o_ref[pl.ds(r0, tq), :])
            delta = [jnp.sum(dd[h], axis=-1, keepdims=True) for h in range(2)]
            lse_t = lse_ref[pl.ds(r0, tq), :]
            lse_h = [lse_t[:, 0:1], lse_t[:, HEAD_DIM:HEAD_DIM + 1]]
            if mode == "fox":
                cq = ex[0][pl.ds(r0, tq), :]
                row = r0 + lax.broadcasted_iota(jnp.int32, (tq, tk), 0)

            def k_tile(j, carry, diagonal=False):
                dq, rs = carry
                c0 = pl.multiple_of(j * tk, tk)
                kt = k_at(c0, tk).astype(BF16)
                vt = v_at(c0, tk).astype(BF16)
                km = _split_heads(kt)
                if mode == "fox":
                    ck = ex[1][j]
                hs = range(2)
                s = [lax.dot_general(qm[h], kt, NT_DIMS, preferred_element_type=F32) for h in hs]
                dp = [lax.dot_general(dom[h], vt, NT_DIMS, preferred_element_type=F32) for h in hs]
                if mode == "fox":
                    s = [s[h] + cq[:, h:h + 1] - ck[h:h + 1, :] for h in hs]
                    if diagonal:
                        keep = (c0 + lax.broadcasted_iota(jnp.int32, (tq, tk), 1)) <= row
                        s = [jnp.where(keep, s[h], MASKED) for h in hs]
                elif mode == "dil":
                    s = [s[h] + ex[0][h, i - j] for h in hs]
                p = [jnp.exp(s[h] - lse_h[h]) for h in hs]
                ds = [p[h] * (dp[h] - delta[h]) for h in hs]
                dsb = [ds[h].astype(BF16) for h in hs]
                pb = [p[h].astype(BF16) for h in hs]
                dq = dq + (jnp.dot(dsb[0], km[0], preferred_element_type=F32)
                           + jnp.dot(dsb[1], km[1], preferred_element_type=F32))
                dk_t = (lax.dot_general(dsb[0], qm[0], TN_DIMS, preferred_element_type=F32)
                        + lax.dot_general(dsb[1], qm[1], TN_DIMS, preferred_element_type=F32))
                dv_t = (lax.dot_general(pb[0], dom[0], TN_DIMS, preferred_element_type=F32)
                        + lax.dot_general(pb[1], dom[1], TN_DIMS, preferred_element_type=F32))
                if mode == "fox":
                    for h in hs:
                        dcum_ref[j, h:h + 1, :] -= jnp.sum(ds[h], axis=0, keepdims=True)
                    rs = tuple(rs[h] + jnp.sum(ds[h], axis=-1, keepdims=True) for h in hs)
                elif mode == "dil":
                    for h in hs:
                        dbias_ref[h, i - j] += ds[h]
                dk_ref[pl.ds(c0, tk), :] += dk_t
                dv_ref[pl.ds(c0, tk), :] += dv_t
                return dq, rs

            zero = (jnp.zeros((tq, 1), F32),) * 2
            init = (jnp.zeros((tq, LANES), F32), zero)
            if mode == "fox":
                dq, rs = k_tile(i, lax.fori_loop(0, i, k_tile, init), True)
            else:
                dq, rs = lax.fori_loop(0, i + 1 if packed else nk, k_tile, init)
            dq_w(r0, dq * SCALE)
            if mode == "fox":
                dcq_ref[pl.ds(r0, tq), :] = jnp.where(lax.broadcasted_iota(jnp.int32, (tq, 2), 1) == 0, rs[0], rs[1])
            return 0

        lax.fori_loop(0, nq, q_tile, 0)

    if packed:
        in_specs = [_pair_spec(seq_q, 2 * mixer, PAIR_W), _pair_spec(seq_q, 2 * mixer), _pair_spec(seq_q, 0),
                    _pair_spec(seq_q, 2 * mixer)]
        args = [src, o, lse, do]
        out_specs = [_pair_spec(seq_q, 2 * mixer, PAIR_W)]
        out_shape = [jax.ShapeDtypeStruct(dbuf.shape, F32)]
        if mode == "fox":
            in_specs += _fox_specs(seq_q, nk, tk)
            out_specs += [_fox_specs(seq_q, nk, tk)[1], _fox_specs(seq_q, nk, tk)[0]]
            out_shape += [jax.ShapeDtypeStruct((nb, 2, nk, 2, tk), F32), jax.ShapeDtypeStruct((nb, 2, seq_q, 2), F32)]
        else:
            tiles = pl.BlockSpec((None, 2, nq, tq, tk), lambda p, b: (p, 0, 0, 0, 0))
            in_specs.append(tiles)
            out_specs.append(tiles)
            out_shape.append(jax.ShapeDtypeStruct((2, 2, nq, tq, tk), F32))
        args += list(extra) + [dbuf]
        in_specs.append(ANY)
        aliases = {len(args) - 1: 0}
    else:
        sq, sk = _pair_spec(seq_q, 0), _pair_spec(seq_k, 0)
        in_specs, args = [sq, sk, sk, sq, sq, sq], list(src) + [o, lse, do]
        out_specs = [sq, sk, sk]
        out_shape = [jax.ShapeDtypeStruct((nb * seq_q, 2 * LANES), F32)] + [
            jax.ShapeDtypeStruct((nb * seq_k, 2 * LANES), F32)] * 2
        aliases = {}
    out, carried = _pallas(body, name=name, grid=(2, nb), in_specs=in_specs, out_specs=out_specs, out_shape=out_shape,
                           args=args, aliases=aliases, sem=("parallel", "arbitrary"), comm=comm)
    return out if comm is None else (out, carried)


def _log_sigmoid(z):
    return jnp.minimum(z, 0.0) - jnp.log(1.0 + jnp.exp(-jnp.abs(z)))


def _split_bf16(x):
    hi = x.astype(BF16)
    return hi, (x - hi.astype(F32)).astype(BF16)


def _tri(n, fn):
    r = lax.broadcasted_iota(jnp.int32, (n, n), 0)
    c = lax.broadcasted_iota(jnp.int32, (n, n), 1)
    return jnp.where(fn(r, c), 1.0, 0.0).astype(BF16)


def _sb_attn_fwd(proj, out_buf, *, nb, name, comm=None):
    seq = proj.shape[0] // nb
    tq = tk = min(ATT_TILE, seq)
    nq = seq // tq

    def body(qkv_ref, _, o_ref, lt_ref):
        rd = [_qkv_readers((qkv_ref.at[:, pr * PAIR_W:(pr + 1) * PAIR_W],), True) for pr in range(2)]
        after = _tri(tk, lambda r, c: r > c)
        ch = [(pr, h) for pr in range(2) for h in range(2)]

        def q_tile(i, _):
            r0 = pl.multiple_of(i * tq, tq)
            qm = [_split_heads((rd[pr][0](r0, tq) * SCALE).astype(BF16)) for pr in range(2)]
            row = r0 + lax.broadcasted_iota(jnp.int32, (tq, tk), 0)

            def k_tile(j, carry, diagonal):
                c, acc = carry
                c0 = pl.multiple_of(j * tk, tk)
                kt = [rd[pr][1](c0, tk).astype(BF16) for pr in range(2)]
                vm = [_split_heads(rd[pr][2](c0, tk).astype(BF16)) for pr in range(2)]
                if diagonal:
                    strict = (c0 + lax.broadcasted_iota(jnp.int32, (tq, tk), 1)) < row
                ns = range(len(ch))
                z = [lax.dot_general(qm[pr][h], kt[pr], NT_DIMS, preferred_element_type=F32) for pr, h in ch]
                ls = [_log_sigmoid(z[n]) for n in ns]
                lk = [ls[n] - z[n] for n in ns]
                if diagonal:
                    lk = [jnp.where(strict, lk[n], 0.0) for n in ns]
                parts = [_split_bf16(lk[n]) for n in ns]
                sfx = [jnp.dot(parts[n][0], after, preferred_element_type=F32)
                       + jnp.dot(parts[n][1], after, preferred_element_type=F32) for n in ns]
                att = [jnp.exp(ls[n] + sfx[n] + c[n]) for n in ns]
                if diagonal:
                    att = [jnp.where(strict, att[n], 0.0) for n in ns]
                acc = tuple(acc[pr] + (jnp.dot(att[2 * pr].astype(BF16), vm[pr][0], preferred_element_type=F32)
                                       + jnp.dot(att[2 * pr + 1].astype(BF16), vm[pr][1], preferred_element_type=F32))
                            for pr in range(2))
                return tuple(c[n] + jnp.sum(lk[n], axis=-1, keepdims=True) for n in ns), acc

            init = ((jnp.zeros((tq, 1), F32),) * 4, (jnp.zeros((tq, LANES), F32),) * 2)
            c, acc = lax.fori_loop(1, i + 1, lambda jj, cr: k_tile(i - jj, cr, False), k_tile(i, init, True))
            for pr in range(2):
                o_ref[pl.ds(r0, tq), pr * LANES:(pr + 1) * LANES] = acc[pr]
                lt_ref[pl.ds(r0, tq), pr * LANES:(pr + 1) * LANES] = _lane_pair(c[2 * pr], c[2 * pr + 1], tq)
            return 0

        lax.fori_loop(0, nq, q_tile, 0)

    both = lambda width, col: pl.BlockSpec((seq, 2 * width), lambda b: (b, col))
    out, carried = _pallas(
        body, name=name, grid=(nb,), in_specs=[both(PAIR_W, MIX_SB), ANY],
        out_specs=[both(LANES, MIX_SB), both(LANES, 0)],
        out_shape=[jax.ShapeDtypeStruct(out_buf.shape, F32), jax.ShapeDtypeStruct((nb * seq, 2 * LANES), F32)],
        args=[proj, out_buf], aliases={1: 0}, sem=("arbitrary",), comm=comm)
    return out if comm is None else (out, carried)


def _sb_attn_bwd(proj, ltot, do, dbuf, *, nb, name, comm=None):
    seq = proj.shape[0] // nb
    tq = tk = min(ATT_TILE, seq)
    nq = seq // tq

    def body(qkv_ref, lt_ref, do_ref, _, d_ref):
        rd = [_qkv_readers((qkv_ref.at[:, pr * PAIR_W:(pr + 1) * PAIR_W],), True) for pr in range(2)]
        upto = _tri(tk, lambda r, c: r <= c)
        before = _tri(tk, lambda r, c: r < c)
        dk_ref = [d_ref.at[:, pr * PAIR_W + LANES:pr * PAIR_W + 2 * LANES] for pr in range(2)]
        dv_ref = [d_ref.at[:, pr * PAIR_W + 2 * LANES:(pr + 1) * PAIR_W] for pr in range(2)]
        for ref in dk_ref + dv_ref:
            ref[...] = jnp.zeros((seq, LANES), F32)
        ch = [(pr, h) for pr in range(2) for h in range(2)]

        def q_tile(i, _):
            r0 = pl.multiple_of(i * tq, tq)
            qm = [_split_heads((rd[pr][0](r0, tq) * SCALE).astype(BF16)) for pr in range(2)]
            dom = [_split_heads(do_ref[pl.ds(r0, tq), pr * LANES:(pr + 1) * LANES].astype(BF16)) for pr in range(2)]
            lt_t = lt_ref[pl.ds(r0, tq), :]
            lt_h = [lt_t[:, pr * LANES + h * HEAD_DIM:pr * LANES + h * HEAD_DIM + 1] for pr, h in ch]
            row = r0 + lax.broadcasted_iota(jnp.int32, (tq, tk), 0)

            def k_tile(j, carry, diagonal):
                pc, qc, dq = carry
                c0 = pl.multiple_of(j * tk, tk)
                kt = [rd[pr][1](c0, tk).astype(BF16) for pr in range(2)]
                vt = [rd[pr][2](c0, tk).astype(BF16) for pr in range(2)]
                km = [_split_heads(kt[pr]) for pr in range(2)]
                if diagonal:
                    strict = (c0 + lax.broadcasted_iota(jnp.int32, (tq, tk), 1)) < row
                ns = range(len(ch))
                z = [lax.dot_general(qm[pr][h], kt[pr], NT_DIMS, preferred_element_type=F32) for pr, h in ch]
                da = [lax.dot_general(dom[pr][h], vt[pr], NT_DIMS, preferred_element_type=F32) for pr, h in ch]
                ls = [_log_sigmoid(z[n]) for n in ns]
                lk = [ls[n] - z[n] for n in ns]
                if diagonal:
                    lk = [jnp.where(strict, lk[n], 0.0) for n in ns]
                parts = [_split_bf16(lk[n]) for n in ns]
                pin = [jnp.dot(parts[n][0], upto, preferred_element_type=F32)
                       + jnp.dot(parts[n][1], upto, preferred_element_type=F32) for n in ns]
                att = [jnp.exp(ls[n] + (lt_h[n] - pc[n] - pin[n])) for n in ns]
                if diagonal:
                    att = [jnp.where(strict, att[n], 0.0) for n in ns]
                dg = [att[n] * da[n] for n in ns]
                qx = [qc[n] + jnp.dot(dg[n].astype(BF16), before, preferred_element_type=F32) for n in ns]
                sig = [jnp.exp(ls[n]) for n in ns]
                dz = [dg[n] * (1.0 - sig[n]) - sig[n] * qx[n] for n in ns]
                if diagonal:
                    dz = [jnp.where(strict, dz[n], 0.0) for n in ns]
                dzb = [dz[n].astype(BF16) for n in ns]
                attb = [att[n].astype(BF16) for n in ns]
                new_dq = []
                for pr in range(2):
                    a, b = 2 * pr, 2 * pr + 1
                    new_dq.append(dq[pr] + (jnp.dot(dzb[a], km[pr][0], preferred_element_type=F32)
                                            + jnp.dot(dzb[b], km[pr][1], preferred_element_type=F32)))
                    dk_ref[pr][pl.ds(c0, tk), :] += (
                        lax.dot_general(dzb[a], qm[pr][0], TN_DIMS, preferred_element_type=F32)
                        + lax.dot_general(dzb[b], qm[pr][1], TN_DIMS, preferred_element_type=F32))
                    dv_ref[pr][pl.ds(c0, tk), :] += (
                        lax.dot_general(attb[a], dom[pr][0], TN_DIMS, preferred_element_type=F32)
                        + lax.dot_general(attb[b], dom[pr][1], TN_DIMS, preferred_element_type=F32))
                return (tuple(pc[n] + jnp.sum(lk[n], axis=-1, keepdims=True) for n in ns),
                        tuple(qc[n] + jnp.sum(dg[n], axis=-1, keepdims=True) for n in ns), tuple(new_dq))

            zero = (jnp.zeros((tq, 1), F32),) * 4
            init = (zero, zero, (jnp.zeros((tq, LANES), F32),) * 2)
            carry = lax.fori_loop(0, i, lambda j, cr: k_tile(j, cr, False), init)
            _, _, dq = k_tile(i, carry, True)
            for pr in range(2):
                d_ref[pl.ds(r0, tq), pr * PAIR_W:pr * PAIR_W + LANES] = dq[pr] * SCALE
            return 0

        lax.fori_loop(0, nq, q_tile, 0)

    both = lambda width, col: pl.BlockSpec((seq, 2 * width), lambda b: (b, col))
    out, carried = _pallas(
        body, name=name, grid=(nb,), in_specs=[both(PAIR_W, MIX_SB), both(LANES, 0), both(LANES, MIX_SB), ANY],
        out_specs=[both(PAIR_W, MIX_SB)], out_shape=[jax.ShapeDtypeStruct(dbuf.shape, F32)],
        args=[proj, ltot, do, dbuf], aliases={3: 0}, sem=("arbitrary",), comm=comm)
    return out[0] if comm is None else (out[0], carried)


def _lane_scan(x, reverse=False):
    n = x.shape[-1]
    lane = lax.broadcasted_iota(jnp.int32, x.shape, 1)
    k = 1
    while k < n:
        if reverse:
            x = x + jnp.where(lane < n - k, pltpu.roll(x, n - k, 1), 0.0)
        else:
            x = x + jnp.where(lane >= k, pltpu.roll(x, k, 1), 0.0)
        k *= 2
    return x


def _fox_gate_fwd(f_rows, b_rows):
    def body(f_ref, b_ref, o_ref):
        o_ref[...] = _lane_scan(_log_sigmoid(f_ref[...] + b_ref[...]))

    return pl.pallas_call(body, name="fox_gate_fwd", out_shape=jax.ShapeDtypeStruct(f_rows.shape, F32))(f_rows, b_rows)


def _fox_gate_bwd(dcum, f_rows, b_rows):
    def body(d_ref, f_ref, b_ref, df_ref, db_ref):
        z = f_ref[...] + b_ref[...]
        df = _lane_scan(d_ref[...], reverse=True) * jnp.exp(_log_sigmoid(-z))
        df_ref[...] = df
        rs = jnp.sum(df, axis=-1, keepdims=True)
        tot = rs
        for e in range(1, f_rows.shape[0] // N_HEADS):
            tot = tot + pltpu.roll(rs, e * N_HEADS, 0)
        db_ref[...] = tot

    return pl.pallas_call(
        body, name="fox_gate_bwd",
        out_shape=[jax.ShapeDtypeStruct(f_rows.shape, F32), jax.ShapeDtypeStruct((f_rows.shape[0], 1), F32)],
    )(dcum, f_rows, b_rows)


def _dil_tables(seq):
    t = min(ATT_TILE, seq)
    n = seq // t
    a = np.arange(t)
    d = (np.arange(n)[:, None, None] * t + a[None, :, None] - a[None, None, :]).astype(np.int64)
    count = np.zeros(d.shape, np.int64)
    for window, dil in DILATED_PATTERNS:
        count += (d >= 0) & (d % dil == 0) & (d // dil <= window // dil)
    nn = np.maximum(d, 0)
    max_exact = NUM_BUCKETS // 2
    nf = np.maximum(nn, 1).astype(np.float32)
    large = max_exact + (np.log(nf / np.float32(max_exact)) / np.float32(math.log(MAX_DISTANCE / max_exact))
                         * np.float32(NUM_BUCKETS - max_exact)).astype(np.int32)
    bucket = np.where(nn < max_exact, nn, np.minimum(large, NUM_BUCKETS - 1))
    bucket = np.where(count > 0, bucket, -1).astype(np.int32)
    logc = np.where(count > 0, np.log(np.maximum(count, 1)), MASKED).astype(np.float32)
    return bucket, logc


def _dil_bias(rel_bias, seq, comm=None):
    bucket, logc = _dil_tables(seq)
    n, t, _ = bucket.shape

    def body(rb_ref, bk_ref, lc_ref, o_ref):
        h = pl.program_id(0) * 2 + pl.program_id(1)
        bk = bk_ref[...]
        out = lc_ref[...]
        for b in range(NUM_BUCKETS):
            out = jnp.where(bk == b, out + rb_ref[b, h], out)
        o_ref[...] = out

    full = pl.BlockSpec((n, t, t), lambda p, h: (0, 0, 0))
    out, carried = _pallas(
        body, name="dil_bias", grid=(2, 2),
        in_specs=[pl.BlockSpec(memory_space=pltpu.SMEM), full, full],
        out_specs=[pl.BlockSpec((None, None, n, t, t), lambda p, h: (p, h, 0, 0, 0))],
        out_shape=[jax.ShapeDtypeStruct((2, 2, n, t, t), F32)],
        args=[rel_bias, jnp.asarray(bucket), jnp.asarray(logc)], sem=("parallel", "parallel"), comm=comm)
    return out[0] if comm is None else (out[0], carried)


def _dil_bias_bwd(dbias, seq):
    bucket, _ = _dil_tables(seq)
    n, t, _ = bucket.shape

    def body(d_ref, bk_ref, o_ref):
        bk = bk_ref[...]
        lane = lax.broadcasted_iota(jnp.int32, (1, LANES), 1)
        for b in range(NUM_BUCKETS):
            rowv = jnp.zeros((1, LANES), F32)
            for h in range(N_HEADS):
                s = jnp.sum(jnp.where(bk == b, d_ref[h // 2, h % 2], 0.0))
                rowv = jnp.where(lane == h, s, rowv)
            o_ref[b:b + 1, :] = rowv

    out = pl.pallas_call(body, name="dil_bias_bwd", out_shape=jax.ShapeDtypeStruct((NUM_BUCKETS, LANES), F32),
                         compiler_params=pltpu.CompilerParams(vmem_limit_bytes=VMEM_LIMIT))(dbias, jnp.asarray(bucket))
    return out[:, :N_HEADS]


def _shift_rows(x, k, row, fill=0.0):
    n = x.shape[0]
    if k > 0:
        return jnp.where(row >= k, pltpu.roll(x, k, 0), fill)
    return jnp.where(row < n + k, pltpu.roll(x, n + k, 0), fill)


def _row_scan(a, u, row, reverse=False):
    n = a.shape[0]
    k = 1
    while k < n:
        s = -k if reverse else k
        u = a * _shift_rows(u, s, row) + u
        a = a * _shift_rows(a, s, row, 1.0)
        k *= 2
    return u


def _sigmoid(x):
    return 1.0 / (1.0 + jnp.exp(-x))


def _gelu(g):
    return 0.5 * g * (1.0 + lax.erf(g * (2.0 ** -0.5)))


def _gelu_grad(g):
    return 0.5 * (1.0 + lax.erf(g * (2.0 ** -0.5))) + g * jnp.exp(-0.5 * g * g) * (1.0 / math.sqrt(2.0 * math.pi))


def _neg_expm1(x):
    small = -x * (1.0 + x * (0.5 + x * (1.0 / 6.0 + x * (1.0 / 24.0))))
    return jnp.where(x > -0.03, small, 1.0 - jnp.exp(x))


def _lru_core(x, vec, wa, wx, row):
    xs = [_shift_rows(x, 3 - j, row) if j < 3 else x for j in range(4)]
    xc = vec[4:5, :]
    for j in range(4):
        xc = xc + vec[j:j + 1, :] * xs[j]
    xcb = xc.astype(BF16)
    r = _sigmoid(jnp.dot(xcb, wa, preferred_element_type=F32) + vec[5:6, :])
    ig = _sigmoid(jnp.dot(xcb, wx, preferred_element_type=F32) + vec[6:7, :])
    lam = vec[7:8, :]
    sp = jnp.maximum(-lam, 0.0) - _log_sigmoid(jnp.abs(lam))
    la = -LRU_C * r * sp
    a = jnp.exp(la)
    mult = jnp.sqrt(_neg_expm1(2.0 * la))
    return xs, xc, xcb, r, ig, sp, la, a, mult


def _lru_specs(seq):
    xg = pl.BlockSpec((seq, LRU_W), lambda hf, b: (b, COL_LRU // LRU_W + hf))
    mix = pl.BlockSpec((seq, LANES), lambda hf, b: (b, 2 * MIX_LRU + hf))
    vec = pl.BlockSpec((SUBLANES, LANES), lambda hf, b: (0, hf))
    mat = pl.BlockSpec((None, LANES, LANES), lambda hf, b: (hf, 0, 0))
    return xg, mix, vec, mat


def _lru_fwd(proj, vec, wa, wx, out_buf, *, nb, name):
    seq = proj.shape[0] // nb

    def body(xg_ref, vec_ref, wa_ref, wx_ref, _, o_ref):
        row = lax.broadcasted_iota(jnp.int32, (seq, LANES), 0)
        _, xc, _, _, ig, _, _, a, mult = _lru_core(xg_ref[:, 0:LANES], vec_ref[...], wa_ref[...], wx_ref[...], row)
        h = _row_scan(a, mult * (ig * xc), row)
        o_ref[...] = h * _gelu(xg_ref[:, LANES:LRU_W])

    xg, mix, vecs, mat = _lru_specs(seq)
    return pl.pallas_call(
        body, name=name, grid=(2, nb), in_specs=[xg, vecs, mat, mat, ANY], out_specs=mix,
        out_shape=jax.ShapeDtypeStruct(out_buf.shape, F32), input_output_aliases={4: 0},
        compiler_params=_params(("parallel", "arbitrary")))(proj, vec, wa, wx, out_buf)


def _lru_bwd(proj, vec, wa, wx, dout, dbuf, *, nb, name):
    seq = proj.shape[0] // nb

    def body(xg_ref, vec_ref, wa_ref, wx_ref, do_ref, _, d_ref, dvec_ref, dwa_ref, dwx_ref):
        row = lax.broadcasted_iota(jnp.int32, (seq, LANES), 0)
        vec_, wa_, wx_ = vec_ref[...], wa_ref[...], wx_ref[...]
        xs, xc, xcb, r, ig, sp, la, a, mult = _lru_core(xg_ref[:, 0:LANES], vec_, wa_, wx_, row)
        h = _row_scan(a, mult * (ig * xc), row)
        gate, do = xg_ref[:, LANES:LRU_W], do_ref[...]
        d_ref[:, LANES:LRU_W] = do * h * _gelu_grad(gate)
        dh = do * _gelu(gate)
        gacc = _row_scan(_shift_rows(a, -1, row), dh, row, reverse=True)
        da = gacc * _shift_rows(h, 1, row)
        dmult = gacc * (ig * xc)
        dig = gacc * (mult * xc)
        dxc = gacc * (mult * ig)
        dla = da * a - dmult * (a * a) / mult
        dr = (-LRU_C) * sp * dla
        dsp = jnp.sum((-LRU_C) * r * dla, axis=0, keepdims=True)
        dpr = dr * r * (1.0 - r)
        dpi = dig * ig * (1.0 - ig)
        dprb, dpib = dpr.astype(BF16), dpi.astype(BF16)
        dxc = (dxc + lax.dot_general(dprb, wa_, NT_DIMS, preferred_element_type=F32)
               + lax.dot_general(dpib, wx_, NT_DIMS, preferred_element_type=F32))
        dx = vec_[3:4, :] * dxc
        for j in range(3):
            dx = dx + vec_[j:j + 1, :] * _shift_rows(dxc, -(3 - j), row)
        d_ref[:, 0:LANES] = dx

        @pl.when(pl.program_id(1) == 0)
        def _():
            dvec_ref[...] = jnp.zeros_like(dvec_ref)
            dwa_ref[...] = jnp.zeros_like(dwa_ref)
            dwx_ref[...] = jnp.zeros_like(dwx_ref)

        for j in range(4):
            dvec_ref[j:j + 1, :] += jnp.sum(dxc * xs[j], axis=0, keepdims=True)
        dvec_ref[4:5, :] += jnp.sum(dxc, axis=0, keepdims=True)
        dvec_ref[5:6, :] += jnp.sum(dpr, axis=0, keepdims=True)
        dvec_ref[6:7, :] += jnp.sum(dpi, axis=0, keepdims=True)
        lam = vec_[7:8, :]
        dvec_ref[7:8, :] += -dsp * _sigmoid(-lam)
        dwa_ref[...] += lax.dot_general(xcb, dprb, TN_DIMS, preferred_element_type=F32)
        dwx_ref[...] += lax.dot_general(xcb, dpib, TN_DIMS, preferred_element_type=F32)

    xg, mix, vecs, mat = _lru_specs(seq)
    return pl.pallas_call(
        body, name=name, grid=(2, nb), in_specs=[xg, vecs, mat, mat, mix, ANY], out_specs=[xg, vecs, mat, mat],
        out_shape=[jax.ShapeDtypeStruct(dbuf.shape, F32), jax.ShapeDtypeStruct((SUBLANES, 2 * LANES), F32),
                   jax.ShapeDtypeStruct((2, LANES, LANES), F32), jax.ShapeDtypeStruct((2, LANES, LANES), F32)],
        input_output_aliases={5: 0},
        compiler_params=_params(("parallel", "arbitrary")))(proj, vec, wa, wx, dout, dbuf)


FFN_ROWS = 256
FFN_COLS = 1408


def _with_halo(halo, x, k):
    xx = jnp.concatenate([halo, x], axis=0)
    return pltpu.roll(xx, k, 0)[SUBLANES:, :]


def _ffn_conv(x_ref, halo_ref, cw, pos):
    x, halo = x_ref[...], halo_ref[...]
    x1 = jnp.where(pos >= 1, _with_halo(halo, x, 1), 0.0)
    x2 = jnp.where(pos >= 2, _with_halo(halo, x, 2), 0.0)
    return cw[3:4, :] + cw[0:1, :] * x2 + cw[1:2, :] * x1 + cw[2:3, :] * x, x1, x2


def _ffn_specs(tm, tn, gate_off):
    prev = lambda i: jnp.maximum(i * (tm // SUBLANES) - 1, 0)
    up = pl.BlockSpec((tm, tn), lambda j, i: (i, j))
    gate = pl.BlockSpec((tm, tn), lambda j, i: (i, j + gate_off))
    up_h = pl.BlockSpec((SUBLANES, tn), lambda j, i: (prev(i), j))
    gate_h = pl.BlockSpec((SUBLANES, tn), lambda j, i: (prev(i), j + gate_off))
    cw_up = pl.BlockSpec((SUBLANES, tn), lambda j, i: (0, j))
    cw_gate = pl.BlockSpec((SUBLANES, tn), lambda j, i: (0, j + gate_off))
    return up, gate, up_h, gate_h, cw_up, cw_gate


def _ffn_act(hf, cw, *, seq, name):
    t, w2 = hf.shape
    w = w2 // 2
    tm, tn = _tile(seq, FFN_ROWS, SUBLANES), _tile(w, FFN_COLS)

    def body(u_ref, g_ref, uh_ref, gh_ref, cu_ref, cg_ref, o_ref):
        pos = (pl.program_id(1) * tm + lax.broadcasted_iota(jnp.int32, (tm, 1), 0)) % seq
        up, _, _ = _ffn_conv(u_ref, uh_ref, cu_ref[...], pos)
        gate, _, _ = _ffn_conv(g_ref, gh_ref, cg_ref[...], pos)
        o_ref[...] = (_gelu(gate) * up).astype(BF16)

    specs = _ffn_specs(tm, tn, w // tn)
    return pl.pallas_call(
        body, name=name, grid=(w // tn, t // tm), in_specs=list(specs), out_specs=specs[0],
        out_shape=jax.ShapeDtypeStruct((t, w), BF16),
        compiler_params=_params(("parallel", "parallel")))(hf, hf, hf, hf, cw, cw)


def _ffn_bwd(hf, cw, dact, *, seq, name, comm=None):
    t, w2 = hf.shape
    w = w2 // 2
    tm, tn = _tile(seq, FFN_ROWS, 2 * SUBLANES), _tile(w, FFN_COLS)
    ext = tm + SUBLANES
    last = t // SUBLANES - 1

    def body(u_ref, g_ref, uh_ref, gh_ref, cu_ref, cg_ref, un_ref, gn_ref, da_ref, dn_ref, d_ref, dcu_ref, dcg_ref):
        pos = (pl.program_id(1) * tm + lax.broadcasted_iota(jnp.int32, (ext, 1), 0)) % seq

        def conv(x_ref, prev_ref, next_ref, cwv):
            xx = jnp.concatenate([prev_ref[...], x_ref[...], next_ref[...]], axis=0)
            x1 = jnp.where(pos >= 1, pltpu.roll(xx, 1, 0)[SUBLANES:, :], 0.0)
            x2 = jnp.where(pos >= 2, pltpu.roll(xx, 2, 0)[SUBLANES:, :], 0.0)
            x0 = xx[SUBLANES:, :]
            return cwv[3:4, :] + cwv[0:1, :] * x2 + cwv[1:2, :] * x1 + cwv[2:3, :] * x0, (x2, x1, x0)

        def back(d, cwv):
            d1 = jnp.where(pos < seq - 1, pltpu.roll(d, ext - 1, 0), 0.0)
            d2 = jnp.where(pos < seq - 2, pltpu.roll(d, ext - 2, 0), 0.0)
            return (cwv[2:3, :] * d + cwv[1:2, :] * d1 + cwv[0:1, :] * d2)[:tm, :].astype(BF16)

        cu, cg = cu_ref[...], cg_ref[...]
        up, u_taps = conv(u_ref, uh_ref, un_ref, cu)
        gate, g_taps = conv(g_ref, gh_ref, gn_ref, cg)
        da = jnp.concatenate([da_ref[...], dn_ref[...]], axis=0)
        cdf = 0.5 * (1.0 + lax.erf(gate * (2.0 ** -0.5)))
        d_up = da * (gate * cdf)
        d_gate = da * up * (cdf + gate * jnp.exp(-0.5 * gate * gate) * (1.0 / math.sqrt(2.0 * math.pi)))
        d_ref[0] = back(d_up, cu)
        d_ref[1] = back(d_gate, cg)

        @pl.when(pl.program_id(1) == 0)
        def _():
            dcu_ref[...] = jnp.zeros_like(dcu_ref)
            dcg_ref[...] = jnp.zeros_like(dcg_ref)

        for ref, d, taps in ((dcu_ref, d_up, u_taps), (dcg_ref, d_gate, g_taps)):
            own = d[:tm, :]
            for j in range(3):
                ref[j:j + 1, :] += jnp.sum(own * taps[j][:tm, :], axis=0, keepdims=True)
            ref[3:4, :] += jnp.sum(own, axis=0, keepdims=True)

    gate_off = w // tn
    specs = _ffn_specs(tm, tn, gate_off)
    tile, cwt = specs[0], specs[4]
    nxt = lambda i: jnp.minimum((i + 1) * (tm // SUBLANES), last)
    up_n = pl.BlockSpec((SUBLANES, tn), lambda j, i: (nxt(i), j))
    gate_n = pl.BlockSpec((SUBLANES, tn), lambda j, i: (nxt(i), j + gate_off))
    out, carried = _pallas(
        body, name=name, grid=(w // tn, t // tm), in_specs=list(specs) + [up_n, gate_n, tile, up_n],
        out_specs=[pl.BlockSpec((2, tm, tn), lambda j, i: (0, i, j)), cwt, cwt],
        out_shape=[jax.ShapeDtypeStruct((2, t, w), BF16), jax.ShapeDtypeStruct((SUBLANES, w), F32),
                   jax.ShapeDtypeStruct((SUBLANES, w), F32)],
        args=[hf, hf, hf, hf, cw, cw, hf, hf, dact, dact], sem=("parallel", "arbitrary"), comm=comm)
    return out if comm is None else (out, carried)


def _adamw(w, g, m, v, *, name, rows=256, lead=None):
    nl, r, c = w.shape
    tr = _tile(r, rows, SUBLANES)

    def body(w_ref, g_ref, m_ref, v_ref, d_ref, nm_ref, nv_ref):
        g_ = g_ref[...]
        nm = ADAM_B1 * m_ref[...] + (1.0 - ADAM_B1) * g_
        nv = ADAM_B2 * v_ref[...] + (1.0 - ADAM_B2) * (g_ * g_)
        m_hat = nm / (1.0 - ADAM_B1 ** ADAM_STEP)
        v_hat = nv / (1.0 - ADAM_B2 ** ADAM_STEP)
        d_ref[...] = -ADAM_LR * (m_hat / (jnp.sqrt(v_hat) + ADAM_EPS) + ADAM_WD * w_ref[...])
        nm_ref[...] = nm
        nv_ref[...] = nv

    shape = jax.ShapeDtypeStruct((nl, r, c), F32)
    if lead is None:
        spec, grid = pl.BlockSpec((None, tr, c), lambda l, i: (l, i, 0)), (nl, r // tr)
    else:
        spec, grid = pl.BlockSpec((lead, r, c), lambda i: (i, 0, 0)), (nl // lead,)
    return pl.pallas_call(body, name=name, grid=grid, in_specs=[spec] * 4, out_specs=[spec] * 3,
                          out_shape=[shape] * 3, compiler_params=_params(("parallel",) * len(grid)))(w, g, m, v)


def _mesh_pos():
    return lax.axis_index("x"), lax.axis_index("y"), lax.axis_index("c")


def _peers(x, y):
    chips = [(1 - x, y), (x, 1 - y), (1 - x, 1 - y)]
    return [(px, py, 2 * px + py) for px, py in chips]


def _remote(src, dst, send_sems, recv_sems, idx, to):
    return pltpu.make_async_remote_copy(src, dst, send_sems.at[idx], recv_sems.at[idx], device_id=to,
                                        device_id_type=MESH)


class _Comm:
    def __init__(self, operands, out_shape, aliases, sems, copies):
        self.operands, self.out_shape, self.aliases, self.sems, self.copies = operands, out_shape, aliases, sems, copies

    def start(self, ins, outs, sems):
        for send, _ in self.copies(ins, outs, sems):
            send.start()

    def wait(self, ins, outs, sems):
        pairs = self.copies(ins, outs, sems)
        for _, recv in pairs:
            recv.wait_recv()
        for send, _ in pairs:
            send.wait_send()


def _pallas(body, *, name, grid, in_specs, out_specs, out_shape, args, aliases=None, scratch=(), sem, comm=None):
    n_in, n_out = len(in_specs), len(out_specs)
    aliases = dict(aliases or {})
    if comm is None:
        out = pl.pallas_call(body, name=name, grid=grid, in_specs=in_specs, out_specs=out_specs, out_shape=out_shape,
                             input_output_aliases=aliases, scratch_shapes=list(scratch),
                             compiler_params=_params(sem))(*args)
        return list(out), []
    nci, nco, ncs = len(comm.operands), len(comm.out_shape), len(comm.sems)

    def carried(*refs):
        ins, cin = refs[:n_in], refs[n_in:n_in + nci]
        o0 = n_in + nci
        outs, cout = refs[o0:o0 + n_out], refs[o0 + n_out:o0 + n_out + nco]
        s0 = o0 + n_out + nco
        own, csem = refs[s0:len(refs) - ncs], refs[len(refs) - ncs:]
        ids = [pl.program_id(ax) for ax in range(len(grid))]
        first, last = ids[0] == 0, ids[0] == grid[0] - 1
        for i, g in zip(ids[1:], grid[1:]):
            first, last = jnp.logical_and(first, i == 0), jnp.logical_and(last, i == g - 1)

        @pl.when(first)
        def _():
            comm.start(cin, cout, csem)

        body(*ins, *outs, *own)

        @pl.when(last)
        def _():
            comm.wait(cin, cout, csem)

    aliases.update({n_in + i: n_out + j for i, j in comm.aliases.items()})
    out = pl.pallas_call(
        carried, name=name, grid=grid, in_specs=list(in_specs) + [ANY] * nci, out_specs=list(out_specs) + [ANY] * nco,
        out_shape=list(out_shape) + list(comm.out_shape), input_output_aliases=aliases,
        scratch_shapes=list(scratch) + list(comm.sems),
        compiler_params=_params(("arbitrary",) * len(grid)))(*args, *comm.operands)
    return list(out[:n_out]), list(out[n_out:])


def _run_comm(comm, *, name):
    nci, nco = len(comm.operands), len(comm.out_shape)

    def body(*refs):
        ins, outs, sems = refs[:nci], refs[nci:nci + nco], refs[nci + nco:]
        comm.start(ins, outs, sems)
        comm.wait(ins, outs, sems)

    return pl.pallas_call(body, name=name, in_specs=[ANY] * nci, out_specs=[ANY] * nco, out_shape=list(comm.out_shape),
                          input_output_aliases=dict(comm.aliases), scratch_shapes=list(comm.sems))(*comm.operands)


def _pair_sems(*shape):
    return [pltpu.SemaphoreType.DMA(shape), pltpu.SemaphoreType.DMA(shape)]


def _gather_comm(bufs, layer, stage):
    n = len(bufs)

    def copies(ins, outs, sems):
        x, y, c = _mesh_pos()
        me = 2 * x + y
        pairs = []
        for i in range(n):
            h = bufs[i].shape[2] // 2
            mine, other = pl.ds(c * h, h), pl.ds((1 - c) * h, h)
            for r, (px, py, k) in enumerate(_peers(x, y)):
                if stage == 0:
                    send = _remote(ins[i].at[layer, me, mine, :], outs[i].at[layer, me, mine, :], *sems, (i, r), (px, py, c))
                    land = outs[i].at[layer, k, mine, :]
                    recv = _remote(land, land, *sems, (i, r), (px, py, c))
                else:
                    send = _remote(ins[i].at[layer, k, mine, :], outs[i].at[layer, k, mine, :], *sems, (i, r), (x, y, 1 - c))
                    land = outs[i].at[layer, k, other, :]
                    recv = _remote(land, land, *sems, (i, r), (x, y, 1 - c))
                pairs.append((send, recv))
        return pairs

    return _Comm(bufs, [jax.ShapeDtypeStruct(b.shape, b.dtype) for b in bufs], {i: i for i in range(n)},
                 _pair_sems(n, 3), copies)


def _reduce_sibling_comm(gs):
    n = len(gs)

    def copies(ins, outs, sems):
        x, y, c = _mesh_pos()
        pairs = []
        for i in range(n):
            h = gs[i].shape[1] // 2
            cp = _remote(ins[i].at[:, pl.ds((1 - c) * h, h), :], outs[i], *sems, i, (x, y, 1 - c))
            pairs.append((cp, cp))
        return pairs

    return _Comm(gs, [jax.ShapeDtypeStruct((g.shape[0], g.shape[1] // 2, g.shape[2]), g.dtype) for g in gs], {},
                 _pair_sems(n), copies)


def _reduce_chips_comm(ps):
    n = len(ps)

    def copies(ins, outs, sems):
        x, y, c = _mesh_pos()
        pairs = []
        for i in range(n):
            for r, (px, py, k) in enumerate(_peers(x, y)):
                cp = _remote(ins[i].at[k], outs[i].at[r], *sems, (i, r), (px, py, c))
                pairs.append((cp, cp))
        return pairs

    return _Comm(ps, [jax.ShapeDtypeStruct((3,) + p.shape[1:], p.dtype) for p in ps], {}, _pair_sems(n, 3), copies)


def _share_halves(bufs, *, name):
    n = len(bufs)

    def body(*refs):
        ins, outs = refs[:n], refs[n:2 * n]
        send_sems, recv_sems = refs[2 * n:]
        x, y, c = _mesh_pos()
        cps = []
        for i in range(n):
            h = bufs[i].shape[1] // 2
            mine = pl.ds(c * h, h)
            cp = _remote(ins[i].at[:, mine, :], outs[i].at[:, mine, :], send_sems, recv_sems, i, (x, y, 1 - c))
            cp.start()
            cps.append(cp)
        for cp in cps:
            cp.wait()

    return pl.pallas_call(
        body, name=name, in_specs=[ANY] * n, out_specs=[ANY] * n,
        out_shape=[jax.ShapeDtypeStruct(b.shape, b.dtype) for b in bufs],
        input_output_aliases={i: i for i in range(n)},
        scratch_shapes=[pltpu.SemaphoreType.DMA((n,)), pltpu.SemaphoreType.DMA((n,))])(*bufs)


def _add_own_half(full, recv, pos, *, name, rows=256):
    k4, h, n = recv.shape
    tr = _tile(h, rows, 16)
    nblk = h // tr

    def body(pos_ref, a_ref, b_ref, o_ref):
        o_ref[...] = (a_ref[...] + b_ref[...]).astype(BF16)

    grid_spec = pltpu.PrefetchScalarGridSpec(
        num_scalar_prefetch=1, grid=(k4, nblk),
        in_specs=[pl.BlockSpec((None, tr, n), lambda k, i, pos_ref: (k, pos_ref[1] * nblk + i, 0)),
                  pl.BlockSpec((None, tr, n), lambda k, i, pos_ref: (k, i, 0))],
        out_specs=pl.BlockSpec((None, tr, n), lambda k, i, pos_ref: (k, i, 0)))
    return pl.pallas_call(body, name=name, grid_spec=grid_spec, out_shape=jax.ShapeDtypeStruct(recv.shape, BF16),
                          compiler_params=_params(("parallel", "parallel")))(pos, full, recv)


def _sum_into(own, others, buf, pos, layer, *, name, rows=256):
    _, h, n = own.shape
    tr = _tile(h, rows, 16)
    nblk = h // tr

    def body(pos_ref, own_ref, oth_ref, _, o_ref):
        acc = own_ref[...].astype(F32)
        for r in range(3):
            acc = acc + oth_ref[r].astype(F32)
        o_ref[...] = acc

    grid_spec = pltpu.PrefetchScalarGridSpec(
        num_scalar_prefetch=1, grid=(nblk,),
        in_specs=[pl.BlockSpec((None, tr, n), lambda i, pos_ref: (pos_ref[0], i, 0)),
                  pl.BlockSpec((3, tr, n), lambda i, pos_ref: (0, i, 0)), ANY],
        out_specs=pl.BlockSpec((None, tr, n), lambda i, pos_ref: (layer, pos_ref[1] * nblk + i, 0)))
    return pl.pallas_call(body, name=name, grid_spec=grid_spec, out_shape=jax.ShapeDtypeStruct(buf.shape, F32),
                          input_output_aliases={3: 0}, compiler_params=_params(("parallel",)))(pos, own, others, buf)


def _pair_comm(buf):
    def copies(ins, outs, sems):
        x, y, c = _mesh_pos()
        land = outs[0].at[1 - c]
        return [(_remote(ins[0].at[c], outs[0].at[c], *sems, 0, (x, y, 1 - c)),
                 _remote(land, land, *sems, 0, (x, y, 1 - c)))]

    return _Comm([buf], [jax.ShapeDtypeStruct(buf.shape, buf.dtype)], {0: 0}, _pair_sems(1), copies)


def _quad_comm(buf):
    def copies(ins, outs, sems):
        x, y, c = _mesh_pos()
        me = 2 * x + y
        pairs = []
        for r, (px, py, k) in enumerate(_peers(x, y)):
            land = outs[0].at[k]
            pairs.append((_remote(ins[0].at[me], outs[0].at[me], *sems, r, (px, py, c)),
                          _remote(land, land, *sems, r, (px, py, c))))
        return pairs

    return _Comm([buf], [jax.ShapeDtypeStruct(buf.shape, buf.dtype)], {0: 0}, _pair_sems(3), copies)


def _chip_bcast(buf, *, name):
    def body(src_ref, out_ref, send_sems, recv_sems, local_sem):
        x, y, c = _mesh_pos()
        me = 2 * x + y
        local = pltpu.make_async_copy(src_ref, out_ref.at[me], local_sem)
        local.start()
        sends = []
        for r, (px, py, _) in enumerate(_peers(x, y)):
            cp = _remote(src_ref, out_ref.at[me], send_sems, recv_sems, r, (px, py, c))
            cp.start()
            sends.append(cp)
        for r, (px, py, k) in enumerate(_peers(x, y)):
            _remote(src_ref, out_ref.at[k], send_sems, recv_sems, r, (px, py, c)).wait_recv()
        for cp in sends:
            cp.wait_send()
        local.wait()

    return pl.pallas_call(
        body, name=name, in_specs=[ANY], out_specs=ANY, out_shape=jax.ShapeDtypeStruct((4,) + buf.shape, buf.dtype),
        scratch_shapes=[pltpu.SemaphoreType.DMA((3,)), pltpu.SemaphoreType.DMA((3,)), pltpu.SemaphoreType.DMA])(buf)


def _sum_slots(buf, *, name, rows=384):
    r, n = buf.shape[-2:]
    k = int(np.prod(buf.shape[:-2]))
    tr = _tile(r, rows, SUBLANES)

    def body(b_ref, o_ref):
        acc = b_ref[0]
        for s in range(1, k):
            acc = acc + b_ref[s]
        o_ref[...] = acc

    return pl.pallas_call(
        body, name=name, grid=(r // tr,), in_specs=[pl.BlockSpec((k, tr, n), lambda i: (0, i, 0))],
        out_specs=pl.BlockSpec((tr, n), lambda i: (i, 0)), out_shape=jax.ShapeDtypeStruct((r, n), F32),
        compiler_params=_params(("parallel",)))(buf.reshape((k, r, n)))


ROW = 1024
BIG = (("w_in", 2), ("w_out", 1), ("w_cq", 1), ("w_ck", 1), ("w_cv", 1), ("w_co", 2), ("w_up", 2), ("w_down", 1))
CONV = ("lru_conv_w", "ffn_conv_w")
REPLICATED = ("norm_mix_g", "b_forget", "lru_conv_b", "lru_w_a", "lru_b_a", "lru_w_x", "lru_b_x", "lru_lambda",
              "norm_cross_g", "norm_mem_g", "norm_ffn_g", "ffn_conv_b", "rel_bias", "final_norm_g")
WEIGHTS = ('norm_mix_g', 'w_in', 'b_forget', 'lru_conv_w', 'lru_conv_b', 'lru_w_a', 'lru_b_a', 'lru_w_x', 'lru_b_x',
           'lru_lambda', 'w_out', 'norm_cross_g', 'norm_mem_g', 'w_cq', 'w_ck', 'w_cv', 'w_co', 'norm_ffn_g', 'w_up',
           'ffn_conv_w', 'ffn_conv_b', 'w_down', 'rel_bias', 'final_norm_g')
INPUTS = ("x", "mem") + WEIGHTS + ("loss_target",) + tuple("m_" + n for n in WEIGHTS) + tuple("v_" + n for n in WEIGHTS)


def _round_up(n, m):
    return -(-n // m) * m


class _Packing:
    def __init__(self, entries):
        self.entries, self.off = entries, {}
        o = 0
        for name, shape in entries:
            self.off[name] = o
            o += _round_up(int(np.prod(shape)), ROW)
        self.used = o
        self.rows = _round_up(o // ROW, SUBLANES)

    def pack(self, arrays):
        parts = []
        for name, shape in self.entries:
            n = int(np.prod(shape))
            parts.append(jnp.pad(arrays[name].reshape(n), (0, _round_up(n, ROW) - n)))
        tail = self.rows * ROW - self.used
        if tail:
            parts.append(jnp.zeros((tail,), F32))
        return jnp.concatenate(parts).reshape(self.rows, ROW)

    def unpack(self, flat, lead=()):
        out = {}
        for name, shape in self.entries:
            n = int(np.prod(shape))
            r0, nr = self.off[name] // ROW, _round_up(n, ROW) // ROW
            rows = lax.slice_in_dim(flat, r0, r0 + nr, axis=len(lead)).reshape(lead + (nr * ROW,))
            out[name] = lax.slice_in_dim(rows, 0, n, axis=len(lead)).reshape(lead + tuple(shape))
        return out


def _to_shards(g, axis):
    r, c = g.shape
    if axis == 1:
        return g.reshape(4, r // 4, c)
    return g.reshape(r, 4, c // 4).transpose(1, 0, 2)


def _from_shards(s, axis):
    _, r, c = s.shape
    if axis == 1:
        return s.reshape(4 * r, c)
    return s.transpose(1, 0, 2).reshape(r, 4 * c)


def _proj_blocks():
    blocks = []
    for mixer in (MIX_SB, MIX_FOX, MIX_DIL):
        for p in range(2):
            blocks += [ORIG_COL[mixer] + part * 2 * LANES + p * LANES for part in range(3)]
    for hf in range(2):
        blocks += [ORIG_LRU_X + hf * LANES, ORIG_LRU_G + hf * LANES]
    return blocks


def _pad_w_in(w):
    parts = [w[..., s:s + LANES] for s in _proj_blocks()]
    parts += [w[..., 1536:1540], jnp.zeros(w.shape[:-1] + (PROJ_W - COL_F - N_HEADS,), w.dtype)]
    return jnp.concatenate(parts, axis=-1)


def _unpad_w_in(wp):
    blocks = _proj_blocks()
    order = sorted(range(len(blocks)), key=lambda i: blocks[i])
    parts = []
    for i in order:
        if blocks[i] == ORIG_COL[MIX_DIL]:
            parts.append(wp[..., COL_F:COL_F + N_HEADS])
        parts.append(wp[..., i * LANES:(i + 1) * LANES])
    return jnp.concatenate(parts, axis=-1)


def _block_diag(w):
    z = jnp.zeros((HEAD_DIM, HEAD_DIM), w.dtype)
    half = lambda a, b: jnp.concatenate([jnp.concatenate([a, z], 1), jnp.concatenate([z, b], 1)], 0)
    return jnp.stack([half(w[0], w[1]), half(w[2], w[3])])


def _block_diag_grad(d):
    return jnp.stack([d[0, :HEAD_DIM, :HEAD_DIM], d[0, HEAD_DIM:, HEAD_DIM:],
                      d[1, :HEAD_DIM, :HEAD_DIM], d[1, HEAD_DIM:, HEAD_DIM:]])


def _fox_layouts(cum, nb, seq):
    tk = min(ATT_TILE, seq)
    col = cum.reshape(nb, 2, 2, seq).transpose(0, 1, 3, 2)
    row = cum.reshape(nb, 2, 2, seq // tk, tk).transpose(0, 1, 3, 2, 4)
    return col, row


def _layer_params(w, l, nb):
    lru_vec = jnp.concatenate([w["lru_conv_w"][l], w["lru_conv_b"][l][None], w["lru_b_a"][l][None],
                               w["lru_b_x"][l][None], w["lru_lambda"][l][None]], axis=0)
    ffn_cw = jnp.concatenate([w["ffn_conv_w"][l], w["ffn_conv_b"][l][None],
                              jnp.zeros((SUBLANES - 4, 2 * D_FF), F32)], axis=0)
    return dict(
        w_in=w["w_in_padded"][l], lru_vec=lru_vec,
        wa=_block_diag(w["lru_w_a"][l]).astype(BF16), wx=_block_diag(w["lru_w_x"][l]).astype(BF16),
        ffn_cw=ffn_cw, b_rows=jnp.tile(w["b_forget"][l], nb).reshape(nb * N_HEADS, 1))


NORM_ROWS = 512


def _merge_comms(comms):
    if len(comms) == 1:
        return comms[0]
    operands, out_shape, aliases, sems, spans = [], [], {}, [], []
    for cm in comms:
        aliases.update({len(operands) + i: len(out_shape) + j for i, j in cm.aliases.items()})
        spans.append((len(operands), len(out_shape), len(sems)))
        operands += list(cm.operands)
        out_shape += list(cm.out_shape)
        sems += list(cm.sems)

    def copies(ins, outs, sm):
        pairs = []
        for cm, (i0, o0, s0) in zip(comms, spans):
            pairs += cm.copies(ins[i0:i0 + len(cm.operands)], outs[o0:o0 + len(cm.out_shape)], sm[s0:s0 + len(cm.sems)])
        return pairs

    return _Comm(operands, out_shape, aliases, sems, copies)


GATHER_FIRST = ("w_in",)
GATHER_MID = ("w_out", "w_cq", "w_ck", "w_cv", "w_co")
GATHER_LAST = ("w_up", "w_down")


class _WeightGather:
    def __init__(self, slots, w, depth):
        self.slots, self.w, self.depth = slots, w, depth

    def plan(self, l, key):
        nxt = l + 1 if l + 1 < self.depth else None
        early = GATHER_FIRST + GATHER_MID
        if l == 0:
            table = {"proj": [(GATHER_MID, 0, 0)],
                     "sb_fwd": [(GATHER_MID, 0, 1), (GATHER_LAST, 0, 0)],
                     "fox_fwd": [(GATHER_LAST, 0, 1)] + ([(early, nxt, 0)] if nxt else []),
                     "dil_fwd": [(early, nxt, 1), (GATHER_LAST, nxt, 0)] if nxt else [],
                     "out": [(GATHER_LAST, nxt, 1)] if nxt else []}
        else:
            everything = early + GATHER_LAST
            table = {"sb_fwd": [(everything, nxt, 0)], "fox_fwd": [(everything, nxt, 1)]} if nxt else {}
        return table.get(key, [])

    def comm(self, l, key):
        entries = self.plan(l, key)
        if not entries:
            return None
        return _merge_comms([_gather_comm([self.slots[n] for n in names], layer, stage)
                             for names, layer, stage in entries])

    def done(self, l, key, landed):
        landed = list(landed)
        for names, layer, stage in self.plan(l, key):
            for n in names:
                self.slots[n] = landed.pop(0)
            if stage == 1:
                self.take(names, layer)

    def take(self, names, layer):
        for n, axis in BIG:
            if n in names:
                self.w[n][layer] = self.slots[n][layer] if n == "w_up" else _from_shards(self.slots[n][layer], axis)
        if "w_in" in names:
            self.w["w_in_padded"][layer] = _pad_w_in(self.w["w_in"][layer])

    def first_comm(self, stage):
        return _gather_comm([self.slots[n] for n in GATHER_FIRST], 0, stage)

    def first_done(self, stage, landed):
        self.slots.update(zip(GATHER_FIRST, landed))
        if stage == 1:
            self.take(GATHER_FIRST, 0)


def _layer_fwd(x, h, mem, w, lp, l, next_g, bias, nb, gather):
    t, d = x.shape
    seq = t // nb
    tag = f"l{l}"
    sv = dict(x0=x)

    def carrying(key, fn):
        comm = gather.comm(l, key)
        res = fn(comm)
        if comm is not None:
            res, landed = res
            gather.done(l, key, landed)
        return res

    proj = carrying("proj", lambda cm: _mm(h, lp["w_in"], name=tag + "_proj", comm=cm))
    mixed, ltot = carrying("sb_fwd", lambda cm: _sb_attn_fwd(proj, lax.empty((t, d), F32), nb=nb,
                                                             name=tag + "_sb_fwd", comm=cm))
    f_rows = proj[:, COL_F:COL_F + N_HEADS].reshape(nb, seq, N_HEADS).transpose(0, 2, 1).reshape(nb * N_HEADS, seq)
    cum_col, cum_row = _fox_layouts(_fox_gate_fwd(f_rows, lp["b_rows"]), nb, seq)
    mixed, lse_fox = carrying("fox_fwd", lambda cm: _softmax_attn_fwd(
        proj, nb=nb, mode="fox", mixer=MIX_FOX, out_buf=mixed, extra=(cum_col, cum_row), name=tag + "_fox_fwd", comm=cm))
    mixed, lse_dil = carrying("dil_fwd", lambda cm: _softmax_attn_fwd(
        proj, nb=nb, mode="dil", mixer=MIX_DIL, out_buf=mixed, extra=(bias,), name=tag + "_dil_fwd", comm=cm))
    mixed = _lru_fwd(proj, lp["lru_vec"], lp["wa"], lp["wx"], mixed, nb=nb, name=tag + "_lru_fwd")
    x1, hq = carrying("out", lambda cm: _mm(mixed, w["w_out"][l], res=x, norm_g=w["norm_cross_g"][l], ti=NORM_ROWS,
                                            name=tag + "_out", comm=cm))
    memn = _rmsnorm(mem, w["norm_mem_g"][l], name=tag + "_norm_mem")
    q = _mm(hq, w["w_cq"][l], name=tag + "_cq")
    k = _mm(memn, w["w_ck"][l], name=tag + "_ck")
    v = _mm(memn, w["w_cv"][l], name=tag + "_cv")
    oc, lse_c = _softmax_attn_fwd((q, k, v), nb=nb, mode="cross", name=tag + "_cross_fwd")
    x2, hn = _mm(oc, w["w_co"][l], res=x1, norm_g=w["norm_ffn_g"][l], ti=NORM_ROWS, name=tag + "_co")
    hf = _mm(hn, w["w_up"][l], b_chunks=4, name=tag + "_up")
    act = _ffn_act(hf, lp["ffn_cw"], seq=seq, name=tag + "_ffn_act")
    if next_g is None:
        x3, h_next = _mm(act, w["w_down"][l], res=x2, name=tag + "_down"), None
    else:
        x3, h_next = _mm(act, w["w_down"][l], res=x2, norm_g=next_g, ti=NORM_ROWS, name=tag + "_down")
    sv.update(h=h, proj=proj, ltot=ltot, f_rows=f_rows, cum_col=cum_col, cum_row=cum_row, lse_fox=lse_fox,
              lse_dil=lse_dil, mixed=mixed, x1=x1, hq=hq, memn=memn, q=q, k=k, v=v, oc=oc, lse_c=lse_c, x2=x2,
              hn=hn, hf=hf, act=act)
    return x3, h_next, sv


class _PendingReduce:
    def __init__(self, full, pos, layer):
        self.names, self.full, self.pos, self.layer = list(full), list(full.values()), pos, layer

    def sibling_comm(self):
        return _reduce_sibling_comm(self.full)

    def add(self, from_sibling):
        self.partial = [_add_own_half(f, r, self.pos, name=f"l{self.layer}_reduce_add_{n}")
                        for f, r, n in zip(self.full, from_sibling, self.names)]

    def chips_comm(self):
        return _reduce_chips_comm(self.partial)

    def finish(self, others, g_shard):
        g_shard = dict(g_shard)
        for p, o, n in zip(self.partial, others, self.names):
            g_shard[n] = _sum_into(p, o, g_shard[n], self.pos, self.layer, name=f"l{self.layer}_reduce_sum_{n}")
        return g_shard


def _layer_bwd(dx3, mem, sv, w, lp, l, bias, nb, pos, pending=None, g_shard=None, reduce_early=False):
    t = dx3.shape[0]
    seq = t // nb
    tag = f"l{l}"
    g = {}
    down_rows = _tile(sv["act"].shape[1], 1408)
    if pending is None:
        g["w_down"] = _mm(sv["act"], dx3, ta=True, ti=down_rows, name=tag + "_dw_down")
    else:
        g["w_down"], from_sibling = _mm(sv["act"], dx3, ta=True, ti=down_rows, comm=pending.sibling_comm(),
                                        name=tag + "_dw_down")
        pending.add(from_sibling)
    dact = _mm(dx3, w["w_down"][l], tb=True, name=tag + "_dact")
    if pending is None:
        dhf, dcu, dcg = _ffn_bwd(sv["hf"], lp["ffn_cw"], dact, seq=seq, name=tag + "_ffn_bwd")
    else:
        (dhf, dcu, dcg), others = _ffn_bwd(sv["hf"], lp["ffn_cw"], dact, seq=seq, name=tag + "_ffn_bwd",
                                           comm=pending.chips_comm())
        g_shard = pending.finish(others, g_shard)
    dcw = jnp.concatenate([dcu, dcg], axis=1)
    g["ffn_conv_w"], g["ffn_conv_b"] = dcw[:3], dcw[3]
    g["w_up"] = _mm(sv["hn"], dhf, ta=True, halves="b", col_shards=4, name=tag + "_dw_up")
    early = _PendingReduce(_big_grad_shards(g, EARLY), pos, l) if reduce_early else None
    res = _mm(dhf, w["w_up"][l], tb=True, halves="a", b_chunks=4, norm_bwd=(sv["x2"], w["norm_ffn_g"][l], dx3), ti=NORM_ROWS,
              comm=early.sibling_comm() if early else None, name=tag + "_dhn")
    if early:
        res, from_sibling = res
        early.add(from_sibling)
    dx2, dg = res
    g["norm_ffn_g"] = dg.reshape(-1)
    g["w_co"] = _mm(sv["oc"], dx2, ta=True, col_shards=4, name=tag + "_dw_co")
    doc = _mm(dx2, w["w_co"][l], tb=True, name=tag + "_doc")
    dq, dk, dv = _softmax_attn_bwd((sv["q"], sv["k"], sv["v"]), sv["oc"], sv["lse_c"], doc, nb=nb, mode="cross",
                                   name=tag + "_cross_bwd")
    g["w_cq"] = _mm(sv["hq"], dq, ta=True, name=tag + "_dw_cq")
    g["w_ck"] = _mm(sv["memn"], dk, ta=True, name=tag + "_dw_ck")
    g["w_cv"] = _mm(sv["memn"], dv, ta=True, name=tag + "_dw_cv")
    dx1, dg = _mm(dq, w["w_cq"][l], tb=True, norm_bwd=(sv["x1"], w["norm_cross_g"][l], dx2), ti=NORM_ROWS,
                  name=tag + "_dhq")
    g["norm_cross_g"] = dg.reshape(-1)
    dmemn = _mm(dv, w["w_cv"][l], tb=True, res=_mm(dk, w["w_ck"][l], tb=True, name=tag + "_dmem_k"),
                name=tag + "_dmem_v")
    _, g["norm_mem_g"] = _rmsnorm_bwd(dmemn, mem, w["norm_mem_g"][l], None, name=tag + "_norm_mem_bwd")
    mixed, proj = sv["mixed"], sv["proj"]
    g["w_out"] = _mm(mixed, dx1, ta=True, name=tag + "_dw_out")
    mid = _PendingReduce(_big_grad_shards(g, GATHER_MID), pos, l) if reduce_early else None
    dmixed = _mm(dx1, w["w_out"][l], tb=True, name=tag + "_dmixed", comm=mid.sibling_comm() if mid else None)
    if mid:
        dmixed, from_sibling = dmixed
        mid.add(from_sibling)
    dproj = _sb_attn_bwd(proj, sv["ltot"], dmixed, lax.empty((t, PROJ_W), F32), nb=nb, name=tag + "_sb_bwd",
                         comm=early.chips_comm() if early else None)
    if early:
        dproj, others = dproj
        g_shard = early.finish(others, g_shard)
    res = _softmax_attn_bwd(
        proj, mixed, sv["lse_fox"], dmixed, nb=nb, mode="fox", mixer=MIX_FOX, dbuf=dproj,
        extra=(sv["cum_col"], sv["cum_row"]), name=tag + "_fox_bwd", comm=mid.chips_comm() if mid else None)
    if mid:
        res, others = res
        g_shard = mid.finish(others, g_shard)
    dproj, dcum_k, dcum_q = res
    dcum = (dcum_k.transpose(0, 1, 3, 2, 4).reshape(nb * N_HEADS, seq)
            + dcum_q.transpose(0, 1, 3, 2).reshape(nb * N_HEADS, seq))
    df_rows, db = _fox_gate_bwd(dcum, sv["f_rows"], lp["b_rows"])
    g["b_forget"] = db[:N_HEADS, 0]
    df = df_rows.reshape(nb, N_HEADS, seq).transpose(0, 2, 1).reshape(t, N_HEADS)
    dproj, dbias = _softmax_attn_bwd(proj, mixed, sv["lse_dil"], dmixed, nb=nb, mode="dil", mixer=MIX_DIL,
                                     dbuf=dproj, extra=(bias,), name=tag + "_dil_bwd")
    dproj, dvec, dwa, dwx = _lru_bwd(proj, lp["lru_vec"], lp["wa"], lp["wx"], dmixed, dproj, nb=nb,
                                     name=tag + "_lru_bwd")
    g["lru_conv_w"], g["lru_conv_b"], g["lru_b_a"], g["lru_b_x"], g["lru_lambda"] = (
        dvec[0:4], dvec[4], dvec[5], dvec[6], dvec[7])
    g["lru_w_a"], g["lru_w_x"] = _block_diag_grad(dwa), _block_diag_grad(dwx)
    dproj = lax.dynamic_update_slice(dproj, jnp.pad(df, ((0, 0), (0, PROJ_W - COL_F - N_HEADS))), (0, COL_F))
    g["w_in_padded"] = _mm(sv["h"], dproj, ta=True, name=tag + "_dw_in")
    dx0, dg = _mm(dproj, lp["w_in"], tb=True, norm_bwd=(sv["x0"], w["norm_mix_g"][l], dx1), ti=NORM_ROWS,
                  name=tag + "_dh")
    g["norm_mix_g"] = dg.reshape(-1)
    return dx0, g, dbias, g_shard


def _big_grad_shards(g, names):
    out = {}
    for n, axis in BIG:
        if n not in names:
            continue
        if n in ("w_up", "w_co"):
            out[n] = g[n]
        else:
            out[n] = _to_shards(_unpad_w_in(g["w_in_padded"]) if n == "w_in" else g[n], axis)
    return out


EARLY = ("w_down", "w_up")


def kernel(*args):
    a = dict(zip(INPUTS, args, strict=True))
    nb, seq, d = a["x"].shape
    depth = a["norm_mix_g"].shape[0]
    x = a["x"].reshape(nb * seq, d)
    mem = a["mem"].reshape(nb * a["mem"].shape[1], d)
    target = a["loss_target"].reshape(nb * seq, d)
    cx, cy, c = _mesh_pos()
    chip = 2 * cx + cy
    pos = jnp.stack([chip, c]).astype(jnp.int32)

    slots = {}
    for n, _ in BIG:
        own = a[n].astype(BF16)[:, None]
        slots[n] = lax.dynamic_update_slice(lax.empty((depth, 4) + own.shape[2:], BF16), own, (0, chip, 0, 0))
    w = {n: a[n] for n in REPLICATED}
    w.update({n: {} for n, _ in BIG}, w_in_padded={})
    gather = _WeightGather(slots, w, depth)
    bias, landed = _dil_bias(w["rel_bias"], seq, comm=gather.first_comm(0))
    gather.first_done(0, landed)
    h, landed = _rmsnorm(x, w["norm_mix_g"][0], name="l0_norm_mix", comm=gather.first_comm(1))
    gather.first_done(1, landed)
    cpk = _Packing([(n, a[n].shape) for n in CONV])
    conv = cpk.unpack(_chip_bcast(cpk.pack({n: a[n] for n in CONV}), name="gather_conv"), lead=(4,))
    for n in CONV:
        w[n] = jnp.moveaxis(conv[n], 0, 2).reshape(a[n].shape[:2] + (4 * a[n].shape[2],))

    lps, saved = [], []
    for l in range(depth):
        lps.append(_layer_params(w, l, nb))
        x, h, sv = _layer_fwd(x, h, mem, w, lps[l], l, w["norm_mix_g"][l + 1] if l + 1 < depth else None, bias, nb,
                              gather)
        saved.append(sv)
    loss, dx, dg_final = _loss_head(x, w["final_norm_g"], target)
    small_g = [None] * depth
    dbias, pending = None, None
    g_shard = {n: lax.empty(a[n].shape, F32) for n, _ in BIG}
    for l in reversed(range(depth)):
        bottom = l == 0
        dx, g, db, g_shard = _layer_bwd(dx, mem, saved[l], w, lps[l], l, bias, nb, pos, pending=pending,
                                        g_shard=g_shard, reduce_early=bottom)
        dbias = db if dbias is None else dbias + db
        small_g[l] = g
        left = [n for n, _ in BIG if not (bottom and n in EARLY + GATHER_MID)]
        pending = _PendingReduce(_big_grad_shards(g, left), pos, l)

    grads = {n: jnp.stack([small_g[l][n] for l in range(depth)]) for n in REPLICATED + CONV
             if n not in ("rel_bias", "final_norm_g")}
    grads["rel_bias"] = _dil_bias_bwd(dbias, seq)
    grads["final_norm_g"] = dg_final
    grads["loss"] = loss.reshape(1)
    spk = _Packing([(n, grads[n].shape) for n in REPLICATED + CONV + ("loss",)])
    s_flat = spk.pack(grads)
    pair = lax.dynamic_update_slice(lax.empty((2,) + s_flat.shape, F32), s_flat[None], (c, 0, 0))
    *from_sibling, pair = _run_comm(_merge_comms([pending.sibling_comm(), _pair_comm(pair)]), name="tail_sibling")
    pending.add(from_sibling)
    quad = lax.dynamic_update_slice(lax.empty((4,) + pair.shape, F32), pair[None], (chip, 0, 0, 0))
    *others, quad = _run_comm(_merge_comms([pending.chips_comm(), _quad_comm(quad)]), name="tail_chips")
    g_shard = pending.finish(others, g_shard)
    names = [n for n, _ in BIG]
    g_shard = dict(zip(names, _share_halves([g_shard[n] for n in names], name="reduce_share")))
    out = {}
    for n in names:
        operands = (a[n], g_shard[n], a["m_" + n], a["v_" + n])
        cols = a[n].shape[2]
        if cols % LANES:
            lead = max(b for b in range(1, 65) if cols % b == 0)
            res = _adamw(*[z.transpose(2, 0, 1) for z in operands], name="adamw_" + n, lead=lead)
            delta, new_m, new_v = [z.transpose(1, 2, 0) for z in res]
        else:
            delta, new_m, new_v = _adamw(*operands, name="adamw_" + n)
        out[n] = (g_shard[n], delta, new_m, new_v)
    total = spk.unpack(_sum_slots(quad, name="small_sum"))
    for n in CONV:
        width = a[n].shape[2]
        total[n] = lax.dynamic_slice_in_dim(total[n], chip * width, width, axis=2)
    apk = _Packing([(n, a[n].shape) for n in REPLICATED + CONV])
    s_out = _adamw(*[apk.pack(src)[None] for src in (
        {n: a[n] for n in REPLICATED + CONV}, total, {n: a["m_" + n] for n in REPLICATED + CONV},
        {n: a["v_" + n] for n in REPLICATED + CONV})], name="adamw_small")
    s_delta, s_m, s_v = [apk.unpack(o[0]) for o in s_out]
    for n in REPLICATED + CONV:
        out[n] = (total[n], s_delta[n], s_m[n], s_v[n])

    return (total["loss"].reshape(()), dx.reshape(nb, seq, d), *[out[n][0] for n in WEIGHTS],
            *[out[n][1] for n in WEIGHTS], *[out[n][2] for n in WEIGHTS], *[out[n][3] for n in WEIGHTS])
```

```python
import math

import numpy as np
import jax
import jax.numpy as jnp
from jax import lax
from jax.experimental import pallas as pl
from jax.experimental.pallas import tpu as pltpu

F32 = jnp.float32
BF16 = jnp.bfloat16

HEAD_DIM = 64
N_HEADS = 4
N_IN = 2820
D_FF = 2816
LRU_C = 8.0
EPS = 1e-6
NUM_BUCKETS = 32
MAX_DISTANCE = 2048
DILATED_PATTERNS = ((128, 1), (512, 4), (2048, 16))
ADAM_LR, ADAM_B1, ADAM_B2, ADAM_EPS, ADAM_WD, ADAM_STEP = 0.001, 0.9, 0.999, 1e-08, 0.01, 10

LANES = 128
SUBLANES = 8
VMEM_LIMIT = 48 * 1024 * 1024

PROJ_W = 3072
PAIR_W = 3 * LANES
LRU_W = 2 * LANES
COL_LRU = 6 * PAIR_W
COL_F = COL_LRU + 2 * LRU_W
MIX_SB, MIX_FOX, MIX_DIL, MIX_LRU = 0, 1, 2, 3
ORIG_COL = {MIX_SB: 0, MIX_FOX: 768, MIX_DIL: 1540}
ORIG_LRU_X, ORIG_LRU_G = 2308, 2564

ATT_TILE = 256
MASKED = -1e30
SCALE = HEAD_DIM ** -0.5

NT_DIMS = (((1,), (1,)), ((), ()))
TN_DIMS = (((0,), (0,)), ((), ()))

MESH = pl.DeviceIdType.MESH
ANY = pl.BlockSpec(memory_space=pl.ANY)


def _params(sem):
    return pltpu.CompilerParams(dimension_semantics=sem, vmem_limit_bytes=VMEM_LIMIT)


def _tile(n, target, unit=LANES):
    if n <= target:
        return n
    t = (target // unit) * unit
    while t > unit and n % t:
        t -= unit
    assert n % t == 0, (n, target, unit)
    return t


def _mm(a, b, *, ta=False, tb=False, res=None, col_shards=1, halves=None, b_chunks=1, norm_g=None, norm_bwd=None,
        comm=None, name, ti=1024, tj=1408, tc=1408):
    if halves == "a":
        m, kc = a.shape[1], 2 * a.shape[2]
    else:
        m, kc = (a.shape[1], a.shape[0]) if ta else a.shape
    n_blk, k_blk = None, kc // 2 if halves == "a" else kc
    if halves == "b":
        n = 2 * b.shape[2]
        assert b.shape[1] == kc
        n_blk = n // 2
    elif b_chunks > 1:
        n = b.shape[1] if tb else b_chunks * b.shape[2]
        assert (b_chunks * b.shape[2] if tb else b.shape[1]) == kc
        if tb:
            k_blk = min(k_blk, kc // b_chunks)
        else:
            n_blk = n // b_chunks
    else:
        n = b.shape[0] if tb else b.shape[1]
        assert (b.shape[1] if tb else b.shape[0]) == kc
    assert n % col_shards == 0
    n_blk = min(n_blk or n, n // col_shards)
    ti, tj, tc = (_tile(m, ti, LANES if ta else SUBLANES), _tile(n_blk, tj),
                  _tile(k_blk, tc, SUBLANES if ta and tb else LANES))
    per_shard, per_half_j, per_half_k = n // col_shards // tj, n // 2 // tj, kc // 2 // tc
    per_chunk = (kc if tb else n) // b_chunks // (tc if tb else tj)
    nk = kc // tc
    dims = (((0 if ta else 1,), (1 if tb else 0,)), ((), ()))
    rows_whole = norm_g is not None or norm_bwd is not None
    assert not rows_whole or (tj == n and col_shards == 1)
    n_extra = (res is not None) + (norm_g is not None) + (3 if norm_bwd is not None else 0)
    n_out = 2 if rows_whole else 1

    def finish(val, ex, outs):
        if res is not None:
            val = ex[0][...] + val
        if norm_g is not None:
            outs[0][...] = val
            outs[1][...] = (_xhat(val) * ex[-1][...]).astype(BF16)
        elif norm_bwd is not None:
            x_ref, g_ref, r_ref = ex[-3:]
            dx, dgr = _norm_bwd_rows(val, x_ref[...], g_ref[...])
            outs[0][...] = r_ref[...] + dx

            @pl.when(pl.program_id(0) == 0)
            def _():
                outs[1][...] = jnp.zeros_like(outs[1])

            outs[1][...] += jnp.sum(dgr, axis=0, keepdims=True)
        else:
            outs[0][...] = val

    def body(*refs):
        a_ref, b_ref = refs[:2]
        ex = refs[2:2 + n_extra]
        outs = refs[2 + n_extra:2 + n_extra + n_out]
        part = lax.dot_general(a_ref[...].astype(BF16), b_ref[...].astype(BF16), dims, preferred_element_type=F32)
        if nk == 1:
            finish(part, ex, outs)
            return
        acc_ref = refs[-1]
        k = pl.program_id(2)

        @pl.when(k == 0)
        def _():
            acc_ref[...] = part

        @pl.when(k > 0)
        def _():
            acc_ref[...] += part

        @pl.when(k == nk - 1)
        def _():
            finish(acc_ref[...], ex, outs)

    if halves == "a":
        a_spec = pl.BlockSpec((None, ti, tc), lambda i, j, k: (k // per_half_k, i, k % per_half_k))
    elif ta:
        a_spec = pl.BlockSpec((tc, ti), lambda i, j, k: (k, i))
    else:
        a_spec = pl.BlockSpec((ti, tc), lambda i, j, k: (i, k))
    if halves == "b":
        b_spec = pl.BlockSpec((None, tc, tj), lambda i, j, k: (j // per_half_j, k, j % per_half_j))
    elif b_chunks > 1 and tb:
        b_spec = pl.BlockSpec((None, tj, tc), lambda i, j, k: (k // per_chunk, j, k % per_chunk))
    elif b_chunks > 1:
        b_spec = pl.BlockSpec((None, tc, tj), lambda i, j, k: (j // per_chunk, k, j % per_chunk))
    elif tb:
        b_spec = pl.BlockSpec((tj, tc), lambda i, j, k: (j, k))
    else:
        b_spec = pl.BlockSpec((tc, tj), lambda i, j, k: (k, j))
    o_spec = pl.BlockSpec((ti, tj), lambda i, j, k: (i, j))
    vec = pl.BlockSpec((1, tj), lambda i, j, k: (0, 0))
    in_specs, args = [a_spec, b_spec], [a, b]
    out_specs, out_shape = [o_spec], [jax.ShapeDtypeStruct((m, n), F32)]
    if res is not None:
        in_specs.append(o_spec)
        args.append(res)
    if norm_g is not None:
        in_specs.append(vec)
        args.append(norm_g.reshape(1, n))
        out_specs.append(o_spec)
        out_shape.append(jax.ShapeDtypeStruct((m, n), BF16))
    if norm_bwd is not None:
        x, g, dres = norm_bwd
        in_specs += [o_spec, vec, o_spec]
        args += [x, g.reshape(1, n), dres]
        out_specs.append(vec)
        out_shape.append(jax.ShapeDtypeStruct((1, n), F32))
    if col_shards > 1:
        assert n_extra == 0
        out_specs = [pl.BlockSpec((None, ti, tj), lambda i, j, k: (j // per_shard, i, j % per_shard))]
        out_shape = [jax.ShapeDtypeStruct((col_shards, m, n // col_shards), F32)]
    sem = ("arbitrary",) * 3 if norm_bwd is not None else ("parallel", "parallel", "arbitrary")
    out, carried = _pallas(body, name=name, grid=(m // ti, n // tj, nk), in_specs=in_specs, out_specs=out_specs,
                           out_shape=out_shape, args=args, scratch=[] if nk == 1 else [pltpu.VMEM((ti, tj), F32)],
                           sem=sem, comm=comm)
    out = out if rows_whole else out[0]
    return out if comm is None else (out, carried)


def _xhat(x):
    return x * lax.rsqrt(jnp.mean(x * x, axis=-1, keepdims=True) + EPS)


def _norm_bwd_rows(dy, x, g):
    rstd = lax.rsqrt(jnp.mean(x * x, axis=-1, keepdims=True) + EPS)
    xh = x * rstd
    dxh = dy * g
    dx = rstd * (dxh - xh * jnp.mean(dxh * xh, axis=-1, keepdims=True))
    return dx, dy * xh


def _rmsnorm(x, g, *, name, rows=512, comm=None):
    t, d = x.shape
    tr = _tile(t, rows, 2 * SUBLANES)

    def body(x_ref, g_ref, o_ref):
        o_ref[...] = (_xhat(x_ref[...]) * g_ref[...]).astype(BF16)

    out, carried = _pallas(
        body, name=name, grid=(t // tr,),
        in_specs=[pl.BlockSpec((tr, d), lambda i: (i, 0)), pl.BlockSpec((1, d), lambda i: (0, 0))],
        out_specs=[pl.BlockSpec((tr, d), lambda i: (i, 0))], out_shape=[jax.ShapeDtypeStruct((t, d), BF16)],
        args=[x, g.reshape(1, d)], sem=("parallel",), comm=comm)
    return out[0] if comm is None else (out[0], carried)


def _rmsnorm_bwd(dy, x, g, dres, *, name, rows=512):
    t, d = x.shape
    tr = _tile(t, rows, SUBLANES)

    def body(*refs):
        if dres is None:
            dy_ref, x_ref, g_ref, dx_ref, dg_ref = refs
        else:
            dy_ref, x_ref, g_ref, r_ref, dx_ref, dg_ref = refs
        dx, dgr = _norm_bwd_rows(dy_ref[...], x_ref[...], g_ref[...])
        dx_ref[...] = dx if dres is None else r_ref[...] + dx

        @pl.when(pl.program_id(0) == 0)
        def _():
            dg_ref[...] = jnp.zeros_like(dg_ref)

        dg_ref[...] += jnp.sum(dgr, axis=0, keepdims=True)

    row = pl.BlockSpec((tr, d), lambda i: (i, 0))
    vec = pl.BlockSpec((1, d), lambda i: (0, 0))
    in_specs = [row, row, vec] + ([] if dres is None else [row])
    args = (dy, x, g.reshape(1, d)) + (() if dres is None else (dres,))
    dx, dg = pl.pallas_call(
        body, name=name, grid=(t // tr,), in_specs=in_specs, out_specs=[row, vec],
        out_shape=[jax.ShapeDtypeStruct((t, d), F32), jax.ShapeDtypeStruct((1, d), F32)],
        compiler_params=_params(("arbitrary",)))(*args)
    return dx, dg.reshape(d)


def _loss_head(x, g, target, *, rows=512):
    t, d = x.shape
    tr = _tile(t, rows, SUBLANES)

    def body(x_ref, g_ref, t_ref, dx_ref, dg_ref, loss_ref):
        x_, g_ = x_ref[...], g_ref[...]
        err = _xhat(x_) * g_ - t_ref[...]
        dx, dgr = _norm_bwd_rows(err * (1.0 / d), x_, g_)
        dx_ref[...] = dx

        @pl.when(pl.program_id(0) == 0)
        def _():
            dg_ref[...] = jnp.zeros_like(dg_ref)
            loss_ref[...] = jnp.zeros_like(loss_ref)

        dg_ref[...] += jnp.sum(dgr, axis=0, keepdims=True)
        loss_ref[...] += 0.5 * jnp.sum(jnp.mean(err * err, axis=-1, keepdims=True), axis=0, keepdims=True)

    row = pl.BlockSpec((tr, d), lambda i: (i, 0))
    vec = pl.BlockSpec((1, d), lambda i: (0, 0))
    one = pl.BlockSpec((1, 1), lambda i: (0, 0))
    dx, dg, loss = pl.pallas_call(
        body, name="loss_head", grid=(t // tr,), in_specs=[row, vec, row], out_specs=[row, vec, one],
        out_shape=[jax.ShapeDtypeStruct((t, d), F32), jax.ShapeDtypeStruct((1, d), F32),
                   jax.ShapeDtypeStruct((1, 1), F32)],
        compiler_params=_params(("arbitrary",)))(x, g.reshape(1, d), target)
    return loss.reshape(()), dx, dg.reshape(d)


def _head_masks(shape):
    lane = lax.broadcasted_iota(jnp.int32, shape, len(shape) - 1)
    return lane < HEAD_DIM, lane >= HEAD_DIM


def _split_heads(x):
    m0, m1 = _head_masks(x.shape)
    zero = jnp.zeros_like(x)
    return jnp.where(m0, x, zero), jnp.where(m1, x, zero)


def _lane_pair(a0, a1, rows):
    m0, _ = _head_masks((rows, LANES))
    return jnp.where(m0, a0, a1)


def _qkv_readers(refs, packed):
    if packed:
        (r,) = refs
        return tuple((lambda r0, n, s=s: r[pl.ds(r0, n), s * LANES:(s + 1) * LANES]) for s in range(3))
    return tuple((lambda r0, n, ref=ref: ref[pl.ds(r0, n), :]) for ref in refs)


def _pair_spec(seq, col0, width=LANES):
    return pl.BlockSpec((seq, width), lambda p, b: (b, col0 + p))


def _fox_specs(seq, nk, tk):
    return [pl.BlockSpec((None, None, seq, 2), lambda p, b: (b, p, 0, 0)),
            pl.BlockSpec((None, None, nk, 2, tk), lambda p, b: (b, p, 0, 0, 0))]


def _softmax_attn_fwd(src, *, nb, mode, mixer=None, out_buf=None, extra=(), name, comm=None):
    packed = mode != "cross"
    n_src = 1 if packed else 3
    seq_q = (src if packed else src[0]).shape[0] // nb
    seq_k = seq_q if packed else src[1].shape[0] // nb
    tq, tk = min(ATT_TILE, seq_q), min(ATT_TILE, seq_k)
    nq, nk = seq_q // tq, seq_k // tk
    n_ex = len(extra)

    def body(*refs):
        q_at, k_at, v_at = _qkv_readers(refs[:n_src], packed)
        ex = refs[n_src:n_src + n_ex]
        o_ref, lse_ref = refs[-2:]

        def q_tile(i, _):
            r0 = pl.multiple_of(i * tq, tq)
            qm = _split_heads((q_at(r0, tq) * SCALE).astype(BF16))
            if mode == "fox":
                cq = ex[0][pl.ds(r0, tq), :]
                row = r0 + lax.broadcasted_iota(jnp.int32, (tq, tk), 0)

            def k_tile(j, carry, diagonal=False):
                m, l, acc = carry
                c0 = pl.multiple_of(j * tk, tk)
                kt = k_at(c0, tk).astype(BF16)
                vm = _split_heads(v_at(c0, tk).astype(BF16))
                if mode == "fox":
                    ck = ex[1][j]
                hs = range(2)
                s = [lax.dot_general(qm[h], kt, NT_DIMS, preferred_element_type=F32) for h in hs]
                if mode == "fox":
                    s = [s[h] + cq[:, h:h + 1] - ck[h:h + 1, :] for h in hs]
                    if diagonal:
                        keep = (c0 + lax.broadcasted_iota(jnp.int32, (tq, tk), 1)) <= row
                        s = [jnp.where(keep, s[h], MASKED) for h in hs]
                elif mode == "dil":
                    s = [s[h] + ex[0][h, i - j] for h in hs]
                new_m = [jnp.maximum(m[h], jnp.max(s[h], axis=-1, keepdims=True)) for h in hs]
                p = [jnp.exp(s[h] - new_m[h]) for h in hs]
                alpha = [jnp.exp(m[h] - new_m[h]) for h in hs]
                new_l = [alpha[h] * l[h] + jnp.sum(p[h], axis=-1, keepdims=True) for h in hs]
                pv = [jnp.dot(p[h].astype(BF16), vm[h], preferred_element_type=F32) for h in hs]
                acc = acc * _lane_pair(alpha[0], alpha[1], tq) + (pv[0] + pv[1])
                return tuple(new_m), tuple(new_l), acc

            init = ((jnp.full((tq, 1), MASKED, F32),) * 2, (jnp.zeros((tq, 1), F32),) * 2,
                    jnp.zeros((tq, LANES), F32))
            if mode == "fox":
                m, l, acc = k_tile(i, lax.fori_loop(0, i, k_tile, init), True)
            else:
                m, l, acc = lax.fori_loop(0, i + 1 if packed else nk, k_tile, init)
            o_ref[pl.ds(r0, tq), :] = acc / _lane_pair(l[0], l[1], tq)
            lse_ref[pl.ds(r0, tq), :] = _lane_pair(m[0] + jnp.log(l[0]), m[1] + jnp.log(l[1]), tq)
            return 0

        lax.fori_loop(0, nq, q_tile, 0)

    lse_shape = jax.ShapeDtypeStruct((nb * seq_q, 2 * LANES), F32)
    if packed:
        in_specs, args = [_pair_spec(seq_q, 2 * mixer, PAIR_W)], [src]
        in_specs += _fox_specs(seq_q, nk, tk) if mode == "fox" else [
            pl.BlockSpec((None, 2, nq, tq, tk), lambda p, b: (p, 0, 0, 0, 0))]
        args += list(extra) + [out_buf]
        in_specs.append(ANY)
        out_specs = [_pair_spec(seq_q, 2 * mixer), _pair_spec(seq_q, 0)]
        out_shape = [jax.ShapeDtypeStruct(out_buf.shape, F32), lse_shape]
        aliases = {len(args) - 1: 0}
    else:
        in_specs = [_pair_spec(seq_q, 0), _pair_spec(seq_k, 0), _pair_spec(seq_k, 0)]
        args = list(src)
        out_specs = [_pair_spec(seq_q, 0), _pair_spec(seq_q, 0)]
        out_shape = [lse_shape, lse_shape]
        aliases = {}
    out, carried = _pallas(body, name=name, grid=(2, nb), in_specs=in_specs, out_specs=out_specs, out_shape=out_shape,
                           args=args, aliases=aliases, sem=("parallel", "arbitrary"), comm=comm)
    return out if comm is None else (out, carried)


def _softmax_attn_bwd(src, o, lse, do, *, nb, mode, mixer=None, dbuf=None, extra=(), name, comm=None):
    packed = mode != "cross"
    n_src = 1 if packed else 3
    seq_q = (src if packed else src[0]).shape[0] // nb
    seq_k = seq_q if packed else src[1].shape[0] // nb
    tq, tk = min(ATT_TILE, seq_q), min(ATT_TILE, seq_k)
    nq, nk = seq_q // tq, seq_k // tk
    n_ex = len(extra)
    n_in = n_src + 3 + n_ex + (1 if packed else 0)

    def body(*refs):
        q_at, k_at, v_at = _qkv_readers(refs[:n_src], packed)
        o_ref, lse_ref, do_ref = refs[n_src:n_src + 3]
        ex = refs[n_src + 3:n_src + 3 + n_ex]
        outs = refs[n_in:]
        if packed:
            d_ref = outs[0]
            dq_w = lambda r0, val: d_ref.__setitem__((pl.ds(r0, tq), slice(0, LANES)), val)
            dk_ref = d_ref.at[:, LANES:2 * LANES]
            dv_ref = d_ref.at[:, 2 * LANES:3 * LANES]
        else:
            dq_ref, dk_ref, dv_ref = outs[:3]
            dq_w = lambda r0, val: dq_ref.__setitem__((pl.ds(r0, tq), slice(None)), val)
        dk_ref[...] = jnp.zeros((seq_k, LANES), F32)
        dv_ref[...] = jnp.zeros((seq_k, LANES), F32)
        if mode == "fox":
            dcum_ref, dcq_ref = outs[-2:]
            dcum_ref[...] = jnp.zeros_like(dcum_ref)
        if mode == "dil":
            dbias_ref = outs[-1]

            @pl.when(pl.program_id(1) == 0)
            def _():
                dbias_ref[...] = jnp.zeros_like(dbias_ref)

        def q_tile(i, _):
            r0 = pl.multiple_of(i * tq, tq)
            qm = _split_heads((q_at(r0, tq) * SCALE).astype(BF16))
            do_f = do_ref[pl.ds(r0, tq), :]
            dom = _split_heads(do_f.astype(BF16))
            dd = _split_heads(do_f * o_ref[pl.ds(r0, tq), :])
            delta = [jnp.sum(dd[h], axis=-1, keepdims=True) for h in range(2)]
            lse_t = lse_ref[pl.ds(r0, tq), :]
            lse_h = [lse_t[:, 0:1], lse_t[:, HEAD_DIM:HEAD_DIM + 1]]
            if mode == "fox":
                cq = ex[0][pl.ds(r0, tq), :]
                row = r0 + lax.broadcasted_iota(jnp.int32, (tq, tk), 0)

            def k_tile(j, carry, diagonal=False):
                dq, rs = carry
                c0 = pl.multiple_of(j * tk, tk)
                kt = k_at(c0, tk).astype(BF16)
                vt = v_at(c0, tk).astype(BF16)
                km = _split_heads(kt)
                if mode == "fox":
                    ck = ex[1][j]
                hs = range(2)
                s = [lax.dot_general(qm[h], kt, NT_DIMS, preferred_element_type=F32) for h in hs]
                dp = [lax.dot_general(dom[h], vt, NT_DIMS, preferred_element_type=F32) for h in hs]
                if mode == "fox":
                    s = [s[h] + cq[:, h:h + 1] - ck[h:h + 1, :] for h in hs]
                    if diagonal:
                        keep = (c0 + lax.broadcasted_iota(jnp.int32, (tq, tk), 1)) <= row
                        s = [jnp.where(keep, s[h], MASKED) for h in hs]
                elif mode == "dil":
                    s = [s[h] + ex[0][h, i - j] for h in hs]
                p = [jnp.exp(s[h] - lse_h[h]) for h in hs]
                ds = [p[h] * (dp[h] - delta[h]) for h in hs]
                dsb = [ds[h].astype(BF16) for h in hs]
                pb = [p[h].astype(BF16) for h in hs]
                dq = dq + (jnp.dot(dsb[0], km[0], preferred_element_type=F32)
                           + jnp.dot(dsb[1], km[1], preferred_element_type=F32))
                dk_t = (lax.dot_general(dsb[0], qm[0], TN_DIMS, preferred_element_type=F32)
                        + lax.dot_general(dsb[1], qm[1], TN_DIMS, preferred_element_type=F32))
                dv_t = (lax.dot_general(pb[0], dom[0], TN_DIMS, preferred_element_type=F32)
                        + lax.dot_general(pb[1], dom[1], TN_DIMS, preferred_element_type=F32))
                if mode == "fox":
                    for h in hs:
                        dcum_ref[j, h:h + 1, :] -= jnp.sum(ds[h], axis=0, keepdims=True)
                    rs = tuple(rs[h] + jnp.sum(ds[h], axis=-1, keepdims=True) for h in hs)
                elif mode == "dil":
                    for h in hs:
                        dbias_ref[h, i - j] += ds[h]
                dk_ref[pl.ds(c0, tk), :] += dk_t
                dv_ref[pl.ds(c0, tk), :] += dv_t
                return dq, rs

            zero = (jnp.zeros((tq, 1), F32),) * 2
            init = (jnp.zeros((tq, LANES), F32), zero)
            if mode == "fox":
                dq, rs = k_tile(i, lax.fori_loop(0, i, k_tile, init), True)
            else:
                dq, rs = lax.fori_loop(0, i + 1 if packed else nk, k_tile, init)
            dq_w(r0, dq * SCALE)
            if mode == "fox":
                dcq_ref[pl.ds(r0, tq), :] = jnp.where(lax.broadcasted_iota(jnp.int32, (tq, 2), 1) == 0, rs[0], rs[1])
            return 0

        lax.fori_loop(0, nq, q_tile, 0)

    if packed:
        in_specs = [_pair_spec(seq_q, 2 * mixer, PAIR_W), _pair_spec(seq_q, 2 * mixer), _pair_spec(seq_q, 0),
                    _pair_spec(seq_q, 2 * mixer)]
        args = [src, o, lse, do]
        out_specs = [_pair_spec(seq_q, 2 * mixer, PAIR_W)]
        out_shape = [jax.ShapeDtypeStruct(dbuf.shape, F32)]
        if mode == "fox":
            in_specs += _fox_specs(seq_q, nk, tk)
            out_specs += [_fox_specs(seq_q, nk, tk)[1], _fox_specs(seq_q, nk, tk)[0]]
            out_shape += [jax.ShapeDtypeStruct((nb, 2, nk, 2, tk), F32), jax.ShapeDtypeStruct((nb, 2, seq_q, 2), F32)]
        else:
            tiles = pl.BlockSpec((None, 2, nq, tq, tk), lambda p, b: (p, 0, 0, 0, 0))
            in_specs.append(tiles)
            out_specs.append(tiles)
            out_shape.append(jax.ShapeDtypeStruct((2, 2, nq, tq, tk), F32))
        args += list(extra) + [dbuf]
        in_specs.append(ANY)
        aliases = {len(args) - 1: 0}
    else:
        sq, sk = _pair_spec(seq_q, 0), _pair_spec(seq_k, 0)
        in_specs, args = [sq, sk, sk, sq, sq, sq], list(src) + [o, lse, do]
        out_specs = [sq, sk, sk]
        out_shape = [jax.ShapeDtypeStruct((nb * seq_q, 2 * LANES), F32)] + [
            jax.ShapeDtypeStruct((nb * seq_k, 2 * LANES), F32)] * 2
        aliases = {}
    out, carried = _pallas(body, name=name, grid=(2, nb), in_specs=in_specs, out_specs=out_specs, out_shape=out_shape,
                           args=args, aliases=aliases, sem=("parallel", "arbitrary"), comm=comm)
    return out if comm is None else (out, carried)


def _log_sigmoid(z):
    return jnp.minimum(z, 0.0) - jnp.log(1.0 + jnp.exp(-jnp.abs(z)))


def _split_bf16(x):
    hi = x.astype(BF16)
    return hi, (x - hi.astype(F32)).astype(BF16)


def _tri(n, fn):
    r = lax.broadcasted_iota(jnp.int32, (n, n), 0)
    c = lax.broadcasted_iota(jnp.int32, (n, n), 1)
    return jnp.where(fn(r, c), 1.0, 0.0).astype(BF16)


def _sb_attn_fwd(proj, out_buf, *, nb, name, comm=None):
    seq = proj.shape[0] // nb
    tq = tk = min(ATT_TILE, seq)
    nq = seq // tq

    def body(qkv_ref, _, o_ref, lt_ref):
        rd = [_qkv_readers((qkv_ref.at[:, pr * PAIR_W:(pr + 1) * PAIR_W],), True) for pr in range(2)]
        after = _tri(tk, lambda r, c: r > c)
        ch = [(pr, h) for pr in range(2) for h in range(2)]

        def q_tile(i, _):
            r0 = pl.multiple_of(i * tq, tq)
            qm = [_split_heads((rd[pr][0](r0, tq) * SCALE).astype(BF16)) for pr in range(2)]
            row = r0 + lax.broadcasted_iota(jnp.int32, (tq, tk), 0)

            def k_tile(j, carry, diagonal):
                c, acc = carry
                c0 = pl.multiple_of(j * tk, tk)
                kt = [rd[pr][1](c0, tk).astype(BF16) for pr in range(2)]
                vm = [_split_heads(rd[pr][2](c0, tk).astype(BF16)) for pr in range(2)]
                if diagonal:
                    strict = (c0 + lax.broadcasted_iota(jnp.int32, (tq, tk), 1)) < row
                ns = range(len(ch))
                z = [lax.dot_general(qm[pr][h], kt[pr], NT_DIMS, preferred_element_type=F32) for pr, h in ch]
                ls = [_log_sigmoid(z[n]) for n in ns]
                lk = [ls[n] - z[n] for n in ns]
                if diagonal:
                    lk = [jnp.where(strict, lk[n], 0.0) for n in ns]
                parts = [_split_bf16(lk[n]) for n in ns]
                sfx = [jnp.dot(parts[n][0], after, preferred_element_type=F32)
                       + jnp.dot(parts[n][1], after, preferred_element_type=F32) for n in ns]
                att = [jnp.exp(ls[n] + sfx[n] + c[n]) for n in ns]
                if diagonal:
                    att = [jnp.where(strict, att[n], 0.0) for n in ns]
                acc = tuple(acc[pr] + (jnp.dot(att[2 * pr].astype(BF16), vm[pr][0], preferred_element_type=F32)
                                       + jnp.dot(att[2 * pr + 1].astype(BF16), vm[pr][1], preferred_element_type=F32))
                            for pr in range(2))
                return tuple(c[n] + jnp.sum(lk[n], axis=-1, keepdims=True) for n in ns), acc

            init = ((jnp.zeros((tq, 1), F32),) * 4, (jnp.zeros((tq, LANES), F32),) * 2)
            c, acc = lax.fori_loop(1, i + 1, lambda jj, cr: k_tile(i - jj, cr, False), k_tile(i, init, True))
            for pr in range(2):
                o_ref[pl.ds(r0, tq), pr * LANES:(pr + 1) * LANES] = acc[pr]
                lt_ref[pl.ds(r0, tq), pr * LANES:(pr + 1) * LANES] = _lane_pair(c[2 * pr], c[2 * pr + 1], tq)
            return 0

        lax.fori_loop(0, nq, q_tile, 0)

    both = lambda width, col: pl.BlockSpec((seq, 2 * width), lambda b: (b, col))
    out, carried = _pallas(
        body, name=name, grid=(nb,), in_specs=[both(PAIR_W, MIX_SB), ANY],
        out_specs=[both(LANES, MIX_SB), both(LANES, 0)],
        out_shape=[jax.ShapeDtypeStruct(out_buf.shape, F32), jax.ShapeDtypeStruct((nb * seq, 2 * LANES), F32)],
        args=[proj, out_buf], aliases={1: 0}, sem=("arbitrary",), comm=comm)
    return out if comm is None else (out, carried)


def _sb_attn_bwd(proj, ltot, do, dbuf, *, nb, name, comm=None):
    seq = proj.shape[0] // nb
    tq = tk = min(ATT_TILE, seq)
    nq = seq // tq

    def body(qkv_ref, lt_ref, do_ref, _, d_ref):
        rd = [_qkv_readers((qkv_ref.at[:, pr * PAIR_W:(pr + 1) * PAIR_W],), True) for pr in range(2)]
        upto = _tri(tk, lambda r, c: r <= c)
        before = _tri(tk, lambda r, c: r < c)
        dk_ref = [d_ref.at[:, pr * PAIR_W + LANES:pr * PAIR_W + 2 * LANES] for pr in range(2)]
        dv_ref = [d_ref.at[:, pr * PAIR_W + 2 * LANES:(pr + 1) * PAIR_W] for pr in range(2)]
        for ref in dk_ref + dv_ref:
            ref[...] = jnp.zeros((seq, LANES), F32)
        ch = [(pr, h) for pr in range(2) for h in range(2)]

        def q_tile(i, _):
            r0 = pl.multiple_of(i * tq, tq)
            qm = [_split_heads((rd[pr][0](r0, tq) * SCALE).astype(BF16)) for pr in range(2)]
            dom = [_split_heads(do_ref[pl.ds(r0, tq), pr * LANES:(pr + 1) * LANES].astype(BF16)) for pr in range(2)]
            lt_t = lt_ref[pl.ds(r0, tq), :]
            lt_h = [lt_t[:, pr * LANES + h * HEAD_DIM:pr * LANES + h * HEAD_DIM + 1] for pr, h in ch]
            row = r0 + lax.broadcasted_iota(jnp.int32, (tq, tk), 0)

            def k_tile(j, carry, diagonal):
                pc, qc, dq = carry
                c0 = pl.multiple_of(j * tk, tk)
                kt = [rd[pr][1](c0, tk).astype(BF16) for pr in range(2)]
                vt = [rd[pr][2](c0, tk).astype(BF16) for pr in range(2)]
                km = [_split_heads(kt[pr]) for pr in range(2)]
                if diagonal:
                    strict = (c0 + lax.broadcasted_iota(jnp.int32, (tq, tk), 1)) < row
                ns = range(len(ch))
                z = [lax.dot_general(qm[pr][h], kt[pr], NT_DIMS, preferred_element_type=F32) for pr, h in ch]
                da = [lax.dot_general(dom[pr][h], vt[pr], NT_DIMS, preferred_element_type=F32) for pr, h in ch]
                ls = [_log_sigmoid(z[n]) for n in ns]
                lk = [ls[n] - z[n] for n in ns]
                if diagonal:
                    lk = [jnp.where(strict, lk[n], 0.0) for n in ns]
                parts = [_split_bf16(lk[n]) for n in ns]
                pin = [jnp.dot(parts[n][0], upto, preferred_element_type=F32)
                       + jnp.dot(parts[n][1], upto, preferred_element_type=F32) for n in ns]
                att = [jnp.exp(ls[n] + (lt_h[n] - pc[n] - pin[n])) for n in ns]
                if diagonal:
                    att = [jnp.where(strict, att[n], 0.0) for n in ns]
                dg = [att[n] * da[n] for n in ns]
                qx = [qc[n] + jnp.dot(dg[n].astype(BF16), before, preferred_element_type=F32) for n in ns]
                sig = [jnp.exp(ls[n]) for n in ns]
                dz = [dg[n] * (1.0 - sig[n]) - sig[n] * qx[n] for n in ns]
                if diagonal:
                    dz = [jnp.where(strict, dz[n], 0.0) for n in ns]
                dzb = [dz[n].astype(BF16) for n in ns]
                attb = [att[n].astype(BF16) for n in ns]
                new_dq = []
                for pr in range(2):
                    a, b = 2 * pr, 2 * pr + 1
                    new_dq.append(dq[pr] + (jnp.dot(dzb[a], km[pr][0], preferred_element_type=F32)
                                            + jnp.dot(dzb[b], km[pr][1], preferred_element_type=F32)))
                    dk_ref[pr][pl.ds(c0, tk), :] += (
                        lax.dot_general(dzb[a], qm[pr][0], TN_DIMS, preferred_element_type=F32)
                        + lax.dot_general(dzb[b], qm[pr][1], TN_DIMS, preferred_element_type=F32))
                    dv_ref[pr][pl.ds(c0, tk), :] += (
                        lax.dot_general(attb[a], dom[pr][0], TN_DIMS, preferred_element_type=F32)
                        + lax.dot_general(attb[b], dom[pr][1], TN_DIMS, preferred_element_type=F32))
                return (tuple(pc[n] + jnp.sum(lk[n], axis=-1, keepdims=True) for n in ns),
                        tuple(qc[n] + jnp.sum(dg[n], axis=-1, keepdims=True) for n in ns), tuple(new_dq))

            zero = (jnp.zeros((tq, 1), F32),) * 4
            init = (zero, zero, (jnp.zeros((tq, LANES), F32),) * 2)
            carry = lax.fori_loop(0, i, lambda j, cr: k_tile(j, cr, False), init)
            _, _, dq = k_tile(i, carry, True)
            for pr in range(2):
                d_ref[pl.ds(r0, tq), pr * PAIR_W:pr * PAIR_W + LANES] = dq[pr] * SCALE
            return 0

        lax.fori_loop(0, nq, q_tile, 0)

    both = lambda width, col: pl.BlockSpec((seq, 2 * width), lambda b: (b, col))
    out, carried = _pallas(
        body, name=name, grid=(nb,), in_specs=[both(PAIR_W, MIX_SB), both(LANES, 0), both(LANES, MIX_SB), ANY],
        out_specs=[both(PAIR_W, MIX_SB)], out_shape=[jax.ShapeDtypeStruct(dbuf.shape, F32)],
        args=[proj, ltot, do, dbuf], aliases={3: 0}, sem=("arbitrary",), comm=comm)
    return out[0] if comm is None else (out[0], carried)


def _lane_scan(x, reverse=False):
    n = x.shape[-1]
    lane = lax.broadcasted_iota(jnp.int32, x.shape, 1)
    k = 1
    while k < n:
        if reverse:
            x = x + jnp.where(lane < n - k, pltpu.roll(x, n - k, 1), 0.0)
        else:
            x = x + jnp.where(lane >= k, pltpu.roll(x, k, 1), 0.0)
        k *= 2
    return x


def _fox_gate_fwd(f_rows, b_rows):
    def body(f_ref, b_ref, o_ref):
        o_ref[...] = _lane_scan(_log_sigmoid(f_ref[...] + b_ref[...]))

    return pl.pallas_call(body, name="fox_gate_fwd", out_shape=jax.ShapeDtypeStruct(f_rows.shape, F32))(f_rows, b_rows)


def _fox_gate_bwd(dcum, f_rows, b_rows):
    def body(d_ref, f_ref, b_ref, df_ref, db_ref):
        z = f_ref[...] + b_ref[...]
        df = _lane_scan(d_ref[...], reverse=True) * jnp.exp(_log_sigmoid(-z))
        df_ref[...] = df
        rs = jnp.sum(df, axis=-1, keepdims=True)
        tot = rs
        for e in range(1, f_rows.shape[0] // N_HEADS):
            tot = tot + pltpu.roll(rs, e * N_HEADS, 0)
        db_ref[...] = tot

    return pl.pallas_call(
        body, name="fox_gate_bwd",
        out_shape=[jax.ShapeDtypeStruct(f_rows.shape, F32), jax.ShapeDtypeStruct((f_rows.shape[0], 1), F32)],
    )(dcum, f_rows, b_rows)


def _dil_tables(seq):
    t = min(ATT_TILE, seq)
    n = seq // t
    a = np.arange(t)
    d = (np.arange(n)[:, None, None] * t + a[None, :, None] - a[None, None, :]).astype(np.int64)
    count = np.zeros(d.shape, np.int64)
    for window, dil in DILATED_PATTERNS:
        count += (d >= 0) & (d % dil == 0) & (d // dil <= window // dil)
    nn = np.maximum(d, 0)
    max_exact = NUM_BUCKETS // 2
    nf = np.maximum(nn, 1).astype(np.float32)
    large = max_exact + (np.log(nf / np.float32(max_exact)) / np.float32(math.log(MAX_DISTANCE / max_exact))
                         * np.float32(NUM_BUCKETS - max_exact)).astype(np.int32)
    bucket = np.where(nn < max_exact, nn, np.minimum(large, NUM_BUCKETS - 1))
    bucket = np.where(count > 0, bucket, -1).astype(np.int32)
    logc = np.where(count > 0, np.log(np.maximum(count, 1)), MASKED).astype(np.float32)
    return bucket, logc


def _dil_bias(rel_bias, seq, comm=None):
    bucket, logc = _dil_tables(seq)
    n, t, _ = bucket.shape

    def body(rb_ref, bk_ref, lc_ref, o_ref):
        h = pl.program_id(0) * 2 + pl.program_id(1)
        bk = bk_ref[...]
        out = lc_ref[...]
        for b in range(NUM_BUCKETS):
            out = jnp.where(bk == b, out + rb_ref[b, h], out)
        o_ref[...] = out

    full = pl.BlockSpec((n, t, t), lambda p, h: (0, 0, 0))
    out, carried = _pallas(
        body, name="dil_bias", grid=(2, 2),
        in_specs=[pl.BlockSpec(memory_space=pltpu.SMEM), full, full],
        out_specs=[pl.BlockSpec((None, None, n, t, t), lambda p, h: (p, h, 0, 0, 0))],
        out_shape=[jax.ShapeDtypeStruct((2, 2, n, t, t), F32)],
        args=[rel_bias, jnp.asarray(bucket), jnp.asarray(logc)], sem=("parallel", "parallel"), comm=comm)
    return out[0] if comm is None else (out[0], carried)


def _dil_bias_bwd(dbias, seq):
    bucket, _ = _dil_tables(seq)
    n, t, _ = bucket.shape

    def body(d_ref, bk_ref, o_ref):
        bk = bk_ref[...]
        lane = lax.broadcasted_iota(jnp.int32, (1, LANES), 1)
        for b in range(NUM_BUCKETS):
            rowv = jnp.zeros((1, LANES), F32)
            for h in range(N_HEADS):
                s = jnp.sum(jnp.where(bk == b, d_ref[h // 2, h % 2], 0.0))
                rowv = jnp.where(lane == h, s, rowv)
            o_ref[b:b + 1, :] = rowv

    out = pl.pallas_call(body, name="dil_bias_bwd", out_shape=jax.ShapeDtypeStruct((NUM_BUCKETS, LANES), F32),
                         compiler_params=pltpu.CompilerParams(vmem_limit_bytes=VMEM_LIMIT))(dbias, jnp.asarray(bucket))
    return out[:, :N_HEADS]


def _shift_rows(x, k, row, fill=0.0):
    n = x.shape[0]
    if k > 0:
        return jnp.where(row >= k, pltpu.roll(x, k, 0), fill)
    return jnp.where(row < n + k, pltpu.roll(x, n + k, 0), fill)


def _row_scan(a, u, row, reverse=False):
    n = a.shape[0]
    k = 1
    while k < n:
        s = -k if reverse else k
        u = a * _shift_rows(u, s, row) + u
        a = a * _shift_rows(a, s, row, 1.0)
        k *= 2
    return u


def _sigmoid(x):
    return 1.0 / (1.0 + jnp.exp(-x))


def _gelu(g):
    return 0.5 * g * (1.0 + lax.erf(g * (2.0 ** -0.5)))


def _gelu_grad(g):
    return 0.5 * (1.0 + lax.erf(g * (2.0 ** -0.5))) + g * jnp.exp(-0.5 * g * g) * (1.0 / math.sqrt(2.0 * math.pi))


def _neg_expm1(x):
    small = -x * (1.0 + x * (0.5 + x * (1.0 / 6.0 + x * (1.0 / 24.0))))
    return jnp.where(x > -0.03, small, 1.0 - jnp.exp(x))


def _lru_core(x, vec, wa, wx, row):
    xs = [_shift_rows(x, 3 - j, row) if j < 3 else x for j in range(4)]
    xc = vec[4:5, :]
    for j in range(4):
        xc = xc + vec[j:j + 1, :] * xs[j]
    xcb = xc.astype(BF16)
    r = _sigmoid(jnp.dot(xcb, wa, preferred_element_type=F32) + vec[5:6, :])
    ig = _sigmoid(jnp.dot(xcb, wx, preferred_element_type=F32) + vec[6:7, :])
    lam = vec[7:8, :]
    sp = jnp.maximum(-lam, 0.0) - _log_sigmoid(jnp.abs(lam))
    la = -LRU_C * r * sp
    a = jnp.exp(la)
    mult = jnp.sqrt(_neg_expm1(2.0 * la))
    return xs, xc, xcb, r, ig, sp, la, a, mult


def _lru_specs(seq):
    xg = pl.BlockSpec((seq, LRU_W), lambda hf, b: (b, COL_LRU // LRU_W + hf))
    mix = pl.BlockSpec((seq, LANES), lambda hf, b: (b, 2 * MIX_LRU + hf))
    vec = pl.BlockSpec((SUBLANES, LANES), lambda hf, b: (0, hf))
    mat = pl.BlockSpec((None, LANES, LANES), lambda hf, b: (hf, 0, 0))
    return xg, mix, vec, mat


def _lru_fwd(proj, vec, wa, wx, out_buf, *, nb, name):
    seq = proj.shape[0] // nb

    def body(xg_ref, vec_ref, wa_ref, wx_ref, _, o_ref):
        row = lax.broadcasted_iota(jnp.int32, (seq, LANES), 0)
        _, xc, _, _, ig, _, _, a, mult = _lru_core(xg_ref[:, 0:LANES], vec_ref[...], wa_ref[...], wx_ref[...], row)
        h = _row_scan(a, mult * (ig * xc), row)
        o_ref[...] = h * _gelu(xg_ref[:, LANES:LRU_W])

    xg, mix, vecs, mat = _lru_specs(seq)
    return pl.pallas_call(
        body, name=name, grid=(2, nb), in_specs=[xg, vecs, mat, mat, ANY], out_specs=mix,
        out_shape=jax.ShapeDtypeStruct(out_buf.shape, F32), input_output_aliases={4: 0},
        compiler_params=_params(("parallel", "arbitrary")))(proj, vec, wa, wx, out_buf)


def _lru_bwd(proj, vec, wa, wx, dout, dbuf, *, nb, name):
    seq = proj.shape[0] // nb

    def body(xg_ref, vec_ref, wa_ref, wx_ref, do_ref, _, d_ref, dvec_ref, dwa_ref, dwx_ref):
        row = lax.broadcasted_iota(jnp.int32, (seq, LANES), 0)
        vec_, wa_, wx_ = vec_ref[...], wa_ref[...], wx_ref[...]
        xs, xc, xcb, r, ig, sp, la, a, mult = _lru_core(xg_ref[:, 0:LANES], vec_, wa_, wx_, row)
        h = _row_scan(a, mult * (ig * xc), row)
        gate, do = xg_ref[:, LANES:LRU_W], do_ref[...]
        d_ref[:, LANES:LRU_W] = do * h * _gelu_grad(gate)
        dh = do * _gelu(gate)
        gacc = _row_scan(_shift_rows(a, -1, row), dh, row, reverse=True)
        da = gacc * _shift_rows(h, 1, row)
        dmult = gacc * (ig * xc)
        dig = gacc * (mult * xc)
        dxc = gacc * (mult * ig)
        dla = da * a - dmult * (a * a) / mult
        dr = (-LRU_C) * sp * dla
        dsp = jnp.sum((-LRU_C) * r * dla, axis=0, keepdims=True)
        dpr = dr * r * (1.0 - r)
        dpi = dig * ig * (1.0 - ig)
        dprb, dpib = dpr.astype(BF16), dpi.astype(BF16)
        dxc = (dxc + lax.dot_general(dprb, wa_, NT_DIMS, preferred_element_type=F32)
               + lax.dot_general(dpib, wx_, NT_DIMS, preferred_element_type=F32))
        dx = vec_[3:4, :] * dxc
        for j in range(3):
            dx = dx + vec_[j:j + 1, :] * _shift_rows(dxc, -(3 - j), row)
        d_ref[:, 0:LANES] = dx

        @pl.when(pl.program_id(1) == 0)
        def _():
            dvec_ref[...] = jnp.zeros_like(dvec_ref)
            dwa_ref[...] = jnp.zeros_like(dwa_ref)
            dwx_ref[...] = jnp.zeros_like(dwx_ref)

        for j in range(4):
            dvec_ref[j:j + 1, :] += jnp.sum(dxc * xs[j], axis=0, keepdims=True)
        dvec_ref[4:5, :] += jnp.sum(dxc, axis=0, keepdims=True)
        dvec_ref[5:6, :] += jnp.sum(dpr, axis=0, keepdims=True)
        dvec_ref[6:7, :] += jnp.sum(dpi, axis=0, keepdims=True)
        lam = vec_[7:8, :]
        dvec_ref[7:8, :] += -dsp * _sigmoid(-lam)
        dwa_ref[...] += lax.dot_general(xcb, dprb, TN_DIMS, preferred_element_type=F32)
        dwx_ref[...] += lax.dot_general(xcb, dpib, TN_DIMS, preferred_element_type=F32)

    xg, mix, vecs, mat = _lru_specs(seq)
    return pl.pallas_call(
        body, name=name, grid=(2, nb), in_specs=[xg, vecs, mat, mat, mix, ANY], out_specs=[xg, vecs, mat, mat],
        out_shape=[jax.ShapeDtypeStruct(dbuf.shape, F32), jax.ShapeDtypeStruct((SUBLANES, 2 * LANES), F32),
                   jax.ShapeDtypeStruct((2, LANES, LANES), F32), jax.ShapeDtypeStruct((2, LANES, LANES), F32)],
        input_output_aliases={5: 0},
        compiler_params=_params(("parallel", "arbitrary")))(proj, vec, wa, wx, dout, dbuf)


FFN_ROWS = 256
FFN_COLS = 1408


def _with_halo(halo, x, k):
    xx = jnp.concatenate([halo, x], axis=0)
    return pltpu.roll(xx, k, 0)[SUBLANES:, :]


def _ffn_conv(x_ref, halo_ref, cw, pos):
    x, halo = x_ref[...], halo_ref[...]
    x1 = jnp.where(pos >= 1, _with_halo(halo, x, 1), 0.0)
    x2 = jnp.where(pos >= 2, _with_halo(halo, x, 2), 0.0)
    return cw[3:4, :] + cw[0:1, :] * x2 + cw[1:2, :] * x1 + cw[2:3, :] * x, x1, x2


def _ffn_specs(tm, tn, gate_off):
    prev = lambda i: jnp.maximum(i * (tm // SUBLANES) - 1, 0)
    up = pl.BlockSpec((tm, tn), lambda j, i: (i, j))
    gate = pl.BlockSpec((tm, tn), lambda j, i: (i, j + gate_off))
    up_h = pl.BlockSpec((SUBLANES, tn), lambda j, i: (prev(i), j))
    gate_h = pl.BlockSpec((SUBLANES, tn), lambda j, i: (prev(i), j + gate_off))
    cw_up = pl.BlockSpec((SUBLANES, tn), lambda j, i: (0, j))
    cw_gate = pl.BlockSpec((SUBLANES, tn), lambda j, i: (0, j + gate_off))
    return up, gate, up_h, gate_h, cw_up, cw_gate


def _ffn_act(hf, cw, *, seq, name):
    t, w2 = hf.shape
    w = w2 // 2
    tm, tn = _tile(seq, FFN_ROWS, SUBLANES), _tile(w, FFN_COLS)

    def body(u_ref, g_ref, uh_ref, gh_ref, cu_ref, cg_ref, o_ref):
        pos = (pl.program_id(1) * tm + lax.broadcasted_iota(jnp.int32, (tm, 1), 0)) % seq
        up, _, _ = _ffn_conv(u_ref, uh_ref, cu_ref[...], pos)
        gate, _, _ = _ffn_conv(g_ref, gh_ref, cg_ref[...], pos)
        o_ref[...] = (_gelu(gate) * up).astype(BF16)

    specs = _ffn_specs(tm, tn, w // tn)
    return pl.pallas_call(
        body, name=name, grid=(w // tn, t // tm), in_specs=list(specs), out_specs=specs[0],
        out_shape=jax.ShapeDtypeStruct((t, w), BF16),
        compiler_params=_params(("parallel", "parallel")))(hf, hf, hf, hf, cw, cw)


def _ffn_bwd(hf, cw, dact, *, seq, name, comm=None):
    t, w2 = hf.shape
    w = w2 // 2
    tm, tn = _tile(seq, FFN_ROWS, 2 * SUBLANES), _tile(w, FFN_COLS)
    ext = tm + SUBLANES
    last = t // SUBLANES - 1

    def body(u_ref, g_ref, uh_ref, gh_ref, cu_ref, cg_ref, un_ref, gn_ref, da_ref, dn_ref, d_ref, dcu_ref, dcg_ref):
        pos = (pl.program_id(1) * tm + lax.broadcasted_iota(jnp.int32, (ext, 1), 0)) % seq

        def conv(x_ref, prev_ref, next_ref, cwv):
            xx = jnp.concatenate([prev_ref[...], x_ref[...], next_ref[...]], axis=0)
            x1 = jnp.where(pos >= 1, pltpu.roll(xx, 1, 0)[SUBLANES:, :], 0.0)
            x2 = jnp.where(pos >= 2, pltpu.roll(xx, 2, 0)[SUBLANES:, :], 0.0)
            x0 = xx[SUBLANES:, :]
            return cwv[3:4, :] + cwv[0:1, :] * x2 + cwv[1:2, :] * x1 + cwv[2:3, :] * x0, (x2, x1, x0)

        def back(d, cwv):
            d1 = jnp.where(pos < seq - 1, pltpu.roll(d, ext - 1, 0), 0.0)
            d2 = jnp.where(pos < seq - 2, pltpu.roll(d, ext - 2, 0), 0.0)
            return (cwv[2:3, :] * d + cwv[1:2, :] * d1 + cwv[0:1, :] * d2)[:tm, :].astype(BF16)

        cu, cg = cu_ref[...], cg_ref[...]
        up, u_taps = conv(u_ref, uh_ref, un_ref, cu)
        gate, g_taps = conv(g_ref, gh_ref, gn_ref, cg)
        da = jnp.concatenate([da_ref[...], dn_ref[...]], axis=0)
        cdf = 0.5 * (1.0 + lax.erf(gate * (2.0 ** -0.5)))
        d_up = da * (gate * cdf)
        d_gate = da * up * (cdf + gate * jnp.exp(-0.5 * gate * gate) * (1.0 / math.sqrt(2.0 * math.pi)))
        d_ref[0] = back(d_up, cu)
        d_ref[1] = back(d_gate, cg)

        @pl.when(pl.program_id(1) == 0)
        def _():
            dcu_ref[...] = jnp.zeros_like(dcu_ref)
            dcg_ref[...] = jnp.zeros_like(dcg_ref)

        for ref, d, taps in ((dcu_ref, d_up, u_taps), (dcg_ref, d_gate, g_taps)):
            own = d[:tm, :]
            for j in range(3):
                ref[j:j + 1, :] += jnp.sum(own * taps[j][:tm, :], axis=0, keepdims=True)
            ref[3:4, :] += jnp.sum(own, axis=0, keepdims=True)

    gate_off = w // tn
    specs = _ffn_specs(tm, tn, gate_off)
    tile, cwt = specs[0], specs[4]
    nxt = lambda i: jnp.minimum((i + 1) * (tm // SUBLANES), last)
    up_n = pl.BlockSpec((SUBLANES, tn), lambda j, i: (nxt(i), j))
    gate_n = pl.BlockSpec((SUBLANES, tn), lambda j, i: (nxt(i), j + gate_off))
    out, carried = _pallas(
        body, name=name, grid=(w // tn, t // tm), in_specs=list(specs) + [up_n, gate_n, tile, up_n],
        out_specs=[pl.BlockSpec((2, tm, tn), lambda j, i: (0, i, j)), cwt, cwt],
        out_shape=[jax.ShapeDtypeStruct((2, t, w), BF16), jax.ShapeDtypeStruct((SUBLANES, w), F32),
                   jax.ShapeDtypeStruct((SUBLANES, w), F32)],
        args=[hf, hf, hf, hf, cw, cw, hf, hf, dact, dact], sem=("parallel", "arbitrary"), comm=comm)
    return out if comm is None else (out, carried)


def _adamw(w, g, m, v, *, name, rows=256, lead=None):
    nl, r, c = w.shape
    tr = _tile(r, rows, SUBLANES)

    def body(w_ref, g_ref, m_ref, v_ref, d_ref, nm_ref, nv_ref):
        g_ = g_ref[...]
        nm = ADAM_B1 * m_ref[...] + (1.0 - ADAM_B1) * g_
        nv = ADAM_B2 * v_ref[...] + (1.0 - ADAM_B2) * (g_ * g_)
        m_hat = nm / (1.0 - ADAM_B1 ** ADAM_STEP)
        v_hat = nv / (1.0 - ADAM_B2 ** ADAM_STEP)
        d_ref[...] = -ADAM_LR * (m_hat / (jnp.sqrt(v_hat) + ADAM_EPS) + ADAM_WD * w_ref[...])
        nm_ref[...] = nm
        nv_ref[...] = nv

    shape = jax.ShapeDtypeStruct((nl, r, c), F32)
    if lead is None:
        spec, grid = pl.BlockSpec((None, tr, c), lambda l, i: (l, i, 0)), (nl, r // tr)
    else:
        spec, grid = pl.BlockSpec((lead, r, c), lambda i: (i, 0, 0)), (nl // lead,)
    return pl.pallas_call(body, name=name, grid=grid, in_specs=[spec] * 4, out_specs=[spec] * 3,
                          out_shape=[shape] * 3, compiler_params=_params(("parallel",) * len(grid)))(w, g, m, v)


def _mesh_pos():
    return lax.axis_index("x"), lax.axis_index("y"), lax.axis_index("c")


def _peers(x, y):
    chips = [(1 - x, y), (x, 1 - y), (1 - x, 1 - y)]
    return [(px, py, 2 * px + py) for px, py in chips]


def _remote(src, dst, send_sems, recv_sems, idx, to):
    return pltpu.make_async_remote_copy(src, dst, send_sems.at[idx], recv_sems.at[idx], device_id=to,
                                        device_id_type=MESH)


class _Comm:
    def __init__(self, operands, out_shape, aliases, sems, copies):
        self.operands, self.out_shape, self.aliases, self.sems, self.copies = operands, out_shape, aliases, sems, copies

    def start(self, ins, outs, sems):
        for send, _ in self.copies(ins, outs, sems):
            send.start()

    def wait(self, ins, outs, sems):
        pairs = self.copies(ins, outs, sems)
        for _, recv in pairs:
            recv.wait_recv()
        for send, _ in pairs:
            send.wait_send()


def _pallas(body, *, name, grid, in_specs, out_specs, out_shape, args, aliases=None, scratch=(), sem, comm=None):
    n_in, n_out = len(in_specs), len(out_specs)
    aliases = dict(aliases or {})
    if comm is None:
        out = pl.pallas_call(body, name=name, grid=grid, in_specs=in_specs, out_specs=out_specs, out_shape=out_shape,
                             input_output_aliases=aliases, scratch_shapes=list(scratch),
                             compiler_params=_params(sem))(*args)
        return list(out), []
    nci, nco, ncs = len(comm.operands), len(comm.out_shape), len(comm.sems)

    def carried(*refs):
        ins, cin = refs[:n_in], refs[n_in:n_in + nci]
        o0 = n_in + nci
        outs, cout = refs[o0:o0 + n_out], refs[o0 + n_out:o0 + n_out + nco]
        s0 = o0 + n_out + nco
        own, csem = refs[s0:len(refs) - ncs], refs[len(refs) - ncs:]
        ids = [pl.program_id(ax) for ax in range(len(grid))]
        first, last = ids[0] == 0, ids[0] == grid[0] - 1
        for i, g in zip(ids[1:], grid[1:]):
            first, last = jnp.logical_and(first, i == 0), jnp.logical_and(last, i == g - 1)

        @pl.when(first)
        def _():
            comm.start(cin, cout, csem)

        body(*ins, *outs, *own)

        @pl.when(last)
        def _():
            comm.wait(cin, cout, csem)

    aliases.update({n_in + i: n_out + j for i, j in comm.aliases.items()})
    out = pl.pallas_call(
        carried, name=name, grid=grid, in_specs=list(in_specs) + [ANY] * nci, out_specs=list(out_specs) + [ANY] * nco,
        out_shape=list(out_shape) + list(comm.out_shape), input_output_aliases=aliases,
        scratch_shapes=list(scratch) + list(comm.sems),
        compiler_params=_params(("arbitrary",) * len(grid)))(*args, *comm.operands)
    return list(out[:n_out]), list(out[n_out:])


def _run_comm(comm, *, name):
    nci, nco = len(comm.operands), len(comm.out_shape)

    def body(*refs):
        ins, outs, sems = refs[:nci], refs[nci:nci + nco], refs[nci + nco:]
        comm.start(ins, outs, sems)
        comm.wait(ins, outs, sems)

    return pl.pallas_call(body, name=name, in_specs=[ANY] * nci, out_specs=[ANY] * nco, out_shape=list(comm.out_shape),
                          input_output_aliases=dict(comm.aliases), scratch_shapes=list(comm.sems))(*comm.operands)


def _pair_sems(*shape):
    return [pltpu.SemaphoreType.DMA(shape), pltpu.SemaphoreType.DMA(shape)]


def _gather_comm(bufs, layer, stage):
    n = len(bufs)

    def copies(ins, outs, sems):
        x, y, c = _mesh_pos()
        me = 2 * x + y
        pairs = []
        for i in range(n):
            h = bufs[i].shape[2] // 2
            mine, other = pl.ds(c * h, h), pl.ds((1 - c) * h, h)
            for r, (px, py, k) in enumerate(_peers(x, y)):
                if stage == 0:
                    send = _remote(ins[i].at[layer, me, mine, :], outs[i].at[layer, me, mine, :], *sems, (i, r), (px, py, c))
                    land = outs[i].at[layer, k, mine, :]
                    recv = _remote(land, land, *sems, (i, r), (px, py, c))
                else:
                    send = _remote(ins[i].at[layer, k, mine, :], outs[i].at[layer, k, mine, :], *sems, (i, r), (x, y, 1 - c))
                    land = outs[i].at[layer, k, other, :]
                    recv = _remote(land, land, *sems, (i, r), (x, y, 1 - c))
                pairs.append((send, recv))
        return pairs

    return _Comm(bufs, [jax.ShapeDtypeStruct(b.shape, b.dtype) for b in bufs], {i: i for i in range(n)},
                 _pair_sems(n, 3), copies)


def _reduce_sibling_comm(gs):
    n = len(gs)

    def copies(ins, outs, sems):
        x, y, c = _mesh_pos()
        pairs = []
        for i in range(n):
            h = gs[i].shape[1] // 2
            cp = _remote(ins[i].at[:, pl.ds((1 - c) * h, h), :], outs[i], *sems, i, (x, y, 1 - c))
            pairs.append((cp, cp))
        return pairs

    return _Comm(gs, [jax.ShapeDtypeStruct((g.shape[0], g.shape[1] // 2, g.shape[2]), g.dtype) for g in gs], {},
                 _pair_sems(n), copies)


def _reduce_chips_comm(ps):
    n = len(ps)

    def copies(ins, outs, sems):
        x, y, c = _mesh_pos()
        pairs = []
        for i in range(n):
            for r, (px, py, k) in enumerate(_peers(x, y)):
                cp = _remote(ins[i].at[k], outs[i].at[r], *sems, (i, r), (px, py, c))
                pairs.append((cp, cp))
        return pairs

    return _Comm(ps, [jax.ShapeDtypeStruct((3,) + p.shape[1:], p.dtype) for p in ps], {}, _pair_sems(n, 3), copies)


def _share_halves(bufs, *, name):
    n = len(bufs)

    def body(*refs):
        ins, outs = refs[:n], refs[n:2 * n]
        send_sems, recv_sems = refs[2 * n:]
        x, y, c = _mesh_pos()
        cps = []
        for i in range(n):
            h = bufs[i].shape[1] // 2
            mine = pl.ds(c * h, h)
            cp = _remote(ins[i].at[:, mine, :], outs[i].at[:, mine, :], send_sems, recv_sems, i, (x, y, 1 - c))
            cp.start()
            cps.append(cp)
        for cp in cps:
            cp.wait()

    return pl.pallas_call(
        body, name=name, in_specs=[ANY] * n, out_specs=[ANY] * n,
        out_shape=[jax.ShapeDtypeStruct(b.shape, b.dtype) for b in bufs],
        input_output_aliases={i: i for i in range(n)},
        scratch_shapes=[pltpu.SemaphoreType.DMA((n,)), pltpu.SemaphoreType.DMA((n,))])(*bufs)


def _add_own_half(full, recv, pos, *, name, rows=256):
    k4, h, n = recv.shape
    tr = _tile(h, rows, 16)
    nblk = h // tr

    def body(pos_ref, a_ref, b_ref, o_ref):
        o_ref[...] = (a_ref[...] + b_ref[...]).astype(BF16)

    grid_spec = pltpu.PrefetchScalarGridSpec(
        num_scalar_prefetch=1, grid=(k4, nblk),
        in_specs=[pl.BlockSpec((None, tr, n), lambda k, i, pos_ref: (k, pos_ref[1] * nblk + i, 0)),
                  pl.BlockSpec((None, tr, n), lambda k, i, pos_ref: (k, i, 0))],
        out_specs=pl.BlockSpec((None, tr, n), lambda k, i, pos_ref: (k, i, 0)))
    return pl.pallas_call(body, name=name, grid_spec=grid_spec, out_shape=jax.ShapeDtypeStruct(recv.shape, BF16),
                          compiler_params=_params(("parallel", "parallel")))(pos, full, recv)


def _sum_into(own, others, buf, pos, layer, *, name, rows=256):
    _, h, n = own.shape
    tr = _tile(h, rows, 16)
    nblk = h // tr

    def body(pos_ref, own_ref, oth_ref, _, o_ref):
        acc = own_ref[...].astype(F32)
        for r in range(3):
            acc = acc + oth_ref[r].astype(F32)
        o_ref[...] = acc

    grid_spec = pltpu.PrefetchScalarGridSpec(
        num_scalar_prefetch=1, grid=(nblk,),
        in_specs=[pl.BlockSpec((None, tr, n), lambda i, pos_ref: (pos_ref[0], i, 0)),
                  pl.BlockSpec((3, tr, n), lambda i, pos_ref: (0, i, 0)), ANY],
        out_specs=pl.BlockSpec((None, tr, n), lambda i, pos_ref: (layer, pos_ref[1] * nblk + i, 0)))
    return pl.pallas_call(body, name=name, grid_spec=grid_spec, out_shape=jax.ShapeDtypeStruct(buf.shape, F32),
                          input_output_aliases={3: 0}, compiler_params=_params(("parallel",)))(pos, own, others, buf)


def _pair_comm(buf):
    def copies(ins, outs, sems):
        x, y, c = _mesh_pos()
        land = outs[0].at[1 - c]
        return [(_remote(ins[0].at[c], outs[0].at[c], *sems, 0, (x, y, 1 - c)),
                 _remote(land, land, *sems, 0, (x, y, 1 - c)))]

    return _Comm([buf], [jax.ShapeDtypeStruct(buf.shape, buf.dtype)], {0: 0}, _pair_sems(1), copies)


def _quad_comm(buf):
    def copies(ins, outs, sems):
        x, y, c = _mesh_pos()
        me = 2 * x + y
        pairs = []
        for r, (px, py, k) in enumerate(_peers(x, y)):
            land = outs[0].at[k]
            pairs.append((_remote(ins[0].at[me], outs[0].at[me], *sems, r, (px, py, c)),
                          _remote(land, land, *sems, r, (px, py, c))))
        return pairs

    return _Comm([buf], [jax.ShapeDtypeStruct(buf.shape, buf.dtype)], {0: 0}, _pair_sems(3), copies)


def _chip_bcast(buf, *, name):
    def body(src_ref, out_ref, send_sems, recv_sems, local_sem):
        x, y, c = _mesh_pos()
        me = 2 * x + y
        local = pltpu.make_async_copy(src_ref, out_ref.at[me], local_sem)
        local.start()
        sends = []
        for r, (px, py, _) in enumerate(_peers(x, y)):
            cp = _remote(src_ref, out_ref.at[me], send_sems, recv_sems, r, (px, py, c))
            cp.start()
            sends.append(cp)
        for r, (px, py, k) in enumerate(_peers(x, y)):
            _remote(src_ref, out_ref.at[k], send_sems, recv_sems, r, (px, py, c)).wait_recv()
        for cp in sends:
            cp.wait_send()
        local.wait()

    return pl.pallas_call(
        body, name=name, in_specs=[ANY], out_specs=ANY, out_shape=jax.ShapeDtypeStruct((4,) + buf.shape, buf.dtype),
        scratch_shapes=[pltpu.SemaphoreType.DMA((3,)), pltpu.SemaphoreType.DMA((3,)), pltpu.SemaphoreType.DMA])(buf)


def _sum_slots(buf, *, name, rows=384):
    r, n = buf.shape[-2:]
    k = int(np.prod(buf.shape[:-2]))
    tr = _tile(r, rows, SUBLANES)

    def body(b_ref, o_ref):
        acc = b_ref[0]
        for s in range(1, k):
            acc = acc + b_ref[s]
        o_ref[...] = acc

    return pl.pallas_call(
        body, name=name, grid=(r // tr,), in_specs=[pl.BlockSpec((k, tr, n), lambda i: (0, i, 0))],
        out_specs=pl.BlockSpec((tr, n), lambda i: (i, 0)), out_shape=jax.ShapeDtypeStruct((r, n), F32),
        compiler_params=_params(("parallel",)))(buf.reshape((k, r, n)))


ROW = 1024
BIG = (("w_in", 2), ("w_out", 1), ("w_cq", 1), ("w_ck", 1), ("w_cv", 1), ("w_co", 2), ("w_up", 2), ("w_down", 1))
CONV = ("lru_conv_w", "ffn_conv_w")
REPLICATED = ("norm_mix_g", "b_forget", "lru_conv_b", "lru_w_a", "lru_b_a", "lru_w_x", "lru_b_x", "lru_lambda",
              "norm_cross_g", "norm_mem_g", "norm_ffn_g", "ffn_conv_b", "rel_bias", "final_norm_g")
WEIGHTS = ('norm_mix_g', 'w_in', 'b_forget', 'lru_conv_w', 'lru_conv_b', 'lru_w_a', 'lru_b_a', 'lru_w_x', 'lru_b_x',
           'lru_lambda', 'w_out', 'norm_cross_g', 'norm_mem_g', 'w_cq', 'w_ck', 'w_cv', 'w_co', 'norm_ffn_g', 'w_up',
           'ffn_conv_w', 'ffn_conv_b', 'w_down', 'rel_bias', 'final_norm_g')
INPUTS = ("x", "mem") + WEIGHTS + ("loss_target",) + tuple("m_" + n for n in WEIGHTS) + tuple("v_" + n for n in WEIGHTS)


def _round_up(n, m):
    return -(-n // m) * m


class _Packing:
    def __init__(self, entries):
        self.entries, self.off = entries, {}
        o = 0
        for name, shape in entries:
            self.off[name] = o
            o += _round_up(int(np.prod(shape)), ROW)
        self.used = o
        self.rows = _round_up(o // ROW, SUBLANES)

    def pack(self, arrays):
        parts = []
        for name, shape in self.entries:
            n = int(np.prod(shape))
            parts.append(jnp.pad(arrays[name].reshape(n), (0, _round_up(n, ROW) - n)))
        tail = self.rows * ROW - self.used
        if tail:
            parts.append(jnp.zeros((tail,), F32))
        return jnp.concatenate(parts).reshape(self.rows, ROW)

    def unpack(self, flat, lead=()):
        out = {}
        for name, shape in self.entries:
            n = int(np.prod(shape))
            r0, nr = self.off[name] // ROW, _round_up(n, ROW) // ROW
            rows = lax.slice_in_dim(flat, r0, r0 + nr, axis=len(lead)).reshape(lead + (nr * ROW,))
            out[name] = lax.slice_in_dim(rows, 0, n, axis=len(lead)).reshape(lead + tuple(shape))
        return out


def _to_shards(g, axis):
    r, c = g.shape
    if axis == 1:
        return g.reshape(4, r // 4, c)
    return g.reshape(r, 4, c // 4).transpose(1, 0, 2)


def _from_shards(s, axis):
    _, r, c = s.shape
    if axis == 1:
        return s.reshape(4 * r, c)
    return s.transpose(1, 0, 2).reshape(r, 4 * c)


def _proj_blocks():
    blocks = []
    for mixer in (MIX_SB, MIX_FOX, MIX_DIL):
        for p in range(2):
            blocks += [ORIG_COL[mixer] + part * 2 * LANES + p * LANES for part in range(3)]
    for hf in range(2):
        blocks += [ORIG_LRU_X + hf * LANES, ORIG_LRU_G + hf * LANES]
    return blocks


def _pad_w_in(w):
    parts = [w[..., s:s + LANES] for s in _proj_blocks()]
    parts += [w[..., 1536:1540], jnp.zeros(w.shape[:-1] + (PROJ_W - COL_F - N_HEADS,), w.dtype)]
    return jnp.concatenate(parts, axis=-1)


def _unpad_w_in(wp):
    blocks = _proj_blocks()
    order = sorted(range(len(blocks)), key=lambda i: blocks[i])
    parts = []
    for i in order:
        if blocks[i] == ORIG_COL[MIX_DIL]:
            parts.append(wp[..., COL_F:COL_F + N_HEADS])
        parts.append(wp[..., i * LANES:(i + 1) * LANES])
    return jnp.concatenate(parts, axis=-1)


def _block_diag(w):
    z = jnp.zeros((HEAD_DIM, HEAD_DIM), w.dtype)
    half = lambda a, b: jnp.concatenate([jnp.concatenate([a, z], 1), jnp.concatenate([z, b], 1)], 0)
    return jnp.stack([half(w[0], w[1]), half(w[2], w[3])])


def _block_diag_grad(d):
    return jnp.stack([d[0, :HEAD_DIM, :HEAD_DIM], d[0, HEAD_DIM:, HEAD_DIM:],
                      d[1, :HEAD_DIM, :HEAD_DIM], d[1, HEAD_DIM:, HEAD_DIM:]])


def _fox_layouts(cum, nb, seq):
    tk = min(ATT_TILE, seq)
    col = cum.reshape(nb, 2, 2, seq).transpose(0, 1, 3, 2)
    row = cum.reshape(nb, 2, 2, seq // tk, tk).transpose(0, 1, 3, 2, 4)
    return col, row


def _layer_params(w, l, nb):
    lru_vec = jnp.concatenate([w["lru_conv_w"][l], w["lru_conv_b"][l][None], w["lru_b_a"][l][None],
                               w["lru_b_x"][l][None], w["lru_lambda"][l][None]], axis=0)
    ffn_cw = jnp.concatenate([w["ffn_conv_w"][l], w["ffn_conv_b"][l][None],
                              jnp.zeros((SUBLANES - 4, 2 * D_FF), F32)], axis=0)
    return dict(
        w_in=w["w_in_padded"][l], lru_vec=lru_vec,
        wa=_block_diag(w["lru_w_a"][l]).astype(BF16), wx=_block_diag(w["lru_w_x"][l]).astype(BF16),
        ffn_cw=ffn_cw, b_rows=jnp.tile(w["b_forget"][l], nb).reshape(nb * N_HEADS, 1))


NORM_ROWS = 512
NORM_FWD_ROWS = 1024


def _merge_comms(comms):
    if len(comms) == 1:
        return comms[0]
    operands, out_shape, aliases, sems, spans = [], [], {}, [], []
    for cm in comms:
        aliases.update({len(operands) + i: len(out_shape) + j for i, j in cm.aliases.items()})
        spans.append((len(operands), len(out_shape), len(sems)))
        operands += list(cm.operands)
        out_shape += list(cm.out_shape)
        sems += list(cm.sems)

    def copies(ins, outs, sm):
        pairs = []
        for cm, (i0, o0, s0) in zip(comms, spans):
            pairs += cm.copies(ins[i0:i0 + len(cm.operands)], outs[o0:o0 + len(cm.out_shape)], sm[s0:s0 + len(cm.sems)])
        return pairs

    return _Comm(operands, out_shape, aliases, sems, copies)


GATHER_FIRST = ("w_in",)
GATHER_MID = ("w_out", "w_cq", "w_ck", "w_cv", "w_co")
GATHER_LAST = ("w_up", "w_down")


class _WeightGather:
    def __init__(self, slots, w, depth):
        self.slots, self.w, self.depth = slots, w, depth

    def plan(self, l, key):
        nxt = l + 1 if l + 1 < self.depth else None
        early = GATHER_FIRST + GATHER_MID
        if l == 0:
            table = {"proj": [(GATHER_MID, 0, 0)],
                     "sb_fwd": [(GATHER_MID, 0, 1), (GATHER_LAST, 0, 0)],
                     "fox_fwd": [(GATHER_LAST, 0, 1)] + ([(early, nxt, 0)] if nxt else []),
                     "dil_fwd": [(early, nxt, 1), (GATHER_LAST, nxt, 0)] if nxt else [],
                     "out": [(GATHER_LAST, nxt, 1)] if nxt else []}
        else:
            everything = early + GATHER_LAST
            table = {"sb_fwd": [(everything, nxt, 0)], "fox_fwd": [(everything, nxt, 1)]} if nxt else {}
        return table.get(key, [])

    def comm(self, l, key):
        entries = self.plan(l, key)
        if not entries:
            return None
        return _merge_comms([_gather_comm([self.slots[n] for n in names], layer, stage)
                             for names, layer, stage in entries])

    def done(self, l, key, landed):
        landed = list(landed)
        for names, layer, stage in self.plan(l, key):
            for n in names:
                self.slots[n] = landed.pop(0)
            if stage == 1:
                self.take(names, layer)

    def take(self, names, layer):
        for n, axis in BIG:
            if n in names:
                self.w[n][layer] = self.slots[n][layer] if n == "w_up" else _from_shards(self.slots[n][layer], axis)
        if "w_in" in names:
            self.w["w_in_padded"][layer] = _pad_w_in(self.w["w_in"][layer])

    def first_comm(self, stage):
        return _gather_comm([self.slots[n] for n in GATHER_FIRST], 0, stage)

    def first_done(self, stage, landed):
        self.slots.update(zip(GATHER_FIRST, landed))
        if stage == 1:
            self.take(GATHER_FIRST, 0)


def _layer_fwd(x, h, mem, w, lp, l, next_g, bias, nb, gather):
    t, d = x.shape
    seq = t // nb
    tag = f"l{l}"
    sv = dict(x0=x)

    def carrying(key, fn):
        comm = gather.comm(l, key)
        res = fn(comm)
        if comm is not None:
            res, landed = res
            gather.done(l, key, landed)
        return res

    proj = carrying("proj", lambda cm: _mm(h, lp["w_in"], name=tag + "_proj", comm=cm))
    mixed, ltot = carrying("sb_fwd", lambda cm: _sb_attn_fwd(proj, lax.empty((t, d), F32), nb=nb,
                                                             name=tag + "_sb_fwd", comm=cm))
    f_rows = proj[:, COL_F:COL_F + N_HEADS].reshape(nb, seq, N_HEADS).transpose(0, 2, 1).reshape(nb * N_HEADS, seq)
    cum_col, cum_row = _fox_layouts(_fox_gate_fwd(f_rows, lp["b_rows"]), nb, seq)
    mixed, lse_fox = carrying("fox_fwd", lambda cm: _softmax_attn_fwd(
        proj, nb=nb, mode="fox", mixer=MIX_FOX, out_buf=mixed, extra=(cum_col, cum_row), name=tag + "_fox_fwd", comm=cm))
    mixed, lse_dil = carrying("dil_fwd", lambda cm: _softmax_attn_fwd(
        proj, nb=nb, mode="dil", mixer=MIX_DIL, out_buf=mixed, extra=(bias,), name=tag + "_dil_fwd", comm=cm))
    mixed = _lru_fwd(proj, lp["lru_vec"], lp["wa"], lp["wx"], mixed, nb=nb, name=tag + "_lru_fwd")
    x1, hq = carrying("out", lambda cm: _mm(mixed, w["w_out"][l], res=x, norm_g=w["norm_cross_g"][l], ti=NORM_FWD_ROWS,
                                            name=tag + "_out", comm=cm))
    memn = _rmsnorm(mem, w["norm_mem_g"][l], name=tag + "_norm_mem")
    q = _mm(hq, w["w_cq"][l], name=tag + "_cq")
    k = _mm(memn, w["w_ck"][l], name=tag + "_ck")
    v = _mm(memn, w["w_cv"][l], name=tag + "_cv")
    oc, lse_c = _softmax_attn_fwd((q, k, v), nb=nb, mode="cross", name=tag + "_cross_fwd")
    x2, hn = _mm(oc, w["w_co"][l], res=x1, norm_g=w["norm_ffn_g"][l], ti=NORM_FWD_ROWS, name=tag + "_co")
    hf = _mm(hn, w["w_up"][l], b_chunks=4, name=tag + "_up")
    act = _ffn_act(hf, lp["ffn_cw"], seq=seq, name=tag + "_ffn_act")
    if next_g is None:
        x3, h_next = _mm(act, w["w_down"][l], res=x2, name=tag + "_down"), None
    else:
        x3, h_next = _mm(act, w["w_down"][l], res=x2, norm_g=next_g, ti=NORM_FWD_ROWS, name=tag + "_down")
    sv.update(h=h, proj=proj, ltot=ltot, f_rows=f_rows, cum_col=cum_col, cum_row=cum_row, lse_fox=lse_fox,
              lse_dil=lse_dil, mixed=mixed, x1=x1, hq=hq, memn=memn, q=q, k=k, v=v, oc=oc, lse_c=lse_c, x2=x2,
              hn=hn, hf=hf, act=act)
    return x3, h_next, sv


class _PendingReduce:
    def __init__(self, full, pos, layer):
        self.names, self.full, self.pos, self.layer = list(full), list(full.values()), pos, layer

    def sibling_comm(self):
        return _reduce_sibling_comm(self.full)

    def add(self, from_sibling):
        self.partial = [_add_own_half(f, r, self.pos, name=f"l{self.layer}_reduce_add_{n}")
                        for f, r, n in zip(self.full, from_sibling, self.names)]

    def chips_comm(self):
        return _reduce_chips_comm(self.partial)

    def finish(self, others, g_shard):
        g_shard = dict(g_shard)
        for p, o, n in zip(self.partial, others, self.names):
            g_shard[n] = _sum_into(p, o, g_shard[n], self.pos, self.layer, name=f"l{self.layer}_reduce_sum_{n}")
        return g_shard


def _layer_bwd(dx3, mem, sv, w, lp, l, bias, nb, pos, pending=None, g_shard=None, reduce_early=False):
    t = dx3.shape[0]
    seq = t // nb
    tag = f"l{l}"
    g = {}
    down_rows = _tile(sv["act"].shape[1], 1408)
    if pending is None:
        g["w_down"] = _mm(sv["act"], dx3, ta=True, ti=down_rows, name=tag + "_dw_down")
    else:
        g["w_down"], from_sibling = _mm(sv["act"], dx3, ta=True, ti=down_rows, comm=pending.sibling_comm(),
                                        name=tag + "_dw_down")
        pending.add(from_sibling)
    dact = _mm(dx3, w["w_down"][l], tb=True, name=tag + "_dact")
    if pending is None:
        dhf, dcu, dcg = _ffn_bwd(sv["hf"], lp["ffn_cw"], dact, seq=seq, name=tag + "_ffn_bwd")
    else:
        (dhf, dcu, dcg), others = _ffn_bwd(sv["hf"], lp["ffn_cw"], dact, seq=seq, name=tag + "_ffn_bwd",
                                           comm=pending.chips_comm())
        g_shard = pending.finish(others, g_shard)
    dcw = jnp.concatenate([dcu, dcg], axis=1)
    g["ffn_conv_w"], g["ffn_conv_b"] = dcw[:3], dcw[3]
    g["w_up"] = _mm(sv["hn"], dhf, ta=True, halves="b", col_shards=4, name=tag + "_dw_up")
    early = _PendingReduce(_big_grad_shards(g, EARLY), pos, l) if reduce_early else None
    res = _mm(dhf, w["w_up"][l], tb=True, halves="a", b_chunks=4, norm_bwd=(sv["x2"], w["norm_ffn_g"][l], dx3), ti=NORM_ROWS,
              comm=early.sibling_comm() if early else None, name=tag + "_dhn")
    if early:
        res, from_sibling = res
        early.add(from_sibling)
    dx2, dg = res
    g["norm_ffn_g"] = dg.reshape(-1)
    g["w_co"] = _mm(sv["oc"], dx2, ta=True, col_shards=4, name=tag + "_dw_co")
    doc = _mm(dx2, w["w_co"][l], tb=True, name=tag + "_doc")
    dq, dk, dv = _softmax_attn_bwd((sv["q"], sv["k"], sv["v"]), sv["oc"], sv["lse_c"], doc, nb=nb, mode="cross",
                                   name=tag + "_cross_bwd")
    g["w_cq"] = _mm(sv["hq"], dq, ta=True, name=tag + "_dw_cq")
    g["w_ck"] = _mm(sv["memn"], dk, ta=True, name=tag + "_dw_ck")
    g["w_cv"] = _mm(sv["memn"], dv, ta=True, name=tag + "_dw_cv")
    dx1, dg = _mm(dq, w["w_cq"][l], tb=True, norm_bwd=(sv["x1"], w["norm_cross_g"][l], dx2), ti=NORM_ROWS,
                  name=tag + "_dhq")
    g["norm_cross_g"] = dg.reshape(-1)
    dmemn = _mm(dv, w["w_cv"][l], tb=True, res=_mm(dk, w["w_ck"][l], tb=True, name=tag + "_dmem_k"),
                name=tag + "_dmem_v")
    _, g["norm_mem_g"] = _rmsnorm_bwd(dmemn, mem, w["norm_mem_g"][l], None, name=tag + "_norm_mem_bwd")
    mixed, proj = sv["mixed"], sv["proj"]
    g["w_out"] = _mm(mixed, dx1, ta=True, name=tag + "_dw_out")
    mid = _PendingReduce(_big_grad_shards(g, GATHER_MID), pos, l) if reduce_early else None
    dmixed = _mm(dx1, w["w_out"][l], tb=True, name=tag + "_dmixed", comm=mid.sibling_comm() if mid else None)
    if mid:
        dmixed, from_sibling = dmixed
        mid.add(from_sibling)
    dproj = _sb_attn_bwd(proj, sv["ltot"], dmixed, lax.empty((t, PROJ_W), F32), nb=nb, name=tag + "_sb_bwd",
                         comm=early.chips_comm() if early else None)
    if early:
        dproj, others = dproj
        g_shard = early.finish(others, g_shard)
    res = _softmax_attn_bwd(
        proj, mixed, sv["lse_fox"], dmixed, nb=nb, mode="fox", mixer=MIX_FOX, dbuf=dproj,
        extra=(sv["cum_col"], sv["cum_row"]), name=tag + "_fox_bwd", comm=mid.chips_comm() if mid else None)
    if mid:
        res, others = res
        g_shard = mid.finish(others, g_shard)
    dproj, dcum_k, dcum_q = res
    dcum = (dcum_k.transpose(0, 1, 3, 2, 4).reshape(nb * N_HEADS, seq)
            + dcum_q.transpose(0, 1, 3, 2).reshape(nb * N_HEADS, seq))
    df_rows, db = _fox_gate_bwd(dcum, sv["f_rows"], lp["b_rows"])
    g["b_forget"] = db[:N_HEADS, 0]
    df = df_rows.reshape(nb, N_HEADS, seq).transpose(0, 2, 1).reshape(t, N_HEADS)
    dproj, dbias = _softmax_attn_bwd(proj, mixed, sv["lse_dil"], dmixed, nb=nb, mode="dil", mixer=MIX_DIL,
                                     dbuf=dproj, extra=(bias,), name=tag + "_dil_bwd")
    dproj, dvec, dwa, dwx = _lru_bwd(proj, lp["lru_vec"], lp["wa"], lp["wx"], dmixed, dproj, nb=nb,
                                     name=tag + "_lru_bwd")
    g["lru_conv_w"], g["lru_conv_b"], g["lru_b_a"], g["lru_b_x"], g["lru_lambda"] = (
        dvec[0:4], dvec[4], dvec[5], dvec[6], dvec[7])
    g["lru_w_a"], g["lru_w_x"] = _block_diag_grad(dwa), _block_diag_grad(dwx)
    dproj = lax.dynamic_update_slice(dproj, jnp.pad(df, ((0, 0), (0, PROJ_W - COL_F - N_HEADS))), (0, COL_F))
    g["w_in_padded"] = _mm(sv["h"], dproj, ta=True, name=tag + "_dw_in")
    dx0, dg = _mm(dproj, lp["w_in"], tb=True, norm_bwd=(sv["x0"], w["norm_mix_g"][l], dx1), ti=NORM_ROWS,
                  name=tag + "_dh")
    g["norm_mix_g"] = dg.reshape(-1)
    return dx0, g, dbias, g_shard


def _big_grad_shards(g, names):
    out = {}
    for n, axis in BIG:
        if n not in names:
            continue
        if n in ("w_up", "w_co"):
            out[n] = g[n]
        else:
            out[n] = _to_shards(_unpad_w_in(g["w_in_padded"]) if n == "w_in" else g[n], axis)
    return out


EARLY = ("w_down", "w_up")


def kernel(*args):
    a = dict(zip(INPUTS, args, strict=True))
    nb, seq, d = a["x"].shape
    depth = a["norm_mix_g"].shape[0]
    x = a["x"].reshape(nb * seq, d)
    mem = a["mem"].reshape(nb * a["mem"].shape[1], d)
    target = a["loss_target"].reshape(nb * seq, d)
    cx, cy, c = _mesh_pos()
    chip = 2 * cx + cy
    pos = jnp.stack([chip, c]).astype(jnp.int32)

    slots = {}
    for n, _ in BIG:
        own = a[n].astype(BF16)[:, None]
        slots[n] = lax.dynamic_update_slice(lax.empty((depth, 4) + own.shape[2:], BF16), own, (0, chip, 0, 0))
    w = {n: a[n] for n in REPLICATED}
    w.update({n: {} for n, _ in BIG}, w_in_padded={})
    gather = _WeightGather(slots, w, depth)
    bias, landed = _dil_bias(w["rel_bias"], seq, comm=gather.first_comm(0))
    gather.first_done(0, landed)
    h, landed = _rmsnorm(x, w["norm_mix_g"][0], name="l0_norm_mix", comm=gather.first_comm(1))
    gather.first_done(1, landed)
    cpk = _Packing([(n, a[n].shape) for n in CONV])
    conv = cpk.unpack(_chip_bcast(cpk.pack({n: a[n] for n in CONV}), name="gather_conv"), lead=(4,))
    for n in CONV:
        w[n] = jnp.moveaxis(conv[n], 0, 2).reshape(a[n].shape[:2] + (4 * a[n].shape[2],))

    lps, saved = [], []
    for l in range(depth):
        lps.append(_layer_params(w, l, nb))
        x, h, sv = _layer_fwd(x, h, mem, w, lps[l], l, w["norm_mix_g"][l + 1] if l + 1 < depth else None, bias, nb,
                              gather)
        saved.append(sv)
    loss, dx, dg_final = _loss_head(x, w["final_norm_g"], target)
    small_g = [None] * depth
    dbias, pending = None, None
    g_shard = {n: lax.empty(a[n].shape, F32) for n, _ in BIG}
    for l in reversed(range(depth)):
        bottom = l == 0
        dx, g, db, g_shard = _layer_bwd(dx, mem, saved[l], w, lps[l], l, bias, nb, pos, pending=pending,
                                        g_shard=g_shard, reduce_early=bottom)
        dbias = db if dbias is None else dbias + db
        small_g[l] = g
        left = [n for n, _ in BIG if not (bottom and n in EARLY + GATHER_MID)]
        pending = _PendingReduce(_big_grad_shards(g, left), pos, l)

    grads = {n: jnp.stack([small_g[l][n] for l in range(depth)]) for n in REPLICATED + CONV
             if n not in ("rel_bias", "final_norm_g")}
    grads["rel_bias"] = _dil_bias_bwd(dbias, seq)
    grads["final_norm_g"] = dg_final
    grads["loss"] = loss.reshape(1)
    spk = _Packing([(n, grads[n].shape) for n in REPLICATED + CONV + ("loss",)])
    s_flat = spk.pack(grads)
    pair = lax.dynamic_update_slice(lax.empty((2,) + s_flat.shape, F32), s_flat[None], (c, 0, 0))
    *from_sibling, pair = _run_comm(_merge_comms([pending.sibling_comm(), _pair_comm(pair)]), name="tail_sibling")
    pending.add(from_sibling)
    quad = lax.dynamic_update_slice(lax.empty((4,) + pair.shape, F32), pair[None], (chip, 0, 0, 0))
    *others, quad = _run_comm(_merge_comms([pending.chips_comm(), _quad_comm(quad)]), name="tail_chips")
    g_shard = pending.finish(others, g_shard)
    names = [n for n, _ in BIG]
    g_shard = dict(zip(names, _share_halves([g_shard[n] for n in names], name="reduce_share")))
    out = {}
    for n in names:
        operands = (a[n], g_shard[n], a["m_" + n], a["v_" + n])
        cols = a[n].shape[2]
        if cols % LANES:
            lead = max(b for b in range(1, 65) if cols % b == 0)
            res = _adamw(*[z.transpose(2, 0, 1) for z in operands], name="adamw_" + n, lead=lead)
            delta, new_m, new_v = [z.transpose(1, 2, 0) for z in res]
        else:
            delta, new_m, new_v = _adamw(*operands, name="adamw_" + n)
        out[n] = (g_shard[n], delta, new_m, new_v)
    total = spk.unpack(_sum_slots(quad, name="small_sum"))
    for n in CONV:
        width = a[n].shape[2]
        total[n] = lax.dynamic_slice_in_dim(total[n], chip * width, width, axis=2)
    apk = _Packing([(n, a[n].shape) for n in REPLICATED + CONV])
    s_out = _adamw(*[apk.pack(src)[None] for src in (
        {n: a[n] for n in REPLICATED + CONV}, total, {n: a["m_" + n] for n in REPLICATED + CONV},
        {n: a["v_" + n] for n in REPLICATED + CONV})], name="adamw_small")
    s_delta, s_m, s_v = [apk.unpack(o[0]) for o in s_out]
    for n in REPLICATED + CONV:
        out[n] = (total[n], s_delta[n], s_m[n], s_v[n])

    return (total["loss"].reshape(()), dx.reshape(nb, seq, d), *[out[n][0] for n in WEIGHTS],
            *[out[n][1] for n in WEIGHTS], *[out[n][2] for n in WEIGHTS], *[out[n][3] for n in WEIGHTS])
```

```python
import math

import numpy as np
import jax
import jax.numpy as jnp
from jax import lax
from jax.experimental import pallas as pl
from jax.experimental.pallas import tpu as pltpu

F32 = jnp.float32
BF16 = jnp.bfloat16

HEAD_DIM = 64
N_HEADS = 4
N_IN = 2820
D_FF = 2816
LRU_C = 8.0
EPS = 1e-6
NUM_BUCKETS = 32
MAX_DISTANCE = 2048
DILATED_PATTERNS = ((128, 1), (512, 4), (2048, 16))
ADAM_LR, ADAM_B1, ADAM_B2, ADAM_EPS, ADAM_WD, ADAM_STEP = 0.001, 0.9, 0.999, 1e-08, 0.01, 10

LANES = 128
SUBLANES = 8
VMEM_LIMIT = 48 * 1024 * 1024

PROJ_W = 3072
PAIR_W = 3 * LANES
LRU_W = 2 * LANES
COL_LRU = 6 * PAIR_W
COL_F = COL_LRU + 2 * LRU_W
MIX_SB, MIX_FOX, MIX_DIL, MIX_LRU = 0, 1, 2, 3
ORIG_COL = {MIX_SB: 0, MIX_FOX: 768, MIX_DIL: 1540}
ORIG_LRU_X, ORIG_LRU_G = 2308, 2564

ATT_TILE = 256
FOX_TILE = 512
MASKED = -1e30
SCALE = HEAD_DIM ** -0.5

NT_DIMS = (((1,), (1,)), ((), ()))
TN_DIMS = (((0,), (0,)), ((), ()))

MESH = pl.DeviceIdType.MESH
ANY = pl.BlockSpec(memory_space=pl.ANY)


def _params(sem):
    return pltpu.CompilerParams(dimension_semantics=sem, vmem_limit_bytes=VMEM_LIMIT)


def _tile(n, target, unit=LANES):
    if n <= target:
        return n
    t = (target // unit) * unit
    while t > unit and n % t:
        t -= unit
    assert n % t == 0, (n, target, unit)
    return t


def _mm(a, b, *, ta=False, tb=False, res=None, col_shards=1, halves=None, b_chunks=1, norm_g=None, norm_bwd=None,
        comm=None, name, ti=1024, tj=1408, tc=1408):
    if halves == "a":
        m, kc = a.shape[1], 2 * a.shape[2]
    else:
        m, kc = (a.shape[1], a.shape[0]) if ta else a.shape
    n_blk, k_blk = None, kc // 2 if halves == "a" else kc
    if halves == "b":
        n = 2 * b.shape[2]
        assert b.shape[1] == kc
        n_blk = n // 2
    elif b_chunks > 1:
        n = b.shape[1] if tb else b_chunks * b.shape[2]
        assert (b_chunks * b.shape[2] if tb else b.shape[1]) == kc
        if tb:
            k_blk = min(k_blk, kc // b_chunks)
        else:
            n_blk = n // b_chunks
    else:
        n = b.shape[0] if tb else b.shape[1]
        assert (b.shape[1] if tb else b.shape[0]) == kc
    assert n % col_shards == 0
    n_blk = min(n_blk or n, n // col_shards)
    ti, tj, tc = (_tile(m, ti, LANES if ta else SUBLANES), _tile(n_blk, tj),
                  _tile(k_blk, tc, SUBLANES if ta and tb else LANES))
    per_shard, per_half_j, per_half_k = n // col_shards // tj, n // 2 // tj, kc // 2 // tc
    per_chunk = (kc if tb else n) // b_chunks // (tc if tb else tj)
    nk = kc // tc
    dims = (((0 if ta else 1,), (1 if tb else 0,)), ((), ()))
    rows_whole = norm_g is not None or norm_bwd is not None
    assert not rows_whole or (tj == n and col_shards == 1)
    n_extra = (res is not None) + (norm_g is not None) + (3 if norm_bwd is not None else 0)
    n_out = 2 if rows_whole else 1

    def finish(val, ex, outs):
        if res is not None:
            val = ex[0][...] + val
        if norm_g is not None:
            outs[0][...] = val
            outs[1][...] = (_xhat(val) * ex[-1][...]).astype(BF16)
        elif norm_bwd is not None:
            x_ref, g_ref, r_ref = ex[-3:]
            dx, dgr = _norm_bwd_rows(val, x_ref[...], g_ref[...])
            outs[0][...] = r_ref[...] + dx

            @pl.when(pl.program_id(0) == 0)
            def _():
                outs[1][...] = jnp.zeros_like(outs[1])

            outs[1][...] += jnp.sum(dgr, axis=0, keepdims=True)
        else:
            outs[0][...] = val

    def body(*refs):
        a_ref, b_ref = refs[:2]
        ex = refs[2:2 + n_extra]
        outs = refs[2 + n_extra:2 + n_extra + n_out]
        part = lax.dot_general(a_ref[...].astype(BF16), b_ref[...].astype(BF16), dims, preferred_element_type=F32)
        if nk == 1:
            finish(part, ex, outs)
            return
        acc_ref = refs[-1]
        k = pl.program_id(2)

        @pl.when(k == 0)
        def _():
            acc_ref[...] = part

        @pl.when(k > 0)
        def _():
            acc_ref[...] += part

        @pl.when(k == nk - 1)
        def _():
            finish(acc_ref[...], ex, outs)

    if halves == "a":
        a_spec = pl.BlockSpec((None, ti, tc), lambda i, j, k: (k // per_half_k, i, k % per_half_k))
    elif ta:
        a_spec = pl.BlockSpec((tc, ti), lambda i, j, k: (k, i))
    else:
        a_spec = pl.BlockSpec((ti, tc), lambda i, j, k: (i, k))
    if halves == "b":
        b_spec = pl.BlockSpec((None, tc, tj), lambda i, j, k: (j // per_half_j, k, j % per_half_j))
    elif b_chunks > 1 and tb:
        b_spec = pl.BlockSpec((None, tj, tc), lambda i, j, k: (k // per_chunk, j, k % per_chunk))
    elif b_chunks > 1:
        b_spec = pl.BlockSpec((None, tc, tj), lambda i, j, k: (j // per_chunk, k, j % per_chunk))
    elif tb:
        b_spec = pl.BlockSpec((tj, tc), lambda i, j, k: (j, k))
    else:
        b_spec = pl.BlockSpec((tc, tj), lambda i, j, k: (k, j))
    o_spec = pl.BlockSpec((ti, tj), lambda i, j, k: (i, j))
    vec = pl.BlockSpec((1, tj), lambda i, j, k: (0, 0))
    in_specs, args = [a_spec, b_spec], [a, b]
    out_specs, out_shape = [o_spec], [jax.ShapeDtypeStruct((m, n), F32)]
    if res is not None:
        in_specs.append(o_spec)
        args.append(res)
    if norm_g is not None:
        in_specs.append(vec)
        args.append(norm_g.reshape(1, n))
        out_specs.append(o_spec)
        out_shape.append(jax.ShapeDtypeStruct((m, n), BF16))
    if norm_bwd is not None:
        x, g, dres = norm_bwd
        in_specs += [o_spec, vec, o_spec]
        args += [x, g.reshape(1, n), dres]
        out_specs.append(vec)
        out_shape.append(jax.ShapeDtypeStruct((1, n), F32))
    if col_shards > 1:
        assert n_extra == 0
        out_specs = [pl.BlockSpec((None, ti, tj), lambda i, j, k: (j // per_shard, i, j % per_shard))]
        out_shape = [jax.ShapeDtypeStruct((col_shards, m, n // col_shards), F32)]
    sem = ("arbitrary",) * 3 if norm_bwd is not None else ("parallel", "parallel", "arbitrary")
    out, carried = _pallas(body, name=name, grid=(m // ti, n // tj, nk), in_specs=in_specs, out_specs=out_specs,
                           out_shape=out_shape, args=args, scratch=[] if nk == 1 else [pltpu.VMEM((ti, tj), F32)],
                           sem=sem, comm=comm)
    out = out if rows_whole else out[0]
    return out if comm is None else (out, carried)


def _xhat(x):
    return x * lax.rsqrt(jnp.mean(x * x, axis=-1, keepdims=True) + EPS)


def _norm_bwd_rows(dy, x, g):
    rstd = lax.rsqrt(jnp.mean(x * x, axis=-1, keepdims=True) + EPS)
    xh = x * rstd
    dxh = dy * g
    dx = rstd * (dxh - xh * jnp.mean(dxh * xh, axis=-1, keepdims=True))
    return dx, dy * xh


def _rmsnorm(x, g, *, name, rows=512, comm=None):
    t, d = x.shape
    tr = _tile(t, rows, 2 * SUBLANES)

    def body(x_ref, g_ref, o_ref):
        o_ref[...] = (_xhat(x_ref[...]) * g_ref[...]).astype(BF16)

    out, carried = _pallas(
        body, name=name, grid=(t // tr,),
        in_specs=[pl.BlockSpec((tr, d), lambda i: (i, 0)), pl.BlockSpec((1, d), lambda i: (0, 0))],
        out_specs=[pl.BlockSpec((tr, d), lambda i: (i, 0))], out_shape=[jax.ShapeDtypeStruct((t, d), BF16)],
        args=[x, g.reshape(1, d)], sem=("parallel",), comm=comm)
    return out[0] if comm is None else (out[0], carried)


def _rmsnorm_bwd(dy, x, g, dres, *, name, rows=512):
    t, d = x.shape
    tr = _tile(t, rows, SUBLANES)

    def body(*refs):
        if dres is None:
            dy_ref, x_ref, g_ref, dx_ref, dg_ref = refs
        else:
            dy_ref, x_ref, g_ref, r_ref, dx_ref, dg_ref = refs
        dx, dgr = _norm_bwd_rows(dy_ref[...], x_ref[...], g_ref[...])
        dx_ref[...] = dx if dres is None else r_ref[...] + dx

        @pl.when(pl.program_id(0) == 0)
        def _():
            dg_ref[...] = jnp.zeros_like(dg_ref)

        dg_ref[...] += jnp.sum(dgr, axis=0, keepdims=True)

    row = pl.BlockSpec((tr, d), lambda i: (i, 0))
    vec = pl.BlockSpec((1, d), lambda i: (0, 0))
    in_specs = [row, row, vec] + ([] if dres is None else [row])
    args = (dy, x, g.reshape(1, d)) + (() if dres is None else (dres,))
    dx, dg = pl.pallas_call(
        body, name=name, grid=(t // tr,), in_specs=in_specs, out_specs=[row, vec],
        out_shape=[jax.ShapeDtypeStruct((t, d), F32), jax.ShapeDtypeStruct((1, d), F32)],
        compiler_params=_params(("arbitrary",)))(*args)
    return dx, dg.reshape(d)


def _loss_head(x, g, target, *, rows=512):
    t, d = x.shape
    tr = _tile(t, rows, SUBLANES)

    def body(x_ref, g_ref, t_ref, dx_ref, dg_ref, loss_ref):
        x_, g_ = x_ref[...], g_ref[...]
        err = _xhat(x_) * g_ - t_ref[...]
        dx, dgr = _norm_bwd_rows(err * (1.0 / d), x_, g_)
        dx_ref[...] = dx

        @pl.when(pl.program_id(0) == 0)
        def _():
            dg_ref[...] = jnp.zeros_like(dg_ref)
            loss_ref[...] = jnp.zeros_like(loss_ref)

        dg_ref[...] += jnp.sum(dgr, axis=0, keepdims=True)
        loss_ref[...] += 0.5 * jnp.sum(jnp.mean(err * err, axis=-1, keepdims=True), axis=0, keepdims=True)

    row = pl.BlockSpec((tr, d), lambda i: (i, 0))
    vec = pl.BlockSpec((1, d), lambda i: (0, 0))
    one = pl.BlockSpec((1, 1), lambda i: (0, 0))
    dx, dg, loss = pl.pallas_call(
        body, name="loss_head", grid=(t // tr,), in_specs=[row, vec, row], out_specs=[row, vec, one],
        out_shape=[jax.ShapeDtypeStruct((t, d), F32), jax.ShapeDtypeStruct((1, d), F32),
                   jax.ShapeDtypeStruct((1, 1), F32)],
        compiler_params=_params(("arbitrary",)))(x, g.reshape(1, d), target)
    return loss.reshape(()), dx, dg.reshape(d)


def _head_masks(shape):
    lane = lax.broadcasted_iota(jnp.int32, shape, len(shape) - 1)
    return lane < HEAD_DIM, lane >= HEAD_DIM


def _split_heads(x):
    m0, m1 = _head_masks(x.shape)
    zero = jnp.zeros_like(x)
    return jnp.where(m0, x, zero), jnp.where(m1, x, zero)


def _lane_pair(a0, a1, rows):
    m0, _ = _head_masks((rows, LANES))
    return jnp.where(m0, a0, a1)


def _qkv_readers(refs, packed):
    if packed:
        (r,) = refs
        return tuple((lambda r0, n, s=s: r[pl.ds(r0, n), s * LANES:(s + 1) * LANES]) for s in range(3))
    return tuple((lambda r0, n, ref=ref: ref[pl.ds(r0, n), :]) for ref in refs)


def _pair_spec(seq, col0, width=LANES):
    return pl.BlockSpec((seq, width), lambda p, b: (b, col0 + p))


def _fox_specs(seq, nk, tk):
    return [pl.BlockSpec((None, None, seq, 2), lambda p, b: (b, p, 0, 0)),
            pl.BlockSpec((None, None, nk, 2, tk), lambda p, b: (b, p, 0, 0, 0))]


def _softmax_attn_fwd(src, *, nb, mode, mixer=None, out_buf=None, extra=(), name, comm=None):
    packed = mode != "cross"
    n_src = 1 if packed else 3
    seq_q = (src if packed else src[0]).shape[0] // nb
    seq_k = seq_q if packed else src[1].shape[0] // nb
    tile = FOX_TILE if mode == "fox" else ATT_TILE
    tq, tk = min(tile, seq_q), min(tile, seq_k)
    nq, nk = seq_q // tq, seq_k // tk
    n_ex = len(extra)

    def body(*refs):
        q_at, k_at, v_at = _qkv_readers(refs[:n_src], packed)
        ex = refs[n_src:n_src + n_ex]
        o_ref, lse_ref = refs[-2:]

        def q_tile(i, _):
            r0 = pl.multiple_of(i * tq, tq)
            qm = _split_heads((q_at(r0, tq) * SCALE).astype(BF16))
            if mode == "fox":
                cq = ex[0][pl.ds(r0, tq), :]
                row = r0 + lax.broadcasted_iota(jnp.int32, (tq, tk), 0)

            def k_tile(j, carry, diagonal=False):
                m, l, acc = carry
                c0 = pl.multiple_of(j * tk, tk)
                kt = k_at(c0, tk).astype(BF16)
                vm = _split_heads(v_at(c0, tk).astype(BF16))
                if mode == "fox":
                    ck = ex[1][j]
                hs = range(2)
                s = [lax.dot_general(qm[h], kt, NT_DIMS, preferred_element_type=F32) for h in hs]
                if mode == "fox":
                    s = [s[h] + cq[:, h:h + 1] - ck[h:h + 1, :] for h in hs]
                    if diagonal:
                        keep = (c0 + lax.broadcasted_iota(jnp.int32, (tq, tk), 1)) <= row
                        s = [jnp.where(keep, s[h], MASKED) for h in hs]
                elif mode == "dil":
                    s = [s[h] + ex[0][h, i - j] for h in hs]
                new_m = [jnp.maximum(m[h], jnp.max(s[h], axis=-1, keepdims=True)) for h in hs]
                p = [jnp.exp(s[h] - new_m[h]) for h in hs]
                alpha = [jnp.exp(m[h] - new_m[h]) for h in hs]
                new_l = [alpha[h] * l[h] + jnp.sum(p[h], axis=-1, keepdims=True) for h in hs]
                pv = [jnp.dot(p[h].astype(BF16), vm[h], preferred_element_type=F32) for h in hs]
                acc = acc * _lane_pair(alpha[0], alpha[1], tq) + (pv[0] + pv[1])
                return tuple(new_m), tuple(new_l), acc

            init = ((jnp.full((tq, 1), MASKED, F32),) * 2, (jnp.zeros((tq, 1), F32),) * 2,
                    jnp.zeros((tq, LANES), F32))
            if mode == "fox":
                m, l, acc = k_tile(i, lax.fori_loop(0, i, k_tile, init), True)
            else:
                m, l, acc = lax.fori_loop(0, i + 1 if packed else nk, k_tile, init)
            o_ref[pl.ds(r0, tq), :] = acc / _lane_pair(l[0], l[1], tq)
            lse_ref[pl.ds(r0, tq), :] = _lane_pair(m[0] + jnp.log(l[0]), m[1] + jnp.log(l[1]), tq)
            return 0

        lax.fori_loop(0, nq, q_tile, 0)

    lse_shape = jax.ShapeDtypeStruct((nb * seq_q, 2 * LANES), F32)
    if packed:
        in_specs, args = [_pair_spec(seq_q, 2 * mixer, PAIR_W)], [src]
        in_specs += _fox_specs(seq_q, nk, tk) if mode == "fox" else [
            pl.BlockSpec((None, 2, nq, tq, tk), lambda p, b: (p, 0, 0, 0, 0))]
        args += list(extra) + [out_buf]
        in_specs.append(ANY)
        out_specs = [_pair_spec(seq_q, 2 * mixer), _pair_spec(seq_q, 0)]
        out_shape = [jax.ShapeDtypeStruct(out_buf.shape, F32), lse_shape]
        aliases = {len(args) - 1: 0}
    else:
        in_specs = [_pair_spec(seq_q, 0), _pair_spec(seq_k, 0), _pair_spec(seq_k, 0)]
        args = list(src)
        out_specs = [_pair_spec(seq_q, 0), _pair_spec(seq_q, 0)]
        out_shape = [lse_shape, lse_shape]
        aliases = {}
    out, carried = _pallas(body, name=name, grid=(2, nb), in_specs=in_specs, out_specs=out_specs, out_shape=out_shape,
                           args=args, aliases=aliases, sem=("parallel", "arbitrary"), comm=comm)
    return out if comm is None else (out, carried)


def _softmax_attn_bwd(src, o, lse, do, *, nb, mode, mixer=None, dbuf=None, extra=(), name, comm=None):
    packed = mode != "cross"
    n_src = 1 if packed else 3
    seq_q = (src if packed else src[0]).shape[0] // nb
    seq_k = seq_q if packed else src[1].shape[0] // nb
    tile = FOX_TILE if mode == "fox" else ATT_TILE
    tq, tk = min(tile, seq_q), min(tile, seq_k)
    nq, nk = seq_q // tq, seq_k // tk
    n_ex = len(extra)
    n_in = n_src + 3 + n_ex + (1 if packed else 0)

    def body(*refs):
        q_at, k_at, v_at = _qkv_readers(refs[:n_src], packed)
        o_ref, lse_ref, do_ref = refs[n_src:n_src + 3]
        ex = refs[n_src + 3:n_src + 3 + n_ex]
        outs = refs[n_in:]
        if packed:
            d_ref = outs[0]
            dq_w = lambda r0, val: d_ref.__setitem__((pl.ds(r0, tq), slice(0, LANES)), val)
            dk_ref = d_ref.at[:, LANES:2 * LANES]
            dv_ref = d_ref.at[:, 2 * LANES:3 * LANES]
        else:
            dq_ref, dk_ref, dv_ref = outs[:3]
            dq_w = lambda r0, val: dq_ref.__setitem__((pl.ds(r0, tq), slice(None)), val)
        dk_ref[...] = jnp.zeros((seq_k, LANES), F32)
        dv_ref[...] = jnp.zeros((seq_k, LANES), F32)
        if mode == "fox":
            dcum_ref, dcq_ref = outs[-2:]
            dcum_ref[...] = jnp.zeros_like(dcum_ref)
        if mode == "dil":
            dbias_ref = outs[-1]

            @pl.when(pl.program_id(1) == 0)
            def _():
                dbias_ref[...] = jnp.zeros_like(dbias_ref)

        def q_tile(i, _):
            r0 = pl.multiple_of(i * tq, tq)
            qm = _split_heads((q_at(r0, tq) * SCALE).astype(BF16))
            do_f = do_ref[pl.ds(r0, tq), :]
            dom = _split_heads(do_f.astype(BF16))
            dd = _split_heads(do_f * o_ref[pl.ds(r0, tq), :])
            delta = [jnp.sum(dd[h], axis=-1, keepdims=True) for h in range(2)]
            lse_t = lse_ref[pl.ds(r0, tq), :]
            lse_h = [lse_t[:, 0:1], lse_t[:, HEAD_DIM:HEAD_DIM + 1]]
            if mode == "fox":
                cq = ex[0][pl.ds(r0, tq), :]
                row = r0 + lax.broadcasted_iota(jnp.int32, (tq, tk), 0)

            def k_tile(j, carry, diagonal=False):
                dq, rs = carry
                c0 = pl.multiple_of(j * tk, tk)
                kt = k_at(c0, tk).astype(BF16)
                vt = v_at(c0, tk).astype(BF16)
                km = _split_heads(kt)
                if mode == "fox":
                    ck = ex[1][j]
                hs = range(2)
                s = [lax.dot_general(qm[h], kt, NT_DIMS, preferred_element_type=F32) for h in hs]
                dp = [lax.dot_general(dom[h], vt, NT_DIMS, preferred_element_type=F32) for h in hs]
                if mode == "fox":
                    s = [s[h] + cq[:, h:h + 1] - ck[h:h + 1, :] for h in hs]
                    if diagonal:
                        keep = (c0 + lax.broadcasted_iota(jnp.int32, (tq, tk), 1)) <= row
                        s = [jnp.where(keep, s[h], MASKED) for h in hs]
                elif mode == "dil":
                    s = [s[h] + ex[0][h, i - j] for h in hs]
                p = [jnp.exp(s[h] - lse_h[h]) for h in hs]
                ds = [p[h] * (dp[h] - delta[h]) for h in hs]
                dsb = [ds[h].astype(BF16) for h in hs]
                pb = [p[h].astype(BF16) for h in hs]
                dq = dq + (jnp.dot(dsb[0], km[0], preferred_element_type=F32)
                           + jnp.dot(dsb[1], km[1], preferred_element_type=F32))
                dk_t = (lax.dot_general(dsb[0], qm[0], TN_DIMS, preferred_element_type=F32)
                        + lax.dot_general(dsb[1], qm[1], TN_DIMS, preferred_element_type=F32))
                dv_t = (lax.dot_general(pb[0], dom[0], TN_DIMS, preferred_element_type=F32)
                        + lax.dot_general(pb[1], dom[1], TN_DIMS, preferred_element_type=F32))
                if mode == "fox":
                    for h in hs:
                        dcum_ref[j, h:h + 1, :] -= jnp.sum(ds[h], axis=0, keepdims=True)
                    rs = tuple(rs[h] + jnp.sum(ds[h], axis=-1, keepdims=True) for h in hs)
                elif mode == "dil":
                    for h in hs:
                        dbias_ref[h, i - j] += ds[h]
                dk_ref[pl.ds(c0, tk), :] += dk_t
                dv_ref[pl.ds(c0, tk), :] += dv_t
                return dq, rs

            zero = (jnp.zeros((tq, 1), F32),) * 2
            init = (jnp.zeros((tq, LANES), F32), zero)
            if mode == "fox":
                dq, rs = k_tile(i, lax.fori_loop(0, i, k_tile, init), True)
            else:
                dq, rs = lax.fori_loop(0, i + 1 if packed else nk, k_tile, init)
            dq_w(r0, dq * SCALE)
            if mode == "fox":
                dcq_ref[pl.ds(r0, tq), :] = jnp.where(lax.broadcasted_iota(jnp.int32, (tq, 2), 1) == 0, rs[0], rs[1])
            return 0

        lax.fori_loop(0, nq, q_tile, 0)

    if packed:
        in_specs = [_pair_spec(seq_q, 2 * mixer, PAIR_W), _pair_spec(seq_q, 2 * mixer), _pair_spec(seq_q, 0),
                    _pair_spec(seq_q, 2 * mixer)]
        args = [src, o, lse, do]
        out_specs = [_pair_spec(seq_q, 2 * mixer, PAIR_W)]
        out_shape = [jax.ShapeDtypeStruct(dbuf.shape, F32)]
        if mode == "fox":
            in_specs += _fox_specs(seq_q, nk, tk)
            out_specs += [_fox_specs(seq_q, nk, tk)[1], _fox_specs(seq_q, nk, tk)[0]]
            out_shape += [jax.ShapeDtypeStruct((nb, 2, nk, 2, tk), F32), jax.ShapeDtypeStruct((nb, 2, seq_q, 2), F32)]
        else:
            tiles = pl.BlockSpec((None, 2, nq, tq, tk), lambda p, b: (p, 0, 0, 0, 0))
            in_specs.append(tiles)
            out_specs.append(tiles)
            out_shape.append(jax.ShapeDtypeStruct((2, 2, nq, tq, tk), F32))
        args += list(extra) + [dbuf]
        in_specs.append(ANY)
        aliases = {len(args) - 1: 0}
    else:
        sq, sk = _pair_spec(seq_q, 0), _pair_spec(seq_k, 0)
        in_specs, args = [sq, sk, sk, sq, sq, sq], list(src) + [o, lse, do]
        out_specs = [sq, sk, sk]
        out_shape = [jax.ShapeDtypeStruct((nb * seq_q, 2 * LANES), F32)] + [
            jax.ShapeDtypeStruct((nb * seq_k, 2 * LANES), F32)] * 2
        aliases = {}
    out, carried = _pallas(body, name=name, grid=(2, nb), in_specs=in_specs, out_specs=out_specs, out_shape=out_shape,
                           args=args, aliases=aliases, sem=("parallel", "arbitrary"), comm=comm)
    return out if comm is None else (out, carried)


def _log_sigmoid(z):
    return jnp.minimum(z, 0.0) - jnp.log(1.0 + jnp.exp(-jnp.abs(z)))


def _split_bf16(x):
    hi = x.astype(BF16)
    return hi, (x - hi.astype(F32)).astype(BF16)


def _tri(n, fn):
    r = lax.broadcasted_iota(jnp.int32, (n, n), 0)
    c = lax.broadcasted_iota(jnp.int32, (n, n), 1)
    return jnp.where(fn(r, c), 1.0, 0.0).astype(BF16)


def _sb_attn_fwd(proj, out_buf, *, nb, name, comm=None):
    seq = proj.shape[0] // nb
    tq = tk = min(ATT_TILE, seq)
    nq = seq // tq

    def body(qkv_ref, _, o_ref, lt_ref):
        rd = [_qkv_readers((qkv_ref.at[:, pr * PAIR_W:(pr + 1) * PAIR_W],), True) for pr in range(2)]
        after = _tri(tk, lambda r, c: r > c)
        ch = [(pr, h) for pr in range(2) for h in range(2)]

        def q_tile(i, _):
            r0 = pl.multiple_of(i * tq, tq)
            qm = [_split_heads((rd[pr][0](r0, tq) * SCALE).astype(BF16)) for pr in range(2)]
            row = r0 + lax.broadcasted_iota(jnp.int32, (tq, tk), 0)

            def k_tile(j, carry, diagonal):
                c, acc = carry
                c0 = pl.multiple_of(j * tk, tk)
                kt = [rd[pr][1](c0, tk).astype(BF16) for pr in range(2)]
                vm = [_split_heads(rd[pr][2](c0, tk).astype(BF16)) for pr in range(2)]
                if diagonal:
                    strict = (c0 + lax.broadcasted_iota(jnp.int32, (tq, tk), 1)) < row
                ns = range(len(ch))
                z = [lax.dot_general(qm[pr][h], kt[pr], NT_DIMS, preferred_element_type=F32) for pr, h in ch]
                ls = [_log_sigmoid(z[n]) for n in ns]
                lk = [ls[n] - z[n] for n in ns]
                if diagonal:
                    lk = [jnp.where(strict, lk[n], 0.0) for n in ns]
                parts = [_split_bf16(lk[n]) for n in ns]
                sfx = [jnp.dot(parts[n][0], after, preferred_element_type=F32)
                       + jnp.dot(parts[n][1], after, preferred_element_type=F32) for n in ns]
                att = [jnp.exp(ls[n] + sfx[n] + c[n]) for n in ns]
                if diagonal:
                    att = [jnp.where(strict, att[n], 0.0) for n in ns]
                acc = tuple(acc[pr] + (jnp.dot(att[2 * pr].astype(BF16), vm[pr][0], preferred_element_type=F32)
                                       + jnp.dot(att[2 * pr + 1].astype(BF16), vm[pr][1], preferred_element_type=F32))
                            for pr in range(2))
                return tuple(c[n] + jnp.sum(lk[n], axis=-1, keepdims=True) for n in ns), acc

            init = ((jnp.zeros((tq, 1), F32),) * 4, (jnp.zeros((tq, LANES), F32),) * 2)
            c, acc = lax.fori_loop(1, i + 1, lambda jj, cr: k_tile(i - jj, cr, False), k_tile(i, init, True))
            for pr in range(2):
                o_ref[pl.ds(r0, tq), pr * LANES:(pr + 1) * LANES] = acc[pr]
                lt_ref[pl.ds(r0, tq), pr * LANES:(pr + 1) * LANES] = _lane_pair(c[2 * pr], c[2 * pr + 1], tq)
            return 0

        lax.fori_loop(0, nq, q_tile, 0)

    both = lambda width, col: pl.BlockSpec((seq, 2 * width), lambda b: (b, col))
    out, carried = _pallas(
        body, name=name, grid=(nb,), in_specs=[both(PAIR_W, MIX_SB), ANY],
        out_specs=[both(LANES, MIX_SB), both(LANES, 0)],
        out_shape=[jax.ShapeDtypeStruct(out_buf.shape, F32), jax.ShapeDtypeStruct((nb * seq, 2 * LANES), F32)],
        args=[proj, out_buf], aliases={1: 0}, sem=("arbitrary",), comm=comm)
    return out if comm is None else (out, carried)


def _sb_attn_bwd(proj, ltot, do, dbuf, *, nb, name, comm=None):
    seq = proj.shape[0] // nb
    tq = tk = min(ATT_TILE, seq)
    nq = seq // tq

    def body(qkv_ref, lt_ref, do_ref, _, d_ref):
        rd = [_qkv_readers((qkv_ref.at[:, pr * PAIR_W:(pr + 1) * PAIR_W],), True) for pr in range(2)]
        upto = _tri(tk, lambda r, c: r <= c)
        before = _tri(tk, lambda r, c: r < c)
        dk_ref = [d_ref.at[:, pr * PAIR_W + LANES:pr * PAIR_W + 2 * LANES] for pr in range(2)]
        dv_ref = [d_ref.at[:, pr * PAIR_W + 2 * LANES:(pr + 1) * PAIR_W] for pr in range(2)]
        for ref in dk_ref + dv_ref:
            ref[...] = jnp.zeros((seq, LANES), F32)
        ch = [(pr, h) for pr in range(2) for h in range(2)]

        def q_tile(i, _):
            r0 = pl.multiple_of(i * tq, tq)
            qm = [_split_heads((rd[pr][0](r0, tq) * SCALE).astype(BF16)) for pr in range(2)]
            dom = [_split_heads(do_ref[pl.ds(r0, tq), pr * LANES:(pr + 1) * LANES].astype(BF16)) for pr in range(2)]
            lt_t = lt_ref[pl.ds(r0, tq), :]
            lt_h = [lt_t[:, pr * LANES + h * HEAD_DIM:pr * LANES + h * HEAD_DIM + 1] for pr, h in ch]
            row = r0 + lax.broadcasted_iota(jnp.int32, (tq, tk), 0)

            def k_tile(j, carry, diagonal):
                pc, qc, dq = carry
                c0 = pl.multiple_of(j * tk, tk)
                kt = [rd[pr][1](c0, tk).astype(BF16) for pr in range(2)]
                vt = [rd[pr][2](c0, tk).astype(BF16) for pr in range(2)]
                km = [_split_heads(kt[pr]) for pr in range(2)]
                if diagonal:
                    strict = (c0 + lax.broadcasted_iota(jnp.int32, (tq, tk), 1)) < row
                ns = range(len(ch))
                z = [lax.dot_general(qm[pr][h], kt[pr], NT_DIMS, preferred_element_type=F32) for pr, h in ch]
                da = [lax.dot_general(dom[pr][h], vt[pr], NT_DIMS, preferred_element_type=F32) for pr, h in ch]
                ls = [_log_sigmoid(z[n]) for n in ns]
                lk = [ls[n] - z[n] for n in ns]
                if diagonal:
                    lk = [jnp.where(strict, lk[n], 0.0) for n in ns]
                parts = [_split_bf16(lk[n]) for n in ns]
                pin = [jnp.dot(parts[n][0], upto, preferred_element_type=F32)
                       + jnp.dot(parts[n][1], upto, preferred_element_type=F32) for n in ns]
                att = [jnp.exp(ls[n] + (lt_h[n] - pc[n] - pin[n])) for n in ns]
                if diagonal:
                    att = [jnp.where(strict, att[n], 0.0) for n in ns]
                dg = [att[n] * da[n] for n in ns]
                qx = [qc[n] + jnp.dot(dg[n].astype(BF16), before, preferred_element_type=F32) for n in ns]
                sig = [jnp.exp(ls[n]) for n in ns]
                dz = [dg[n] * (1.0 - sig[n]) - sig[n] * qx[n] for n in ns]
                if diagonal:
                    dz = [jnp.where(strict, dz[n], 0.0) for n in ns]
                dzb = [dz[n].astype(BF16) for n in ns]
                attb = [att[n].astype(BF16) for n in ns]
                new_dq = []
                for pr in range(2):
                    a, b = 2 * pr, 2 * pr + 1
                    new_dq.append(dq[pr] + (jnp.dot(dzb[a], km[pr][0], preferred_element_type=F32)
                                            + jnp.dot(dzb[b], km[pr][1], preferred_element_type=F32)))
                    dk_ref[pr][pl.ds(c0, tk), :] += (
                        lax.dot_general(dzb[a], qm[pr][0], TN_DIMS, preferred_element_type=F32)
                        + lax.dot_general(dzb[b], qm[pr][1], TN_DIMS, preferred_element_type=F32))
                    dv_ref[pr][pl.ds(c0, tk), :] += (
                        lax.dot_general(attb[a], dom[pr][0], TN_DIMS, preferred_element_type=F32)
                        + lax.dot_general(attb[b], dom[pr][1], TN_DIMS, preferred_element_type=F32))
                return (tuple(pc[n] + jnp.sum(lk[n], axis=-1, keepdims=True) for n in ns),
                        tuple(qc[n] + jnp.sum(dg[n], axis=-1, keepdims=True) for n in ns), tuple(new_dq))

            zero = (jnp.zeros((tq, 1), F32),) * 4
            init = (zero, zero, (jnp.zeros((tq, LANES), F32),) * 2)
            carry = lax.fori_loop(0, i, lambda j, cr: k_tile(j, cr, False), init)
            _, _, dq = k_tile(i, carry, True)
            for pr in range(2):
                d_ref[pl.ds(r0, tq), pr * PAIR_W:pr * PAIR_W + LANES] = dq[pr] * SCALE
            return 0

        lax.fori_loop(0, nq, q_tile, 0)

    both = lambda width, col: pl.BlockSpec((seq, 2 * width), lambda b: (b, col))
    out, carried = _pallas(
        body, name=name, grid=(nb,), in_specs=[both(PAIR_W, MIX_SB), both(LANES, 0), both(LANES, MIX_SB), ANY],
        out_specs=[both(PAIR_W, MIX_SB)], out_shape=[jax.ShapeDtypeStruct(dbuf.shape, F32)],
        args=[proj, ltot, do, dbuf], aliases={3: 0}, sem=("arbitrary",), comm=comm)
    return out[0] if comm is None else (out[0], carried)


def _lane_scan(x, reverse=False):
    n = x.shape[-1]
    lane = lax.broadcasted_iota(jnp.int32, x.shape, 1)
    k = 1
    while k < n:
        if reverse:
            x = x + jnp.where(lane < n - k, pltpu.roll(x, n - k, 1), 0.0)
        else:
            x = x + jnp.where(lane >= k, pltpu.roll(x, k, 1), 0.0)
        k *= 2
    return x


def _fox_gate_fwd(f_rows, b_rows):
    def body(f_ref, b_ref, o_ref):
        o_ref[...] = _lane_scan(_log_sigmoid(f_ref[...] + b_ref[...]))

    return pl.pallas_call(body, name="fox_gate_fwd", out_shape=jax.ShapeDtypeStruct(f_rows.shape, F32))(f_rows, b_rows)


def _fox_gate_bwd(dcum, f_rows, b_rows):
    def body(d_ref, f_ref, b_ref, df_ref, db_ref):
        z = f_ref[...] + b_ref[...]
        df = _lane_scan(d_ref[...], reverse=True) * jnp.exp(_log_sigmoid(-z))
        df_ref[...] = df
        rs = jnp.sum(df, axis=-1, keepdims=True)
        tot = rs
        for e in range(1, f_rows.shape[0] // N_HEADS):
            tot = tot + pltpu.roll(rs, e * N_HEADS, 0)
        db_ref[...] = tot

    return pl.pallas_call(
        body, name="fox_gate_bwd",
        out_shape=[jax.ShapeDtypeStruct(f_rows.shape, F32), jax.ShapeDtypeStruct((f_rows.shape[0], 1), F32)],
    )(dcum, f_rows, b_rows)


def _dil_tables(seq):
    t = min(ATT_TILE, seq)
    n = seq // t
    a = np.arange(t)
    d = (np.arange(n)[:, None, None] * t + a[None, :, None] - a[None, None, :]).astype(np.int64)
    count = np.zeros(d.shape, np.int64)
    for window, dil in DILATED_PATTERNS:
        count += (d >= 0) & (d % dil == 0) & (d // dil <= window // dil)
    nn = np.maximum(d, 0)
    max_exact = NUM_BUCKETS // 2
    nf = np.maximum(nn, 1).astype(np.float32)
    large = max_exact + (np.log(nf / np.float32(max_exact)) / np.float32(math.log(MAX_DISTANCE / max_exact))
                         * np.float32(NUM_BUCKETS - max_exact)).astype(np.int32)
    bucket = np.where(nn < max_exact, nn, np.minimum(large, NUM_BUCKETS - 1))
    bucket = np.where(count > 0, bucket, -1).astype(np.int32)
    logc = np.where(count > 0, np.log(np.maximum(count, 1)), MASKED).astype(np.float32)
    return bucket, logc


def _dil_bias(rel_bias, seq, comm=None):
    bucket, logc = _dil_tables(seq)
    n, t, _ = bucket.shape

    def body(rb_ref, bk_ref, lc_ref, o_ref):
        h = pl.program_id(0) * 2 + pl.program_id(1)
        bk = bk_ref[...]
        out = lc_ref[...]
        for b in range(NUM_BUCKETS):
            out = jnp.where(bk == b, out + rb_ref[b, h], out)
        o_ref[...] = out

    full = pl.BlockSpec((n, t, t), lambda p, h: (0, 0, 0))
    out, carried = _pallas(
        body, name="dil_bias", grid=(2, 2),
        in_specs=[pl.BlockSpec(memory_space=pltpu.SMEM), full, full],
        out_specs=[pl.BlockSpec((None, None, n, t, t), lambda p, h: (p, h, 0, 0, 0))],
        out_shape=[jax.ShapeDtypeStruct((2, 2, n, t, t), F32)],
        args=[rel_bias, jnp.asarray(bucket), jnp.asarray(logc)], sem=("parallel", "parallel"), comm=comm)
    return out[0] if comm is None else (out[0], carried)


def _dil_bias_bwd(dbias, seq):
    bucket, _ = _dil_tables(seq)
    n, t, _ = bucket.shape

    def body(d_ref, bk_ref, o_ref):
        bk = bk_ref[...]
        lane = lax.broadcasted_iota(jnp.int32, (1, LANES), 1)
        for b in range(NUM_BUCKETS):
            rowv = jnp.zeros((1, LANES), F32)
            for h in range(N_HEADS):
                s = jnp.sum(jnp.where(bk == b, d_ref[h // 2, h % 2], 0.0))
                rowv = jnp.where(lane == h, s, rowv)
            o_ref[b:b + 1, :] = rowv

    out = pl.pallas_call(body, name="dil_bias_bwd", out_shape=jax.ShapeDtypeStruct((NUM_BUCKETS, LANES), F32),
                         compiler_params=pltpu.CompilerParams(vmem_limit_bytes=VMEM_LIMIT))(dbias, jnp.asarray(bucket))
    return out[:, :N_HEADS]


def _shift_rows(x, k, row, fill=0.0):
    n = x.shape[0]
    if k > 0:
        return jnp.where(row >= k, pltpu.roll(x, k, 0), fill)
    return jnp.where(row < n + k, pltpu.roll(x, n + k, 0), fill)


def _row_scan(a, u, row, reverse=False):
    n = a.shape[0]
    k = 1
    while k < n:
        s = -k if reverse else k
        u = a * _shift_rows(u, s, row) + u
        a = a * _shift_rows(a, s, row, 1.0)
        k *= 2
    return u


def _sigmoid(x):
    return 1.0 / (1.0 + jnp.exp(-x))


def _gelu(g):
    return 0.5 * g * (1.0 + lax.erf(g * (2.0 ** -0.5)))


def _gelu_grad(g):
    return 0.5 * (1.0 + lax.erf(g * (2.0 ** -0.5))) + g * jnp.exp(-0.5 * g * g) * (1.0 / math.sqrt(2.0 * math.pi))


def _neg_expm1(x):
    small = -x * (1.0 + x * (0.5 + x * (1.0 / 6.0 + x * (1.0 / 24.0))))
    return jnp.where(x > -0.03, small, 1.0 - jnp.exp(x))


def _lru_core(x, vec, wa, wx, row):
    xs = [_shift_rows(x, 3 - j, row) if j < 3 else x for j in range(4)]
    xc = vec[4:5, :]
    for j in range(4):
        xc = xc + vec[j:j + 1, :] * xs[j]
    xcb = xc.astype(BF16)
    r = _sigmoid(jnp.dot(xcb, wa, preferred_element_type=F32) + vec[5:6, :])
    ig = _sigmoid(jnp.dot(xcb, wx, preferred_element_type=F32) + vec[6:7, :])
    lam = vec[7:8, :]
    sp = jnp.maximum(-lam, 0.0) - _log_sigmoid(jnp.abs(lam))
    la = -LRU_C * r * sp
    a = jnp.exp(la)
    mult = jnp.sqrt(_neg_expm1(2.0 * la))
    return xs, xc, xcb, r, ig, sp, la, a, mult


def _lru_specs(seq):
    xg = pl.BlockSpec((seq, LRU_W), lambda hf, b: (b, COL_LRU // LRU_W + hf))
    mix = pl.BlockSpec((seq, LANES), lambda hf, b: (b, 2 * MIX_LRU + hf))
    vec = pl.BlockSpec((SUBLANES, LANES), lambda hf, b: (0, hf))
    mat = pl.BlockSpec((None, LANES, LANES), lambda hf, b: (hf, 0, 0))
    return xg, mix, vec, mat


def _lru_fwd(proj, vec, wa, wx, out_buf, *, nb, name):
    seq = proj.shape[0] // nb

    def body(xg_ref, vec_ref, wa_ref, wx_ref, _, o_ref):
        row = lax.broadcasted_iota(jnp.int32, (seq, LANES), 0)
        _, xc, _, _, ig, _, _, a, mult = _lru_core(xg_ref[:, 0:LANES], vec_ref[...], wa_ref[...], wx_ref[...], row)
        h = _row_scan(a, mult * (ig * xc), row)
        o_ref[...] = h * _gelu(xg_ref[:, LANES:LRU_W])

    xg, mix, vecs, mat = _lru_specs(seq)
    return pl.pallas_call(
        body, name=name, grid=(2, nb), in_specs=[xg, vecs, mat, mat, ANY], out_specs=mix,
        out_shape=jax.ShapeDtypeStruct(out_buf.shape, F32), input_output_aliases={4: 0},
        compiler_params=_params(("parallel", "arbitrary")))(proj, vec, wa, wx, out_buf)


def _lru_bwd(proj, vec, wa, wx, dout, dbuf, *, nb, name):
    seq = proj.shape[0] // nb

    def body(xg_ref, vec_ref, wa_ref, wx_ref, do_ref, _, d_ref, dvec_ref, dwa_ref, dwx_ref):
        row = lax.broadcasted_iota(jnp.int32, (seq, LANES), 0)
        vec_, wa_, wx_ = vec_ref[...], wa_ref[...], wx_ref[...]
        xs, xc, xcb, r, ig, sp, la, a, mult = _lru_core(xg_ref[:, 0:LANES], vec_, wa_, wx_, row)
        h = _row_scan(a, mult * (ig * xc), row)
        gate, do = xg_ref[:, LANES:LRU_W], do_ref[...]
        d_ref[:, LANES:LRU_W] = do * h * _gelu_grad(gate)
        dh = do * _gelu(gate)
        gacc = _row_scan(_shift_rows(a, -1, row), dh, row, reverse=True)
        da = gacc * _shift_rows(h, 1, row)
        dmult = gacc * (ig * xc)
        dig = gacc * (mult * xc)
        dxc = gacc * (mult * ig)
        dla = da * a - dmult * (a * a) / mult
        dr = (-LRU_C) * sp * dla
        dsp = jnp.sum((-LRU_C) * r * dla, axis=0, keepdims=True)
        dpr = dr * r * (1.0 - r)
        dpi = dig * ig * (1.0 - ig)
        dprb, dpib = dpr.astype(BF16), dpi.astype(BF16)
        dxc = (dxc + lax.dot_general(dprb, wa_, NT_DIMS, preferred_element_type=F32)
               + lax.dot_general(dpib, wx_, NT_DIMS, preferred_element_type=F32))
        dx = vec_[3:4, :] * dxc
        for j in range(3):
            dx = dx + vec_[j:j + 1, :] * _shift_rows(dxc, -(3 - j), row)
        d_ref[:, 0:LANES] = dx

        @pl.when(pl.program_id(1) == 0)
        def _():
            dvec_ref[...] = jnp.zeros_like(dvec_ref)
            dwa_ref[...] = jnp.zeros_like(dwa_ref)
            dwx_ref[...] = jnp.zeros_like(dwx_ref)

        for j in range(4):
            dvec_ref[j:j + 1, :] += jnp.sum(dxc * xs[j], axis=0, keepdims=True)
        dvec_ref[4:5, :] += jnp.sum(dxc, axis=0, keepdims=True)
        dvec_ref[5:6, :] += jnp.sum(dpr, axis=0, keepdims=True)
        dvec_ref[6:7, :] += jnp.sum(dpi, axis=0, keepdims=True)
        lam = vec_[7:8, :]
        dvec_ref[7:8, :] += -dsp * _sigmoid(-lam)
        dwa_ref[...] += lax.dot_general(xcb, dprb, TN_DIMS, preferred_element_type=F32)
        dwx_ref[...] += lax.dot_general(xcb, dpib, TN_DIMS, preferred_element_type=F32)

    xg, mix, vecs, mat = _lru_specs(seq)
    return pl.pallas_call(
        body, name=name, grid=(2, nb), in_specs=[xg, vecs, mat, mat, mix, ANY], out_specs=[xg, vecs, mat, mat],
        out_shape=[jax.ShapeDtypeStruct(dbuf.shape, F32), jax.ShapeDtypeStruct((SUBLANES, 2 * LANES), F32),
                   jax.ShapeDtypeStruct((2, LANES, LANES), F32), jax.ShapeDtypeStruct((2, LANES, LANES), F32)],
        input_output_aliases={5: 0},
        compiler_params=_params(("parallel", "arbitrary")))(proj, vec, wa, wx, dout, dbuf)


FFN_ROWS = 256
FFN_COLS = 1408


def _with_halo(halo, x, k):
    xx = jnp.concatenate([halo, x], axis=0)
    return pltpu.roll(xx, k, 0)[SUBLANES:, :]


def _ffn_conv(x_ref, halo_ref, cw, pos):
    x, halo = x_ref[...], halo_ref[...]
    x1 = jnp.where(pos >= 1, _with_halo(halo, x, 1), 0.0)
    x2 = jnp.where(pos >= 2, _with_halo(halo, x, 2), 0.0)
    return cw[3:4, :] + cw[0:1, :] * x2 + cw[1:2, :] * x1 + cw[2:3, :] * x, x1, x2


def _ffn_specs(tm, tn, gate_off):
    prev = lambda i: jnp.maximum(i * (tm // SUBLANES) - 1, 0)
    up = pl.BlockSpec((tm, tn), lambda j, i: (i, j))
    gate = pl.BlockSpec((tm, tn), lambda j, i: (i, j + gate_off))
    up_h = pl.BlockSpec((SUBLANES, tn), lambda j, i: (prev(i), j))
    gate_h = pl.BlockSpec((SUBLANES, tn), lambda j, i: (prev(i), j + gate_off))
    cw_up = pl.BlockSpec((SUBLANES, tn), lambda j, i: (0, j))
    cw_gate = pl.BlockSpec((SUBLANES, tn), lambda j, i: (0, j + gate_off))
    return up, gate, up_h, gate_h, cw_up, cw_gate


def _ffn_act(hf, cw, *, seq, name):
    t, w2 = hf.shape
    w = w2 // 2
    tm, tn = _tile(seq, FFN_ROWS, SUBLANES), _tile(w, FFN_COLS)

    def body(u_ref, g_ref, uh_ref, gh_ref, cu_ref, cg_ref, o_ref):
        pos = (pl.program_id(1) * tm + lax.broadcasted_iota(jnp.int32, (tm, 1), 0)) % seq
        up, _, _ = _ffn_conv(u_ref, uh_ref, cu_ref[...], pos)
        gate, _, _ = _ffn_conv(g_ref, gh_ref, cg_ref[...], pos)
        o_ref[...] = (_gelu(gate) * up).astype(BF16)

    specs = _ffn_specs(tm, tn, w // tn)
    return pl.pallas_call(
        body, name=name, grid=(w // tn, t // tm), in_specs=list(specs), out_specs=specs[0],
        out_shape=jax.ShapeDtypeStruct((t, w), BF16),
        compiler_params=_params(("parallel", "parallel")))(hf, hf, hf, hf, cw, cw)


def _ffn_bwd(hf, cw, dact, *, seq, name, comm=None):
    t, w2 = hf.shape
    w = w2 // 2
    tm, tn = _tile(seq, FFN_ROWS, 2 * SUBLANES), _tile(w, FFN_COLS)
    ext = tm + SUBLANES
    last = t // SUBLANES - 1

    def body(u_ref, g_ref, uh_ref, gh_ref, cu_ref, cg_ref, un_ref, gn_ref, da_ref, dn_ref, d_ref, dcu_ref, dcg_ref):
        pos = (pl.program_id(1) * tm + lax.broadcasted_iota(jnp.int32, (ext, 1), 0)) % seq

        def conv(x_ref, prev_ref, next_ref, cwv):
            xx = jnp.concatenate([prev_ref[...], x_ref[...], next_ref[...]], axis=0)
            x1 = jnp.where(pos >= 1, pltpu.roll(xx, 1, 0)[SUBLANES:, :], 0.0)
            x2 = jnp.where(pos >= 2, pltpu.roll(xx, 2, 0)[SUBLANES:, :], 0.0)
            x0 = xx[SUBLANES:, :]
            return cwv[3:4, :] + cwv[0:1, :] * x2 + cwv[1:2, :] * x1 + cwv[2:3, :] * x0, (x2, x1, x0)

        def back(d, cwv):
            d1 = jnp.where(pos < seq - 1, pltpu.roll(d, ext - 1, 0), 0.0)
            d2 = jnp.where(pos < seq - 2, pltpu.roll(d, ext - 2, 0), 0.0)
            return (cwv[2:3, :] * d + cwv[1:2, :] * d1 + cwv[0:1, :] * d2)[:tm, :].astype(BF16)

        cu, cg = cu_ref[...], cg_ref[...]
        up, u_taps = conv(u_ref, uh_ref, un_ref, cu)
        gate, g_taps = conv(g_ref, gh_ref, gn_ref, cg)
        da = jnp.concatenate([da_ref[...], dn_ref[...]], axis=0)
        cdf = 0.5 * (1.0 + lax.erf(gate * (2.0 ** -0.5)))
        d_up = da * (gate * cdf)
        d_gate = da * up * (cdf + gate * jnp.exp(-0.5 * gate * gate) * (1.0 / math.sqrt(2.0 * math.pi)))
        d_ref[0] = back(d_up, cu)
        d_ref[1] = back(d_gate, cg)

        @pl.when(pl.program_id(1) == 0)
        def _():
            dcu_ref[...] = jnp.zeros_like(dcu_ref)
            dcg_ref[...] = jnp.zeros_like(dcg_ref)

        for ref, d, taps in ((dcu_ref, d_up, u_taps), (dcg_ref, d_gate, g_taps)):
            own = d[:tm, :]
            for j in range(3):
                ref[j:j + 1, :] += jnp.sum(own * taps[j][:tm, :], axis=0, keepdims=True)
            ref[3:4, :] += jnp.sum(own, axis=0, keepdims=True)

    gate_off = w // tn
    specs = _ffn_specs(tm, tn, gate_off)
    tile, cwt = specs[0], specs[4]
    nxt = lambda i: jnp.minimum((i + 1) * (tm // SUBLANES), last)
    up_n = pl.BlockSpec((SUBLANES, tn), lambda j, i: (nxt(i), j))
    gate_n = pl.BlockSpec((SUBLANES, tn), lambda j, i: (nxt(i), j + gate_off))
    out, carried = _pallas(
        body, name=name, grid=(w // tn, t // tm), in_specs=list(specs) + [up_n, gate_n, tile, up_n],
        out_specs=[pl.BlockSpec((2, tm, tn), lambda j, i: (0, i, j)), cwt, cwt],
        out_shape=[jax.ShapeDtypeStruct((2, t, w), BF16), jax.ShapeDtypeStruct((SUBLANES, w), F32),
                   jax.ShapeDtypeStruct((SUBLANES, w), F32)],
        args=[hf, hf, hf, hf, cw, cw, hf, hf, dact, dact], sem=("parallel", "arbitrary"), comm=comm)
    return out if comm is None else (out, carried)


def _adamw(w, g, m, v, *, name, rows=256, lead=None):
    nl, r, c = w.shape
    tr = _tile(r, rows, SUBLANES)

    def body(w_ref, g_ref, m_ref, v_ref, d_ref, nm_ref, nv_ref):
        g_ = g_ref[...]
        nm = ADAM_B1 * m_ref[...] + (1.0 - ADAM_B1) * g_
        nv = ADAM_B2 * v_ref[...] + (1.0 - ADAM_B2) * (g_ * g_)
        m_hat = nm / (1.0 - ADAM_B1 ** ADAM_STEP)
        v_hat = nv / (1.0 - ADAM_B2 ** ADAM_STEP)
        d_ref[...] = -ADAM_LR * (m_hat / (jnp.sqrt(v_hat) + ADAM_EPS) + ADAM_WD * w_ref[...])
        nm_ref[...] = nm
        nv_ref[...] = nv

    shape = jax.ShapeDtypeStruct((nl, r, c), F32)
    if lead is None:
        spec, grid = pl.BlockSpec((None, tr, c), lambda l, i: (l, i, 0)), (nl, r // tr)
    else:
        spec, grid = pl.BlockSpec((lead, r, c), lambda i: (i, 0, 0)), (nl // lead,)
    return pl.pallas_call(body, name=name, grid=grid, in_specs=[spec] * 4, out_specs=[spec] * 3,
                          out_shape=[shape] * 3, compiler_params=_params(("parallel",) * len(grid)))(w, g, m, v)


def _mesh_pos():
    return lax.axis_index("x"), lax.axis_index("y"), lax.axis_index("c")


def _peers(x, y):
    chips = [(1 - x, y), (x, 1 - y), (1 - x, 1 - y)]
    return [(px, py, 2 * px + py) for px, py in chips]


def _remote(src, dst, send_sems, recv_sems, idx, to):
    return pltpu.make_async_remote_copy(src, dst, send_sems.at[idx], recv_sems.at[idx], device_id=to,
                                        device_id_type=MESH)


class _Comm:
    def __init__(self, operands, out_shape, aliases, sems, copies):
        self.operands, self.out_shape, self.aliases, self.sems, self.copies = operands, out_shape, aliases, sems, copies

    def start(self, ins, outs, sems):
        for send, _ in self.copies(ins, outs, sems):
            send.start()

    def wait(self, ins, outs, sems):
        pairs = self.copies(ins, outs, sems)
        for _, recv in pairs:
            recv.wait_recv()
        for send, _ in pairs:
            send.wait_send()


def _pallas(body, *, name, grid, in_specs, out_specs, out_shape, args, aliases=None, scratch=(), sem, comm=None):
    n_in, n_out = len(in_specs), len(out_specs)
    aliases = dict(aliases or {})
    if comm is None:
        out = pl.pallas_call(body, name=name, grid=grid, in_specs=in_specs, out_specs=out_specs, out_shape=out_shape,
                             input_output_aliases=aliases, scratch_shapes=list(scratch),
                             compiler_params=_params(sem))(*args)
        return list(out), []
    nci, nco, ncs = len(comm.operands), len(comm.out_shape), len(comm.sems)

    def carried(*refs):
        ins, cin = refs[:n_in], refs[n_in:n_in + nci]
        o0 = n_in + nci
        outs, cout = refs[o0:o0 + n_out], refs[o0 + n_out:o0 + n_out + nco]
        s0 = o0 + n_out + nco
        own, csem = refs[s0:len(refs) - ncs], refs[len(refs) - ncs:]
        ids = [pl.program_id(ax) for ax in range(len(grid))]
        first, last = ids[0] == 0, ids[0] == grid[0] - 1
        for i, g in zip(ids[1:], grid[1:]):
            first, last = jnp.logical_and(first, i == 0), jnp.logical_and(last, i == g - 1)

        @pl.when(first)
        def _():
            comm.start(cin, cout, csem)

        body(*ins, *outs, *own)

        @pl.when(last)
        def _():
            comm.wait(cin, cout, csem)

    aliases.update({n_in + i: n_out + j for i, j in comm.aliases.items()})
    out = pl.pallas_call(
        carried, name=name, grid=grid, in_specs=list(in_specs) + [ANY] * nci, out_specs=list(out_specs) + [ANY] * nco,
        out_shape=list(out_shape) + list(comm.out_shape), input_output_aliases=aliases,
        scratch_shapes=list(scratch) + list(comm.sems),
        compiler_params=_params(("arbitrary",) * len(grid)))(*args, *comm.operands)
    return list(out[:n_out]), list(out[n_out:])


def _run_comm(comm, *, name):
    nci, nco = len(comm.operands), len(comm.out_shape)

    def body(*refs):
        ins, outs, sems = refs[:nci], refs[nci:nci + nco], refs[nci + nco:]
        comm.start(ins, outs, sems)
        comm.wait(ins, outs, sems)

    return pl.pallas_call(body, name=name, in_specs=[ANY] * nci, out_specs=[ANY] * nco, out_shape=list(comm.out_shape),
                          input_output_aliases=dict(comm.aliases), scratch_shapes=list(comm.sems))(*comm.operands)


def _pair_sems(*shape):
    return [pltpu.SemaphoreType.DMA(shape), pltpu.SemaphoreType.DMA(shape)]


def _gather_comm(bufs, layer, stage):
    n = len(bufs)

    def copies(ins, outs, sems):
        x, y, c = _mesh_pos()
        me = 2 * x + y
        pairs = []
        for i in range(n):
            h = bufs[i].shape[2] // 2
            mine, other = pl.ds(c * h, h), pl.ds((1 - c) * h, h)
            for r, (px, py, k) in enumerate(_peers(x, y)):
                if stage == 0:
                    send = _remote(ins[i].at[layer, me, mine, :], outs[i].at[layer, me, mine, :], *sems, (i, r), (px, py, c))
                    land = outs[i].at[layer, k, mine, :]
                    recv = _remote(land, land, *sems, (i, r), (px, py, c))
                else:
                    send = _remote(ins[i].at[layer, k, mine, :], outs[i].at[layer, k, mine, :], *sems, (i, r), (x, y, 1 - c))
                    land = outs[i].at[layer, k, other, :]
                    recv = _remote(land, land, *sems, (i, r), (x, y, 1 - c))
                pairs.append((send, recv))
        return pairs

    return _Comm(bufs, [jax.ShapeDtypeStruct(b.shape, b.dtype) for b in bufs], {i: i for i in range(n)},
                 _pair_sems(n, 3), copies)


def _reduce_sibling_comm(gs):
    n = len(gs)

    def copies(ins, outs, sems):
        x, y, c = _mesh_pos()
        pairs = []
        for i in range(n):
            h = gs[i].shape[1] // 2
            cp = _remote(ins[i].at[:, pl.ds((1 - c) * h, h), :], outs[i], *sems, i, (x, y, 1 - c))
            pairs.append((cp, cp))
        return pairs

    return _Comm(gs, [jax.ShapeDtypeStruct((g.shape[0], g.shape[1] // 2, g.shape[2]), g.dtype) for g in gs], {},
                 _pair_sems(n), copies)


def _reduce_chips_comm(ps):
    n = len(ps)

    def copies(ins, outs, sems):
        x, y, c = _mesh_pos()
        pairs = []
        for i in range(n):
            for r, (px, py, k) in enumerate(_peers(x, y)):
                cp = _remote(ins[i].at[k], outs[i].at[r], *sems, (i, r), (px, py, c))
                pairs.append((cp, cp))
        return pairs

    return _Comm(ps, [jax.ShapeDtypeStruct((3,) + p.shape[1:], p.dtype) for p in ps], {}, _pair_sems(n, 3), copies)


def _share_halves(bufs, *, name):
    n = len(bufs)

    def body(*refs):
        ins, outs = refs[:n], refs[n:2 * n]
        send_sems, recv_sems = refs[2 * n:]
        x, y, c = _mesh_pos()
        cps = []
        for i in range(n):
            h = bufs[i].shape[1] // 2
            mine = pl.ds(c * h, h)
            cp = _remote(ins[i].at[:, mine, :], outs[i].at[:, mine, :], send_sems, recv_sems, i, (x, y, 1 - c))
            cp.start()
            cps.append(cp)
        for cp in cps:
            cp.wait()

    return pl.pallas_call(
        body, name=name, in_specs=[ANY] * n, out_specs=[ANY] * n,
        out_shape=[jax.ShapeDtypeStruct(b.shape, b.dtype) for b in bufs],
        input_output_aliases={i: i for i in range(n)},
        scratch_shapes=[pltpu.SemaphoreType.DMA((n,)), pltpu.SemaphoreType.DMA((n,))])(*bufs)


def _add_own_half(full, recv, pos, *, name, rows=256):
    k4, h, n = recv.shape
    tr = _tile(h, rows, 16)
    nblk = h // tr

    def body(pos_ref, a_ref, b_ref, o_ref):
        o_ref[...] = (a_ref[...] + b_ref[...]).astype(BF16)

    grid_spec = pltpu.PrefetchScalarGridSpec(
        num_scalar_prefetch=1, grid=(k4, nblk),
        in_specs=[pl.BlockSpec((None, tr, n), lambda k, i, pos_ref: (k, pos_ref[1] * nblk + i, 0)),
                  pl.BlockSpec((None, tr, n), lambda k, i, pos_ref: (k, i, 0))],
        out_specs=pl.BlockSpec((None, tr, n), lambda k, i, pos_ref: (k, i, 0)))
    return pl.pallas_call(body, name=name, grid_spec=grid_spec, out_shape=jax.ShapeDtypeStruct(recv.shape, BF16),
                          compiler_params=_params(("parallel", "parallel")))(pos, full, recv)


def _sum_into(own, others, buf, pos, layer, *, name, rows=256):
    _, h, n = own.shape
    tr = _tile(h, rows, 16)
    nblk = h // tr

    def body(pos_ref, own_ref, oth_ref, _, o_ref):
        acc = own_ref[...].astype(F32)
        for r in range(3):
            acc = acc + oth_ref[r].astype(F32)
        o_ref[...] = acc

    grid_spec = pltpu.PrefetchScalarGridSpec(
        num_scalar_prefetch=1, grid=(nblk,),
        in_specs=[pl.BlockSpec((None, tr, n), lambda i, pos_ref: (pos_ref[0], i, 0)),
                  pl.BlockSpec((3, tr, n), lambda i, pos_ref: (0, i, 0)), ANY],
        out_specs=pl.BlockSpec((None, tr, n), lambda i, pos_ref: (layer, pos_ref[1] * nblk + i, 0)))
    return pl.pallas_call(body, name=name, grid_spec=grid_spec, out_shape=jax.ShapeDtypeStruct(buf.shape, F32),
                          input_output_aliases={3: 0}, compiler_params=_params(("parallel",)))(pos, own, others, buf)


def _pair_comm(buf):
    def copies(ins, outs, sems):
        x, y, c = _mesh_pos()
        land = outs[0].at[1 - c]
        return [(_remote(ins[0].at[c], outs[0].at[c], *sems, 0, (x, y, 1 - c)),
                 _remote(land, land, *sems, 0, (x, y, 1 - c)))]

    return _Comm([buf], [jax.ShapeDtypeStruct(buf.shape, buf.dtype)], {0: 0}, _pair_sems(1), copies)


def _quad_comm(buf):
    def copies(ins, outs, sems):
        x, y, c = _mesh_pos()
        me = 2 * x + y
        pairs = []
        for r, (px, py, k) in enumerate(_peers(x, y)):
            land = outs[0].at[k]
            pairs.append((_remote(ins[0].at[me], outs[0].at[me], *sems, r, (px, py, c)),
                          _remote(land, land, *sems, r, (px, py, c))))
        return pairs

    return _Comm([buf], [jax.ShapeDtypeStruct(buf.shape, buf.dtype)], {0: 0}, _pair_sems(3), copies)


def _chip_bcast(buf, *, name):
    def body(src_ref, out_ref, send_sems, recv_sems, local_sem):
        x, y, c = _mesh_pos()
        me = 2 * x + y
        local = pltpu.make_async_copy(src_ref, out_ref.at[me], local_sem)
        local.start()
        sends = []
        for r, (px, py, _) in enumerate(_peers(x, y)):
            cp = _remote(src_ref, out_ref.at[me], send_sems, recv_sems, r, (px, py, c))
            cp.start()
            sends.append(cp)
        for r, (px, py, k) in enumerate(_peers(x, y)):
            _remote(src_ref, out_ref.at[k], send_sems, recv_sems, r, (px, py, c)).wait_recv()
        for cp in sends:
            cp.wait_send()
        local.wait()

    return pl.pallas_call(
        body, name=name, in_specs=[ANY], out_specs=ANY, out_shape=jax.ShapeDtypeStruct((4,) + buf.shape, buf.dtype),
        scratch_shapes=[pltpu.SemaphoreType.DMA((3,)), pltpu.SemaphoreType.DMA((3,)), pltpu.SemaphoreType.DMA])(buf)


def _sum_slots(buf, *, name, rows=384):
    r, n = buf.shape[-2:]
    k = int(np.prod(buf.shape[:-2]))
    tr = _tile(r, rows, SUBLANES)

    def body(b_ref, o_ref):
        acc = b_ref[0]
        for s in range(1, k):
            acc = acc + b_ref[s]
        o_ref[...] = acc

    return pl.pallas_call(
        body, name=name, grid=(r // tr,), in_specs=[pl.BlockSpec((k, tr, n), lambda i: (0, i, 0))],
        out_specs=pl.BlockSpec((tr, n), lambda i: (i, 0)), out_shape=jax.ShapeDtypeStruct((r, n), F32),
        compiler_params=_params(("parallel",)))(buf.reshape((k, r, n)))


ROW = 1024
BIG = (("w_in", 2), ("w_out", 1), ("w_cq", 1), ("w_ck", 1), ("w_cv", 1), ("w_co", 2), ("w_up", 2), ("w_down", 1))
CONV = ("lru_conv_w", "ffn_conv_w")
REPLICATED = ("norm_mix_g", "b_forget", "lru_conv_b", "lru_w_a", "lru_b_a", "lru_w_x", "lru_b_x", "lru_lambda",
              "norm_cross_g", "norm_mem_g", "norm_ffn_g", "ffn_conv_b", "rel_bias", "final_norm_g")
WEIGHTS = ('norm_mix_g', 'w_in', 'b_forget', 'lru_conv_w', 'lru_conv_b', 'lru_w_a', 'lru_b_a', 'lru_w_x', 'lru_b_x',
           'lru_lambda', 'w_out', 'norm_cross_g', 'norm_mem_g', 'w_cq', 'w_ck', 'w_cv', 'w_co', 'norm_ffn_g', 'w_up',
           'ffn_conv_w', 'ffn_conv_b', 'w_down', 'rel_bias', 'final_norm_g')
INPUTS = ("x", "mem") + WEIGHTS + ("loss_target",) + tuple("m_" + n for n in WEIGHTS) + tuple("v_" + n for n in WEIGHTS)


def _round_up(n, m):
    return -(-n // m) * m


class _Packing:
    def __init__(self, entries):
        self.entries, self.off = entries, {}
        o = 0
        for name, shape in entries:
            self.off[name] = o
            o += _round_up(int(np.prod(shape)), ROW)
        self.used = o
        self.rows = _round_up(o // ROW, SUBLANES)

    def pack(self, arrays):
        parts = []
        for name, shape in self.entries:
            n = int(np.prod(shape))
            parts.append(jnp.pad(arrays[name].reshape(n), (0, _round_up(n, ROW) - n)))
        tail = self.rows * ROW - self.used
        if tail:
            parts.append(jnp.zeros((tail,), F32))
        return jnp.concatenate(parts).reshape(self.rows, ROW)

    def unpack(self, flat, lead=()):
        out = {}
        for name, shape in self.entries:
            n = int(np.prod(shape))
            r0, nr = self.off[name] // ROW, _round_up(n, ROW) // ROW
            rows = lax.slice_in_dim(flat, r0, r0 + nr, axis=len(lead)).reshape(lead + (nr * ROW,))
            out[name] = lax.slice_in_dim(rows, 0, n, axis=len(lead)).reshape(lead + tuple(shape))
        return out


def _to_shards(g, axis):
    r, c = g.shape
    if axis == 1:
        return g.reshape(4, r // 4, c)
    return g.reshape(r, 4, c // 4).transpose(1, 0, 2)


def _from_shards(s, axis):
    _, r, c = s.shape
    if axis == 1:
        return s.reshape(4 * r, c)
    return s.transpose(1, 0, 2).reshape(r, 4 * c)


def _proj_blocks():
    blocks = []
    for mixer in (MIX_SB, MIX_FOX, MIX_DIL):
        for p in range(2):
            blocks += [ORIG_COL[mixer] + part * 2 * LANES + p * LANES for part in range(3)]
    for hf in range(2):
        blocks += [ORIG_LRU_X + hf * LANES, ORIG_LRU_G + hf * LANES]
    return blocks


def _pad_w_in(w):
    parts = [w[..., s:s + LANES] for s in _proj_blocks()]
    parts += [w[..., 1536:1540], jnp.zeros(w.shape[:-1] + (PROJ_W - COL_F - N_HEADS,), w.dtype)]
    return jnp.concatenate(parts, axis=-1)


def _unpad_w_in(wp):
    blocks = _proj_blocks()
    order = sorted(range(len(blocks)), key=lambda i: blocks[i])
    parts = []
    for i in order:
        if blocks[i] == ORIG_COL[MIX_DIL]:
            parts.append(wp[..., COL_F:COL_F + N_HEADS])
        parts.append(wp[..., i * LANES:(i + 1) * LANES])
    return jnp.concatenate(parts, axis=-1)


def _block_diag(w):
    z = jnp.zeros((HEAD_DIM, HEAD_DIM), w.dtype)
    half = lambda a, b: jnp.concatenate([jnp.concatenate([a, z], 1), jnp.concatenate([z, b], 1)], 0)
    return jnp.stack([half(w[0], w[1]), half(w[2], w[3])])


def _block_diag_grad(d):
    return jnp.stack([d[0, :HEAD_DIM, :HEAD_DIM], d[0, HEAD_DIM:, HEAD_DIM:],
                      d[1, :HEAD_DIM, :HEAD_DIM], d[1, HEAD_DIM:, HEAD_DIM:]])


def _fox_layouts(cum, nb, seq):
    tk = min(FOX_TILE, seq)
    col = cum.reshape(nb, 2, 2, seq).transpose(0, 1, 3, 2)
    row = cum.reshape(nb, 2, 2, seq // tk, tk).transpose(0, 1, 3, 2, 4)
    return col, row


def _layer_params(w, l, nb):
    lru_vec = jnp.concatenate([w["lru_conv_w"][l], w["lru_conv_b"][l][None], w["lru_b_a"][l][None],
                               w["lru_b_x"][l][None], w["lru_lambda"][l][None]], axis=0)
    ffn_cw = jnp.concatenate([w["ffn_conv_w"][l], w["ffn_conv_b"][l][None],
                              jnp.zeros((SUBLANES - 4, 2 * D_FF), F32)], axis=0)
    return dict(
        w_in=w["w_in_padded"][l], lru_vec=lru_vec,
        wa=_block_diag(w["lru_w_a"][l]).astype(BF16), wx=_block_diag(w["lru_w_x"][l]).astype(BF16),
        ffn_cw=ffn_cw, b_rows=jnp.tile(w["b_forget"][l], nb).reshape(nb * N_HEADS, 1))


NORM_ROWS = 512
NORM_FWD_ROWS = 1024


def _merge_comms(comms):
    if len(comms) == 1:
        return comms[0]
    operands, out_shape, aliases, sems, spans = [], [], {}, [], []
    for cm in comms:
        aliases.update({len(operands) + i: len(out_shape) + j for i, j in cm.aliases.items()})
        spans.append((len(operands), len(out_shape), len(sems)))
        operands += list(cm.operands)
        out_shape += list(cm.out_shape)
        sems += list(cm.sems)

    def copies(ins, outs, sm):
        pairs = []
        for cm, (i0, o0, s0) in zip(comms, spans):
            pairs += cm.copies(ins[i0:i0 + len(cm.operands)], outs[o0:o0 + len(cm.out_shape)], sm[s0:s0 + len(cm.sems)])
        return pairs

    return _Comm(operands, out_shape, aliases, sems, copies)


GATHER_FIRST = ("w_in",)
GATHER_MID = ("w_out", "w_cq", "w_ck", "w_cv", "w_co")
GATHER_LAST = ("w_up", "w_down")


class _WeightGather:
    def __init__(self, slots, w, depth):
        self.slots, self.w, self.depth = slots, w, depth

    def plan(self, l, key):
        nxt = l + 1 if l + 1 < self.depth else None
        early = GATHER_FIRST + GATHER_MID
        if l == 0:
            table = {"proj": [(GATHER_MID, 0, 0)],
                     "sb_fwd": [(GATHER_MID, 0, 1), (GATHER_LAST, 0, 0)],
                     "fox_fwd": [(GATHER_LAST, 0, 1)] + ([(early, nxt, 0)] if nxt else []),
                     "dil_fwd": [(early, nxt, 1), (GATHER_LAST, nxt, 0)] if nxt else [],
                     "out": [(GATHER_LAST, nxt, 1)] if nxt else []}
        else:
            everything = early + GATHER_LAST
            table = {"sb_fwd": [(everything, nxt, 0)], "fox_fwd": [(everything, nxt, 1)]} if nxt else {}
        return table.get(key, [])

    def comm(self, l, key):
        entries = self.plan(l, key)
        if not entries:
            return None
        return _merge_comms([_gather_comm([self.slots[n] for n in names], layer, stage)
                             for names, layer, stage in entries])

    def done(self, l, key, landed):
        landed = list(landed)
        for names, layer, stage in self.plan(l, key):
            for n in names:
                self.slots[n] = landed.pop(0)
            if stage == 1:
                self.take(names, layer)

    def take(self, names, layer):
        for n, axis in BIG:
            if n in names:
                self.w[n][layer] = self.slots[n][layer] if n == "w_up" else _from_shards(self.slots[n][layer], axis)
        if "w_in" in names:
            self.w["w_in_padded"][layer] = _pad_w_in(self.w["w_in"][layer])

    def first_comm(self, stage):
        return _gather_comm([self.slots[n] for n in GATHER_FIRST], 0, stage)

    def first_done(self, stage, landed):
        self.slots.update(zip(GATHER_FIRST, landed))
        if stage == 1:
            self.take(GATHER_FIRST, 0)


def _layer_fwd(x, h, mem, w, lp, l, next_g, bias, nb, gather):
    t, d = x.shape
    seq = t // nb
    tag = f"l{l}"
    sv = dict(x0=x)

    def carrying(key, fn):
        comm = gather.comm(l, key)
        res = fn(comm)
        if comm is not None:
            res, landed = res
            gather.done(l, key, landed)
        return res

    proj = carrying("proj", lambda cm: _mm(h, lp["w_in"], name=tag + "_proj", comm=cm))
    mixed, ltot = carrying("sb_fwd", lambda cm: _sb_attn_fwd(proj, lax.empty((t, d), F32), nb=nb,
                                                             name=tag + "_sb_fwd", comm=cm))
    f_rows = proj[:, COL_F:COL_F + N_HEADS].reshape(nb, seq, N_HEADS).transpose(0, 2, 1).reshape(nb * N_HEADS, seq)
    cum_col, cum_row = _fox_layouts(_fox_gate_fwd(f_rows, lp["b_rows"]), nb, seq)
    mixed, lse_fox = carrying("fox_fwd", lambda cm: _softmax_attn_fwd(
        proj, nb=nb, mode="fox", mixer=MIX_FOX, out_buf=mixed, extra=(cum_col, cum_row), name=tag + "_fox_fwd", comm=cm))
    mixed, lse_dil = carrying("dil_fwd", lambda cm: _softmax_attn_fwd(
        proj, nb=nb, mode="dil", mixer=MIX_DIL, out_buf=mixed, extra=(bias,), name=tag + "_dil_fwd", comm=cm))
    mixed = _lru_fwd(proj, lp["lru_vec"], lp["wa"], lp["wx"], mixed, nb=nb, name=tag + "_lru_fwd")
    x1, hq = carrying("out", lambda cm: _mm(mixed, w["w_out"][l], res=x, norm_g=w["norm_cross_g"][l], ti=NORM_FWD_ROWS,
                                            name=tag + "_out", comm=cm))
    memn = _rmsnorm(mem, w["norm_mem_g"][l], name=tag + "_norm_mem")
    q = _mm(hq, w["w_cq"][l], name=tag + "_cq")
    k = _mm(memn, w["w_ck"][l], name=tag + "_ck")
    v = _mm(memn, w["w_cv"][l], name=tag + "_cv")
    oc, lse_c = _softmax_attn_fwd((q, k, v), nb=nb, mode="cross", name=tag + "_cross_fwd")
    x2, hn = _mm(oc, w["w_co"][l], res=x1, norm_g=w["norm_ffn_g"][l], ti=NORM_FWD_ROWS, name=tag + "_co")
    hf = _mm(hn, w["w_up"][l], b_chunks=4, name=tag + "_up")
    act = _ffn_act(hf, lp["ffn_cw"], seq=seq, name=tag + "_ffn_act")
    if next_g is None:
        x3, h_next = _mm(act, w["w_down"][l], res=x2, name=tag + "_down"), None
    else:
        x3, h_next = _mm(act, w["w_down"][l], res=x2, norm_g=next_g, ti=NORM_FWD_ROWS, name=tag + "_down")
    sv.update(h=h, proj=proj, ltot=ltot, f_rows=f_rows, cum_col=cum_col, cum_row=cum_row, lse_fox=lse_fox,
              lse_dil=lse_dil, mixed=mixed, x1=x1, hq=hq, memn=memn, q=q, k=k, v=v, oc=oc, lse_c=lse_c, x2=x2,
              hn=hn, hf=hf, act=act)
    return x3, h_next, sv


class _PendingReduce:
    def __init__(self, full, pos, layer):
        self.names, self.full, self.pos, self.layer = list(full), list(full.values()), pos, layer

    def sibling_comm(self):
        return _reduce_sibling_comm(self.full)

    def add(self, from_sibling):
        self.partial = [_add_own_half(f, r, self.pos, name=f"l{self.layer}_reduce_add_{n}")
                        for f, r, n in zip(self.full, from_sibling, self.names)]

    def chips_comm(self):
        return _reduce_chips_comm(self.partial)

    def finish(self, others, g_shard):
        g_shard = dict(g_shard)
        for p, o, n in zip(self.partial, others, self.names):
            g_shard[n] = _sum_into(p, o, g_shard[n], self.pos, self.layer, name=f"l{self.layer}_reduce_sum_{n}")
        return g_shard


def _layer_bwd(dx3, mem, sv, w, lp, l, bias, nb, pos, pending=None, g_shard=None, reduce_early=False):
    t = dx3.shape[0]
    seq = t // nb
    tag = f"l{l}"
    g = {}
    down_rows = _tile(sv["act"].shape[1], 1408)
    if pending is None:
        g["w_down"] = _mm(sv["act"], dx3, ta=True, ti=down_rows, name=tag + "_dw_down")
    else:
        g["w_down"], from_sibling = _mm(sv["act"], dx3, ta=True, ti=down_rows, comm=pending.sibling_comm(),
                                        name=tag + "_dw_down")
        pending.add(from_sibling)
    dact = _mm(dx3, w["w_down"][l], tb=True, name=tag + "_dact")
    if pending is None:
        dhf, dcu, dcg = _ffn_bwd(sv["hf"], lp["ffn_cw"], dact, seq=seq, name=tag + "_ffn_bwd")
    else:
        (dhf, dcu, dcg), others = _ffn_bwd(sv["hf"], lp["ffn_cw"], dact, seq=seq, name=tag + "_ffn_bwd",
                                           comm=pending.chips_comm())
        g_shard = pending.finish(others, g_shard)
    dcw = jnp.concatenate([dcu, dcg], axis=1)
    g["ffn_conv_w"], g["ffn_conv_b"] = dcw[:3], dcw[3]
    g["w_up"] = _mm(sv["hn"], dhf, ta=True, halves="b", col_shards=4, name=tag + "_dw_up")
    early = _PendingReduce(_big_grad_shards(g, EARLY), pos, l) if reduce_early else None
    res = _mm(dhf, w["w_up"][l], tb=True, halves="a", b_chunks=4, norm_bwd=(sv["x2"], w["norm_ffn_g"][l], dx3), ti=NORM_ROWS,
              comm=early.sibling_comm() if early else None, name=tag + "_dhn")
    if early:
        res, from_sibling = res
        early.add(from_sibling)
    dx2, dg = res
    g["norm_ffn_g"] = dg.reshape(-1)
    g["w_co"] = _mm(sv["oc"], dx2, ta=True, col_shards=4, name=tag + "_dw_co")
    doc = _mm(dx2, w["w_co"][l], tb=True, name=tag + "_doc")
    dq, dk, dv = _softmax_attn_bwd((sv["q"], sv["k"], sv["v"]), sv["oc"], sv["lse_c"], doc, nb=nb, mode="cross",
                                   name=tag + "_cross_bwd")
    g["w_cq"] = _mm(sv["hq"], dq, ta=True, name=tag + "_dw_cq")
    g["w_ck"] = _mm(sv["memn"], dk, ta=True, name=tag + "_dw_ck")
    g["w_cv"] = _mm(sv["memn"], dv, ta=True, name=tag + "_dw_cv")
    dx1, dg = _mm(dq, w["w_cq"][l], tb=True, norm_bwd=(sv["x1"], w["norm_cross_g"][l], dx2), ti=NORM_ROWS,
                  name=tag + "_dhq")
    g["norm_cross_g"] = dg.reshape(-1)
    dmemn = _mm(dv, w["w_cv"][l], tb=True, res=_mm(dk, w["w_ck"][l], tb=True, name=tag + "_dmem_k"),
                name=tag + "_dmem_v")
    _, g["norm_mem_g"] = _rmsnorm_bwd(dmemn, mem, w["norm_mem_g"][l], None, name=tag + "_norm_mem_bwd")
    mixed, proj = sv["mixed"], sv["proj"]
    g["w_out"] = _mm(mixed, dx1, ta=True, name=tag + "_dw_out")
    mid = _PendingReduce(_big_grad_shards(g, GATHER_MID), pos, l) if reduce_early else None
    dmixed = _mm(dx1, w["w_out"][l], tb=True, name=tag + "_dmixed", comm=mid.sibling_comm() if mid else None)
    if mid:
        dmixed, from_sibling = dmixed
        mid.add(from_sibling)
    dproj = _sb_attn_bwd(proj, sv["ltot"], dmixed, lax.empty((t, PROJ_W), F32), nb=nb, name=tag + "_sb_bwd",
                         comm=early.chips_comm() if early else None)
    if early:
        dproj, others = dproj
        g_shard = early.finish(others, g_shard)
    res = _softmax_attn_bwd(
        proj, mixed, sv["lse_fox"], dmixed, nb=nb, mode="fox", mixer=MIX_FOX, dbuf=dproj,
        extra=(sv["cum_col"], sv["cum_row"]), name=tag + "_fox_bwd", comm=mid.chips_comm() if mid else None)
    if mid:
        res, others = res
        g_shard = mid.finish(others, g_shard)
    dproj, dcum_k, dcum_q = res
    dcum = (dcum_k.transpose(0, 1, 3, 2, 4).reshape(nb * N_HEADS, seq)
            + dcum_q.transpose(0, 1, 3, 2).reshape(nb * N_HEADS, seq))
    df_rows, db = _fox_gate_bwd(dcum, sv["f_rows"], lp["b_rows"])
    g["b_forget"] = db[:N_HEADS, 0]
    df = df_rows.reshape(nb, N_HEADS, seq).transpose(0, 2, 1).reshape(t, N_HEADS)
    dproj, dbias = _softmax_attn_bwd(proj, mixed, sv["lse_dil"], dmixed, nb=nb, mode="dil", mixer=MIX_DIL,
                                     dbuf=dproj, extra=(bias,), name=tag + "_dil_bwd")
    dproj, dvec, dwa, dwx = _lru_bwd(proj, lp["lru_vec"], lp["wa"], lp["wx"], dmixed, dproj, nb=nb,
                                     name=tag + "_lru_bwd")
    g["lru_conv_w"], g["lru_conv_b"], g["lru_b_a"], g["lru_b_x"], g["lru_lambda"] = (
        dvec[0:4], dvec[4], dvec[5], dvec[6], dvec[7])
    g["lru_w_a"], g["lru_w_x"] = _block_diag_grad(dwa), _block_diag_grad(dwx)
    dproj = lax.dynamic_update_slice(dproj, jnp.pad(df, ((0, 0), (0, PROJ_W - COL_F - N_HEADS))), (0, COL_F))
    g["w_in_padded"] = _mm(sv["h"], dproj, ta=True, name=tag + "_dw_in")
    dx0, dg = _mm(dproj, lp["w_in"], tb=True, norm_bwd=(sv["x0"], w["norm_mix_g"][l], dx1), ti=NORM_ROWS,
                  name=tag + "_dh")
    g["norm_mix_g"] = dg.reshape(-1)
    return dx0, g, dbias, g_shard


def _big_grad_shards(g, names):
    out = {}
    for n, axis in BIG:
        if n not in names:
            continue
        if n in ("w_up", "w_co"):
            out[n] = g[n]
        else:
            out[n] = _to_shards(_unpad_w_in(g["w_in_padded"]) if n == "w_in" else g[n], axis)
    return out


EARLY = ("w_down", "w_up")


def kernel(*args):
    a = dict(zip(INPUTS, args, strict=True))
    nb, seq, d = a["x"].shape
    depth = a["norm_mix_g"].shape[0]
    x = a["x"].reshape(nb * seq, d)
    mem = a["mem"].reshape(nb * a["mem"].shape[1], d)
    target = a["loss_target"].reshape(nb * seq, d)
    cx, cy, c = _mesh_pos()
    chip = 2 * cx + cy
    pos = jnp.stack([chip, c]).astype(jnp.int32)

    slots = {}
    for n, _ in BIG:
        own = a[n].astype(BF16)[:, None]
        slots[n] = lax.dynamic_update_slice(lax.empty((depth, 4) + own.shape[2:], BF16), own, (0, chip, 0, 0))
    w = {n: a[n] for n in REPLICATED}
    w.update({n: {} for n, _ in BIG}, w_in_padded={})
    gather = _WeightGather(slots, w, depth)
    bias, landed = _dil_bias(w["rel_bias"], seq, comm=gather.first_comm(0))
    gather.first_done(0, landed)
    h, landed = _rmsnorm(x, w["norm_mix_g"][0], name="l0_norm_mix", comm=gather.first_comm(1))
    gather.first_done(1, landed)
    cpk = _Packing([(n, a[n].shape) for n in CONV])
    conv = cpk.unpack(_chip_bcast(cpk.pack({n: a[n] for n in CONV}), name="gather_conv"), lead=(4,))
    for n in CONV:
        w[n] = jnp.moveaxis(conv[n], 0, 2).reshape(a[n].shape[:2] + (4 * a[n].shape[2],))

    lps, saved = [], []
    for l in range(depth):
        lps.append(_layer_params(w, l, nb))
        x, h, sv = _layer_fwd(x, h, mem, w, lps[l], l, w["norm_mix_g"][l + 1] if l + 1 < depth else None, bias, nb,
                              gather)
        saved.append(sv)
    loss, dx, dg_final = _loss_head(x, w["final_norm_g"], target)
    small_g = [None] * depth
    dbias, pending = None, None
    g_shard = {n: lax.empty(a[n].shape, F32) for n, _ in BIG}
    for l in reversed(range(depth)):
        bottom = l == 0
        dx, g, db, g_shard = _layer_bwd(dx, mem, saved[l], w, lps[l], l, bias, nb, pos, pending=pending,
                                        g_shard=g_shard, reduce_early=bottom)
        dbias = db if dbias is None else dbias + db
        small_g[l] = g
        left = [n for n, _ in BIG if not (bottom and n in EARLY + GATHER_MID)]
        pending = _PendingReduce(_big_grad_shards(g, left), pos, l)

    grads = {n: jnp.stack([small_g[l][n] for l in range(depth)]) for n in REPLICATED + CONV
             if n not in ("rel_bias", "final_norm_g")}
    grads["rel_bias"] = _dil_bias_bwd(dbias, seq)
    grads["final_norm_g"] = dg_final
    grads["loss"] = loss.reshape(1)
    spk = _Packing([(n, grads[n].shape) for n in REPLICATED + CONV + ("loss",)])
    s_flat = spk.pack(grads)
    pair = lax.dynamic_update_slice(lax.empty((2,) + s_flat.shape, F32), s_flat[None], (c, 0, 0))
    *from_sibling, pair = _run_comm(_merge_comms([pending.sibling_comm(), _pair_comm(pair)]), name="tail_sibling")
    pending.add(from_sibling)
    quad = lax.dynamic_update_slice(lax.empty((4,) + pair.shape, F32), pair[None], (chip, 0, 0, 0))
    *others, quad = _run_comm(_merge_comms([pending.chips_comm(), _quad_comm(quad)]), name="tail_chips")
    g_shard = pending.finish(others, g_shard)
    names = [n for n, _ in BIG]
    g_shard = dict(zip(names, _share_halves([g_shard[n] for n in names], name="reduce_share")))
    out = {}
    for n in names:
        operands = (a[n], g_shard[n], a["m_" + n], a["v_" + n])
        cols = a[n].shape[2]
        if cols % LANES:
            lead = max(b for b in range(1, 65) if cols % b == 0)
            res = _adamw(*[z.transpose(2, 0, 1) for z in operands], name="adamw_" + n, lead=lead)
            delta, new_m, new_v = [z.transpose(1, 2, 0) for z in res]
        else:
            delta, new_m, new_v = _adamw(*operands, name="adamw_" + n)
        out[n] = (g_shard[n], delta, new_m, new_v)
    total = spk.unpack(_sum_slots(quad, name="small_sum"))
    for n in CONV:
        width = a[n].shape[2]
        total[n] = lax.dynamic_slice_in_dim(total[n], chip * width, width, axis=2)
    apk = _Packing([(n, a[n].shape) for n in REPLICATED + CONV])
    s_out = _adamw(*[apk.pack(src)[None] for src in (
        {n: a[n] for n in REPLICATED + CONV}, total, {n: a["m_" + n] for n in REPLICATED + CONV},
        {n: a["v_" + n] for n in REPLICATED + CONV})], name="adamw_small")
    s_delta, s_m, s_v = [apk.unpack(o[0]) for o in s_out]
    for n in REPLICATED + CONV:
        out[n] = (total[n], s_delta[n], s_m[n], s_v[n])

    return (total["loss"].reshape(()), dx.reshape(nb, seq, d), *[out[n][0] for n in WEIGHTS],
            *[out[n][1] for n in WEIGHTS], *[out[n][2] for n in WEIGHTS], *[out[n][3] for n in WEIGHTS])
```

```python
import math

import numpy as np
import jax
import jax.numpy as jnp
from jax import lax
from jax.experimental import pallas as pl
from jax.experimental.pallas import tpu as pltpu

F32 = jnp.float32
BF16 = jnp.bfloat16

HEAD_DIM = 64
N_HEADS = 4
N_IN = 2820
D_FF = 2816
LRU_C = 8.0
EPS = 1e-6
NUM_BUCKETS = 32
MAX_DISTANCE = 2048
DILATED_PATTERNS = ((128, 1), (512, 4), (2048, 16))
ADAM_LR, ADAM_B1, ADAM_B2, ADAM_EPS, ADAM_WD, ADAM_STEP = 0.001, 0.9, 0.999, 1e-08, 0.01, 10

LANES = 128
SUBLANES = 8
VMEM_LIMIT = 48 * 1024 * 1024

PROJ_W = 3072
PAIR_W = 3 * LANES
LRU_W = 2 * LANES
COL_LRU = 6 * PAIR_W
COL_F = COL_LRU + 2 * LRU_W
MIX_SB, MIX_FOX, MIX_DIL, MIX_LRU = 0, 1, 2, 3
ORIG_COL = {MIX_SB: 0, MIX_FOX: 768, MIX_DIL: 1540}
ORIG_LRU_X, ORIG_LRU_G = 2308, 2564

ATT_TILE = 256
WIDE_TILE = 512
WIDE_BWD_VMEM = 58 * 1024 * 1024
MASKED = -1e30
SCALE = HEAD_DIM ** -0.5

NT_DIMS = (((1,), (1,)), ((), ()))
TN_DIMS = (((0,), (0,)), ((), ()))

MESH = pl.DeviceIdType.MESH
ANY = pl.BlockSpec(memory_space=pl.ANY)


def _params(sem, vmem=None):
    return pltpu.CompilerParams(dimension_semantics=sem, vmem_limit_bytes=vmem or VMEM_LIMIT)


def _tile(n, target, unit=LANES):
    if n <= target:
        return n
    t = (target // unit) * unit
    while t > unit and n % t:
        t -= unit
    assert n % t == 0, (n, target, unit)
    return t


def _mm(a, b, *, ta=False, tb=False, res=None, col_shards=1, halves=None, b_chunks=1, norm_g=None, norm_bwd=None,
        comm=None, name, ti=1024, tj=1408, tc=1408):
    if halves == "a":
        m, kc = a.shape[1], 2 * a.shape[2]
    else:
        m, kc = (a.shape[1], a.shape[0]) if ta else a.shape
    n_blk, k_blk = None, kc // 2 if halves == "a" else kc
    if halves == "b":
        n = 2 * b.shape[2]
        assert b.shape[1] == kc
        n_blk = n // 2
    elif b_chunks > 1:
        n = b.shape[1] if tb else b_chunks * b.shape[2]
        assert (b_chunks * b.shape[2] if tb else b.shape[1]) == kc
        if tb:
            k_blk = min(k_blk, kc // b_chunks)
        else:
            n_blk = n // b_chunks
    else:
        n = b.shape[0] if tb else b.shape[1]
        assert (b.shape[1] if tb else b.shape[0]) == kc
    assert n % col_shards == 0
    n_blk = min(n_blk or n, n // col_shards)
    ti, tj, tc = (_tile(m, ti, LANES if ta else SUBLANES), _tile(n_blk, tj),
                  _tile(k_blk, tc, SUBLANES if ta and tb else LANES))
    per_shard, per_half_j, per_half_k = n // col_shards // tj, n // 2 // tj, kc // 2 // tc
    per_chunk = (kc if tb else n) // b_chunks // (tc if tb else tj)
    nk = kc // tc
    dims = (((0 if ta else 1,), (1 if tb else 0,)), ((), ()))
    rows_whole = norm_g is not None or norm_bwd is not None
    assert not rows_whole or (tj == n and col_shards == 1)
    n_extra = (res is not None) + (norm_g is not None) + (3 if norm_bwd is not None else 0)
    n_out = 2 if rows_whole else 1

    def finish(val, ex, outs):
        if res is not None:
            val = ex[0][...] + val
        if norm_g is not None:
            outs[0][...] = val
            outs[1][...] = (_xhat(val) * ex[-1][...]).astype(BF16)
        elif norm_bwd is not None:
            x_ref, g_ref, r_ref = ex[-3:]
            dx, dgr = _norm_bwd_rows(val, x_ref[...], g_ref[...])
            outs[0][...] = r_ref[...] + dx

            @pl.when(pl.program_id(0) == 0)
            def _():
                outs[1][...] = jnp.zeros_like(outs[1])

            outs[1][...] += jnp.sum(dgr, axis=0, keepdims=True)
        else:
            outs[0][...] = val

    def body(*refs):
        a_ref, b_ref = refs[:2]
        ex = refs[2:2 + n_extra]
        outs = refs[2 + n_extra:2 + n_extra + n_out]
        part = lax.dot_general(a_ref[...].astype(BF16), b_ref[...].astype(BF16), dims, preferred_element_type=F32)
        if nk == 1:
            finish(part, ex, outs)
            return
        acc_ref = refs[-1]
        k = pl.program_id(2)

        @pl.when(k == 0)
        def _():
            acc_ref[...] = part

        @pl.when(k > 0)
        def _():
            acc_ref[...] += part

        @pl.when(k == nk - 1)
        def _():
            finish(acc_ref[...], ex, outs)

    if halves == "a":
        a_spec = pl.BlockSpec((None, ti, tc), lambda i, j, k: (k // per_half_k, i, k % per_half_k))
    elif ta:
        a_spec = pl.BlockSpec((tc, ti), lambda i, j, k: (k, i))
    else:
        a_spec = pl.BlockSpec((ti, tc), lambda i, j, k: (i, k))
    if halves == "b":
        b_spec = pl.BlockSpec((None, tc, tj), lambda i, j, k: (j // per_half_j, k, j % per_half_j))
    elif b_chunks > 1 and tb:
        b_spec = pl.BlockSpec((None, tj, tc), lambda i, j, k: (k // per_chunk, j, k % per_chunk))
    elif b_chunks > 1:
        b_spec = pl.BlockSpec((None, tc, tj), lambda i, j, k: (j // per_chunk, k, j % per_chunk))
    elif tb:
        b_spec = pl.BlockSpec((tj, tc), lambda i, j, k: (j, k))
    else:
        b_spec = pl.BlockSpec((tc, tj), lambda i, j, k: (k, j))
    o_spec = pl.BlockSpec((ti, tj), lambda i, j, k: (i, j))
    vec = pl.BlockSpec((1, tj), lambda i, j, k: (0, 0))
    in_specs, args = [a_spec, b_spec], [a, b]
    out_specs, out_shape = [o_spec], [jax.ShapeDtypeStruct((m, n), F32)]
    if res is not None:
        in_specs.append(o_spec)
        args.append(res)
    if norm_g is not None:
        in_specs.append(vec)
        args.append(norm_g.reshape(1, n))
        out_specs.append(o_spec)
        out_shape.append(jax.ShapeDtypeStruct((m, n), BF16))
    if norm_bwd is not None:
        x, g, dres = norm_bwd
        in_specs += [o_spec, vec, o_spec]
        args += [x, g.reshape(1, n), dres]
        out_specs.append(vec)
        out_shape.append(jax.ShapeDtypeStruct((1, n), F32))
    if col_shards > 1:
        assert n_extra == 0
        out_specs = [pl.BlockSpec((None, ti, tj), lambda i, j, k: (j // per_shard, i, j % per_shard))]
        out_shape = [jax.ShapeDtypeStruct((col_shards, m, n // col_shards), F32)]
    sem = ("arbitrary",) * 3 if norm_bwd is not None else ("parallel", "parallel", "arbitrary")
    out, carried = _pallas(body, name=name, grid=(m // ti, n // tj, nk), in_specs=in_specs, out_specs=out_specs,
                           out_shape=out_shape, args=args, scratch=[] if nk == 1 else [pltpu.VMEM((ti, tj), F32)],
                           sem=sem, comm=comm)
    out = out if rows_whole else out[0]
    return out if comm is None else (out, carried)


def _xhat(x):
    return x * lax.rsqrt(jnp.mean(x * x, axis=-1, keepdims=True) + EPS)


def _norm_bwd_rows(dy, x, g):
    rstd = lax.rsqrt(jnp.mean(x * x, axis=-1, keepdims=True) + EPS)
    xh = x * rstd
    dxh = dy * g
    dx = rstd * (dxh - xh * jnp.mean(dxh * xh, axis=-1, keepdims=True))
    return dx, dy * xh


def _rmsnorm(x, g, *, name, rows=512, comm=None):
    t, d = x.shape
    tr = _tile(t, rows, 2 * SUBLANES)

    def body(x_ref, g_ref, o_ref):
        o_ref[...] = (_xhat(x_ref[...]) * g_ref[...]).astype(BF16)

    out, carried = _pallas(
        body, name=name, grid=(t // tr,),
        in_specs=[pl.BlockSpec((tr, d), lambda i: (i, 0)), pl.BlockSpec((1, d), lambda i: (0, 0))],
        out_specs=[pl.BlockSpec((tr, d), lambda i: (i, 0))], out_shape=[jax.ShapeDtypeStruct((t, d), BF16)],
        args=[x, g.reshape(1, d)], sem=("parallel",), comm=comm)
    return out[0] if comm is None else (out[0], carried)


def _rmsnorm_bwd(dy, x, g, dres, *, name, rows=512):
    t, d = x.shape
    tr = _tile(t, rows, SUBLANES)

    def body(*refs):
        if dres is None:
            dy_ref, x_ref, g_ref, dx_ref, dg_ref = refs
        else:
            dy_ref, x_ref, g_ref, r_ref, dx_ref, dg_ref = refs
        dx, dgr = _norm_bwd_rows(dy_ref[...], x_ref[...], g_ref[...])
        dx_ref[...] = dx if dres is None else r_ref[...] + dx

        @pl.when(pl.program_id(0) == 0)
        def _():
            dg_ref[...] = jnp.zeros_like(dg_ref)

        dg_ref[...] += jnp.sum(dgr, axis=0, keepdims=True)

    row = pl.BlockSpec((tr, d), lambda i: (i, 0))
    vec = pl.BlockSpec((1, d), lambda i: (0, 0))
    in_specs = [row, row, vec] + ([] if dres is None else [row])
    args = (dy, x, g.reshape(1, d)) + (() if dres is None else (dres,))
    dx, dg = pl.pallas_call(
        body, name=name, grid=(t // tr,), in_specs=in_specs, out_specs=[row, vec],
        out_shape=[jax.ShapeDtypeStruct((t, d), F32), jax.ShapeDtypeStruct((1, d), F32)],
        compiler_params=_params(("arbitrary",)))(*args)
    return dx, dg.reshape(d)


def _loss_head(x, g, target, *, rows=512):
    t, d = x.shape
    tr = _tile(t, rows, SUBLANES)

    def body(x_ref, g_ref, t_ref, dx_ref, dg_ref, loss_ref):
        x_, g_ = x_ref[...], g_ref[...]
        err = _xhat(x_) * g_ - t_ref[...]
        dx, dgr = _norm_bwd_rows(err * (1.0 / d), x_, g_)
        dx_ref[...] = dx

        @pl.when(pl.program_id(0) == 0)
        def _():
            dg_ref[...] = jnp.zeros_like(dg_ref)
            loss_ref[...] = jnp.zeros_like(loss_ref)

        dg_ref[...] += jnp.sum(dgr, axis=0, keepdims=True)
        loss_ref[...] += 0.5 * jnp.sum(jnp.mean(err * err, axis=-1, keepdims=True), axis=0, keepdims=True)

    row = pl.BlockSpec((tr, d), lambda i: (i, 0))
    vec = pl.BlockSpec((1, d), lambda i: (0, 0))
    one = pl.BlockSpec((1, 1), lambda i: (0, 0))
    dx, dg, loss = pl.pallas_call(
        body, name="loss_head", grid=(t // tr,), in_specs=[row, vec, row], out_specs=[row, vec, one],
        out_shape=[jax.ShapeDtypeStruct((t, d), F32), jax.ShapeDtypeStruct((1, d), F32),
                   jax.ShapeDtypeStruct((1, 1), F32)],
        compiler_params=_params(("arbitrary",)))(x, g.reshape(1, d), target)
    return loss.reshape(()), dx, dg.reshape(d)


def _head_masks(shape):
    lane = lax.broadcasted_iota(jnp.int32, shape, len(shape) - 1)
    return lane < HEAD_DIM, lane >= HEAD_DIM


def _split_heads(x):
    m0, m1 = _head_masks(x.shape)
    zero = jnp.zeros_like(x)
    return jnp.where(m0, x, zero), jnp.where(m1, x, zero)


def _lane_pair(a0, a1, rows):
    m0, _ = _head_masks((rows, LANES))
    return jnp.where(m0, a0, a1)


def _qkv_readers(refs, packed):
    if packed:
        (r,) = refs
        return tuple((lambda r0, n, s=s: r[pl.ds(r0, n), s * LANES:(s + 1) * LANES]) for s in range(3))
    return tuple((lambda r0, n, ref=ref: ref[pl.ds(r0, n), :]) for ref in refs)


def _pair_spec(seq, col0, width=LANES):
    return pl.BlockSpec((seq, width), lambda p, b: (b, col0 + p))


def _fox_specs(seq, nk, tk):
    return [pl.BlockSpec((None, None, seq, 2), lambda p, b: (b, p, 0, 0)),
            pl.BlockSpec((None, None, nk, 2, tk), lambda p, b: (b, p, 0, 0, 0))]


def _softmax_attn_fwd(src, *, nb, mode, mixer=None, out_buf=None, extra=(), name, comm=None):
    packed = mode != "cross"
    n_src = 1 if packed else 3
    seq_q = (src if packed else src[0]).shape[0] // nb
    seq_k = seq_q if packed else src[1].shape[0] // nb
    tile = ATT_TILE if mode == "cross" else WIDE_TILE
    tq, tk = min(tile, seq_q), min(tile, seq_k)
    nq, nk = seq_q // tq, seq_k // tk
    n_ex = len(extra)

    def body(*refs):
        q_at, k_at, v_at = _qkv_readers(refs[:n_src], packed)
        ex = refs[n_src:n_src + n_ex]
        o_ref, lse_ref = refs[-2:]

        def q_tile(i, _):
            r0 = pl.multiple_of(i * tq, tq)
            qm = _split_heads((q_at(r0, tq) * SCALE).astype(BF16))
            if mode == "fox":
                cq = ex[0][pl.ds(r0, tq), :]
                row = r0 + lax.broadcasted_iota(jnp.int32, (tq, tk), 0)

            def k_tile(j, carry, diagonal=False):
                m, l, acc = carry
                c0 = pl.multiple_of(j * tk, tk)
                kt = k_at(c0, tk).astype(BF16)
                vm = _split_heads(v_at(c0, tk).astype(BF16))
                if mode == "fox":
                    ck = ex[1][j]
                hs = range(2)
                s = [lax.dot_general(qm[h], kt, NT_DIMS, preferred_element_type=F32) for h in hs]
                if mode == "fox":
                    s = [s[h] + cq[:, h:h + 1] - ck[h:h + 1, :] for h in hs]
                    if diagonal:
                        keep = (c0 + lax.broadcasted_iota(jnp.int32, (tq, tk), 1)) <= row
                        s = [jnp.where(keep, s[h], MASKED) for h in hs]
                elif mode == "dil":
                    s = [s[h] + ex[0][h, i - j] for h in hs]
                new_m = [jnp.maximum(m[h], jnp.max(s[h], axis=-1, keepdims=True)) for h in hs]
                p = [jnp.exp(s[h] - new_m[h]) for h in hs]
                alpha = [jnp.exp(m[h] - new_m[h]) for h in hs]
                new_l = [alpha[h] * l[h] + jnp.sum(p[h], axis=-1, keepdims=True) for h in hs]
                pv = [jnp.dot(p[h].astype(BF16), vm[h], preferred_element_type=F32) for h in hs]
                acc = acc * _lane_pair(alpha[0], alpha[1], tq) + (pv[0] + pv[1])
                return tuple(new_m), tuple(new_l), acc

            init = ((jnp.full((tq, 1), MASKED, F32),) * 2, (jnp.zeros((tq, 1), F32),) * 2,
                    jnp.zeros((tq, LANES), F32))
            if mode == "fox":
                m, l, acc = k_tile(i, lax.fori_loop(0, i, k_tile, init), True)
            else:
                m, l, acc = lax.fori_loop(0, i + 1 if packed else nk, k_tile, init)
            o_ref[pl.ds(r0, tq), :] = acc / _lane_pair(l[0], l[1], tq)
            lse_ref[pl.ds(r0, tq), :] = _lane_pair(m[0] + jnp.log(l[0]), m[1] + jnp.log(l[1]), tq)
            return 0

        lax.fori_loop(0, nq, q_tile, 0)

    lse_shape = jax.ShapeDtypeStruct((nb * seq_q, 2 * LANES), F32)
    if packed:
        in_specs, args = [_pair_spec(seq_q, 2 * mixer, PAIR_W)], [src]
        in_specs += _fox_specs(seq_q, nk, tk) if mode == "fox" else [
            pl.BlockSpec((None, 2, nq, tq, tk), lambda p, b: (p, 0, 0, 0, 0))]
        args += list(extra) + [out_buf]
        in_specs.append(ANY)
        out_specs = [_pair_spec(seq_q, 2 * mixer), _pair_spec(seq_q, 0)]
        out_shape = [jax.ShapeDtypeStruct(out_buf.shape, F32), lse_shape]
        aliases = {len(args) - 1: 0}
    else:
        in_specs = [_pair_spec(seq_q, 0), _pair_spec(seq_k, 0), _pair_spec(seq_k, 0)]
        args = list(src)
        out_specs = [_pair_spec(seq_q, 0), _pair_spec(seq_q, 0)]
        out_shape = [lse_shape, lse_shape]
        aliases = {}
    out, carried = _pallas(body, name=name, grid=(2, nb), in_specs=in_specs, out_specs=out_specs, out_shape=out_shape,
                           args=args, aliases=aliases, sem=("parallel", "arbitrary"), comm=comm)
    return out if comm is None else (out, carried)


def _softmax_attn_bwd(src, o, lse, do, *, nb, mode, mixer=None, dbuf=None, extra=(), name, comm=None):
    packed = mode != "cross"
    n_src = 1 if packed else 3
    seq_q = (src if packed else src[0]).shape[0] // nb
    seq_k = seq_q if packed else src[1].shape[0] // nb
    tile = ATT_TILE if mode == "cross" else WIDE_TILE
    tq, tk = min(tile, seq_q), min(tile, seq_k)
    nq, nk = seq_q // tq, seq_k // tk
    n_ex = len(extra)
    n_in = n_src + 3 + n_ex + (1 if packed else 0)

    def body(*refs):
        q_at, k_at, v_at = _qkv_readers(refs[:n_src], packed)
        o_ref, lse_ref, do_ref = refs[n_src:n_src + 3]
        ex = refs[n_src + 3:n_src + 3 + n_ex]
        outs = refs[n_in:]
        if packed:
            d_ref = outs[0]
            dq_w = lambda r0, val: d_ref.__setitem__((pl.ds(r0, tq), slice(0, LANES)), val)
            dk_ref = d_ref.at[:, LANES:2 * LANES]
            dv_ref = d_ref.at[:, 2 * LANES:3 * LANES]
        else:
            dq_ref, dk_ref, dv_ref = outs[:3]
            dq_w = lambda r0, val: dq_ref.__setitem__((pl.ds(r0, tq), slice(None)), val)
        dk_ref[...] = jnp.zeros((seq_k, LANES), F32)
        dv_ref[...] = jnp.zeros((seq_k, LANES), F32)
        if mode == "fox":
            dcum_ref, dcq_ref = outs[-2:]
            dcum_ref[...] = jnp.zeros_like(dcum_ref)
        if mode == "dil":
            dbias_ref = outs[-1]

            @pl.when(pl.program_id(1) == 0)
            def _():
                dbias_ref[...] = jnp.zeros_like(dbias_ref)

        def q_tile(i, _):
            r0 = pl.multiple_of(i * tq, tq)
            qm = _split_heads((q_at(r0, tq) * SCALE).astype(BF16))
            do_f = do_ref[pl.ds(r0, tq), :]
            dom = _split_heads(do_f.astype(BF16))
            dd = _split_heads(do_f * o_ref[pl.ds(r0, tq), :])
            delta = [jnp.sum(dd[h], axis=-1, keepdims=True) for h in range(2)]
            lse_t = lse_ref[pl.ds(r0, tq), :]
            lse_h = [lse_t[:, 0:1], lse_t[:, HEAD_DIM:HEAD_DIM + 1]]
            if mode == "fox":
                cq = ex[0][pl.ds(r0, tq), :]
                row = r0 + lax.broadcasted_iota(jnp.int32, (tq, tk), 0)

            def k_tile(j, carry, diagonal=False):
                dq, rs = carry
                c0 = pl.multiple_of(j * tk, tk)
                kt = k_at(c0, tk).astype(BF16)
                vt = v_at(c0, tk).astype(BF16)
                km = _split_heads(kt)
                if mode == "fox":
                    ck = ex[1][j]
                hs = range(2)
                s = [lax.dot_general(qm[h], kt, NT_DIMS, preferred_element_type=F32) for h in hs]
                dp = [lax.dot_general(dom[h], vt, NT_DIMS, preferred_element_type=F32) for h in hs]
                if mode == "fox":
                    s = [s[h] + cq[:, h:h + 1] - ck[h:h + 1, :] for h in hs]
                    if diagonal:
                        keep = (c0 + lax.broadcasted_iota(jnp.int32, (tq, tk), 1)) <= row
                        s = [jnp.where(keep, s[h], MASKED) for h in hs]
                elif mode == "dil":
                    s = [s[h] + ex[0][h, i - j] for h in hs]
                p = [jnp.exp(s[h] - lse_h[h]) for h in hs]
                ds = [p[h] * (dp[h] - delta[h]) for h in hs]
                dsb = [ds[h].astype(BF16) for h in hs]
                pb = [p[h].astype(BF16) for h in hs]
                dq = dq + (jnp.dot(dsb[0], km[0], preferred_element_type=F32)
                           + jnp.dot(dsb[1], km[1], preferred_element_type=F32))
                dk_t = (lax.dot_general(dsb[0], qm[0], TN_DIMS, preferred_element_type=F32)
                        + lax.dot_general(dsb[1], qm[1], TN_DIMS, preferred_element_type=F32))
                dv_t = (lax.dot_general(pb[0], dom[0], TN_DIMS, preferred_element_type=F32)
                        + lax.dot_general(pb[1], dom[1], TN_DIMS, preferred_element_type=F32))
                if mode == "fox":
                    for h in hs:
                        dcum_ref[j, h:h + 1, :] -= jnp.sum(ds[h], axis=0, keepdims=True)
                    rs = tuple(rs[h] + jnp.sum(ds[h], axis=-1, keepdims=True) for h in hs)
                elif mode == "dil":
                    for h in hs:
                        dbias_ref[h, i - j] += ds[h]
                dk_ref[pl.ds(c0, tk), :] += dk_t
                dv_ref[pl.ds(c0, tk), :] += dv_t
                return dq, rs

            zero = (jnp.zeros((tq, 1), F32),) * 2
            init = (jnp.zeros((tq, LANES), F32), zero)
            if mode == "fox":
                dq, rs = k_tile(i, lax.fori_loop(0, i, k_tile, init), True)
            else:
                dq, rs = lax.fori_loop(0, i + 1 if packed else nk, k_tile, init)
            dq_w(r0, dq * SCALE)
            if mode == "fox":
                dcq_ref[pl.ds(r0, tq), :] = jnp.where(lax.broadcasted_iota(jnp.int32, (tq, 2), 1) == 0, rs[0], rs[1])
            return 0

        lax.fori_loop(0, nq, q_tile, 0)

    if packed:
        in_specs = [_pair_spec(seq_q, 2 * mixer, PAIR_W), _pair_spec(seq_q, 2 * mixer), _pair_spec(seq_q, 0),
                    _pair_spec(seq_q, 2 * mixer)]
        args = [src, o, lse, do]
        out_specs = [_pair_spec(seq_q, 2 * mixer, PAIR_W)]
        out_shape = [jax.ShapeDtypeStruct(dbuf.shape, F32)]
        if mode == "fox":
            in_specs += _fox_specs(seq_q, nk, tk)
            out_specs += [_fox_specs(seq_q, nk, tk)[1], _fox_specs(seq_q, nk, tk)[0]]
            out_shape += [jax.ShapeDtypeStruct((nb, 2, nk, 2, tk), F32), jax.ShapeDtypeStruct((nb, 2, seq_q, 2), F32)]
        else:
            tiles = pl.BlockSpec((None, 2, nq, tq, tk), lambda p, b: (p, 0, 0, 0, 0))
            in_specs.append(tiles)
            out_specs.append(tiles)
            out_shape.append(jax.ShapeDtypeStruct((2, 2, nq, tq, tk), F32))
        args += list(extra) + [dbuf]
        in_specs.append(ANY)
        aliases = {len(args) - 1: 0}
    else:
        sq, sk = _pair_spec(seq_q, 0), _pair_spec(seq_k, 0)
        in_specs, args = [sq, sk, sk, sq, sq, sq], list(src) + [o, lse, do]
        out_specs = [sq, sk, sk]
        out_shape = [jax.ShapeDtypeStruct((nb * seq_q, 2 * LANES), F32)] + [
            jax.ShapeDtypeStruct((nb * seq_k, 2 * LANES), F32)] * 2
        aliases = {}
    out, carried = _pallas(body, name=name, grid=(2, nb), in_specs=in_specs, out_specs=out_specs, out_shape=out_shape,
                           args=args, aliases=aliases, sem=("parallel", "arbitrary"), comm=comm,
                           vmem=WIDE_BWD_VMEM if mode == "dil" else None)
    return out if comm is None else (out, carried)


def _log_sigmoid(z):
    return jnp.minimum(z, 0.0) - jnp.log(1.0 + jnp.exp(-jnp.abs(z)))


def _split_bf16(x):
    hi = x.astype(BF16)
    return hi, (x - hi.astype(F32)).astype(BF16)


def _tri(n, fn):
    r = lax.broadcasted_iota(jnp.int32, (n, n), 0)
    c = lax.broadcasted_iota(jnp.int32, (n, n), 1)
    return jnp.where(fn(r, c), 1.0, 0.0).astype(BF16)


def _sb_attn_fwd(proj, out_buf, *, nb, name, comm=None):
    seq = proj.shape[0] // nb
    tq = tk = min(ATT_TILE, seq)
    nq = seq // tq

    def body(qkv_ref, _, o_ref, lt_ref):
        rd = [_qkv_readers((qkv_ref.at[:, pr * PAIR_W:(pr + 1) * PAIR_W],), True) for pr in range(2)]
        after = _tri(tk, lambda r, c: r > c)
        ch = [(pr, h) for pr in range(2) for h in range(2)]

        def q_tile(i, _):
            r0 = pl.multiple_of(i * tq, tq)
            qm = [_split_heads((rd[pr][0](r0, tq) * SCALE).astype(BF16)) for pr in range(2)]
            row = r0 + lax.broadcasted_iota(jnp.int32, (tq, tk), 0)

            def k_tile(j, carry, diagonal):
                c, acc = carry
                c0 = pl.multiple_of(j * tk, tk)
                kt = [rd[pr][1](c0, tk).astype(BF16) for pr in range(2)]
                vm = [_split_heads(rd[pr][2](c0, tk).astype(BF16)) for pr in range(2)]
                if diagonal:
                    strict = (c0 + lax.broadcasted_iota(jnp.int32, (tq, tk), 1)) < row
                ns = range(len(ch))
                z = [lax.dot_general(qm[pr][h], kt[pr], NT_DIMS, preferred_element_type=F32) for pr, h in ch]
                ls = [_log_sigmoid(z[n]) for n in ns]
                lk = [ls[n] - z[n] for n in ns]
                if diagonal:
                    lk = [jnp.where(strict, lk[n], 0.0) for n in ns]
                parts = [_split_bf16(lk[n]) for n in ns]
                sfx = [jnp.dot(parts[n][0], after, preferred_element_type=F32)
                       + jnp.dot(parts[n][1], after, preferred_element_type=F32) for n in ns]
                att = [jnp.exp(ls[n] + sfx[n] + c[n]) for n in ns]
                if diagonal:
                    att = [jnp.where(strict, att[n], 0.0) for n in ns]
                acc = tuple(acc[pr] + (jnp.dot(att[2 * pr].astype(BF16), vm[pr][0], preferred_element_type=F32)
                                       + jnp.dot(att[2 * pr + 1].astype(BF16), vm[pr][1], preferred_element_type=F32))
                            for pr in range(2))
                return tuple(c[n] + jnp.sum(lk[n], axis=-1, keepdims=True) for n in ns), acc

            init = ((jnp.zeros((tq, 1), F32),) * 4, (jnp.zeros((tq, LANES), F32),) * 2)
            c, acc = lax.fori_loop(1, i + 1, lambda jj, cr: k_tile(i - jj, cr, False), k_tile(i, init, True))
            for pr in range(2):
                o_ref[pl.ds(r0, tq), pr * LANES:(pr + 1) * LANES] = acc[pr]
                lt_ref[pl.ds(r0, tq), pr * LANES:(pr + 1) * LANES] = _lane_pair(c[2 * pr], c[2 * pr + 1], tq)
            return 0

        lax.fori_loop(0, nq, q_tile, 0)

    both = lambda width, col: pl.BlockSpec((seq, 2 * width), lambda b: (b, col))
    out, carried = _pallas(
        body, name=name, grid=(nb,), in_specs=[both(PAIR_W, MIX_SB), ANY],
        out_specs=[both(LANES, MIX_SB), both(LANES, 0)],
        out_shape=[jax.ShapeDtypeStruct(out_buf.shape, F32), jax.ShapeDtypeStruct((nb * seq, 2 * LANES), F32)],
        args=[proj, out_buf], aliases={1: 0}, sem=("arbitrary",), comm=comm)
    return out if comm is None else (out, carried)


def _sb_attn_bwd(proj, ltot, do, dbuf, *, nb, name, comm=None):
    seq = proj.shape[0] // nb
    tq = tk = min(ATT_TILE, seq)
    nq = seq // tq

    def body(qkv_ref, lt_ref, do_ref, _, d_ref):
        rd = [_qkv_readers((qkv_ref.at[:, pr * PAIR_W:(pr + 1) * PAIR_W],), True) for pr in range(2)]
        upto = _tri(tk, lambda r, c: r <= c)
        before = _tri(tk, lambda r, c: r < c)
        dk_ref = [d_ref.at[:, pr * PAIR_W + LANES:pr * PAIR_W + 2 * LANES] for pr in range(2)]
        dv_ref = [d_ref.at[:, pr * PAIR_W + 2 * LANES:(pr + 1) * PAIR_W] for pr in range(2)]
        for ref in dk_ref + dv_ref:
            ref[...] = jnp.zeros((seq, LANES), F32)
        ch = [(pr, h) for pr in range(2) for h in range(2)]

        def q_tile(i, _):
            r0 = pl.multiple_of(i * tq, tq)
            qm = [_split_heads((rd[pr][0](r0, tq) * SCALE).astype(BF16)) for pr in range(2)]
            dom = [_split_heads(do_ref[pl.ds(r0, tq), pr * LANES:(pr + 1) * LANES].astype(BF16)) for pr in range(2)]
            lt_t = lt_ref[pl.ds(r0, tq), :]
            lt_h = [lt_t[:, pr * LANES + h * HEAD_DIM:pr * LANES + h * HEAD_DIM + 1] for pr, h in ch]
            row = r0 + lax.broadcasted_iota(jnp.int32, (tq, tk), 0)

            def k_tile(j, carry, diagonal):
                pc, qc, dq = carry
                c0 = pl.multiple_of(j * tk, tk)
                kt = [rd[pr][1](c0, tk).astype(BF16) for pr in range(2)]
                vt = [rd[pr][2](c0, tk).astype(BF16) for pr in range(2)]
                km = [_split_heads(kt[pr]) for pr in range(2)]
                if diagonal:
                    strict = (c0 + lax.broadcasted_iota(jnp.int32, (tq, tk), 1)) < row
                ns = range(len(ch))
                z = [lax.dot_general(qm[pr][h], kt[pr], NT_DIMS, preferred_element_type=F32) for pr, h in ch]
                da = [lax.dot_general(dom[pr][h], vt[pr], NT_DIMS, preferred_element_type=F32) for pr, h in ch]
                ls = [_log_sigmoid(z[n]) for n in ns]
                lk = [ls[n] - z[n] for n in ns]
                if diagonal:
                    lk = [jnp.where(strict, lk[n], 0.0) for n in ns]
                parts = [_split_bf16(lk[n]) for n in ns]
                pin = [jnp.dot(parts[n][0], upto, preferred_element_type=F32)
                       + jnp.dot(parts[n][1], upto, preferred_element_type=F32) for n in ns]
                att = [jnp.exp(ls[n] + (lt_h[n] - pc[n] - pin[n])) for n in ns]
                if diagonal:
                    att = [jnp.where(strict, att[n], 0.0) for n in ns]
                dg = [att[n] * da[n] for n in ns]
                qx = [qc[n] + jnp.dot(dg[n].astype(BF16), before, preferred_element_type=F32) for n in ns]
                sig = [jnp.exp(ls[n]) for n in ns]
                dz = [dg[n] * (1.0 - sig[n]) - sig[n] * qx[n] for n in ns]
                if diagonal:
                    dz = [jnp.where(strict, dz[n], 0.0) for n in ns]
                dzb = [dz[n].astype(BF16) for n in ns]
                attb = [att[n].astype(BF16) for n in ns]
                new_dq = []
                for pr in range(2):
                    a, b = 2 * pr, 2 * pr + 1
                    new_dq.append(dq[pr] + (jnp.dot(dzb[a], km[pr][0], preferred_element_type=F32)
                                            + jnp.dot(dzb[b], km[pr][1], preferred_element_type=F32)))
                    dk_ref[pr][pl.ds(c0, tk), :] += (
                        lax.dot_general(dzb[a], qm[pr][0], TN_DIMS, preferred_element_type=F32)
                        + lax.dot_general(dzb[b], qm[pr][1], TN_DIMS, preferred_element_type=F32))
                    dv_ref[pr][pl.ds(c0, tk), :] += (
                        lax.dot_general(attb[a], dom[pr][0], TN_DIMS, preferred_element_type=F32)
                        + lax.dot_general(attb[b], dom[pr][1], TN_DIMS, preferred_element_type=F32))
                return (tuple(pc[n] + jnp.sum(lk[n], axis=-1, keepdims=True) for n in ns),
                        tuple(qc[n] + jnp.sum(dg[n], axis=-1, keepdims=True) for n in ns), tuple(new_dq))

            zero = (jnp.zeros((tq, 1), F32),) * 4
            init = (zero, zero, (jnp.zeros((tq, LANES), F32),) * 2)
            carry = lax.fori_loop(0, i, lambda j, cr: k_tile(j, cr, False), init)
            _, _, dq = k_tile(i, carry, True)
            for pr in range(2):
                d_ref[pl.ds(r0, tq), pr * PAIR_W:pr * PAIR_W + LANES] = dq[pr] * SCALE
            return 0

        lax.fori_loop(0, nq, q_tile, 0)

    both = lambda width, col: pl.BlockSpec((seq, 2 * width), lambda b: (b, col))
    out, carried = _pallas(
        body, name=name, grid=(nb,), in_specs=[both(PAIR_W, MIX_SB), both(LANES, 0), both(LANES, MIX_SB), ANY],
        out_specs=[both(PAIR_W, MIX_SB)], out_shape=[jax.ShapeDtypeStruct(dbuf.shape, F32)],
        args=[proj, ltot, do, dbuf], aliases={3: 0}, sem=("arbitrary",), comm=comm)
    return out[0] if comm is None else (out[0], carried)


def _lane_scan(x, reverse=False):
    n = x.shape[-1]
    lane = lax.broadcasted_iota(jnp.int32, x.shape, 1)
    k = 1
    while k < n:
        if reverse:
            x = x + jnp.where(lane < n - k, pltpu.roll(x, n - k, 1), 0.0)
        else:
            x = x + jnp.where(lane >= k, pltpu.roll(x, k, 1), 0.0)
        k *= 2
    return x


def _fox_gate_fwd(f_rows, b_rows):
    def body(f_ref, b_ref, o_ref):
        o_ref[...] = _lane_scan(_log_sigmoid(f_ref[...] + b_ref[...]))

    return pl.pallas_call(body, name="fox_gate_fwd", out_shape=jax.ShapeDtypeStruct(f_rows.shape, F32))(f_rows, b_rows)


def _fox_gate_bwd(dcum, f_rows, b_rows):
    def body(d_ref, f_ref, b_ref, df_ref, db_ref):
        z = f_ref[...] + b_ref[...]
        df = _lane_scan(d_ref[...], reverse=True) * jnp.exp(_log_sigmoid(-z))
        df_ref[...] = df
        rs = jnp.sum(df, axis=-1, keepdims=True)
        tot = rs
        for e in range(1, f_rows.shape[0] // N_HEADS):
            tot = tot + pltpu.roll(rs, e * N_HEADS, 0)
        db_ref[...] = tot

    return pl.pallas_call(
        body, name="fox_gate_bwd",
        out_shape=[jax.ShapeDtypeStruct(f_rows.shape, F32), jax.ShapeDtypeStruct((f_rows.shape[0], 1), F32)],
    )(dcum, f_rows, b_rows)


def _dil_tables(seq):
    t = min(WIDE_TILE, seq)
    n = seq // t
    a = np.arange(t)
    d = (np.arange(n)[:, None, None] * t + a[None, :, None] - a[None, None, :]).astype(np.int64)
    count = np.zeros(d.shape, np.int64)
    for window, dil in DILATED_PATTERNS:
        count += (d >= 0) & (d % dil == 0) & (d // dil <= window // dil)
    nn = np.maximum(d, 0)
    max_exact = NUM_BUCKETS // 2
    nf = np.maximum(nn, 1).astype(np.float32)
    large = max_exact + (np.log(nf / np.float32(max_exact)) / np.float32(math.log(MAX_DISTANCE / max_exact))
                         * np.float32(NUM_BUCKETS - max_exact)).astype(np.int32)
    bucket = np.where(nn < max_exact, nn, np.minimum(large, NUM_BUCKETS - 1))
    bucket = np.where(count > 0, bucket, -1).astype(np.int32)
    logc = np.where(count > 0, np.log(np.maximum(count, 1)), MASKED).astype(np.float32)
    return bucket, logc


def _dil_bias(rel_bias, seq, comm=None):
    bucket, logc = _dil_tables(seq)
    n, t, _ = bucket.shape

    def body(rb_ref, bk_ref, lc_ref, o_ref):
        h = pl.program_id(0) * 2 + pl.program_id(1)
        bk = bk_ref[...]
        out = lc_ref[...]
        for b in range(NUM_BUCKETS):
            out = jnp.where(bk == b, out + rb_ref[b, h], out)
        o_ref[...] = out

    full = pl.BlockSpec((n, t, t), lambda p, h: (0, 0, 0))
    out, carried = _pallas(
        body, name="dil_bias", grid=(2, 2),
        in_specs=[pl.BlockSpec(memory_space=pltpu.SMEM), full, full],
        out_specs=[pl.BlockSpec((None, None, n, t, t), lambda p, h: (p, h, 0, 0, 0))],
        out_shape=[jax.ShapeDtypeStruct((2, 2, n, t, t), F32)],
        args=[rel_bias, jnp.asarray(bucket), jnp.asarray(logc)], sem=("parallel", "parallel"), comm=comm)
    return out[0] if comm is None else (out[0], carried)


def _dil_bias_bwd(dbias, seq):
    bucket, _ = _dil_tables(seq)
    n, t, _ = bucket.shape

    def body(d_ref, bk_ref, o_ref):
        bk = bk_ref[...]
        lane = lax.broadcasted_iota(jnp.int32, (1, LANES), 1)
        for b in range(NUM_BUCKETS):
            rowv = jnp.zeros((1, LANES), F32)
            for h in range(N_HEADS):
                s = jnp.sum(jnp.where(bk == b, d_ref[h // 2, h % 2], 0.0))
                rowv = jnp.where(lane == h, s, rowv)
            o_ref[b:b + 1, :] = rowv

    out = pl.pallas_call(body, name="dil_bias_bwd", out_shape=jax.ShapeDtypeStruct((NUM_BUCKETS, LANES), F32),
                         compiler_params=pltpu.CompilerParams(vmem_limit_bytes=VMEM_LIMIT))(dbias, jnp.asarray(bucket))
    return out[:, :N_HEADS]


def _shift_rows(x, k, row, fill=0.0):
    n = x.shape[0]
    if k > 0:
        return jnp.where(row >= k, pltpu.roll(x, k, 0), fill)
    return jnp.where(row < n + k, pltpu.roll(x, n + k, 0), fill)


def _row_scan(a, u, row, reverse=False):
    n = a.shape[0]
    k = 1
    while k < n:
        s = -k if reverse else k
        u = a * _shift_rows(u, s, row) + u
        a = a * _shift_rows(a, s, row, 1.0)
        k *= 2
    return u


def _sigmoid(x):
    return 1.0 / (1.0 + jnp.exp(-x))


def _gelu(g):
    return 0.5 * g * (1.0 + lax.erf(g * (2.0 ** -0.5)))


def _gelu_grad(g):
    return 0.5 * (1.0 + lax.erf(g * (2.0 ** -0.5))) + g * jnp.exp(-0.5 * g * g) * (1.0 / math.sqrt(2.0 * math.pi))


def _neg_expm1(x):
    small = -x * (1.0 + x * (0.5 + x * (1.0 / 6.0 + x * (1.0 / 24.0))))
    return jnp.where(x > -0.03, small, 1.0 - jnp.exp(x))


def _lru_core(x, vec, wa, wx, row):
    xs = [_shift_rows(x, 3 - j, row) if j < 3 else x for j in range(4)]
    xc = vec[4:5, :]
    for j in range(4):
        xc = xc + vec[j:j + 1, :] * xs[j]
    xcb = xc.astype(BF16)
    r = _sigmoid(jnp.dot(xcb, wa, preferred_element_type=F32) + vec[5:6, :])
    ig = _sigmoid(jnp.dot(xcb, wx, preferred_element_type=F32) + vec[6:7, :])
    lam = vec[7:8, :]
    sp = jnp.maximum(-lam, 0.0) - _log_sigmoid(jnp.abs(lam))
    la = -LRU_C * r * sp
    a = jnp.exp(la)
    mult = jnp.sqrt(_neg_expm1(2.0 * la))
    return xs, xc, xcb, r, ig, sp, la, a, mult


def _lru_specs(seq):
    xg = pl.BlockSpec((seq, LRU_W), lambda hf, b: (b, COL_LRU // LRU_W + hf))
    mix = pl.BlockSpec((seq, LANES), lambda hf, b: (b, 2 * MIX_LRU + hf))
    vec = pl.BlockSpec((SUBLANES, LANES), lambda hf, b: (0, hf))
    mat = pl.BlockSpec((None, LANES, LANES), lambda hf, b: (hf, 0, 0))
    return xg, mix, vec, mat


def _lru_fwd(proj, vec, wa, wx, out_buf, *, nb, name):
    seq = proj.shape[0] // nb

    def body(xg_ref, vec_ref, wa_ref, wx_ref, _, o_ref):
        row = lax.broadcasted_iota(jnp.int32, (seq, LANES), 0)
        _, xc, _, _, ig, _, _, a, mult = _lru_core(xg_ref[:, 0:LANES], vec_ref[...], wa_ref[...], wx_ref[...], row)
        h = _row_scan(a, mult * (ig * xc), row)
        o_ref[...] = h * _gelu(xg_ref[:, LANES:LRU_W])

    xg, mix, vecs, mat = _lru_specs(seq)
    return pl.pallas_call(
        body, name=name, grid=(2, nb), in_specs=[xg, vecs, mat, mat, ANY], out_specs=mix,
        out_shape=jax.ShapeDtypeStruct(out_buf.shape, F32), input_output_aliases={4: 0},
        compiler_params=_params(("parallel", "arbitrary")))(proj, vec, wa, wx, out_buf)


def _lru_bwd(proj, vec, wa, wx, dout, dbuf, *, nb, name):
    seq = proj.shape[0] // nb

    def body(xg_ref, vec_ref, wa_ref, wx_ref, do_ref, _, d_ref, dvec_ref, dwa_ref, dwx_ref):
        row = lax.broadcasted_iota(jnp.int32, (seq, LANES), 0)
        vec_, wa_, wx_ = vec_ref[...], wa_ref[...], wx_ref[...]
        xs, xc, xcb, r, ig, sp, la, a, mult = _lru_core(xg_ref[:, 0:LANES], vec_, wa_, wx_, row)
        h = _row_scan(a, mult * (ig * xc), row)
        gate, do = xg_ref[:, LANES:LRU_W], do_ref[...]
        d_ref[:, LANES:LRU_W] = do * h * _gelu_grad(gate)
        dh = do * _gelu(gate)
        gacc = _row_scan(_shift_rows(a, -1, row), dh, row, reverse=True)
        da = gacc * _shift_rows(h, 1, row)
        dmult = gacc * (ig * xc)
        dig = gacc * (mult * xc)
        dxc = gacc * (mult * ig)
        dla = da * a - dmult * (a * a) / mult
        dr = (-LRU_C) * sp * dla
        dsp = jnp.sum((-LRU_C) * r * dla, axis=0, keepdims=True)
        dpr = dr * r * (1.0 - r)
        dpi = dig * ig * (1.0 - ig)
        dprb, dpib = dpr.astype(BF16), dpi.astype(BF16)
        dxc = (dxc + lax.dot_general(dprb, wa_, NT_DIMS, preferred_element_type=F32)
               + lax.dot_general(dpib, wx_, NT_DIMS, preferred_element_type=F32))
        dx = vec_[3:4, :] * dxc
        for j in range(3):
            dx = dx + vec_[j:j + 1, :] * _shift_rows(dxc, -(3 - j), row)
        d_ref[:, 0:LANES] = dx

        @pl.when(pl.program_id(1) == 0)
        def _():
            dvec_ref[...] = jnp.zeros_like(dvec_ref)
            dwa_ref[...] = jnp.zeros_like(dwa_ref)
            dwx_ref[...] = jnp.zeros_like(dwx_ref)

        for j in range(4):
            dvec_ref[j:j + 1, :] += jnp.sum(dxc * xs[j], axis=0, keepdims=True)
        dvec_ref[4:5, :] += jnp.sum(dxc, axis=0, keepdims=True)
        dvec_ref[5:6, :] += jnp.sum(dpr, axis=0, keepdims=True)
        dvec_ref[6:7, :] += jnp.sum(dpi, axis=0, keepdims=True)
        lam = vec_[7:8, :]
        dvec_ref[7:8, :] += -dsp * _sigmoid(-lam)
        dwa_ref[...] += lax.dot_general(xcb, dprb, TN_DIMS, preferred_element_type=F32)
        dwx_ref[...] += lax.dot_general(xcb, dpib, TN_DIMS, preferred_element_type=F32)

    xg, mix, vecs, mat = _lru_specs(seq)
    return pl.pallas_call(
        body, name=name, grid=(2, nb), in_specs=[xg, vecs, mat, mat, mix, ANY], out_specs=[xg, vecs, mat, mat],
        out_shape=[jax.ShapeDtypeStruct(dbuf.shape, F32), jax.ShapeDtypeStruct((SUBLANES, 2 * LANES), F32),
                   jax.ShapeDtypeStruct((2, LANES, LANES), F32), jax.ShapeDtypeStruct((2, LANES, LANES), F32)],
        input_output_aliases={5: 0},
        compiler_params=_params(("parallel", "arbitrary")))(proj, vec, wa, wx, dout, dbuf)


FFN_ROWS = 256
FFN_COLS = 1408


def _with_halo(halo, x, k):
    xx = jnp.concatenate([halo, x], axis=0)
    return pltpu.roll(xx, k, 0)[SUBLANES:, :]


def _ffn_conv(x_ref, halo_ref, cw, pos):
    x, halo = x_ref[...], halo_ref[...]
    x1 = jnp.where(pos >= 1, _with_halo(halo, x, 1), 0.0)
    x2 = jnp.where(pos >= 2, _with_halo(halo, x, 2), 0.0)
    return cw[3:4, :] + cw[0:1, :] * x2 + cw[1:2, :] * x1 + cw[2:3, :] * x, x1, x2


def _ffn_specs(tm, tn, gate_off):
    prev = lambda i: jnp.maximum(i * (tm // SUBLANES) - 1, 0)
    up = pl.BlockSpec((tm, tn), lambda j, i: (i, j))
    gate = pl.BlockSpec((tm, tn), lambda j, i: (i, j + gate_off))
    up_h = pl.BlockSpec((SUBLANES, tn), lambda j, i: (prev(i), j))
    gate_h = pl.BlockSpec((SUBLANES, tn), lambda j, i: (prev(i), j + gate_off))
    cw_up = pl.BlockSpec((SUBLANES, tn), lambda j, i: (0, j))
    cw_gate = pl.BlockSpec((SUBLANES, tn), lambda j, i: (0, j + gate_off))
    return up, gate, up_h, gate_h, cw_up, cw_gate


def _ffn_act(hf, cw, *, seq, name):
    t, w2 = hf.shape
    w = w2 // 2
    tm, tn = _tile(seq, FFN_ROWS, SUBLANES), _tile(w, FFN_COLS)

    def body(u_ref, g_ref, uh_ref, gh_ref, cu_ref, cg_ref, o_ref):
        pos = (pl.program_id(1) * tm + lax.broadcasted_iota(jnp.int32, (tm, 1), 0)) % seq
        up, _, _ = _ffn_conv(u_ref, uh_ref, cu_ref[...], pos)
        gate, _, _ = _ffn_conv(g_ref, gh_ref, cg_ref[...], pos)
        o_ref[...] = (_gelu(gate) * up).astype(BF16)

    specs = _ffn_specs(tm, tn, w // tn)
    return pl.pallas_call(
        body, name=name, grid=(w // tn, t // tm), in_specs=list(specs), out_specs=specs[0],
        out_shape=jax.ShapeDtypeStruct((t, w), BF16),
        compiler_params=_params(("parallel", "parallel")))(hf, hf, hf, hf, cw, cw)


def _ffn_bwd(hf, cw, dact, *, seq, name, comm=None):
    t, w2 = hf.shape
    w = w2 // 2
    tm, tn = _tile(seq, FFN_ROWS, 2 * SUBLANES), _tile(w, FFN_COLS)
    ext = tm + SUBLANES
    last = t // SUBLANES - 1

    def body(u_ref, g_ref, uh_ref, gh_ref, cu_ref, cg_ref, un_ref, gn_ref, da_ref, dn_ref, d_ref, dcu_ref, dcg_ref):
        pos = (pl.program_id(1) * tm + lax.broadcasted_iota(jnp.int32, (ext, 1), 0)) % seq

        def conv(x_ref, prev_ref, next_ref, cwv):
            xx = jnp.concatenate([prev_ref[...], x_ref[...], next_ref[...]], axis=0)
            x1 = jnp.where(pos >= 1, pltpu.roll(xx, 1, 0)[SUBLANES:, :], 0.0)
            x2 = jnp.where(pos >= 2, pltpu.roll(xx, 2, 0)[SUBLANES:, :], 0.0)
            x0 = xx[SUBLANES:, :]
            return cwv[3:4, :] + cwv[0:1, :] * x2 + cwv[1:2, :] * x1 + cwv[2:3, :] * x0, (x2, x1, x0)

        def back(d, cwv):
            d1 = jnp.where(pos < seq - 1, pltpu.roll(d, ext - 1, 0), 0.0)
            d2 = jnp.where(pos < seq - 2, pltpu.roll(d, ext - 2, 0), 0.0)
            return (cwv[2:3, :] * d + cwv[1:2, :] * d1 + cwv[0:1, :] * d2)[:tm, :].astype(BF16)

        cu, cg = cu_ref[...], cg_ref[...]
        up, u_taps = conv(u_ref, uh_ref, un_ref, cu)
        gate, g_taps = conv(g_ref, gh_ref, gn_ref, cg)
        da = jnp.concatenate([da_ref[...], dn_ref[...]], axis=0)
        cdf = 0.5 * (1.0 + lax.erf(gate * (2.0 ** -0.5)))
        d_up = da * (gate * cdf)
        d_gate = da * up * (cdf + gate * jnp.exp(-0.5 * gate * gate) * (1.0 / math.sqrt(2.0 * math.pi)))
        d_ref[0] = back(d_up, cu)
        d_ref[1] = back(d_gate, cg)

        @pl.when(pl.program_id(1) == 0)
        def _():
            dcu_ref[...] = jnp.zeros_like(dcu_ref)
            dcg_ref[...] = jnp.zeros_like(dcg_ref)

        for ref, d, taps in ((dcu_ref, d_up, u_taps), (dcg_ref, d_gate, g_taps)):
            own = d[:tm, :]
            for j in range(3):
                ref[j:j + 1, :] += jnp.sum(own * taps[j][:tm, :], axis=0, keepdims=True)
            ref[3:4, :] += jnp.sum(own, axis=0, keepdims=True)

    gate_off = w // tn
    specs = _ffn_specs(tm, tn, gate_off)
    tile, cwt = specs[0], specs[4]
    nxt = lambda i: jnp.minimum((i + 1) * (tm // SUBLANES), last)
    up_n = pl.BlockSpec((SUBLANES, tn), lambda j, i: (nxt(i), j))
    gate_n = pl.BlockSpec((SUBLANES, tn), lambda j, i: (nxt(i), j + gate_off))
    out, carried = _pallas(
        body, name=name, grid=(w // tn, t // tm), in_specs=list(specs) + [up_n, gate_n, tile, up_n],
        out_specs=[pl.BlockSpec((2, tm, tn), lambda j, i: (0, i, j)), cwt, cwt],
        out_shape=[jax.ShapeDtypeStruct((2, t, w), BF16), jax.ShapeDtypeStruct((SUBLANES, w), F32),
                   jax.ShapeDtypeStruct((SUBLANES, w), F32)],
        args=[hf, hf, hf, hf, cw, cw, hf, hf, dact, dact], sem=("parallel", "arbitrary"), comm=comm)
    return out if comm is None else (out, carried)


def _adamw(w, g, m, v, *, name, rows=256, lead=None):
    nl, r, c = w.shape
    tr = _tile(r, rows, SUBLANES)

    def body(w_ref, g_ref, m_ref, v_ref, d_ref, nm_ref, nv_ref):
        g_ = g_ref[...]
        nm = ADAM_B1 * m_ref[...] + (1.0 - ADAM_B1) * g_
        nv = ADAM_B2 * v_ref[...] + (1.0 - ADAM_B2) * (g_ * g_)
        m_hat = nm / (1.0 - ADAM_B1 ** ADAM_STEP)
        v_hat = nv / (1.0 - ADAM_B2 ** ADAM_STEP)
        d_ref[...] = -ADAM_LR * (m_hat / (jnp.sqrt(v_hat) + ADAM_EPS) + ADAM_WD * w_ref[...])
        nm_ref[...] = nm
        nv_ref[...] = nv

    shape = jax.ShapeDtypeStruct((nl, r, c), F32)
    if lead is None:
        spec, grid = pl.BlockSpec((None, tr, c), lambda l, i: (l, i, 0)), (nl, r // tr)
    else:
        spec, grid = pl.BlockSpec((lead, r, c), lambda i: (i, 0, 0)), (nl // lead,)
    return pl.pallas_call(body, name=name, grid=grid, in_specs=[spec] * 4, out_specs=[spec] * 3,
                          out_shape=[shape] * 3, compiler_params=_params(("parallel",) * len(grid)))(w, g, m, v)


def _mesh_pos():
    return lax.axis_index("x"), lax.axis_index("y"), lax.axis_index("c")


def _peers(x, y):
    chips = [(1 - x, y), (x, 1 - y), (1 - x, 1 - y)]
    return [(px, py, 2 * px + py) for px, py in chips]


def _remote(src, dst, send_sems, recv_sems, idx, to):
    return pltpu.make_async_remote_copy(src, dst, send_sems.at[idx], recv_sems.at[idx], device_id=to,
                                        device_id_type=MESH)


class _Comm:
    def __init__(self, operands, out_shape, aliases, sems, copies):
        self.operands, self.out_shape, self.aliases, self.sems, self.copies = operands, out_shape, aliases, sems, copies

    def start(self, ins, outs, sems):
        for send, _ in self.copies(ins, outs, sems):
            send.start()

    def wait(self, ins, outs, sems):
        pairs = self.copies(ins, outs, sems)
        for _, recv in pairs:
            recv.wait_recv()
        for send, _ in pairs:
            send.wait_send()


def _pallas(body, *, name, grid, in_specs, out_specs, out_shape, args, aliases=None, scratch=(), sem, comm=None,
            vmem=None):
    n_in, n_out = len(in_specs), len(out_specs)
    aliases = dict(aliases or {})
    if comm is None:
        out = pl.pallas_call(body, name=name, grid=grid, in_specs=in_specs, out_specs=out_specs, out_shape=out_shape,
                             input_output_aliases=aliases, scratch_shapes=list(scratch),
                             compiler_params=_params(sem, vmem))(*args)
        return list(out), []
    nci, nco, ncs = len(comm.operands), len(comm.out_shape), len(comm.sems)

    def carried(*refs):
        ins, cin = refs[:n_in], refs[n_in:n_in + nci]
        o0 = n_in + nci
        outs, cout = refs[o0:o0 + n_out], refs[o0 + n_out:o0 + n_out + nco]
        s0 = o0 + n_out + nco
        own, csem = refs[s0:len(refs) - ncs], refs[len(refs) - ncs:]
        ids = [pl.program_id(ax) for ax in range(len(grid))]
        first, last = ids[0] == 0, ids[0] == grid[0] - 1
        for i, g in zip(ids[1:], grid[1:]):
            first, last = jnp.logical_and(first, i == 0), jnp.logical_and(last, i == g - 1)

        @pl.when(first)
        def _():
            comm.start(cin, cout, csem)

        body(*ins, *outs, *own)

        @pl.when(last)
        def _():
            comm.wait(cin, cout, csem)

    aliases.update({n_in + i: n_out + j for i, j in comm.aliases.items()})
    out = pl.pallas_call(
        carried, name=name, grid=grid, in_specs=list(in_specs) + [ANY] * nci, out_specs=list(out_specs) + [ANY] * nco,
        out_shape=list(out_shape) + list(comm.out_shape), input_output_aliases=aliases,
        scratch_shapes=list(scratch) + list(comm.sems),
        compiler_params=_params(("arbitrary",) * len(grid), vmem))(*args, *comm.operands)
    return list(out[:n_out]), list(out[n_out:])


def _run_comm(comm, *, name):
    nci, nco = len(comm.operands), len(comm.out_shape)

    def body(*refs):
        ins, outs, sems = refs[:nci], refs[nci:nci + nco], refs[nci + nco:]
        comm.start(ins, outs, sems)
        comm.wait(ins, outs, sems)

    return pl.pallas_call(body, name=name, in_specs=[ANY] * nci, out_specs=[ANY] * nco, out_shape=list(comm.out_shape),
                          input_output_aliases=dict(comm.aliases), scratch_shapes=list(comm.sems))(*comm.operands)


def _pair_sems(*shape):
    return [pltpu.SemaphoreType.DMA(shape), pltpu.SemaphoreType.DMA(shape)]


def _gather_comm(bufs, layer, stage):
    n = len(bufs)

    def copies(ins, outs, sems):
        x, y, c = _mesh_pos()
        me = 2 * x + y
        pairs = []
        for i in range(n):
            h = bufs[i].shape[2] // 2
            mine, other = pl.ds(c * h, h), pl.ds((1 - c) * h, h)
            for r, (px, py, k) in enumerate(_peers(x, y)):
                if stage == 0:
                    send = _remote(ins[i].at[layer, me, mine, :], outs[i].at[layer, me, mine, :], *sems, (i, r), (px, py, c))
                    land = outs[i].at[layer, k, mine, :]
                    recv = _remote(land, land, *sems, (i, r), (px, py, c))
                else:
                    send = _remote(ins[i].at[layer, k, mine, :], outs[i].at[layer, k, mine, :], *sems, (i, r), (x, y, 1 - c))
                    land = outs[i].at[layer, k, other, :]
                    recv = _remote(land, land, *sems, (i, r), (x, y, 1 - c))
                pairs.append((send, recv))
        return pairs

    return _Comm(bufs, [jax.ShapeDtypeStruct(b.shape, b.dtype) for b in bufs], {i: i for i in range(n)},
                 _pair_sems(n, 3), copies)


def _reduce_sibling_comm(gs):
    n = len(gs)

    def copies(ins, outs, sems):
        x, y, c = _mesh_pos()
        pairs = []
        for i in range(n):
            h = gs[i].shape[1] // 2
            cp = _remote(ins[i].at[:, pl.ds((1 - c) * h, h), :], outs[i], *sems, i, (x, y, 1 - c))
            pairs.append((cp, cp))
        return pairs

    return _Comm(gs, [jax.ShapeDtypeStruct((g.shape[0], g.shape[1] // 2, g.shape[2]), g.dtype) for g in gs], {},
                 _pair_sems(n), copies)


def _reduce_chips_comm(ps):
    n = len(ps)

    def copies(ins, outs, sems):
        x, y, c = _mesh_pos()
        pairs = []
        for i in range(n):
            for r, (px, py, k) in enumerate(_peers(x, y)):
                cp = _remote(ins[i].at[k], outs[i].at[r], *sems, (i, r), (px, py, c))
                pairs.append((cp, cp))
        return pairs

    return _Comm(ps, [jax.ShapeDtypeStruct((3,) + p.shape[1:], p.dtype) for p in ps], {}, _pair_sems(n, 3), copies)


def _share_halves(bufs, *, name):
    n = len(bufs)

    def body(*refs):
        ins, outs = refs[:n], refs[n:2 * n]
        send_sems, recv_sems = refs[2 * n:]
        x, y, c = _mesh_pos()
        cps = []
        for i in range(n):
            h = bufs[i].shape[1] // 2
            mine = pl.ds(c * h, h)
            cp = _remote(ins[i].at[:, mine, :], outs[i].at[:, mine, :], send_sems, recv_sems, i, (x, y, 1 - c))
            cp.start()
            cps.append(cp)
        for cp in cps:
            cp.wait()

    return pl.pallas_call(
        body, name=name, in_specs=[ANY] * n, out_specs=[ANY] * n,
        out_shape=[jax.ShapeDtypeStruct(b.shape, b.dtype) for b in bufs],
        input_output_aliases={i: i for i in range(n)},
        scratch_shapes=[pltpu.SemaphoreType.DMA((n,)), pltpu.SemaphoreType.DMA((n,))])(*bufs)


def _add_own_half(full, recv, pos, *, name, rows=256):
    k4, h, n = recv.shape
    tr = _tile(h, rows, 16)
    nblk = h // tr

    def body(pos_ref, a_ref, b_ref, o_ref):
        o_ref[...] = (a_ref[...] + b_ref[...]).astype(BF16)

    grid_spec = pltpu.PrefetchScalarGridSpec(
        num_scalar_prefetch=1, grid=(k4, nblk),
        in_specs=[pl.BlockSpec((None, tr, n), lambda k, i, pos_ref: (k, pos_ref[1] * nblk + i, 0)),
                  pl.BlockSpec((None, tr, n), lambda k, i, pos_ref: (k, i, 0))],
        out_specs=pl.BlockSpec((None, tr, n), lambda k, i, pos_ref: (k, i, 0)))
    return pl.pallas_call(body, name=name, grid_spec=grid_spec, out_shape=jax.ShapeDtypeStruct(recv.shape, BF16),
                          compiler_params=_params(("parallel", "parallel")))(pos, full, recv)


def _sum_into(own, others, buf, pos, layer, *, name, rows=256):
    _, h, n = own.shape
    tr = _tile(h, rows, 16)
    nblk = h // tr

    def body(pos_ref, own_ref, oth_ref, _, o_ref):
        acc = own_ref[...].astype(F32)
        for r in range(3):
            acc = acc + oth_ref[r].astype(F32)
        o_ref[...] = acc

    grid_spec = pltpu.PrefetchScalarGridSpec(
        num_scalar_prefetch=1, grid=(nblk,),
        in_specs=[pl.BlockSpec((None, tr, n), lambda i, pos_ref: (pos_ref[0], i, 0)),
                  pl.BlockSpec((3, tr, n), lambda i, pos_ref: (0, i, 0)), ANY],
        out_specs=pl.BlockSpec((None, tr, n), lambda i, pos_ref: (layer, pos_ref[1] * nblk + i, 0)))
    return pl.pallas_call(body, name=name, grid_spec=grid_spec, out_shape=jax.ShapeDtypeStruct(buf.shape, F32),
                          input_output_aliases={3: 0}, compiler_params=_params(("parallel",)))(pos, own, others, buf)


def _pair_comm(buf):
    def copies(ins, outs, sems):
        x, y, c = _mesh_pos()
        land = outs[0].at[1 - c]
        return [(_remote(ins[0].at[c], outs[0].at[c], *sems, 0, (x, y, 1 - c)),
                 _remote(land, land, *sems, 0, (x, y, 1 - c)))]

    return _Comm([buf], [jax.ShapeDtypeStruct(buf.shape, buf.dtype)], {0: 0}, _pair_sems(1), copies)


def _quad_comm(buf):
    def copies(ins, outs, sems):
        x, y, c = _mesh_pos()
        me = 2 * x + y
        pairs = []
        for r, (px, py, k) in enumerate(_peers(x, y)):
            land = outs[0].at[k]
            pairs.append((_remote(ins[0].at[me], outs[0].at[me], *sems, r, (px, py, c)),
                          _remote(land, land, *sems, r, (px, py, c))))
        return pairs

    return _Comm([buf], [jax.ShapeDtypeStruct(buf.shape, buf.dtype)], {0: 0}, _pair_sems(3), copies)


def _chip_bcast(buf, *, name):
    def body(src_ref, out_ref, send_sems, recv_sems, local_sem):
        x, y, c = _mesh_pos()
        me = 2 * x + y
        local = pltpu.make_async_copy(src_ref, out_ref.at[me], local_sem)
        local.start()
        sends = []
        for r, (px, py, _) in enumerate(_peers(x, y)):
            cp = _remote(src_ref, out_ref.at[me], send_sems, recv_sems, r, (px, py, c))
            cp.start()
            sends.append(cp)
        for r, (px, py, k) in enumerate(_peers(x, y)):
            _remote(src_ref, out_ref.at[k], send_sems, recv_sems, r, (px, py, c)).wait_recv()
        for cp in sends:
            cp.wait_send()
        local.wait()

    return pl.pallas_call(
        body, name=name, in_specs=[ANY], out_specs=ANY, out_shape=jax.ShapeDtypeStruct((4,) + buf.shape, buf.dtype),
        scratch_shapes=[pltpu.SemaphoreType.DMA((3,)), pltpu.SemaphoreType.DMA((3,)), pltpu.SemaphoreType.DMA])(buf)


def _sum_slots(buf, *, name, rows=384):
    r, n = buf.shape[-2:]
    k = int(np.prod(buf.shape[:-2]))
    tr = _tile(r, rows, SUBLANES)

    def body(b_ref, o_ref):
        acc = b_ref[0]
        for s in range(1, k):
            acc = acc + b_ref[s]
        o_ref[...] = acc

    return pl.pallas_call(
        body, name=name, grid=(r // tr,), in_specs=[pl.BlockSpec((k, tr, n), lambda i: (0, i, 0))],
        out_specs=pl.BlockSpec((tr, n), lambda i: (i, 0)), out_shape=jax.ShapeDtypeStruct((r, n), F32),
        compiler_params=_params(("parallel",)))(buf.reshape((k, r, n)))


ROW = 1024
BIG = (("w_in", 2), ("w_out", 1), ("w_cq", 1), ("w_ck", 1), ("w_cv", 1), ("w_co", 2), ("w_up", 2), ("w_down", 1))
CONV = ("lru_conv_w", "ffn_conv_w")
REPLICATED = ("norm_mix_g", "b_forget", "lru_conv_b", "lru_w_a", "lru_b_a", "lru_w_x", "lru_b_x", "lru_lambda",
              "norm_cross_g", "norm_mem_g", "norm_ffn_g", "ffn_conv_b", "rel_bias", "final_norm_g")
WEIGHTS = ('norm_mix_g', 'w_in', 'b_forget', 'lru_conv_w', 'lru_conv_b', 'lru_w_a', 'lru_b_a', 'lru_w_x', 'lru_b_x',
           'lru_lambda', 'w_out', 'norm_cross_g', 'norm_mem_g', 'w_cq', 'w_ck', 'w_cv', 'w_co', 'norm_ffn_g', 'w_up',
           'ffn_conv_w', 'ffn_conv_b', 'w_down', 'rel_bias', 'final_norm_g')
INPUTS = ("x", "mem") + WEIGHTS + ("loss_target",) + tuple("m_" + n for n in WEIGHTS) + tuple("v_" + n for n in WEIGHTS)


def _round_up(n, m):
    return -(-n // m) * m


class _Packing:
    def __init__(self, entries):
        self.entries, self.off = entries, {}
        o = 0
        for name, shape in entries:
            self.off[name] = o
            o += _round_up(int(np.prod(shape)), ROW)
        self.used = o
        self.rows = _round_up(o // ROW, SUBLANES)

    def pack(self, arrays):
        parts = []
        for name, shape in self.entries:
            n = int(np.prod(shape))
            parts.append(jnp.pad(arrays[name].reshape(n), (0, _round_up(n, ROW) - n)))
        tail = self.rows * ROW - self.used
        if tail:
            parts.append(jnp.zeros((tail,), F32))
        return jnp.concatenate(parts).reshape(self.rows, ROW)

    def unpack(self, flat, lead=()):
        out = {}
        for name, shape in self.entries:
            n = int(np.prod(shape))
            r0, nr = self.off[name] // ROW, _round_up(n, ROW) // ROW
            rows = lax.slice_in_dim(flat, r0, r0 + nr, axis=len(lead)).reshape(lead + (nr * ROW,))
            out[name] = lax.slice_in_dim(rows, 0, n, axis=len(lead)).reshape(lead + tuple(shape))
        return out


def _to_shards(g, axis):
    r, c = g.shape
    if axis == 1:
        return g.reshape(4, r // 4, c)
    return g.reshape(r, 4, c // 4).transpose(1, 0, 2)


def _from_shards(s, axis):
    _, r, c = s.shape
    if axis == 1:
        return s.reshape(4 * r, c)
    return s.transpose(1, 0, 2).reshape(r, 4 * c)


def _proj_blocks():
    blocks = []
    for mixer in (MIX_SB, MIX_FOX, MIX_DIL):
        for p in range(2):
            blocks += [ORIG_COL[mixer] + part * 2 * LANES + p * LANES for part in range(3)]
    for hf in range(2):
        blocks += [ORIG_LRU_X + hf * LANES, ORIG_LRU_G + hf * LANES]
    return blocks


def _pad_w_in(w):
    parts = [w[..., s:s + LANES] for s in _proj_blocks()]
    parts += [w[..., 1536:1540], jnp.zeros(w.shape[:-1] + (PROJ_W - COL_F - N_HEADS,), w.dtype)]
    return jnp.concatenate(parts, axis=-1)


def _unpad_w_in(wp):
    blocks = _proj_blocks()
    order = sorted(range(len(blocks)), key=lambda i: blocks[i])
    parts = []
    for i in order:
        if blocks[i] == ORIG_COL[MIX_DIL]:
            parts.append(wp[..., COL_F:COL_F + N_HEADS])
        parts.append(wp[..., i * LANES:(i + 1) * LANES])
    return jnp.concatenate(parts, axis=-1)


def _block_diag(w):
    z = jnp.zeros((HEAD_DIM, HEAD_DIM), w.dtype)
    half = lambda a, b: jnp.concatenate([jnp.concatenate([a, z], 1), jnp.concatenate([z, b], 1)], 0)
    return jnp.stack([half(w[0], w[1]), half(w[2], w[3])])


def _block_diag_grad(d):
    return jnp.stack([d[0, :HEAD_DIM, :HEAD_DIM], d[0, HEAD_DIM:, HEAD_DIM:],
                      d[1, :HEAD_DIM, :HEAD_DIM], d[1, HEAD_DIM:, HEAD_DIM:]])


def _fox_layouts(cum, nb, seq):
    tk = min(WIDE_TILE, seq)
    col = cum.reshape(nb, 2, 2, seq).transpose(0, 1, 3, 2)
    row = cum.reshape(nb, 2, 2, seq // tk, tk).transpose(0, 1, 3, 2, 4)
    return col, row


def _layer_params(w, l, nb):
    lru_vec = jnp.concatenate([w["lru_conv_w"][l], w["lru_conv_b"][l][None], w["lru_b_a"][l][None],
                               w["lru_b_x"][l][None], w["lru_lambda"][l][None]], axis=0)
    ffn_cw = jnp.concatenate([w["ffn_conv_w"][l], w["ffn_conv_b"][l][None],
                              jnp.zeros((SUBLANES - 4, 2 * D_FF), F32)], axis=0)
    return dict(
        w_in=w["w_in_padded"][l], lru_vec=lru_vec,
        wa=_block_diag(w["lru_w_a"][l]).astype(BF16), wx=_block_diag(w["lru_w_x"][l]).astype(BF16),
        ffn_cw=ffn_cw, b_rows=jnp.tile(w["b_forget"][l], nb).reshape(nb * N_HEADS, 1))


NORM_ROWS = 512
NORM_FWD_ROWS = 1024


def _merge_comms(comms):
    if len(comms) == 1:
        return comms[0]
    operands, out_shape, aliases, sems, spans = [], [], {}, [], []
    for cm in comms:
        aliases.update({len(operands) + i: len(out_shape) + j for i, j in cm.aliases.items()})
        spans.append((len(operands), len(out_shape), len(sems)))
        operands += list(cm.operands)
        out_shape += list(cm.out_shape)
        sems += list(cm.sems)

    def copies(ins, outs, sm):
        pairs = []
        for cm, (i0, o0, s0) in zip(comms, spans):
            pairs += cm.copies(ins[i0:i0 + len(cm.operands)], outs[o0:o0 + len(cm.out_shape)], sm[s0:s0 + len(cm.sems)])
        return pairs

    return _Comm(operands, out_shape, aliases, sems, copies)


GATHER_FIRST = ("w_in",)
GATHER_MID = ("w_out", "w_cq", "w_ck", "w_cv", "w_co")
GATHER_LAST = ("w_up", "w_down")


class _WeightGather:
    def __init__(self, slots, w, depth):
        self.slots, self.w, self.depth = slots, w, depth

    def plan(self, l, key):
        nxt = l + 1 if l + 1 < self.depth else None
        early = GATHER_FIRST + GATHER_MID
        if l == 0:
            table = {"proj": [(GATHER_MID, 0, 0)],
                     "sb_fwd": [(GATHER_MID, 0, 1), (GATHER_LAST, 0, 0)],
                     "fox_fwd": [(GATHER_LAST, 0, 1)] + ([(early, nxt, 0)] if nxt else []),
                     "dil_fwd": [(early, nxt, 1), (GATHER_LAST, nxt, 0)] if nxt else [],
                     "out": [(GATHER_LAST, nxt, 1)] if nxt else []}
        else:
            everything = early + GATHER_LAST
            table = {"sb_fwd": [(everything, nxt, 0)], "fox_fwd": [(everything, nxt, 1)]} if nxt else {}
        return table.get(key, [])

    def comm(self, l, key):
        entries = self.plan(l, key)
        if not entries:
            return None
        return _merge_comms([_gather_comm([self.slots[n] for n in names], layer, stage)
                             for names, layer, stage in entries])

    def done(self, l, key, landed):
        landed = list(landed)
        for names, layer, stage in self.plan(l, key):
            for n in names:
                self.slots[n] = landed.pop(0)
            if stage == 1:
                self.take(names, layer)

    def take(self, names, layer):
        for n, axis in BIG:
            if n in names:
                self.w[n][layer] = self.slots[n][layer] if n == "w_up" else _from_shards(self.slots[n][layer], axis)
        if "w_in" in names:
            self.w["w_in_padded"][layer] = _pad_w_in(self.w["w_in"][layer])

    def first_comm(self, stage):
        return _gather_comm([self.slots[n] for n in GATHER_FIRST], 0, stage)

    def first_done(self, stage, landed):
        self.slots.update(zip(GATHER_FIRST, landed))
        if stage == 1:
            self.take(GATHER_FIRST, 0)


def _layer_fwd(x, h, mem, w, lp, l, next_g, bias, nb, gather):
    t, d = x.shape
    seq = t // nb
    tag = f"l{l}"
    sv = dict(x0=x)

    def carrying(key, fn):
        comm = gather.comm(l, key)
        res = fn(comm)
        if comm is not None:
            res, landed = res
            gather.done(l, key, landed)
        return res

    proj = carrying("proj", lambda cm: _mm(h, lp["w_in"], name=tag + "_proj", comm=cm))
    mixed, ltot = carrying("sb_fwd", lambda cm: _sb_attn_fwd(proj, lax.empty((t, d), F32), nb=nb,
                                                             name=tag + "_sb_fwd", comm=cm))
    f_rows = proj[:, COL_F:COL_F + N_HEADS].reshape(nb, seq, N_HEADS).transpose(0, 2, 1).reshape(nb * N_HEADS, seq)
    cum_col, cum_row = _fox_layouts(_fox_gate_fwd(f_rows, lp["b_rows"]), nb, seq)
    mixed, lse_fox = carrying("fox_fwd", lambda cm: _softmax_attn_fwd(
        proj, nb=nb, mode="fox", mixer=MIX_FOX, out_buf=mixed, extra=(cum_col, cum_row), name=tag + "_fox_fwd", comm=cm))
    mixed, lse_dil = carrying("dil_fwd", lambda cm: _softmax_attn_fwd(
        proj, nb=nb, mode="dil", mixer=MIX_DIL, out_buf=mixed, extra=(bias,), name=tag + "_dil_fwd", comm=cm))
    mixed = _lru_fwd(proj, lp["lru_vec"], lp["wa"], lp["wx"], mixed, nb=nb, name=tag + "_lru_fwd")
    x1, hq = carrying("out", lambda cm: _mm(mixed, w["w_out"][l], res=x, norm_g=w["norm_cross_g"][l], ti=NORM_FWD_ROWS,
                                            name=tag + "_out", comm=cm))
    memn = _rmsnorm(mem, w["norm_mem_g"][l], name=tag + "_norm_mem")
    q = _mm(hq, w["w_cq"][l], name=tag + "_cq")
    k = _mm(memn, w["w_ck"][l], name=tag + "_ck")
    v = _mm(memn, w["w_cv"][l], name=tag + "_cv")
    oc, lse_c = _softmax_attn_fwd((q, k, v), nb=nb, mode="cross", name=tag + "_cross_fwd")
    x2, hn = _mm(oc, w["w_co"][l], res=x1, norm_g=w["norm_ffn_g"][l], ti=NORM_FWD_ROWS, name=tag + "_co")
    hf = _mm(hn, w["w_up"][l], b_chunks=4, name=tag + "_up")
    act = _ffn_act(hf, lp["ffn_cw"], seq=seq, name=tag + "_ffn_act")
    if next_g is None:
        x3, h_next = _mm(act, w["w_down"][l], res=x2, name=tag + "_down"), None
    else:
        x3, h_next = _mm(act, w["w_down"][l], res=x2, norm_g=next_g, ti=NORM_FWD_ROWS, name=tag + "_down")
    sv.update(h=h, proj=proj, ltot=ltot, f_rows=f_rows, cum_col=cum_col, cum_row=cum_row, lse_fox=lse_fox,
              lse_dil=lse_dil, mixed=mixed, x1=x1, hq=hq, memn=memn, q=q, k=k, v=v, oc=oc, lse_c=lse_c, x2=x2,
              hn=hn, hf=hf, act=act)
    return x3, h_next, sv


class _PendingReduce:
    def __init__(self, full, pos, layer):
        self.names, self.full, self.pos, self.layer = list(full), list(full.values()), pos, layer

    def sibling_comm(self):
        return _reduce_sibling_comm(self.full)

    def add(self, from_sibling):
        self.partial = [_add_own_half(f, r, self.pos, name=f"l{self.layer}_reduce_add_{n}")
                        for f, r, n in zip(self.full, from_sibling, self.names)]

    def chips_comm(self):
        return _reduce_chips_comm(self.partial)

    def finish(self, others, g_shard):
        g_shard = dict(g_shard)
        for p, o, n in zip(self.partial, others, self.names):
            g_shard[n] = _sum_into(p, o, g_shard[n], self.pos, self.layer, name=f"l{self.layer}_reduce_sum_{n}")
        return g_shard


def _layer_bwd(dx3, mem, sv, w, lp, l, bias, nb, pos, pending=None, g_shard=None, reduce_early=False):
    t = dx3.shape[0]
    seq = t // nb
    tag = f"l{l}"
    g = {}
    down_rows = _tile(sv["act"].shape[1], 1408)
    if pending is None:
        g["w_down"] = _mm(sv["act"], dx3, ta=True, ti=down_rows, name=tag + "_dw_down")
    else:
        g["w_down"], from_sibling = _mm(sv["act"], dx3, ta=True, ti=down_rows, comm=pending.sibling_comm(),
                                        name=tag + "_dw_down")
        pending.add(from_sibling)
    dact = _mm(dx3, w["w_down"][l], tb=True, name=tag + "_dact")
    if pending is None:
        dhf, dcu, dcg = _ffn_bwd(sv["hf"], lp["ffn_cw"], dact, seq=seq, name=tag + "_ffn_bwd")
    else:
        (dhf, dcu, dcg), others = _ffn_bwd(sv["hf"], lp["ffn_cw"], dact, seq=seq, name=tag + "_ffn_bwd",
                                           comm=pending.chips_comm())
        g_shard = pending.finish(others, g_shard)
    dcw = jnp.concatenate([dcu, dcg], axis=1)
    g["ffn_conv_w"], g["ffn_conv_b"] = dcw[:3], dcw[3]
    g["w_up"] = _mm(sv["hn"], dhf, ta=True, halves="b", col_shards=4, name=tag + "_dw_up")
    early = _PendingReduce(_big_grad_shards(g, EARLY), pos, l) if reduce_early else None
    res = _mm(dhf, w["w_up"][l], tb=True, halves="a", b_chunks=4, norm_bwd=(sv["x2"], w["norm_ffn_g"][l], dx3), ti=NORM_ROWS,
              comm=early.sibling_comm() if early else None, name=tag + "_dhn")
    if early:
        res, from_sibling = res
        early.add(from_sibling)
    dx2, dg = res
    g["norm_ffn_g"] = dg.reshape(-1)
    g["w_co"] = _mm(sv["oc"], dx2, ta=True, col_shards=4, name=tag + "_dw_co")
    doc = _mm(dx2, w["w_co"][l], tb=True, name=tag + "_doc")
    dq, dk, dv = _softmax_attn_bwd((sv["q"], sv["k"], sv["v"]), sv["oc"], sv["lse_c"], doc, nb=nb, mode="cross",
                                   name=tag + "_cross_bwd")
    g["w_cq"] = _mm(sv["hq"], dq, ta=True, name=tag + "_dw_cq")
    g["w_ck"] = _mm(sv["memn"], dk, ta=True, name=tag + "_dw_ck")
    g["w_cv"] = _mm(sv["memn"], dv, ta=True, name=tag + "_dw_cv")
    dx1, dg = _mm(dq, w["w_cq"][l], tb=True, norm_bwd=(sv["x1"], w["norm_cross_g"][l], dx2), ti=NORM_ROWS,
                  name=tag + "_dhq")
    g["norm_cross_g"] = dg.reshape(-1)
    dmemn = _mm(dv, w["w_cv"][l], tb=True, res=_mm(dk, w["w_ck"][l], tb=True, name=tag + "_dmem_k"),
                name=tag + "_dmem_v")
    _, g["norm_mem_g"] = _rmsnorm_bwd(dmemn, mem, w["norm_mem_g"][l], None, name=tag + "_norm_mem_bwd")
    mixed, proj = sv["mixed"], sv["proj"]
    g["w_out"] = _mm(mixed, dx1, ta=True, name=tag + "_dw_out")
    mid = _PendingReduce(_big_grad_shards(g, GATHER_MID), pos, l) if reduce_early else None
    dmixed = _mm(dx1, w["w_out"][l], tb=True, name=tag + "_dmixed", comm=mid.sibling_comm() if mid else None)
    if mid:
        dmixed, from_sibling = dmixed
        mid.add(from_sibling)
    dproj = _sb_attn_bwd(proj, sv["ltot"], dmixed, lax.empty((t, PROJ_W), F32), nb=nb, name=tag + "_sb_bwd",
                         comm=early.chips_comm() if early else None)
    if early:
        dproj, others = dproj
        g_shard = early.finish(others, g_shard)
    res = _softmax_attn_bwd(
        proj, mixed, sv["lse_fox"], dmixed, nb=nb, mode="fox", mixer=MIX_FOX, dbuf=dproj,
        extra=(sv["cum_col"], sv["cum_row"]), name=tag + "_fox_bwd", comm=mid.chips_comm() if mid else None)
    if mid:
        res, others = res
        g_shard = mid.finish(others, g_shard)
    dproj, dcum_k, dcum_q = res
    dcum = (dcum_k.transpose(0, 1, 3, 2, 4).reshape(nb * N_HEADS, seq)
            + dcum_q.transpose(0, 1, 3, 2).reshape(nb * N_HEADS, seq))
    df_rows, db = _fox_gate_bwd(dcum, sv["f_rows"], lp["b_rows"])
    g["b_forget"] = db[:N_HEADS, 0]
    df = df_rows.reshape(nb, N_HEADS, seq).transpose(0, 2, 1).reshape(t, N_HEADS)
    dproj, dbias = _softmax_attn_bwd(proj, mixed, sv["lse_dil"], dmixed, nb=nb, mode="dil", mixer=MIX_DIL,
                                     dbuf=dproj, extra=(bias,), name=tag + "_dil_bwd")
    dproj, dvec, dwa, dwx = _lru_bwd(proj, lp["lru_vec"], lp["wa"], lp["wx"], dmixed, dproj, nb=nb,
                                     name=tag + "_lru_bwd")
    g["lru_conv_w"], g["lru_conv_b"], g["lru_b_a"], g["lru_b_x"], g["lru_lambda"] = (
        dvec[0:4], dvec[4], dvec[5], dvec[6], dvec[7])
    g["lru_w_a"], g["lru_w_x"] = _block_diag_grad(dwa), _block_diag_grad(dwx)
    dproj = lax.dynamic_update_slice(dproj, jnp.pad(df, ((0, 0), (0, PROJ_W - COL_F - N_HEADS))), (0, COL_F))
    g["w_in_padded"] = _mm(sv["h"], dproj, ta=True, name=tag + "_dw_in")
    dx0, dg = _mm(dproj, lp["w_in"], tb=True, norm_bwd=(sv["x0"], w["norm_mix_g"][l], dx1), ti=NORM_ROWS,
                  name=tag + "_dh")
    g["norm_mix_g"] = dg.reshape(-1)
    return dx0, g, dbias, g_shard


def _big_grad_shards(g, names):
    out = {}
    for n, axis in BIG:
        if n not in names:
            continue
        if n in ("w_up", "w_co"):
            out[n] = g[n]
        else:
            out[n] = _to_shards(_unpad_w_in(g["w_in_padded"]) if n == "w_in" else g[n], axis)
    return out


EARLY = ("w_down", "w_up")


def kernel(*args):
    a = dict(zip(INPUTS, args, strict=True))
    nb, seq, d = a["x"].shape
    depth = a["norm_mix_g"].shape[0]
    x = a["x"].reshape(nb * seq, d)
    mem = a["mem"].reshape(nb * a["mem"].shape[1], d)
    target = a["loss_target"].reshape(nb * seq, d)
    cx, cy, c = _mesh_pos()
    chip = 2 * cx + cy
    pos = jnp.stack([chip, c]).astype(jnp.int32)

    slots = {}
    for n, _ in BIG:
        own = a[n].astype(BF16)[:, None]
        slots[n] = lax.dynamic_update_slice(lax.empty((depth, 4) + own.shape[2:], BF16), own, (0, chip, 0, 0))
    w = {n: a[n] for n in REPLICATED}
    w.update({n: {} for n, _ in BIG}, w_in_padded={})
    gather = _WeightGather(slots, w, depth)
    bias, landed = _dil_bias(w["rel_bias"], seq, comm=gather.first_comm(0))
    gather.first_done(0, landed)
    h, landed = _rmsnorm(x, w["norm_mix_g"][0], name="l0_norm_mix", comm=gather.first_comm(1))
    gather.first_done(1, landed)
    cpk = _Packing([(n, a[n].shape) for n in CONV])
    conv = cpk.unpack(_chip_bcast(cpk.pack({n: a[n] for n in CONV}), name="gather_conv"), lead=(4,))
    for n in CONV:
        w[n] = jnp.moveaxis(conv[n], 0, 2).reshape(a[n].shape[:2] + (4 * a[n].shape[2],))

    lps, saved = [], []
    for l in range(depth):
        lps.append(_layer_params(w, l, nb))
        x, h, sv = _layer_fwd(x, h, mem, w, lps[l], l, w["norm_mix_g"][l + 1] if l + 1 < depth else None, bias, nb,
                              gather)
        saved.append(sv)
    loss, dx, dg_final = _loss_head(x, w["final_norm_g"], target)
    small_g = [None] * depth
    dbias, pending = None, None
    g_shard = {n: lax.empty(a[n].shape, F32) for n, _ in BIG}
    for l in reversed(range(depth)):
        bottom = l == 0
        dx, g, db, g_shard = _layer_bwd(dx, mem, saved[l], w, lps[l], l, bias, nb, pos, pending=pending,
                                        g_shard=g_shard, reduce_early=bottom)
        dbias = db if dbias is None else dbias + db
        small_g[l] = g
        left = [n for n, _ in BIG if not (bottom and n in EARLY + GATHER_MID)]
        pending = _PendingReduce(_big_grad_shards(g, left), pos, l)

    grads = {n: jnp.stack([small_g[l][n] for l in range(depth)]) for n in REPLICATED + CONV
             if n not in ("rel_bias", "final_norm_g")}
    grads["rel_bias"] = _dil_bias_bwd(dbias, seq)
    grads["final_norm_g"] = dg_final
    grads["loss"] = loss.reshape(1)
    spk = _Packing([(n, grads[n].shape) for n in REPLICATED + CONV + ("loss",)])
    s_flat = spk.pack(grads)
    pair = lax.dynamic_update_slice(lax.empty((2,) + s_flat.shape, F32), s_flat[None], (c, 0, 0))
    *from_sibling, pair = _run_comm(_merge_comms([pending.sibling_comm(), _pair_comm(pair)]), name="tail_sibling")
    pending.add(from_sibling)
    quad = lax.dynamic_update_slice(lax.empty((4,) + pair.shape, F32), pair[None], (chip, 0, 0, 0))
    *others, quad = _run_comm(_merge_comms([pending.chips_comm(), _quad_comm(quad)]), name="tail_chips")
    g_shard = pending.finish(others, g_shard)
    names = [n for n, _ in BIG]
    g_shard = dict(zip(names, _share_halves([g_shard[n] for n in names], name="reduce_share")))
    out = {}
    for n in names:
        operands = (a[n], g_shard[n], a["m_" + n], a["v_" + n])
        cols = a[n].shape[2]
        if cols % LANES:
            lead = max(b for b in range(1, 65) if cols % b == 0)
            res = _adamw(*[z.transpose(2, 0, 1) for z in operands], name="adamw_" + n, lead=lead)
            delta, new_m, new_v = [z.transpose(1, 2, 0) for z in res]
        else:
            delta, new_m, new_v = _adamw(*operands, name="adamw_" + n)
        out[n] = (g_shard[n], delta, new_m, new_v)
    total = spk.unpack(_sum_slots(quad, name="small_sum"))
    for n in CONV:
        width = a[n].shape[2]
        total[n] = lax.dynamic_slice_in_dim(total[n], chip * width, width, axis=2)
    apk = _Packing([(n, a[n].shape) for n in REPLICATED + CONV])
    s_out = _adamw(*[apk.pack(src)[None] for src in (
        {n: a[n] for n in REPLICATED + CONV}, total, {n: a["m_" + n] for n in REPLICATED + CONV},
        {n: a["v_" + n] for n in REPLICATED + CONV})], name="adamw_small")
    s_delta, s_m, s_v = [apk.unpack(o[0]) for o in s_out]
    for n in REPLICATED + CONV:
        out[n] = (total[n], s_delta[n], s_m[n], s_v[n])

    return (total["loss"].reshape(()), dx.reshape(nb, seq, d), *[out[n][0] for n in WEIGHTS],
            *[out[n][1] for n in WEIGHTS], *[out[n][2] for n in WEIGHTS], *[out[n][3] for n in WEIGHTS])
```
